```python
import math
import jax, jax.numpy as jnp
from jax import lax
import numpy as np

D_MODEL = 1024
BATCH = 4
SEQ = 8192
DEPTH = 2

HEAD_DIM = 64
NSA_HEADS = 8
NSA_KV_HEADS = 2
NSA_GROUP = NSA_HEADS // NSA_KV_HEADS
RET_HEADS = 8
RET_DK = 32
RET_DV = 64
NSA_WIDTH = NSA_HEADS * HEAD_DIM
RET_WIDTH = RET_HEADS * RET_DV
MIX_WIDTH = NSA_WIDTH + RET_WIDTH
KV_WIDTH = NSA_KV_HEADS * HEAD_DIM
CMP_LEN = 32
CMP_STRIDE = 16
SLC_LEN = 64
SLC_TOPK = 16
WIN = 512
Q_BLOCK = 128
RET_CHUNK = 128
D_FF = 2816
N_EXPERTS = 8
TOP_K_EXPERTS = 2
N_DENSE = (DEPTH + 1) // 2
N_MOE = DEPTH // 2
EPS = 1e-6
NEG = -1e30
BIG = 1e9
IN_WIDTHS = (NSA_WIDTH, KV_WIDTH, KV_WIDTH, KV_WIDTH, KV_WIDTH, KV_WIDTH, KV_WIDTH,
             3 * NSA_HEADS, RET_HEADS * RET_DK, RET_HEADS * RET_DK, RET_WIDTH, RET_WIDTH)
IN_COLS = sum(IN_WIDTHS)

kernel_name = "hymba_nsa_retention_moe_block"


def rms_norm(x, g):
    xf = x.astype(jnp.float32)
    y = xf * lax.rsqrt(jnp.mean(xf * xf, axis=-1, keepdims=True) + EPS)
    return (y * g).astype(x.dtype)


def alibi_slopes(n):
    return jnp.exp2(-8.0 * jnp.arange(1, n + 1, dtype=jnp.float32) / n)


def _cmp_to_slc(nc, ns):
    cs = jnp.arange(nc) * CMP_STRIDE
    ce = cs + CMP_LEN
    ss = jnp.arange(ns) * SLC_LEN
    se = ss + SLC_LEN
    ov = jnp.clip(jnp.minimum(ce[:, None], se[None]) - jnp.maximum(cs[:, None], ss[None]), 0, None)
    return ov.astype(jnp.float32) / CMP_LEN


def _nsa(q, k_cmp, v_cmp, k_slc, v_slc, k_win, v_win, gates):
    B, G, R, T, Dh = q.shape
    nc = k_cmp.shape[2]
    ns = T // SLC_LEN
    n_sel = min(SLC_TOPK, ns)
    scale = Dh ** -0.5
    slopes = alibi_slopes(NSA_HEADS).reshape(G, R)[:, :, None, None]
    cmp_end = jnp.arange(nc) * CMP_STRIDE + CMP_LEN - 1
    overlap = _cmp_to_slc(nc, ns)
    k_blk = k_slc.reshape(B, G, ns, SLC_LEN, Dh)
    v_blk = v_slc.reshape(B, G, ns, SLC_LEN, Dh)
    pad = ((0, 0), (0, 0), (WIN, 0), (0, 0))
    k_win_p = jnp.pad(k_win, pad)
    v_win_p = jnp.pad(v_win, pad)
    bi = jnp.arange(B)[:, None, None]
    gi = jnp.arange(G)[None, :, None]
    blk = jnp.arange(ns)

    def block(i):
        q0 = i * Q_BLOCK
        qb = lax.dynamic_slice_in_dim(q, q0, Q_BLOCK, axis=3)
        gb = lax.dynamic_slice_in_dim(gates, q0, Q_BLOCK, axis=3)
        t = q0 + jnp.arange(Q_BLOCK)

        s = jnp.einsum('bgrqd,bgcd->bgrqc', qb, k_cmp).astype(jnp.float32) * scale
        dist = (t[:, None] - cmp_end[None, :]).astype(jnp.float32)
        s = jnp.where(dist >= 0, s - slopes * dist, NEG)
        p_cmp = jax.nn.softmax(s, axis=-1) * (t >= CMP_LEN - 1)[:, None]
        o_cmp = jnp.einsum('bgrqc,bgcd->bgrqd', p_cmp.astype(v_cmp.dtype), v_cmp)

        imp = jnp.einsum('bgrqc,cs->bgqs', p_cmp, overlap)
        cur = t // SLC_LEN
        forced = (blk[None] == 0) | (blk[None] == cur[:, None]) | (blk[None] == cur[:, None] - 1)
        valid = blk[None] * SLC_LEN <= t[:, None]
        imp = jnp.where(forced, BIG, jnp.where(valid, imp, -BIG))
        _, sel = lax.top_k(imp, n_sel)
        sel_flat = sel.reshape(B, G, Q_BLOCK * n_sel)
        k_sel = k_blk[bi, gi, sel_flat].reshape(B, G, Q_BLOCK, n_sel * SLC_LEN, Dh)
        v_sel = v_blk[bi, gi, sel_flat].reshape(B, G, Q_BLOCK, n_sel * SLC_LEN, Dh)
        pos = (sel[..., None] * SLC_LEN + jnp.arange(SLC_LEN)).reshape(B, G, Q_BLOCK, n_sel * SLC_LEN)
        s = jnp.einsum('bgrqd,bgqkd->bgrqk', qb, k_sel).astype(jnp.float32) * scale
        dist = (t[:, None] - pos).astype(jnp.float32)[:, :, None]
        s = jnp.where(dist >= 0, s - slopes * dist, NEG)
        p = jax.nn.softmax(s, axis=-1)
        o_slc = jnp.einsum('bgrqk,bgqkd->bgrqd', p.astype(v_sel.dtype), v_sel)

        kw = lax.dynamic_slice_in_dim(k_win_p, q0, WIN + Q_BLOCK, axis=2)
        vw = lax.dynamic_slice_in_dim(v_win_p, q0, WIN + Q_BLOCK, axis=2)
        posw = q0 - WIN + jnp.arange(WIN + Q_BLOCK)
        dist = (t[:, None] - posw[None, :])
        ok = (dist >= 0) & (dist < WIN) & (posw[None, :] >= 0)
        s = jnp.einsum('bgrqd,bgkd->bgrqk', qb, kw).astype(jnp.float32) * scale
        s = jnp.where(ok, s - slopes * dist.astype(jnp.float32), NEG)
        p = jax.nn.softmax(s, axis=-1)
        o_win = jnp.einsum('bgrqk,bgkd->bgrqd', p.astype(vw.dtype), vw)

        return gb[..., 0:1] * o_cmp + gb[..., 1:2] * o_slc + gb[..., 2:3] * o_win

    out = lax.map(block, jnp.arange(T // Q_BLOCK))
    return out.transpose(1, 0, 4, 2, 3, 5).reshape(B, T, G * R * Dh)


def _retention(q, k, v):
    B, T, H, dk = q.shape
    dv = v.shape[-1]
    nch = T // RET_CHUNK
    log_g = jnp.log1p(-jnp.exp2(-5.0 - jnp.arange(H, dtype=jnp.float32)))
    idx = jnp.arange(RET_CHUNK, dtype=jnp.float32)
    diff = idx[:, None] - idx[None, :]
    decay = jnp.where(diff >= 0, jnp.exp(jnp.maximum(diff, 0.0) * log_g[:, None, None]), 0.0)
    zeta = jnp.exp((RET_CHUNK - 1 - idx) * log_g[:, None])
    xi = jnp.exp((idx + 1) * log_g[:, None])
    g_chunk = jnp.exp(RET_CHUNK * log_g)

    def chunks(a):
        return a.reshape(B, nch, RET_CHUNK, H, a.shape[-1]).transpose(1, 0, 3, 2, 4).astype(jnp.float32)

    def step(state, qkv):
        qc, kc, vc = qkv
        inner = jnp.einsum('bhid,bhjd->bhij', qc, kc) * decay
        o = jnp.einsum('bhij,bhje->bhie', inner, vc) + jnp.einsum('bhid,bhde->bhie', qc, state) * xi[..., None]
        state = state * g_chunk[:, None, None] + jnp.einsum('bhjd,bhje->bhde', kc * zeta[..., None], vc)
        return state, o

    _, o = lax.scan(step, jnp.zeros((B, H, dk, dv), jnp.float32), (chunks(q), chunks(k), chunks(v)))
    return o.transpose(1, 0, 3, 2, 4).reshape(B, T, H, dv)


def _mixer(h, w_in, q_norm_g, k_norm_g, cmp_pos, w_cmp, ret_norm_g, w_out):
    B, T, _ = h.shape
    G, R, Dh = NSA_KV_HEADS, NSA_GROUP, HEAD_DIM
    points = [int(p) for p in np.cumsum(IN_WIDTHS)[:-1]]
    q, kc, vc, ks, vs, kw, vw, gts, rq, rk, rv, rg = jnp.split(h @ w_in, points, axis=-1)

    q = rms_norm(q.reshape(B, T, G, R, Dh).transpose(0, 2, 3, 1, 4), q_norm_g)

    def kv(a):
        return a.reshape(B, T, G, Dh).transpose(0, 2, 1, 3)

    k_slc = rms_norm(kv(ks), k_norm_g[1])
    k_win = rms_norm(kv(kw), k_norm_g[2])
    nc = (T - CMP_LEN) // CMP_STRIDE + 1
    cidx = jnp.arange(nc)[:, None] * CMP_STRIDE + jnp.arange(CMP_LEN)[None, :]

    def compress(a, pos_emb, w):
        blocks = kv(a)[:, :, cidx] + pos_emb
        return blocks.reshape(B, G, nc, CMP_LEN * Dh) @ w

    k_cmp = rms_norm(compress(kc, cmp_pos[0], w_cmp[0]), k_norm_g[0])
    v_cmp = compress(vc, cmp_pos[1], w_cmp[1])
    gates = jax.nn.sigmoid(gts.reshape(B, T, G, R, 3).transpose(0, 2, 3, 1, 4))
    nsa = _nsa(q, k_cmp, v_cmp, k_slc, kv(vs), k_win, kv(vw), gates)

    rq = rq.reshape(B, T, RET_HEADS, RET_DK) * (RET_DK ** -0.5)
    rk = rk.reshape(B, T, RET_HEADS, RET_DK)
    rv = rv.reshape(B, T, RET_HEADS, RET_DV)
    ret = rms_norm(_retention(rq, rk, rv), ret_norm_g.reshape(RET_HEADS, RET_DV))
    ret = ret.reshape(B, T, RET_WIDTH).astype(h.dtype) * jax.nn.silu(rg)

    return jnp.concatenate([nsa, ret], axis=-1) @ w_out


def _swiglu(h, wg, wu, wd):
    return (jax.nn.silu(h @ wg) * (h @ wu)) @ wd


def _moe(h, router, router_b, wg, wu, wd):
    logits = (h @ router).astype(jnp.float32) + router_b
    top_vals, top_idx = lax.top_k(logits, TOP_K_EXPERTS)
    w = jax.nn.softmax(top_vals, axis=-1)
    combine = jnp.sum(jax.nn.one_hot(top_idx, N_EXPERTS, dtype=jnp.float32) * w[..., None], axis=-2)
    y = jnp.zeros_like(h)
    for e in range(N_EXPERTS):
        y = y + combine[..., e:e + 1].astype(h.dtype) * _swiglu(h, wg[e], wu[e], wd[e])
    return y


def setup_inputs(seed: int = 0) -> dict:
    key = jax.random.key(seed)
    ks = jax.random.split(key, 18)

    def nrm(k, shape, scale):
        return jax.random.normal(k, shape, jnp.float32) * scale

    def gain(k, shape):
        return 1.0 + 0.01 * jax.random.normal(k, shape, jnp.float32)

    return {
        "x": nrm(ks[0], (BATCH, SEQ, D_MODEL), 1.0),
        "norm_mix_g": gain(ks[1], (DEPTH, D_MODEL)),
        "w_in": nrm(ks[2], (DEPTH, D_MODEL, IN_COLS), D_MODEL ** -0.5),
        "q_norm_g": gain(ks[3], (DEPTH, HEAD_DIM)),
        "k_norm_g": gain(ks[4], (DEPTH, 3, HEAD_DIM)),
        "cmp_pos": nrm(ks[5], (DEPTH, 2, CMP_LEN, HEAD_DIM), 0.02),
        "w_cmp": nrm(ks[6], (DEPTH, 2, CMP_LEN * HEAD_DIM, HEAD_DIM), (CMP_LEN * HEAD_DIM) ** -0.5),
        "ret_norm_g": gain(ks[7], (DEPTH, RET_WIDTH)),
        "w_out": nrm(ks[8], (DEPTH, MIX_WIDTH, D_MODEL), MIX_WIDTH ** -0.5),
        "norm_ffn_g": gain(ks[9], (DEPTH, D_MODEL)),
        "ffn_w_gate": nrm(ks[10], (N_DENSE, D_MODEL, D_FF), D_MODEL ** -0.5),
        "ffn_w_up": nrm(ks[11], (N_DENSE, D_MODEL, D_FF), D_MODEL ** -0.5),
        "ffn_w_down": nrm(ks[12], (N_DENSE, D_FF, D_MODEL), D_FF ** -0.5),
        "moe_router": nrm(ks[13], (N_MOE, D_MODEL, N_EXPERTS), D_MODEL ** -0.5),
        "moe_router_b": nrm(ks[14], (N_MOE, N_EXPERTS), 0.01),
        "moe_w_gate": nrm(ks[15], (N_MOE, N_EXPERTS, D_MODEL, D_FF), D_MODEL ** -0.5),
        "moe_w_up": nrm(ks[16], (N_MOE, N_EXPERTS, D_MODEL, D_FF), D_MODEL ** -0.5),
        "moe_w_down": nrm(ks[17], (N_MOE, N_EXPERTS, D_FF, D_MODEL), D_FF ** -0.5),
    }


def reference(x, norm_mix_g, w_in, q_norm_g, k_norm_g, cmp_pos, w_cmp, ret_norm_g, w_out,
              norm_ffn_g, ffn_w_gate, ffn_w_up, ffn_w_down,
              moe_router, moe_router_b, moe_w_gate, moe_w_up, moe_w_down):
    for l in range(DEPTH):
        h = rms_norm(x, norm_mix_g[l])
        x = x + _mixer(h, w_in[l], q_norm_g[l], k_norm_g[l], cmp_pos[l], w_cmp[l], ret_norm_g[l], w_out[l])
        h = rms_norm(x, norm_ffn_g[l])
        j = l // 2
        if l % 2 == 0:
            x = x + _swiglu(h, ffn_w_gate[j], ffn_w_up[j], ffn_w_down[j])
        else:
            x = x + _moe(h, moe_router[j], moe_router_b[j], moe_w_gate[j], moe_w_up[j], moe_w_down[j])
    return x
```

```python
import functools

import numpy as np
import jax
import jax.numpy as jnp
from jax import lax
from jax.experimental import pallas as pl
from jax.experimental.pallas import tpu as pltpu

F32 = jnp.float32
BF16 = jnp.bfloat16

HEAD_DIM = 64
NSA_HEADS = 8
NSA_KV_HEADS = 2
NSA_GROUP = NSA_HEADS // NSA_KV_HEADS
RET_HEADS = 8
RET_DK = 32
RET_DV = 64
NSA_WIDTH = NSA_HEADS * HEAD_DIM
RET_WIDTH = RET_HEADS * RET_DV
KV_WIDTH = NSA_KV_HEADS * HEAD_DIM
CMP_LEN = 32
CMP_STRIDE = 16
SLC_LEN = 64
SLC_TOPK = 16
WIN = 512
Q_BLOCK = 128
RET_CHUNK = 128
N_EXPERTS = 8
EPS = 1e-6
NEG = -1e30
BIG = 1e9
LANES = 128
GQ = NSA_GROUP * Q_BLOCK
KEY_STEP = 128
VMEM_LIMIT = 60 * 1024 * 1024

_C_Q = 0
_C_KV = _C_Q + NSA_WIDTH
_C_KC = _C_KV + 4 * KV_WIDTH
_C_VC = _C_KC + KV_WIDTH
_C_GT = _C_VC + KV_WIDTH
_C_RET = _C_GT + LANES
_RET_COLS = 2 * RET_HEADS * RET_DK + 2 * RET_WIDTH
_C_END = _C_RET + _RET_COLS


def _params(n_axes, vmem=VMEM_LIMIT):
    return pltpu.CompilerParams(dimension_semantics=("arbitrary",) * n_axes, vmem_limit_bytes=vmem)


def _dot(a, b):
    return jnp.dot(a, b, preferred_element_type=F32)


def _dot_nt(a, b):
    return lax.dot_general(a, b, (((1,), (1,)), ((), ())), preferred_element_type=F32)


def _rms(x, g):
    return x * lax.rsqrt(jnp.mean(x * x, axis=-1, keepdims=True) + EPS) * g


def _inproj_kernel(x_ref, g_ref, w_ref, q_ref, kv_ref, kc_ref, vc_ref, gt_ref, ret_ref):
    h = _rms(x_ref[...], g_ref[...]).astype(BF16)
    q_ref[...] = _dot(h, w_ref[:, _C_Q:_C_KV])
    kv_ref[...] = _dot(h, w_ref[:, _C_KV:_C_KC])
    kc_ref[...] = _dot(h, w_ref[:, _C_KC:_C_VC])
    vc_ref[...] = _dot(h, w_ref[:, _C_VC:_C_GT])
    gt_ref[...] = _dot(h, w_ref[:, _C_GT:_C_RET])
    ret_ref[...] = _dot(h, w_ref[:, _C_RET:_C_END])


def _inproj(x2, g, w, tm=512):
    n, d = x2.shape
    widths = (NSA_WIDTH, 4 * KV_WIDTH, KV_WIDTH, KV_WIDTH, LANES, _RET_COLS)
    return pl.pallas_call(
        _inproj_kernel,
        grid=(n // tm,),
        in_specs=[pl.BlockSpec((tm, d), lambda i: (i, 0)),
                  pl.BlockSpec((1, d), lambda i: (0, 0)),
                  pl.BlockSpec((d, _C_END), lambda i: (0, 0))],
        out_specs=[pl.BlockSpec((tm, c), lambda i: (i, 0)) for c in widths],
        out_shape=[jax.ShapeDtypeStruct((n, c), F32) for c in widths],
        compiler_params=_params(1),
        name="inproj",
    )(x2, g, w)


def _prep_kernel(q_ref, kv_ref, gt_ref, qg_ref, kg_ref, qT_ref, ks_ref, kw_ref, vsT_ref, vwT_ref, gT_ref):
    q = q_ref[...]
    qg = qg_ref[...]
    scale = HEAD_DIM ** -0.5
    heads = []
    for h in range(NSA_HEADS):
        heads.append(_rms(q[:, h * HEAD_DIM:(h + 1) * HEAD_DIM], qg) * scale)
    qt = jnp.concatenate(heads, axis=1).T
    for g in range(NSA_KV_HEADS):
        for r in range(NSA_GROUP):
            h = g * NSA_GROUP + r
            qT_ref[g, :, r * Q_BLOCK:(r + 1) * Q_BLOCK] = qt[h * HEAD_DIM:(h + 1) * HEAD_DIM, :].astype(BF16)
    kv = kv_ref[...]
    ks, kw = kv[:, 0:KV_WIDTH], kv[:, KV_WIDTH:2 * KV_WIDTH]
    vst = kv[:, 2 * KV_WIDTH:3 * KV_WIDTH].T
    vwt = kv[:, 3 * KV_WIDTH:4 * KV_WIDTH].T
    for g in range(NSA_KV_HEADS):
        sl = slice(g * HEAD_DIM, (g + 1) * HEAD_DIM)
        ks_ref[g] = _rms(ks[:, sl], kg_ref[0:1, :]).astype(BF16)
        kw_ref[g] = _rms(kw[:, sl], kg_ref[1:2, :]).astype(BF16)
        vsT_ref[g] = vst[sl, :].astype(BF16)
        vwT_ref[g] = vwt[sl, :].astype(BF16)
    gT_ref[...] = jax.nn.sigmoid(gt_ref[...].T[0:32, :])


def _prep(q, kv, gt, qg, kg, B, T):
    nq = T // Q_BLOCK
    G = NSA_KV_HEADS
    row = lambda b, i: (b * nq + i, 0)
    return pl.pallas_call(
        _prep_kernel,
        grid=(B, nq),
        in_specs=[pl.BlockSpec((Q_BLOCK, NSA_WIDTH), row),
                  pl.BlockSpec((Q_BLOCK, 4 * KV_WIDTH), row),
                  pl.BlockSpec((Q_BLOCK, LANES), row),
                  pl.BlockSpec((1, HEAD_DIM), lambda b, i: (0, 0)),
                  pl.BlockSpec((2, HEAD_DIM), lambda b, i: (0, 0))],
        out_specs=[pl.BlockSpec((None, G, None, HEAD_DIM, GQ), lambda b, i: (b, 0, i, 0, 0)),
                   pl.BlockSpec((None, G, Q_BLOCK, HEAD_DIM), lambda b, i: (b, 0, i, 0)),
                   pl.BlockSpec((None, G, Q_BLOCK, HEAD_DIM), lambda b, i: (b, 0, i, 0)),
                   pl.BlockSpec((None, G, HEAD_DIM, Q_BLOCK), lambda b, i: (b, 0, 0, i)),
                   pl.BlockSpec((None, G, HEAD_DIM, Q_BLOCK), lambda b, i: (b, 0, 0, i)),
                   pl.BlockSpec((None, None, 32, Q_BLOCK), lambda b, i: (b, i, 0, 0))],
        out_shape=[jax.ShapeDtypeStruct((B, G, nq, HEAD_DIM, GQ), BF16),
                   jax.ShapeDtypeStruct((B, G, T, HEAD_DIM), BF16),
                   jax.ShapeDtypeStruct((B, G, T, HEAD_DIM), BF16),
                   jax.ShapeDtypeStruct((B, G, HEAD_DIM, T), BF16),
                   jax.ShapeDtypeStruct((B, G, HEAD_DIM, T), BF16),
                   jax.ShapeDtypeStruct((B, nq, 32, Q_BLOCK), F32)],
        compiler_params=_params(2),
        name="nsa_prep",
    )(q, kv, gt, qg, kg)


def _compress_kernel(kc_ref, vc_ref, wk_ref, wv_ref, pk_ref, pv_ref, kg_ref, kcmp_ref, vcT_ref):
    ncp = kc_ref.shape[0]

    def comp(a, w_ref, p_ref):
        lo = _dot((a + p_ref[0:1, :]).astype(BF16), w_ref[0])
        hi = _dot((a + p_ref[1:2, :]).astype(BF16), w_ref[1])
        return lo + pltpu.roll(hi, ncp - 1, 0)

    k = comp(kc_ref[...], wk_ref, pk_ref)
    v = comp(vc_ref[...], wv_ref, pv_ref).T
    for g in range(NSA_KV_HEADS):
        sl = slice(g * HEAD_DIM, (g + 1) * HEAD_DIM)
        kcmp_ref[g] = _rms(k[:, sl], kg_ref[...]).astype(BF16)
        vcT_ref[g] = v[sl, :].astype(BF16)


def _compress(kc, vc, wk, wv, pk, pv, kg, B, T):
    ncp = T // CMP_STRIDE
    G = NSA_KV_HEADS
    hw = CMP_STRIDE * KV_WIDTH
    kc = kc.reshape(B, ncp, hw)
    vc = vc.reshape(B, ncp, hw)
    const3 = lambda b: (0, 0, 0)
    const2 = lambda b: (0, 0)
    return pl.pallas_call(
        _compress_kernel,
        grid=(B,),
        in_specs=[pl.BlockSpec((None, ncp, hw), lambda b: (b, 0, 0)),
                  pl.BlockSpec((None, ncp, hw), lambda b: (b, 0, 0)),
                  pl.BlockSpec((2, hw, KV_WIDTH), const3),
                  pl.BlockSpec((2, hw, KV_WIDTH), const3),
                  pl.BlockSpec((2, hw), const2),
                  pl.BlockSpec((2, hw), const2),
                  pl.BlockSpec((1, HEAD_DIM), const2)],
        out_specs=[pl.BlockSpec((None, G, ncp, HEAD_DIM), lambda b: (b, 0, 0, 0)),
                   pl.BlockSpec((None, G, HEAD_DIM, ncp), lambda b: (b, 0, 0, 0))],
        out_shape=[jax.ShapeDtypeStruct((B, G, ncp, HEAD_DIM), BF16),
                   jax.ShapeDtypeStruct((B, G, HEAD_DIM, ncp), BF16)],
        compiler_params=_params(1),
        name="nsa_compress",
    )(kc, vc, wk, wv, pk, pv, kg)


def _compress_weights(w, pos):
    G = NSA_KV_HEADS
    w4 = w.reshape(2, CMP_STRIDE, HEAD_DIM, HEAD_DIM)
    eye = jnp.eye(G, dtype=w.dtype)
    wbd = jnp.einsum('hlde,gk->hlgdke', w4, eye).reshape(2, CMP_STRIDE * KV_WIDTH, KV_WIDTH)
    p = pos.reshape(2, CMP_STRIDE, 1, HEAD_DIM)
    p = jnp.broadcast_to(p, (2, CMP_STRIDE, G, HEAD_DIM)).reshape(2, CMP_STRIDE * KV_WIDTH)
    return wbd.astype(BF16), p


def _split3(x):
    hi = x.astype(BF16)
    r = x - hi.astype(F32)
    mid = r.astype(BF16)
    lo = (r - mid.astype(F32)).astype(BF16)
    return hi, mid, lo


def _nsa_cmp_kernel(qT_ref, kc_ref, vcT_ref, ovT_ref, pair_ref, slope_ref, ocmp_ref, sel_ref, flag_ref, *, n_sel):
    ncp = kc_ref.shape[0]
    ns = ovT_ref.shape[0]
    q0 = pl.program_id(2) * Q_BLOCK
    s = _dot(kc_ref[...], qT_ref[...])
    lane = lax.broadcasted_iota(jnp.int32, (1, GQ), 1)
    t_row = q0 + (lane & (Q_BLOCK - 1))
    cend = lax.broadcasted_iota(jnp.int32, (ncp, 1), 0) * CMP_STRIDE + (CMP_LEN - 1)
    dist = t_row - cend
    s = jnp.where(dist >= 0, s - slope_ref[0:1, :] * dist.astype(F32), NEG)
    m = jnp.max(s, axis=0, keepdims=True)
    e = jnp.exp(s - m)
    p = e / jnp.sum(e, axis=0, keepdims=True) * (t_row >= CMP_LEN - 1).astype(F32)
    ocmp_ref[...] = _dot(vcT_ref[...], p.astype(BF16))

    ps = p[:, 0:Q_BLOCK]
    for r in range(1, NSA_GROUP):
        ps = ps + p[:, r * Q_BLOCK:(r + 1) * Q_BLOCK]
    ov = ovT_ref[...]
    hi, mid, lo = _split3(ps)
    imp = _dot(ov, hi) + _dot(ov, mid) + _dot(ov, lo)

    blk = lax.broadcasted_iota(jnp.int32, (ns, 1), 0)
    tq = q0 + lax.broadcasted_iota(jnp.int32, (1, Q_BLOCK), 1)
    cur = tq // SLC_LEN
    forced = (blk == 0) | (blk == cur) | (blk == cur - 1)
    valid = blk * SLC_LEN <= tq
    imp = jnp.where(forced, BIG, jnp.where(valid, imp, -BIG))
    blk_f = blk.astype(F32)
    sel = jnp.zeros((ns, Q_BLOCK), F32)
    for _ in range(n_sel):
        mx = jnp.max(imp, axis=0, keepdims=True)
        idx = jnp.min(jnp.where(imp == mx, blk_f, float(ns)), axis=0, keepdims=True)
        pick = blk_f == idx
        sel = jnp.where(pick, 1.0, sel)
        imp = jnp.where(pick, -3e38, imp)
    sel_ref[...] = sel
    cnt = _dot_nt(jnp.ones((8, Q_BLOCK), BF16), sel.astype(BF16))
    flag_ref[...] = (_dot(cnt.astype(BF16), pair_ref[...]) > 0).astype(jnp.int32)


def _nsa_cmp(qT, kcmp, vcT, ovT, pair, slopes, B, T):
    G = NSA_KV_HEADS
    nq = T // Q_BLOCK
    ncp = T // CMP_STRIDE
    ns = T // SLC_LEN
    n_sel = min(SLC_TOPK, ns)
    return pl.pallas_call(
        functools.partial(_nsa_cmp_kernel, n_sel=n_sel),
        grid=(B, G, nq),
        in_specs=[pl.BlockSpec((None, None, None, HEAD_DIM, GQ), lambda b, g, i: (b, g, i, 0, 0)),
                  pl.BlockSpec((None, None, ncp, HEAD_DIM), lambda b, g, i: (b, g, 0, 0)),
                  pl.BlockSpec((None, None, HEAD_DIM, ncp), lambda b, g, i: (b, g, 0, 0)),
                  pl.BlockSpec((ns, ncp), lambda b, g, i: (0, 0)),
                  pl.BlockSpec((ns, ns), lambda b, g, i: (0, 0)),
                  pl.BlockSpec((None, 8, GQ), lambda b, g, i: (g, 0, 0))],
        out_specs=[pl.BlockSpec((None, None, None, HEAD_DIM, GQ), lambda b, g, i: (b, g, i, 0, 0)),
                   pl.BlockSpec((None, None, None, ns, Q_BLOCK), lambda b, g, i: (b, g, i, 0, 0)),
                   pl.BlockSpec((None, None, None, 8, ns), lambda b, g, i: (b, g, i, 0, 0))],
        out_shape=[jax.ShapeDtypeStruct((B, G, nq, HEAD_DIM, GQ), F32),
                   jax.ShapeDtypeStruct((B, G, nq, ns, Q_BLOCK), F32),
                   jax.ShapeDtypeStruct((B, G, nq, 8, ns), jnp.int32)],
        compiler_params=_params(3),
        name="nsa_cmp",
    )(qT, kcmp, vcT, ovT, pair, slopes)


def _nsa_main_kernel(flags_ref, qT_ref, ks_ref, vsT_ref, kw_ref, vwT_ref, sel_ref, gT_ref, ocmp_ref, slope_ref,
                     out_ref, m_sc, l_sc, acc_sc):
    b, g, i = pl.program_id(0), pl.program_id(1), pl.program_id(2)
    nq = pl.num_programs(2)
    n_steps = sel_ref.shape[0] // 2
    qT = qT_ref[...]
    lane = lax.broadcasted_iota(jnp.int32, (1, GQ), 1)
    t_row = i * Q_BLOCK + (lane & (Q_BLOCK - 1))
    slope = slope_ref[0:1, :]
    key_iota = lax.broadcasted_iota(jnp.int32, (KEY_STEP, 1), 0)

    def reset():
        m_sc[...] = jnp.full(m_sc.shape, NEG, F32)
        l_sc[...] = jnp.zeros(l_sc.shape, F32)
        acc_sc[...] = jnp.zeros(acc_sc.shape, F32)

    def attend(k_ref, vT_ref, j, mask_fn):
        k0 = pl.multiple_of(j * KEY_STEP, KEY_STEP)
        s = _dot(k_ref[pl.ds(k0, KEY_STEP), :], qT)
        dist = t_row - (k0 + key_iota)
        mask = mask_fn(dist, j)
        s = jnp.where(mask, s - slope * dist.astype(F32), NEG)
        m_old = m_sc[...]
        m_new = jnp.maximum(m_old, jnp.max(s, axis=0, keepdims=True))
        p = jnp.where(mask, jnp.exp(s - m_new), 0.0)
        alpha = jnp.exp(m_old - m_new)
        l_sc[...] = alpha * l_sc[...] + jnp.sum(p, axis=0, keepdims=True)
        acc_sc[...] = alpha * acc_sc[...] + _dot(vT_ref[:, pl.ds(k0, KEY_STEP)], p.astype(BF16))
        m_sc[...] = m_new

    def sel_mask(dist, j):
        ra = sel_ref[pl.ds(2 * j, 1), :]
        rb = sel_ref[pl.ds(2 * j + 1, 1), :]
        ra = jnp.concatenate([ra] * NSA_GROUP, axis=1)
        rb = jnp.concatenate([rb] * NSA_GROUP, axis=1)
        picked = jnp.where(key_iota < SLC_LEN, ra, rb) > 0.5
        return picked & (dist >= 0)

    def win_mask(dist, j):
        return (dist >= 0) & (dist < WIN)

    reset()
    base = ((b * pl.num_programs(1) + g) * nq + i) * n_steps

    def slc_body(j, c):
        @pl.when(flags_ref[base + j] > 0)
        def _():
            attend(ks_ref, vsT_ref, j, sel_mask)
        return c

    lax.fori_loop(0, i + 1, slc_body, 0)
    o_slc = acc_sc[...] / l_sc[...]

    reset()

    def win_body(j, c):
        attend(kw_ref, vwT_ref, j, win_mask)
        return c

    lax.fori_loop(jnp.maximum(i - WIN // KEY_STEP, 0), i + 1, win_body, 0)
    o_win = acc_sc[...] / l_sc[...]

    def gate(k):
        rows = [gT_ref[pl.ds(g * (NSA_GROUP * 3) + r * 3 + k, 1), :] for r in range(NSA_GROUP)]
        return jnp.concatenate(rows, axis=1)

    o = gate(0) * ocmp_ref[...] + gate(1) * o_slc + gate(2) * o_win
    o = jnp.concatenate([o, jnp.zeros_like(o)], axis=0)
    for r in range(NSA_GROUP):
        out_ref[:, r * HEAD_DIM:(r + 1) * HEAD_DIM] = o[:, r * Q_BLOCK:(r + 1) * Q_BLOCK].T[:, 0:HEAD_DIM]


def _nsa_main(flags, qT, ks, vsT, kw, vwT, sel, gT, ocmp, slopes, B, T):
    G = NSA_KV_HEADS
    nq = T // Q_BLOCK
    ns = T // SLC_LEN
    kspec = pl.BlockSpec((None, None, T, HEAD_DIM), lambda b, g, i, f: (b, g, 0, 0))
    vspec = pl.BlockSpec((None, None, HEAD_DIM, T), lambda b, g, i, f: (b, g, 0, 0))
    tile = lambda b, g, i, f: (b, g, i, 0, 0)
    grid_spec = pltpu.PrefetchScalarGridSpec(
        num_scalar_prefetch=1,
        grid=(B, G, nq),
        in_specs=[pl.BlockSpec((None, None, None, HEAD_DIM, GQ), tile),
                  kspec, vspec, kspec, vspec,
                  pl.BlockSpec((None, None, None, ns, Q_BLOCK), tile),
                  pl.BlockSpec((None, None, 32, Q_BLOCK), lambda b, g, i, f: (b, i, 0, 0)),
                  pl.BlockSpec((None, None, None, HEAD_DIM, GQ), tile),
                  pl.BlockSpec((None, 8, GQ), lambda b, g, i, f: (g, 0, 0))],
        out_specs=pl.BlockSpec((None, Q_BLOCK, NSA_GROUP * HEAD_DIM), lambda b, g, i, f: (b, i, g)),
        scratch_shapes=[pltpu.VMEM((1, GQ), F32), pltpu.VMEM((1, GQ), F32), pltpu.VMEM((HEAD_DIM, GQ), F32)],
    )
    return pl.pallas_call(
        _nsa_main_kernel,
        grid_spec=grid_spec,
        out_shape=jax.ShapeDtypeStruct((B, T, NSA_WIDTH), F32),
        compiler_params=_params(3),
        name="nsa_main",
    )(flags, qT, ks, vsT, kw, vwT, sel, gT, ocmp, slopes)


def _ret_kernel(p_ref, decay_ref, xi_ref, zeta_ref, gch_ref, ng_ref, out_ref, state_ref):
    @pl.when(pl.program_id(1) == 0)
    def _():
        state_ref[...] = jnp.zeros(state_ref.shape, F32)

    kw = RET_HEADS * RET_DK
    p = p_ref[...]
    rq = p[:, 0:kw] * (RET_DK ** -0.5)
    rk = p[:, kw:2 * kw]
    rkT = rk.T
    rv = p[:, 2 * kw:2 * kw + RET_WIDTH]
    rg = p[:, 2 * kw + RET_WIDTH:2 * kw + 2 * RET_WIDTH]
    xi = xi_ref[...]
    outs = []
    for h in range(RET_HEADS):
        qh = rq[:, h * RET_DK:(h + 1) * RET_DK]
        kh = rk[:, h * RET_DK:(h + 1) * RET_DK]
        khT = rkT[h * RET_DK:(h + 1) * RET_DK, :]
        vh = rv[:, h * RET_DV:(h + 1) * RET_DV]
        st = state_ref[h]
        inner = _dot_nt(qh, kh) * decay_ref[h]
        o = _dot(inner, vh) + _dot(qh, st) * xi[:, h:h + 1]
        state_ref[h] = st * gch_ref[h:h + 1, 0:1] + _dot(khT * zeta_ref[h:h + 1, :], vh)
        outs.append(_rms(o, ng_ref[:, h * RET_DV:(h + 1) * RET_DV]))
    out_ref[...] = jnp.concatenate(outs, axis=1) * (rg * jax.nn.sigmoid(rg))


def _ret_consts():
    H, C = RET_HEADS, RET_CHUNK
    log_g = jnp.log1p(-jnp.exp2(-5.0 - jnp.arange(H, dtype=F32)))
    idx = jnp.arange(C, dtype=F32)
    diff = idx[:, None] - idx[None, :]
    decay = jnp.where(diff >= 0, jnp.exp(jnp.maximum(diff, 0.0) * log_g[:, None, None]), 0.0)
    zeta = jnp.exp((C - 1 - idx) * log_g[:, None])
    xi = jnp.exp((idx + 1) * log_g[:, None]).T
    g_chunk = jnp.broadcast_to(jnp.exp(C * log_g)[:, None], (H, LANES))
    return decay, xi, zeta, g_chunk


def _retention(pret, ng, B, T):
    nch = T // RET_CHUNK
    decay, xi, zeta, gch = _ret_consts()
    c2 = lambda b, c: (0, 0)
    return pl.pallas_call(
        _ret_kernel,
        grid=(B, nch),
        in_specs=[pl.BlockSpec((RET_CHUNK, _RET_COLS), lambda b, c: (b * nch + c, 0)),
                  pl.BlockSpec((RET_HEADS, RET_CHUNK, RET_CHUNK), lambda b, c: (0, 0, 0)),
                  pl.BlockSpec((RET_CHUNK, RET_HEADS), c2),
                  pl.BlockSpec((RET_HEADS, RET_CHUNK), c2),
                  pl.BlockSpec((RET_HEADS, LANES), c2),
                  pl.BlockSpec((1, RET_WIDTH), c2)],
        out_specs=pl.BlockSpec((RET_CHUNK, RET_WIDTH), lambda b, c: (b * nch + c, 0)),
        out_shape=jax.ShapeDtypeStruct((B * T, RET_WIDTH), F32),
        scratch_shapes=[pltpu.VMEM((RET_HEADS, RET_DK, RET_DV), F32)],
        compiler_params=_params(2),
        name="retention",
    )(pret, decay, xi, zeta, gch, ng)


def _outproj_kernel(x_ref, nsa_ref, ret_ref, w_ref, o_ref):
    o_ref[...] = (x_ref[...] + _dot(nsa_ref[...].astype(BF16), w_ref[0:NSA_WIDTH, :])
                  + _dot(ret_ref[...].astype(BF16), w_ref[NSA_WIDTH:, :]))


def _outproj(x2, nsa, ret, w, tm=512):
    n, d = x2.shape
    return pl.pallas_call(
        _outproj_kernel,
        grid=(n // tm,),
        in_specs=[pl.BlockSpec((tm, d), lambda i: (i, 0)),
                  pl.BlockSpec((tm, NSA_WIDTH), lambda i: (i, 0)),
                  pl.BlockSpec((tm, RET_WIDTH), lambda i: (i, 0)),
                  pl.BlockSpec((NSA_WIDTH + RET_WIDTH, d), lambda i: (0, 0))],
        out_specs=pl.BlockSpec((tm, d), lambda i: (i, 0)),
        out_shape=jax.ShapeDtypeStruct((n, d), F32),
        compiler_params=_params(1),
        name="outproj",
    )(x2, nsa, ret, w)


def _ffn_kernel(x_ref, g_ref, wg_ref, wu_ref, wd_ref, o_ref, h_sc):
    f = pl.program_id(1)

    @pl.when(f == 0)
    def _():
        x = x_ref[...]
        h_sc[...] = _rms(x, g_ref[...]).astype(BF16)
        o_ref[...] = x

    h = h_sc[...]
    a = _dot(h, wg_ref[...])
    act = (a * jax.nn.sigmoid(a) * _dot(h, wu_ref[...])).astype(BF16)
    o_ref[...] += _dot(act, wd_ref[...])


def _ffn(x2, g, wg, wu, wd, tm=512, fc=1408):
    n, d = x2.shape
    dff = wg.shape[1]
    return pl.pallas_call(
        _ffn_kernel,
        grid=(n // tm, dff // fc),
        in_specs=[pl.BlockSpec((tm, d), lambda i, f: (i, 0)),
                  pl.BlockSpec((1, d), lambda i, f: (0, 0)),
                  pl.BlockSpec((d, fc), lambda i, f: (0, f)),
                  pl.BlockSpec((d, fc), lambda i, f: (0, f)),
                  pl.BlockSpec((fc, d), lambda i, f: (f, 0))],
        out_specs=pl.BlockSpec((tm, d), lambda i, f: (i, 0)),
        out_shape=jax.ShapeDtypeStruct((n, d), F32),
        scratch_shapes=[pltpu.VMEM((tm, d), BF16)],
        compiler_params=_params(2),
        name="ffn_dense",
    )(x2, g, wg, wu, wd)


def _router_kernel(x_ref, g_ref, r_ref, rb_ref, tri_ref, h_ref, rank_ref, comb_ref, rankT_ref, cnt_ref):
    h = _rms(x_ref[...], g_ref[...])
    h_ref[...] = h.astype(BF16)
    hh, hm, hl = _split3(h)
    rh, rm, rl = _split3(r_ref[...])
    logits = (_dot(hh, rh) + (_dot(hh, rm) + _dot(hm, rh)) + (_dot(hh, rl) + _dot(hm, rm) + _dot(hl, rh))
              + rb_ref[...])
    lane = lax.broadcasted_iota(jnp.int32, logits.shape, 1).astype(F32)
    logits = jnp.where(lane < N_EXPERTS, logits, NEG)
    m1 = jnp.max(logits, axis=1, keepdims=True)
    i1 = jnp.min(jnp.where(logits == m1, lane, float(LANES)), axis=1, keepdims=True)
    l2 = jnp.where(lane == i1, NEG, logits)
    m2 = jnp.max(l2, axis=1, keepdims=True)
    i2 = jnp.min(jnp.where(l2 == m2, lane, float(LANES)), axis=1, keepdims=True)
    e2 = jnp.exp(m2 - m1)
    w1 = 1.0 / (1.0 + e2)
    w2 = e2 / (1.0 + e2)
    use1, use2 = lane == i1, lane == i2
    comb_ref[...] = jnp.where(use1, w1, 0.0) + jnp.where(use2, w2, 0.0)
    use = (use1 | use2).astype(F32)
    rank = jnp.where(use > 0, _dot(tri_ref[...], use.astype(BF16)), -1.0)
    rank_ref[...] = rank
    rankT_ref[...] = rank.T[0:N_EXPERTS, :]
    cnt_ref[...] = jnp.broadcast_to(jnp.sum(use, axis=0, keepdims=True), cnt_ref.shape).astype(jnp.int32)


def _router(x2, g, router, rb, tm):
    n, d = x2.shape
    nt = n // tm
    tri = (jnp.arange(tm)[:, None] > jnp.arange(tm)[None, :]).astype(BF16)
    rpad = jnp.zeros((d, LANES), F32).at[:, :N_EXPERTS].set(router)
    rbpad = jnp.zeros((1, LANES), F32).at[0, :N_EXPERTS].set(rb)
    c2 = lambda i: (0, 0)
    return pl.pallas_call(
        _router_kernel,
        grid=(nt,),
        in_specs=[pl.BlockSpec((tm, d), lambda i: (i, 0)),
                  pl.BlockSpec((1, d), c2),
                  pl.BlockSpec((d, LANES), c2),
                  pl.BlockSpec((1, LANES), c2),
                  pl.BlockSpec((tm, tm), c2)],
        out_specs=[pl.BlockSpec((tm, d), lambda i: (i, 0)),
                   pl.BlockSpec((tm, LANES), lambda i: (i, 0)),
                   pl.BlockSpec((tm, LANES), lambda i: (i, 0)),
                   pl.BlockSpec((N_EXPERTS, tm), lambda i: (0, i)),
                   pl.BlockSpec((None, 8, LANES), lambda i: (i, 0, 0))],
        out_shape=[jax.ShapeDtypeStruct((n, d), BF16),
                   jax.ShapeDtypeStruct((n, LANES), F32),
                   jax.ShapeDtypeStruct((n, LANES), F32),
                   jax.ShapeDtypeStruct((N_EXPERTS, n), F32),
                   jax.ShapeDtypeStruct((nt, 8, LANES), jnp.int32)],
        compiler_params=_params(1),
        name="moe_router",
    )(x2, g, rpad, rbpad, tri)


MOE_SUB = 128


def _moe_kernel(cnt_ref, h_ref, rankT_ref, rank_ref, comb_ref, wg_ref, wu_ref, wd_ref, x_ref, o_ref, hc_sc, oacc_sc):
    t, e, f = pl.program_id(0), pl.program_id(1), pl.program_id(2)
    nf = pl.num_programs(2)
    tm = h_ref.shape[0]
    nsub = (cnt_ref[t * N_EXPERTS + e] + (MOE_SUB - 1)) // MOE_SUB

    @pl.when((e == 0) & (f == 0))
    def _():
        o_ref[...] = x_ref[...]

    @pl.when(f == 0)
    def _():
        rank_row = rankT_ref[...]

        def gather(s, c):
            r0 = pl.multiple_of(s * MOE_SUB, MOE_SUB)
            rows = (lax.broadcasted_iota(jnp.int32, (MOE_SUB, 1), 0) + r0).astype(F32)
            onehot = (rows == rank_row).astype(BF16)
            hc_sc[pl.ds(r0, MOE_SUB), :] = _dot(onehot, h_ref[...]).astype(BF16)
            oacc_sc[pl.ds(r0, MOE_SUB), :] = jnp.zeros((MOE_SUB, oacc_sc.shape[1]), F32)
            return c

        lax.fori_loop(0, nsub, gather, 0)

    def expert(s, c):
        r0 = pl.multiple_of(s * MOE_SUB, MOE_SUB)
        rows = hc_sc[pl.ds(r0, MOE_SUB), :]
        a = _dot(rows, wg_ref[...])
        act = (a * jax.nn.sigmoid(a) * _dot(rows, wu_ref[...])).astype(BF16)
        oacc_sc[pl.ds(r0, MOE_SUB), :] += _dot(act, wd_ref[...])
        return c

    lax.fori_loop(0, nsub, expert, 0)

    @pl.when(f == nf - 1)
    def _():
        is_e = lax.broadcasted_iota(jnp.int32, (1, LANES), 1) == e
        rank_col = jnp.sum(jnp.where(is_e, rank_ref[...], 0.0), axis=1, keepdims=True)
        comb_col = jnp.sum(jnp.where(is_e, comb_ref[...], 0.0), axis=1, keepdims=True)

        def scatter(s, c):
            r0 = pl.multiple_of(s * MOE_SUB, MOE_SUB)
            cols = (lax.broadcasted_iota(jnp.int32, (1, MOE_SUB), 1) + r0).astype(F32)
            onehot = (rank_col == cols).astype(BF16)
            y = _dot(onehot, oacc_sc[pl.ds(r0, MOE_SUB), :].astype(BF16))
            o_ref[...] += comb_col * y
            return c

        lax.fori_loop(0, nsub, scatter, 0)


def _moe(counts, h, rankT, rank, comb, wg, wu, wd, x2, tm, fc=1408):
    n, d = x2.shape
    dff = wg.shape[2]
    grid_spec = pltpu.PrefetchScalarGridSpec(
        num_scalar_prefetch=1,
        grid=(n // tm, N_EXPERTS, dff // fc),
        in_specs=[pl.BlockSpec((tm, d), lambda t, e, f, c: (t, 0)),
                  pl.BlockSpec((None, 1, tm), lambda t, e, f, c: (e, 0, t)),
                  pl.BlockSpec((tm, LANES), lambda t, e, f, c: (t, 0)),
                  pl.BlockSpec((tm, LANES), lambda t, e, f, c: (t, 0)),
                  pl.BlockSpec((None, d, fc), lambda t, e, f, c: (e, 0, f)),
                  pl.BlockSpec((None, d, fc), lambda t, e, f, c: (e, 0, f)),
                  pl.BlockSpec((None, fc, d), lambda t, e, f, c: (e, f, 0)),
                  pl.BlockSpec((tm, d), lambda t, e, f, c: (t, 0))],
        out_specs=pl.BlockSpec((tm, d), lambda t, e, f, c: (t, 0)),
        scratch_shapes=[pltpu.VMEM((tm, d), BF16), pltpu.VMEM((tm, d), F32)],
    )
    return pl.pallas_call(
        _moe_kernel,
        grid_spec=grid_spec,
        out_shape=jax.ShapeDtypeStruct((n, d), F32),
        compiler_params=_params(3),
        name="moe_experts",
    )(counts, h, rankT.reshape(N_EXPERTS, 1, n), rank, comb, wg, wu, wd, x2)


def _permute_w_in(w):
    o = np.cumsum((0, NSA_WIDTH) + (KV_WIDTH,) * 6 + (3 * NSA_HEADS,))
    q, kc, vc, ks, vs, kw, vw, gts = (w[:, o[k]:o[k + 1]] for k in range(8))
    ret = w[:, o[8]:]
    pad = jnp.zeros((w.shape[0], LANES - 3 * NSA_HEADS), w.dtype)
    return jnp.concatenate([q, ks, kw, vs, vw, kc, vc, gts, pad, ret], axis=1).astype(BF16)


def _nsa_consts(T):
    ncp = T // CMP_STRIDE
    ns = T // SLC_LEN
    cs = np.arange(ncp) * CMP_STRIDE
    ss = np.arange(ns) * SLC_LEN
    ov = np.clip(np.minimum(cs[None, :] + CMP_LEN, ss[:, None] + SLC_LEN) - np.maximum(cs[None, :], ss[:, None]), 0, None)
    ovT = (ov.astype(np.float32) / CMP_LEN)
    ovT[:, ncp - 1] = 0.0
    pair = (np.arange(ns)[:, None] // 2 == np.arange(ns)[None, :]).astype(np.float32)
    h = np.arange(NSA_HEADS).reshape(NSA_KV_HEADS, NSA_GROUP) + 1
    slopes = np.exp2(-8.0 * h / NSA_HEADS).astype(np.float32)
    slopes = np.repeat(slopes, Q_BLOCK, axis=1)[:, None, :].repeat(8, axis=1)
    return jnp.asarray(ovT, BF16), jnp.asarray(pair, BF16), jnp.asarray(slopes, F32)


def _mixer(x2, B, T, norm_g, w_in, q_norm_g, k_norm_g, cmp_pos, w_cmp, ret_norm_g, w_out):
    ns = T // SLC_LEN
    q, kv, kc, vc, gt, pret = _inproj(x2, norm_g[None, :], _permute_w_in(w_in))
    qT, ks, kw, vsT, vwT, gT = _prep(q, kv, gt, q_norm_g[None, :], k_norm_g[1:3], B, T)
    wk, pk = _compress_weights(w_cmp[0], cmp_pos[0])
    wv, pv = _compress_weights(w_cmp[1], cmp_pos[1])
    kcmp, vcT = _compress(kc, vc, wk, wv, pk, pv, k_norm_g[0:1], B, T)
    ovT, pair, slopes = _nsa_consts(T)
    ocmp, sel, flags = _nsa_cmp(qT, kcmp, vcT, ovT, pair, slopes, B, T)
    flags = flags[:, :, :, 0, :ns // 2].reshape(-1)
    nsa = _nsa_main(flags, qT, ks, vsT, kw, vwT, sel, gT, ocmp, slopes, B, T)
    ret = _retention(pret, ret_norm_g[None, :], B, T)
    return _outproj(x2, nsa.reshape(B * T, NSA_WIDTH), ret, w_out.astype(BF16))


def _moe_layer(x2, norm_g, router, router_b, wg, wu, wd, tm=1024):
    tm = min(tm, x2.shape[0])
    h, rank, comb, rankT, cnt = _router(x2, norm_g[None, :], router, router_b, tm)
    counts = cnt[:, 0, :N_EXPERTS].reshape(-1)
    return _moe(counts, h, rankT, rank, comb, wg.astype(BF16), wu.astype(BF16), wd.astype(BF16), x2, tm)


def kernel(x, norm_mix_g, w_in, q_norm_g, k_norm_g, cmp_pos, w_cmp, ret_norm_g, w_out, norm_ffn_g,
           ffn_w_gate, ffn_w_up, ffn_w_down, moe_router, moe_router_b, moe_w_gate, moe_w_up, moe_w_down):
    B, T, D = x.shape
    depth = norm_mix_g.shape[0]
    x2 = x.reshape(B * T, D)
    for l in range(depth):
        x2 = _mixer(x2, B, T, norm_mix_g[l], w_in[l], q_norm_g[l], k_norm_g[l], cmp_pos[l], w_cmp[l],
                    ret_norm_g[l], w_out[l])
        j = l // 2
        if l % 2 == 0:
            x2 = _ffn(x2, norm_ffn_g[l][None, :], ffn_w_gate[j].astype(BF16), ffn_w_up[j].astype(BF16),
                      ffn_w_down[j].astype(BF16))
        else:
            x2 = _moe_layer(x2, norm_ffn_g[l], moe_router[j], moe_router_b[j], moe_w_gate[j], moe_w_up[j],
                            moe_w_down[j])
    return x2.reshape(B, T, D)
```

```python
import functools

import numpy as np
import jax
import jax.numpy as jnp
from jax import lax
from jax.experimental import pallas as pl
from jax.experimental.pallas import tpu as pltpu

F32 = jnp.float32
BF16 = jnp.bfloat16

HEAD_DIM = 64
NSA_HEADS = 8
NSA_KV_HEADS = 2
NSA_GROUP = NSA_HEADS // NSA_KV_HEADS
RET_HEADS = 8
RET_DK = 32
RET_DV = 64
NSA_WIDTH = NSA_HEADS * HEAD_DIM
RET_WIDTH = RET_HEADS * RET_DV
KV_WIDTH = NSA_KV_HEADS * HEAD_DIM
CMP_LEN = 32
CMP_STRIDE = 16
SLC_LEN = 64
SLC_TOPK = 16
WIN = 512
Q_BLOCK = 128
RET_CHUNK = 128
N_EXPERTS = 8
EPS = 1e-6
NEG = -1e30
BIG = 1e9
LANES = 128
GQ = NSA_GROUP * Q_BLOCK
KEY_STEP = 128
STEP_GROUP = 4
WIN_KEYS = WIN + Q_BLOCK
LOG2E = 1.4426950408889634
VMEM_LIMIT = 60 * 1024 * 1024

_C_Q = 0
_C_KV = _C_Q + NSA_WIDTH
_C_KC = _C_KV + 4 * KV_WIDTH
_C_VC = _C_KC + KV_WIDTH
_C_GT = _C_VC + KV_WIDTH
_C_RET = _C_GT + LANES
_RET_COLS = 2 * RET_HEADS * RET_DK + 2 * RET_WIDTH
_C_END = _C_RET + _RET_COLS


def _params(n_axes, vmem=VMEM_LIMIT):
    return pltpu.CompilerParams(dimension_semantics=("arbitrary",) * n_axes, vmem_limit_bytes=vmem)


def _dot(a, b):
    return jnp.dot(a, b, preferred_element_type=F32)


def _dot_nt(a, b):
    return lax.dot_general(a, b, (((1,), (1,)), ((), ())), preferred_element_type=F32)


def _rms(x, g):
    return x * lax.rsqrt(jnp.mean(x * x, axis=-1, keepdims=True) + EPS) * g


def _inproj_kernel(x_ref, g_ref, w_ref, q_ref, kv_ref, kc_ref, vc_ref, gt_ref, ret_ref):
    h = _rms(x_ref[...], g_ref[...]).astype(BF16)
    q_ref[...] = _dot(h, w_ref[:, _C_Q:_C_KV])
    kv_ref[...] = _dot(h, w_ref[:, _C_KV:_C_KC])
    kc_ref[...] = _dot(h, w_ref[:, _C_KC:_C_VC])
    vc_ref[...] = _dot(h, w_ref[:, _C_VC:_C_GT])
    gt_ref[...] = _dot(h, w_ref[:, _C_GT:_C_RET])
    ret_ref[...] = _dot(h, w_ref[:, _C_RET:_C_END])


def _inproj(x2, g, w, tm=512):
    n, d = x2.shape
    widths = (NSA_WIDTH, 4 * KV_WIDTH, KV_WIDTH, KV_WIDTH, LANES, _RET_COLS)
    return pl.pallas_call(
        _inproj_kernel,
        grid=(n // tm,),
        in_specs=[pl.BlockSpec((tm, d), lambda i: (i, 0)),
                  pl.BlockSpec((1, d), lambda i: (0, 0)),
                  pl.BlockSpec((d, _C_END), lambda i: (0, 0))],
        out_specs=[pl.BlockSpec((tm, c), lambda i: (i, 0)) for c in widths],
        out_shape=[jax.ShapeDtypeStruct((n, c), F32) for c in widths],
        compiler_params=_params(1),
        name="inproj",
    )(x2, g, w)


def _prep_kernel(q_ref, kv_ref, gt_ref, qg_ref, kg_ref, qT_ref, ks_ref, kw_ref, vsT_ref, vwT_ref, gT_ref):
    q = q_ref[...]
    qg = qg_ref[...]
    scale = HEAD_DIM ** -0.5 * LOG2E
    heads = []
    for h in range(NSA_HEADS):
        heads.append(_rms(q[:, h * HEAD_DIM:(h + 1) * HEAD_DIM], qg) * scale)
    qt = jnp.concatenate(heads, axis=1).T
    for g in range(NSA_KV_HEADS):
        for r in range(NSA_GROUP):
            h = g * NSA_GROUP + r
            qT_ref[g, :, r * Q_BLOCK:(r + 1) * Q_BLOCK] = qt[h * HEAD_DIM:(h + 1) * HEAD_DIM, :].astype(BF16)
    kv = kv_ref[...]
    ks, kw = kv[:, 0:KV_WIDTH], kv[:, KV_WIDTH:2 * KV_WIDTH]
    vst = kv[:, 2 * KV_WIDTH:3 * KV_WIDTH].T
    vwt = kv[:, 3 * KV_WIDTH:4 * KV_WIDTH].T
    pos = pl.program_id(1) * Q_BLOCK + lax.broadcasted_iota(jnp.int32, (Q_BLOCK, HEAD_DIM), 0)
    col = lax.broadcasted_iota(jnp.int32, (Q_BLOCK, HEAD_DIM), 1)
    kpos = jnp.where(col < 3, pos // SLC_LEN, jnp.where(col < 6, pos % SLC_LEN, 0)).astype(F32)
    for g in range(NSA_KV_HEADS):
        sl = slice(g * HEAD_DIM, (g + 1) * HEAD_DIM)
        ks_ref[g] = jnp.concatenate([_rms(ks[:, sl], kg_ref[0:1, :]), kpos], axis=1).astype(BF16)
        kw_ref[g] = jnp.concatenate([_rms(kw[:, sl], kg_ref[1:2, :]), kpos], axis=1).astype(BF16)
        vsT_ref[g] = vst[sl, :].astype(BF16)
        vwT_ref[g] = vwt[sl, :].astype(BF16)
    gT_ref[...] = jax.nn.sigmoid(gt_ref[...].T[0:32, :])


def _prep(q, kv, gt, qg, kg, B, T):
    nq = T // Q_BLOCK
    G = NSA_KV_HEADS
    row = lambda b, i: (b * nq + i, 0)
    return pl.pallas_call(
        _prep_kernel,
        grid=(B, nq),
        in_specs=[pl.BlockSpec((Q_BLOCK, NSA_WIDTH), row),
                  pl.BlockSpec((Q_BLOCK, 4 * KV_WIDTH), row),
                  pl.BlockSpec((Q_BLOCK, LANES), row),
                  pl.BlockSpec((1, HEAD_DIM), lambda b, i: (0, 0)),
                  pl.BlockSpec((2, HEAD_DIM), lambda b, i: (0, 0))],
        out_specs=[pl.BlockSpec((None, G, None, HEAD_DIM, GQ), lambda b, i: (b, 0, i, 0, 0)),
                   pl.BlockSpec((None, G, Q_BLOCK, 2 * HEAD_DIM), lambda b, i: (b, 0, i, 0)),
                   pl.BlockSpec((None, G, Q_BLOCK, 2 * HEAD_DIM), lambda b, i: (b, 0, i, 0)),
                   pl.BlockSpec((None, G, HEAD_DIM, Q_BLOCK), lambda b, i: (b, 0, 0, i)),
                   pl.BlockSpec((None, G, HEAD_DIM, Q_BLOCK), lambda b, i: (b, 0, 0, i)),
                   pl.BlockSpec((None, None, 32, Q_BLOCK), lambda b, i: (b, i, 0, 0))],
        out_shape=[jax.ShapeDtypeStruct((B, G, nq, HEAD_DIM, GQ), BF16),
                   jax.ShapeDtypeStruct((B, G, T, 2 * HEAD_DIM), BF16),
                   jax.ShapeDtypeStruct((B, G, T, 2 * HEAD_DIM), BF16),
                   jax.ShapeDtypeStruct((B, G, HEAD_DIM, T), BF16),
                   jax.ShapeDtypeStruct((B, G, HEAD_DIM, T), BF16),
                   jax.ShapeDtypeStruct((B, nq, 32, Q_BLOCK), F32)],
        compiler_params=_params(2),
        name="nsa_prep",
    )(q, kv, gt, qg, kg)


def _compress_kernel(kc_ref, vc_ref, wk_ref, wv_ref, pk_ref, pv_ref, kg_ref, kcmp_ref, vcT_ref):
    ncp = kc_ref.shape[0]

    def comp(a, w_ref, p_ref):
        lo = _dot((a + p_ref[0:1, :]).astype(BF16), w_ref[0])
        hi = _dot((a + p_ref[1:2, :]).astype(BF16), w_ref[1])
        return lo + pltpu.roll(hi, ncp - 1, 0)

    k = comp(kc_ref[...], wk_ref, pk_ref)
    v = comp(vc_ref[...], wv_ref, pv_ref).T
    for g in range(NSA_KV_HEADS):
        sl = slice(g * HEAD_DIM, (g + 1) * HEAD_DIM)
        kcmp_ref[g] = _rms(k[:, sl], kg_ref[...]).astype(BF16)
        vcT_ref[g] = v[sl, :].astype(BF16)


def _compress(kc, vc, wk, wv, pk, pv, kg, B, T):
    ncp = T // CMP_STRIDE
    G = NSA_KV_HEADS
    hw = CMP_STRIDE * KV_WIDTH
    kc = kc.reshape(B, ncp, hw)
    vc = vc.reshape(B, ncp, hw)
    const3 = lambda b: (0, 0, 0)
    const2 = lambda b: (0, 0)
    return pl.pallas_call(
        _compress_kernel,
        grid=(B,),
        in_specs=[pl.BlockSpec((None, ncp, hw), lambda b: (b, 0, 0)),
                  pl.BlockSpec((None, ncp, hw), lambda b: (b, 0, 0)),
                  pl.BlockSpec((2, hw, KV_WIDTH), const3),
                  pl.BlockSpec((2, hw, KV_WIDTH), const3),
                  pl.BlockSpec((2, hw), const2),
                  pl.BlockSpec((2, hw), const2),
                  pl.BlockSpec((1, HEAD_DIM), const2)],
        out_specs=[pl.BlockSpec((None, G, ncp, HEAD_DIM), lambda b: (b, 0, 0, 0)),
                   pl.BlockSpec((None, G, HEAD_DIM, ncp), lambda b: (b, 0, 0, 0))],
        out_shape=[jax.ShapeDtypeStruct((B, G, ncp, HEAD_DIM), BF16),
                   jax.ShapeDtypeStruct((B, G, HEAD_DIM, ncp), BF16)],
        compiler_params=_params(1),
        name="nsa_compress",
    )(kc, vc, wk, wv, pk, pv, kg)


def _compress_weights(w, pos):
    G = NSA_KV_HEADS
    w4 = w.reshape(2, CMP_STRIDE, HEAD_DIM, HEAD_DIM)
    eye = jnp.eye(G, dtype=w.dtype)
    wbd = jnp.einsum('hlde,gk->hlgdke', w4, eye).reshape(2, CMP_STRIDE * KV_WIDTH, KV_WIDTH)
    p = pos.reshape(2, CMP_STRIDE, 1, HEAD_DIM)
    p = jnp.broadcast_to(p, (2, CMP_STRIDE, G, HEAD_DIM)).reshape(2, CMP_STRIDE * KV_WIDTH)
    return wbd.astype(BF16), p


def _split3(x):
    hi = x.astype(BF16)
    r = x - hi.astype(F32)
    mid = r.astype(BF16)
    lo = (r - mid.astype(F32)).astype(BF16)
    return hi, mid, lo


def _nsa_cmp_kernel(qT_ref, kc_ref, vcT_ref, ovT_ref, pair_ref, slope_ref, ocmp_ref, sel_ref, list_ref, cnt_ref, *,
                    n_sel):
    ncp = kc_ref.shape[0]
    ns = ovT_ref.shape[0]
    i = pl.program_id(2)
    q0 = i * Q_BLOCK
    s = _dot(kc_ref[...], qT_ref[...])
    lane = lax.broadcasted_iota(jnp.int32, (1, GQ), 1)
    t_row = q0 + (lane & (Q_BLOCK - 1))
    cend = lax.broadcasted_iota(jnp.int32, (ncp, 1), 0) * CMP_STRIDE + (CMP_LEN - 1)
    dist = t_row - cend
    s = jnp.where(dist >= 0, s - slope_ref[0:1, :] * dist.astype(F32), NEG)
    m = jnp.max(s, axis=0, keepdims=True)
    e = jnp.exp2(s - m)
    p = e / jnp.sum(e, axis=0, keepdims=True) * (t_row >= CMP_LEN - 1).astype(F32)
    ocmp_ref[...] = _dot(vcT_ref[...], p.astype(BF16))

    ps = p[:, 0:Q_BLOCK]
    for r in range(1, NSA_GROUP):
        ps = ps + p[:, r * Q_BLOCK:(r + 1) * Q_BLOCK]
    ov = ovT_ref[...]
    hi, mid, lo = _split3(ps)
    imp = _dot(ov, hi) + _dot(ov, mid) + _dot(ov, lo)

    blk = lax.broadcasted_iota(jnp.int32, (ns, 1), 0)
    tq = q0 + lax.broadcasted_iota(jnp.int32, (1, Q_BLOCK), 1)
    cur = tq // SLC_LEN
    forced = (blk == 0) | (blk == cur) | (blk == cur - 1)
    valid = blk * SLC_LEN <= tq
    imp = jnp.where(forced, BIG, jnp.where(valid, imp, -BIG))
    blk_f = blk.astype(F32)
    sel = jnp.zeros((ns, Q_BLOCK), F32)
    for _ in range(n_sel):
        mx = jnp.max(imp, axis=0, keepdims=True)
        idx = jnp.min(jnp.where(imp == mx, blk_f, float(ns)), axis=0, keepdims=True)
        pick = blk_f == idx
        sel = jnp.where(pick, 1.0, sel)
        imp = jnp.where(pick, -3e38, imp)
    sel_ref[...] = sel
    cnt = _dot_nt(jnp.ones((8, Q_BLOCK), BF16), sel.astype(BF16))
    step = lax.broadcasted_iota(jnp.int32, (1, ns), 1)
    need = ((_dot(cnt.astype(BF16), pair_ref[...]) > 0) & (step < i)).astype(F32)
    before = (lax.broadcasted_iota(jnp.int32, (ns, ns), 0) < step).astype(BF16)
    slot = _dot(need.astype(BF16), before)
    slot_iota = lax.broadcasted_iota(jnp.int32, (ns, 1), 0).astype(F32)
    place = ((slot[0:1, :] == slot_iota) & (need[0:1, :] > 0)).astype(BF16)
    steps = _dot_nt(jnp.broadcast_to(step.astype(BF16), (8, ns)), place)
    total = jnp.sum(need, axis=1, keepdims=True)
    list_ref[...] = jnp.where(step.astype(F32) < total, steps, -1.0).astype(jnp.int32)
    cnt_ref[...] = jnp.broadcast_to(total, cnt_ref.shape).astype(jnp.int32)


def _nsa_cmp(qT, kcmp, vcT, ovT, pair, slopes, B, T):
    G = NSA_KV_HEADS
    nq = T // Q_BLOCK
    ncp = T // CMP_STRIDE
    ns = T // SLC_LEN
    n_sel = min(SLC_TOPK, ns)
    return pl.pallas_call(
        functools.partial(_nsa_cmp_kernel, n_sel=n_sel),
        grid=(B, G, nq),
        in_specs=[pl.BlockSpec((None, None, None, HEAD_DIM, GQ), lambda b, g, i: (b, g, i, 0, 0)),
                  pl.BlockSpec((None, None, ncp, HEAD_DIM), lambda b, g, i: (b, g, 0, 0)),
                  pl.BlockSpec((None, None, HEAD_DIM, ncp), lambda b, g, i: (b, g, 0, 0)),
                  pl.BlockSpec((ns, ncp), lambda b, g, i: (0, 0)),
                  pl.BlockSpec((ns, ns), lambda b, g, i: (0, 0)),
                  pl.BlockSpec((None, 8, GQ), lambda b, g, i: (g, 0, 0))],
        out_specs=[pl.BlockSpec((None, None, None, HEAD_DIM, GQ), lambda b, g, i: (b, g, i, 0, 0)),
                   pl.BlockSpec((None, None, None, ns, Q_BLOCK), lambda b, g, i: (b, g, i, 0, 0)),
                   pl.BlockSpec((None, None, None, 8, ns), lambda b, g, i: (b, g, i, 0, 0)),
                   pl.BlockSpec((None, None, None, 8, ns), lambda b, g, i: (b, g, i, 0, 0))],
        out_shape=[jax.ShapeDtypeStruct((B, G, nq, HEAD_DIM, GQ), F32),
                   jax.ShapeDtypeStruct((B, G, nq, ns, Q_BLOCK), F32),
                   jax.ShapeDtypeStruct((B, G, nq, 8, ns), jnp.int32),
                   jax.ShapeDtypeStruct((B, G, nq, 8, ns), jnp.int32)],
        compiler_params=_params(3),
        name="nsa_cmp",
    )(qT, kcmp, vcT, ovT, pair, slopes)


def _nsa_main_kernel(list_ref, cnt_ref, qT_ref, qaug_ref, ks_ref, vsT_ref, kw_ref, vwT_ref, sel_ref, gT_ref, ocmp_ref,
                     lowb_ref, causb_ref, out_ref, m_sc, l_sc, acc_sc):
    b, g, i = pl.program_id(0), pl.program_id(1), pl.program_id(2)
    tile_id = (b * pl.num_programs(1) + g) * pl.num_programs(2) + i
    n_steps = sel_ref.shape[0] // 2
    q = jnp.concatenate([qT_ref[...], qaug_ref[...]], axis=0)
    k0 = pl.multiple_of(i * Q_BLOCK, Q_BLOCK)

    def sel_bias(j, valid):
        def row(r):
            picked = (sel_ref[pl.ds(r, 1), :] > 0.5) & valid
            return jnp.concatenate([jnp.where(picked, 0.0, NEG)] * NSA_GROUP, axis=1)
        return row(2 * j), row(2 * j + 1)

    def add_sel_bias(s, j, valid):
        ba, bb = sel_bias(j, valid)
        return jnp.concatenate([s[0:SLC_LEN] + ba, s[SLC_LEN:] + bb], axis=0)

    s = _dot(kw_ref[pl.ds(k0, WIN_KEYS), :], q)
    s = jnp.concatenate([s[0:Q_BLOCK] + lowb_ref[...], s[Q_BLOCK:WIN], s[WIN:] + causb_ref[...]], axis=0)
    m = jnp.max(s, axis=0, keepdims=True)
    p = jnp.exp2(s - m)
    o_win = _dot(vwT_ref[:, pl.ds(k0, WIN_KEYS)], p.astype(BF16)) / jnp.sum(p, axis=0, keepdims=True)

    s = _dot(ks_ref[pl.ds(k0, KEY_STEP), :], q)
    s = add_sel_bias(s, i, True) + causb_ref[...]
    m = jnp.max(s, axis=0, keepdims=True)
    p = jnp.exp2(s - m)
    m_sc[...] = m
    l_sc[...] = jnp.sum(p, axis=0, keepdims=True)
    acc_sc[...] = _dot(vsT_ref[:, pl.ds(k0, KEY_STEP)], p.astype(BF16))

    def group(t, c):
        ks, vs, biases = [], [], []
        for x in range(STEP_GROUP):
            j = list_ref[tile_id * n_steps + t * STEP_GROUP + x]
            valid = j >= 0
            j = jnp.maximum(j, 0)
            kj = pl.multiple_of(j * KEY_STEP, KEY_STEP)
            ks.append(ks_ref[pl.ds(kj, KEY_STEP), :])
            vs.append(vsT_ref[:, pl.ds(kj, KEY_STEP)])
            biases.append((j, valid))
        s = _dot(jnp.concatenate(ks, axis=0), q)
        s = jnp.concatenate([add_sel_bias(s[x * KEY_STEP:(x + 1) * KEY_STEP], *biases[x])
                             for x in range(STEP_GROUP)], axis=0)
        m_old = m_sc[...]
        m_new = jnp.maximum(m_old, jnp.max(s, axis=0, keepdims=True))
        p = jnp.exp2(s - m_new)
        alpha = jnp.exp2(m_old - m_new)
        l_sc[...] = alpha * l_sc[...] + jnp.sum(p, axis=0, keepdims=True)
        acc_sc[...] = alpha * acc_sc[...] + _dot(jnp.concatenate(vs, axis=1), p.astype(BF16))
        m_sc[...] = m_new
        return c

    lax.fori_loop(0, (cnt_ref[tile_id] + (STEP_GROUP - 1)) // STEP_GROUP, group, 0)
    o_slc = acc_sc[...] / l_sc[...]

    def gate(k):
        rows = [gT_ref[pl.ds(g * (NSA_GROUP * 3) + r * 3 + k, 1), :] for r in range(NSA_GROUP)]
        return jnp.concatenate(rows, axis=1)

    o = gate(0) * ocmp_ref[...] + gate(1) * o_slc + gate(2) * o_win
    o = jnp.concatenate([o, jnp.zeros_like(o)], axis=0)
    for r in range(NSA_GROUP):
        out_ref[:, r * HEAD_DIM:(r + 1) * HEAD_DIM] = o[:, r * Q_BLOCK:(r + 1) * Q_BLOCK].T[:, 0:HEAD_DIM]


def _nsa_main(lists, counts, qT, qaug, ks, vsT, kw, vwT, sel, gT, ocmp, lowb, causb, B, T):
    G = NSA_KV_HEADS
    nq = T // Q_BLOCK
    ns = T // SLC_LEN
    whole = lambda b, g, i, *_: (b, g, 0, 0)
    tile = lambda b, g, i, *_: (b, g, i, 0, 0)
    const = lambda b, g, i, *_: (0, 0)
    grid_spec = pltpu.PrefetchScalarGridSpec(
        num_scalar_prefetch=2,
        grid=(B, G, nq),
        in_specs=[pl.BlockSpec((None, None, None, HEAD_DIM, GQ), tile),
                  pl.BlockSpec((None, HEAD_DIM, GQ), lambda b, g, i, *_: (g, 0, 0)),
                  pl.BlockSpec((None, None, T, 2 * HEAD_DIM), whole),
                  pl.BlockSpec((None, None, HEAD_DIM, T), whole),
                  pl.BlockSpec((None, None, T + WIN, 2 * HEAD_DIM), whole),
                  pl.BlockSpec((None, None, HEAD_DIM, T + WIN), whole),
                  pl.BlockSpec((None, None, None, ns, Q_BLOCK), tile),
                  pl.BlockSpec((None, None, 32, Q_BLOCK), lambda b, g, i, *_: (b, i, 0, 0)),
                  pl.BlockSpec((None, None, None, HEAD_DIM, GQ), tile),
                  pl.BlockSpec((Q_BLOCK, GQ), const),
                  pl.BlockSpec((Q_BLOCK, GQ), const)],
        out_specs=pl.BlockSpec((None, Q_BLOCK, NSA_GROUP * HEAD_DIM), lambda b, g, i, *_: (b, i, g)),
        scratch_shapes=[pltpu.VMEM((1, GQ), F32), pltpu.VMEM((1, GQ), F32), pltpu.VMEM((HEAD_DIM, GQ), F32)],
    )
    return pl.pallas_call(
        _nsa_main_kernel,
        grid_spec=grid_spec,
        out_shape=jax.ShapeDtypeStruct((B, T, NSA_WIDTH), F32),
        compiler_params=_params(3),
        name="nsa_main",
    )(lists, counts, qT, qaug, ks, vsT, kw, vwT, sel, gT, ocmp, lowb, causb)


def _ret_kernel(p_ref, decay_ref, xi_ref, zeta_ref, gch_ref, ng_ref, out_ref, state_ref):
    @pl.when(pl.program_id(1) == 0)
    def _():
        state_ref[...] = jnp.zeros(state_ref.shape, F32)

    kw = RET_HEADS * RET_DK
    p = p_ref[...]
    rq = p[:, 0:kw] * (RET_DK ** -0.5)
    rk = p[:, kw:2 * kw]
    rkT = rk.T
    rv = p[:, 2 * kw:2 * kw + RET_WIDTH]
    rg = p[:, 2 * kw + RET_WIDTH:2 * kw + 2 * RET_WIDTH]
    xi = xi_ref[...]
    outs = []
    for h in range(RET_HEADS):
        qh = rq[:, h * RET_DK:(h + 1) * RET_DK]
        kh = rk[:, h * RET_DK:(h + 1) * RET_DK]
        khT = rkT[h * RET_DK:(h + 1) * RET_DK, :]
        vh = rv[:, h * RET_DV:(h + 1) * RET_DV]
        st = state_ref[h]
        inner = _dot_nt(qh, kh) * decay_ref[h]
        o = _dot(inner, vh) + _dot(qh, st) * xi[:, h:h + 1]
        state_ref[h] = st * gch_ref[h:h + 1, 0:1] + _dot(khT * zeta_ref[h:h + 1, :], vh)
        outs.append(_rms(o, ng_ref[:, h * RET_DV:(h + 1) * RET_DV]))
    out_ref[...] = jnp.concatenate(outs, axis=1) * (rg * jax.nn.sigmoid(rg))


def _ret_consts():
    H, C = RET_HEADS, RET_CHUNK
    log_g = jnp.log1p(-jnp.exp2(-5.0 - jnp.arange(H, dtype=F32)))
    idx = jnp.arange(C, dtype=F32)
    diff = idx[:, None] - idx[None, :]
    decay = jnp.where(diff >= 0, jnp.exp(jnp.maximum(diff, 0.0) * log_g[:, None, None]), 0.0)
    zeta = jnp.exp((C - 1 - idx) * log_g[:, None])
    xi = jnp.exp((idx + 1) * log_g[:, None]).T
    g_chunk = jnp.broadcast_to(jnp.exp(C * log_g)[:, None], (H, LANES))
    return decay, xi, zeta, g_chunk


def _retention(pret, ng, B, T):
    nch = T // RET_CHUNK
    decay, xi, zeta, gch = _ret_consts()
    c2 = lambda b, c: (0, 0)
    return pl.pallas_call(
        _ret_kernel,
        grid=(B, nch),
        in_specs=[pl.BlockSpec((RET_CHUNK, _RET_COLS), lambda b, c: (b * nch + c, 0)),
                  pl.BlockSpec((RET_HEADS, RET_CHUNK, RET_CHUNK), lambda b, c: (0, 0, 0)),
                  pl.BlockSpec((RET_CHUNK, RET_HEADS), c2),
                  pl.BlockSpec((RET_HEADS, RET_CHUNK), c2),
                  pl.BlockSpec((RET_HEADS, LANES), c2),
                  pl.BlockSpec((1, RET_WIDTH), c2)],
        out_specs=pl.BlockSpec((RET_CHUNK, RET_WIDTH), lambda b, c: (b * nch + c, 0)),
        out_shape=jax.ShapeDtypeStruct((B * T, RET_WIDTH), F32),
        scratch_shapes=[pltpu.VMEM((RET_HEADS, RET_DK, RET_DV), F32)],
        compiler_params=_params(2),
        name="retention",
    )(pret, decay, xi, zeta, gch, ng)


def _outproj_kernel(x_ref, nsa_ref, ret_ref, w_ref, o_ref):
    o_ref[...] = (x_ref[...] + _dot(nsa_ref[...].astype(BF16), w_ref[0:NSA_WIDTH, :])
                  + _dot(ret_ref[...].astype(BF16), w_ref[NSA_WIDTH:, :]))


def _outproj(x2, nsa, ret, w, tm=512):
    n, d = x2.shape
    return pl.pallas_call(
        _outproj_kernel,
        grid=(n // tm,),
        in_specs=[pl.BlockSpec((tm, d), lambda i: (i, 0)),
                  pl.BlockSpec((tm, NSA_WIDTH), lambda i: (i, 0)),
                  pl.BlockSpec((tm, RET_WIDTH), lambda i: (i, 0)),
                  pl.BlockSpec((NSA_WIDTH + RET_WIDTH, d), lambda i: (0, 0))],
        out_specs=pl.BlockSpec((tm, d), lambda i: (i, 0)),
        out_shape=jax.ShapeDtypeStruct((n, d), F32),
        compiler_params=_params(1),
        name="outproj",
    )(x2, nsa, ret, w)


def _ffn_kernel(x_ref, g_ref, wg_ref, wu_ref, wd_ref, o_ref, h_sc):
    f = pl.program_id(1)

    @pl.when(f == 0)
    def _():
        x = x_ref[...]
        h_sc[...] = _rms(x, g_ref[...]).astype(BF16)
        o_ref[...] = x

    h = h_sc[...]
    a = _dot(h, wg_ref[...])
    act = (a * jax.nn.sigmoid(a) * _dot(h, wu_ref[...])).astype(BF16)
    o_ref[...] += _dot(act, wd_ref[...])


def _ffn(x2, g, wg, wu, wd, tm=512, fc=1408):
    n, d = x2.shape
    dff = wg.shape[1]
    return pl.pallas_call(
        _ffn_kernel,
        grid=(n // tm, dff // fc),
        in_specs=[pl.BlockSpec((tm, d), lambda i, f: (i, 0)),
                  pl.BlockSpec((1, d), lambda i, f: (0, 0)),
                  pl.BlockSpec((d, fc), lambda i, f: (0, f)),
                  pl.BlockSpec((d, fc), lambda i, f: (0, f)),
                  pl.BlockSpec((fc, d), lambda i, f: (f, 0))],
        out_specs=pl.BlockSpec((tm, d), lambda i, f: (i, 0)),
        out_shape=jax.ShapeDtypeStruct((n, d), F32),
        scratch_shapes=[pltpu.VMEM((tm, d), BF16)],
        compiler_params=_params(2),
        name="ffn_dense",
    )(x2, g, wg, wu, wd)


def _router_kernel(x_ref, g_ref, r_ref, rb_ref, tri_ref, h_ref, rank_ref, comb_ref, rankT_ref, cnt_ref):
    h = _rms(x_ref[...], g_ref[...])
    h_ref[...] = h.astype(BF16)
    hh, hm, hl = _split3(h)
    rh, rm, rl = _split3(r_ref[...])
    logits = (_dot(hh, rh) + (_dot(hh, rm) + _dot(hm, rh)) + (_dot(hh, rl) + _dot(hm, rm) + _dot(hl, rh))
              + rb_ref[...])
    lane = lax.broadcasted_iota(jnp.int32, logits.shape, 1).astype(F32)
    logits = jnp.where(lane < N_EXPERTS, logits, NEG)
    m1 = jnp.max(logits, axis=1, keepdims=True)
    i1 = jnp.min(jnp.where(logits == m1, lane, float(LANES)), axis=1, keepdims=True)
    l2 = jnp.where(lane == i1, NEG, logits)
    m2 = jnp.max(l2, axis=1, keepdims=True)
    i2 = jnp.min(jnp.where(l2 == m2, lane, float(LANES)), axis=1, keepdims=True)
    e2 = jnp.exp(m2 - m1)
    w1 = 1.0 / (1.0 + e2)
    w2 = e2 / (1.0 + e2)
    use1, use2 = lane == i1, lane == i2
    comb_ref[...] = jnp.where(use1, w1, 0.0) + jnp.where(use2, w2, 0.0)
    use = (use1 | use2).astype(F32)
    rank = jnp.where(use > 0, _dot(tri_ref[...], use.astype(BF16)), -1.0)
    rank_ref[...] = rank
    rankT_ref[...] = rank.T[0:N_EXPERTS, :]
    cnt_ref[...] = jnp.broadcast_to(jnp.sum(use, axis=0, keepdims=True), cnt_ref.shape).astype(jnp.int32)


def _router(x2, g, router, rb, tm):
    n, d = x2.shape
    nt = n // tm
    tri = (jnp.arange(tm)[:, None] > jnp.arange(tm)[None, :]).astype(BF16)
    rpad = jnp.zeros((d, LANES), F32).at[:, :N_EXPERTS].set(router)
    rbpad = jnp.zeros((1, LANES), F32).at[0, :N_EXPERTS].set(rb)
    c2 = lambda i: (0, 0)
    return pl.pallas_call(
        _router_kernel,
        grid=(nt,),
        in_specs=[pl.BlockSpec((tm, d), lambda i: (i, 0)),
                  pl.BlockSpec((1, d), c2),
                  pl.BlockSpec((d, LANES), c2),
                  pl.BlockSpec((1, LANES), c2),
                  pl.BlockSpec((tm, tm), c2)],
        out_specs=[pl.BlockSpec((tm, d), lambda i: (i, 0)),
                   pl.BlockSpec((tm, LANES), lambda i: (i, 0)),
                   pl.BlockSpec((tm, LANES), lambda i: (i, 0)),
                   pl.BlockSpec((N_EXPERTS, tm), lambda i: (0, i)),
                   pl.BlockSpec((None, 8, LANES), lambda i: (i, 0, 0))],
        out_shape=[jax.ShapeDtypeStruct((n, d), BF16),
                   jax.ShapeDtypeStruct((n, LANES), F32),
                   jax.ShapeDtypeStruct((n, LANES), F32),
                   jax.ShapeDtypeStruct((N_EXPERTS, n), F32),
                   jax.ShapeDtypeStruct((nt, 8, LANES), jnp.int32)],
        compiler_params=_params(1),
        name="moe_router",
    )(x2, g, rpad, rbpad, tri)


MOE_SUB = 128


def _moe_kernel(cnt_ref, h_ref, rankT_ref, rank_ref, comb_ref, wg_ref, wu_ref, wd_ref, x_ref, o_ref, hc_sc, oacc_sc):
    t, e, f = pl.program_id(0), pl.program_id(1), pl.program_id(2)
    nf = pl.num_programs(2)
    tm = h_ref.shape[0]
    nsub = (cnt_ref[t * N_EXPERTS + e] + (MOE_SUB - 1)) // MOE_SUB

    @pl.when((e == 0) & (f == 0))
    def _():
        o_ref[...] = x_ref[...]

    @pl.when(f == 0)
    def _():
        rank_row = rankT_ref[...]

        def gather(s, c):
            r0 = pl.multiple_of(s * MOE_SUB, MOE_SUB)
            rows = (lax.broadcasted_iota(jnp.int32, (MOE_SUB, 1), 0) + r0).astype(F32)
            onehot = (rows == rank_row).astype(BF16)
            hc_sc[pl.ds(r0, MOE_SUB), :] = _dot(onehot, h_ref[...]).astype(BF16)
            oacc_sc[pl.ds(r0, MOE_SUB), :] = jnp.zeros((MOE_SUB, oacc_sc.shape[1]), F32)
            return c

        lax.fori_loop(0, nsub, gather, 0)

    def expert(s, c):
        r0 = pl.multiple_of(s * MOE_SUB, MOE_SUB)
        rows = hc_sc[pl.ds(r0, MOE_SUB), :]
        a = _dot(rows, wg_ref[...])
        act = (a * jax.nn.sigmoid(a) * _dot(rows, wu_ref[...])).astype(BF16)
        oacc_sc[pl.ds(r0, MOE_SUB), :] += _dot(act, wd_ref[...])
        return c

    lax.fori_loop(0, nsub, expert, 0)

    @pl.when(f == nf - 1)
    def _():
        is_e = lax.broadcasted_iota(jnp.int32, (1, LANES), 1) == e
        rank_col = jnp.sum(jnp.where(is_e, rank_ref[...], 0.0), axis=1, keepdims=True)
        comb_col = jnp.sum(jnp.where(is_e, comb_ref[...], 0.0), axis=1, keepdims=True)

        def scatter(s, c):
            r0 = pl.multiple_of(s * MOE_SUB, MOE_SUB)
            cols = (lax.broadcasted_iota(jnp.int32, (1, MOE_SUB), 1) + r0).astype(F32)
            onehot = (rank_col == cols).astype(BF16)
            y = _dot(onehot, oacc_sc[pl.ds(r0, MOE_SUB), :].astype(BF16))
            o_ref[...] += comb_col * y
            return c

        lax.fori_loop(0, nsub, scatter, 0)


def _moe(counts, h, rankT, rank, comb, wg, wu, wd, x2, tm, fc=1408):
    n, d = x2.shape
    dff = wg.shape[2]
    grid_spec = pltpu.PrefetchScalarGridSpec(
        num_scalar_prefetch=1,
        grid=(n // tm, N_EXPERTS, dff // fc),
        in_specs=[pl.BlockSpec((tm, d), lambda t, e, f, c: (t, 0)),
                  pl.BlockSpec((None, 1, tm), lambda t, e, f, c: (e, 0, t)),
                  pl.BlockSpec((tm, LANES), lambda t, e, f, c: (t, 0)),
                  pl.BlockSpec((tm, LANES), lambda t, e, f, c: (t, 0)),
                  pl.BlockSpec((None, d, fc), lambda t, e, f, c: (e, 0, f)),
                  pl.BlockSpec((None, d, fc), lambda t, e, f, c: (e, 0, f)),
                  pl.BlockSpec((None, fc, d), lambda t, e, f, c: (e, f, 0)),
                  pl.BlockSpec((tm, d), lambda t, e, f, c: (t, 0))],
        out_specs=pl.BlockSpec((tm, d), lambda t, e, f, c: (t, 0)),
        scratch_shapes=[pltpu.VMEM((tm, d), BF16), pltpu.VMEM((tm, d), F32)],
    )
    return pl.pallas_call(
        _moe_kernel,
        grid_spec=grid_spec,
        out_shape=jax.ShapeDtypeStruct((n, d), F32),
        compiler_params=_params(3),
        name="moe_experts",
    )(counts, h, rankT.reshape(N_EXPERTS, 1, n), rank, comb, wg, wu, wd, x2)


def _permute_w_in(w):
    o = np.cumsum((0, NSA_WIDTH) + (KV_WIDTH,) * 6 + (3 * NSA_HEADS,))
    q, kc, vc, ks, vs, kw, vw, gts = (w[:, o[k]:o[k + 1]] for k in range(8))
    ret = w[:, o[8]:]
    pad = jnp.zeros((w.shape[0], LANES - 3 * NSA_HEADS), w.dtype)
    return jnp.concatenate([q, ks, kw, vs, vw, kc, vc, gts, pad, ret], axis=1).astype(BF16)


def _nsa_consts(T):
    ncp = T // CMP_STRIDE
    ns = T // SLC_LEN
    cs = np.arange(ncp) * CMP_STRIDE
    ss = np.arange(ns) * SLC_LEN
    ov = np.clip(np.minimum(cs[None, :] + CMP_LEN, ss[:, None] + SLC_LEN) - np.maximum(cs[None, :], ss[:, None]), 0, None)
    ovT = (ov.astype(np.float32) / CMP_LEN)
    ovT[:, ncp - 1] = 0.0
    pair = (np.arange(ns)[:, None] // 2 == np.arange(ns)[None, :]).astype(np.float32)
    h = np.arange(NSA_HEADS).reshape(NSA_KV_HEADS, NSA_GROUP) + 1
    slopes = np.exp2(-8.0 * h / NSA_HEADS).astype(np.float32)
    slopes = np.repeat(slopes, Q_BLOCK, axis=1)
    parts, rest = [], np.float64(LOG2E)
    for _ in range(3):
        part = np.float64(np.asarray(rest).astype(BF16))
        parts.append(part)
        rest = rest - part
    qaug = np.zeros((NSA_KV_HEADS, HEAD_DIM, GQ), np.float32)
    for k, part in enumerate(parts):
        qaug[:, k, :] = part * SLC_LEN * slopes
        qaug[:, 3 + k, :] = part * slopes
    kq = np.arange(Q_BLOCK)[:, None] - np.tile(np.arange(Q_BLOCK), NSA_GROUP)[None, :]
    causb = np.where(kq <= 0, 0.0, NEG).astype(np.float32)
    lowb = np.where(kq > 0, 0.0, NEG).astype(np.float32)
    slopes_l2 = (slopes * np.float32(LOG2E))[:, None, :].repeat(8, axis=1)
    return (jnp.asarray(ovT, BF16), jnp.asarray(pair, BF16), jnp.asarray(slopes_l2, F32), jnp.asarray(qaug, BF16),
            jnp.asarray(lowb), jnp.asarray(causb))


def _mixer(x2, B, T, norm_g, w_in, q_norm_g, k_norm_g, cmp_pos, w_cmp, ret_norm_g, w_out):
    ns = T // SLC_LEN
    q, kv, kc, vc, gt, pret = _inproj(x2, norm_g[None, :], _permute_w_in(w_in))
    qT, ks, kw, vsT, vwT, gT = _prep(q, kv, gt, q_norm_g[None, :], k_norm_g[1:3], B, T)
    wk, pk = _compress_weights(w_cmp[0], cmp_pos[0])
    wv, pv = _compress_weights(w_cmp[1], cmp_pos[1])
    kcmp, vcT = _compress(kc, vc, wk, wv, pk, pv, k_norm_g[0:1], B, T)
    ovT, pair, slopes, qaug, lowb, causb = _nsa_consts(T)
    ocmp, sel, lists, counts = _nsa_cmp(qT, kcmp, vcT, ovT, pair, slopes, B, T)
    lists = lists[:, :, :, 0, :ns // 2].reshape(-1)
    counts = counts[:, :, :, 0, 0].reshape(-1)
    kpad = jnp.zeros((WIN, 2 * HEAD_DIM), BF16).at[:, HEAD_DIM:HEAD_DIM + 3].set(-2.0 ** 100)
    kw = jnp.concatenate([jnp.broadcast_to(kpad, kw.shape[:2] + kpad.shape), kw], axis=2)
    vwT = jnp.pad(vwT, ((0, 0), (0, 0), (0, 0), (WIN, 0)))
    nsa = _nsa_main(lists, counts, qT, qaug, ks, vsT, kw, vwT, sel, gT, ocmp, lowb, causb, B, T)
    ret = _retention(pret, ret_norm_g[None, :], B, T)
    return _outproj(x2, nsa.reshape(B * T, NSA_WIDTH), ret, w_out.astype(BF16))


def _moe_layer(x2, norm_g, router, router_b, wg, wu, wd, tm=1024):
    tm = min(tm, x2.shape[0])
    h, rank, comb, rankT, cnt = _router(x2, norm_g[None, :], router, router_b, tm)
    counts = cnt[:, 0, :N_EXPERTS].reshape(-1)
    return _moe(counts, h, rankT, rank, comb, wg.astype(BF16), wu.astype(BF16), wd.astype(BF16), x2, tm)


def kernel(x, norm_mix_g, w_in, q_norm_g, k_norm_g, cmp_pos, w_cmp, ret_norm_g, w_out, norm_ffn_g,
           ffn_w_gate, ffn_w_up, ffn_w_down, moe_router, moe_router_b, moe_w_gate, moe_w_up, moe_w_down):
    B, T, D = x.shape
    depth = norm_mix_g.shape[0]
    x2 = x.reshape(B * T, D)
    for l in range(depth):
        x2 = _mixer(x2, B, T, norm_mix_g[l], w_in[l], q_norm_g[l], k_norm_g[l], cmp_pos[l], w_cmp[l],
                    ret_norm_g[l], w_out[l])
        j = l // 2
        if l % 2 == 0:
            x2 = _ffn(x2, norm_ffn_g[l][None, :], ffn_w_gate[j].astype(BF16), ffn_w_up[j].astype(BF16),
                      ffn_w_down[j].astype(BF16))
        else:
            x2 = _moe_layer(x2, norm_ffn_g[l], moe_router[j], moe_router_b[j], moe_w_gate[j], moe_w_up[j],
                            moe_w_down[j])
    return x2.reshape(B, T, D)
```

```python
import functools

import numpy as np
import jax
import jax.numpy as jnp
from jax import lax
from jax.experimental import pallas as pl
from jax.experimental.pallas import tpu as pltpu

F32 = jnp.float32
BF16 = jnp.bfloat16

HEAD_DIM = 64
NSA_HEADS = 8
NSA_KV_HEADS = 2
NSA_GROUP = NSA_HEADS // NSA_KV_HEADS
RET_HEADS = 8
RET_DK = 32
RET_DV = 64
NSA_WIDTH = NSA_HEADS * HEAD_DIM
RET_WIDTH = RET_HEADS * RET_DV
KV_WIDTH = NSA_KV_HEADS * HEAD_DIM
CMP_LEN = 32
CMP_STRIDE = 16
SLC_LEN = 64
SLC_TOPK = 16
WIN = 512
Q_BLOCK = 128
RET_CHUNK = 128
N_EXPERTS = 8
EPS = 1e-6
NEG = -1e30
BIG = 1e9
LANES = 128
GQ = NSA_GROUP * Q_BLOCK
KEY_STEP = 128
STEP_GROUP = 8
N_FORCED = 3
CMP_CHUNK = 128
CMP_TAIL = CMP_CHUNK + 8
WIN_KEYS = WIN + Q_BLOCK
LOG2E = 1.4426950408889634
VMEM_LIMIT = 60 * 1024 * 1024

_C_Q = 0
_C_KV = _C_Q + NSA_WIDTH
_C_KC = _C_KV + 4 * KV_WIDTH
_C_VC = _C_KC + KV_WIDTH
_C_GT = _C_VC + KV_WIDTH
_C_RET = _C_GT + LANES
_RET_COLS = 2 * RET_HEADS * RET_DK + 2 * RET_WIDTH
_C_END = _C_RET + _RET_COLS


def _params(n_axes, vmem=VMEM_LIMIT):
    return pltpu.CompilerParams(dimension_semantics=("arbitrary",) * n_axes, vmem_limit_bytes=vmem)


def _dot(a, b):
    return jnp.dot(a, b, preferred_element_type=F32)


def _dot_nt(a, b):
    return lax.dot_general(a, b, (((1,), (1,)), ((), ())), preferred_element_type=F32)


def _rms(x, g):
    return x * lax.rsqrt(jnp.mean(x * x, axis=-1, keepdims=True) + EPS) * g


def _inproj_kernel(x_ref, g_ref, w_ref, q_ref, kv_ref, kc_ref, vc_ref, gt_ref, ret_ref):
    h = _rms(x_ref[...], g_ref[...]).astype(BF16)
    q_ref[...] = _dot(h, w_ref[:, _C_Q:_C_KV])
    kv_ref[...] = _dot(h, w_ref[:, _C_KV:_C_KC])
    kc_ref[...] = _dot(h, w_ref[:, _C_KC:_C_VC])
    vc_ref[...] = _dot(h, w_ref[:, _C_VC:_C_GT])
    gt_ref[...] = _dot(h, w_ref[:, _C_GT:_C_RET])
    ret_ref[...] = _dot(h, w_ref[:, _C_RET:_C_END])


def _inproj(x2, g, w, tm=512):
    n, d = x2.shape
    widths = (NSA_WIDTH, 4 * KV_WIDTH, KV_WIDTH, KV_WIDTH, LANES, _RET_COLS)
    return pl.pallas_call(
        _inproj_kernel,
        grid=(n // tm,),
        in_specs=[pl.BlockSpec((tm, d), lambda i: (i, 0)),
                  pl.BlockSpec((1, d), lambda i: (0, 0)),
                  pl.BlockSpec((d, _C_END), lambda i: (0, 0))],
        out_specs=[pl.BlockSpec((tm, c), lambda i: (i, 0)) for c in widths],
        out_shape=[jax.ShapeDtypeStruct((n, c), F32) for c in widths],
        compiler_params=_params(1),
        name="inproj",
    )(x2, g, w)


def _prep_kernel(q_ref, kv_ref, gt_ref, qg_ref, kg_ref, qT_ref, ks_ref, kw_ref, vsT_ref, vwT_ref, gT_ref):
    q = q_ref[...]
    qg = qg_ref[...]
    scale = HEAD_DIM ** -0.5 * LOG2E
    heads = []
    for h in range(NSA_HEADS):
        heads.append(_rms(q[:, h * HEAD_DIM:(h + 1) * HEAD_DIM], qg) * scale)
    qt = jnp.concatenate(heads, axis=1).T
    for g in range(NSA_KV_HEADS):
        for r in range(NSA_GROUP):
            h = g * NSA_GROUP + r
            qT_ref[g, :, r * Q_BLOCK:(r + 1) * Q_BLOCK] = qt[h * HEAD_DIM:(h + 1) * HEAD_DIM, :].astype(BF16)
    kv = kv_ref[...]
    ks, kw = kv[:, 0:KV_WIDTH], kv[:, KV_WIDTH:2 * KV_WIDTH]
    vst = kv[:, 2 * KV_WIDTH:3 * KV_WIDTH].T
    vwt = kv[:, 3 * KV_WIDTH:4 * KV_WIDTH].T
    pos = pl.program_id(1) * Q_BLOCK + lax.broadcasted_iota(jnp.int32, (Q_BLOCK, HEAD_DIM), 0)
    col = lax.broadcasted_iota(jnp.int32, (Q_BLOCK, HEAD_DIM), 1)
    kpos = jnp.where(col < 3, pos // SLC_LEN, jnp.where(col < 6, pos % SLC_LEN, 0)).astype(F32)
    for g in range(NSA_KV_HEADS):
        sl = slice(g * HEAD_DIM, (g + 1) * HEAD_DIM)
        ks_ref[g] = jnp.concatenate([_rms(ks[:, sl], kg_ref[0:1, :]), kpos], axis=1).astype(BF16)
        kw_ref[g] = jnp.concatenate([_rms(kw[:, sl], kg_ref[1:2, :]), kpos], axis=1).astype(BF16)
        vsT_ref[g] = vst[sl, :].astype(BF16)
        vwT_ref[g] = vwt[sl, :].astype(BF16)
    gT_ref[...] = jax.nn.sigmoid(gt_ref[...].T[0:32, :])


def _prep(q, kv, gt, qg, kg, B, T):
    nq = T // Q_BLOCK
    G = NSA_KV_HEADS
    row = lambda b, i: (b * nq + i, 0)
    return pl.pallas_call(
        _prep_kernel,
        grid=(B, nq),
        in_specs=[pl.BlockSpec((Q_BLOCK, NSA_WIDTH), row),
                  pl.BlockSpec((Q_BLOCK, 4 * KV_WIDTH), row),
                  pl.BlockSpec((Q_BLOCK, LANES), row),
                  pl.BlockSpec((1, HEAD_DIM), lambda b, i: (0, 0)),
                  pl.BlockSpec((2, HEAD_DIM), lambda b, i: (0, 0))],
        out_specs=[pl.BlockSpec((None, G, None, HEAD_DIM, GQ), lambda b, i: (b, 0, i, 0, 0)),
                   pl.BlockSpec((None, G, Q_BLOCK, 2 * HEAD_DIM), lambda b, i: (b, 0, i, 0)),
                   pl.BlockSpec((None, G, Q_BLOCK, 2 * HEAD_DIM), lambda b, i: (b, 0, i, 0)),
                   pl.BlockSpec((None, G, HEAD_DIM, Q_BLOCK), lambda b, i: (b, 0, 0, i)),
                   pl.BlockSpec((None, G, HEAD_DIM, Q_BLOCK), lambda b, i: (b, 0, 0, i)),
                   pl.BlockSpec((None, None, 32, Q_BLOCK), lambda b, i: (b, i, 0, 0))],
        out_shape=[jax.ShapeDtypeStruct((B, G, nq, HEAD_DIM, GQ), BF16),
                   jax.ShapeDtypeStruct((B, G, T, 2 * HEAD_DIM), BF16),
                   jax.ShapeDtypeStruct((B, G, T, 2 * HEAD_DIM), BF16),
                   jax.ShapeDtypeStruct((B, G, HEAD_DIM, T), BF16),
                   jax.ShapeDtypeStruct((B, G, HEAD_DIM, T), BF16),
                   jax.ShapeDtypeStruct((B, nq, 32, Q_BLOCK), F32)],
        compiler_params=_params(2),
        name="nsa_prep",
    )(q, kv, gt, qg, kg)


def _compress_kernel(kc_ref, vc_ref, wk_ref, wv_ref, pk_ref, pv_ref, kg_ref, kcmp_ref, vcT_ref):
    ncp = kc_ref.shape[0]

    def comp(a, w_ref, p_ref):
        lo = _dot((a + p_ref[0:1, :]).astype(BF16), w_ref[0])
        hi = _dot((a + p_ref[1:2, :]).astype(BF16), w_ref[1])
        return lo + pltpu.roll(hi, ncp - 1, 0)

    k = comp(kc_ref[...], wk_ref, pk_ref)
    v = comp(vc_ref[...], wv_ref, pv_ref).T
    cend = lax.broadcasted_iota(jnp.int32, (ncp, HEAD_DIM), 0) * CMP_STRIDE + (CMP_LEN - 1)
    col = lax.broadcasted_iota(jnp.int32, (ncp, HEAD_DIM), 1)
    kpos = jnp.where(col < 3, cend // SLC_LEN, jnp.where(col < 6, cend % SLC_LEN, 0)).astype(F32)
    for g in range(NSA_KV_HEADS):
        sl = slice(g * HEAD_DIM, (g + 1) * HEAD_DIM)
        kcmp_ref[g] = jnp.concatenate([_rms(k[:, sl], kg_ref[...]), kpos], axis=1).astype(BF16)
        vcT_ref[g] = v[sl, :].astype(BF16)


def _compress(kc, vc, wk, wv, pk, pv, kg, B, T):
    ncp = T // CMP_STRIDE
    G = NSA_KV_HEADS
    hw = CMP_STRIDE * KV_WIDTH
    kc = kc.reshape(B, ncp, hw)
    vc = vc.reshape(B, ncp, hw)
    const3 = lambda b: (0, 0, 0)
    const2 = lambda b: (0, 0)
    return pl.pallas_call(
        _compress_kernel,
        grid=(B,),
        in_specs=[pl.BlockSpec((None, ncp, hw), lambda b: (b, 0, 0)),
                  pl.BlockSpec((None, ncp, hw), lambda b: (b, 0, 0)),
                  pl.BlockSpec((2, hw, KV_WIDTH), const3),
                  pl.BlockSpec((2, hw, KV_WIDTH), const3),
                  pl.BlockSpec((2, hw), const2),
                  pl.BlockSpec((2, hw), const2),
                  pl.BlockSpec((1, HEAD_DIM), const2)],
        out_specs=[pl.BlockSpec((None, G, ncp, 2 * HEAD_DIM), lambda b: (b, 0, 0, 0)),
                   pl.BlockSpec((None, G, HEAD_DIM, ncp), lambda b: (b, 0, 0, 0))],
        out_shape=[jax.ShapeDtypeStruct((B, G, ncp, 2 * HEAD_DIM), BF16),
                   jax.ShapeDtypeStruct((B, G, HEAD_DIM, ncp), BF16)],
        compiler_params=_params(1),
        name="nsa_compress",
    )(kc, vc, wk, wv, pk, pv, kg)


def _compress_weights(w, pos):
    G = NSA_KV_HEADS
    w4 = w.reshape(2, CMP_STRIDE, HEAD_DIM, HEAD_DIM)
    eye = jnp.eye(G, dtype=w.dtype)
    wbd = jnp.einsum('hlde,gk->hlgdke', w4, eye).reshape(2, CMP_STRIDE * KV_WIDTH, KV_WIDTH)
    p = pos.reshape(2, CMP_STRIDE, 1, HEAD_DIM)
    p = jnp.broadcast_to(p, (2, CMP_STRIDE, G, HEAD_DIM)).reshape(2, CMP_STRIDE * KV_WIDTH)
    return wbd.astype(BF16), p


def _split3(x):
    hi = x.astype(BF16)
    r = x - hi.astype(F32)
    mid = r.astype(BF16)
    lo = (r - mid.astype(F32)).astype(BF16)
    return hi, mid, lo


def _nsa_cmp_kernel(qT_ref, qaug_ref, kc_ref, vcT_ref, ovT_ref, ocmp_ref, sel_ref, flag_ref, *, n_sel):
    ncp = kc_ref.shape[0]
    ns = ovT_ref.shape[0]
    i = pl.program_id(2)
    q0 = i * Q_BLOCK
    q = jnp.concatenate([qT_ref[...], qaug_ref[...]], axis=0)
    lane = lax.broadcasted_iota(jnp.int32, (1, GQ), 1)
    t_row = q0 + (lane & (Q_BLOCK - 1))
    has_cmp = (t_row >= CMP_LEN - 1).astype(F32)
    tq = q0 + lax.broadcasted_iota(jnp.int32, (1, Q_BLOCK), 1)
    cur = tq // SLC_LEN

    def prefix(rows):
        nsk = rows * CMP_STRIDE // SLC_LEN
        tail0 = max(rows - CMP_TAIL, 0)
        s = _dot(kc_ref[0:rows, :], q)
        cend = (lax.broadcasted_iota(jnp.int32, (rows - tail0, 1), 0) + tail0) * CMP_STRIDE + (CMP_LEN - 1)
        tail = jnp.where(t_row >= cend, s[tail0:], NEG)
        s = jnp.concatenate([s[0:tail0], tail], axis=0) if tail0 else tail
        m = jnp.max(s, axis=0, keepdims=True)
        e = jnp.exp2(s - m)
        p = e * (has_cmp / jnp.sum(e, axis=0, keepdims=True))
        ocmp_ref[...] = _dot(vcT_ref[:, 0:rows], p.astype(BF16))

        ps = p[:, 0:Q_BLOCK]
        for r in range(1, NSA_GROUP):
            ps = ps + p[:, r * Q_BLOCK:(r + 1) * Q_BLOCK]
        ov = ovT_ref[0:nsk, 0:rows]
        hi, mid, lo = _split3(ps)
        imp = _dot(ov, hi) + _dot(ov, mid) + _dot(ov, lo)

        blk = lax.broadcasted_iota(jnp.int32, (nsk, 1), 0)
        forced = (blk == 0) | (blk == cur) | (blk == cur - 1)
        valid = blk * SLC_LEN <= tq
        imp = jnp.where(forced, -3e38, jnp.where(valid, imp, -BIG))
        blk_f = blk.astype(F32)
        sel = forced.astype(F32)
        for _ in range(n_sel - N_FORCED):
            mx = jnp.max(imp, axis=0, keepdims=True)
            idx = jnp.min(jnp.where(imp == mx, blk_f, float(ns)), axis=0, keepdims=True)
            pick = blk_f == idx
            sel = jnp.where(pick, 1.0, sel)
            imp = jnp.where(pick, -3e38, imp)
        sel_ref[0:nsk, :] = sel
        cnt = _dot_nt(jnp.ones((8, Q_BLOCK), BF16), sel.astype(BF16))
        flag_ref[:, 0:nsk] = (cnt > 0).astype(jnp.int32)
        if nsk < ns:
            sel_ref[nsk:, :] = jnp.zeros((ns - nsk, Q_BLOCK), F32)
            flag_ref[:, nsk:] = jnp.zeros((8, ns - nsk), jnp.int32)

    n_variants = ncp // CMP_CHUNK
    variant = (i * (Q_BLOCK // CMP_STRIDE) + (Q_BLOCK // CMP_STRIDE - 2)) // CMP_CHUNK
    for k in range(n_variants):
        pl.when(variant == k)(functools.partial(prefix, (k + 1) * CMP_CHUNK))


def _nsa_cmp(qT, qaug, kcmp, vcT, ovT, B, T):
    G = NSA_KV_HEADS
    nq = T // Q_BLOCK
    ncp = T // CMP_STRIDE
    ns = T // SLC_LEN
    n_sel = min(SLC_TOPK, ns)
    assert ncp % CMP_CHUNK == 0 and n_sel > N_FORCED
    return pl.pallas_call(
        functools.partial(_nsa_cmp_kernel, n_sel=n_sel),
        grid=(B, G, nq),
        in_specs=[pl.BlockSpec((None, None, None, HEAD_DIM, GQ), lambda b, g, i: (b, g, i, 0, 0)),
                  pl.BlockSpec((None, HEAD_DIM, GQ), lambda b, g, i: (g, 0, 0)),
                  pl.BlockSpec((None, None, ncp, 2 * HEAD_DIM), lambda b, g, i: (b, g, 0, 0)),
                  pl.BlockSpec((None, None, HEAD_DIM, ncp), lambda b, g, i: (b, g, 0, 0)),
                  pl.BlockSpec((ns, ncp), lambda b, g, i: (0, 0))],
        out_specs=[pl.BlockSpec((None, None, None, HEAD_DIM, GQ), lambda b, g, i: (b, g, i, 0, 0)),
                   pl.BlockSpec((None, None, None, ns, Q_BLOCK), lambda b, g, i: (b, g, i, 0, 0)),
                   pl.BlockSpec((None, None, None, 8, ns), lambda b, g, i: (b, g, i, 0, 0))],
        out_shape=[jax.ShapeDtypeStruct((B, G, nq, HEAD_DIM, GQ), F32),
                   jax.ShapeDtypeStruct((B, G, nq, ns, Q_BLOCK), F32),
                   jax.ShapeDtypeStruct((B, G, nq, 8, ns), jnp.int32)],
        compiler_params=_params(3),
        name="nsa_cmp",
    )(qT, qaug, kcmp, vcT, ovT)


def _nsa_main_kernel(flag_ref, qT_ref, qaug_ref, ks_ref, vsT_ref, kw_ref, vwT_ref, sel_ref, gT_ref, ocmp_ref,
                     lowb_ref, causb_ref, out_ref, m_sc, l_sc, acc_sc, list_sc):
    b, g, i = pl.program_id(0), pl.program_id(1), pl.program_id(2)
    tile_id = (b * pl.num_programs(1) + g) * pl.num_programs(2) + i
    ns = sel_ref.shape[0]
    q = jnp.concatenate([qT_ref[...], qaug_ref[...]], axis=0)
    k0 = pl.multiple_of(i * Q_BLOCK, Q_BLOCK)

    def scan(j, n):
        list_sc[n] = j
        f = flag_ref[tile_id * ns + 2 * j] | flag_ref[tile_id * ns + 2 * j + 1]
        return n + f

    n_needed = lax.fori_loop(0, i, scan, 0)
    for x in range(STEP_GROUP):
        list_sc[n_needed + x] = -1

    def sel_bias(j, valid):
        def row(r):
            picked = (sel_ref[pl.ds(r, 1), :] > 0.5) & valid
            return jnp.concatenate([jnp.where(picked, 0.0, NEG)] * NSA_GROUP, axis=1)
        return row(2 * j), row(2 * j + 1)

    def add_sel_bias(s, j, valid):
        ba, bb = sel_bias(j, valid)
        return jnp.concatenate([s[0:SLC_LEN] + ba, s[SLC_LEN:] + bb], axis=0)

    s = _dot(kw_ref[pl.ds(k0, WIN_KEYS), :], q)
    s = jnp.concatenate([s[0:Q_BLOCK] + lowb_ref[...], s[Q_BLOCK:WIN], s[WIN:] + causb_ref[...]], axis=0)
    m = jnp.max(s, axis=0, keepdims=True)
    p = jnp.exp2(s - m)
    o_win = _dot(vwT_ref[:, pl.ds(k0, WIN_KEYS)], p.astype(BF16)) / jnp.sum(p, axis=0, keepdims=True)

    s = _dot(ks_ref[pl.ds(k0, KEY_STEP), :], q)
    s = add_sel_bias(s, i, True) + causb_ref[...]
    m = jnp.max(s, axis=0, keepdims=True)
    p = jnp.exp2(s - m)
    m_sc[...] = m
    l_sc[...] = jnp.sum(p, axis=0, keepdims=True)
    acc_sc[...] = _dot(vsT_ref[:, pl.ds(k0, KEY_STEP)], p.astype(BF16))

    def group(t, c):
        ks, vs, biases = [], [], []
        for x in range(STEP_GROUP):
            j = list_sc[t * STEP_GROUP + x]
            valid = j >= 0
            j = jnp.maximum(j, 0)
            kj = pl.multiple_of(j * KEY_STEP, KEY_STEP)
            ks.append(ks_ref[pl.ds(kj, KEY_STEP), :])
            vs.append(vsT_ref[:, pl.ds(kj, KEY_STEP)])
            biases.append((j, valid))
        s = _dot(jnp.concatenate(ks, axis=0), q)
        s = jnp.concatenate([add_sel_bias(s[x * KEY_STEP:(x + 1) * KEY_STEP], *biases[x])
                             for x in range(STEP_GROUP)], axis=0)
        m_old = m_sc[...]
        m_new = jnp.maximum(m_old, jnp.max(s, axis=0, keepdims=True))
        p = jnp.exp2(s - m_new)
        alpha = jnp.exp2(m_old - m_new)
        l_sc[...] = alpha * l_sc[...] + jnp.sum(p, axis=0, keepdims=True)
        acc_sc[...] = alpha * acc_sc[...] + _dot(jnp.concatenate(vs, axis=1), p.astype(BF16))
        m_sc[...] = m_new
        return c

    lax.fori_loop(0, (n_needed + (STEP_GROUP - 1)) // STEP_GROUP, group, 0)
    o_slc = acc_sc[...] / l_sc[...]

    def gate(k):
        rows = [gT_ref[pl.ds(g * (NSA_GROUP * 3) + r * 3 + k, 1), :] for r in range(NSA_GROUP)]
        return jnp.concatenate(rows, axis=1)

    o = gate(0) * ocmp_ref[...] + gate(1) * o_slc + gate(2) * o_win
    o = jnp.concatenate([o, jnp.zeros_like(o)], axis=0)
    for r in range(NSA_GROUP):
        out_ref[:, r * HEAD_DIM:(r + 1) * HEAD_DIM] = o[:, r * Q_BLOCK:(r + 1) * Q_BLOCK].T[:, 0:HEAD_DIM]


def _nsa_main(flags, qT, qaug, ks, vsT, kw, vwT, sel, gT, ocmp, lowb, causb, B, T):
    G = NSA_KV_HEADS
    nq = T // Q_BLOCK
    ns = T // SLC_LEN
    whole = lambda b, g, i, *_: (b, g, 0, 0)
    tile = lambda b, g, i, *_: (b, g, i, 0, 0)
    const = lambda b, g, i, *_: (0, 0)
    grid_spec = pltpu.PrefetchScalarGridSpec(
        num_scalar_prefetch=1,
        grid=(B, G, nq),
        in_specs=[pl.BlockSpec((None, None, None, HEAD_DIM, GQ), tile),
                  pl.BlockSpec((None, HEAD_DIM, GQ), lambda b, g, i, *_: (g, 0, 0)),
                  pl.BlockSpec((None, None, T, 2 * HEAD_DIM), whole),
                  pl.BlockSpec((None, None, HEAD_DIM, T), whole),
                  pl.BlockSpec((None, None, T + WIN, 2 * HEAD_DIM), whole),
                  pl.BlockSpec((None, None, HEAD_DIM, T + WIN), whole),
                  pl.BlockSpec((None, None, None, ns, Q_BLOCK), tile),
                  pl.BlockSpec((None, None, 32, Q_BLOCK), lambda b, g, i, *_: (b, i, 0, 0)),
                  pl.BlockSpec((None, None, None, HEAD_DIM, GQ), tile),
                  pl.BlockSpec((Q_BLOCK, GQ), const),
                  pl.BlockSpec((Q_BLOCK, GQ), const)],
        out_specs=pl.BlockSpec((None, Q_BLOCK, NSA_GROUP * HEAD_DIM), lambda b, g, i, *_: (b, i, g)),
        scratch_shapes=[pltpu.VMEM((1, GQ), F32), pltpu.VMEM((1, GQ), F32), pltpu.VMEM((HEAD_DIM, GQ), F32),
                        pltpu.SMEM((nq + STEP_GROUP,), jnp.int32)],
    )
    return pl.pallas_call(
        _nsa_main_kernel,
        grid_spec=grid_spec,
        out_shape=jax.ShapeDtypeStruct((B, T, NSA_WIDTH), F32),
        compiler_params=_params(3),
        name="nsa_main",
    )(flags, qT, qaug, ks, vsT, kw, vwT, sel, gT, ocmp, lowb, causb)


def _ret_kernel(p_ref, decay_ref, xi_ref, zeta_ref, gch_ref, ng_ref, out_ref, state_ref):
    @pl.when(pl.program_id(1) == 0)
    def _():
        state_ref[...] = jnp.zeros(state_ref.shape, F32)

    kw = RET_HEADS * RET_DK
    p = p_ref[...]
    rq = p[:, 0:kw] * (RET_DK ** -0.5)
    rk = p[:, kw:2 * kw]
    rkT = rk.T
    rv = p[:, 2 * kw:2 * kw + RET_WIDTH]
    rg = p[:, 2 * kw + RET_WIDTH:2 * kw + 2 * RET_WIDTH]
    xi = xi_ref[...]
    outs = []
    for h in range(RET_HEADS):
        qh = rq[:, h * RET_DK:(h + 1) * RET_DK]
        kh = rk[:, h * RET_DK:(h + 1) * RET_DK]
        khT = rkT[h * RET_DK:(h + 1) * RET_DK, :]
        vh = rv[:, h * RET_DV:(h + 1) * RET_DV]
        st = state_ref[h]
        inner = _dot_nt(qh, kh) * decay_ref[h]
        o = _dot(inner, vh) + _dot(qh, st) * xi[:, h:h + 1]
        state_ref[h] = st * gch_ref[h:h + 1, 0:1] + _dot(khT * zeta_ref[h:h + 1, :], vh)
        outs.append(_rms(o, ng_ref[:, h * RET_DV:(h + 1) * RET_DV]))
    out_ref[...] = jnp.concatenate(outs, axis=1) * (rg * jax.nn.sigmoid(rg))


def _ret_consts():
    H, C = RET_HEADS, RET_CHUNK
    log_g = jnp.log1p(-jnp.exp2(-5.0 - jnp.arange(H, dtype=F32)))
    idx = jnp.arange(C, dtype=F32)
    diff = idx[:, None] - idx[None, :]
    decay = jnp.where(diff >= 0, jnp.exp(jnp.maximum(diff, 0.0) * log_g[:, None, None]), 0.0)
    zeta = jnp.exp((C - 1 - idx) * log_g[:, None])
    xi = jnp.exp((idx + 1) * log_g[:, None]).T
    g_chunk = jnp.broadcast_to(jnp.exp(C * log_g)[:, None], (H, LANES))
    return decay, xi, zeta, g_chunk


def _retention(pret, ng, B, T):
    nch = T // RET_CHUNK
    decay, xi, zeta, gch = _ret_consts()
    c2 = lambda b, c: (0, 0)
    return pl.pallas_call(
        _ret_kernel,
        grid=(B, nch),
        in_specs=[pl.BlockSpec((RET_CHUNK, _RET_COLS), lambda b, c: (b * nch + c, 0)),
                  pl.BlockSpec((RET_HEADS, RET_CHUNK, RET_CHUNK), lambda b, c: (0, 0, 0)),
                  pl.BlockSpec((RET_CHUNK, RET_HEADS), c2),
                  pl.BlockSpec((RET_HEADS, RET_CHUNK), c2),
                  pl.BlockSpec((RET_HEADS, LANES), c2),
                  pl.BlockSpec((1, RET_WIDTH), c2)],
        out_specs=pl.BlockSpec((RET_CHUNK, RET_WIDTH), lambda b, c: (b * nch + c, 0)),
        out_shape=jax.ShapeDtypeStruct((B * T, RET_WIDTH), F32),
        scratch_shapes=[pltpu.VMEM((RET_HEADS, RET_DK, RET_DV), F32)],
        compiler_params=_params(2),
        name="retention",
    )(pret, decay, xi, zeta, gch, ng)


def _outproj_kernel(x_ref, nsa_ref, ret_ref, w_ref, o_ref):
    o_ref[...] = (x_ref[...] + _dot(nsa_ref[...].astype(BF16), w_ref[0:NSA_WIDTH, :])
                  + _dot(ret_ref[...].astype(BF16), w_ref[NSA_WIDTH:, :]))


def _outproj(x2, nsa, ret, w, tm=512):
    n, d = x2.shape
    return pl.pallas_call(
        _outproj_kernel,
        grid=(n // tm,),
        in_specs=[pl.BlockSpec((tm, d), lambda i: (i, 0)),
                  pl.BlockSpec((tm, NSA_WIDTH), lambda i: (i, 0)),
                  pl.BlockSpec((tm, RET_WIDTH), lambda i: (i, 0)),
                  pl.BlockSpec((NSA_WIDTH + RET_WIDTH, d), lambda i: (0, 0))],
        out_specs=pl.BlockSpec((tm, d), lambda i: (i, 0)),
        out_shape=jax.ShapeDtypeStruct((n, d), F32),
        compiler_params=_params(1),
        name="outproj",
    )(x2, nsa, ret, w)


def _ffn_kernel(x_ref, g_ref, wg_ref, wu_ref, wd_ref, o_ref, h_sc):
    f = pl.program_id(1)

    @pl.when(f == 0)
    def _():
        x = x_ref[...]
        h_sc[...] = _rms(x, g_ref[...]).astype(BF16)
        o_ref[...] = x

    h = h_sc[...]
    a = _dot(h, wg_ref[...])
    act = (a * jax.nn.sigmoid(a) * _dot(h, wu_ref[...])).astype(BF16)
    o_ref[...] += _dot(act, wd_ref[...])


def _ffn(x2, g, wg, wu, wd, tm=512, fc=1408):
    n, d = x2.shape
    dff = wg.shape[1]
    return pl.pallas_call(
        _ffn_kernel,
        grid=(n // tm, dff // fc),
        in_specs=[pl.BlockSpec((tm, d), lambda i, f: (i, 0)),
                  pl.BlockSpec((1, d), lambda i, f: (0, 0)),
                  pl.BlockSpec((d, fc), lambda i, f: (0, f)),
                  pl.BlockSpec((d, fc), lambda i, f: (0, f)),
                  pl.BlockSpec((fc, d), lambda i, f: (f, 0))],
        out_specs=pl.BlockSpec((tm, d), lambda i, f: (i, 0)),
        out_shape=jax.ShapeDtypeStruct((n, d), F32),
        scratch_shapes=[pltpu.VMEM((tm, d), BF16)],
        compiler_params=_params(2),
        name="ffn_dense",
    )(x2, g, wg, wu, wd)


def _router_kernel(x_ref, g_ref, r_ref, rb_ref, tri_ref, h_ref, rank_ref, comb_ref, rankT_ref, cnt_ref):
    h = _rms(x_ref[...], g_ref[...])
    h_ref[...] = h.astype(BF16)
    hh, hm, hl = _split3(h)
    rh, rm, rl = _split3(r_ref[...])
    logits = (_dot(hh, rh) + (_dot(hh, rm) + _dot(hm, rh)) + (_dot(hh, rl) + _dot(hm, rm) + _dot(hl, rh))
              + rb_ref[...])
    lane = lax.broadcasted_iota(jnp.int32, logits.shape, 1).astype(F32)
    logits = jnp.where(lane < N_EXPERTS, logits, NEG)
    m1 = jnp.max(logits, axis=1, keepdims=True)
    i1 = jnp.min(jnp.where(logits == m1, lane, float(LANES)), axis=1, keepdims=True)
    l2 = jnp.where(lane == i1, NEG, logits)
    m2 = jnp.max(l2, axis=1, keepdims=True)
    i2 = jnp.min(jnp.where(l2 == m2, lane, float(LANES)), axis=1, keepdims=True)
    e2 = jnp.exp(m2 - m1)
    w1 = 1.0 / (1.0 + e2)
    w2 = e2 / (1.0 + e2)
    use1, use2 = lane == i1, lane == i2
    comb_ref[...] = jnp.where(use1, w1, 0.0) + jnp.where(use2, w2, 0.0)
    use = (use1 | use2).astype(F32)
    rank = jnp.where(use > 0, _dot(tri_ref[...], use.astype(BF16)), -1.0)
    rank_ref[...] = rank
    rankT_ref[...] = rank.T[0:N_EXPERTS, :]
    cnt_ref[...] = jnp.broadcast_to(jnp.sum(use, axis=0, keepdims=True), cnt_ref.shape).astype(jnp.int32)


def _router(x2, g, router, rb, tm):
    n, d = x2.shape
    nt = n // tm
    tri = (jnp.arange(tm)[:, None] > jnp.arange(tm)[None, :]).astype(BF16)
    rpad = jnp.zeros((d, LANES), F32).at[:, :N_EXPERTS].set(router)
    rbpad = jnp.zeros((1, LANES), F32).at[0, :N_EXPERTS].set(rb)
    c2 = lambda i: (0, 0)
    return pl.pallas_call(
        _router_kernel,
        grid=(nt,),
        in_specs=[pl.BlockSpec((tm, d), lambda i: (i, 0)),
                  pl.BlockSpec((1, d), c2),
                  pl.BlockSpec((d, LANES), c2),
                  pl.BlockSpec((1, LANES), c2),
                  pl.BlockSpec((tm, tm), c2)],
        out_specs=[pl.BlockSpec((tm, d), lambda i: (i, 0)),
                   pl.BlockSpec((tm, LANES), lambda i: (i, 0)),
                   pl.BlockSpec((tm, LANES), lambda i: (i, 0)),
                   pl.BlockSpec((N_EXPERTS, tm), lambda i: (0, i)),
                   pl.BlockSpec((None, 8, LANES), lambda i: (i, 0, 0))],
        out_shape=[jax.ShapeDtypeStruct((n, d), BF16),
                   jax.ShapeDtypeStruct((n, LANES), F32),
                   jax.ShapeDtypeStruct((n, LANES), F32),
                   jax.ShapeDtypeStruct((N_EXPERTS, n), F32),
                   jax.ShapeDtypeStruct((nt, 8, LANES), jnp.int32)],
        compiler_params=_params(1),
        name="moe_router",
    )(x2, g, rpad, rbpad, tri)


MOE_SUB = 128


def _moe_kernel(cnt_ref, h_ref, rankT_ref, rank_ref, comb_ref, wg_ref, wu_ref, wd_ref, x_ref, o_ref, hc_sc, oacc_sc):
    t, e, f = pl.program_id(0), pl.program_id(1), pl.program_id(2)
    nf = pl.num_programs(2)
    tm = h_ref.shape[0]
    nsub = (cnt_ref[t * N_EXPERTS + e] + (MOE_SUB - 1)) // MOE_SUB

    @pl.when((e == 0) & (f == 0))
    def _():
        o_ref[...] = x_ref[...]

    @pl.when(f == 0)
    def _():
        rank_row = rankT_ref[...]

        def gather(s, c):
            r0 = pl.multiple_of(s * MOE_SUB, MOE_SUB)
            rows = (lax.broadcasted_iota(jnp.int32, (MOE_SUB, 1), 0) + r0).astype(F32)
            onehot = (rows == rank_row).astype(BF16)
            hc_sc[pl.ds(r0, MOE_SUB), :] = _dot(onehot, h_ref[...]).astype(BF16)
            oacc_sc[pl.ds(r0, MOE_SUB), :] = jnp.zeros((MOE_SUB, oacc_sc.shape[1]), F32)
            return c

        lax.fori_loop(0, nsub, gather, 0)

    def expert(s, c):
        r0 = pl.multiple_of(s * MOE_SUB, MOE_SUB)
        rows = hc_sc[pl.ds(r0, MOE_SUB), :]
        a = _dot(rows, wg_ref[...])
        act = (a * jax.nn.sigmoid(a) * _dot(rows, wu_ref[...])).astype(BF16)
        oacc_sc[pl.ds(r0, MOE_SUB), :] += _dot(act, wd_ref[...])
        return c

    lax.fori_loop(0, nsub, expert, 0)

    @pl.when(f == nf - 1)
    def _():
        is_e = lax.broadcasted_iota(jnp.int32, (1, LANES), 1) == e
        rank_col = jnp.sum(jnp.where(is_e, rank_ref[...], 0.0), axis=1, keepdims=True)
        comb_col = jnp.sum(jnp.where(is_e, comb_ref[...], 0.0), axis=1, keepdims=True)

        def scatter(s, c):
            r0 = pl.multiple_of(s * MOE_SUB, MOE_SUB)
            cols = (lax.broadcasted_iota(jnp.int32, (1, MOE_SUB), 1) + r0).astype(F32)
            onehot = (rank_col == cols).astype(BF16)
            y = _dot(onehot, oacc_sc[pl.ds(r0, MOE_SUB), :].astype(BF16))
            o_ref[...] += comb_col * y
            return c

        lax.fori_loop(0, nsub, scatter, 0)


def _moe(counts, h, rankT, rank, comb, wg, wu, wd, x2, tm, fc=1408):
    n, d = x2.shape
    dff = wg.shape[2]
    grid_spec = pltpu.PrefetchScalarGridSpec(
        num_scalar_prefetch=1,
        grid=(n // tm, N_EXPERTS, dff // fc),
        in_specs=[pl.BlockSpec((tm, d), lambda t, e, f, c: (t, 0)),
                  pl.BlockSpec((None, 1, tm), lambda t, e, f, c: (e, 0, t)),
                  pl.BlockSpec((tm, LANES), lambda t, e, f, c: (t, 0)),
                  pl.BlockSpec((tm, LANES), lambda t, e, f, c: (t, 0)),
                  pl.BlockSpec((None, d, fc), lambda t, e, f, c: (e, 0, f)),
                  pl.BlockSpec((None, d, fc), lambda t, e, f, c: (e, 0, f)),
                  pl.BlockSpec((None, fc, d), lambda t, e, f, c: (e, f, 0)),
                  pl.BlockSpec((tm, d), lambda t, e, f, c: (t, 0))],
        out_specs=pl.BlockSpec((tm, d), lambda t, e, f, c: (t, 0)),
        scratch_shapes=[pltpu.VMEM((tm, d), BF16), pltpu.VMEM((tm, d), F32)],
    )
    return pl.pallas_call(
        _moe_kernel,
        grid_spec=grid_spec,
        out_shape=jax.ShapeDtypeStruct((n, d), F32),
        compiler_params=_params(3),
        name="moe_experts",
    )(counts, h, rankT.reshape(N_EXPERTS, 1, n), rank, comb, wg, wu, wd, x2)


def _permute_w_in(w):
    o = np.cumsum((0, NSA_WIDTH) + (KV_WIDTH,) * 6 + (3 * NSA_HEADS,))
    q, kc, vc, ks, vs, kw, vw, gts = (w[:, o[k]:o[k + 1]] for k in range(8))
    ret = w[:, o[8]:]
    pad = jnp.zeros((w.shape[0], LANES - 3 * NSA_HEADS), w.dtype)
    return jnp.concatenate([q, ks, kw, vs, vw, kc, vc, gts, pad, ret], axis=1).astype(BF16)


def _nsa_consts(T):
    ncp = T // CMP_STRIDE
    ns = T // SLC_LEN
    cs = np.arange(ncp) * CMP_STRIDE
    ss = np.arange(ns) * SLC_LEN
    ov = np.clip(np.minimum(cs[None, :] + CMP_LEN, ss[:, None] + SLC_LEN) - np.maximum(cs[None, :], ss[:, None]), 0, None)
    ovT = (ov.astype(np.float32) / CMP_LEN)
    ovT[:, ncp - 1] = 0.0
    h = np.arange(NSA_HEADS).reshape(NSA_KV_HEADS, NSA_GROUP) + 1
    slopes = np.exp2(-8.0 * h / NSA_HEADS).astype(np.float32)
    slopes = np.repeat(slopes, Q_BLOCK, axis=1)
    parts, rest = [], np.float64(LOG2E)
    for _ in range(3):
        part = np.float64(np.asarray(rest).astype(BF16))
        parts.append(part)
        rest = rest - part
    qaug = np.zeros((NSA_KV_HEADS, HEAD_DIM, GQ), np.float32)
    for k, part in enumerate(parts):
        qaug[:, k, :] = part * SLC_LEN * slopes
        qaug[:, 3 + k, :] = part * slopes
    kq = np.arange(Q_BLOCK)[:, None] - np.tile(np.arange(Q_BLOCK), NSA_GROUP)[None, :]
    causb = np.where(kq <= 0, 0.0, NEG).astype(np.float32)
    lowb = np.where(kq > 0, 0.0, NEG).astype(np.float32)
    return jnp.asarray(ovT, BF16), jnp.asarray(qaug, BF16), jnp.asarray(lowb), jnp.asarray(causb)


def _mixer(x2, B, T, norm_g, w_in, q_norm_g, k_norm_g, cmp_pos, w_cmp, ret_norm_g, w_out):
    ns = T // SLC_LEN
    q, kv, kc, vc, gt, pret = _inproj(x2, norm_g[None, :], _permute_w_in(w_in))
    qT, ks, kw, vsT, vwT, gT = _prep(q, kv, gt, q_norm_g[None, :], k_norm_g[1:3], B, T)
    wk, pk = _compress_weights(w_cmp[0], cmp_pos[0])
    wv, pv = _compress_weights(w_cmp[1], cmp_pos[1])
    kcmp, vcT = _compress(kc, vc, wk, wv, pk, pv, k_norm_g[0:1], B, T)
    ovT, qaug, lowb, causb = _nsa_consts(T)
    ocmp, sel, flags = _nsa_cmp(qT, qaug, kcmp, vcT, ovT, B, T)
    flags = flags[:, :, :, 0, :].reshape(-1)
    kpad = jnp.zeros((WIN, 2 * HEAD_DIM), BF16).at[:, HEAD_DIM:HEAD_DIM + 3].set(-2.0 ** 100)
    kw = jnp.concatenate([jnp.broadcast_to(kpad, kw.shape[:2] + kpad.shape), kw], axis=2)
    vwT = jnp.pad(vwT, ((0, 0), (0, 0), (0, 0), (WIN, 0)))
    nsa = _nsa_main(flags, qT, qaug, ks, vsT, kw, vwT, sel, gT, ocmp, lowb, causb, B, T)
    ret = _retention(pret, ret_norm_g[None, :], B, T)
    return _outproj(x2, nsa.reshape(B * T, NSA_WIDTH), ret, w_out.astype(BF16))


def _moe_layer(x2, norm_g, router, router_b, wg, wu, wd, tm=1024):
    tm = min(tm, x2.shape[0])
    h, rank, comb, rankT, cnt = _router(x2, norm_g[None, :], router, router_b, tm)
    counts = cnt[:, 0, :N_EXPERTS].reshape(-1)
    return _moe(counts, h, rankT, rank, comb, wg.astype(BF16), wu.astype(BF16), wd.astype(BF16), x2, tm)


def kernel(x, norm_mix_g, w_in, q_norm_g, k_norm_g, cmp_pos, w_cmp, ret_norm_g, w_out, norm_ffn_g,
           ffn_w_gate, ffn_w_up, ffn_w_down, moe_router, moe_router_b, moe_w_gate, moe_w_up, moe_w_down):
    B, T, D = x.shape
    depth = norm_mix_g.shape[0]
    x2 = x.reshape(B * T, D)
    for l in range(depth):
        x2 = _mixer(x2, B, T, norm_mix_g[l], w_in[l], q_norm_g[l], k_norm_g[l], cmp_pos[l], w_cmp[l],
                    ret_norm_g[l], w_out[l])
        j = l // 2
        if l % 2 == 0:
            x2 = _ffn(x2, norm_ffn_g[l][None, :], ffn_w_gate[j].astype(BF16), ffn_w_up[j].astype(BF16),
                      ffn_w_down[j].astype(BF16))
        else:
            x2 = _moe_layer(x2, norm_ffn_g[l], moe_router[j], moe_router_b[j], moe_w_gate[j], moe_w_up[j],
                            moe_w_down[j])
    return x2.reshape(B, T, D)
```

```python
import functools

import numpy as np
import jax
import jax.numpy as jnp
from jax import lax
from jax.experimental import pallas as pl
from jax.experimental.pallas import tpu as pltpu

F32 = jnp.float32
BF16 = jnp.bfloat16

HEAD_DIM = 64
NSA_HEADS = 8
NSA_KV_HEADS = 2
NSA_GROUP = NSA_HEADS // NSA_KV_HEADS
RET_HEADS = 8
RET_DK = 32
RET_DV = 64
NSA_WIDTH = NSA_HEADS * HEAD_DIM
RET_WIDTH = RET_HEADS * RET_DV
KV_WIDTH = NSA_KV_HEADS * HEAD_DIM
CMP_LEN = 32
CMP_STRIDE = 16
SLC_LEN = 64
SLC_TOPK = 16
WIN = 512
Q_BLOCK = 128
RET_CHUNK = 128
N_EXPERTS = 8
EPS = 1e-6
NEG = -1e30
BIG = 1e9
LANES = 128
GQ = NSA_GROUP * Q_BLOCK
KEY_STEP = 128
STEP_GROUP = 4
LANE_SPLIT = 1
N_FORCED = 3
CMP_CHUNK = 128
CMP_TAIL = CMP_CHUNK + 8
WIN_KEYS = WIN + Q_BLOCK
LOG2E = 1.4426950408889634
VMEM_LIMIT = 60 * 1024 * 1024

_C_Q = 0
_C_KV = _C_Q + NSA_WIDTH
_C_KC = _C_KV + 4 * KV_WIDTH
_C_VC = _C_KC + KV_WIDTH
_C_GT = _C_VC + KV_WIDTH
_C_RET = _C_GT + LANES
_RET_COLS = 2 * RET_HEADS * RET_DK + 2 * RET_WIDTH
_C_END = _C_RET + _RET_COLS


def _params(n_axes, vmem=VMEM_LIMIT):
    return pltpu.CompilerParams(dimension_semantics=("arbitrary",) * n_axes, vmem_limit_bytes=vmem)


def _dot(a, b):
    return jnp.dot(a, b, preferred_element_type=F32)


def _dot_nt(a, b):
    return lax.dot_general(a, b, (((1,), (1,)), ((), ())), preferred_element_type=F32)


def _rms(x, g):
    return x * lax.rsqrt(jnp.mean(x * x, axis=-1, keepdims=True) + EPS) * g


def _inproj_kernel(x_ref, g_ref, w_ref, q_ref, kv_ref, kc_ref, vc_ref, gt_ref, ret_ref):
    h = _rms(x_ref[...], g_ref[...]).astype(BF16)
    q_ref[...] = _dot(h, w_ref[:, _C_Q:_C_KV])
    kv_ref[...] = _dot(h, w_ref[:, _C_KV:_C_KC])
    kc_ref[...] = _dot(h, w_ref[:, _C_KC:_C_VC])
    vc_ref[...] = _dot(h, w_ref[:, _C_VC:_C_GT])
    gt_ref[...] = _dot(h, w_ref[:, _C_GT:_C_RET])
    ret_ref[...] = _dot(h, w_ref[:, _C_RET:_C_END])


def _inproj(x2, g, w, tm=512):
    n, d = x2.shape
    widths = (NSA_WIDTH, 4 * KV_WIDTH, KV_WIDTH, KV_WIDTH, LANES, _RET_COLS)
    return pl.pallas_call(
        _inproj_kernel,
        grid=(n // tm,),
        in_specs=[pl.BlockSpec((tm, d), lambda i: (i, 0)),
                  pl.BlockSpec((1, d), lambda i: (0, 0)),
                  pl.BlockSpec((d, _C_END), lambda i: (0, 0))],
        out_specs=[pl.BlockSpec((tm, c), lambda i: (i, 0)) for c in widths],
        out_shape=[jax.ShapeDtypeStruct((n, c), F32) for c in widths],
        compiler_params=_params(1),
        name="inproj",
    )(x2, g, w)


def _group_rms(x, g, ones_ref):
    w = x.shape[1]
    ones = ones_ref[0:w, 0:w]
    hi, mid, lo = _split3(x * x)
    ms = (_dot(hi, ones) + _dot(mid, ones) + _dot(lo, ones)) * (1.0 / HEAD_DIM)
    return x * lax.rsqrt(ms + EPS) * g


def _group_ones():
    lane = np.arange(NSA_WIDTH) // HEAD_DIM
    return jnp.asarray(lane[:, None] == lane[None, :], BF16)


def _prep_kernel(q_ref, kv_ref, gt_ref, qg_ref, kg_ref, ones_ref, qT_ref, ks_ref, kw_ref, vsT_ref, vwT_ref, gT_ref):
    scale = HEAD_DIM ** -0.5 * LOG2E
    qt = (_group_rms(q_ref[...], qg_ref[...], ones_ref) * scale).T
    for g in range(NSA_KV_HEADS):
        for r in range(NSA_GROUP):
            h = g * NSA_GROUP + r
            qT_ref[g, :, r * Q_BLOCK:(r + 1) * Q_BLOCK] = qt[h * HEAD_DIM:(h + 1) * HEAD_DIM, :].astype(BF16)
    kv = kv_ref[...]
    ks, kw = kv[:, 0:KV_WIDTH], kv[:, KV_WIDTH:2 * KV_WIDTH]
    vst = kv[:, 2 * KV_WIDTH:3 * KV_WIDTH].T
    vwt = kv[:, 3 * KV_WIDTH:4 * KV_WIDTH].T
    pos = pl.program_id(1) * Q_BLOCK + lax.broadcasted_iota(jnp.int32, (Q_BLOCK, HEAD_DIM), 0)
    col = lax.broadcasted_iota(jnp.int32, (Q_BLOCK, HEAD_DIM), 1)
    kpos = jnp.where(col < 3, pos // SLC_LEN, jnp.where(col < 6, pos % SLC_LEN, 0)).astype(F32)
    ks = _group_rms(ks, kg_ref[0:1, :], ones_ref)
    kw = _group_rms(kw, kg_ref[1:2, :], ones_ref)
    for g in range(NSA_KV_HEADS):
        sl = slice(g * HEAD_DIM, (g + 1) * HEAD_DIM)
        ks_ref[g] = jnp.concatenate([ks[:, sl], kpos], axis=1).astype(BF16)
        kw_ref[g] = jnp.concatenate([kw[:, sl], kpos], axis=1).astype(BF16)
        vsT_ref[g] = vst[sl, :].astype(BF16)
        vwT_ref[g] = vwt[sl, :].astype(BF16)
    gT_ref[...] = jax.nn.sigmoid(gt_ref[...].T[0:32, :])


def _prep(q, kv, gt, qg, kg, B, T):
    nq = T // Q_BLOCK
    G = NSA_KV_HEADS
    row = lambda b, i: (b * nq + i, 0)
    return pl.pallas_call(
        _prep_kernel,
        grid=(B, nq),
        in_specs=[pl.BlockSpec((Q_BLOCK, NSA_WIDTH), row),
                  pl.BlockSpec((Q_BLOCK, 4 * KV_WIDTH), row),
                  pl.BlockSpec((Q_BLOCK, LANES), row),
                  pl.BlockSpec((1, NSA_WIDTH), lambda b, i: (0, 0)),
                  pl.BlockSpec((2, KV_WIDTH), lambda b, i: (0, 0)),
                  pl.BlockSpec((NSA_WIDTH, NSA_WIDTH), lambda b, i: (0, 0))],
        out_specs=[pl.BlockSpec((None, G, None, HEAD_DIM, GQ), lambda b, i: (b, 0, i, 0, 0)),
                   pl.BlockSpec((None, G, Q_BLOCK, 2 * HEAD_DIM), lambda b, i: (b, 0, i, 0)),
                   pl.BlockSpec((None, G, Q_BLOCK, 2 * HEAD_DIM), lambda b, i: (b, 0, i, 0)),
                   pl.BlockSpec((None, G, HEAD_DIM, Q_BLOCK), lambda b, i: (b, 0, 0, i)),
                   pl.BlockSpec((None, G, HEAD_DIM, Q_BLOCK), lambda b, i: (b, 0, 0, i)),
                   pl.BlockSpec((None, None, 32, Q_BLOCK), lambda b, i: (b, i, 0, 0))],
        out_shape=[jax.ShapeDtypeStruct((B, G, nq, HEAD_DIM, GQ), BF16),
                   jax.ShapeDtypeStruct((B, G, T, 2 * HEAD_DIM), BF16),
                   jax.ShapeDtypeStruct((B, G, T, 2 * HEAD_DIM), BF16),
                   jax.ShapeDtypeStruct((B, G, HEAD_DIM, T), BF16),
                   jax.ShapeDtypeStruct((B, G, HEAD_DIM, T), BF16),
                   jax.ShapeDtypeStruct((B, nq, 32, Q_BLOCK), F32)],
        compiler_params=_params(2),
        name="nsa_prep",
    )(q, kv, gt, jnp.tile(qg, (1, NSA_HEADS)), jnp.tile(kg, (1, NSA_KV_HEADS)), _group_ones())


def _compress_kernel(kc_ref, vc_ref, wk_ref, wv_ref, pk_ref, pv_ref, kg_ref, kcmp_ref, vcT_ref):
    ncp = kc_ref.shape[0]

    def comp(a, w_ref, p_ref):
        lo = _dot((a + p_ref[0:1, :]).astype(BF16), w_ref[0])
        hi = _dot((a + p_ref[1:2, :]).astype(BF16), w_ref[1])
        return lo + pltpu.roll(hi, ncp - 1, 0)

    k = comp(kc_ref[...], wk_ref, pk_ref)
    v = comp(vc_ref[...], wv_ref, pv_ref).T
    cend = lax.broadcasted_iota(jnp.int32, (ncp, HEAD_DIM), 0) * CMP_STRIDE + (CMP_LEN - 1)
    col = lax.broadcasted_iota(jnp.int32, (ncp, HEAD_DIM), 1)
    kpos = jnp.where(col < 3, cend // SLC_LEN, jnp.where(col < 6, cend % SLC_LEN, 0)).astype(F32)
    for g in range(NSA_KV_HEADS):
        sl = slice(g * HEAD_DIM, (g + 1) * HEAD_DIM)
        kcmp_ref[g] = jnp.concatenate([_rms(k[:, sl], kg_ref[...]), kpos], axis=1).astype(BF16)
        vcT_ref[g] = v[sl, :].astype(BF16)


def _compress(kc, vc, wk, wv, pk, pv, kg, B, T):
    ncp = T // CMP_STRIDE
    G = NSA_KV_HEADS
    hw = CMP_STRIDE * KV_WIDTH
    kc = kc.reshape(B, ncp, hw)
    vc = vc.reshape(B, ncp, hw)
    const3 = lambda b: (0, 0, 0)
    const2 = lambda b: (0, 0)
    return pl.pallas_call(
        _compress_kernel,
        grid=(B,),
        in_specs=[pl.BlockSpec((None, ncp, hw), lambda b: (b, 0, 0)),
                  pl.BlockSpec((None, ncp, hw), lambda b: (b, 0, 0)),
                  pl.BlockSpec((2, hw, KV_WIDTH), const3),
                  pl.BlockSpec((2, hw, KV_WIDTH), const3),
                  pl.BlockSpec((2, hw), const2),
                  pl.BlockSpec((2, hw), const2),
                  pl.BlockSpec((1, HEAD_DIM), const2)],
        out_specs=[pl.BlockSpec((None, G, ncp, 2 * HEAD_DIM), lambda b: (b, 0, 0, 0)),
                   pl.BlockSpec((None, G, HEAD_DIM, ncp), lambda b: (b, 0, 0, 0))],
        out_shape=[jax.ShapeDtypeStruct((B, G, ncp, 2 * HEAD_DIM), BF16),
                   jax.ShapeDtypeStruct((B, G, HEAD_DIM, ncp), BF16)],
        compiler_params=_params(1),
        name="nsa_compress",
    )(kc, vc, wk, wv, pk, pv, kg)


def _compress_weights(w, pos):
    G = NSA_KV_HEADS
    w4 = w.reshape(2, CMP_STRIDE, HEAD_DIM, HEAD_DIM)
    eye = jnp.eye(G, dtype=w.dtype)
    wbd = jnp.einsum('hlde,gk->hlgdke', w4, eye).reshape(2, CMP_STRIDE * KV_WIDTH, KV_WIDTH)
    p = pos.reshape(2, CMP_STRIDE, 1, HEAD_DIM)
    p = jnp.broadcast_to(p, (2, CMP_STRIDE, G, HEAD_DIM)).reshape(2, CMP_STRIDE * KV_WIDTH)
    return wbd.astype(BF16), p


def _split3(x):
    hi = x.astype(BF16)
    r = x - hi.astype(F32)
    mid = r.astype(BF16)
    lo = (r - mid.astype(F32)).astype(BF16)
    return hi, mid, lo


def _nsa_cmp_kernel(qT_ref, qaug_ref, kc_ref, vcT_ref, ovT_ref, ocmp_ref, sel_ref, flag_ref, *, n_sel):
    ncp = kc_ref.shape[0]
    ns = ovT_ref.shape[0]
    i = pl.program_id(2)
    q0 = i * Q_BLOCK
    q = jnp.concatenate([qT_ref[...], qaug_ref[...]], axis=0)
    lane = lax.broadcasted_iota(jnp.int32, (1, GQ), 1)
    t_row = q0 + (lane & (Q_BLOCK - 1))
    has_cmp = (t_row >= CMP_LEN - 1).astype(F32)
    tq = q0 + lax.broadcasted_iota(jnp.int32, (1, Q_BLOCK), 1)
    cur = tq // SLC_LEN

    def prefix(rows):
        nsk = rows * CMP_STRIDE // SLC_LEN
        tail0 = max(rows - CMP_TAIL, 0)
        s = _dot(kc_ref[0:rows, :], q)
        cend = (lax.broadcasted_iota(jnp.int32, (rows - tail0, 1), 0) + tail0) * CMP_STRIDE + (CMP_LEN - 1)
        tail = jnp.where(t_row >= cend, s[tail0:], NEG)
        s = jnp.concatenate([s[0:tail0], tail], axis=0) if tail0 else tail
        m = jnp.max(s, axis=0, keepdims=True)
        e = jnp.exp2(s - m)
        p = e * (has_cmp / jnp.sum(e, axis=0, keepdims=True))
        ocmp_ref[...] = _dot(vcT_ref[:, 0:rows], p.astype(BF16))

        ps = p[:, 0:Q_BLOCK]
        for r in range(1, NSA_GROUP):
            ps = ps + p[:, r * Q_BLOCK:(r + 1) * Q_BLOCK]
        ov = ovT_ref[0:nsk, 0:rows]
        hi, mid, lo = _split3(ps)
        imp = _dot(ov, hi) + _dot(ov, mid) + _dot(ov, lo)

        blk = lax.broadcasted_iota(jnp.int32, (nsk, 1), 0)
        forced = (blk == 0) | (blk == cur) | (blk == cur - 1)
        valid = blk * SLC_LEN <= tq
        imp = jnp.where(forced, -3e38, jnp.where(valid, imp, -BIG))
        blk_f = blk.astype(F32)
        sel = forced.astype(F32)
        for _ in range(n_sel - N_FORCED):
            mx = jnp.max(imp, axis=0, keepdims=True)
            idx = jnp.min(jnp.where(imp == mx, blk_f, float(ns)), axis=0, keepdims=True)
            pick = blk_f == idx
            sel = jnp.where(pick, 1.0, sel)
            imp = jnp.where(pick, -3e38, imp)
        sel_ref[0:nsk, :] = sel
        cnt = _dot_nt(jnp.ones((8, Q_BLOCK), BF16), sel.astype(BF16))
        flag_ref[:, 0:nsk] = (cnt > 0).astype(jnp.int32)
        if nsk < ns:
            sel_ref[nsk:, :] = jnp.zeros((ns - nsk, Q_BLOCK), F32)
            flag_ref[:, nsk:] = jnp.zeros((8, ns - nsk), jnp.int32)

    n_variants = ncp // CMP_CHUNK
    variant = (i * (Q_BLOCK // CMP_STRIDE) + (Q_BLOCK // CMP_STRIDE - 2)) // CMP_CHUNK
    for k in range(n_variants):
        pl.when(variant == k)(functools.partial(prefix, (k + 1) * CMP_CHUNK))


def _nsa_cmp(qT, qaug, kcmp, vcT, ovT, B, T):
    G = NSA_KV_HEADS
    nq = T // Q_BLOCK
    ncp = T // CMP_STRIDE
    ns = T // SLC_LEN
    n_sel = min(SLC_TOPK, ns)
    assert ncp % CMP_CHUNK == 0 and n_sel > N_FORCED
    return pl.pallas_call(
        functools.partial(_nsa_cmp_kernel, n_sel=n_sel),
        grid=(B, G, nq),
        in_specs=[pl.BlockSpec((None, None, None, HEAD_DIM, GQ), lambda b, g, i: (b, g, i, 0, 0)),
                  pl.BlockSpec((None, HEAD_DIM, GQ), lambda b, g, i: (g, 0, 0)),
                  pl.BlockSpec((None, None, ncp, 2 * HEAD_DIM), lambda b, g, i: (b, g, 0, 0)),
                  pl.BlockSpec((None, None, HEAD_DIM, ncp), lambda b, g, i: (b, g, 0, 0)),
                  pl.BlockSpec((ns, ncp), lambda b, g, i: (0, 0))],
        out_specs=[pl.BlockSpec((None, None, None, HEAD_DIM, GQ), lambda b, g, i: (b, g, i, 0, 0)),
                   pl.BlockSpec((None, None, None, ns, Q_BLOCK), lambda b, g, i: (b, g, i, 0, 0)),
                   pl.BlockSpec((None, None, None, 8, ns), lambda b, g, i: (b, g, i, 0, 0))],
        out_shape=[jax.ShapeDtypeStruct((B, G, nq, HEAD_DIM, GQ), F32),
                   jax.ShapeDtypeStruct((B, G, nq, ns, Q_BLOCK), F32),
                   jax.ShapeDtypeStruct((B, G, nq, 8, ns), jnp.int32)],
        compiler_params=_params(3),
        name="nsa_cmp",
    )(qT, qaug, kcmp, vcT, ovT)


def _nsa_main_kernel(list_ref, cnt_ref, qT_ref, qaug_ref, ks_ref, vsT_ref, kw_ref, vwT_ref, sel_ref, gT_ref, ocmp_ref,
                     lowb_ref, causb_ref, out_ref, m_sc, l_sc, acc_sc):
    b, g, i = pl.program_id(0), pl.program_id(1), pl.program_id(2)
    tile_id = (b * pl.num_programs(1) + g) * pl.num_programs(2) + i
    n_steps = sel_ref.shape[0] // 2
    q = jnp.concatenate([qT_ref[...], qaug_ref[...]], axis=0)
    k0 = pl.multiple_of(i * Q_BLOCK, Q_BLOCK)
    part_w = GQ // LANE_SPLIT
    parts = [slice(c * part_w, (c + 1) * part_w) for c in range(LANE_SPLIT)]
    reps = part_w // Q_BLOCK

    def sel_bias(j, valid):
        def row(r):
            picked = (sel_ref[pl.ds(r, 1), :] > 0.5) & valid
            return jnp.concatenate([jnp.where(picked, 0.0, NEG)] * reps, axis=1)
        return row(2 * j), row(2 * j + 1)

    def add_sel_bias(s, ba, bb):
        return jnp.concatenate([s[0:SLC_LEN] + ba, s[SLC_LEN:] + bb], axis=0)

    lowb = jnp.concatenate([lowb_ref[...]] * reps, axis=1)
    causb = jnp.concatenate([causb_ref[...]] * reps, axis=1)

    kd = ks_ref[pl.ds(k0, KEY_STEP), :]
    vd = vsT_ref[:, pl.ds(k0, KEY_STEP)]
    kwin = kw_ref[pl.ds(k0, WIN_KEYS), :]
    vwin = vwT_ref[:, pl.ds(k0, WIN_KEYS)]
    ba, bb = sel_bias(i, True)
    o_win = []
    for c in parts:
        s = add_sel_bias(_dot(kd, q[:, c]), ba, bb) + causb
        m = jnp.max(s, axis=0, keepdims=True)
        p = jnp.exp2(s - m)
        m_sc[:, c] = m
        l_sc[:, c] = jnp.sum(p, axis=0, keepdims=True)
        acc_sc[:, c] = _dot(vd, p.astype(BF16))
        s = _dot(kwin, q[:, c])
        s = jnp.concatenate([s[0:Q_BLOCK] + lowb, s[Q_BLOCK:WIN], s[WIN:] + causb], axis=0)
        m = jnp.max(s, axis=0, keepdims=True)
        p = jnp.exp2(s - m)
        o_win.append(_dot(vwin, p.astype(BF16)) / jnp.sum(p, axis=0, keepdims=True))
    o_win = jnp.concatenate(o_win, axis=1)

    def group(t, carry):
        ks, vs, biases = [], [], []
        for x in range(STEP_GROUP):
            j = list_ref[tile_id * n_steps + t * STEP_GROUP + x]
            valid = j >= 0
            j = jnp.maximum(j, 0)
            kj = pl.multiple_of(j * KEY_STEP, KEY_STEP)
            ks.append(ks_ref[pl.ds(kj, KEY_STEP), :])
            vs.append(vsT_ref[:, pl.ds(kj, KEY_STEP)])
            biases.append(sel_bias(j, valid))
        kcat = jnp.concatenate(ks, axis=0)
        vcat = jnp.concatenate(vs, axis=1)
        for c in parts:
            s = _dot(kcat, q[:, c])
            s = jnp.concatenate([add_sel_bias(s[x * KEY_STEP:(x + 1) * KEY_STEP], *biases[x])
                                 for x in range(STEP_GROUP)], axis=0)
            m_old = m_sc[:, c]
            m_new = jnp.maximum(m_old, jnp.max(s, axis=0, keepdims=True))
            p = jnp.exp2(s - m_new)
            alpha = jnp.exp2(m_old - m_new)
            l_sc[:, c] = alpha * l_sc[:, c] + jnp.sum(p, axis=0, keepdims=True)
            acc_sc[:, c] = alpha * acc_sc[:, c] + _dot(vcat, p.astype(BF16))
            m_sc[:, c] = m_new
        return carry

    lax.fori_loop(0, (cnt_ref[tile_id] + (STEP_GROUP - 1)) // STEP_GROUP, group, 0)
    o_slc = acc_sc[...] / l_sc[...]

    def gate(k):
        rows = [gT_ref[pl.ds(g * (NSA_GROUP * 3) + r * 3 + k, 1), :] for r in range(NSA_GROUP)]
        return jnp.concatenate(rows, axis=1)

    o = gate(0) * ocmp_ref[...] + gate(1) * o_slc + gate(2) * o_win
    o = jnp.concatenate([o, jnp.zeros_like(o)], axis=0)
    for r in range(NSA_GROUP):
        out_ref[:, r * HEAD_DIM:(r + 1) * HEAD_DIM] = o[:, r * Q_BLOCK:(r + 1) * Q_BLOCK].T[:, 0:HEAD_DIM]


def _nsa_steps_kernel(flagT_ref, pairT_ref, list_ref, cnt_ref, *, nq):
    n_steps, nt = list_ref.shape
    need = _dot(pairT_ref[...], flagT_ref[...].astype(BF16)) > 0
    step = lax.broadcasted_iota(jnp.int32, (n_steps, 1), 0)
    own = lax.broadcasted_iota(jnp.int32, (1, nt), 1) % nq
    need = need & (step < own)
    need_f = need.astype(F32)
    earlier = (lax.broadcasted_iota(jnp.int32, (n_steps, n_steps), 1) < step).astype(BF16)
    slot = _dot(earlier, need_f.astype(BF16))
    total = jnp.sum(need_f, axis=0, keepdims=True)
    cnt_ref[...] = jnp.broadcast_to(total, cnt_ref.shape).astype(jnp.int32)
    step_f = step.astype(F32)
    for p in range(n_steps):
        val = jnp.sum(jnp.where(need & (slot == p), step_f, 0.0), axis=0, keepdims=True)
        list_ref[p:p + 1, :] = jnp.where(total > p, val, -1.0).astype(jnp.int32)


def _nsa_steps(flags, nq):
    nt, ns = flags.shape
    n_steps = ns // 2
    pairT = jnp.asarray(np.arange(n_steps)[:, None] == np.arange(ns)[None, :] // 2, BF16)
    lists, counts = pl.pallas_call(
        functools.partial(_nsa_steps_kernel, nq=nq),
        out_shape=[jax.ShapeDtypeStruct((n_steps, nt), jnp.int32), jax.ShapeDtypeStruct((8, nt), jnp.int32)],
        name="nsa_steps",
    )(flags.T.astype(F32), pairT)
    return lists.T.reshape(-1), counts[0]


def _nsa_main(lists, counts, qT, qaug, ks, vsT, kw, vwT, sel, gT, ocmp, lowb, causb, B, T):
    G = NSA_KV_HEADS
    nq = T // Q_BLOCK
    ns = T // SLC_LEN
    whole = lambda b, g, i, *_: (b, g, 0, 0)
    tile = lambda b, g, i, *_: (b, g, i, 0, 0)
    const = lambda b, g, i, *_: (0, 0)
    grid_spec = pltpu.PrefetchScalarGridSpec(
        num_scalar_prefetch=2,
        grid=(B, G, nq),
        in_specs=[pl.BlockSpec((None, None, None, HEAD_DIM, GQ), tile),
                  pl.BlockSpec((None, HEAD_DIM, GQ), lambda b, g, i, *_: (g, 0, 0)),
                  pl.BlockSpec((None, None, T, 2 * HEAD_DIM), whole),
                  pl.BlockSpec((None, None, HEAD_DIM, T), whole),
                  pl.BlockSpec((None, None, T + WIN, 2 * HEAD_DIM), whole),
                  pl.BlockSpec((None, None, HEAD_DIM, T + WIN), whole),
                  pl.BlockSpec((None, None, None, ns, Q_BLOCK), tile),
                  pl.BlockSpec((None, None, 32, Q_BLOCK), lambda b, g, i, *_: (b, i, 0, 0)),
                  pl.BlockSpec((None, None, None, HEAD_DIM, GQ), tile),
                  pl.BlockSpec((Q_BLOCK, Q_BLOCK), const),
                  pl.BlockSpec((Q_BLOCK, Q_BLOCK), const)],
        out_specs=pl.BlockSpec((None, Q_BLOCK, NSA_GROUP * HEAD_DIM), lambda b, g, i, *_: (b, i, g)),
        scratch_shapes=[pltpu.VMEM((1, GQ), F32), pltpu.VMEM((1, GQ), F32), pltpu.VMEM((HEAD_DIM, GQ), F32)],
    )
    return pl.pallas_call(
        _nsa_main_kernel,
        grid_spec=grid_spec,
        out_shape=jax.ShapeDtypeStruct((B, T, NSA_WIDTH), F32),
        compiler_params=_params(3),
        name="nsa_main",
    )(lists, counts, qT, qaug, ks, vsT, kw, vwT, sel, gT, ocmp, lowb, causb)


def _ret_kernel(p_ref, decay_ref, xi_ref, zeta_ref, gch_ref, ng_ref, out_ref, state_ref):
    @pl.when(pl.program_id(1) == 0)
    def _():
        state_ref[...] = jnp.zeros(state_ref.shape, F32)

    kw = RET_HEADS * RET_DK
    p = p_ref[...]
    rq = p[:, 0:kw] * (RET_DK ** -0.5)
    rk = p[:, kw:2 * kw]
    rkT = rk.T
    rv = p[:, 2 * kw:2 * kw + RET_WIDTH]
    rg = p[:, 2 * kw + RET_WIDTH:2 * kw + 2 * RET_WIDTH]
    xi = xi_ref[...]
    outs = []
    for h in range(RET_HEADS):
        qh = rq[:, h * RET_DK:(h + 1) * RET_DK]
        kh = rk[:, h * RET_DK:(h + 1) * RET_DK]
        khT = rkT[h * RET_DK:(h + 1) * RET_DK, :]
        vh = rv[:, h * RET_DV:(h + 1) * RET_DV]
        st = state_ref[h]
        inner = _dot_nt(qh, kh) * decay_ref[h]
        o = _dot(inner, vh) + _dot(qh, st) * xi[:, h:h + 1]
        state_ref[h] = st * gch_ref[h:h + 1, 0:1] + _dot(khT * zeta_ref[h:h + 1, :], vh)
        outs.append(_rms(o, ng_ref[:, h * RET_DV:(h + 1) * RET_DV]))
    out_ref[...] = jnp.concatenate(outs, axis=1) * (rg * jax.nn.sigmoid(rg))


def _ret_consts():
    H, C = RET_HEADS, RET_CHUNK
    log_g = jnp.log1p(-jnp.exp2(-5.0 - jnp.arange(H, dtype=F32)))
    idx = jnp.arange(C, dtype=F32)
    diff = idx[:, None] - idx[None, :]
    decay = jnp.where(diff >= 0, jnp.exp(jnp.maximum(diff, 0.0) * log_g[:, None, None]), 0.0)
    zeta = jnp.exp((C - 1 - idx) * log_g[:, None])
    xi = jnp.exp((idx + 1) * log_g[:, None]).T
    g_chunk = jnp.broadcast_to(jnp.exp(C * log_g)[:, None], (H, LANES))
    return decay, xi, zeta, g_chunk


def _retention(pret, ng, B, T):
    nch = T // RET_CHUNK
    decay, xi, zeta, gch = _ret_consts()
    c2 = lambda b, c: (0, 0)
    return pl.pallas_call(
        _ret_kernel,
        grid=(B, nch),
        in_specs=[pl.BlockSpec((RET_CHUNK, _RET_COLS), lambda b, c: (b * nch + c, 0)),
                  pl.BlockSpec((RET_HEADS, RET_CHUNK, RET_CHUNK), lambda b, c: (0, 0, 0)),
                  pl.BlockSpec((RET_CHUNK, RET_HEADS), c2),
                  pl.BlockSpec((RET_HEADS, RET_CHUNK), c2),
                  pl.BlockSpec((RET_HEADS, LANES), c2),
                  pl.BlockSpec((1, RET_WIDTH), c2)],
        out_specs=pl.BlockSpec((RET_CHUNK, RET_WIDTH), lambda b, c: (b * nch + c, 0)),
        out_shape=jax.ShapeDtypeStruct((B * T, RET_WIDTH), F32),
        scratch_shapes=[pltpu.VMEM((RET_HEADS, RET_DK, RET_DV), F32)],
        compiler_params=_params(2),
        name="retention",
    )(pret, decay, xi, zeta, gch, ng)


def _outproj_kernel(x_ref, nsa_ref, ret_ref, w_ref, o_ref):
    o_ref[...] = (x_ref[...] + _dot(nsa_ref[...].astype(BF16), w_ref[0:NSA_WIDTH, :])
                  + _dot(ret_ref[...].astype(BF16), w_ref[NSA_WIDTH:, :]))


def _outproj(x2, nsa, ret, w, tm=512):
    n, d = x2.shape
    return pl.pallas_call(
        _outproj_kernel,
        grid=(n // tm,),
        in_specs=[pl.BlockSpec((tm, d), lambda i: (i, 0)),
                  pl.BlockSpec((tm, NSA_WIDTH), lambda i: (i, 0)),
                  pl.BlockSpec((tm, RET_WIDTH), lambda i: (i, 0)),
                  pl.BlockSpec((NSA_WIDTH + RET_WIDTH, d), lambda i: (0, 0))],
        out_specs=pl.BlockSpec((tm, d), lambda i: (i, 0)),
        out_shape=jax.ShapeDtypeStruct((n, d), F32),
        compiler_params=_params(1),
        name="outproj",
    )(x2, nsa, ret, w)


def _ffn_kernel(x_ref, g_ref, wg_ref, wu_ref, wd_ref, o_ref, h_sc):
    f = pl.program_id(1)

    @pl.when(f == 0)
    def _():
        x = x_ref[...]
        h_sc[...] = _rms(x, g_ref[...]).astype(BF16)
        o_ref[...] = x

    h = h_sc[...]
    a = _dot(h, wg_ref[...])
    act = (a * jax.nn.sigmoid(a) * _dot(h, wu_ref[...])).astype(BF16)
    o_ref[...] += _dot(act, wd_ref[...])


def _ffn(x2, g, wg, wu, wd, tm=512, fc=1408):
    n, d = x2.shape
    dff = wg.shape[1]
    return pl.pallas_call(
        _ffn_kernel,
        grid=(n // tm, dff // fc),
        in_specs=[pl.BlockSpec((tm, d), lambda i, f: (i, 0)),
                  pl.BlockSpec((1, d), lambda i, f: (0, 0)),
                  pl.BlockSpec((d, fc), lambda i, f: (0, f)),
                  pl.BlockSpec((d, fc), lambda i, f: (0, f)),
                  pl.BlockSpec((fc, d), lambda i, f: (f, 0))],
        out_specs=pl.BlockSpec((tm, d), lambda i, f: (i, 0)),
        out_shape=jax.ShapeDtypeStruct((n, d), F32),
        scratch_shapes=[pltpu.VMEM((tm, d), BF16)],
        compiler_params=_params(2),
        name="ffn_dense",
    )(x2, g, wg, wu, wd)


def _router_kernel(x_ref, g_ref, r_ref, rb_ref, tri_ref, h_ref, rank_ref, comb_ref, rankT_ref, cnt_ref):
    h = _rms(x_ref[...], g_ref[...])
    h_ref[...] = h.astype(BF16)
    hh, hm, hl = _split3(h)
    rh, rm, rl = _split3(r_ref[...])
    logits = (_dot(hh, rh) + (_dot(hh, rm) + _dot(hm, rh)) + (_dot(hh, rl) + _dot(hm, rm) + _dot(hl, rh))
              + rb_ref[...])
    lane = lax.broadcasted_iota(jnp.int32, logits.shape, 1).astype(F32)
    logits = jnp.where(lane < N_EXPERTS, logits, NEG)
    m1 = jnp.max(logits, axis=1, keepdims=True)
    i1 = jnp.min(jnp.where(logits == m1, lane, float(LANES)), axis=1, keepdims=True)
    l2 = jnp.where(lane == i1, NEG, logits)
    m2 = jnp.max(l2, axis=1, keepdims=True)
    i2 = jnp.min(jnp.where(l2 == m2, lane, float(LANES)), axis=1, keepdims=True)
    e2 = jnp.exp(m2 - m1)
    w1 = 1.0 / (1.0 + e2)
    w2 = e2 / (1.0 + e2)
    use1, use2 = lane == i1, lane == i2
    comb_ref[...] = jnp.where(use1, w1, 0.0) + jnp.where(use2, w2, 0.0)
    use = (use1 | use2).astype(F32)
    rank = jnp.where(use > 0, _dot(tri_ref[...], use.astype(BF16)), -1.0)
    rank_ref[...] = rank
    rankT_ref[...] = rank.T[0:N_EXPERTS, :]
    cnt_ref[...] = jnp.broadcast_to(jnp.sum(use, axis=0, keepdims=True), cnt_ref.shape).astype(jnp.int32)


def _router(x2, g, router, rb, tm):
    n, d = x2.shape
    nt = n // tm
    tri = (jnp.arange(tm)[:, None] > jnp.arange(tm)[None, :]).astype(BF16)
    rpad = jnp.zeros((d, LANES), F32).at[:, :N_EXPERTS].set(router)
    rbpad = jnp.zeros((1, LANES), F32).at[0, :N_EXPERTS].set(rb)
    c2 = lambda i: (0, 0)
    return pl.pallas_call(
        _router_kernel,
        grid=(nt,),
        in_specs=[pl.BlockSpec((tm, d), lambda i: (i, 0)),
                  pl.BlockSpec((1, d), c2),
                  pl.BlockSpec((d, LANES), c2),
                  pl.BlockSpec((1, LANES), c2),
                  pl.BlockSpec((tm, tm), c2)],
        out_specs=[pl.BlockSpec((tm, d), lambda i: (i, 0)),
                   pl.BlockSpec((tm, LANES), lambda i: (i, 0)),
                   pl.BlockSpec((tm, LANES), lambda i: (i, 0)),
                   pl.BlockSpec((N_EXPERTS, tm), lambda i: (0, i)),
                   pl.BlockSpec((None, 8, LANES), lambda i: (i, 0, 0))],
        out_shape=[jax.ShapeDtypeStruct((n, d), BF16),
                   jax.ShapeDtypeStruct((n, LANES), F32),
                   jax.ShapeDtypeStruct((n, LANES), F32),
                   jax.ShapeDtypeStruct((N_EXPERTS, n), F32),
                   jax.ShapeDtypeStruct((nt, 8, LANES), jnp.int32)],
        compiler_params=_params(1),
        name="moe_router",
    )(x2, g, rpad, rbpad, tri)


MOE_SUB = 128
MOE_MOVE = 2 * MOE_SUB


def _moe_kernel(cnt_ref, h_ref, rankT_ref, rank_ref, comb_ref, wg_ref, wu_ref, wd_ref, x_ref, o_ref, hc_sc, oacc_sc):
    t, e, f = pl.program_id(0), pl.program_id(1), pl.program_id(2)
    nf = pl.num_programs(2)
    tm = h_ref.shape[0]
    nsub = (cnt_ref[t * N_EXPERTS + e] + (MOE_SUB - 1)) // MOE_SUB
    nmove = (nsub + 1) // 2

    @pl.when((e == 0) & (f == 0))
    def _():
        o_ref[...] = x_ref[...]

    @pl.when(f == 0)
    def _():
        rank_row = rankT_ref[...]

        def gather(s, c):
            r0 = pl.multiple_of(s * MOE_MOVE, MOE_MOVE)
            rows = (lax.broadcasted_iota(jnp.int32, (MOE_MOVE, 1), 0) + r0).astype(F32)
            onehot = (rows == rank_row).astype(BF16)
            hc_sc[pl.ds(r0, MOE_MOVE), :] = _dot(onehot, h_ref[...]).astype(BF16)
            oacc_sc[pl.ds(r0, MOE_MOVE), :] = jnp.zeros((MOE_MOVE, oacc_sc.shape[1]), F32)
            return c

        lax.fori_loop(0, nmove, gather, 0)

    def expert(s, c):
        r0 = pl.multiple_of(s * MOE_SUB, MOE_SUB)
        rows = hc_sc[pl.ds(r0, MOE_SUB), :]
        a = _dot(rows, wg_ref[...])
        act = (a * jax.nn.sigmoid(a) * _dot(rows, wu_ref[...])).astype(BF16)
        oacc_sc[pl.ds(r0, MOE_SUB), :] += _dot(act, wd_ref[...])
        return c

    lax.fori_loop(0, nsub, expert, 0)

    @pl.when(f == nf - 1)
    def _():
        is_e = lax.broadcasted_iota(jnp.int32, (1, LANES), 1) == e
        rank_col = jnp.sum(jnp.where(is_e, rank_ref[...], 0.0), axis=1, keepdims=True)
        comb_col = jnp.sum(jnp.where(is_e, comb_ref[...], 0.0), axis=1, keepdims=True)

        def scatter(s, c):
            r0 = pl.multiple_of(s * MOE_MOVE, MOE_MOVE)
            cols = (lax.broadcasted_iota(jnp.int32, (1, MOE_MOVE), 1) + r0).astype(F32)
            onehot = (rank_col == cols).astype(BF16)
            y = _dot(onehot, oacc_sc[pl.ds(r0, MOE_MOVE), :].astype(BF16))
            o_ref[...] += comb_col * y
            return c

        lax.fori_loop(0, nmove, scatter, 0)


def _moe(counts, h, rankT, rank, comb, wg, wu, wd, x2, tm, fc=1408):
    n, d = x2.shape
    dff = wg.shape[2]
    grid_spec = pltpu.PrefetchScalarGridSpec(
        num_scalar_prefetch=1,
        grid=(n // tm, N_EXPERTS, dff // fc),
        in_specs=[pl.BlockSpec((tm, d), lambda t, e, f, c: (t, 0)),
                  pl.BlockSpec((None, 1, tm), lambda t, e, f, c: (e, 0, t)),
                  pl.BlockSpec((tm, LANES), lambda t, e, f, c: (t, 0)),
                  pl.BlockSpec((tm, LANES), lambda t, e, f, c: (t, 0)),
                  pl.BlockSpec((None, d, fc), lambda t, e, f, c: (e, 0, f)),
                  pl.BlockSpec((None, d, fc), lambda t, e, f, c: (e, 0, f)),
                  pl.BlockSpec((None, fc, d), lambda t, e, f, c: (e, f, 0)),
                  pl.BlockSpec((tm, d), lambda t, e, f, c: (t, 0))],
        out_specs=pl.BlockSpec((tm, d), lambda t, e, f, c: (t, 0)),
        scratch_shapes=[pltpu.VMEM((tm, d), BF16), pltpu.VMEM((tm, d), F32)],
    )
    return pl.pallas_call(
        _moe_kernel,
        grid_spec=grid_spec,
        out_shape=jax.ShapeDtypeStruct((n, d), F32),
        compiler_params=_params(3),
        name="moe_experts",
    )(counts, h, rankT.reshape(N_EXPERTS, 1, n), rank, comb, wg, wu, wd, x2)


def _permute_w_in(w):
    o = np.cumsum((0, NSA_WIDTH) + (KV_WIDTH,) * 6 + (3 * NSA_HEADS,))
    q, kc, vc, ks, vs, kw, vw, gts = (w[:, o[k]:o[k + 1]] for k in range(8))
    ret = w[:, o[8]:]
    pad = jnp.zeros((w.shape[0], LANES - 3 * NSA_HEADS), w.dtype)
    return jnp.concatenate([q, ks, kw, vs, vw, kc, vc, gts, pad, ret], axis=1).astype(BF16)


def _nsa_consts(T):
    ncp = T // CMP_STRIDE
    ns = T // SLC_LEN
    cs = np.arange(ncp) * CMP_STRIDE
    ss = np.arange(ns) * SLC_LEN
    ov = np.clip(np.minimum(cs[None, :] + CMP_LEN, ss[:, None] + SLC_LEN) - np.maximum(cs[None, :], ss[:, None]), 0, None)
    ovT = (ov.astype(np.float32) / CMP_LEN)
    ovT[:, ncp - 1] = 0.0
    h = np.arange(NSA_HEADS).reshape(NSA_KV_HEADS, NSA_GROUP) + 1
    slopes = np.exp2(-8.0 * h / NSA_HEADS).astype(np.float32)
    slopes = np.repeat(slopes, Q_BLOCK, axis=1)
    parts, rest = [], np.float64(LOG2E)
    for _ in range(3):
        part = np.float64(np.asarray(rest).astype(BF16))
        parts.append(part)
        rest = rest - part
    qaug = np.zeros((NSA_KV_HEADS, HEAD_DIM, GQ), np.float32)
    for k, part in enumerate(parts):
        qaug[:, k, :] = part * SLC_LEN * slopes
        qaug[:, 3 + k, :] = part * slopes
    kq = np.arange(Q_BLOCK)[:, None] - np.arange(Q_BLOCK)[None, :]
    causb = np.where(kq <= 0, 0.0, NEG).astype(np.float32)
    lowb = np.where(kq > 0, 0.0, NEG).astype(np.float32)
    return jnp.asarray(ovT, BF16), jnp.asarray(qaug, BF16), jnp.asarray(lowb), jnp.asarray(causb)


def _mixer(x2, B, T, norm_g, w_in, q_norm_g, k_norm_g, cmp_pos, w_cmp, ret_norm_g, w_out):
    ns = T // SLC_LEN
    q, kv, kc, vc, gt, pret = _inproj(x2, norm_g[None, :], _permute_w_in(w_in))
    qT, ks, kw, vsT, vwT, gT = _prep(q, kv, gt, q_norm_g[None, :], k_norm_g[1:3], B, T)
    wk, pk = _compress_weights(w_cmp[0], cmp_pos[0])
    wv, pv = _compress_weights(w_cmp[1], cmp_pos[1])
    kcmp, vcT = _compress(kc, vc, wk, wv, pk, pv, k_norm_g[0:1], B, T)
    ovT, qaug, lowb, causb = _nsa_consts(T)
    ocmp, sel, flags = _nsa_cmp(qT, qaug, kcmp, vcT, ovT, B, T)
    lists, counts = _nsa_steps(flags[:, :, :, 0, :].reshape(-1, ns), T // Q_BLOCK)
    kpad = jnp.zeros((WIN, 2 * HEAD_DIM), BF16).at[:, HEAD_DIM:HEAD_DIM + 3].set(-2.0 ** 100)
    kw = jnp.concatenate([jnp.broadcast_to(kpad, kw.shape[:2] + kpad.shape), kw], axis=2)
    vwT = jnp.pad(vwT, ((0, 0), (0, 0), (0, 0), (WIN, 0)))
    nsa = _nsa_main(lists, counts, qT, qaug, ks, vsT, kw, vwT, sel, gT, ocmp, lowb, causb, B, T)
    ret = _retention(pret, ret_norm_g[None, :], B, T)
    return _outproj(x2, nsa.reshape(B * T, NSA_WIDTH), ret, w_out.astype(BF16))


def _moe_layer(x2, norm_g, router, router_b, wg, wu, wd, tm=1024):
    tm = min(tm, x2.shape[0])
    h, rank, comb, rankT, cnt = _router(x2, norm_g[None, :], router, router_b, tm)
    counts = cnt[:, 0, :N_EXPERTS].reshape(-1)
    return _moe(counts, h, rankT, rank, comb, wg.astype(BF16), wu.astype(BF16), wd.astype(BF16), x2, tm)


def kernel(x, norm_mix_g, w_in, q_norm_g, k_norm_g, cmp_pos, w_cmp, ret_norm_g, w_out, norm_ffn_g,
           ffn_w_gate, ffn_w_up, ffn_w_down, moe_router, moe_router_b, moe_w_gate, moe_w_up, moe_w_down):
    B, T, D = x.shape
    depth = norm_mix_g.shape[0]
    x2 = x.reshape(B * T, D)
    for l in range(depth):
        x2 = _mixer(x2, B, T, norm_mix_g[l], w_in[l], q_norm_g[l], k_norm_g[l], cmp_pos[l], w_cmp[l],
                    ret_norm_g[l], w_out[l])
        j = l // 2
        if l % 2 == 0:
            x2 = _ffn(x2, norm_ffn_g[l][None, :], ffn_w_gate[j].astype(BF16), ffn_w_up[j].astype(BF16),
                      ffn_w_down[j].astype(BF16))
        else:
            x2 = _moe_layer(x2, norm_ffn_g[l], moe_router[j], moe_router_b[j], moe_w_gate[j], moe_w_up[j],
                            moe_w_down[j])
    return x2.reshape(B, T, D)
```

```python
import functools

import numpy as np
import jax
import jax.numpy as jnp
from jax import lax
from jax.experimental import pallas as pl
from jax.experimental.pallas import tpu as pltpu

F32 = jnp.float32
BF16 = jnp.bfloat16

HEAD_DIM = 64
NSA_HEADS = 8
NSA_KV_HEADS = 2
NSA_GROUP = NSA_HEADS // NSA_KV_HEADS
RET_HEADS = 8
RET_DK = 32
RET_DV = 64
NSA_WIDTH = NSA_HEADS * HEAD_DIM
RET_WIDTH = RET_HEADS * RET_DV
KV_WIDTH = NSA_KV_HEADS * HEAD_DIM
CMP_LEN = 32
CMP_STRIDE = 16
SLC_LEN = 64
SLC_TOPK = 16
WIN = 512
Q_BLOCK = 128
RET_CHUNK = 128
N_EXPERTS = 8
EPS = 1e-6
NEG = -1e30
BIG = 1e9
LANES = 128
GQ = NSA_GROUP * Q_BLOCK
KEY_STEP = 128
STEP_GROUP = 4
N_FORCED = 3
CMP_CHUNK = 128
CMP_TAIL = CMP_CHUNK + 8
WIN_KEYS = WIN + Q_BLOCK
V_ROWS = HEAD_DIM + 16
LOG2E = 1.4426950408889634
VMEM_LIMIT = 60 * 1024 * 1024

_C_Q = 0
_C_KV = _C_Q + NSA_WIDTH
_C_KC = _C_KV + 4 * KV_WIDTH
_C_VC = _C_KC + KV_WIDTH
_C_GT = _C_VC + KV_WIDTH
_C_RET = _C_GT + LANES
_RET_COLS = 2 * RET_HEADS * RET_DK + 2 * RET_WIDTH
_C_END = _C_RET + _RET_COLS


def _params(n_axes, vmem=VMEM_LIMIT):
    return pltpu.CompilerParams(dimension_semantics=("arbitrary",) * n_axes, vmem_limit_bytes=vmem)


def _dot(a, b):
    return jnp.dot(a, b, preferred_element_type=F32)


def _dot_nt(a, b):
    return lax.dot_general(a, b, (((1,), (1,)), ((), ())), preferred_element_type=F32)


def _rms(x, g):
    return x * lax.rsqrt(jnp.mean(x * x, axis=-1, keepdims=True) + EPS) * g


def _inproj_kernel(x_ref, g_ref, w_ref, q_ref, kv_ref, kc_ref, vc_ref, gt_ref, ret_ref):
    h = _rms(x_ref[...], g_ref[...]).astype(BF16)
    q_ref[...] = _dot(h, w_ref[:, _C_Q:_C_KV])
    kv_ref[...] = _dot(h, w_ref[:, _C_KV:_C_KC])
    kc_ref[...] = _dot(h, w_ref[:, _C_KC:_C_VC])
    vc_ref[...] = _dot(h, w_ref[:, _C_VC:_C_GT])
    gt_ref[...] = _dot(h, w_ref[:, _C_GT:_C_RET])
    ret_ref[...] = _dot(h, w_ref[:, _C_RET:_C_END])


def _inproj(x2, g, w, tm=512):
    n, d = x2.shape
    widths = (NSA_WIDTH, 4 * KV_WIDTH, KV_WIDTH, KV_WIDTH, LANES, _RET_COLS)
    return pl.pallas_call(
        _inproj_kernel,
        grid=(n // tm,),
        in_specs=[pl.BlockSpec((tm, d), lambda i: (i, 0)),
                  pl.BlockSpec((1, d), lambda i: (0, 0)),
                  pl.BlockSpec((d, _C_END), lambda i: (0, 0))],
        out_specs=[pl.BlockSpec((tm, c), lambda i: (i, 0)) for c in widths],
        out_shape=[jax.ShapeDtypeStruct((n, c), F32) for c in widths],
        compiler_params=_params(1),
        name="inproj",
    )(x2, g, w)


def _group_rms(x, g, ones_ref):
    w = x.shape[1]
    ones = ones_ref[0:w, 0:w]
    hi, mid, lo = _split3(x * x)
    ms = (_dot(hi, ones) + _dot(mid, ones) + _dot(lo, ones)) * (1.0 / HEAD_DIM)
    return x * lax.rsqrt(ms + EPS) * g


def _group_ones():
    lane = np.arange(NSA_WIDTH) // HEAD_DIM
    return jnp.asarray(lane[:, None] == lane[None, :], BF16)


def _prep_kernel(q_ref, kv_ref, gt_ref, qg_ref, kg_ref, ones_ref, qT_ref, ks_ref, kw_ref, vsT_ref, vwT_ref, gT_ref):
    scale = HEAD_DIM ** -0.5 * LOG2E
    qt = (_group_rms(q_ref[...], qg_ref[...], ones_ref) * scale).T
    for g in range(NSA_KV_HEADS):
        for r in range(NSA_GROUP):
            h = g * NSA_GROUP + r
            qT_ref[g, :, r * Q_BLOCK:(r + 1) * Q_BLOCK] = qt[h * HEAD_DIM:(h + 1) * HEAD_DIM, :].astype(BF16)
    kv = kv_ref[...]
    ks, kw = kv[:, 0:KV_WIDTH], kv[:, KV_WIDTH:2 * KV_WIDTH]
    vst = kv[:, 2 * KV_WIDTH:3 * KV_WIDTH].T
    vwt = kv[:, 3 * KV_WIDTH:4 * KV_WIDTH].T
    pos = pl.program_id(1) * Q_BLOCK + lax.broadcasted_iota(jnp.int32, (Q_BLOCK, HEAD_DIM), 0)
    col = lax.broadcasted_iota(jnp.int32, (Q_BLOCK, HEAD_DIM), 1)
    kpos = jnp.where(col < 3, pos // SLC_LEN, jnp.where(col < 6, pos % SLC_LEN, 0)).astype(F32)
    ones_row = (lax.broadcasted_iota(jnp.int32, (V_ROWS - HEAD_DIM, Q_BLOCK), 0) == 0).astype(F32)
    ks = _group_rms(ks, kg_ref[0:1, :], ones_ref)
    kw = _group_rms(kw, kg_ref[1:2, :], ones_ref)
    for g in range(NSA_KV_HEADS):
        sl = slice(g * HEAD_DIM, (g + 1) * HEAD_DIM)
        ks_ref[g] = jnp.concatenate([ks[:, sl], kpos], axis=1).astype(BF16)
        kw_ref[g] = jnp.concatenate([kw[:, sl], kpos], axis=1).astype(BF16)
        vsT_ref[g] = jnp.concatenate([vst[sl, :], ones_row], axis=0).astype(BF16)
        vwT_ref[g] = jnp.concatenate([vwt[sl, :], ones_row], axis=0).astype(BF16)
    gT_ref[...] = jax.nn.sigmoid(gt_ref[...].T[0:32, :])


def _prep(q, kv, gt, qg, kg, B, T):
    nq = T // Q_BLOCK
    G = NSA_KV_HEADS
    row = lambda b, i: (b * nq + i, 0)
    return pl.pallas_call(
        _prep_kernel,
        grid=(B, nq),
        in_specs=[pl.BlockSpec((Q_BLOCK, NSA_WIDTH), row),
                  pl.BlockSpec((Q_BLOCK, 4 * KV_WIDTH), row),
                  pl.BlockSpec((Q_BLOCK, LANES), row),
                  pl.BlockSpec((1, NSA_WIDTH), lambda b, i: (0, 0)),
                  pl.BlockSpec((2, KV_WIDTH), lambda b, i: (0, 0)),
                  pl.BlockSpec((NSA_WIDTH, NSA_WIDTH), lambda b, i: (0, 0))],
        out_specs=[pl.BlockSpec((None, G, None, HEAD_DIM, GQ), lambda b, i: (b, 0, i, 0, 0)),
                   pl.BlockSpec((None, G, Q_BLOCK, 2 * HEAD_DIM), lambda b, i: (b, 0, i, 0)),
                   pl.BlockSpec((None, G, Q_BLOCK, 2 * HEAD_DIM), lambda b, i: (b, 0, i, 0)),
                   pl.BlockSpec((None, G, V_ROWS, Q_BLOCK), lambda b, i: (b, 0, 0, i)),
                   pl.BlockSpec((None, G, V_ROWS, Q_BLOCK), lambda b, i: (b, 0, 0, i)),
                   pl.BlockSpec((None, None, 32, Q_BLOCK), lambda b, i: (b, i, 0, 0))],
        out_shape=[jax.ShapeDtypeStruct((B, G, nq, HEAD_DIM, GQ), BF16),
                   jax.ShapeDtypeStruct((B, G, T, 2 * HEAD_DIM), BF16),
                   jax.ShapeDtypeStruct((B, G, T, 2 * HEAD_DIM), BF16),
                   jax.ShapeDtypeStruct((B, G, V_ROWS, T), BF16),
                   jax.ShapeDtypeStruct((B, G, V_ROWS, T), BF16),
                   jax.ShapeDtypeStruct((B, nq, 32, Q_BLOCK), F32)],
        compiler_params=_params(2),
        name="nsa_prep",
    )(q, kv, gt, jnp.tile(qg, (1, NSA_HEADS)), jnp.tile(kg, (1, NSA_KV_HEADS)), _group_ones())


def _compress_kernel(kc_ref, vc_ref, wk_ref, wv_ref, pk_ref, pv_ref, kg_ref, kcmp_ref, vcT_ref):
    ncp = kc_ref.shape[0] // CMP_STRIDE

    def comp(a_ref, w_ref, p_ref):
        lo = jnp.zeros((ncp, KV_WIDTH), F32)
        hi = jnp.zeros((ncp, KV_WIDTH), F32)
        for l in range(CMP_STRIDE):
            a = a_ref[pl.ds(l, ncp, stride=CMP_STRIDE), :]
            lo += _dot((a + p_ref[0, l:l + 1, :]).astype(BF16), w_ref[0, l])
            hi += _dot((a + p_ref[1, l:l + 1, :]).astype(BF16), w_ref[1, l])
        return lo + pltpu.roll(hi, ncp - 1, 0)

    k = comp(kc_ref, wk_ref, pk_ref)
    v = comp(vc_ref, wv_ref, pv_ref).T
    cend = lax.broadcasted_iota(jnp.int32, (ncp, HEAD_DIM), 0) * CMP_STRIDE + (CMP_LEN - 1)
    col = lax.broadcasted_iota(jnp.int32, (ncp, HEAD_DIM), 1)
    kpos = jnp.where(col < 3, cend // SLC_LEN, jnp.where(col < 6, cend % SLC_LEN, 0)).astype(F32)
    for g in range(NSA_KV_HEADS):
        sl = slice(g * HEAD_DIM, (g + 1) * HEAD_DIM)
        kcmp_ref[g] = jnp.concatenate([_rms(k[:, sl], kg_ref[...]), kpos], axis=1).astype(BF16)
        vcT_ref[g] = v[sl, :].astype(BF16)


def _compress(kc, vc, wk, wv, pk, pv, kg, B, T):
    ncp = T // CMP_STRIDE
    G = NSA_KV_HEADS
    const4 = lambda b: (0, 0, 0, 0)
    const3 = lambda b: (0, 0, 0)
    const2 = lambda b: (0, 0)
    return pl.pallas_call(
        _compress_kernel,
        grid=(B,),
        in_specs=[pl.BlockSpec((T, KV_WIDTH), lambda b: (b, 0)),
                  pl.BlockSpec((T, KV_WIDTH), lambda b: (b, 0)),
                  pl.BlockSpec((2, CMP_STRIDE, KV_WIDTH, KV_WIDTH), const4),
                  pl.BlockSpec((2, CMP_STRIDE, KV_WIDTH, KV_WIDTH), const4),
                  pl.BlockSpec((2, CMP_STRIDE, KV_WIDTH), const3),
                  pl.BlockSpec((2, CMP_STRIDE, KV_WIDTH), const3),
                  pl.BlockSpec((1, HEAD_DIM), const2)],
        out_specs=[pl.BlockSpec((None, G, ncp, 2 * HEAD_DIM), lambda b: (b, 0, 0, 0)),
                   pl.BlockSpec((None, G, HEAD_DIM, ncp), lambda b: (b, 0, 0, 0))],
        out_shape=[jax.ShapeDtypeStruct((B, G, ncp, 2 * HEAD_DIM), BF16),
                   jax.ShapeDtypeStruct((B, G, HEAD_DIM, ncp), BF16)],
        compiler_params=_params(1),
        name="nsa_compress",
    )(kc, vc, wk, wv, pk, pv, kg)


def _compress_weights(w, pos):
    G = NSA_KV_HEADS
    w4 = w.reshape(2, CMP_STRIDE, HEAD_DIM, HEAD_DIM)
    eye = jnp.eye(G, dtype=w.dtype)
    wbd = jnp.einsum('hlde,gk->hlgdke', w4, eye).reshape(2, CMP_STRIDE, KV_WIDTH, KV_WIDTH)
    p = pos.reshape(2, CMP_STRIDE, 1, HEAD_DIM)
    p = jnp.broadcast_to(p, (2, CMP_STRIDE, G, HEAD_DIM)).reshape(2, CMP_STRIDE, KV_WIDTH)
    return wbd.astype(BF16), p


def _split3(x):
    hi = x.astype(BF16)
    r = x - hi.astype(F32)
    mid = r.astype(BF16)
    lo = (r - mid.astype(F32)).astype(BF16)
    return hi, mid, lo


def _nsa_cmp_kernel(qT_ref, qaug_ref, kc_ref, vcT_ref, ovT_ref, ocmp_ref, sel_ref, flag_ref, *, n_sel):
    ncp = kc_ref.shape[0]
    ns = ovT_ref.shape[0]
    i = pl.program_id(2)
    q0 = i * Q_BLOCK
    q = jnp.concatenate([qT_ref[...], qaug_ref[...]], axis=0)
    lane = lax.broadcasted_iota(jnp.int32, (1, GQ), 1)
    t_row = q0 + (lane & (Q_BLOCK - 1))
    has_cmp = (t_row >= CMP_LEN - 1).astype(F32)
    tq = q0 + lax.broadcasted_iota(jnp.int32, (1, Q_BLOCK), 1)
    cur = tq // SLC_LEN

    def prefix(rows):
        nsk = rows * CMP_STRIDE // SLC_LEN
        tail0 = max(rows - CMP_TAIL, 0)
        s = _dot(kc_ref[0:rows, :], q)
        cend = (lax.broadcasted_iota(jnp.int32, (rows - tail0, 1), 0) + tail0) * CMP_STRIDE + (CMP_LEN - 1)
        tail = jnp.where(t_row >= cend, s[tail0:], NEG)
        s = jnp.concatenate([s[0:tail0], tail], axis=0) if tail0 else tail
        m = jnp.max(s, axis=0, keepdims=True)
        e = jnp.exp2(s - m)
        p = e * (has_cmp / jnp.sum(e, axis=0, keepdims=True))
        ocmp_ref[...] = _dot(vcT_ref[:, 0:rows], p.astype(BF16))

        ps = p[:, 0:Q_BLOCK]
        for r in range(1, NSA_GROUP):
            ps = ps + p[:, r * Q_BLOCK:(r + 1) * Q_BLOCK]
        ov = ovT_ref[0:nsk, 0:rows]
        hi, mid, lo = _split3(ps)
        imp = _dot(ov, hi) + _dot(ov, mid) + _dot(ov, lo)

        blk = lax.broadcasted_iota(jnp.int32, (nsk, 1), 0)
        forced = (blk == 0) | (blk == cur) | (blk == cur - 1)
        valid = blk * SLC_LEN <= tq
        imp = jnp.where(forced, -3e38, jnp.where(valid, imp, -BIG))
        blk_f = blk.astype(F32)
        sel = forced.astype(F32)
        for _ in range(n_sel - N_FORCED):
            mx = jnp.max(imp, axis=0, keepdims=True)
            idx = jnp.min(jnp.where(imp == mx, blk_f, float(ns)), axis=0, keepdims=True)
            pick = blk_f == idx
            sel = jnp.where(pick, 1.0, sel)
            imp = jnp.where(pick, -3e38, imp)
        sel_ref[0:nsk, :] = sel
        cnt = _dot_nt(jnp.ones((8, Q_BLOCK), BF16), sel.astype(BF16))
        flag_ref[:, 0:nsk] = (cnt > 0).astype(jnp.int32)
        if nsk < ns:
            sel_ref[nsk:, :] = jnp.zeros((ns - nsk, Q_BLOCK), F32)
            flag_ref[:, nsk:] = jnp.zeros((8, ns - nsk), jnp.int32)

    n_variants = ncp // CMP_CHUNK
    variant = (i * (Q_BLOCK // CMP_STRIDE) + (Q_BLOCK // CMP_STRIDE - 2)) // CMP_CHUNK
    for k in range(n_variants):
        pl.when(variant == k)(functools.partial(prefix, (k + 1) * CMP_CHUNK))


def _nsa_cmp(qT, qaug, kcmp, vcT, ovT, B, T):
    G = NSA_KV_HEADS
    nq = T // Q_BLOCK
    ncp = T // CMP_STRIDE
    ns = T // SLC_LEN
    n_sel = min(SLC_TOPK, ns)
    assert ncp % CMP_CHUNK == 0 and n_sel > N_FORCED
    return pl.pallas_call(
        functools.partial(_nsa_cmp_kernel, n_sel=n_sel),
        grid=(B, G, nq),
        in_specs=[pl.BlockSpec((None, None, None, HEAD_DIM, GQ), lambda b, g, i: (b, g, i, 0, 0)),
                  pl.BlockSpec((None, HEAD_DIM, GQ), lambda b, g, i: (g, 0, 0)),
                  pl.BlockSpec((None, None, ncp, 2 * HEAD_DIM), lambda b, g, i: (b, g, 0, 0)),
                  pl.BlockSpec((None, None, HEAD_DIM, ncp), lambda b, g, i: (b, g, 0, 0)),
                  pl.BlockSpec((ns, ncp), lambda b, g, i: (0, 0))],
        out_specs=[pl.BlockSpec((None, None, None, HEAD_DIM, GQ), lambda b, g, i: (b, g, i, 0, 0)),
                   pl.BlockSpec((None, None, None, ns, Q_BLOCK), lambda b, g, i: (b, g, i, 0, 0)),
                   pl.BlockSpec((None, None, None, 8, ns), lambda b, g, i: (b, g, i, 0, 0))],
        out_shape=[jax.ShapeDtypeStruct((B, G, nq, HEAD_DIM, GQ), F32),
                   jax.ShapeDtypeStruct((B, G, nq, ns, Q_BLOCK), F32),
                   jax.ShapeDtypeStruct((B, G, nq, 8, ns), jnp.int32)],
        compiler_params=_params(3),
        name="nsa_cmp",
    )(qT, qaug, kcmp, vcT, ovT)


def _nsa_main_kernel(list_ref, cnt_ref, qT_ref, qaug_ref, ks_ref, vsT_ref, kw_ref, vwT_ref, sel_ref, gT_ref, ocmp_ref,
                     lowb_ref, causb_ref, out_ref, m_sc, acc_sc):
    b, g, i = pl.program_id(0), pl.program_id(1), pl.program_id(2)
    tile_id = (b * pl.num_programs(1) + g) * pl.num_programs(2) + i
    n_steps = sel_ref.shape[0] // 2
    q = jnp.concatenate([qT_ref[...], qaug_ref[...]], axis=0)
    k0 = pl.multiple_of(i * Q_BLOCK, Q_BLOCK)

    def sel_bias(j, valid):
        def row(r):
            picked = (sel_ref[pl.ds(r, 1), :] > 0.5) & valid
            return jnp.concatenate([jnp.where(picked, 0.0, NEG)] * NSA_GROUP, axis=1)
        return row(2 * j), row(2 * j + 1)

    def add_sel_bias(s, ba, bb):
        return jnp.concatenate([s[0:SLC_LEN] + ba, s[SLC_LEN:] + bb], axis=0)

    lowb = jnp.concatenate([lowb_ref[...]] * NSA_GROUP, axis=1)
    causb = jnp.concatenate([causb_ref[...]] * NSA_GROUP, axis=1)

    kd = ks_ref[pl.ds(k0, KEY_STEP), :]
    vd = vsT_ref[:, pl.ds(k0, KEY_STEP)]
    kwin = kw_ref[pl.ds(k0, WIN_KEYS), :]
    vwin = vwT_ref[:, pl.ds(k0, WIN_KEYS)]
    ba, bb = sel_bias(i, True)
    sd = _dot(kd, q)
    sw = _dot(kwin, q)
    sd = add_sel_bias(sd, ba, bb) + causb
    sw = jnp.concatenate([sw[0:Q_BLOCK] + lowb, sw[Q_BLOCK:WIN], sw[WIN:] + causb], axis=0)
    md = jnp.max(sd, axis=0, keepdims=True)
    mw = jnp.max(sw, axis=0, keepdims=True)
    accd = _dot(vd, jnp.exp2((sd - md).astype(BF16)))
    ow = _dot(vwin, jnp.exp2((sw - mw).astype(BF16)))
    m_sc[...] = md
    acc_sc[...] = accd
    o_win = ow[0:HEAD_DIM] / ow[HEAD_DIM:HEAD_DIM + 1]

    def scores(t):
        ks, vs, biases = [], [], []
        for x in range(STEP_GROUP):
            j = list_ref[tile_id * n_steps + t * STEP_GROUP + x]
            valid = j >= 0
            j = jnp.maximum(j, 0)
            kj = pl.multiple_of(j * KEY_STEP, KEY_STEP)
            ks.append(ks_ref[pl.ds(kj, KEY_STEP), :])
            vs.append(vsT_ref[:, pl.ds(kj, KEY_STEP)])
            biases.append(sel_bias(j, valid))
        s = _dot(jnp.concatenate(ks, axis=0), q)
        s = jnp.concatenate([add_sel_bias(s[x * KEY_STEP:(x + 1) * KEY_STEP], *biases[x])
                             for x in range(STEP_GROUP)], axis=0)
        return s, jnp.max(s, axis=0, keepdims=True), jnp.concatenate(vs, axis=1)

    def accumulate(s, smax, vcat):
        m_old = m_sc[...]
        m_new = jnp.maximum(m_old, smax)
        alpha = jnp.exp2(m_old - m_new)
        acc_sc[...] = alpha * acc_sc[...] + _dot(vcat, jnp.exp2((s - m_new).astype(BF16)))
        m_sc[...] = m_new

    def pair(t, carry):
        staged = [scores(2 * t), scores(2 * t + 1)]
        for args in staged:
            accumulate(*args)
        return carry

    def single(t, carry):
        accumulate(*scores(t))
        return carry

    n_groups = (cnt_ref[tile_id] + (STEP_GROUP - 1)) // STEP_GROUP
    lax.fori_loop(0, n_groups // 2, pair, 0)
    lax.fori_loop(n_groups // 2 * 2, n_groups, single, 0)
    o_slc = acc_sc[0:HEAD_DIM, :] / acc_sc[HEAD_DIM:HEAD_DIM + 1, :]

    def gate(k):
        rows = [gT_ref[pl.ds(g * (NSA_GROUP * 3) + r * 3 + k, 1), :] for r in range(NSA_GROUP)]
        return jnp.concatenate(rows, axis=1)

    o = gate(0) * ocmp_ref[...] + gate(1) * o_slc + gate(2) * o_win
    o = jnp.concatenate([o, jnp.zeros_like(o)], axis=0)
    for r in range(NSA_GROUP):
        out_ref[:, r * HEAD_DIM:(r + 1) * HEAD_DIM] = o[:, r * Q_BLOCK:(r + 1) * Q_BLOCK].T[:, 0:HEAD_DIM]


def _nsa_steps_kernel(flagT_ref, pairT_ref, list_ref, cnt_ref, *, nq):
    n_steps, nt = list_ref.shape
    need = _dot(pairT_ref[...], flagT_ref[...].astype(BF16)) > 0
    step = lax.broadcasted_iota(jnp.int32, (n_steps, 1), 0)
    own = lax.broadcasted_iota(jnp.int32, (1, nt), 1) % nq
    need = need & (step < own)
    need_f = need.astype(F32)
    earlier = (lax.broadcasted_iota(jnp.int32, (n_steps, n_steps), 1) < step).astype(BF16)
    slot = _dot(earlier, need_f.astype(BF16))
    total = jnp.sum(need_f, axis=0, keepdims=True)
    cnt_ref[...] = jnp.broadcast_to(total, cnt_ref.shape).astype(jnp.int32)
    step_f = step.astype(F32)
    for p in range(n_steps):
        val = jnp.sum(jnp.where(need & (slot == p), step_f, 0.0), axis=0, keepdims=True)
        list_ref[p:p + 1, :] = jnp.where(total > p, val, -1.0).astype(jnp.int32)


def _nsa_steps(flags, nq):
    nt, ns = flags.shape
    n_steps = ns // 2
    pairT = jnp.asarray(np.arange(n_steps)[:, None] == np.arange(ns)[None, :] // 2, BF16)
    lists, counts = pl.pallas_call(
        functools.partial(_nsa_steps_kernel, nq=nq),
        out_shape=[jax.ShapeDtypeStruct((n_steps, nt), jnp.int32), jax.ShapeDtypeStruct((8, nt), jnp.int32)],
        name="nsa_steps",
    )(flags.T.astype(F32), pairT)
    return lists.T.reshape(-1), counts[0]


def _nsa_main(lists, counts, qT, qaug, ks, vsT, kw, vwT, sel, gT, ocmp, lowb, causb, B, T):
    G = NSA_KV_HEADS
    nq = T // Q_BLOCK
    ns = T // SLC_LEN
    whole = lambda b, g, i, *_: (b, g, 0, 0)
    tile = lambda b, g, i, *_: (b, g, i, 0, 0)
    const = lambda b, g, i, *_: (0, 0)
    grid_spec = pltpu.PrefetchScalarGridSpec(
        num_scalar_prefetch=2,
        grid=(B, G, nq),
        in_specs=[pl.BlockSpec((None, None, None, HEAD_DIM, GQ), tile),
                  pl.BlockSpec((None, HEAD_DIM, GQ), lambda b, g, i, *_: (g, 0, 0)),
                  pl.BlockSpec((None, None, T, 2 * HEAD_DIM), whole),
                  pl.BlockSpec((None, None, V_ROWS, T), whole),
                  pl.BlockSpec((None, None, T + WIN, 2 * HEAD_DIM), whole),
                  pl.BlockSpec((None, None, V_ROWS, T + WIN), whole),
                  pl.BlockSpec((None, None, None, ns, Q_BLOCK), tile),
                  pl.BlockSpec((None, None, 32, Q_BLOCK), lambda b, g, i, *_: (b, i, 0, 0)),
                  pl.BlockSpec((None, None, None, HEAD_DIM, GQ), tile),
                  pl.BlockSpec((Q_BLOCK, Q_BLOCK), const),
                  pl.BlockSpec((Q_BLOCK, Q_BLOCK), const)],
        out_specs=pl.BlockSpec((None, Q_BLOCK, NSA_GROUP * HEAD_DIM), lambda b, g, i, *_: (b, i, g)),
        scratch_shapes=[pltpu.VMEM((1, GQ), F32), pltpu.VMEM((V_ROWS, GQ), F32)],
    )
    return pl.pallas_call(
        _nsa_main_kernel,
        grid_spec=grid_spec,
        out_shape=jax.ShapeDtypeStruct((B, T, NSA_WIDTH), F32),
        compiler_params=_params(3),
        name="nsa_main",
    )(lists, counts, qT, qaug, ks, vsT, kw, vwT, sel, gT, ocmp, lowb, causb)


def _ret_kernel(p_ref, decay_ref, xi_ref, zeta_ref, gch_ref, ng_ref, out_ref, state_ref):
    @pl.when(pl.program_id(1) == 0)
    def _():
        state_ref[...] = jnp.zeros(state_ref.shape, F32)

    kw = RET_HEADS * RET_DK
    p = p_ref[...]
    rq = p[:, 0:kw] * (RET_DK ** -0.5)
    rk = p[:, kw:2 * kw]
    rkT = rk.T
    rv = p[:, 2 * kw:2 * kw + RET_WIDTH]
    rg = p[:, 2 * kw + RET_WIDTH:2 * kw + 2 * RET_WIDTH]
    xi = xi_ref[...]
    outs = []
    for h in range(RET_HEADS):
        qh = rq[:, h * RET_DK:(h + 1) * RET_DK]
        kh = rk[:, h * RET_DK:(h + 1) * RET_DK]
        khT = rkT[h * RET_DK:(h + 1) * RET_DK, :]
        vh = rv[:, h * RET_DV:(h + 1) * RET_DV]
        st = state_ref[h]
        inner = _dot_nt(qh, kh) * decay_ref[h]
        o = _dot(inner, vh) + _dot(qh, st) * xi[:, h:h + 1]
        state_ref[h] = st * gch_ref[h:h + 1, 0:1] + _dot(khT * zeta_ref[h:h + 1, :], vh)
        outs.append(_rms(o, ng_ref[:, h * RET_DV:(h + 1) * RET_DV]))
    out_ref[...] = jnp.concatenate(outs, axis=1) * (rg * jax.nn.sigmoid(rg))


def _ret_consts():
    H, C = RET_HEADS, RET_CHUNK
    log_g = jnp.log1p(-jnp.exp2(-5.0 - jnp.arange(H, dtype=F32)))
    idx = jnp.arange(C, dtype=F32)
    diff = idx[:, None] - idx[None, :]
    decay = jnp.where(diff >= 0, jnp.exp(jnp.maximum(diff, 0.0) * log_g[:, None, None]), 0.0)
    zeta = jnp.exp((C - 1 - idx) * log_g[:, None])
    xi = jnp.exp((idx + 1) * log_g[:, None]).T
    g_chunk = jnp.broadcast_to(jnp.exp(C * log_g)[:, None], (H, LANES))
    return decay, xi, zeta, g_chunk


def _retention(pret, ng, B, T):
    nch = T // RET_CHUNK
    decay, xi, zeta, gch = _ret_consts()
    c2 = lambda b, c: (0, 0)
    return pl.pallas_call(
        _ret_kernel,
        grid=(B, nch),
        in_specs=[pl.BlockSpec((RET_CHUNK, _RET_COLS), lambda b, c: (b * nch + c, 0)),
                  pl.BlockSpec((RET_HEADS, RET_CHUNK, RET_CHUNK), lambda b, c: (0, 0, 0)),
                  pl.BlockSpec((RET_CHUNK, RET_HEADS), c2),
                  pl.BlockSpec((RET_HEADS, RET_CHUNK), c2),
                  pl.BlockSpec((RET_HEADS, LANES), c2),
                  pl.BlockSpec((1, RET_WIDTH), c2)],
        out_specs=pl.BlockSpec((RET_CHUNK, RET_WIDTH), lambda b, c: (b * nch + c, 0)),
        out_shape=jax.ShapeDtypeStruct((B * T, RET_WIDTH), F32),
        scratch_shapes=[pltpu.VMEM((RET_HEADS, RET_DK, RET_DV), F32)],
        compiler_params=_params(2),
        name="retention",
    )(pret, decay, xi, zeta, gch, ng)


def _outproj_kernel(x_ref, nsa_ref, ret_ref, w_ref, o_ref):
    o_ref[...] = (x_ref[...] + _dot(nsa_ref[...].astype(BF16), w_ref[0:NSA_WIDTH, :])
                  + _dot(ret_ref[...].astype(BF16), w_ref[NSA_WIDTH:, :]))


def _outproj(x2, nsa, ret, w, tm=512):
    n, d = x2.shape
    return pl.pallas_call(
        _outproj_kernel,
        grid=(n // tm,),
        in_specs=[pl.BlockSpec((tm, d), lambda i: (i, 0)),
                  pl.BlockSpec((tm, NSA_WIDTH), lambda i: (i, 0)),
                  pl.BlockSpec((tm, RET_WIDTH), lambda i: (i, 0)),
                  pl.BlockSpec((NSA_WIDTH + RET_WIDTH, d), lambda i: (0, 0))],
        out_specs=pl.BlockSpec((tm, d), lambda i: (i, 0)),
        out_shape=jax.ShapeDtypeStruct((n, d), F32),
        compiler_params=_params(1),
        name="outproj",
    )(x2, nsa, ret, w)


def _ffn_kernel(x_ref, g_ref, wg_ref, wu_ref, wd_ref, o_ref, h_sc):
    f = pl.program_id(1)

    @pl.when(f == 0)
    def _():
        x = x_ref[...]
        h_sc[...] = _rms(x, g_ref[...]).astype(BF16)
        o_ref[...] = x

    h = h_sc[...]
    a = _dot(h, wg_ref[...])
    act = (a * jax.nn.sigmoid(a) * _dot(h, wu_ref[...])).astype(BF16)
    o_ref[...] += _dot(act, wd_ref[...])


def _ffn(x2, g, wg, wu, wd, tm=512, fc=1408):
    n, d = x2.shape
    dff = wg.shape[1]
    return pl.pallas_call(
        _ffn_kernel,
        grid=(n // tm, dff // fc),
        in_specs=[pl.BlockSpec((tm, d), lambda i, f: (i, 0)),
                  pl.BlockSpec((1, d), lambda i, f: (0, 0)),
                  pl.BlockSpec((d, fc), lambda i, f: (0, f)),
                  pl.BlockSpec((d, fc), lambda i, f: (0, f)),
                  pl.BlockSpec((fc, d), lambda i, f: (f, 0))],
        out_specs=pl.BlockSpec((tm, d), lambda i, f: (i, 0)),
        out_shape=jax.ShapeDtypeStruct((n, d), F32),
        scratch_shapes=[pltpu.VMEM((tm, d), BF16)],
        compiler_params=_params(2),
        name="ffn_dense",
    )(x2, g, wg, wu, wd)


def _router_kernel(x_ref, g_ref, r_ref, rb_ref, tri_ref, h_ref, rank_ref, comb_ref, rankT_ref, cnt_ref):
    h = _rms(x_ref[...], g_ref[...])
    h_ref[...] = h.astype(BF16)
    hh, hm, hl = _split3(h)
    rh, rm, rl = _split3(r_ref[...])
    logits = (_dot(hh, rh) + (_dot(hh, rm) + _dot(hm, rh)) + (_dot(hh, rl) + _dot(hm, rm) + _dot(hl, rh))
              + rb_ref[...])
    lane = lax.broadcasted_iota(jnp.int32, logits.shape, 1).astype(F32)
    logits = jnp.where(lane < N_EXPERTS, logits, NEG)
    m1 = jnp.max(logits, axis=1, keepdims=True)
    i1 = jnp.min(jnp.where(logits == m1, lane, float(LANES)), axis=1, keepdims=True)
    l2 = jnp.where(lane == i1, NEG, logits)
    m2 = jnp.max(l2, axis=1, keepdims=True)
    i2 = jnp.min(jnp.where(l2 == m2, lane, float(LANES)), axis=1, keepdims=True)
    e2 = jnp.exp(m2 - m1)
    w1 = 1.0 / (1.0 + e2)
    w2 = e2 / (1.0 + e2)
    use1, use2 = lane == i1, lane == i2
    comb_ref[...] = jnp.where(use1, w1, 0.0) + jnp.where(use2, w2, 0.0)
    use = (use1 | use2).astype(F32)
    rank = jnp.where(use > 0, _dot(tri_ref[...], use.astype(BF16)), -1.0)
    rank_ref[...] = rank
    rankT_ref[...] = rank.T[0:N_EXPERTS, :]
    cnt_ref[...] = jnp.broadcast_to(jnp.sum(use, axis=0, keepdims=True), cnt_ref.shape).astype(jnp.int32)


def _router(x2, g, router, rb, tm):
    n, d = x2.shape
    nt = n // tm
    tri = (jnp.arange(tm)[:, None] > jnp.arange(tm)[None, :]).astype(BF16)
    rpad = jnp.zeros((d, LANES), F32).at[:, :N_EXPERTS].set(router)
    rbpad = jnp.zeros((1, LANES), F32).at[0, :N_EXPERTS].set(rb)
    c2 = lambda i: (0, 0)
    return pl.pallas_call(
        _router_kernel,
        grid=(nt,),
        in_specs=[pl.BlockSpec((tm, d), lambda i: (i, 0)),
                  pl.BlockSpec((1, d), c2),
                  pl.BlockSpec((d, LANES), c2),
                  pl.BlockSpec((1, LANES), c2),
                  pl.BlockSpec((tm, tm), c2)],
        out_specs=[pl.BlockSpec((tm, d), lambda i: (i, 0)),
                   pl.BlockSpec((tm, LANES), lambda i: (i, 0)),
                   pl.BlockSpec((tm, LANES), lambda i: (i, 0)),
                   pl.BlockSpec((N_EXPERTS, tm), lambda i: (0, i)),
                   pl.BlockSpec((None, 8, LANES), lambda i: (i, 0, 0))],
        out_shape=[jax.ShapeDtypeStruct((n, d), BF16),
                   jax.ShapeDtypeStruct((n, LANES), F32),
                   jax.ShapeDtypeStruct((n, LANES), F32),
                   jax.ShapeDtypeStruct((N_EXPERTS, n), F32),
                   jax.ShapeDtypeStruct((nt, 8, LANES), jnp.int32)],
        compiler_params=_params(1),
        name="moe_router",
    )(x2, g, rpad, rbpad, tri)


MOE_SUB = 128
MOE_MOVE = 2 * MOE_SUB


def _moe_kernel(cnt_ref, h_ref, rankT_ref, rank_ref, comb_ref, wg_ref, wu_ref, wd_ref, x_ref, o_ref, hc_sc, oacc_sc):
    t, e, f = pl.program_id(0), pl.program_id(1), pl.program_id(2)
    nf = pl.num_programs(2)
    tm = h_ref.shape[0]
    nsub = (cnt_ref[t * N_EXPERTS + e] + (MOE_SUB - 1)) // MOE_SUB
    nmove = (nsub + 1) // 2

    @pl.when((e == 0) & (f == 0))
    def _():
        o_ref[...] = x_ref[...]

    @pl.when(f == 0)
    def _():
        rank_row = rankT_ref[...]

        def gather(s, c):
            r0 = pl.multiple_of(s * MOE_MOVE, MOE_MOVE)
            rows = (lax.broadcasted_iota(jnp.int32, (MOE_MOVE, 1), 0) + r0).astype(F32)
            onehot = (rows == rank_row).astype(BF16)
            hc_sc[pl.ds(r0, MOE_MOVE), :] = _dot(onehot, h_ref[...]).astype(BF16)
            oacc_sc[pl.ds(r0, MOE_MOVE), :] = jnp.zeros((MOE_MOVE, oacc_sc.shape[1]), F32)
            return c

        lax.fori_loop(0, nmove, gather, 0)

    def expert(n_rows, s, c):
        r0 = pl.multiple_of(s * n_rows, n_rows)
        rows = hc_sc[pl.ds(r0, n_rows), :]
        a = _dot(rows, wg_ref[...])
        act = (a * jax.nn.sigmoid(a) * _dot(rows, wu_ref[...])).astype(BF16)
        oacc_sc[pl.ds(r0, n_rows), :] += _dot(act, wd_ref[...])
        return c

    lax.fori_loop(0, nsub // 2, functools.partial(expert, MOE_MOVE), 0)
    lax.fori_loop(nsub // 2 * 2, nsub, functools.partial(expert, MOE_SUB), 0)

    @pl.when(f == nf - 1)
    def _():
        is_e = lax.broadcasted_iota(jnp.int32, (1, LANES), 1) == e
        rank_col = jnp.sum(jnp.where(is_e, rank_ref[...], 0.0), axis=1, keepdims=True)
        comb_col = jnp.sum(jnp.where(is_e, comb_ref[...], 0.0), axis=1, keepdims=True)

        def scatter(s, c):
            r0 = pl.multiple_of(s * MOE_MOVE, MOE_MOVE)
            cols = (lax.broadcasted_iota(jnp.int32, (1, MOE_MOVE), 1) + r0).astype(F32)
            onehot = (rank_col == cols).astype(BF16)
            y = _dot(onehot, oacc_sc[pl.ds(r0, MOE_MOVE), :].astype(BF16))
            o_ref[...] += comb_col * y
            return c

        lax.fori_loop(0, nmove, scatter, 0)


def _moe(counts, h, rankT, rank, comb, wg, wu, wd, x2, tm, fc=1408):
    n, d = x2.shape
    dff = wg.shape[2]
    grid_spec = pltpu.PrefetchScalarGridSpec(
        num_scalar_prefetch=1,
        grid=(n // tm, N_EXPERTS, dff // fc),
        in_specs=[pl.BlockSpec((tm, d), lambda t, e, f, c: (t, 0)),
                  pl.BlockSpec((None, 1, tm), lambda t, e, f, c: (e, 0, t)),
                  pl.BlockSpec((tm, LANES), lambda t, e, f, c: (t, 0)),
                  pl.BlockSpec((tm, LANES), lambda t, e, f, c: (t, 0)),
                  pl.BlockSpec((None, d, fc), lambda t, e, f, c: (e, 0, f)),
                  pl.BlockSpec((None, d, fc), lambda t, e, f, c: (e, 0, f)),
                  pl.BlockSpec((None, fc, d), lambda t, e, f, c: (e, f, 0)),
                  pl.BlockSpec((tm, d), lambda t, e, f, c: (t, 0))],
        out_specs=pl.BlockSpec((tm, d), lambda t, e, f, c: (t, 0)),
        scratch_shapes=[pltpu.VMEM((tm, d), BF16), pltpu.VMEM((tm, d), F32)],
    )
    return pl.pallas_call(
        _moe_kernel,
        grid_spec=grid_spec,
        out_shape=jax.ShapeDtypeStruct((n, d), F32),
        compiler_params=_params(3),
        name="moe_experts",
    )(counts, h, rankT.reshape(N_EXPERTS, 1, n), rank, comb, wg, wu, wd, x2)


def _permute_w_in(w):
    o = np.cumsum((0, NSA_WIDTH) + (KV_WIDTH,) * 6 + (3 * NSA_HEADS,))
    q, kc, vc, ks, vs, kw, vw, gts = (w[:, o[k]:o[k + 1]] for k in range(8))
    ret = w[:, o[8]:]
    pad = jnp.zeros((w.shape[0], LANES - 3 * NSA_HEADS), w.dtype)
    return jnp.concatenate([q, ks, kw, vs, vw, kc, vc, gts, pad, ret], axis=1).astype(BF16)


def _nsa_consts(T):
    ncp = T // CMP_STRIDE
    ns = T // SLC_LEN
    cs = np.arange(ncp) * CMP_STRIDE
    ss = np.arange(ns) * SLC_LEN
    ov = np.clip(np.minimum(cs[None, :] + CMP_LEN, ss[:, None] + SLC_LEN) - np.maximum(cs[None, :], ss[:, None]), 0, None)
    ovT = (ov.astype(np.float32) / CMP_LEN)
    ovT[:, ncp - 1] = 0.0
    h = np.arange(NSA_HEADS).reshape(NSA_KV_HEADS, NSA_GROUP) + 1
    slopes = np.exp2(-8.0 * h / NSA_HEADS).astype(np.float32)
    slopes = np.repeat(slopes, Q_BLOCK, axis=1)
    parts, rest = [], np.float64(LOG2E)
    for _ in range(3):
        part = np.float64(np.asarray(rest).astype(BF16))
        parts.append(part)
        rest = rest - part
    qaug = np.zeros((NSA_KV_HEADS, HEAD_DIM, GQ), np.float32)
    for k, part in enumerate(parts):
        qaug[:, k, :] = part * SLC_LEN * slopes
        qaug[:, 3 + k, :] = part * slopes
    kq = np.arange(Q_BLOCK)[:, None] - np.arange(Q_BLOCK)[None, :]
    causb = np.where(kq <= 0, 0.0, NEG).astype(np.float32)
    lowb = np.where(kq > 0, 0.0, NEG).astype(np.float32)
    return jnp.asarray(ovT, BF16), jnp.asarray(qaug, BF16), jnp.asarray(lowb), jnp.asarray(causb)


def _mixer(x2, B, T, norm_g, w_in, q_norm_g, k_norm_g, cmp_pos, w_cmp, ret_norm_g, w_out):
    ns = T // SLC_LEN
    q, kv, kc, vc, gt, pret = _inproj(x2, norm_g[None, :], _permute_w_in(w_in))
    qT, ks, kw, vsT, vwT, gT = _prep(q, kv, gt, q_norm_g[None, :], k_norm_g[1:3], B, T)
    wk, pk = _compress_weights(w_cmp[0], cmp_pos[0])
    wv, pv = _compress_weights(w_cmp[1], cmp_pos[1])
    kcmp, vcT = _compress(kc, vc, wk, wv, pk, pv, k_norm_g[0:1], B, T)
    ovT, qaug, lowb, causb = _nsa_consts(T)
    ocmp, sel, flags = _nsa_cmp(qT, qaug, kcmp, vcT, ovT, B, T)
    lists, counts = _nsa_steps(flags[:, :, :, 0, :].reshape(-1, ns), T // Q_BLOCK)
    kpad = jnp.zeros((WIN, 2 * HEAD_DIM), BF16).at[:, HEAD_DIM:HEAD_DIM + 3].set(-2.0 ** 100)
    kw = jnp.concatenate([jnp.broadcast_to(kpad, kw.shape[:2] + kpad.shape), kw], axis=2)
    vwT = jnp.pad(vwT, ((0, 0), (0, 0), (0, 0), (WIN, 0)))
    nsa = _nsa_main(lists, counts, qT, qaug, ks, vsT, kw, vwT, sel, gT, ocmp, lowb, causb, B, T)
    ret = _retention(pret, ret_norm_g[None, :], B, T)
    return _outproj(x2, nsa.reshape(B * T, NSA_WIDTH), ret, w_out.astype(BF16))


def _moe_layer(x2, norm_g, router, router_b, wg, wu, wd, tm=1024):
    tm = min(tm, x2.shape[0])
    h, rank, comb, rankT, cnt = _router(x2, norm_g[None, :], router, router_b, tm)
    counts = cnt[:, 0, :N_EXPERTS].reshape(-1)
    return _moe(counts, h, rankT, rank, comb, wg.astype(BF16), wu.astype(BF16), wd.astype(BF16), x2, tm)


def kernel(x, norm_mix_g, w_in, q_norm_g, k_norm_g, cmp_pos, w_cmp, ret_norm_g, w_out, norm_ffn_g,
           ffn_w_gate, ffn_w_up, ffn_w_down, moe_router, moe_router_b, moe_w_gate, moe_w_up, moe_w_down):
    B, T, D = x.shape
    depth = norm_mix_g.shape[0]
    x2 = x.reshape(B * T, D)
    for l in range(depth):
        x2 = _mixer(x2, B, T, norm_mix_g[l], w_in[l], q_norm_g[l], k_norm_g[l], cmp_pos[l], w_cmp[l],
                    ret_norm_g[l], w_out[l])
        j = l // 2
        if l % 2 == 0:
            x2 = _ffn(x2, norm_ffn_g[l][None, :], ffn_w_gate[j].astype(BF16), ffn_w_up[j].astype(BF16),
                      ffn_w_down[j].astype(BF16))
        else:
            x2 = _moe_layer(x2, norm_ffn_g[l], moe_router[j], moe_router_b[j], moe_w_gate[j], moe_w_up[j],
                            moe_w_down[j])
    return x2.reshape(B, T, D)
```

```python
import functools

import numpy as np
import jax
import jax.numpy as jnp
from jax import lax
from jax.experimental import pallas as pl
from jax.experimental.pallas import tpu as pltpu

F32 = jnp.float32
BF16 = jnp.bfloat16

HEAD_DIM = 64
NSA_HEADS = 8
NSA_KV_HEADS = 2
NSA_GROUP = NSA_HEADS // NSA_KV_HEADS
RET_HEADS = 8
RET_DK = 32
RET_DV = 64
NSA_WIDTH = NSA_HEADS * HEAD_DIM
RET_WIDTH = RET_HEADS * RET_DV
KV_WIDTH = NSA_KV_HEADS * HEAD_DIM
CMP_LEN = 32
CMP_STRIDE = 16
SLC_LEN = 64
SLC_TOPK = 16
WIN = 512
Q_BLOCK = 128
RET_CHUNK = 128
N_EXPERTS = 8
EPS = 1e-6
NEG = -1e30
BIG = 1e9
LANES = 128
GQ = NSA_GROUP * Q_BLOCK
KEY_STEP = 128
STEP_GROUP = 4
N_FORCED = 3
CMP_CHUNK = 128
CMP_TILES = 4
MAIN_TILES = 4
CMP_TAIL = CMP_CHUNK + 8
WIN_KEYS = WIN + Q_BLOCK
V_ROWS = HEAD_DIM + 16
LOG2E = 1.4426950408889634
VMEM_LIMIT = 60 * 1024 * 1024

_C_Q = 0
_C_KV = _C_Q + NSA_WIDTH
_C_KC = _C_KV + 4 * KV_WIDTH
_C_VC = _C_KC + KV_WIDTH
_C_GT = _C_VC + KV_WIDTH
_C_RET = _C_GT + LANES
_RET_COLS = 2 * RET_HEADS * RET_DK + 2 * RET_WIDTH
_C_END = _C_RET + _RET_COLS


def _params(n_axes, vmem=VMEM_LIMIT):
    return pltpu.CompilerParams(dimension_semantics=("arbitrary",) * n_axes, vmem_limit_bytes=vmem)


def _dot(a, b):
    return jnp.dot(a, b, preferred_element_type=F32)


def _dot_nt(a, b):
    return lax.dot_general(a, b, (((1,), (1,)), ((), ())), preferred_element_type=F32)


def _rms(x, g):
    return x * lax.rsqrt(jnp.mean(x * x, axis=-1, keepdims=True) + EPS) * g


def _inproj_kernel(x_ref, g_ref, w_ref, q_ref, kv_ref, kc_ref, vc_ref, gt_ref, ret_ref):
    h = _rms(x_ref[...], g_ref[...]).astype(BF16)
    q_ref[...] = _dot(h, w_ref[:, _C_Q:_C_KV])
    kv_ref[...] = _dot(h, w_ref[:, _C_KV:_C_KC])
    kc_ref[...] = _dot(h, w_ref[:, _C_KC:_C_VC])
    vc_ref[...] = _dot(h, w_ref[:, _C_VC:_C_GT])
    gt_ref[...] = _dot(h, w_ref[:, _C_GT:_C_RET])
    ret_ref[...] = _dot(h, w_ref[:, _C_RET:_C_END])


def _inproj(x2, g, w, tm=512):
    n, d = x2.shape
    widths = (NSA_WIDTH, 4 * KV_WIDTH, KV_WIDTH, KV_WIDTH, LANES, _RET_COLS)
    return pl.pallas_call(
        _inproj_kernel,
        grid=(n // tm,),
        in_specs=[pl.BlockSpec((tm, d), lambda i: (i, 0)),
                  pl.BlockSpec((1, d), lambda i: (0, 0)),
                  pl.BlockSpec((d, _C_END), lambda i: (0, 0))],
        out_specs=[pl.BlockSpec((tm, c), lambda i: (i, 0)) for c in widths],
        out_shape=[jax.ShapeDtypeStruct((n, c), F32) for c in widths],
        compiler_params=_params(1),
        name="inproj",
    )(x2, g, w)


def _group_rms(x, g, ones_ref):
    w = x.shape[1]
    ones = ones_ref[0:w, 0:w]
    hi, mid, lo = _split3(x * x)
    ms = (_dot(hi, ones) + _dot(mid, ones) + _dot(lo, ones)) * (1.0 / HEAD_DIM)
    return x * lax.rsqrt(ms + EPS) * g


def _group_ones():
    lane = np.arange(NSA_WIDTH) // HEAD_DIM
    return jnp.asarray(lane[:, None] == lane[None, :], BF16)


def _prep_kernel(q_ref, kv_ref, gt_ref, qg_ref, kg_ref, ones_ref, qT_ref, ks_ref, kw_ref, vsT_ref, vwT_ref, gT_ref):
    scale = HEAD_DIM ** -0.5 * LOG2E
    qt = (_group_rms(q_ref[...], qg_ref[...], ones_ref) * scale).T
    for g in range(NSA_KV_HEADS):
        for r in range(NSA_GROUP):
            h = g * NSA_GROUP + r
            qT_ref[g, :, r * Q_BLOCK:(r + 1) * Q_BLOCK] = qt[h * HEAD_DIM:(h + 1) * HEAD_DIM, :].astype(BF16)
    kv = kv_ref[...]
    ks, kw = kv[:, 0:KV_WIDTH], kv[:, KV_WIDTH:2 * KV_WIDTH]
    vst = kv[:, 2 * KV_WIDTH:3 * KV_WIDTH].T
    vwt = kv[:, 3 * KV_WIDTH:4 * KV_WIDTH].T
    pos = pl.program_id(1) * Q_BLOCK + lax.broadcasted_iota(jnp.int32, (Q_BLOCK, HEAD_DIM), 0)
    col = lax.broadcasted_iota(jnp.int32, (Q_BLOCK, HEAD_DIM), 1)
    kpos = jnp.where(col < 3, pos // SLC_LEN, jnp.where(col < 6, pos % SLC_LEN, 0)).astype(F32)
    ones_row = (lax.broadcasted_iota(jnp.int32, (V_ROWS - HEAD_DIM, Q_BLOCK), 0) == 0).astype(F32)
    ks = _group_rms(ks, kg_ref[0:1, :], ones_ref)
    kw = _group_rms(kw, kg_ref[1:2, :], ones_ref)
    for g in range(NSA_KV_HEADS):
        sl = slice(g * HEAD_DIM, (g + 1) * HEAD_DIM)
        ks_ref[g] = jnp.concatenate([ks[:, sl], kpos], axis=1).astype(BF16)
        kw_ref[g] = jnp.concatenate([kw[:, sl], kpos], axis=1).astype(BF16)
        vsT_ref[g] = jnp.concatenate([vst[sl, :], ones_row], axis=0).astype(BF16)
        vwT_ref[g] = jnp.concatenate([vwt[sl, :], ones_row], axis=0).astype(BF16)
    gT_ref[...] = jax.nn.sigmoid(gt_ref[...].T[0:32, :])


def _prep(q, kv, gt, qg, kg, B, T):
    nq = T // Q_BLOCK
    G = NSA_KV_HEADS
    row = lambda b, i: (b * nq + i, 0)
    return pl.pallas_call(
        _prep_kernel,
        grid=(B, nq),
        in_specs=[pl.BlockSpec((Q_BLOCK, NSA_WIDTH), row),
                  pl.BlockSpec((Q_BLOCK, 4 * KV_WIDTH), row),
                  pl.BlockSpec((Q_BLOCK, LANES), row),
                  pl.BlockSpec((1, NSA_WIDTH), lambda b, i: (0, 0)),
                  pl.BlockSpec((2, KV_WIDTH), lambda b, i: (0, 0)),
                  pl.BlockSpec((NSA_WIDTH, NSA_WIDTH), lambda b, i: (0, 0))],
        out_specs=[pl.BlockSpec((None, G, None, HEAD_DIM, GQ), lambda b, i: (b, 0, i, 0, 0)),
                   pl.BlockSpec((None, G, Q_BLOCK, 2 * HEAD_DIM), lambda b, i: (b, 0, i, 0)),
                   pl.BlockSpec((None, G, Q_BLOCK, 2 * HEAD_DIM), lambda b, i: (b, 0, i, 0)),
                   pl.BlockSpec((None, G, V_ROWS, Q_BLOCK), lambda b, i: (b, 0, 0, i)),
                   pl.BlockSpec((None, G, V_ROWS, Q_BLOCK), lambda b, i: (b, 0, 0, i)),
                   pl.BlockSpec((None, None, 32, Q_BLOCK), lambda b, i: (b, i, 0, 0))],
        out_shape=[jax.ShapeDtypeStruct((B, G, nq, HEAD_DIM, GQ), BF16),
                   jax.ShapeDtypeStruct((B, G, T, 2 * HEAD_DIM), BF16),
                   jax.ShapeDtypeStruct((B, G, T, 2 * HEAD_DIM), BF16),
                   jax.ShapeDtypeStruct((B, G, V_ROWS, T), BF16),
                   jax.ShapeDtypeStruct((B, G, V_ROWS, T), BF16),
                   jax.ShapeDtypeStruct((B, nq, 32, Q_BLOCK), F32)],
        compiler_params=_params(2),
        name="nsa_prep",
    )(q, kv, gt, jnp.tile(qg, (1, NSA_HEADS)), jnp.tile(kg, (1, NSA_KV_HEADS)), _group_ones())


def _compress_kernel(kc_ref, vc_ref, wk_ref, wv_ref, pk_ref, pv_ref, kg_ref, kcmp_ref, vcT_ref):
    ncp = kc_ref.shape[0] // CMP_STRIDE

    def comp(a_ref, w_ref, p_ref):
        lo = jnp.zeros((ncp, KV_WIDTH), F32)
        hi = jnp.zeros((ncp, KV_WIDTH), F32)
        for l in range(CMP_STRIDE):
            a = a_ref[pl.ds(l, ncp, stride=CMP_STRIDE), :]
            lo += _dot((a + p_ref[0, l:l + 1, :]).astype(BF16), w_ref[0, l])
            hi += _dot((a + p_ref[1, l:l + 1, :]).astype(BF16), w_ref[1, l])
        return lo + pltpu.roll(hi, ncp - 1, 0)

    k = comp(kc_ref, wk_ref, pk_ref)
    v = comp(vc_ref, wv_ref, pv_ref).T
    cend = lax.broadcasted_iota(jnp.int32, (ncp, HEAD_DIM), 0) * CMP_STRIDE + (CMP_LEN - 1)
    col = lax.broadcasted_iota(jnp.int32, (ncp, HEAD_DIM), 1)
    kpos = jnp.where(col < 3, cend // SLC_LEN, jnp.where(col < 6, cend % SLC_LEN, 0)).astype(F32)
    for g in range(NSA_KV_HEADS):
        sl = slice(g * HEAD_DIM, (g + 1) * HEAD_DIM)
        kcmp_ref[g] = jnp.concatenate([_rms(k[:, sl], kg_ref[...]), kpos], axis=1).astype(BF16)
        vcT_ref[g] = v[sl, :].astype(BF16)


def _compress(kc, vc, wk, wv, pk, pv, kg, B, T):
    ncp = T // CMP_STRIDE
    G = NSA_KV_HEADS
    const4 = lambda b: (0, 0, 0, 0)
    const3 = lambda b: (0, 0, 0)
    const2 = lambda b: (0, 0)
    return pl.pallas_call(
        _compress_kernel,
        grid=(B,),
        in_specs=[pl.BlockSpec((T, KV_WIDTH), lambda b: (b, 0)),
                  pl.BlockSpec((T, KV_WIDTH), lambda b: (b, 0)),
                  pl.BlockSpec((2, CMP_STRIDE, KV_WIDTH, KV_WIDTH), const4),
                  pl.BlockSpec((2, CMP_STRIDE, KV_WIDTH, KV_WIDTH), const4),
                  pl.BlockSpec((2, CMP_STRIDE, KV_WIDTH), const3),
                  pl.BlockSpec((2, CMP_STRIDE, KV_WIDTH), const3),
                  pl.BlockSpec((1, HEAD_DIM), const2)],
        out_specs=[pl.BlockSpec((None, G, ncp, 2 * HEAD_DIM), lambda b: (b, 0, 0, 0)),
                   pl.BlockSpec((None, G, HEAD_DIM, ncp), lambda b: (b, 0, 0, 0))],
        out_shape=[jax.ShapeDtypeStruct((B, G, ncp, 2 * HEAD_DIM), BF16),
                   jax.ShapeDtypeStruct((B, G, HEAD_DIM, ncp), BF16)],
        compiler_params=_params(1),
        name="nsa_compress",
    )(kc, vc, wk, wv, pk, pv, kg)


def _compress_weights(w, pos):
    G = NSA_KV_HEADS
    w4 = w.reshape(2, CMP_STRIDE, HEAD_DIM, HEAD_DIM)
    eye = jnp.eye(G, dtype=w.dtype)
    wbd = jnp.einsum('hlde,gk->hlgdke', w4, eye).reshape(2, CMP_STRIDE, KV_WIDTH, KV_WIDTH)
    p = pos.reshape(2, CMP_STRIDE, 1, HEAD_DIM)
    p = jnp.broadcast_to(p, (2, CMP_STRIDE, G, HEAD_DIM)).reshape(2, CMP_STRIDE, KV_WIDTH)
    return wbd.astype(BF16), p


def _split3(x):
    hi = x.astype(BF16)
    r = x - hi.astype(F32)
    mid = r.astype(BF16)
    lo = (r - mid.astype(F32)).astype(BF16)
    return hi, mid, lo


def _nsa_cmp_kernel(qT_ref, qaug_ref, kc_ref, vcT_ref, ovT_ref, ocmp_ref, sel_ref, flag_ref, *, n_sel):
    ncp = kc_ref.shape[0]
    ns = ovT_ref.shape[0]
    tiles = range(CMP_TILES)
    i0 = pl.program_id(2) * CMP_TILES
    lane = lax.broadcasted_iota(jnp.int32, (1, GQ), 1)
    q = [jnp.concatenate([qT_ref[u], qaug_ref[...]], axis=0) for u in tiles]
    t_row = [(i0 + u) * Q_BLOCK + (lane & (Q_BLOCK - 1)) for u in tiles]
    has_cmp = [(t_row[u] >= CMP_LEN - 1).astype(F32) for u in tiles]
    tq = [(i0 + u) * Q_BLOCK + lax.broadcasted_iota(jnp.int32, (1, Q_BLOCK), 1) for u in tiles]
    cur = [tq[u] // SLC_LEN for u in tiles]

    def prefix(rows):
        nsk = rows * CMP_STRIDE // SLC_LEN
        tail0 = max(rows - CMP_TAIL, 0)
        kc = kc_ref[0:rows, :]
        s = [_dot(kc, q[u]) for u in tiles]
        cend = (lax.broadcasted_iota(jnp.int32, (rows - tail0, 1), 0) + tail0) * CMP_STRIDE + (CMP_LEN - 1)
        tail = [jnp.where(t_row[u] >= cend, s[u][tail0:], NEG) for u in tiles]
        s = [jnp.concatenate([s[u][0:tail0], tail[u]], axis=0) if tail0 else tail[u] for u in tiles]
        m = [jnp.max(s[u], axis=0, keepdims=True) for u in tiles]
        e = [jnp.exp2(s[u] - m[u]) for u in tiles]
        p = [e[u] * (has_cmp[u] / jnp.sum(e[u], axis=0, keepdims=True)) for u in tiles]
        vc = vcT_ref[:, 0:rows]
        for u in tiles:
            ocmp_ref[u] = _dot(vc, p[u].astype(BF16))

        ps = [p[u][:, 0:Q_BLOCK] for u in tiles]
        for r in range(1, NSA_GROUP):
            ps = [ps[u] + p[u][:, r * Q_BLOCK:(r + 1) * Q_BLOCK] for u in tiles]
        ov = ovT_ref[0:nsk, 0:rows]
        split = [_split3(ps[u]) for u in tiles]
        imp = [_dot(ov, split[u][0]) + _dot(ov, split[u][1]) + _dot(ov, split[u][2]) for u in tiles]

        blk = lax.broadcasted_iota(jnp.int32, (nsk, 1), 0)
        forced = [(blk == 0) | (blk == cur[u]) | (blk == cur[u] - 1) for u in tiles]
        valid = [blk * SLC_LEN <= tq[u] for u in tiles]
        imp = [jnp.where(forced[u], -3e38, jnp.where(valid[u], imp[u], -BIG)) for u in tiles]
        blk_f = blk.astype(F32)
        sel = [forced[u].astype(F32) for u in tiles]
        for _ in range(n_sel - N_FORCED):
            mx = [jnp.max(imp[u], axis=0, keepdims=True) for u in tiles]
            idx = [jnp.min(jnp.where(imp[u] == mx[u], blk_f, float(ns)), axis=0, keepdims=True) for u in tiles]
            pick = [blk_f == idx[u] for u in tiles]
            sel = [jnp.where(pick[u], 1.0, sel[u]) for u in tiles]
            imp = [jnp.where(pick[u], -3e38, imp[u]) for u in tiles]
        ones = jnp.ones((8, Q_BLOCK), BF16)
        for u in tiles:
            sel_ref[u, 0:nsk, :] = sel[u]
            cnt = _dot_nt(ones, sel[u].astype(BF16))
            flag_ref[u, :, 0:nsk] = (cnt > 0).astype(jnp.int32)
            if nsk < ns:
                sel_ref[u, nsk:, :] = jnp.zeros((ns - nsk, Q_BLOCK), F32)
                flag_ref[u, :, nsk:] = jnp.zeros((8, ns - nsk), jnp.int32)

    n_variants = ncp // CMP_CHUNK
    last = i0 + CMP_TILES - 1
    variant = (last * (Q_BLOCK // CMP_STRIDE) + (Q_BLOCK // CMP_STRIDE - 2)) // CMP_CHUNK
    for k in range(n_variants):
        pl.when(variant == k)(functools.partial(prefix, (k + 1) * CMP_CHUNK))


def _nsa_cmp(qT, qaug, kcmp, vcT, ovT, B, T):
    G = NSA_KV_HEADS
    nq = T // Q_BLOCK
    ncp = T // CMP_STRIDE
    ns = T // SLC_LEN
    n_sel = min(SLC_TOPK, ns)
    assert ncp % CMP_CHUNK == 0 and n_sel > N_FORCED and nq % CMP_TILES == 0
    tile = lambda b, g, i: (b, g, i, 0, 0)
    return pl.pallas_call(
        functools.partial(_nsa_cmp_kernel, n_sel=n_sel),
        grid=(B, G, nq // CMP_TILES),
        in_specs=[pl.BlockSpec((None, None, CMP_TILES, HEAD_DIM, GQ), tile),
                  pl.BlockSpec((None, HEAD_DIM, GQ), lambda b, g, i: (g, 0, 0)),
                  pl.BlockSpec((None, None, ncp, 2 * HEAD_DIM), lambda b, g, i: (b, g, 0, 0)),
                  pl.BlockSpec((None, None, HEAD_DIM, ncp), lambda b, g, i: (b, g, 0, 0)),
                  pl.BlockSpec((ns, ncp), lambda b, g, i: (0, 0))],
        out_specs=[pl.BlockSpec((None, None, CMP_TILES, HEAD_DIM, GQ), tile),
                   pl.BlockSpec((None, None, CMP_TILES, ns, Q_BLOCK), tile),
                   pl.BlockSpec((None, None, CMP_TILES, 8, ns), tile)],
        out_shape=[jax.ShapeDtypeStruct((B, G, nq, HEAD_DIM, GQ), F32),
                   jax.ShapeDtypeStruct((B, G, nq, ns, Q_BLOCK), F32),
                   jax.ShapeDtypeStruct((B, G, nq, 8, ns), jnp.int32)],
        compiler_params=_params(3),
        name="nsa_cmp",
    )(qT, qaug, kcmp, vcT, ovT)


def _nsa_main_kernel(list_ref, cnt_ref, qT_ref, qaug_ref, ks_ref, vsT_ref, kw_ref, vwT_ref, sel_ref, gT_ref, ocmp_ref,
                     lowb_ref, causb_ref, out_ref, m_sc, acc_sc, win_sc):
    b, g = pl.program_id(0), pl.program_id(1)
    tiles = range(MAIN_TILES)
    i = [pl.program_id(2) * MAIN_TILES + u for u in tiles]
    tile_id = [(b * pl.num_programs(1) + g) * (pl.num_programs(2) * MAIN_TILES) + i[u] for u in tiles]
    n_steps = sel_ref.shape[1] // 2
    q = [jnp.concatenate([qT_ref[u], qaug_ref[...]], axis=0) for u in tiles]
    k0 = [pl.multiple_of(i[u] * Q_BLOCK, Q_BLOCK) for u in tiles]

    def sel_bias(u, j, valid):
        def row(r):
            picked = (sel_ref[u, pl.ds(r, 1), :] > 0.5) & valid
            return jnp.concatenate([jnp.where(picked, 0.0, NEG)] * NSA_GROUP, axis=1)
        return row(2 * j), row(2 * j + 1)

    def add_sel_bias(s, ba, bb):
        return jnp.concatenate([s[0:SLC_LEN] + ba, s[SLC_LEN:] + bb], axis=0)

    lowb = jnp.concatenate([lowb_ref[...]] * NSA_GROUP, axis=1)
    causb = jnp.concatenate([causb_ref[...]] * NSA_GROUP, axis=1)

    bias_d = [sel_bias(u, i[u], True) for u in tiles]
    sd = [_dot(ks_ref[pl.ds(k0[u], KEY_STEP), :], q[u]) for u in tiles]
    sw = [_dot(kw_ref[pl.ds(k0[u], WIN_KEYS), :], q[u]) for u in tiles]
    sd = [add_sel_bias(sd[u], *bias_d[u]) + causb for u in tiles]
    sw = [jnp.concatenate([sw[u][0:Q_BLOCK] + lowb, sw[u][Q_BLOCK:WIN], sw[u][WIN:] + causb], axis=0) for u in tiles]
    md = [jnp.max(sd[u], axis=0, keepdims=True) for u in tiles]
    mw = [jnp.max(sw[u], axis=0, keepdims=True) for u in tiles]
    accd = [_dot(vsT_ref[:, pl.ds(k0[u], KEY_STEP)], jnp.exp2((sd[u] - md[u]).astype(BF16))) for u in tiles]
    ow = [_dot(vwT_ref[:, pl.ds(k0[u], WIN_KEYS)], jnp.exp2((sw[u] - mw[u]).astype(BF16))) for u in tiles]
    for u in tiles:
        m_sc[u] = md[u]
        acc_sc[u] = accd[u]
        win_sc[u] = ow[u][0:HEAD_DIM] / ow[u][HEAD_DIM:HEAD_DIM + 1]

    def scores(u, t):
        ks, vs, biases = [], [], []
        for x in range(STEP_GROUP):
            j = list_ref[tile_id[u] * n_steps + t * STEP_GROUP + x]
            valid = j >= 0
            j = jnp.maximum(j, 0)
            kj = pl.multiple_of(j * KEY_STEP, KEY_STEP)
            ks.append(ks_ref[pl.ds(kj, KEY_STEP), :])
            vs.append(vsT_ref[:, pl.ds(kj, KEY_STEP)])
            biases.append(sel_bias(u, j, valid))
        s = _dot(jnp.concatenate(ks, axis=0), q[u])
        s = jnp.concatenate([add_sel_bias(s[x * KEY_STEP:(x + 1) * KEY_STEP], *biases[x])
                             for x in range(STEP_GROUP)], axis=0)
        return s, jnp.max(s, axis=0, keepdims=True), jnp.concatenate(vs, axis=1)

    def accumulate(u, s, smax, vcat):
        m_old = m_sc[u]
        m_new = jnp.maximum(m_old, smax)
        alpha = jnp.exp2(m_old - m_new)
        acc_sc[u] = alpha * acc_sc[u] + _dot(vcat, jnp.exp2((s - m_new).astype(BF16)))
        m_sc[u] = m_new

    def pair(u, t, carry):
        staged = [scores(u, 2 * t), scores(u, 2 * t + 1)]
        for args in staged:
            accumulate(u, *args)
        return carry

    def single(u, t, carry):
        accumulate(u, *scores(u, t))
        return carry

    for u in tiles:
        n_groups = (cnt_ref[tile_id[u]] + (STEP_GROUP - 1)) // STEP_GROUP
        lax.fori_loop(0, n_groups // 2, functools.partial(pair, u), 0)
        lax.fori_loop(n_groups // 2 * 2, n_groups, functools.partial(single, u), 0)

    def gate(u, k):
        rows = [gT_ref[u, pl.ds(g * (NSA_GROUP * 3) + r * 3 + k, 1), :] for r in range(NSA_GROUP)]
        return jnp.concatenate(rows, axis=1)

    o_slc = [acc_sc[u, 0:HEAD_DIM, :] / acc_sc[u, HEAD_DIM:HEAD_DIM + 1, :] for u in tiles]
    o = [gate(u, 0) * ocmp_ref[u] + gate(u, 1) * o_slc[u] + gate(u, 2) * win_sc[u] for u in tiles]
    o = [jnp.concatenate([o[u], jnp.zeros_like(o[u])], axis=0) for u in tiles]
    for r in range(NSA_GROUP):
        ot = [o[u][:, r * Q_BLOCK:(r + 1) * Q_BLOCK].T[:, 0:HEAD_DIM] for u in tiles]
        for u in tiles:
            out_ref[u * Q_BLOCK:(u + 1) * Q_BLOCK, r * HEAD_DIM:(r + 1) * HEAD_DIM] = ot[u]


def _nsa_steps_kernel(flagT_ref, pairT_ref, list_ref, cnt_ref, *, nq):
    n_steps, nt = list_ref.shape
    need = _dot(pairT_ref[...], flagT_ref[...].astype(BF16)) > 0
    step = lax.broadcasted_iota(jnp.int32, (n_steps, 1), 0)
    own = lax.broadcasted_iota(jnp.int32, (1, nt), 1) % nq
    need = need & (step < own)
    need_f = need.astype(F32)
    earlier = (lax.broadcasted_iota(jnp.int32, (n_steps, n_steps), 1) < step).astype(BF16)
    slot = _dot(earlier, need_f.astype(BF16))
    total = jnp.sum(need_f, axis=0, keepdims=True)
    cnt_ref[...] = jnp.broadcast_to(total, cnt_ref.shape).astype(jnp.int32)
    step_f = step.astype(F32)
    for p in range(n_steps):
        val = jnp.sum(jnp.where(need & (slot == p), step_f, 0.0), axis=0, keepdims=True)
        list_ref[p:p + 1, :] = jnp.where(total > p, val, -1.0).astype(jnp.int32)


def _nsa_steps(flags, nq):
    nt, ns = flags.shape
    n_steps = ns // 2
    pairT = jnp.asarray(np.arange(n_steps)[:, None] == np.arange(ns)[None, :] // 2, BF16)
    lists, counts = pl.pallas_call(
        functools.partial(_nsa_steps_kernel, nq=nq),
        out_shape=[jax.ShapeDtypeStruct((n_steps, nt), jnp.int32), jax.ShapeDtypeStruct((8, nt), jnp.int32)],
        name="nsa_steps",
    )(flags.T.astype(F32), pairT)
    return lists.T.reshape(-1), counts[0]


def _nsa_main(lists, counts, qT, qaug, ks, vsT, kw, vwT, sel, gT, ocmp, lowb, causb, B, T):
    G = NSA_KV_HEADS
    nq = T // Q_BLOCK
    ns = T // SLC_LEN
    whole = lambda b, g, i, *_: (b, g, 0, 0)
    tile = lambda b, g, i, *_: (b, g, i, 0, 0)
    const = lambda b, g, i, *_: (0, 0)
    grid_spec = pltpu.PrefetchScalarGridSpec(
        num_scalar_prefetch=2,
        grid=(B, G, nq // MAIN_TILES),
        in_specs=[pl.BlockSpec((None, None, MAIN_TILES, HEAD_DIM, GQ), tile),
                  pl.BlockSpec((None, HEAD_DIM, GQ), lambda b, g, i, *_: (g, 0, 0)),
                  pl.BlockSpec((None, None, T, 2 * HEAD_DIM), whole),
                  pl.BlockSpec((None, None, V_ROWS, T), whole),
                  pl.BlockSpec((None, None, T + WIN, 2 * HEAD_DIM), whole),
                  pl.BlockSpec((None, None, V_ROWS, T + WIN), whole),
                  pl.BlockSpec((None, None, MAIN_TILES, ns, Q_BLOCK), tile),
                  pl.BlockSpec((None, MAIN_TILES, 32, Q_BLOCK), lambda b, g, i, *_: (b, i, 0, 0)),
                  pl.BlockSpec((None, None, MAIN_TILES, HEAD_DIM, GQ), tile),
                  pl.BlockSpec((Q_BLOCK, Q_BLOCK), const),
                  pl.BlockSpec((Q_BLOCK, Q_BLOCK), const)],
        out_specs=pl.BlockSpec((None, MAIN_TILES * Q_BLOCK, NSA_GROUP * HEAD_DIM), lambda b, g, i, *_: (b, i, g)),
        scratch_shapes=[pltpu.VMEM((MAIN_TILES, 1, GQ), F32), pltpu.VMEM((MAIN_TILES, V_ROWS, GQ), F32),
                        pltpu.VMEM((MAIN_TILES, HEAD_DIM, GQ), F32)],
    )
    return pl.pallas_call(
        _nsa_main_kernel,
        grid_spec=grid_spec,
        out_shape=jax.ShapeDtypeStruct((B, T, NSA_WIDTH), F32),
        compiler_params=_params(3),
        name="nsa_main",
    )(lists, counts, qT, qaug, ks, vsT, kw, vwT, sel, gT, ocmp, lowb, causb)


def _ret_kernel(p_ref, decay_ref, xi_ref, zeta_ref, gch_ref, ng_ref, out_ref, state_ref):
    @pl.when(pl.program_id(1) == 0)
    def _():
        state_ref[...] = jnp.zeros(state_ref.shape, F32)

    kw = RET_HEADS * RET_DK
    p = p_ref[...]
    rq = p[:, 0:kw] * (RET_DK ** -0.5)
    rk = p[:, kw:2 * kw]
    rkT = rk.T
    rv = p[:, 2 * kw:2 * kw + RET_WIDTH]
    rg = p[:, 2 * kw + RET_WIDTH:2 * kw + 2 * RET_WIDTH]
    xi = xi_ref[...]
    outs = []
    for h in range(RET_HEADS):
        qh = rq[:, h * RET_DK:(h + 1) * RET_DK]
        kh = rk[:, h * RET_DK:(h + 1) * RET_DK]
        khT = rkT[h * RET_DK:(h + 1) * RET_DK, :]
        vh = rv[:, h * RET_DV:(h + 1) * RET_DV]
        st = state_ref[h]
        inner = _dot_nt(qh, kh) * decay_ref[h]
        o = _dot(inner, vh) + _dot(qh, st) * xi[:, h:h + 1]
        state_ref[h] = st * gch_ref[h:h + 1, 0:1] + _dot(khT * zeta_ref[h:h + 1, :], vh)
        outs.append(_rms(o, ng_ref[:, h * RET_DV:(h + 1) * RET_DV]))
    out_ref[...] = jnp.concatenate(outs, axis=1) * (rg * jax.nn.sigmoid(rg))


def _ret_consts():
    H, C = RET_HEADS, RET_CHUNK
    log_g = jnp.log1p(-jnp.exp2(-5.0 - jnp.arange(H, dtype=F32)))
    idx = jnp.arange(C, dtype=F32)
    diff = idx[:, None] - idx[None, :]
    decay = jnp.where(diff >= 0, jnp.exp(jnp.maximum(diff, 0.0) * log_g[:, None, None]), 0.0)
    zeta = jnp.exp((C - 1 - idx) * log_g[:, None])
    xi = jnp.exp((idx + 1) * log_g[:, None]).T
    g_chunk = jnp.broadcast_to(jnp.exp(C * log_g)[:, None], (H, LANES))
    return decay, xi, zeta, g_chunk


def _retention(pret, ng, B, T):
    nch = T // RET_CHUNK
    decay, xi, zeta, gch = _ret_consts()
    c2 = lambda b, c: (0, 0)
    return pl.pallas_call(
        _ret_kernel,
        grid=(B, nch),
        in_specs=[pl.BlockSpec((RET_CHUNK, _RET_COLS), lambda b, c: (b * nch + c, 0)),
                  pl.BlockSpec((RET_HEADS, RET_CHUNK, RET_CHUNK), lambda b, c: (0, 0, 0)),
                  pl.BlockSpec((RET_CHUNK, RET_HEADS), c2),
                  pl.BlockSpec((RET_HEADS, RET_CHUNK), c2),
                  pl.BlockSpec((RET_HEADS, LANES), c2),
                  pl.BlockSpec((1, RET_WIDTH), c2)],
        out_specs=pl.BlockSpec((RET_CHUNK, RET_WIDTH), lambda b, c: (b * nch + c, 0)),
        out_shape=jax.ShapeDtypeStruct((B * T, RET_WIDTH), F32),
        scratch_shapes=[pltpu.VMEM((RET_HEADS, RET_DK, RET_DV), F32)],
        compiler_params=_params(2),
        name="retention",
    )(pret, decay, xi, zeta, gch, ng)


def _outproj_kernel(x_ref, nsa_ref, ret_ref, w_ref, o_ref):
    o_ref[...] = (x_ref[...] + _dot(nsa_ref[...].astype(BF16), w_ref[0:NSA_WIDTH, :])
                  + _dot(ret_ref[...].astype(BF16), w_ref[NSA_WIDTH:, :]))


def _outproj(x2, nsa, ret, w, tm=512):
    n, d = x2.shape
    return pl.pallas_call(
        _outproj_kernel,
        grid=(n // tm,),
        in_specs=[pl.BlockSpec((tm, d), lambda i: (i, 0)),
                  pl.BlockSpec((tm, NSA_WIDTH), lambda i: (i, 0)),
                  pl.BlockSpec((tm, RET_WIDTH), lambda i: (i, 0)),
                  pl.BlockSpec((NSA_WIDTH + RET_WIDTH, d), lambda i: (0, 0))],
        out_specs=pl.BlockSpec((tm, d), lambda i: (i, 0)),
        out_shape=jax.ShapeDtypeStruct((n, d), F32),
        compiler_params=_params(1),
        name="outproj",
    )(x2, nsa, ret, w)


def _ffn_kernel(x_ref, g_ref, wg_ref, wu_ref, wd_ref, o_ref, h_sc):
    f = pl.program_id(1)

    @pl.when(f == 0)
    def _():
        x = x_ref[...]
        h_sc[...] = _rms(x, g_ref[...]).astype(BF16)
        o_ref[...] = x

    h = h_sc[...]
    a = _dot(h, wg_ref[...])
    act = (a * jax.nn.sigmoid(a) * _dot(h, wu_ref[...])).astype(BF16)
    o_ref[...] += _dot(act, wd_ref[...])


def _ffn(x2, g, wg, wu, wd, tm=512, fc=1408):
    n, d = x2.shape
    dff = wg.shape[1]
    return pl.pallas_call(
        _ffn_kernel,
        grid=(n // tm, dff // fc),
        in_specs=[pl.BlockSpec((tm, d), lambda i, f: (i, 0)),
                  pl.BlockSpec((1, d), lambda i, f: (0, 0)),
                  pl.BlockSpec((d, fc), lambda i, f: (0, f)),
                  pl.BlockSpec((d, fc), lambda i, f: (0, f)),
                  pl.BlockSpec((fc, d), lambda i, f: (f, 0))],
        out_specs=pl.BlockSpec((tm, d), lambda i, f: (i, 0)),
        out_shape=jax.ShapeDtypeStruct((n, d), F32),
        scratch_shapes=[pltpu.VMEM((tm, d), BF16)],
        compiler_params=_params(2),
        name="ffn_dense",
    )(x2, g, wg, wu, wd)


def _router_kernel(x_ref, g_ref, r_ref, rb_ref, tri_ref, h_ref, rank_ref, comb_ref, rankT_ref, cnt_ref):
    h = _rms(x_ref[...], g_ref[...])
    h_ref[...] = h.astype(BF16)
    hh, hm, hl = _split3(h)
    rh, rm, rl = _split3(r_ref[...])
    logits = (_dot(hh, rh) + (_dot(hh, rm) + _dot(hm, rh)) + (_dot(hh, rl) + _dot(hm, rm) + _dot(hl, rh))
              + rb_ref[...])
    lane = lax.broadcasted_iota(jnp.int32, logits.shape, 1).astype(F32)
    logits = jnp.where(lane < N_EXPERTS, logits, NEG)
    m1 = jnp.max(logits, axis=1, keepdims=True)
    i1 = jnp.min(jnp.where(logits == m1, lane, float(LANES)), axis=1, keepdims=True)
    l2 = jnp.where(lane == i1, NEG, logits)
    m2 = jnp.max(l2, axis=1, keepdims=True)
    i2 = jnp.min(jnp.where(l2 == m2, lane, float(LANES)), axis=1, keepdims=True)
    e2 = jnp.exp(m2 - m1)
    w1 = 1.0 / (1.0 + e2)
    w2 = e2 / (1.0 + e2)
    use1, use2 = lane == i1, lane == i2
    comb_ref[...] = jnp.where(use1, w1, 0.0) + jnp.where(use2, w2, 0.0)
    use = (use1 | use2).astype(F32)
    rank = jnp.where(use > 0, _dot(tri_ref[...], use.astype(BF16)), -1.0)
    rank_ref[...] = rank
    rankT_ref[...] = rank.T[0:N_EXPERTS, :]
    cnt_ref[...] = jnp.broadcast_to(jnp.sum(use, axis=0, keepdims=True), cnt_ref.shape).astype(jnp.int32)


def _router(x2, g, router, rb, tm):
    n, d = x2.shape
    nt = n // tm
    tri = (jnp.arange(tm)[:, None] > jnp.arange(tm)[None, :]).astype(BF16)
    rpad = jnp.zeros((d, LANES), F32).at[:, :N_EXPERTS].set(router)
    rbpad = jnp.zeros((1, LANES), F32).at[0, :N_EXPERTS].set(rb)
    c2 = lambda i: (0, 0)
    return pl.pallas_call(
        _router_kernel,
        grid=(nt,),
        in_specs=[pl.BlockSpec((tm, d), lambda i: (i, 0)),
                  pl.BlockSpec((1, d), c2),
                  pl.BlockSpec((d, LANES), c2),
                  pl.BlockSpec((1, LANES), c2),
                  pl.BlockSpec((tm, tm), c2)],
        out_specs=[pl.BlockSpec((tm, d), lambda i: (i, 0)),
                   pl.BlockSpec((tm, LANES), lambda i: (i, 0)),
                   pl.BlockSpec((tm, LANES), lambda i: (i, 0)),
                   pl.BlockSpec((N_EXPERTS, tm), lambda i: (0, i)),
                   pl.BlockSpec((None, 8, LANES), lambda i: (i, 0, 0))],
        out_shape=[jax.ShapeDtypeStruct((n, d), BF16),
                   jax.ShapeDtypeStruct((n, LANES), F32),
                   jax.ShapeDtypeStruct((n, LANES), F32),
                   jax.ShapeDtypeStruct((N_EXPERTS, n), F32),
                   jax.ShapeDtypeStruct((nt, 8, LANES), jnp.int32)],
        compiler_params=_params(1),
        name="moe_router",
    )(x2, g, rpad, rbpad, tri)


MOE_SUB = 128
MOE_MOVE = 2 * MOE_SUB


def _moe_kernel(cnt_ref, h_ref, rankT_ref, rank_ref, comb_ref, wg_ref, wu_ref, wd_ref, x_ref, o_ref, hc_sc, oacc_sc):
    t, e, f = pl.program_id(0), pl.program_id(1), pl.program_id(2)
    nf = pl.num_programs(2)
    tm = h_ref.shape[0]
    nsub = (cnt_ref[t * N_EXPERTS + e] + (MOE_SUB - 1)) // MOE_SUB
    nmove = (nsub + 1) // 2

    @pl.when((e == 0) & (f == 0))
    def _():
        o_ref[...] = x_ref[...]

    @pl.when(f == 0)
    def _():
        rank_row = rankT_ref[...]

        def gather(s, c):
            r0 = pl.multiple_of(s * MOE_MOVE, MOE_MOVE)
            rows = (lax.broadcasted_iota(jnp.int32, (MOE_MOVE, 1), 0) + r0).astype(F32)
            onehot = (rows == rank_row).astype(BF16)
            hc_sc[pl.ds(r0, MOE_MOVE), :] = _dot(onehot, h_ref[...]).astype(BF16)
            oacc_sc[pl.ds(r0, MOE_MOVE), :] = jnp.zeros((MOE_MOVE, oacc_sc.shape[1]), F32)
            return c

        lax.fori_loop(0, nmove, gather, 0)

    def expert(n_rows, s, c):
        r0 = pl.multiple_of(s * n_rows, n_rows)
        rows = hc_sc[pl.ds(r0, n_rows), :]
        a = _dot(rows, wg_ref[...])
        act = (a * jax.nn.sigmoid(a) * _dot(rows, wu_ref[...])).astype(BF16)
        oacc_sc[pl.ds(r0, n_rows), :] += _dot(act, wd_ref[...])
        return c

    lax.fori_loop(0, nsub // 2, functools.partial(expert, MOE_MOVE), 0)
    lax.fori_loop(nsub // 2 * 2, nsub, functools.partial(expert, MOE_SUB), 0)

    @pl.when(f == nf - 1)
    def _():
        is_e = lax.broadcasted_iota(jnp.int32, (1, LANES), 1) == e
        rank_col = jnp.sum(jnp.where(is_e, rank_ref[...], 0.0), axis=1, keepdims=True)
        comb_col = jnp.sum(jnp.where(is_e, comb_ref[...], 0.0), axis=1, keepdims=True)

        def scatter(s, c):
            r0 = pl.multiple_of(s * MOE_MOVE, MOE_MOVE)
            cols = (lax.broadcasted_iota(jnp.int32, (1, MOE_MOVE), 1) + r0).astype(F32)
            onehot = (rank_col == cols).astype(BF16)
            y = _dot(onehot, oacc_sc[pl.ds(r0, MOE_MOVE), :].astype(BF16))
            o_ref[...] += comb_col * y
            return c

        lax.fori_loop(0, nmove, scatter, 0)


def _moe(counts, h, rankT, rank, comb, wg, wu, wd, x2, tm, fc=1408):
    n, d = x2.shape
    dff = wg.shape[2]
    grid_spec = pltpu.PrefetchScalarGridSpec(
        num_scalar_prefetch=1,
        grid=(n // tm, N_EXPERTS, dff // fc),
        in_specs=[pl.BlockSpec((tm, d), lambda t, e, f, c: (t, 0)),
                  pl.BlockSpec((None, 1, tm), lambda t, e, f, c: (e, 0, t)),
                  pl.BlockSpec((tm, LANES), lambda t, e, f, c: (t, 0)),
                  pl.BlockSpec((tm, LANES), lambda t, e, f, c: (t, 0)),
                  pl.BlockSpec((None, d, fc), lambda t, e, f, c: (e, 0, f)),
                  pl.BlockSpec((None, d, fc), lambda t, e, f, c: (e, 0, f)),
                  pl.BlockSpec((None, fc, d), lambda t, e, f, c: (e, f, 0)),
                  pl.BlockSpec((tm, d), lambda t, e, f, c: (t, 0))],
        out_specs=pl.BlockSpec((tm, d), lambda t, e, f, c: (t, 0)),
        scratch_shapes=[pltpu.VMEM((tm, d), BF16), pltpu.VMEM((tm, d), F32)],
    )
    return pl.pallas_call(
        _moe_kernel,
        grid_spec=grid_spec,
        out_shape=jax.ShapeDtypeStruct((n, d), F32),
        compiler_params=_params(3),
        name="moe_experts",
    )(counts, h, rankT.reshape(N_EXPERTS, 1, n), rank, comb, wg, wu, wd, x2)


def _permute_w_in(w):
    o = np.cumsum((0, NSA_WIDTH) + (KV_WIDTH,) * 6 + (3 * NSA_HEADS,))
    q, kc, vc, ks, vs, kw, vw, gts = (w[:, o[k]:o[k + 1]] for k in range(8))
    ret = w[:, o[8]:]
    pad = jnp.zeros((w.shape[0], LANES - 3 * NSA_HEADS), w.dtype)
    return jnp.concatenate([q, ks, kw, vs, vw, kc, vc, gts, pad, ret], axis=1).astype(BF16)


def _nsa_consts(T):
    ncp = T // CMP_STRIDE
    ns = T // SLC_LEN
    cs = np.arange(ncp) * CMP_STRIDE
    ss = np.arange(ns) * SLC_LEN
    ov = np.clip(np.minimum(cs[None, :] + CMP_LEN, ss[:, None] + SLC_LEN) - np.maximum(cs[None, :], ss[:, None]), 0, None)
    ovT = (ov.astype(np.float32) / CMP_LEN)
    ovT[:, ncp - 1] = 0.0
    h = np.arange(NSA_HEADS).reshape(NSA_KV_HEADS, NSA_GROUP) + 1
    slopes = np.exp2(-8.0 * h / NSA_HEADS).astype(np.float32)
    slopes = np.repeat(slopes, Q_BLOCK, axis=1)
    parts, rest = [], np.float64(LOG2E)
    for _ in range(3):
        part = np.float64(np.asarray(rest).astype(BF16))
        parts.append(part)
        rest = rest - part
    qaug = np.zeros((NSA_KV_HEADS, HEAD_DIM, GQ), np.float32)
    for k, part in enumerate(parts):
        qaug[:, k, :] = part * SLC_LEN * slopes
        qaug[:, 3 + k, :] = part * slopes
    kq = np.arange(Q_BLOCK)[:, None] - np.arange(Q_BLOCK)[None, :]
    causb = np.where(kq <= 0, 0.0, NEG).astype(np.float32)
    lowb = np.where(kq > 0, 0.0, NEG).astype(np.float32)
    return jnp.asarray(ovT, BF16), jnp.asarray(qaug, BF16), jnp.asarray(lowb), jnp.asarray(causb)


def _mixer(x2, B, T, norm_g, w_in, q_norm_g, k_norm_g, cmp_pos, w_cmp, ret_norm_g, w_out):
    ns = T // SLC_LEN
    q, kv, kc, vc, gt, pret = _inproj(x2, norm_g[None, :], _permute_w_in(w_in))
    qT, ks, kw, vsT, vwT, gT = _prep(q, kv, gt, q_norm_g[None, :], k_norm_g[1:3], B, T)
    wk, pk = _compress_weights(w_cmp[0], cmp_pos[0])
    wv, pv = _compress_weights(w_cmp[1], cmp_pos[1])
    kcmp, vcT = _compress(kc, vc, wk, wv, pk, pv, k_norm_g[0:1], B, T)
    ovT, qaug, lowb, causb = _nsa_consts(T)
    ocmp, sel, flags = _nsa_cmp(qT, qaug, kcmp, vcT, ovT, B, T)
    lists, counts = _nsa_steps(flags[:, :, :, 0, :].reshape(-1, ns), T // Q_BLOCK)
    kpad = jnp.zeros((WIN, 2 * HEAD_DIM), BF16).at[:, HEAD_DIM:HEAD_DIM + 3].set(-2.0 ** 100)
    kw = jnp.concatenate([jnp.broadcast_to(kpad, kw.shape[:2] + kpad.shape), kw], axis=2)
    vwT = jnp.pad(vwT, ((0, 0), (0, 0), (0, 0), (WIN, 0)))
    nsa = _nsa_main(lists, counts, qT, qaug, ks, vsT, kw, vwT, sel, gT, ocmp, lowb, causb, B, T)
    ret = _retention(pret, ret_norm_g[None, :], B, T)
    return _outproj(x2, nsa.reshape(B * T, NSA_WIDTH), ret, w_out.astype(BF16))


def _moe_layer(x2, norm_g, router, router_b, wg, wu, wd, tm=1024):
    tm = min(tm, x2.shape[0])
    h, rank, comb, rankT, cnt = _router(x2, norm_g[None, :], router, router_b, tm)
    counts = cnt[:, 0, :N_EXPERTS].reshape(-1)
    return _moe(counts, h, rankT, rank, comb, wg.astype(BF16), wu.astype(BF16), wd.astype(BF16), x2, tm)


def kernel(x, norm_mix_g, w_in, q_norm_g, k_norm_g, cmp_pos, w_cmp, ret_norm_g, w_out, norm_ffn_g,
           ffn_w_gate, ffn_w_up, ffn_w_down, moe_router, moe_router_b, moe_w_gate, moe_w_up, moe_w_down):
    B, T, D = x.shape
    depth = norm_mix_g.shape[0]
    x2 = x.reshape(B * T, D)
    for l in range(depth):
        x2 = _mixer(x2, B, T, norm_mix_g[l], w_in[l], q_norm_g[l], k_norm_g[l], cmp_pos[l], w_cmp[l],
                    ret_norm_g[l], w_out[l])
        j = l // 2
        if l % 2 == 0:
            x2 = _ffn(x2, norm_ffn_g[l][None, :], ffn_w_gate[j].astype(BF16), ffn_w_up[j].astype(BF16),
                      ffn_w_down[j].astype(BF16))
        else:
            x2 = _moe_layer(x2, norm_ffn_g[l], moe_router[j], moe_router_b[j], moe_w_gate[j], moe_w_up[j],
                            moe_w_down[j])
    return x2.reshape(B, T, D)
```

```python
import functools

import numpy as np
import jax
import jax.numpy as jnp
from jax import lax
from jax.experimental import pallas as pl
from jax.experimental.pallas import tpu as pltpu

F32 = jnp.float32
BF16 = jnp.bfloat16

HEAD_DIM = 64
NSA_HEADS = 8
NSA_KV_HEADS = 2
NSA_GROUP = NSA_HEADS // NSA_KV_HEADS
RET_HEADS = 8
RET_DK = 32
RET_DV = 64
NSA_WIDTH = NSA_HEADS * HEAD_DIM
RET_WIDTH = RET_HEADS * RET_DV
KV_WIDTH = NSA_KV_HEADS * HEAD_DIM
CMP_LEN = 32
CMP_STRIDE = 16
SLC_LEN = 64
SLC_TOPK = 16
WIN = 512
Q_BLOCK = 128
RET_CHUNK = 128
N_EXPERTS = 8
EPS = 1e-6
NEG = -1e30
BIG = 1e9
LANES = 128
GQ = NSA_GROUP * Q_BLOCK
KEY_STEP = 128
STEP_GROUP = 4
N_FORCED = 3
CMP_CHUNK = 128
CMP_TILES = 4
PREP_TILES = 4
MAIN_TILES = 4
LOOP_TILES = 2
CMP_TAIL = CMP_CHUNK + 8
WIN_KEYS = WIN + Q_BLOCK
V_ROWS = HEAD_DIM + 16
LOG2E = 1.4426950408889634
VMEM_LIMIT = 60 * 1024 * 1024

_C_Q = 0
_C_KV = _C_Q + NSA_WIDTH
_C_KC = _C_KV + 4 * KV_WIDTH
_C_VC = _C_KC + KV_WIDTH
_C_GT = _C_VC + KV_WIDTH
_C_RET = _C_GT + LANES
_RET_COLS = 2 * RET_HEADS * RET_DK + 2 * RET_WIDTH
_C_END = _C_RET + _RET_COLS


def _params(n_axes, vmem=VMEM_LIMIT):
    return pltpu.CompilerParams(dimension_semantics=("arbitrary",) * n_axes, vmem_limit_bytes=vmem)


def _dot(a, b):
    return jnp.dot(a, b, preferred_element_type=F32)


def _dot_nt(a, b):
    return lax.dot_general(a, b, (((1,), (1,)), ((), ())), preferred_element_type=F32)


def _rms(x, g):
    return x * lax.rsqrt(jnp.mean(x * x, axis=-1, keepdims=True) + EPS) * g


def _inproj_kernel(x_ref, g_ref, w_ref, q_ref, kv_ref, kc_ref, vc_ref, gt_ref, ret_ref):
    h = _rms(x_ref[...], g_ref[...]).astype(BF16)
    q_ref[...] = _dot(h, w_ref[:, _C_Q:_C_KV])
    kv_ref[...] = _dot(h, w_ref[:, _C_KV:_C_KC])
    kc_ref[...] = _dot(h, w_ref[:, _C_KC:_C_VC])
    vc_ref[...] = _dot(h, w_ref[:, _C_VC:_C_GT])
    gt_ref[...] = _dot(h, w_ref[:, _C_GT:_C_RET])
    ret_ref[...] = _dot(h, w_ref[:, _C_RET:_C_END])


def _inproj(x2, g, w, tm=512):
    n, d = x2.shape
    widths = (NSA_WIDTH, 4 * KV_WIDTH, KV_WIDTH, KV_WIDTH, LANES, _RET_COLS)
    return pl.pallas_call(
        _inproj_kernel,
        grid=(n // tm,),
        in_specs=[pl.BlockSpec((tm, d), lambda i: (i, 0)),
                  pl.BlockSpec((1, d), lambda i: (0, 0)),
                  pl.BlockSpec((d, _C_END), lambda i: (0, 0))],
        out_specs=[pl.BlockSpec((tm, c), lambda i: (i, 0)) for c in widths],
        out_shape=[jax.ShapeDtypeStruct((n, c), F32) for c in widths],
        compiler_params=_params(1),
        name="inproj",
    )(x2, g, w)


def _group_rms(x, g, ones_ref):
    w = x.shape[1]
    ones = ones_ref[0:w, 0:w]
    hi, mid, lo = _split3(x * x)
    ms = (_dot(hi, ones) + _dot(mid, ones) + _dot(lo, ones)) * (1.0 / HEAD_DIM)
    return x * lax.rsqrt(ms + EPS) * g


def _group_ones():
    lane = np.arange(NSA_WIDTH) // HEAD_DIM
    return jnp.asarray(lane[:, None] == lane[None, :], BF16)


def _prep_kernel(q_ref, kv_ref, gt_ref, qg_ref, kg_ref, ones_ref, qT_ref, ks_ref, kw_ref, vsT_ref, vwT_ref, gT_ref):
    tiles = range(PREP_TILES)
    rows = [slice(u * Q_BLOCK, (u + 1) * Q_BLOCK) for u in tiles]
    n_tok = PREP_TILES * Q_BLOCK
    scale = HEAD_DIM ** -0.5 * LOG2E
    qn = _group_rms(q_ref[...], qg_ref[...], ones_ref) * scale
    qt = [qn[rows[u]].T for u in tiles]
    for g in range(NSA_KV_HEADS):
        for r in range(NSA_GROUP):
            h = g * NSA_GROUP + r
            for u in tiles:
                qT_ref[g, u, :, r * Q_BLOCK:(r + 1) * Q_BLOCK] = qt[u][h * HEAD_DIM:(h + 1) * HEAD_DIM, :].astype(BF16)
    kv = kv_ref[...]
    vst = [kv[rows[u], 2 * KV_WIDTH:3 * KV_WIDTH].T for u in tiles]
    vwt = [kv[rows[u], 3 * KV_WIDTH:4 * KV_WIDTH].T for u in tiles]
    gts = [jax.nn.sigmoid(gt_ref[rows[u], :].T[0:32, :]) for u in tiles]
    pos = pl.program_id(1) * n_tok + lax.broadcasted_iota(jnp.int32, (n_tok, HEAD_DIM), 0)
    col = lax.broadcasted_iota(jnp.int32, (n_tok, HEAD_DIM), 1)
    kpos = jnp.where(col < 3, pos // SLC_LEN, jnp.where(col < 6, pos % SLC_LEN, 0)).astype(F32)
    ones_row = (lax.broadcasted_iota(jnp.int32, (V_ROWS - HEAD_DIM, Q_BLOCK), 0) == 0).astype(F32)
    ks = _group_rms(kv[:, 0:KV_WIDTH], kg_ref[0:1, :], ones_ref)
    kw = _group_rms(kv[:, KV_WIDTH:2 * KV_WIDTH], kg_ref[1:2, :], ones_ref)
    for g in range(NSA_KV_HEADS):
        sl = slice(g * HEAD_DIM, (g + 1) * HEAD_DIM)
        ks_ref[g] = jnp.concatenate([ks[:, sl], kpos], axis=1).astype(BF16)
        kw_ref[g] = jnp.concatenate([kw[:, sl], kpos], axis=1).astype(BF16)
        for u in tiles:
            vsT_ref[g, :, rows[u]] = jnp.concatenate([vst[u][sl, :], ones_row], axis=0).astype(BF16)
            vwT_ref[g, :, rows[u]] = jnp.concatenate([vwt[u][sl, :], ones_row], axis=0).astype(BF16)
    for u in tiles:
        gT_ref[u] = gts[u]


def _prep(q, kv, gt, qg, kg, B, T):
    nq = T // Q_BLOCK
    G = NSA_KV_HEADS
    n_tok = PREP_TILES * Q_BLOCK
    steps = nq // PREP_TILES
    row = lambda b, i: (b * steps + i, 0)
    return pl.pallas_call(
        _prep_kernel,
        grid=(B, steps),
        in_specs=[pl.BlockSpec((n_tok, NSA_WIDTH), row),
                  pl.BlockSpec((n_tok, 4 * KV_WIDTH), row),
                  pl.BlockSpec((n_tok, LANES), row),
                  pl.BlockSpec((1, NSA_WIDTH), lambda b, i: (0, 0)),
                  pl.BlockSpec((2, KV_WIDTH), lambda b, i: (0, 0)),
                  pl.BlockSpec((NSA_WIDTH, NSA_WIDTH), lambda b, i: (0, 0))],
        out_specs=[pl.BlockSpec((None, G, PREP_TILES, HEAD_DIM, GQ), lambda b, i: (b, 0, i, 0, 0)),
                   pl.BlockSpec((None, G, n_tok, 2 * HEAD_DIM), lambda b, i: (b, 0, i, 0)),
                   pl.BlockSpec((None, G, n_tok, 2 * HEAD_DIM), lambda b, i: (b, 0, i, 0)),
                   pl.BlockSpec((None, G, V_ROWS, n_tok), lambda b, i: (b, 0, 0, i)),
                   pl.BlockSpec((None, G, V_ROWS, n_tok), lambda b, i: (b, 0, 0, i)),
                   pl.BlockSpec((None, PREP_TILES, 32, Q_BLOCK), lambda b, i: (b, i, 0, 0))],
        out_shape=[jax.ShapeDtypeStruct((B, G, nq, HEAD_DIM, GQ), BF16),
                   jax.ShapeDtypeStruct((B, G, T, 2 * HEAD_DIM), BF16),
                   jax.ShapeDtypeStruct((B, G, T, 2 * HEAD_DIM), BF16),
                   jax.ShapeDtypeStruct((B, G, V_ROWS, T), BF16),
                   jax.ShapeDtypeStruct((B, G, V_ROWS, T), BF16),
                   jax.ShapeDtypeStruct((B, nq, 32, Q_BLOCK), F32)],
        compiler_params=_params(2),
        name="nsa_prep",
    )(q, kv, gt, jnp.tile(qg, (1, NSA_HEADS)), jnp.tile(kg, (1, NSA_KV_HEADS)), _group_ones())


def _compress_kernel(kc_ref, vc_ref, wk_ref, wv_ref, pk_ref, pv_ref, kg_ref, kcmp_ref, vcT_ref):
    ncp = kc_ref.shape[0] // CMP_STRIDE

    def comp(a_ref, w_ref, p_ref):
        lo = jnp.zeros((ncp, KV_WIDTH), F32)
        hi = jnp.zeros((ncp, KV_WIDTH), F32)
        for l in range(CMP_STRIDE):
            a = a_ref[pl.ds(l, ncp, stride=CMP_STRIDE), :]
            lo += _dot((a + p_ref[0, l:l + 1, :]).astype(BF16), w_ref[0, l])
            hi += _dot((a + p_ref[1, l:l + 1, :]).astype(BF16), w_ref[1, l])
        return lo + pltpu.roll(hi, ncp - 1, 0)

    k = comp(kc_ref, wk_ref, pk_ref)
    v = comp(vc_ref, wv_ref, pv_ref).T
    cend = lax.broadcasted_iota(jnp.int32, (ncp, HEAD_DIM), 0) * CMP_STRIDE + (CMP_LEN - 1)
    col = lax.broadcasted_iota(jnp.int32, (ncp, HEAD_DIM), 1)
    kpos = jnp.where(col < 3, cend // SLC_LEN, jnp.where(col < 6, cend % SLC_LEN, 0)).astype(F32)
    for g in range(NSA_KV_HEADS):
        sl = slice(g * HEAD_DIM, (g + 1) * HEAD_DIM)
        kcmp_ref[g] = jnp.concatenate([_rms(k[:, sl], kg_ref[...]), kpos], axis=1).astype(BF16)
        vcT_ref[g] = v[sl, :].astype(BF16)


def _compress(kc, vc, wk, wv, pk, pv, kg, B, T):
    ncp = T // CMP_STRIDE
    G = NSA_KV_HEADS
    const4 = lambda b: (0, 0, 0, 0)
    const3 = lambda b: (0, 0, 0)
    const2 = lambda b: (0, 0)
    return pl.pallas_call(
        _compress_kernel,
        grid=(B,),
        in_specs=[pl.BlockSpec((T, KV_WIDTH), lambda b: (b, 0)),
                  pl.BlockSpec((T, KV_WIDTH), lambda b: (b, 0)),
                  pl.BlockSpec((2, CMP_STRIDE, KV_WIDTH, KV_WIDTH), const4),
                  pl.BlockSpec((2, CMP_STRIDE, KV_WIDTH, KV_WIDTH), const4),
                  pl.BlockSpec((2, CMP_STRIDE, KV_WIDTH), const3),
                  pl.BlockSpec((2, CMP_STRIDE, KV_WIDTH), const3),
                  pl.BlockSpec((1, HEAD_DIM), const2)],
        out_specs=[pl.BlockSpec((None, G, ncp, 2 * HEAD_DIM), lambda b: (b, 0, 0, 0)),
                   pl.BlockSpec((None, G, HEAD_DIM, ncp), lambda b: (b, 0, 0, 0))],
        out_shape=[jax.ShapeDtypeStruct((B, G, ncp, 2 * HEAD_DIM), BF16),
                   jax.ShapeDtypeStruct((B, G, HEAD_DIM, ncp), BF16)],
        compiler_params=_params(1),
        name="nsa_compress",
    )(kc, vc, wk, wv, pk, pv, kg)


def _compress_weights(w, pos):
    G = NSA_KV_HEADS
    w4 = w.reshape(2, CMP_STRIDE, HEAD_DIM, HEAD_DIM)
    eye = jnp.eye(G, dtype=w.dtype)
    wbd = jnp.einsum('hlde,gk->hlgdke', w4, eye).reshape(2, CMP_STRIDE, KV_WIDTH, KV_WIDTH)
    p = pos.reshape(2, CMP_STRIDE, 1, HEAD_DIM)
    p = jnp.broadcast_to(p, (2, CMP_STRIDE, G, HEAD_DIM)).reshape(2, CMP_STRIDE, KV_WIDTH)
    return wbd.astype(BF16), p


def _split3(x):
    hi = x.astype(BF16)
    r = x - hi.astype(F32)
    mid = r.astype(BF16)
    lo = (r - mid.astype(F32)).astype(BF16)
    return hi, mid, lo


def _nsa_cmp_kernel(qT_ref, qaug_ref, kc_ref, vcT_ref, ovT_ref, ocmp_ref, sel_ref, flag_ref, *, n_sel):
    ncp = kc_ref.shape[0]
    ns = ovT_ref.shape[0]
    tiles = range(CMP_TILES)
    i0 = pl.program_id(2) * CMP_TILES
    lane = lax.broadcasted_iota(jnp.int32, (1, GQ), 1)
    q = [jnp.concatenate([qT_ref[u], qaug_ref[...]], axis=0) for u in tiles]
    t_row = [(i0 + u) * Q_BLOCK + (lane & (Q_BLOCK - 1)) for u in tiles]
    has_cmp = [(t_row[u] >= CMP_LEN - 1).astype(F32) for u in tiles]
    tq = [(i0 + u) * Q_BLOCK + lax.broadcasted_iota(jnp.int32, (1, Q_BLOCK), 1) for u in tiles]
    cur = [tq[u] // SLC_LEN for u in tiles]

    def prefix(rows):
        nsk = rows * CMP_STRIDE // SLC_LEN
        tail0 = max(rows - CMP_TAIL, 0)
        kc = kc_ref[0:rows, :]
        s = [_dot(kc, q[u]) for u in tiles]
        cend = (lax.broadcasted_iota(jnp.int32, (rows - tail0, 1), 0) + tail0) * CMP_STRIDE + (CMP_LEN - 1)
        tail = [jnp.where(t_row[u] >= cend, s[u][tail0:], NEG) for u in tiles]
        s = [jnp.concatenate([s[u][0:tail0], tail[u]], axis=0) if tail0 else tail[u] for u in tiles]
        m = [jnp.max(s[u], axis=0, keepdims=True) for u in tiles]
        e = [jnp.exp2(s[u] - m[u]) for u in tiles]
        p = [e[u] * (has_cmp[u] / jnp.sum(e[u], axis=0, keepdims=True)) for u in tiles]
        vc = vcT_ref[:, 0:rows]
        for u in tiles:
            ocmp_ref[u] = _dot(vc, p[u].astype(BF16))

        ps = [p[u][:, 0:Q_BLOCK] for u in tiles]
        for r in range(1, NSA_GROUP):
            ps = [ps[u] + p[u][:, r * Q_BLOCK:(r + 1) * Q_BLOCK] for u in tiles]
        ov = ovT_ref[0:nsk, 0:rows]
        split = [_split3(ps[u]) for u in tiles]
        imp = [_dot(ov, split[u][0]) + _dot(ov, split[u][1]) + _dot(ov, split[u][2]) for u in tiles]

        blk = lax.broadcasted_iota(jnp.int32, (nsk, 1), 0)
        forced = [(blk == 0) | (blk == cur[u]) | (blk == cur[u] - 1) for u in tiles]
        valid = [blk * SLC_LEN <= tq[u] for u in tiles]
        imp = [jnp.where(forced[u], -3e38, jnp.where(valid[u], imp[u], -BIG)) for u in tiles]
        blk_f = blk.astype(F32)
        sel = [forced[u].astype(F32) for u in tiles]
        for _ in range(n_sel - N_FORCED):
            mx = [jnp.max(imp[u], axis=0, keepdims=True) for u in tiles]
            idx = [jnp.min(jnp.where(imp[u] == mx[u], blk_f, float(ns)), axis=0, keepdims=True) for u in tiles]
            pick = [blk_f == idx[u] for u in tiles]
            sel = [jnp.where(pick[u], 1.0, sel[u]) for u in tiles]
            imp = [jnp.where(pick[u], -3e38, imp[u]) for u in tiles]
        ones = jnp.ones((8, Q_BLOCK), BF16)
        for u in tiles:
            sel_ref[u, 0:nsk, :] = sel[u]
            cnt = _dot_nt(ones, sel[u].astype(BF16))
            flag_ref[u, :, 0:nsk] = (cnt > 0).astype(jnp.int32)
            if nsk < ns:
                sel_ref[u, nsk:, :] = jnp.zeros((ns - nsk, Q_BLOCK), F32)
                flag_ref[u, :, nsk:] = jnp.zeros((8, ns - nsk), jnp.int32)

    n_variants = ncp // CMP_CHUNK
    last = i0 + CMP_TILES - 1
    variant = (last * (Q_BLOCK // CMP_STRIDE) + (Q_BLOCK // CMP_STRIDE - 2)) // CMP_CHUNK
    for k in range(n_variants):
        pl.when(variant == k)(functools.partial(prefix, (k + 1) * CMP_CHUNK))


def _nsa_cmp(qT, qaug, kcmp, vcT, ovT, B, T):
    G = NSA_KV_HEADS
    nq = T // Q_BLOCK
    ncp = T // CMP_STRIDE
    ns = T // SLC_LEN
    n_sel = min(SLC_TOPK, ns)
    assert ncp % CMP_CHUNK == 0 and n_sel > N_FORCED and nq % CMP_TILES == 0
    tile = lambda b, g, i: (b, g, i, 0, 0)
    return pl.pallas_call(
        functools.partial(_nsa_cmp_kernel, n_sel=n_sel),
        grid=(B, G, nq // CMP_TILES),
        in_specs=[pl.BlockSpec((None, None, CMP_TILES, HEAD_DIM, GQ), tile),
                  pl.BlockSpec((None, HEAD_DIM, GQ), lambda b, g, i: (g, 0, 0)),
                  pl.BlockSpec((None, None, ncp, 2 * HEAD_DIM), lambda b, g, i: (b, g, 0, 0)),
                  pl.BlockSpec((None, None, HEAD_DIM, ncp), lambda b, g, i: (b, g, 0, 0)),
                  pl.BlockSpec((ns, ncp), lambda b, g, i: (0, 0))],
        out_specs=[pl.BlockSpec((None, None, CMP_TILES, HEAD_DIM, GQ), tile),
                   pl.BlockSpec((None, None, CMP_TILES, ns, Q_BLOCK), tile),
                   pl.BlockSpec((None, None, CMP_TILES, 8, ns), tile)],
        out_shape=[jax.ShapeDtypeStruct((B, G, nq, HEAD_DIM, GQ), F32),
                   jax.ShapeDtypeStruct((B, G, nq, ns, Q_BLOCK), F32),
                   jax.ShapeDtypeStruct((B, G, nq, 8, ns), jnp.int32)],
        compiler_params=_params(3),
        name="nsa_cmp",
    )(qT, qaug, kcmp, vcT, ovT)


def _nsa_main_kernel(list_ref, cnt_ref, qT_ref, qaug_ref, ks_ref, vsT_ref, kw_ref, vwT_ref, sel_ref, gT_ref, ocmp_ref,
                     lowb_ref, causb_ref, out_ref, m_sc, acc_sc, win_sc):
    b, g = pl.program_id(0), pl.program_id(1)
    tiles = range(MAIN_TILES)
    i = [pl.program_id(2) * MAIN_TILES + u for u in tiles]
    tile_id = [(b * pl.num_programs(1) + g) * (pl.num_programs(2) * MAIN_TILES) + i[u] for u in tiles]
    n_steps = sel_ref.shape[1] // 2
    q = [jnp.concatenate([qT_ref[u], qaug_ref[...]], axis=0) for u in tiles]
    k0 = [pl.multiple_of(i[u] * Q_BLOCK, Q_BLOCK) for u in tiles]

    def sel_bias(u, j, valid):
        def row(r):
            picked = (sel_ref[u, pl.ds(r, 1), :] > 0.5) & valid
            return jnp.concatenate([jnp.where(picked, 0.0, NEG)] * NSA_GROUP, axis=1)
        return row(2 * j), row(2 * j + 1)

    def add_sel_bias(s, ba, bb):
        return jnp.concatenate([s[0:SLC_LEN] + ba, s[SLC_LEN:] + bb], axis=0)

    lowb = jnp.concatenate([lowb_ref[...]] * NSA_GROUP, axis=1)
    causb = jnp.concatenate([causb_ref[...]] * NSA_GROUP, axis=1)

    bias_d = [sel_bias(u, i[u], True) for u in tiles]
    sd = [_dot(ks_ref[pl.ds(k0[u], KEY_STEP), :], q[u]) for u in tiles]
    sw = [_dot(kw_ref[pl.ds(k0[u], WIN_KEYS), :], q[u]) for u in tiles]
    sd = [add_sel_bias(sd[u], *bias_d[u]) + causb for u in tiles]
    sw = [jnp.concatenate([sw[u][0:Q_BLOCK] + lowb, sw[u][Q_BLOCK:WIN], sw[u][WIN:] + causb], axis=0) for u in tiles]
    md = [jnp.max(sd[u], axis=0, keepdims=True) for u in tiles]
    mw = [jnp.max(sw[u], axis=0, keepdims=True) for u in tiles]
    accd = [_dot(vsT_ref[:, pl.ds(k0[u], KEY_STEP)], jnp.exp2((sd[u] - md[u]).astype(BF16))) for u in tiles]
    ow = [_dot(vwT_ref[:, pl.ds(k0[u], WIN_KEYS)], jnp.exp2((sw[u] - mw[u]).astype(BF16))) for u in tiles]
    for u in tiles:
        m_sc[u] = md[u]
        acc_sc[u] = accd[u]
        win_sc[u] = ow[u][0:HEAD_DIM] / ow[u][HEAD_DIM:HEAD_DIM + 1]

    def scores(u, t):
        ks, vs, biases = [], [], []
        for x in range(STEP_GROUP):
            j = list_ref[tile_id[u] * n_steps + t * STEP_GROUP + x]
            valid = j >= 0
            j = jnp.maximum(j, 0)
            kj = pl.multiple_of(j * KEY_STEP, KEY_STEP)
            ks.append(ks_ref[pl.ds(kj, KEY_STEP), :])
            vs.append(vsT_ref[:, pl.ds(kj, KEY_STEP)])
            biases.append(sel_bias(u, j, valid))
        s = _dot(jnp.concatenate(ks, axis=0), q[u])
        s = jnp.concatenate([add_sel_bias(s[x * KEY_STEP:(x + 1) * KEY_STEP], *biases[x])
                             for x in range(STEP_GROUP)], axis=0)
        return s, jnp.max(s, axis=0, keepdims=True), jnp.concatenate(vs, axis=1)

    def accumulate(u, s, smax, vcat):
        m_old = m_sc[u]
        m_new = jnp.maximum(m_old, smax)
        alpha = jnp.exp2(m_old - m_new)
        acc_sc[u] = alpha * acc_sc[u] + _dot(vcat, jnp.exp2((s - m_new).astype(BF16)))
        m_sc[u] = m_new

    def run(work, t, carry):
        staged = [(u, scores(u, t * mult + off)) for (u, mult, off) in work]
        for u, args in staged:
            accumulate(u, *args)
        return carry

    for u0 in range(0, MAIN_TILES, LOOP_TILES):
        us = range(u0, u0 + LOOP_TILES)
        n_groups = functools.reduce(
            jnp.maximum, [(cnt_ref[tile_id[u]] + (STEP_GROUP - 1)) // STEP_GROUP for u in us])
        lax.fori_loop(0, n_groups // 2, functools.partial(run, [(u, 2, off) for off in (0, 1) for u in us]), 0)
        lax.fori_loop(n_groups // 2 * 2, n_groups, functools.partial(run, [(u, 1, 0) for u in us]), 0)

    def gate(u, k):
        rows = [gT_ref[u, pl.ds(g * (NSA_GROUP * 3) + r * 3 + k, 1), :] for r in range(NSA_GROUP)]
        return jnp.concatenate(rows, axis=1)

    o_slc = [acc_sc[u, 0:HEAD_DIM, :] / acc_sc[u, HEAD_DIM:HEAD_DIM + 1, :] for u in tiles]
    o = [gate(u, 0) * ocmp_ref[u] + gate(u, 1) * o_slc[u] + gate(u, 2) * win_sc[u] for u in tiles]
    o = [jnp.concatenate([o[u], jnp.zeros_like(o[u])], axis=0) for u in tiles]
    for r in range(NSA_GROUP):
        ot = [o[u][:, r * Q_BLOCK:(r + 1) * Q_BLOCK].T[:, 0:HEAD_DIM] for u in tiles]
        for u in tiles:
            out_ref[u * Q_BLOCK:(u + 1) * Q_BLOCK, r * HEAD_DIM:(r + 1) * HEAD_DIM] = ot[u]


def _nsa_steps_kernel(flagT_ref, pairT_ref, list_ref, cnt_ref, *, nq):
    n_steps, nt = list_ref.shape
    need = _dot(pairT_ref[...], flagT_ref[...].astype(BF16)) > 0
    step = lax.broadcasted_iota(jnp.int32, (n_steps, 1), 0)
    own = lax.broadcasted_iota(jnp.int32, (1, nt), 1) % nq
    need = need & (step < own)
    need_f = need.astype(F32)
    earlier = (lax.broadcasted_iota(jnp.int32, (n_steps, n_steps), 1) < step).astype(BF16)
    slot = _dot(earlier, need_f.astype(BF16))
    total = jnp.sum(need_f, axis=0, keepdims=True)
    cnt_ref[...] = jnp.broadcast_to(total, cnt_ref.shape).astype(jnp.int32)
    step_f = step.astype(F32)
    for p in range(n_steps):
        val = jnp.sum(jnp.where(need & (slot == p), step_f, 0.0), axis=0, keepdims=True)
        list_ref[p:p + 1, :] = jnp.where(total > p, val, -1.0).astype(jnp.int32)


def _nsa_steps(flags, nq):
    nt, ns = flags.shape
    n_steps = ns // 2
    pairT = jnp.asarray(np.arange(n_steps)[:, None] == np.arange(ns)[None, :] // 2, BF16)
    lists, counts = pl.pallas_call(
        functools.partial(_nsa_steps_kernel, nq=nq),
        out_shape=[jax.ShapeDtypeStruct((n_steps, nt), jnp.int32), jax.ShapeDtypeStruct((8, nt), jnp.int32)],
        name="nsa_steps",
    )(flags.T.astype(F32), pairT)
    return lists.T.reshape(-1), counts[0]


def _nsa_main(lists, counts, qT, qaug, ks, vsT, kw, vwT, sel, gT, ocmp, lowb, causb, B, T):
    G = NSA_KV_HEADS
    nq = T // Q_BLOCK
    ns = T // SLC_LEN
    whole = lambda b, g, i, *_: (b, g, 0, 0)
    tile = lambda b, g, i, *_: (b, g, i, 0, 0)
    const = lambda b, g, i, *_: (0, 0)
    grid_spec = pltpu.PrefetchScalarGridSpec(
        num_scalar_prefetch=2,
        grid=(B, G, nq // MAIN_TILES),
        in_specs=[pl.BlockSpec((None, None, MAIN_TILES, HEAD_DIM, GQ), tile),
                  pl.BlockSpec((None, HEAD_DIM, GQ), lambda b, g, i, *_: (g, 0, 0)),
                  pl.BlockSpec((None, None, T, 2 * HEAD_DIM), whole),
                  pl.BlockSpec((None, None, V_ROWS, T), whole),
                  pl.BlockSpec((None, None, T + WIN, 2 * HEAD_DIM), whole),
                  pl.BlockSpec((None, None, V_ROWS, T + WIN), whole),
                  pl.BlockSpec((None, None, MAIN_TILES, ns, Q_BLOCK), tile),
                  pl.BlockSpec((None, MAIN_TILES, 32, Q_BLOCK), lambda b, g, i, *_: (b, i, 0, 0)),
                  pl.BlockSpec((None, None, MAIN_TILES, HEAD_DIM, GQ), tile),
                  pl.BlockSpec((Q_BLOCK, Q_BLOCK), const),
                  pl.BlockSpec((Q_BLOCK, Q_BLOCK), const)],
        out_specs=pl.BlockSpec((None, MAIN_TILES * Q_BLOCK, NSA_GROUP * HEAD_DIM), lambda b, g, i, *_: (b, i, g)),
        scratch_shapes=[pltpu.VMEM((MAIN_TILES, 1, GQ), F32), pltpu.VMEM((MAIN_TILES, V_ROWS, GQ), F32),
                        pltpu.VMEM((MAIN_TILES, HEAD_DIM, GQ), F32)],
    )
    return pl.pallas_call(
        _nsa_main_kernel,
        grid_spec=grid_spec,
        out_shape=jax.ShapeDtypeStruct((B, T, NSA_WIDTH), F32),
        compiler_params=_params(3),
        name="nsa_main",
    )(lists, counts, qT, qaug, ks, vsT, kw, vwT, sel, gT, ocmp, lowb, causb)


def _ret_kernel(p_ref, decay_ref, xi_ref, zeta_ref, gch_ref, ng_ref, ones_ref, out_ref, state_ref):
    @pl.when(pl.program_id(0) == 0)
    def _():
        state_ref[...] = jnp.zeros(state_ref.shape, F32)

    rows = range(p_ref.shape[0])
    kw = RET_HEADS * RET_DK
    p = [p_ref[b] for b in rows]
    rq = [p[b][:, 0:kw] * (RET_DK ** -0.5) for b in rows]
    rk = [p[b][:, kw:2 * kw] for b in rows]
    rkT = [rk[b].T for b in rows]
    rv = [p[b][:, 2 * kw:2 * kw + RET_WIDTH] for b in rows]
    xi = xi_ref[...]
    outs = [[] for _ in rows]
    for h in range(RET_HEADS):
        dk = slice(h * RET_DK, (h + 1) * RET_DK)
        dv = slice(h * RET_DV, (h + 1) * RET_DV)
        st = [state_ref[b, h] for b in rows]
        inner = [_dot_nt(rq[b][:, dk], rk[b][:, dk]) * decay_ref[h] for b in rows]
        o = [_dot(inner[b], rv[b][:, dv]) + _dot(rq[b][:, dk], st[b]) * xi[:, h:h + 1] for b in rows]
        for b in rows:
            state_ref[b, h] = (st[b] * gch_ref[h:h + 1, 0:1]
                               + _dot(rkT[b][dk, :] * zeta_ref[h:h + 1, :], rv[b][:, dv]))
            outs[b].append(o[b])
    normed = [_group_rms(jnp.concatenate(outs[b], axis=1), ng_ref[...], ones_ref) for b in rows]
    for b in rows:
        rg = p[b][:, 2 * kw + RET_WIDTH:2 * kw + 2 * RET_WIDTH]
        out_ref[b] = normed[b] * (rg * jax.nn.sigmoid(rg))


def _ret_consts():
    H, C = RET_HEADS, RET_CHUNK
    log_g = jnp.log1p(-jnp.exp2(-5.0 - jnp.arange(H, dtype=F32)))
    idx = jnp.arange(C, dtype=F32)
    diff = idx[:, None] - idx[None, :]
    decay = jnp.where(diff >= 0, jnp.exp(jnp.maximum(diff, 0.0) * log_g[:, None, None]), 0.0)
    zeta = jnp.exp((C - 1 - idx) * log_g[:, None])
    xi = jnp.exp((idx + 1) * log_g[:, None]).T
    g_chunk = jnp.broadcast_to(jnp.exp(C * log_g)[:, None], (H, LANES))
    return decay, xi, zeta, g_chunk


def _retention(pret, ng, B, T):
    nch = T // RET_CHUNK
    decay, xi, zeta, gch = _ret_consts()
    c2 = lambda c: (0, 0)
    out = pl.pallas_call(
        _ret_kernel,
        grid=(nch,),
        in_specs=[pl.BlockSpec((B, RET_CHUNK, _RET_COLS), lambda c: (0, c, 0)),
                  pl.BlockSpec((RET_HEADS, RET_CHUNK, RET_CHUNK), lambda c: (0, 0, 0)),
                  pl.BlockSpec((RET_CHUNK, RET_HEADS), c2),
                  pl.BlockSpec((RET_HEADS, RET_CHUNK), c2),
                  pl.BlockSpec((RET_HEADS, LANES), c2),
                  pl.BlockSpec((1, RET_WIDTH), c2),
                  pl.BlockSpec((RET_WIDTH, RET_WIDTH), c2)],
        out_specs=pl.BlockSpec((B, RET_CHUNK, RET_WIDTH), lambda c: (0, c, 0)),
        out_shape=jax.ShapeDtypeStruct((B, T, RET_WIDTH), F32),
        scratch_shapes=[pltpu.VMEM((B, RET_HEADS, RET_DK, RET_DV), F32)],
        compiler_params=_params(1),
        name="retention",
    )(pret.reshape(B, T, _RET_COLS), decay, xi, zeta, gch, ng, _group_ones())
    return out.reshape(B * T, RET_WIDTH)


def _outproj_kernel(x_ref, nsa_ref, ret_ref, w_ref, o_ref):
    o_ref[...] = (x_ref[...] + _dot(nsa_ref[...].astype(BF16), w_ref[0:NSA_WIDTH, :])
                  + _dot(ret_ref[...].astype(BF16), w_ref[NSA_WIDTH:, :]))


def _outproj(x2, nsa, ret, w, tm=512):
    n, d = x2.shape
    return pl.pallas_call(
        _outproj_kernel,
        grid=(n // tm,),
        in_specs=[pl.BlockSpec((tm, d), lambda i: (i, 0)),
                  pl.BlockSpec((tm, NSA_WIDTH), lambda i: (i, 0)),
                  pl.BlockSpec((tm, RET_WIDTH), lambda i: (i, 0)),
                  pl.BlockSpec((NSA_WIDTH + RET_WIDTH, d), lambda i: (0, 0))],
        out_specs=pl.BlockSpec((tm, d), lambda i: (i, 0)),
        out_shape=jax.ShapeDtypeStruct((n, d), F32),
        compiler_params=_params(1),
        name="outproj",
    )(x2, nsa, ret, w)


def _ffn_kernel(x_ref, g_ref, wg_ref, wu_ref, wd_ref, o_ref, h_sc):
    f = pl.program_id(1)

    @pl.when(f == 0)
    def _():
        x = x_ref[...]
        h_sc[...] = _rms(x, g_ref[...]).astype(BF16)
        o_ref[...] = x

    h = h_sc[...]
    a = _dot(h, wg_ref[...])
    act = (a * jax.nn.sigmoid(a) * _dot(h, wu_ref[...])).astype(BF16)
    o_ref[...] += _dot(act, wd_ref[...])


def _ffn(x2, g, wg, wu, wd, tm=512, fc=1408):
    n, d = x2.shape
    dff = wg.shape[1]
    return pl.pallas_call(
        _ffn_kernel,
        grid=(n // tm, dff // fc),
        in_specs=[pl.BlockSpec((tm, d), lambda i, f: (i, 0)),
                  pl.BlockSpec((1, d), lambda i, f: (0, 0)),
                  pl.BlockSpec((d, fc), lambda i, f: (0, f)),
                  pl.BlockSpec((d, fc), lambda i, f: (0, f)),
                  pl.BlockSpec((fc, d), lambda i, f: (f, 0))],
        out_specs=pl.BlockSpec((tm, d), lambda i, f: (i, 0)),
        out_shape=jax.ShapeDtypeStruct((n, d), F32),
        scratch_shapes=[pltpu.VMEM((tm, d), BF16)],
        compiler_params=_params(2),
        name="ffn_dense",
    )(x2, g, wg, wu, wd)


def _router_kernel(x_ref, g_ref, r_ref, rb_ref, tri_ref, h_ref, rank_ref, comb_ref, rankT_ref, cnt_ref):
    h = _rms(x_ref[...], g_ref[...])
    h_ref[...] = h.astype(BF16)
    hh, hm, hl = _split3(h)
    rh, rm, rl = _split3(r_ref[...])
    logits = (_dot(hh, rh) + (_dot(hh, rm) + _dot(hm, rh)) + (_dot(hh, rl) + _dot(hm, rm) + _dot(hl, rh))
              + rb_ref[...])
    lane = lax.broadcasted_iota(jnp.int32, logits.shape, 1).astype(F32)
    logits = jnp.where(lane < N_EXPERTS, logits, NEG)
    m1 = jnp.max(logits, axis=1, keepdims=True)
    i1 = jnp.min(jnp.where(logits == m1, lane, float(LANES)), axis=1, keepdims=True)
    l2 = jnp.where(lane == i1, NEG, logits)
    m2 = jnp.max(l2, axis=1, keepdims=True)
    i2 = jnp.min(jnp.where(l2 == m2, lane, float(LANES)), axis=1, keepdims=True)
    e2 = jnp.exp(m2 - m1)
    w1 = 1.0 / (1.0 + e2)
    w2 = e2 / (1.0 + e2)
    use1, use2 = lane == i1, lane == i2
    comb_ref[...] = jnp.where(use1, w1, 0.0) + jnp.where(use2, w2, 0.0)
    use = (use1 | use2).astype(F32)
    rank = jnp.where(use > 0, _dot(tri_ref[...], use.astype(BF16)), -1.0)
    rank_ref[...] = rank
    rankT_ref[...] = rank.T[0:N_EXPERTS, :]
    cnt_ref[...] = jnp.broadcast_to(jnp.sum(use, axis=0, keepdims=True), cnt_ref.shape).astype(jnp.int32)


def _router(x2, g, router, rb, tm):
    n, d = x2.shape
    nt = n // tm
    tri = (jnp.arange(tm)[:, None] > jnp.arange(tm)[None, :]).astype(BF16)
    rpad = jnp.zeros((d, LANES), F32).at[:, :N_EXPERTS].set(router)
    rbpad = jnp.zeros((1, LANES), F32).at[0, :N_EXPERTS].set(rb)
    c2 = lambda i: (0, 0)
    return pl.pallas_call(
        _router_kernel,
        grid=(nt,),
        in_specs=[pl.BlockSpec((tm, d), lambda i: (i, 0)),
                  pl.BlockSpec((1, d), c2),
                  pl.BlockSpec((d, LANES), c2),
                  pl.BlockSpec((1, LANES), c2),
                  pl.BlockSpec((tm, tm), c2)],
        out_specs=[pl.BlockSpec((tm, d), lambda i: (i, 0)),
                   pl.BlockSpec((tm, LANES), lambda i: (i, 0)),
                   pl.BlockSpec((tm, LANES), lambda i: (i, 0)),
                   pl.BlockSpec((N_EXPERTS, tm), lambda i: (0, i)),
                   pl.BlockSpec((None, 8, LANES), lambda i: (i, 0, 0))],
        out_shape=[jax.ShapeDtypeStruct((n, d), BF16),
                   jax.ShapeDtypeStruct((n, LANES), F32),
                   jax.ShapeDtypeStruct((n, LANES), F32),
                   jax.ShapeDtypeStruct((N_EXPERTS, n), F32),
                   jax.ShapeDtypeStruct((nt, 8, LANES), jnp.int32)],
        compiler_params=_params(1),
        name="moe_router",
    )(x2, g, rpad, rbpad, tri)


MOE_SUB = 128
MOE_MOVE = 2 * MOE_SUB


def _moe_kernel(cnt_ref, h_ref, rankT_ref, rank_ref, comb_ref, wg_ref, wu_ref, wd_ref, x_ref, o_ref, hc_sc, oacc_sc):
    t, e, f = pl.program_id(0), pl.program_id(1), pl.program_id(2)
    nf = pl.num_programs(2)
    tm = h_ref.shape[0]
    nsub = (cnt_ref[t * N_EXPERTS + e] + (MOE_SUB - 1)) // MOE_SUB
    nmove = (nsub + 1) // 2

    @pl.when((e == 0) & (f == 0))
    def _():
        o_ref[...] = x_ref[...]

    @pl.when(f == 0)
    def _():
        rank_row = rankT_ref[...]

        def gather(s, c):
            r0 = pl.multiple_of(s * MOE_MOVE, MOE_MOVE)
            rows = (lax.broadcasted_iota(jnp.int32, (MOE_MOVE, 1), 0) + r0).astype(F32)
            onehot = (rows == rank_row).astype(BF16)
            hc_sc[pl.ds(r0, MOE_MOVE), :] = _dot(onehot, h_ref[...]).astype(BF16)
            oacc_sc[pl.ds(r0, MOE_MOVE), :] = jnp.zeros((MOE_MOVE, oacc_sc.shape[1]), F32)
            return c

        lax.fori_loop(0, nmove, gather, 0)

    def expert(n_rows, s, c):
        r0 = pl.multiple_of(s * n_rows, n_rows)
        rows = hc_sc[pl.ds(r0, n_rows), :]
        a = _dot(rows, wg_ref[...])
        act = (a * jax.nn.sigmoid(a) * _dot(rows, wu_ref[...])).astype(BF16)
        oacc_sc[pl.ds(r0, n_rows), :] += _dot(act, wd_ref[...])
        return c

    lax.fori_loop(0, nsub // 2, functools.partial(expert, MOE_MOVE), 0)
    lax.fori_loop(nsub // 2 * 2, nsub, functools.partial(expert, MOE_SUB), 0)

    @pl.when(f == nf - 1)
    def _():
        is_e = lax.broadcasted_iota(jnp.int32, (1, LANES), 1) == e
        rank_col = jnp.sum(jnp.where(is_e, rank_ref[...], 0.0), axis=1, keepdims=True)
        comb_col = jnp.sum(jnp.where(is_e, comb_ref[...], 0.0), axis=1, keepdims=True)

        def scatter(s, c):
            r0 = pl.multiple_of(s * MOE_MOVE, MOE_MOVE)
            cols = (lax.broadcasted_iota(jnp.int32, (1, MOE_MOVE), 1) + r0).astype(F32)
            onehot = (rank_col == cols).astype(BF16)
            y = _dot(onehot, oacc_sc[pl.ds(r0, MOE_MOVE), :].astype(BF16))
            o_ref[...] += comb_col * y
            return c

        lax.fori_loop(0, nmove, scatter, 0)


def _moe(counts, h, rankT, rank, comb, wg, wu, wd, x2, tm, fc=1408):
    n, d = x2.shape
    dff = wg.shape[2]
    grid_spec = pltpu.PrefetchScalarGridSpec(
        num_scalar_prefetch=1,
        grid=(n // tm, N_EXPERTS, dff // fc),
        in_specs=[pl.BlockSpec((tm, d), lambda t, e, f, c: (t, 0)),
                  pl.BlockSpec((None, 1, tm), lambda t, e, f, c: (e, 0, t)),
                  pl.BlockSpec((tm, LANES), lambda t, e, f, c: (t, 0)),
                  pl.BlockSpec((tm, LANES), lambda t, e, f, c: (t, 0)),
                  pl.BlockSpec((None, d, fc), lambda t, e, f, c: (e, 0, f)),
                  pl.BlockSpec((None, d, fc), lambda t, e, f, c: (e, 0, f)),
                  pl.BlockSpec((None, fc, d), lambda t, e, f, c: (e, f, 0)),
                  pl.BlockSpec((tm, d), lambda t, e, f, c: (t, 0))],
        out_specs=pl.BlockSpec((tm, d), lambda t, e, f, c: (t, 0)),
        scratch_shapes=[pltpu.VMEM((tm, d), BF16), pltpu.VMEM((tm, d), F32)],
    )
    return pl.pallas_call(
        _moe_kernel,
        grid_spec=grid_spec,
        out_shape=jax.ShapeDtypeStruct((n, d), F32),
        compiler_params=_params(3),
        name="moe_experts",
    )(counts, h, rankT.reshape(N_EXPERTS, 1, n), rank, comb, wg, wu, wd, x2)


def _permute_w_in(w):
    o = np.cumsum((0, NSA_WIDTH) + (KV_WIDTH,) * 6 + (3 * NSA_HEADS,))
    q, kc, vc, ks, vs, kw, vw, gts = (w[:, o[k]:o[k + 1]] for k in range(8))
    ret = w[:, o[8]:]
    pad = jnp.zeros((w.shape[0], LANES - 3 * NSA_HEADS), w.dtype)
    return jnp.concatenate([q, ks, kw, vs, vw, kc, vc, gts, pad, ret], axis=1).astype(BF16)


def _nsa_consts(T):
    ncp = T // CMP_STRIDE
    ns = T // SLC_LEN
    cs = np.arange(ncp) * CMP_STRIDE
    ss = np.arange(ns) * SLC_LEN
    ov = np.clip(np.minimum(cs[None, :] + CMP_LEN, ss[:, None] + SLC_LEN) - np.maximum(cs[None, :], ss[:, None]), 0, None)
    ovT = (ov.astype(np.float32) / CMP_LEN)
    ovT[:, ncp - 1] = 0.0
    h = np.arange(NSA_HEADS).reshape(NSA_KV_HEADS, NSA_GROUP) + 1
    slopes = np.exp2(-8.0 * h / NSA_HEADS).astype(np.float32)
    slopes = np.repeat(slopes, Q_BLOCK, axis=1)
    parts, rest = [], np.float64(LOG2E)
    for _ in range(3):
        part = np.float64(np.asarray(rest).astype(BF16))
        parts.append(part)
        rest = rest - part
    qaug = np.zeros((NSA_KV_HEADS, HEAD_DIM, GQ), np.float32)
    for k, part in enumerate(parts):
        qaug[:, k, :] = part * SLC_LEN * slopes
        qaug[:, 3 + k, :] = part * slopes
    kq = np.arange(Q_BLOCK)[:, None] - np.arange(Q_BLOCK)[None, :]
    causb = np.where(kq <= 0, 0.0, NEG).astype(np.float32)
    lowb = np.where(kq > 0, 0.0, NEG).astype(np.float32)
    return jnp.asarray(ovT, BF16), jnp.asarray(qaug, BF16), jnp.asarray(lowb), jnp.asarray(causb)


def _mixer(x2, B, T, norm_g, w_in, q_norm_g, k_norm_g, cmp_pos, w_cmp, ret_norm_g, w_out):
    ns = T // SLC_LEN
    q, kv, kc, vc, gt, pret = _inproj(x2, norm_g[None, :], _permute_w_in(w_in))
    qT, ks, kw, vsT, vwT, gT = _prep(q, kv, gt, q_norm_g[None, :], k_norm_g[1:3], B, T)
    wk, pk = _compress_weights(w_cmp[0], cmp_pos[0])
    wv, pv = _compress_weights(w_cmp[1], cmp_pos[1])
    kcmp, vcT = _compress(kc, vc, wk, wv, pk, pv, k_norm_g[0:1], B, T)
    ovT, qaug, lowb, causb = _nsa_consts(T)
    ocmp, sel, flags = _nsa_cmp(qT, qaug, kcmp, vcT, ovT, B, T)
    lists, counts = _nsa_steps(flags[:, :, :, 0, :].reshape(-1, ns), T // Q_BLOCK)
    kpad = jnp.zeros((WIN, 2 * HEAD_DIM), BF16).at[:, HEAD_DIM:HEAD_DIM + 3].set(-2.0 ** 100)
    kw = jnp.concatenate([jnp.broadcast_to(kpad, kw.shape[:2] + kpad.shape), kw], axis=2)
    vwT = jnp.pad(vwT, ((0, 0), (0, 0), (0, 0), (WIN, 0)))
    nsa = _nsa_main(lists, counts, qT, qaug, ks, vsT, kw, vwT, sel, gT, ocmp, lowb, causb, B, T)
    ret = _retention(pret, ret_norm_g[None, :], B, T)
    return _outproj(x2, nsa.reshape(B * T, NSA_WIDTH), ret, w_out.astype(BF16))


def _moe_layer(x2, norm_g, router, router_b, wg, wu, wd, tm=1024):
    tm = min(tm, x2.shape[0])
    h, rank, comb, rankT, cnt = _router(x2, norm_g[None, :], router, router_b, tm)
    counts = cnt[:, 0, :N_EXPERTS].reshape(-1)
    return _moe(counts, h, rankT, rank, comb, wg.astype(BF16), wu.astype(BF16), wd.astype(BF16), x2, tm)


def kernel(x, norm_mix_g, w_in, q_norm_g, k_norm_g, cmp_pos, w_cmp, ret_norm_g, w_out, norm_ffn_g,
           ffn_w_gate, ffn_w_up, ffn_w_down, moe_router, moe_router_b, moe_w_gate, moe_w_up, moe_w_down):
    B, T, D = x.shape
    depth = norm_mix_g.shape[0]
    x2 = x.reshape(B * T, D)
    for l in range(depth):
        x2 = _mixer(x2, B, T, norm_mix_g[l], w_in[l], q_norm_g[l], k_norm_g[l], cmp_pos[l], w_cmp[l],
                    ret_norm_g[l], w_out[l])
        j = l // 2
        if l % 2 == 0:
            x2 = _ffn(x2, norm_ffn_g[l][None, :], ffn_w_gate[j].astype(BF16), ffn_w_up[j].astype(BF16),
                      ffn_w_down[j].astype(BF16))
        else:
            x2 = _moe_layer(x2, norm_ffn_g[l], moe_router[j], moe_router_b[j], moe_w_gate[j], moe_w_up[j],
                            moe_w_down[j])
    return x2.reshape(B, T, D)
```

```python
import functools

import numpy as np
import jax
import jax.numpy as jnp
from jax import lax
from jax.experimental import pallas as pl
from jax.experimental.pallas import tpu as pltpu

F32 = jnp.float32
BF16 = jnp.bfloat16

HEAD_DIM = 64
NSA_HEADS = 8
NSA_KV_HEADS = 2
NSA_GROUP = NSA_HEADS // NSA_KV_HEADS
RET_HEADS = 8
RET_DK = 32
RET_DV = 64
NSA_WIDTH = NSA_HEADS * HEAD_DIM
RET_WIDTH = RET_HEADS * RET_DV
KV_WIDTH = NSA_KV_HEADS * HEAD_DIM
CMP_LEN = 32
CMP_STRIDE = 16
SLC_LEN = 64
SLC_TOPK = 16
WIN = 512
Q_BLOCK = 128
RET_CHUNK = 128
N_EXPERTS = 8
EPS = 1e-6
NEG = -1e30
BIG = 1e9
LANES = 128
GQ = NSA_GROUP * Q_BLOCK
KEY_STEP = 128
STEP_GROUP = 4
N_FORCED = 3
CMP_CHUNK = 128
CMP_TILES = 4
PREP_TILES = 4
MAIN_TILES = 4
LOOP_TILES = 2
CMP_TAIL = CMP_CHUNK + 8
WIN_KEYS = WIN + Q_BLOCK
V_ROWS = HEAD_DIM + 16
LOG2E = 1.4426950408889634
VMEM_LIMIT = 60 * 1024 * 1024

_C_Q = 0
_C_KV = _C_Q + NSA_WIDTH
_C_KC = _C_KV + 4 * KV_WIDTH
_C_VC = _C_KC + KV_WIDTH
_C_GT = _C_VC + KV_WIDTH
_C_RET = _C_GT + LANES
_RET_COLS = 2 * RET_HEADS * RET_DK + 2 * RET_WIDTH
_C_END = _C_RET + _RET_COLS


def _params(n_axes, vmem=VMEM_LIMIT):
    return pltpu.CompilerParams(dimension_semantics=("arbitrary",) * n_axes, vmem_limit_bytes=vmem)


def _dot(a, b):
    return jnp.dot(a, b, preferred_element_type=F32)


def _dot_nt(a, b):
    return lax.dot_general(a, b, (((1,), (1,)), ((), ())), preferred_element_type=F32)


def _rms(x, g):
    return x * lax.rsqrt(jnp.mean(x * x, axis=-1, keepdims=True) + EPS) * g


def _inproj_kernel(x_ref, g_ref, w_ref, q_ref, kv_ref, kc_ref, vc_ref, gt_ref, ret_ref):
    h = _rms(x_ref[...], g_ref[...]).astype(BF16)
    q_ref[...] = _dot(h, w_ref[:, _C_Q:_C_KV])
    kv_ref[...] = _dot(h, w_ref[:, _C_KV:_C_KC])
    kc_ref[...] = _dot(h, w_ref[:, _C_KC:_C_VC])
    vc_ref[...] = _dot(h, w_ref[:, _C_VC:_C_GT])
    gt_ref[...] = _dot(h, w_ref[:, _C_GT:_C_RET])
    ret_ref[...] = _dot(h, w_ref[:, _C_RET:_C_END])


def _inproj(x2, g, w, tm=1024):
    n, d = x2.shape
    widths = (NSA_WIDTH, 4 * KV_WIDTH, KV_WIDTH, KV_WIDTH, LANES, _RET_COLS)
    return pl.pallas_call(
        _inproj_kernel,
        grid=(n // tm,),
        in_specs=[pl.BlockSpec((tm, d), lambda i: (i, 0)),
                  pl.BlockSpec((1, d), lambda i: (0, 0)),
                  pl.BlockSpec((d, _C_END), lambda i: (0, 0))],
        out_specs=[pl.BlockSpec((tm, c), lambda i: (i, 0)) for c in widths],
        out_shape=[jax.ShapeDtypeStruct((n, c), F32) for c in widths],
        compiler_params=_params(1),
        name="inproj",
    )(x2, g, w)


def _group_rms(x, g, ones_ref):
    w = x.shape[1]
    ones = ones_ref[0:w, 0:w]
    hi, mid, lo = _split3(x * x)
    ms = (_dot(hi, ones) + _dot(mid, ones) + _dot(lo, ones)) * (1.0 / HEAD_DIM)
    return x * lax.rsqrt(ms + EPS) * g


def _group_ones():
    lane = np.arange(NSA_WIDTH) // HEAD_DIM
    return jnp.asarray(lane[:, None] == lane[None, :], BF16)


def _prep_kernel(q_ref, kv_ref, gt_ref, qg_ref, kg_ref, ones_ref, qT_ref, ks_ref, kw_ref, vsT_ref, vwT_ref, gT_ref):
    tiles = range(PREP_TILES)
    rows = [slice(u * Q_BLOCK, (u + 1) * Q_BLOCK) for u in tiles]
    n_tok = PREP_TILES * Q_BLOCK
    scale = HEAD_DIM ** -0.5 * LOG2E
    qn = _group_rms(q_ref[...], qg_ref[...], ones_ref) * scale
    qt = [qn[rows[u]].T for u in tiles]
    for g in range(NSA_KV_HEADS):
        for r in range(NSA_GROUP):
            h = g * NSA_GROUP + r
            for u in tiles:
                qT_ref[g, u, :, r * Q_BLOCK:(r + 1) * Q_BLOCK] = qt[u][h * HEAD_DIM:(h + 1) * HEAD_DIM, :].astype(BF16)
    kv = kv_ref[...]
    vst = [kv[rows[u], 2 * KV_WIDTH:3 * KV_WIDTH].T for u in tiles]
    vwt = [kv[rows[u], 3 * KV_WIDTH:4 * KV_WIDTH].T for u in tiles]
    gts = [jax.nn.sigmoid(gt_ref[rows[u], :].T[0:32, :]) for u in tiles]
    pos = pl.program_id(1) * n_tok + lax.broadcasted_iota(jnp.int32, (n_tok, HEAD_DIM), 0)
    col = lax.broadcasted_iota(jnp.int32, (n_tok, HEAD_DIM), 1)
    kpos = jnp.where(col < 3, pos // SLC_LEN, jnp.where(col < 6, pos % SLC_LEN, 0)).astype(F32)
    ones_row = (lax.broadcasted_iota(jnp.int32, (V_ROWS - HEAD_DIM, Q_BLOCK), 0) == 0).astype(F32)
    ks = _group_rms(kv[:, 0:KV_WIDTH], kg_ref[0:1, :], ones_ref)
    kw = _group_rms(kv[:, KV_WIDTH:2 * KV_WIDTH], kg_ref[1:2, :], ones_ref)
    for g in range(NSA_KV_HEADS):
        sl = slice(g * HEAD_DIM, (g + 1) * HEAD_DIM)
        ks_ref[g] = jnp.concatenate([ks[:, sl], kpos], axis=1).astype(BF16)
        kw_ref[g] = jnp.concatenate([kw[:, sl], kpos], axis=1).astype(BF16)
        for u in tiles:
            vsT_ref[g, :, rows[u]] = jnp.concatenate([vst[u][sl, :], ones_row], axis=0).astype(BF16)
            vwT_ref[g, :, rows[u]] = jnp.concatenate([vwt[u][sl, :], ones_row], axis=0).astype(BF16)
    for u in tiles:
        gT_ref[u] = gts[u]


def _prep(q, kv, gt, qg, kg, B, T):
    nq = T // Q_BLOCK
    G = NSA_KV_HEADS
    n_tok = PREP_TILES * Q_BLOCK
    steps = nq // PREP_TILES
    row = lambda b, i: (b * steps + i, 0)
    return pl.pallas_call(
        _prep_kernel,
        grid=(B, steps),
        in_specs=[pl.BlockSpec((n_tok, NSA_WIDTH), row),
                  pl.BlockSpec((n_tok, 4 * KV_WIDTH), row),
                  pl.BlockSpec((n_tok, LANES), row),
                  pl.BlockSpec((1, NSA_WIDTH), lambda b, i: (0, 0)),
                  pl.BlockSpec((2, KV_WIDTH), lambda b, i: (0, 0)),
                  pl.BlockSpec((NSA_WIDTH, NSA_WIDTH), lambda b, i: (0, 0))],
        out_specs=[pl.BlockSpec((None, G, PREP_TILES, HEAD_DIM, GQ), lambda b, i: (b, 0, i, 0, 0)),
                   pl.BlockSpec((None, G, n_tok, 2 * HEAD_DIM), lambda b, i: (b, 0, i, 0)),
                   pl.BlockSpec((None, G, n_tok, 2 * HEAD_DIM), lambda b, i: (b, 0, i, 0)),
                   pl.BlockSpec((None, G, V_ROWS, n_tok), lambda b, i: (b, 0, 0, i)),
                   pl.BlockSpec((None, G, V_ROWS, n_tok), lambda b, i: (b, 0, 0, i)),
                   pl.BlockSpec((None, PREP_TILES, 32, Q_BLOCK), lambda b, i: (b, i, 0, 0))],
        out_shape=[jax.ShapeDtypeStruct((B, G, nq, HEAD_DIM, GQ), BF16),
                   jax.ShapeDtypeStruct((B, G, T, 2 * HEAD_DIM), BF16),
                   jax.ShapeDtypeStruct((B, G, T, 2 * HEAD_DIM), BF16),
                   jax.ShapeDtypeStruct((B, G, V_ROWS, T), BF16),
                   jax.ShapeDtypeStruct((B, G, V_ROWS, T), BF16),
                   jax.ShapeDtypeStruct((B, nq, 32, Q_BLOCK), F32)],
        compiler_params=_params(2),
        name="nsa_prep",
    )(q, kv, gt, jnp.tile(qg, (1, NSA_HEADS)), jnp.tile(kg, (1, NSA_KV_HEADS)), _group_ones())


def _compress_kernel(kc_ref, vc_ref, wk_ref, wv_ref, pk_ref, pv_ref, kg_ref, kcmp_ref, vcT_ref):
    ncp = kc_ref.shape[0] // CMP_STRIDE

    def comp(a_ref, w_ref, p_ref):
        lo = jnp.zeros((ncp, KV_WIDTH), F32)
        hi = jnp.zeros((ncp, KV_WIDTH), F32)
        for l in range(CMP_STRIDE):
            a = a_ref[pl.ds(l, ncp, stride=CMP_STRIDE), :]
            lo += _dot((a + p_ref[0, l:l + 1, :]).astype(BF16), w_ref[0, l])
            hi += _dot((a + p_ref[1, l:l + 1, :]).astype(BF16), w_ref[1, l])
        return lo + pltpu.roll(hi, ncp - 1, 0)

    k = comp(kc_ref, wk_ref, pk_ref)
    v = comp(vc_ref, wv_ref, pv_ref).T
    cend = lax.broadcasted_iota(jnp.int32, (ncp, HEAD_DIM), 0) * CMP_STRIDE + (CMP_LEN - 1)
    col = lax.broadcasted_iota(jnp.int32, (ncp, HEAD_DIM), 1)
    kpos = jnp.where(col < 3, cend // SLC_LEN, jnp.where(col < 6, cend % SLC_LEN, 0)).astype(F32)
    for g in range(NSA_KV_HEADS):
        sl = slice(g * HEAD_DIM, (g + 1) * HEAD_DIM)
        kcmp_ref[g] = jnp.concatenate([_rms(k[:, sl], kg_ref[...]), kpos], axis=1).astype(BF16)
        vcT_ref[g] = v[sl, :].astype(BF16)


def _compress(kc, vc, wk, wv, pk, pv, kg, B, T):
    ncp = T // CMP_STRIDE
    G = NSA_KV_HEADS
    const4 = lambda b: (0, 0, 0, 0)
    const3 = lambda b: (0, 0, 0)
    const2 = lambda b: (0, 0)
    return pl.pallas_call(
        _compress_kernel,
        grid=(B,),
        in_specs=[pl.BlockSpec((T, KV_WIDTH), lambda b: (b, 0)),
                  pl.BlockSpec((T, KV_WIDTH), lambda b: (b, 0)),
                  pl.BlockSpec((2, CMP_STRIDE, KV_WIDTH, KV_WIDTH), const4),
                  pl.BlockSpec((2, CMP_STRIDE, KV_WIDTH, KV_WIDTH), const4),
                  pl.BlockSpec((2, CMP_STRIDE, KV_WIDTH), const3),
                  pl.BlockSpec((2, CMP_STRIDE, KV_WIDTH), const3),
                  pl.BlockSpec((1, HEAD_DIM), const2)],
        out_specs=[pl.BlockSpec((None, G, ncp, 2 * HEAD_DIM), lambda b: (b, 0, 0, 0)),
                   pl.BlockSpec((None, G, HEAD_DIM, ncp), lambda b: (b, 0, 0, 0))],
        out_shape=[jax.ShapeDtypeStruct((B, G, ncp, 2 * HEAD_DIM), BF16),
                   jax.ShapeDtypeStruct((B, G, HEAD_DIM, ncp), BF16)],
        compiler_params=_params(1),
        name="nsa_compress",
    )(kc, vc, wk, wv, pk, pv, kg)


def _compress_weights(w, pos):
    G = NSA_KV_HEADS
    w4 = w.reshape(2, CMP_STRIDE, HEAD_DIM, HEAD_DIM)
    eye = jnp.eye(G, dtype=w.dtype)
    wbd = jnp.einsum('hlde,gk->hlgdke', w4, eye).reshape(2, CMP_STRIDE, KV_WIDTH, KV_WIDTH)
    p = pos.reshape(2, CMP_STRIDE, 1, HEAD_DIM)
    p = jnp.broadcast_to(p, (2, CMP_STRIDE, G, HEAD_DIM)).reshape(2, CMP_STRIDE, KV_WIDTH)
    return wbd.astype(BF16), p


def _split3(x):
    hi = x.astype(BF16)
    r = x - hi.astype(F32)
    mid = r.astype(BF16)
    lo = (r - mid.astype(F32)).astype(BF16)
    return hi, mid, lo


def _nsa_cmp_kernel(qT_ref, qaug_ref, kc_ref, vcT_ref, ovT_ref, ocmp_ref, sel_ref, flag_ref, *, n_sel):
    ncp = kc_ref.shape[0]
    ns = ovT_ref.shape[0]
    tiles = range(CMP_TILES)
    i0 = pl.program_id(2) * CMP_TILES
    lane = lax.broadcasted_iota(jnp.int32, (1, GQ), 1)
    q = [jnp.concatenate([qT_ref[u], qaug_ref[...]], axis=0) for u in tiles]
    t_row = [(i0 + u) * Q_BLOCK + (lane & (Q_BLOCK - 1)) for u in tiles]
    has_cmp = [(t_row[u] >= CMP_LEN - 1).astype(F32) for u in tiles]
    tq = [(i0 + u) * Q_BLOCK + lax.broadcasted_iota(jnp.int32, (1, Q_BLOCK), 1) for u in tiles]
    cur = [tq[u] // SLC_LEN for u in tiles]

    def prefix(rows):
        nsk = rows * CMP_STRIDE // SLC_LEN
        tail0 = max(rows - CMP_TAIL, 0)
        kc = kc_ref[0:rows, :]
        s = [_dot(kc, q[u]) for u in tiles]
        cend = (lax.broadcasted_iota(jnp.int32, (rows - tail0, 1), 0) + tail0) * CMP_STRIDE + (CMP_LEN - 1)
        tail = [jnp.where(t_row[u] >= cend, s[u][tail0:], NEG) for u in tiles]
        s = [jnp.concatenate([s[u][0:tail0], tail[u]], axis=0) if tail0 else tail[u] for u in tiles]
        m = [jnp.max(s[u], axis=0, keepdims=True) for u in tiles]
        e = [jnp.exp2(s[u] - m[u]) for u in tiles]
        p = [e[u] * (has_cmp[u] / jnp.sum(e[u], axis=0, keepdims=True)) for u in tiles]
        vc = vcT_ref[:, 0:rows]
        for u in tiles:
            ocmp_ref[u] = _dot(vc, p[u].astype(BF16))

        ps = [p[u][:, 0:Q_BLOCK] for u in tiles]
        for r in range(1, NSA_GROUP):
            ps = [ps[u] + p[u][:, r * Q_BLOCK:(r + 1) * Q_BLOCK] for u in tiles]
        ov = ovT_ref[0:nsk, 0:rows]
        split = [_split3(ps[u]) for u in tiles]
        imp = [_dot(ov, split[u][0]) + _dot(ov, split[u][1]) + _dot(ov, split[u][2]) for u in tiles]

        blk = lax.broadcasted_iota(jnp.int32, (nsk, 1), 0)
        forced = [(blk == 0) | (blk == cur[u]) | (blk == cur[u] - 1) for u in tiles]
        valid = [blk * SLC_LEN <= tq[u] for u in tiles]
        imp = [jnp.where(forced[u], -3e38, jnp.where(valid[u], imp[u], -BIG)) for u in tiles]
        blk_f = blk.astype(F32)
        sel = [forced[u].astype(F32) for u in tiles]
        for _ in range(n_sel - N_FORCED):
            mx = [jnp.max(imp[u], axis=0, keepdims=True) for u in tiles]
            idx = [jnp.min(jnp.where(imp[u] == mx[u], blk_f, float(ns)), axis=0, keepdims=True) for u in tiles]
            pick = [blk_f == idx[u] for u in tiles]
            sel = [jnp.where(pick[u], 1.0, sel[u]) for u in tiles]
            imp = [jnp.where(pick[u], -3e38, imp[u]) for u in tiles]
        ones = jnp.ones((8, Q_BLOCK), BF16)
        for u in tiles:
            sel_ref[u, 0:nsk, :] = sel[u]
            cnt = _dot_nt(ones, sel[u].astype(BF16))
            flag_ref[u, :, 0:nsk] = (cnt > 0).astype(jnp.int32)
            if nsk < ns:
                sel_ref[u, nsk:, :] = jnp.zeros((ns - nsk, Q_BLOCK), F32)
                flag_ref[u, :, nsk:] = jnp.zeros((8, ns - nsk), jnp.int32)

    n_variants = ncp // CMP_CHUNK
    last = i0 + CMP_TILES - 1
    variant = (last * (Q_BLOCK // CMP_STRIDE) + (Q_BLOCK // CMP_STRIDE - 2)) // CMP_CHUNK
    for k in range(n_variants):
        pl.when(variant == k)(functools.partial(prefix, (k + 1) * CMP_CHUNK))


def _nsa_cmp(qT, qaug, kcmp, vcT, ovT, B, T):
    G = NSA_KV_HEADS
    nq = T // Q_BLOCK
    ncp = T // CMP_STRIDE
    ns = T // SLC_LEN
    n_sel = min(SLC_TOPK, ns)
    assert ncp % CMP_CHUNK == 0 and n_sel > N_FORCED and nq % CMP_TILES == 0
    tile = lambda b, g, i: (b, g, i, 0, 0)
    return pl.pallas_call(
        functools.partial(_nsa_cmp_kernel, n_sel=n_sel),
        grid=(B, G, nq // CMP_TILES),
        in_specs=[pl.BlockSpec((None, None, CMP_TILES, HEAD_DIM, GQ), tile),
                  pl.BlockSpec((None, HEAD_DIM, GQ), lambda b, g, i: (g, 0, 0)),
                  pl.BlockSpec((None, None, ncp, 2 * HEAD_DIM), lambda b, g, i: (b, g, 0, 0)),
                  pl.BlockSpec((None, None, HEAD_DIM, ncp), lambda b, g, i: (b, g, 0, 0)),
                  pl.BlockSpec((ns, ncp), lambda b, g, i: (0, 0))],
        out_specs=[pl.BlockSpec((None, None, CMP_TILES, HEAD_DIM, GQ), tile),
                   pl.BlockSpec((None, None, CMP_TILES, ns, Q_BLOCK), tile),
                   pl.BlockSpec((None, None, CMP_TILES, 8, ns), tile)],
        out_shape=[jax.ShapeDtypeStruct((B, G, nq, HEAD_DIM, GQ), F32),
                   jax.ShapeDtypeStruct((B, G, nq, ns, Q_BLOCK), F32),
                   jax.ShapeDtypeStruct((B, G, nq, 8, ns), jnp.int32)],
        compiler_params=_params(3),
        name="nsa_cmp",
    )(qT, qaug, kcmp, vcT, ovT)


def _nsa_main_kernel(list_ref, cnt_ref, qT_ref, qaug_ref, ks_ref, vsT_ref, kw_ref, vwT_ref, sel_ref, gT_ref, ocmp_ref,
                     lowb_ref, causb_ref, out_ref, m_sc, acc_sc, win_sc):
    b, g = pl.program_id(0), pl.program_id(1)
    tiles = range(MAIN_TILES)
    i = [pl.program_id(2) * MAIN_TILES + u for u in tiles]
    tile_id = [(b * pl.num_programs(1) + g) * (pl.num_programs(2) * MAIN_TILES) + i[u] for u in tiles]
    n_steps = sel_ref.shape[1] // 2
    q = [jnp.concatenate([qT_ref[u], qaug_ref[...]], axis=0) for u in tiles]
    k0 = [pl.multiple_of(i[u] * Q_BLOCK, Q_BLOCK) for u in tiles]

    def sel_bias(u, j, valid):
        def row(r):
            picked = (sel_ref[u, pl.ds(r, 1), :] > 0.5) & valid
            return jnp.concatenate([jnp.where(picked, 0.0, NEG)] * NSA_GROUP, axis=1)
        return row(2 * j), row(2 * j + 1)

    def add_sel_bias(s, ba, bb):
        return jnp.concatenate([s[0:SLC_LEN] + ba, s[SLC_LEN:] + bb], axis=0)

    lowb = jnp.concatenate([lowb_ref[...]] * NSA_GROUP, axis=1)
    causb = jnp.concatenate([causb_ref[...]] * NSA_GROUP, axis=1)

    bias_d = [sel_bias(u, i[u], True) for u in tiles]
    sd = [_dot(ks_ref[pl.ds(k0[u], KEY_STEP), :], q[u]) for u in tiles]
    sw = [_dot(kw_ref[pl.ds(k0[u], WIN_KEYS), :], q[u]) for u in tiles]
    sd = [add_sel_bias(sd[u], *bias_d[u]) + causb for u in tiles]
    sw = [jnp.concatenate([sw[u][0:Q_BLOCK] + lowb, sw[u][Q_BLOCK:WIN], sw[u][WIN:] + causb], axis=0) for u in tiles]
    md = [jnp.max(sd[u], axis=0, keepdims=True) for u in tiles]
    mw = [jnp.max(sw[u], axis=0, keepdims=True) for u in tiles]
    accd = [_dot(vsT_ref[:, pl.ds(k0[u], KEY_STEP)], jnp.exp2((sd[u] - md[u]).astype(BF16))) for u in tiles]
    ow = [_dot(vwT_ref[:, pl.ds(k0[u], WIN_KEYS)], jnp.exp2((sw[u] - mw[u]).astype(BF16))) for u in tiles]
    for u in tiles:
        m_sc[u] = md[u]
        acc_sc[u] = accd[u]
        win_sc[u] = ow[u][0:HEAD_DIM] / ow[u][HEAD_DIM:HEAD_DIM + 1]

    def scores(u, t):
        ks, vs, biases = [], [], []
        for x in range(STEP_GROUP):
            j = list_ref[tile_id[u] * n_steps + t * STEP_GROUP + x]
            valid = j >= 0
            j = jnp.maximum(j, 0)
            kj = pl.multiple_of(j * KEY_STEP, KEY_STEP)
            ks.append(ks_ref[pl.ds(kj, KEY_STEP), :])
            vs.append(vsT_ref[:, pl.ds(kj, KEY_STEP)])
            biases.append(sel_bias(u, j, valid))
        s = _dot(jnp.concatenate(ks, axis=0), q[u])
        s = jnp.concatenate([add_sel_bias(s[x * KEY_STEP:(x + 1) * KEY_STEP], *biases[x])
                             for x in range(STEP_GROUP)], axis=0)
        return s, jnp.max(s, axis=0, keepdims=True), jnp.concatenate(vs, axis=1)

    def accumulate(u, s, smax, vcat):
        m_old = m_sc[u]
        m_new = jnp.maximum(m_old, smax)
        alpha = jnp.exp2(m_old - m_new)
        acc_sc[u] = alpha * acc_sc[u] + _dot(vcat, jnp.exp2((s - m_new).astype(BF16)))
        m_sc[u] = m_new

    def run(work, t, carry):
        staged = [(u, scores(u, t * mult + off)) for (u, mult, off) in work]
        for u, args in staged:
            accumulate(u, *args)
        return carry

    for u0 in range(0, MAIN_TILES, LOOP_TILES):
        us = range(u0, u0 + LOOP_TILES)
        n_groups = functools.reduce(
            jnp.maximum, [(cnt_ref[tile_id[u]] + (STEP_GROUP - 1)) // STEP_GROUP for u in us])
        lax.fori_loop(0, n_groups // 2, functools.partial(run, [(u, 2, off) for off in (0, 1) for u in us]), 0)
        lax.fori_loop(n_groups // 2 * 2, n_groups, functools.partial(run, [(u, 1, 0) for u in us]), 0)

    def gate(u, k):
        rows = [gT_ref[u, pl.ds(g * (NSA_GROUP * 3) + r * 3 + k, 1), :] for r in range(NSA_GROUP)]
        return jnp.concatenate(rows, axis=1)

    o_slc = [acc_sc[u, 0:HEAD_DIM, :] / acc_sc[u, HEAD_DIM:HEAD_DIM + 1, :] for u in tiles]
    o = [gate(u, 0) * ocmp_ref[u] + gate(u, 1) * o_slc[u] + gate(u, 2) * win_sc[u] for u in tiles]
    o = [jnp.concatenate([o[u], jnp.zeros_like(o[u])], axis=0) for u in tiles]
    for r in range(NSA_GROUP):
        ot = [o[u][:, r * Q_BLOCK:(r + 1) * Q_BLOCK].T[:, 0:HEAD_DIM] for u in tiles]
        for u in tiles:
            out_ref[u * Q_BLOCK:(u + 1) * Q_BLOCK, r * HEAD_DIM:(r + 1) * HEAD_DIM] = ot[u]


def _nsa_steps_kernel(flagT_ref, pairT_ref, list_ref, cnt_ref, *, nq):
    n_steps, nt = list_ref.shape
    need = _dot(pairT_ref[...], flagT_ref[...].astype(BF16)) > 0
    step = lax.broadcasted_iota(jnp.int32, (n_steps, 1), 0)
    own = lax.broadcasted_iota(jnp.int32, (1, nt), 1) % nq
    need = need & (step < own)
    need_f = need.astype(F32)
    earlier = (lax.broadcasted_iota(jnp.int32, (n_steps, n_steps), 1) < step).astype(BF16)
    slot = _dot(earlier, need_f.astype(BF16))
    total = jnp.sum(need_f, axis=0, keepdims=True)
    cnt_ref[...] = jnp.broadcast_to(total, cnt_ref.shape).astype(jnp.int32)
    step_f = step.astype(F32)
    for p in range(n_steps):
        val = jnp.sum(jnp.where(need & (slot == p), step_f, 0.0), axis=0, keepdims=True)
        list_ref[p:p + 1, :] = jnp.where(total > p, val, -1.0).astype(jnp.int32)


def _nsa_steps(flags, nq):
    nt, ns = flags.shape
    n_steps = ns // 2
    pairT = jnp.asarray(np.arange(n_steps)[:, None] == np.arange(ns)[None, :] // 2, BF16)
    lists, counts = pl.pallas_call(
        functools.partial(_nsa_steps_kernel, nq=nq),
        out_shape=[jax.ShapeDtypeStruct((n_steps, nt), jnp.int32), jax.ShapeDtypeStruct((8, nt), jnp.int32)],
        name="nsa_steps",
    )(flags.T.astype(F32), pairT)
    return lists.T.reshape(-1), counts[0]


def _nsa_main(lists, counts, qT, qaug, ks, vsT, kw, vwT, sel, gT, ocmp, lowb, causb, B, T):
    G = NSA_KV_HEADS
    nq = T // Q_BLOCK
    ns = T // SLC_LEN
    whole = lambda b, g, i, *_: (b, g, 0, 0)
    tile = lambda b, g, i, *_: (b, g, i, 0, 0)
    const = lambda b, g, i, *_: (0, 0)
    grid_spec = pltpu.PrefetchScalarGridSpec(
        num_scalar_prefetch=2,
        grid=(B, G, nq // MAIN_TILES),
        in_specs=[pl.BlockSpec((None, None, MAIN_TILES, HEAD_DIM, GQ), tile),
                  pl.BlockSpec((None, HEAD_DIM, GQ), lambda b, g, i, *_: (g, 0, 0)),
                  pl.BlockSpec((None, None, T, 2 * HEAD_DIM), whole),
                  pl.BlockSpec((None, None, V_ROWS, T), whole),
                  pl.BlockSpec((None, None, T + WIN, 2 * HEAD_DIM), whole),
                  pl.BlockSpec((None, None, V_ROWS, T + WIN), whole),
                  pl.BlockSpec((None, None, MAIN_TILES, ns, Q_BLOCK), tile),
                  pl.BlockSpec((None, MAIN_TILES, 32, Q_BLOCK), lambda b, g, i, *_: (b, i, 0, 0)),
                  pl.BlockSpec((None, None, MAIN_TILES, HEAD_DIM, GQ), tile),
                  pl.BlockSpec((Q_BLOCK, Q_BLOCK), const),
                  pl.BlockSpec((Q_BLOCK, Q_BLOCK), const)],
        out_specs=pl.BlockSpec((None, MAIN_TILES * Q_BLOCK, NSA_GROUP * HEAD_DIM), lambda b, g, i, *_: (b, i, g)),
        scratch_shapes=[pltpu.VMEM((MAIN_TILES, 1, GQ), F32), pltpu.VMEM((MAIN_TILES, V_ROWS, GQ), F32),
                        pltpu.VMEM((MAIN_TILES, HEAD_DIM, GQ), F32)],
    )
    return pl.pallas_call(
        _nsa_main_kernel,
        grid_spec=grid_spec,
        out_shape=jax.ShapeDtypeStruct((B, T, NSA_WIDTH), F32),
        compiler_params=_params(3),
        name="nsa_main",
    )(lists, counts, qT, qaug, ks, vsT, kw, vwT, sel, gT, ocmp, lowb, causb)


def _ret_kernel(p_ref, decay_ref, xi_ref, zeta_ref, gch_ref, ng_ref, ones_ref, out_ref, state_ref):
    @pl.when(pl.program_id(0) == 0)
    def _():
        state_ref[...] = jnp.zeros(state_ref.shape, F32)

    rows = range(p_ref.shape[0])
    kw = RET_HEADS * RET_DK
    p = [p_ref[b] for b in rows]
    rq = [p[b][:, 0:kw] * (RET_DK ** -0.5) for b in rows]
    rk = [p[b][:, kw:2 * kw] for b in rows]
    rkT = [rk[b].T for b in rows]
    rv = [p[b][:, 2 * kw:2 * kw + RET_WIDTH] for b in rows]
    xi = xi_ref[...]
    outs = [[] for _ in rows]
    for h in range(RET_HEADS):
        dk = slice(h * RET_DK, (h + 1) * RET_DK)
        dv = slice(h * RET_DV, (h + 1) * RET_DV)
        st = [state_ref[b, h] for b in rows]
        inner = [_dot_nt(rq[b][:, dk], rk[b][:, dk]) * decay_ref[h] for b in rows]
        o = [_dot(inner[b], rv[b][:, dv]) + _dot(rq[b][:, dk], st[b]) * xi[:, h:h + 1] for b in rows]
        for b in rows:
            state_ref[b, h] = (st[b] * gch_ref[h:h + 1, 0:1]
                               + _dot(rkT[b][dk, :] * zeta_ref[h:h + 1, :], rv[b][:, dv]))
            outs[b].append(o[b])
    normed = [_group_rms(jnp.concatenate(outs[b], axis=1), ng_ref[...], ones_ref) for b in rows]
    for b in rows:
        rg = p[b][:, 2 * kw + RET_WIDTH:2 * kw + 2 * RET_WIDTH]
        out_ref[b] = normed[b] * (rg * jax.nn.sigmoid(rg))


def _ret_consts():
    H, C = RET_HEADS, RET_CHUNK
    log_g = jnp.log1p(-jnp.exp2(-5.0 - jnp.arange(H, dtype=F32)))
    idx = jnp.arange(C, dtype=F32)
    diff = idx[:, None] - idx[None, :]
    decay = jnp.where(diff >= 0, jnp.exp(jnp.maximum(diff, 0.0) * log_g[:, None, None]), 0.0)
    zeta = jnp.exp((C - 1 - idx) * log_g[:, None])
    xi = jnp.exp((idx + 1) * log_g[:, None]).T
    g_chunk = jnp.broadcast_to(jnp.exp(C * log_g)[:, None], (H, LANES))
    return decay, xi, zeta, g_chunk


def _retention(pret, ng, B, T):
    nch = T // RET_CHUNK
    decay, xi, zeta, gch = _ret_consts()
    c2 = lambda c: (0, 0)
    out = pl.pallas_call(
        _ret_kernel,
        grid=(nch,),
        in_specs=[pl.BlockSpec((B, RET_CHUNK, _RET_COLS), lambda c: (0, c, 0)),
                  pl.BlockSpec((RET_HEADS, RET_CHUNK, RET_CHUNK), lambda c: (0, 0, 0)),
                  pl.BlockSpec((RET_CHUNK, RET_HEADS), c2),
                  pl.BlockSpec((RET_HEADS, RET_CHUNK), c2),
                  pl.BlockSpec((RET_HEADS, LANES), c2),
                  pl.BlockSpec((1, RET_WIDTH), c2),
                  pl.BlockSpec((RET_WIDTH, RET_WIDTH), c2)],
        out_specs=pl.BlockSpec((B, RET_CHUNK, RET_WIDTH), lambda c: (0, c, 0)),
        out_shape=jax.ShapeDtypeStruct((B, T, RET_WIDTH), F32),
        scratch_shapes=[pltpu.VMEM((B, RET_HEADS, RET_DK, RET_DV), F32)],
        compiler_params=_params(1),
        name="retention",
    )(pret.reshape(B, T, _RET_COLS), decay, xi, zeta, gch, ng, _group_ones())
    return out.reshape(B * T, RET_WIDTH)


def _outproj_kernel(x_ref, nsa_ref, ret_ref, w_ref, o_ref):
    o_ref[...] = (x_ref[...] + _dot(nsa_ref[...].astype(BF16), w_ref[0:NSA_WIDTH, :])
                  + _dot(ret_ref[...].astype(BF16), w_ref[NSA_WIDTH:, :]))


def _outproj(x2, nsa, ret, w, tm=512):
    n, d = x2.shape
    return pl.pallas_call(
        _outproj_kernel,
        grid=(n // tm,),
        in_specs=[pl.BlockSpec((tm, d), lambda i: (i, 0)),
                  pl.BlockSpec((tm, NSA_WIDTH), lambda i: (i, 0)),
                  pl.BlockSpec((tm, RET_WIDTH), lambda i: (i, 0)),
                  pl.BlockSpec((NSA_WIDTH + RET_WIDTH, d), lambda i: (0, 0))],
        out_specs=pl.BlockSpec((tm, d), lambda i: (i, 0)),
        out_shape=jax.ShapeDtypeStruct((n, d), F32),
        compiler_params=_params(1),
        name="outproj",
    )(x2, nsa, ret, w)


def _ffn_kernel(x_ref, g_ref, wg_ref, wu_ref, wd_ref, o_ref, h_sc):
    f = pl.program_id(1)

    @pl.when(f == 0)
    def _():
        x = x_ref[...]
        h_sc[...] = _rms(x, g_ref[...]).astype(BF16)
        o_ref[...] = x

    h = h_sc[...]
    a = _dot(h, wg_ref[...])
    act = (a * jax.nn.sigmoid(a) * _dot(h, wu_ref[...])).astype(BF16)
    o_ref[...] += _dot(act, wd_ref[...])


def _ffn(x2, g, wg, wu, wd, tm=1024, fc=1408):
    n, d = x2.shape
    dff = wg.shape[1]
    return pl.pallas_call(
        _ffn_kernel,
        grid=(n // tm, dff // fc),
        in_specs=[pl.BlockSpec((tm, d), lambda i, f: (i, 0)),
                  pl.BlockSpec((1, d), lambda i, f: (0, 0)),
                  pl.BlockSpec((d, fc), lambda i, f: (0, f)),
                  pl.BlockSpec((d, fc), lambda i, f: (0, f)),
                  pl.BlockSpec((fc, d), lambda i, f: (f, 0))],
        out_specs=pl.BlockSpec((tm, d), lambda i, f: (i, 0)),
        out_shape=jax.ShapeDtypeStruct((n, d), F32),
        scratch_shapes=[pltpu.VMEM((tm, d), BF16)],
        compiler_params=_params(2),
        name="ffn_dense",
    )(x2, g, wg, wu, wd)


def _router_kernel(x_ref, g_ref, r_ref, rb_ref, tri_ref, h_ref, rank_ref, comb_ref, rankT_ref, cnt_ref):
    h = _rms(x_ref[...], g_ref[...])
    h_ref[...] = h.astype(BF16)
    hh, hm, hl = _split3(h)
    rh, rm, rl = _split3(r_ref[...])
    logits = (_dot(hh, rh) + (_dot(hh, rm) + _dot(hm, rh)) + (_dot(hh, rl) + _dot(hm, rm) + _dot(hl, rh))
              + rb_ref[...])
    lane = lax.broadcasted_iota(jnp.int32, logits.shape, 1).astype(F32)
    logits = jnp.where(lane < N_EXPERTS, logits, NEG)
    m1 = jnp.max(logits, axis=1, keepdims=True)
    i1 = jnp.min(jnp.where(logits == m1, lane, float(LANES)), axis=1, keepdims=True)
    l2 = jnp.where(lane == i1, NEG, logits)
    m2 = jnp.max(l2, axis=1, keepdims=True)
    i2 = jnp.min(jnp.where(l2 == m2, lane, float(LANES)), axis=1, keepdims=True)
    e2 = jnp.exp(m2 - m1)
    w1 = 1.0 / (1.0 + e2)
    w2 = e2 / (1.0 + e2)
    use1, use2 = lane == i1, lane == i2
    comb_ref[...] = jnp.where(use1, w1, 0.0) + jnp.where(use2, w2, 0.0)
    use = (use1 | use2).astype(F32)
    rank = jnp.where(use > 0, _dot(tri_ref[...], use.astype(BF16)), -1.0)
    rank_ref[...] = rank
    rankT_ref[...] = rank.T[0:N_EXPERTS, :]
    cnt_ref[...] = jnp.broadcast_to(jnp.sum(use, axis=0, keepdims=True), cnt_ref.shape).astype(jnp.int32)


def _router(x2, g, router, rb, tm):
    n, d = x2.shape
    nt = n // tm
    tri = (jnp.arange(tm)[:, None] > jnp.arange(tm)[None, :]).astype(BF16)
    rpad = jnp.zeros((d, LANES), F32).at[:, :N_EXPERTS].set(router)
    rbpad = jnp.zeros((1, LANES), F32).at[0, :N_EXPERTS].set(rb)
    c2 = lambda i: (0, 0)
    return pl.pallas_call(
        _router_kernel,
        grid=(nt,),
        in_specs=[pl.BlockSpec((tm, d), lambda i: (i, 0)),
                  pl.BlockSpec((1, d), c2),
                  pl.BlockSpec((d, LANES), c2),
                  pl.BlockSpec((1, LANES), c2),
                  pl.BlockSpec((tm, tm), c2)],
        out_specs=[pl.BlockSpec((tm, d), lambda i: (i, 0)),
                   pl.BlockSpec((tm, LANES), lambda i: (i, 0)),
                   pl.BlockSpec((tm, LANES), lambda i: (i, 0)),
                   pl.BlockSpec((N_EXPERTS, tm), lambda i: (0, i)),
                   pl.BlockSpec((None, 8, LANES), lambda i: (i, 0, 0))],
        out_shape=[jax.ShapeDtypeStruct((n, d), BF16),
                   jax.ShapeDtypeStruct((n, LANES), F32),
                   jax.ShapeDtypeStruct((n, LANES), F32),
                   jax.ShapeDtypeStruct((N_EXPERTS, n), F32),
                   jax.ShapeDtypeStruct((nt, 8, LANES), jnp.int32)],
        compiler_params=_params(1),
        name="moe_router",
    )(x2, g, rpad, rbpad, tri)


MOE_SUB = 128
MOE_MOVE = 2 * MOE_SUB


def _moe_kernel(cnt_ref, h_ref, rankT_ref, rank_ref, comb_ref, wg_ref, wu_ref, wd_ref, x_ref, o_ref, hc_sc, oacc_sc):
    t, e, f = pl.program_id(0), pl.program_id(1), pl.program_id(2)
    nf = pl.num_programs(2)
    tm = h_ref.shape[0]
    nsub = (cnt_ref[t * N_EXPERTS + e] + (MOE_SUB - 1)) // MOE_SUB
    nmove = (nsub + 1) // 2

    @pl.when((e == 0) & (f == 0))
    def _():
        o_ref[...] = x_ref[...]

    @pl.when(f == 0)
    def _():
        rank_row = rankT_ref[...]

        def gather(s, c):
            r0 = pl.multiple_of(s * MOE_MOVE, MOE_MOVE)
            rows = (lax.broadcasted_iota(jnp.int32, (MOE_MOVE, 1), 0) + r0).astype(F32)
            onehot = (rows == rank_row).astype(BF16)
            hc_sc[pl.ds(r0, MOE_MOVE), :] = _dot(onehot, h_ref[...]).astype(BF16)
            oacc_sc[pl.ds(r0, MOE_MOVE), :] = jnp.zeros((MOE_MOVE, oacc_sc.shape[1]), F32)
            return c

        lax.fori_loop(0, nmove, gather, 0)

    def expert(n_rows, s, c):
        r0 = pl.multiple_of(s * n_rows, n_rows)
        rows = hc_sc[pl.ds(r0, n_rows), :]
        a = _dot(rows, wg_ref[...])
        act = (a * jax.nn.sigmoid(a) * _dot(rows, wu_ref[...])).astype(BF16)
        oacc_sc[pl.ds(r0, n_rows), :] += _dot(act, wd_ref[...])
        return c

    lax.fori_loop(0, nsub // 2, functools.partial(expert, MOE_MOVE), 0)
    lax.fori_loop(nsub // 2 * 2, nsub, functools.partial(expert, MOE_SUB), 0)

    @pl.when(f == nf - 1)
    def _():
        is_e = lax.broadcasted_iota(jnp.int32, (1, LANES), 1) == e
        rank_col = jnp.sum(jnp.where(is_e, rank_ref[...], 0.0), axis=1, keepdims=True)
        comb_col = jnp.sum(jnp.where(is_e, comb_ref[...], 0.0), axis=1, keepdims=True)

        def scatter(s, c):
            r0 = pl.multiple_of(s * MOE_MOVE, MOE_MOVE)
            cols = (lax.broadcasted_iota(jnp.int32, (1, MOE_MOVE), 1) + r0).astype(F32)
            onehot = (rank_col == cols).astype(BF16)
            y = _dot(onehot, oacc_sc[pl.ds(r0, MOE_MOVE), :].astype(BF16))
            o_ref[...] += comb_col * y
            return c

        lax.fori_loop(0, nmove, scatter, 0)


def _moe(counts, h, rankT, rank, comb, wg, wu, wd, x2, tm, fc=1408):
    n, d = x2.shape
    dff = wg.shape[2]
    grid_spec = pltpu.PrefetchScalarGridSpec(
        num_scalar_prefetch=1,
        grid=(n // tm, N_EXPERTS, dff // fc),
        in_specs=[pl.BlockSpec((tm, d), lambda t, e, f, c: (t, 0)),
                  pl.BlockSpec((None, 1, tm), lambda t, e, f, c: (e, 0, t)),
                  pl.BlockSpec((tm, LANES), lambda t, e, f, c: (t, 0)),
                  pl.BlockSpec((tm, LANES), lambda t, e, f, c: (t, 0)),
                  pl.BlockSpec((None, d, fc), lambda t, e, f, c: (e, 0, f)),
                  pl.BlockSpec((None, d, fc), lambda t, e, f, c: (e, 0, f)),
                  pl.BlockSpec((None, fc, d), lambda t, e, f, c: (e, f, 0)),
                  pl.BlockSpec((tm, d), lambda t, e, f, c: (t, 0))],
        out_specs=pl.BlockSpec((tm, d), lambda t, e, f, c: (t, 0)),
        scratch_shapes=[pltpu.VMEM((tm, d), BF16), pltpu.VMEM((tm, d), F32)],
    )
    return pl.pallas_call(
        _moe_kernel,
        grid_spec=grid_spec,
        out_shape=jax.ShapeDtypeStruct((n, d), F32),
        compiler_params=_params(3),
        name="moe_experts",
    )(counts, h, rankT.reshape(N_EXPERTS, 1, n), rank, comb, wg, wu, wd, x2)


def _permute_w_in(w):
    o = np.cumsum((0, NSA_WIDTH) + (KV_WIDTH,) * 6 + (3 * NSA_HEADS,))
    q, kc, vc, ks, vs, kw, vw, gts = (w[:, o[k]:o[k + 1]] for k in range(8))
    ret = w[:, o[8]:]
    pad = jnp.zeros((w.shape[0], LANES - 3 * NSA_HEADS), w.dtype)
    return jnp.concatenate([q, ks, kw, vs, vw, kc, vc, gts, pad, ret], axis=1).astype(BF16)


def _nsa_consts(T):
    ncp = T // CMP_STRIDE
    ns = T // SLC_LEN
    cs = np.arange(ncp) * CMP_STRIDE
    ss = np.arange(ns) * SLC_LEN
    ov = np.clip(np.minimum(cs[None, :] + CMP_LEN, ss[:, None] + SLC_LEN) - np.maximum(cs[None, :], ss[:, None]), 0, None)
    ovT = (ov.astype(np.float32) / CMP_LEN)
    ovT[:, ncp - 1] = 0.0
    h = np.arange(NSA_HEADS).reshape(NSA_KV_HEADS, NSA_GROUP) + 1
    slopes = np.exp2(-8.0 * h / NSA_HEADS).astype(np.float32)
    slopes = np.repeat(slopes, Q_BLOCK, axis=1)
    parts, rest = [], np.float64(LOG2E)
    for _ in range(3):
        part = np.float64(np.asarray(rest).astype(BF16))
        parts.append(part)
        rest = rest - part
    qaug = np.zeros((NSA_KV_HEADS, HEAD_DIM, GQ), np.float32)
    for k, part in enumerate(parts):
        qaug[:, k, :] = part * SLC_LEN * slopes
        qaug[:, 3 + k, :] = part * slopes
    kq = np.arange(Q_BLOCK)[:, None] - np.arange(Q_BLOCK)[None, :]
    causb = np.where(kq <= 0, 0.0, NEG).astype(np.float32)
    lowb = np.where(kq > 0, 0.0, NEG).astype(np.float32)
    return jnp.asarray(ovT, BF16), jnp.asarray(qaug, BF16), jnp.asarray(lowb), jnp.asarray(causb)


def _mixer(x2, B, T, norm_g, w_in, q_norm_g, k_norm_g, cmp_pos, w_cmp, ret_norm_g, w_out):
    ns = T // SLC_LEN
    q, kv, kc, vc, gt, pret = _inproj(x2, norm_g[None, :], _permute_w_in(w_in))
    qT, ks, kw, vsT, vwT, gT = _prep(q, kv, gt, q_norm_g[None, :], k_norm_g[1:3], B, T)
    wk, pk = _compress_weights(w_cmp[0], cmp_pos[0])
    wv, pv = _compress_weights(w_cmp[1], cmp_pos[1])
    kcmp, vcT = _compress(kc, vc, wk, wv, pk, pv, k_norm_g[0:1], B, T)
    ovT, qaug, lowb, causb = _nsa_consts(T)
    ocmp, sel, flags = _nsa_cmp(qT, qaug, kcmp, vcT, ovT, B, T)
    lists, counts = _nsa_steps(flags[:, :, :, 0, :].reshape(-1, ns), T // Q_BLOCK)
    kpad = jnp.zeros((WIN, 2 * HEAD_DIM), BF16).at[:, HEAD_DIM:HEAD_DIM + 3].set(-2.0 ** 100)
    kw = jnp.concatenate([jnp.broadcast_to(kpad, kw.shape[:2] + kpad.shape), kw], axis=2)
    vwT = jnp.pad(vwT, ((0, 0), (0, 0), (0, 0), (WIN, 0)))
    nsa = _nsa_main(lists, counts, qT, qaug, ks, vsT, kw, vwT, sel, gT, ocmp, lowb, causb, B, T)
    ret = _retention(pret, ret_norm_g[None, :], B, T)
    return _outproj(x2, nsa.reshape(B * T, NSA_WIDTH), ret, w_out.astype(BF16))


def _moe_layer(x2, norm_g, router, router_b, wg, wu, wd, tm=1024):
    tm = min(tm, x2.shape[0])
    h, rank, comb, rankT, cnt = _router(x2, norm_g[None, :], router, router_b, tm)
    counts = cnt[:, 0, :N_EXPERTS].reshape(-1)
    return _moe(counts, h, rankT, rank, comb, wg.astype(BF16), wu.astype(BF16), wd.astype(BF16), x2, tm)


def kernel(x, norm_mix_g, w_in, q_norm_g, k_norm_g, cmp_pos, w_cmp, ret_norm_g, w_out, norm_ffn_g,
           ffn_w_gate, ffn_w_up, ffn_w_down, moe_router, moe_router_b, moe_w_gate, moe_w_up, moe_w_down):
    B, T, D = x.shape
    depth = norm_mix_g.shape[0]
    x2 = x.reshape(B * T, D)
    for l in range(depth):
        x2 = _mixer(x2, B, T, norm_mix_g[l], w_in[l], q_norm_g[l], k_norm_g[l], cmp_pos[l], w_cmp[l],
                    ret_norm_g[l], w_out[l])
        j = l // 2
        if l % 2 == 0:
            x2 = _ffn(x2, norm_ffn_g[l][None, :], ffn_w_gate[j].astype(BF16), ffn_w_up[j].astype(BF16),
                      ffn_w_down[j].astype(BF16))
        else:
            x2 = _moe_layer(x2, norm_ffn_g[l], moe_router[j], moe_router_b[j], moe_w_gate[j], moe_w_up[j],
                            moe_w_down[j])
    return x2.reshape(B, T, D)
```

```python
import functools

import numpy as np
import jax
import jax.numpy as jnp
from jax import lax
from jax.experimental import pallas as pl
from jax.experimental.pallas import tpu as pltpu

F32 = jnp.float32
BF16 = jnp.bfloat16

HEAD_DIM = 64
NSA_HEADS = 8
NSA_KV_HEADS = 2
NSA_GROUP = NSA_HEADS // NSA_KV_HEADS
RET_HEADS = 8
RET_DK = 32
RET_DV = 64
NSA_WIDTH = NSA_HEADS * HEAD_DIM
RET_WIDTH = RET_HEADS * RET_DV
KV_WIDTH = NSA_KV_HEADS * HEAD_DIM
CMP_LEN = 32
CMP_STRIDE = 16
SLC_LEN = 64
SLC_TOPK = 16
WIN = 512
Q_BLOCK = 128
RET_CHUNK = 128
N_EXPERTS = 8
EPS = 1e-6
NEG = -1e30
BIG = 1e9
LANES = 128
GQ = NSA_GROUP * Q_BLOCK
KEY_STEP = 128
STEP_GROUP = 4
N_FORCED = 3
CMP_CHUNK = 128
CMP_TILES = 4
PREP_TILES = 4
MAIN_TILES = 4
LOOP_TILES = 2
CMP_TAIL = CMP_CHUNK + 8
WIN_KEYS = WIN + Q_BLOCK
V_ROWS = HEAD_DIM + 16
LOG2E = 1.4426950408889634
VMEM_LIMIT = 60 * 1024 * 1024

_C_Q = 0
_C_KV = _C_Q + NSA_WIDTH
_C_KC = _C_KV + 4 * KV_WIDTH
_C_VC = _C_KC + KV_WIDTH
_C_GT = _C_VC + KV_WIDTH
_C_RET = _C_GT + LANES
_RET_COLS = 2 * RET_HEADS * RET_DK + 2 * RET_WIDTH
_C_END = _C_RET + _RET_COLS


def _params(n_axes, vmem=VMEM_LIMIT):
    return pltpu.CompilerParams(dimension_semantics=("arbitrary",) * n_axes, vmem_limit_bytes=vmem)


def _dot(a, b):
    return jnp.dot(a, b, preferred_element_type=F32)


def _dot_nt(a, b):
    return lax.dot_general(a, b, (((1,), (1,)), ((), ())), preferred_element_type=F32)


def _rms(x, g):
    return x * lax.rsqrt(jnp.mean(x * x, axis=-1, keepdims=True) + EPS) * g


def _inproj_kernel(x_ref, g_ref, w_ref, q_ref, kv_ref, kc_ref, vc_ref, gt_ref, ret_ref):
    h = _rms(x_ref[...], g_ref[...]).astype(BF16)
    q_ref[...] = _dot(h, w_ref[:, _C_Q:_C_KV])
    kv_ref[...] = _dot(h, w_ref[:, _C_KV:_C_KC])
    kc_ref[...] = _dot(h, w_ref[:, _C_KC:_C_VC])
    vc_ref[...] = _dot(h, w_ref[:, _C_VC:_C_GT])
    gt_ref[...] = _dot(h, w_ref[:, _C_GT:_C_RET])
    ret_ref[...] = _dot(h, w_ref[:, _C_RET:_C_END])


def _inproj(x2, g, w, tm=1024):
    n, d = x2.shape
    widths = (NSA_WIDTH, 4 * KV_WIDTH, KV_WIDTH, KV_WIDTH, LANES, _RET_COLS)
    return pl.pallas_call(
        _inproj_kernel,
        grid=(n // tm,),
        in_specs=[pl.BlockSpec((tm, d), lambda i: (i, 0)),
                  pl.BlockSpec((1, d), lambda i: (0, 0)),
                  pl.BlockSpec((d, _C_END), lambda i: (0, 0))],
        out_specs=[pl.BlockSpec((tm, c), lambda i: (i, 0)) for c in widths],
        out_shape=[jax.ShapeDtypeStruct((n, c), F32) for c in widths],
        compiler_params=_params(1),
        name="inproj",
    )(x2, g, w)


def _group_rms(x, g, ones_ref):
    w = x.shape[1]
    ones = ones_ref[0:w, 0:w]
    hi, mid, lo = _split3(x * x)
    ms = (_dot(hi, ones) + _dot(mid, ones) + _dot(lo, ones)) * (1.0 / HEAD_DIM)
    return x * lax.rsqrt(ms + EPS) * g


def _group_ones():
    lane = np.arange(NSA_WIDTH) // HEAD_DIM
    return jnp.asarray(lane[:, None] == lane[None, :], BF16)


def _prep_kernel(q_ref, kv_ref, gt_ref, qg_ref, kg_ref, ones_ref, qT_ref, ks_ref, kw_ref, vsT_ref, vwT_ref, gT_ref):
    tiles = range(PREP_TILES)
    rows = [slice(u * Q_BLOCK, (u + 1) * Q_BLOCK) for u in tiles]
    n_tok = PREP_TILES * Q_BLOCK
    scale = HEAD_DIM ** -0.5 * LOG2E
    qn = _group_rms(q_ref[...], qg_ref[...], ones_ref) * scale
    qt = [qn[rows[u]].T for u in tiles]
    for g in range(NSA_KV_HEADS):
        for r in range(NSA_GROUP):
            h = g * NSA_GROUP + r
            for u in tiles:
                qT_ref[g, u, :, r * Q_BLOCK:(r + 1) * Q_BLOCK] = qt[u][h * HEAD_DIM:(h + 1) * HEAD_DIM, :].astype(BF16)
    kv = kv_ref[...]
    vst = [kv[rows[u], 2 * KV_WIDTH:3 * KV_WIDTH].T for u in tiles]
    vwt = [kv[rows[u], 3 * KV_WIDTH:4 * KV_WIDTH].T for u in tiles]
    gts = [jax.nn.sigmoid(gt_ref[rows[u], :].T[0:32, :]) for u in tiles]
    pos = pl.program_id(1) * n_tok + lax.broadcasted_iota(jnp.int32, (n_tok, HEAD_DIM), 0)
    col = lax.broadcasted_iota(jnp.int32, (n_tok, HEAD_DIM), 1)
    kpos = jnp.where(col < 3, pos // SLC_LEN, jnp.where(col < 6, pos % SLC_LEN, 0)).astype(F32)
    ones_row = (lax.broadcasted_iota(jnp.int32, (V_ROWS - HEAD_DIM, Q_BLOCK), 0) == 0).astype(F32)
    ks = _group_rms(kv[:, 0:KV_WIDTH], kg_ref[0:1, :], ones_ref)
    kw = _group_rms(kv[:, KV_WIDTH:2 * KV_WIDTH], kg_ref[1:2, :], ones_ref)
    for g in range(NSA_KV_HEADS):
        sl = slice(g * HEAD_DIM, (g + 1) * HEAD_DIM)
        ks_ref[g] = jnp.concatenate([ks[:, sl], kpos], axis=1).astype(BF16)
        kw_ref[g] = jnp.concatenate([kw[:, sl], kpos], axis=1).astype(BF16)
        for u in tiles:
            vsT_ref[g, :, rows[u]] = jnp.concatenate([vst[u][sl, :], ones_row], axis=0).astype(BF16)
            vwT_ref[g, :, rows[u]] = jnp.concatenate([vwt[u][sl, :], ones_row], axis=0).astype(BF16)
    for u in tiles:
        gT_ref[u] = gts[u]


def _prep(q, kv, gt, qg, kg, B, T):
    nq = T // Q_BLOCK
    G = NSA_KV_HEADS
    n_tok = PREP_TILES * Q_BLOCK
    steps = nq // PREP_TILES
    row = lambda b, i: (b * steps + i, 0)
    return pl.pallas_call(
        _prep_kernel,
        grid=(B, steps),
        in_specs=[pl.BlockSpec((n_tok, NSA_WIDTH), row),
                  pl.BlockSpec((n_tok, 4 * KV_WIDTH), row),
                  pl.BlockSpec((n_tok, LANES), row),
                  pl.BlockSpec((1, NSA_WIDTH), lambda b, i: (0, 0)),
                  pl.BlockSpec((2, KV_WIDTH), lambda b, i: (0, 0)),
                  pl.BlockSpec((NSA_WIDTH, NSA_WIDTH), lambda b, i: (0, 0))],
        out_specs=[pl.BlockSpec((None, G, PREP_TILES, HEAD_DIM, GQ), lambda b, i: (b, 0, i, 0, 0)),
                   pl.BlockSpec((None, G, n_tok, 2 * HEAD_DIM), lambda b, i: (b, 0, i, 0)),
                   pl.BlockSpec((None, G, n_tok, 2 * HEAD_DIM), lambda b, i: (b, 0, i, 0)),
                   pl.BlockSpec((None, G, V_ROWS, n_tok), lambda b, i: (b, 0, 0, i)),
                   pl.BlockSpec((None, G, V_ROWS, n_tok), lambda b, i: (b, 0, 0, i)),
                   pl.BlockSpec((None, PREP_TILES, 32, Q_BLOCK), lambda b, i: (b, i, 0, 0))],
        out_shape=[jax.ShapeDtypeStruct((B, G, nq, HEAD_DIM, GQ), BF16),
                   jax.ShapeDtypeStruct((B, G, T, 2 * HEAD_DIM), BF16),
                   jax.ShapeDtypeStruct((B, G, T, 2 * HEAD_DIM), BF16),
                   jax.ShapeDtypeStruct((B, G, V_ROWS, T), BF16),
                   jax.ShapeDtypeStruct((B, G, V_ROWS, T), BF16),
                   jax.ShapeDtypeStruct((B, nq, 32, Q_BLOCK), F32)],
        compiler_params=_params(2),
        name="nsa_prep",
    )(q, kv, gt, jnp.tile(qg, (1, NSA_HEADS)), jnp.tile(kg, (1, NSA_KV_HEADS)), _group_ones())


def _compress_kernel(kc_ref, vc_ref, wk_ref, wv_ref, pk_ref, pv_ref, kg_ref, kcmp_ref, vcT_ref):
    ncp = kc_ref.shape[0] // CMP_STRIDE

    def comp(a_ref, w_ref, p_ref):
        lo = jnp.zeros((ncp, KV_WIDTH), F32)
        hi = jnp.zeros((ncp, KV_WIDTH), F32)
        for l in range(CMP_STRIDE):
            a = a_ref[pl.ds(l, ncp, stride=CMP_STRIDE), :]
            lo += _dot((a + p_ref[0, l:l + 1, :]).astype(BF16), w_ref[0, l])
            hi += _dot((a + p_ref[1, l:l + 1, :]).astype(BF16), w_ref[1, l])
        return lo + pltpu.roll(hi, ncp - 1, 0)

    k = comp(kc_ref, wk_ref, pk_ref)
    v = comp(vc_ref, wv_ref, pv_ref).T
    cend = lax.broadcasted_iota(jnp.int32, (ncp, HEAD_DIM), 0) * CMP_STRIDE + (CMP_LEN - 1)
    col = lax.broadcasted_iota(jnp.int32, (ncp, HEAD_DIM), 1)
    kpos = jnp.where(col < 3, cend // SLC_LEN, jnp.where(col < 6, cend % SLC_LEN, 0)).astype(F32)
    for g in range(NSA_KV_HEADS):
        sl = slice(g * HEAD_DIM, (g + 1) * HEAD_DIM)
        kcmp_ref[g] = jnp.concatenate([_rms(k[:, sl], kg_ref[...]), kpos], axis=1).astype(BF16)
        vcT_ref[g] = v[sl, :].astype(BF16)


def _compress(kc, vc, wk, wv, pk, pv, kg, B, T):
    ncp = T // CMP_STRIDE
    G = NSA_KV_HEADS
    const4 = lambda b: (0, 0, 0, 0)
    const3 = lambda b: (0, 0, 0)
    const2 = lambda b: (0, 0)
    return pl.pallas_call(
        _compress_kernel,
        grid=(B,),
        in_specs=[pl.BlockSpec((T, KV_WIDTH), lambda b: (b, 0)),
                  pl.BlockSpec((T, KV_WIDTH), lambda b: (b, 0)),
                  pl.BlockSpec((2, CMP_STRIDE, KV_WIDTH, KV_WIDTH), const4),
                  pl.BlockSpec((2, CMP_STRIDE, KV_WIDTH, KV_WIDTH), const4),
                  pl.BlockSpec((2, CMP_STRIDE, KV_WIDTH), const3),
                  pl.BlockSpec((2, CMP_STRIDE, KV_WIDTH), const3),
                  pl.BlockSpec((1, HEAD_DIM), const2)],
        out_specs=[pl.BlockSpec((None, G, ncp, 2 * HEAD_DIM), lambda b: (b, 0, 0, 0)),
                   pl.BlockSpec((None, G, HEAD_DIM, ncp), lambda b: (b, 0, 0, 0))],
        out_shape=[jax.ShapeDtypeStruct((B, G, ncp, 2 * HEAD_DIM), BF16),
                   jax.ShapeDtypeStruct((B, G, HEAD_DIM, ncp), BF16)],
        compiler_params=_params(1),
        name="nsa_compress",
    )(kc, vc, wk, wv, pk, pv, kg)


def _compress_weights(w, pos):
    G = NSA_KV_HEADS
    w4 = w.reshape(2, CMP_STRIDE, HEAD_DIM, HEAD_DIM)
    eye = jnp.eye(G, dtype=w.dtype)
    wbd = jnp.einsum('hlde,gk->hlgdke', w4, eye).reshape(2, CMP_STRIDE, KV_WIDTH, KV_WIDTH)
    p = pos.reshape(2, CMP_STRIDE, 1, HEAD_DIM)
    p = jnp.broadcast_to(p, (2, CMP_STRIDE, G, HEAD_DIM)).reshape(2, CMP_STRIDE, KV_WIDTH)
    return wbd.astype(BF16), p


def _split3(x):
    hi = x.astype(BF16)
    r = x - hi.astype(F32)
    mid = r.astype(BF16)
    lo = (r - mid.astype(F32)).astype(BF16)
    return hi, mid, lo


def _nsa_cmp_kernel(qT_ref, qaug_ref, kc_ref, vcT_ref, ovT_ref, ocmp_ref, sel_ref, flag_ref, *, n_sel):
    ncp = kc_ref.shape[0]
    ns = ovT_ref.shape[0]
    tiles = range(CMP_TILES)
    i0 = pl.program_id(2) * CMP_TILES
    lane = lax.broadcasted_iota(jnp.int32, (1, GQ), 1)
    q = [jnp.concatenate([qT_ref[u], qaug_ref[...]], axis=0) for u in tiles]
    t_row = [(i0 + u) * Q_BLOCK + (lane & (Q_BLOCK - 1)) for u in tiles]
    has_cmp = [(t_row[u] >= CMP_LEN - 1).astype(F32) for u in tiles]
    tq = [(i0 + u) * Q_BLOCK + lax.broadcasted_iota(jnp.int32, (1, Q_BLOCK), 1) for u in tiles]
    cur = [tq[u] // SLC_LEN for u in tiles]

    def prefix(rows):
        nsk = rows * CMP_STRIDE // SLC_LEN
        tail0 = max(rows - CMP_TAIL, 0)
        kc = kc_ref[0:rows, :]
        s = [_dot(kc, q[u]) for u in tiles]
        cend = (lax.broadcasted_iota(jnp.int32, (rows - tail0, 1), 0) + tail0) * CMP_STRIDE + (CMP_LEN - 1)
        tail = [jnp.where(t_row[u] >= cend, s[u][tail0:], NEG) for u in tiles]
        s = [jnp.concatenate([s[u][0:tail0], tail[u]], axis=0) if tail0 else tail[u] for u in tiles]
        m = [jnp.max(s[u], axis=0, keepdims=True) for u in tiles]
        e = [jnp.exp2(s[u] - m[u]) for u in tiles]
        p = [e[u] * (has_cmp[u] / jnp.sum(e[u], axis=0, keepdims=True)) for u in tiles]
        vc = vcT_ref[:, 0:rows]
        for u in tiles:
            ocmp_ref[u] = _dot(vc, p[u].astype(BF16))

        ps = [p[u][:, 0:Q_BLOCK] for u in tiles]
        for r in range(1, NSA_GROUP):
            ps = [ps[u] + p[u][:, r * Q_BLOCK:(r + 1) * Q_BLOCK] for u in tiles]
        ov = ovT_ref[0:nsk, 0:rows]
        split = [_split3(ps[u]) for u in tiles]
        imp = [_dot(ov, split[u][0]) + _dot(ov, split[u][1]) + _dot(ov, split[u][2]) for u in tiles]

        blk = lax.broadcasted_iota(jnp.int32, (nsk, 1), 0)
        forced = [(blk == 0) | (blk == cur[u]) | (blk == cur[u] - 1) for u in tiles]
        valid = [blk * SLC_LEN <= tq[u] for u in tiles]
        imp = [jnp.where(forced[u], -3e38, jnp.where(valid[u], imp[u], -BIG)) for u in tiles]
        blk_f = blk.astype(F32)
        sel = [forced[u].astype(F32) for u in tiles]
        for _ in range(n_sel - N_FORCED):
            mx = [jnp.max(imp[u], axis=0, keepdims=True) for u in tiles]
            idx = [jnp.min(jnp.where(imp[u] == mx[u], blk_f, float(ns)), axis=0, keepdims=True) for u in tiles]
            pick = [blk_f == idx[u] for u in tiles]
            sel = [jnp.where(pick[u], 1.0, sel[u]) for u in tiles]
            imp = [jnp.where(pick[u], -3e38, imp[u]) for u in tiles]
        ones = jnp.ones((8, Q_BLOCK), BF16)
        for u in tiles:
            sel_ref[u, 0:nsk, :] = sel[u]
            cnt = _dot_nt(ones, sel[u].astype(BF16))
            flag_ref[u, :, 0:nsk] = (cnt > 0).astype(jnp.int32)
            if nsk < ns:
                sel_ref[u, nsk:, :] = jnp.zeros((ns - nsk, Q_BLOCK), F32)
                flag_ref[u, :, nsk:] = jnp.zeros((8, ns - nsk), jnp.int32)

    n_variants = ncp // CMP_CHUNK
    last = i0 + CMP_TILES - 1
    variant = (last * (Q_BLOCK // CMP_STRIDE) + (Q_BLOCK // CMP_STRIDE - 2)) // CMP_CHUNK
    for k in range(n_variants):
        pl.when(variant == k)(functools.partial(prefix, (k + 1) * CMP_CHUNK))


def _nsa_cmp(qT, qaug, kcmp, vcT, ovT, B, T):
    G = NSA_KV_HEADS
    nq = T // Q_BLOCK
    ncp = T // CMP_STRIDE
    ns = T // SLC_LEN
    n_sel = min(SLC_TOPK, ns)
    assert ncp % CMP_CHUNK == 0 and n_sel > N_FORCED and nq % CMP_TILES == 0
    tile = lambda b, g, i: (b, g, i, 0, 0)
    return pl.pallas_call(
        functools.partial(_nsa_cmp_kernel, n_sel=n_sel),
        grid=(B, G, nq // CMP_TILES),
        in_specs=[pl.BlockSpec((None, None, CMP_TILES, HEAD_DIM, GQ), tile),
                  pl.BlockSpec((None, HEAD_DIM, GQ), lambda b, g, i: (g, 0, 0)),
                  pl.BlockSpec((None, None, ncp, 2 * HEAD_DIM), lambda b, g, i: (b, g, 0, 0)),
                  pl.BlockSpec((None, None, HEAD_DIM, ncp), lambda b, g, i: (b, g, 0, 0)),
                  pl.BlockSpec((ns, ncp), lambda b, g, i: (0, 0))],
        out_specs=[pl.BlockSpec((None, None, CMP_TILES, HEAD_DIM, GQ), tile),
                   pl.BlockSpec((None, None, CMP_TILES, ns, Q_BLOCK), tile),
                   pl.BlockSpec((None, None, CMP_TILES, 8, ns), tile)],
        out_shape=[jax.ShapeDtypeStruct((B, G, nq, HEAD_DIM, GQ), F32),
                   jax.ShapeDtypeStruct((B, G, nq, ns, Q_BLOCK), F32),
                   jax.ShapeDtypeStruct((B, G, nq, 8, ns), jnp.int32)],
        compiler_params=_params(3),
        name="nsa_cmp",
    )(qT, qaug, kcmp, vcT, ovT)


def _nsa_main_kernel(list_ref, cnt_ref, qT_ref, qaug_ref, ks_ref, vsT_ref, kw_ref, vwT_ref, sel_ref, gT_ref, ocmp_ref,
                     lowb_ref, causb_ref, out_ref, m_sc, acc_sc, win_sc):
    b, g = pl.program_id(0), pl.program_id(1)
    tiles = range(MAIN_TILES)
    i = [pl.program_id(2) * MAIN_TILES + u for u in tiles]
    tile_id = [(b * pl.num_programs(1) + g) * (pl.num_programs(2) * MAIN_TILES) + i[u] for u in tiles]
    n_steps = sel_ref.shape[1] // 2
    q = [jnp.concatenate([qT_ref[u], qaug_ref[...]], axis=0) for u in tiles]
    k0 = [pl.multiple_of(i[u] * Q_BLOCK, Q_BLOCK) for u in tiles]

    def sel_bias(u, j, valid):
        def row(r):
            picked = (sel_ref[u, pl.ds(r, 1), :] > 0.5) & valid
            return jnp.concatenate([jnp.where(picked, 0.0, NEG)] * NSA_GROUP, axis=1)
        return row(2 * j), row(2 * j + 1)

    def add_sel_bias(s, ba, bb):
        return jnp.concatenate([s[0:SLC_LEN] + ba, s[SLC_LEN:] + bb], axis=0)

    lowb = jnp.concatenate([lowb_ref[...]] * NSA_GROUP, axis=1)
    causb = jnp.concatenate([causb_ref[...]] * NSA_GROUP, axis=1)

    bias_d = [sel_bias(u, i[u], True) for u in tiles]
    sd = [_dot(ks_ref[pl.ds(k0[u], KEY_STEP), :], q[u]) for u in tiles]
    sw = [_dot(kw_ref[pl.ds(k0[u], WIN_KEYS), :], q[u]) for u in tiles]
    sd = [add_sel_bias(sd[u], *bias_d[u]) + causb for u in tiles]
    sw = [jnp.concatenate([sw[u][0:Q_BLOCK] + lowb, sw[u][Q_BLOCK:WIN], sw[u][WIN:] + causb], axis=0) for u in tiles]
    md = [jnp.max(sd[u], axis=0, keepdims=True) for u in tiles]
    mw = [jnp.max(sw[u], axis=0, keepdims=True) for u in tiles]
    accd = [_dot(vsT_ref[:, pl.ds(k0[u], KEY_STEP)], jnp.exp2((sd[u] - md[u]).astype(BF16))) for u in tiles]
    ow = [_dot(vwT_ref[:, pl.ds(k0[u], WIN_KEYS)], jnp.exp2((sw[u] - mw[u]).astype(BF16))) for u in tiles]
    for u in tiles:
        m_sc[u] = md[u]
        acc_sc[u] = accd[u]
        win_sc[u] = ow[u][0:HEAD_DIM] / ow[u][HEAD_DIM:HEAD_DIM + 1]

    def scores(u, t):
        ks, vs, biases = [], [], []
        for x in range(STEP_GROUP):
            j = list_ref[tile_id[u] * n_steps + t * STEP_GROUP + x]
            valid = j >= 0
            j = jnp.maximum(j, 0)
            kj = pl.multiple_of(j * KEY_STEP, KEY_STEP)
            ks.append(ks_ref[pl.ds(kj, KEY_STEP), :])
            vs.append(vsT_ref[:, pl.ds(kj, KEY_STEP)])
            biases.append(sel_bias(u, j, valid))
        s = _dot(jnp.concatenate(ks, axis=0), q[u])
        s = jnp.concatenate([add_sel_bias(s[x * KEY_STEP:(x + 1) * KEY_STEP], *biases[x])
                             for x in range(STEP_GROUP)], axis=0)
        return s, jnp.max(s, axis=0, keepdims=True), jnp.concatenate(vs, axis=1)

    def accumulate(u, s, smax, vcat):
        m_old = m_sc[u]
        m_new = jnp.maximum(m_old, smax)
        alpha = jnp.exp2(m_old - m_new)
        acc_sc[u] = alpha * acc_sc[u] + _dot(vcat, jnp.exp2((s - m_new).astype(BF16)))
        m_sc[u] = m_new

    def run(work, t, carry):
        staged = [(u, scores(u, t * mult + off)) for (u, mult, off) in work]
        for u, args in staged:
            accumulate(u, *args)
        return carry

    for u0 in range(0, MAIN_TILES, LOOP_TILES):
        us = range(u0, u0 + LOOP_TILES)
        n_groups = functools.reduce(
            jnp.maximum, [(cnt_ref[tile_id[u]] + (STEP_GROUP - 1)) // STEP_GROUP for u in us])
        lax.fori_loop(0, n_groups // 2, functools.partial(run, [(u, 2, off) for off in (0, 1) for u in us]), 0)
        lax.fori_loop(n_groups // 2 * 2, n_groups, functools.partial(run, [(u, 1, 0) for u in us]), 0)

    def gate(u, k):
        rows = [gT_ref[u, pl.ds(g * (NSA_GROUP * 3) + r * 3 + k, 1), :] for r in range(NSA_GROUP)]
        return jnp.concatenate(rows, axis=1)

    o_slc = [acc_sc[u, 0:HEAD_DIM, :] / acc_sc[u, HEAD_DIM:HEAD_DIM + 1, :] for u in tiles]
    o = [gate(u, 0) * ocmp_ref[u] + gate(u, 1) * o_slc[u] + gate(u, 2) * win_sc[u] for u in tiles]
    o = [jnp.concatenate([o[u], jnp.zeros_like(o[u])], axis=0) for u in tiles]
    for r in range(NSA_GROUP):
        ot = [o[u][:, r * Q_BLOCK:(r + 1) * Q_BLOCK].T[:, 0:HEAD_DIM] for u in tiles]
        for u in tiles:
            out_ref[u * Q_BLOCK:(u + 1) * Q_BLOCK, r * HEAD_DIM:(r + 1) * HEAD_DIM] = ot[u]


def _nsa_steps_kernel(flagT_ref, pairT_ref, list_ref, cnt_ref, *, nq):
    n_steps, nt = list_ref.shape
    need = _dot(pairT_ref[...], flagT_ref[...].astype(BF16)) > 0
    step = lax.broadcasted_iota(jnp.int32, (n_steps, 1), 0)
    own = lax.broadcasted_iota(jnp.int32, (1, nt), 1) % nq
    need = need & (step < own)
    need_f = need.astype(F32)
    earlier = (lax.broadcasted_iota(jnp.int32, (n_steps, n_steps), 1) < step).astype(BF16)
    slot = _dot(earlier, need_f.astype(BF16))
    total = jnp.sum(need_f, axis=0, keepdims=True)
    cnt_ref[...] = jnp.broadcast_to(total, cnt_ref.shape).astype(jnp.int32)
    step_f = step.astype(F32)
    for p in range(n_steps):
        val = jnp.sum(jnp.where(need & (slot == p), step_f, 0.0), axis=0, keepdims=True)
        list_ref[p:p + 1, :] = jnp.where(total > p, val, -1.0).astype(jnp.int32)


def _nsa_steps(flags, nq):
    nt, ns = flags.shape
    n_steps = ns // 2
    pairT = jnp.asarray(np.arange(n_steps)[:, None] == np.arange(ns)[None, :] // 2, BF16)
    lists, counts = pl.pallas_call(
        functools.partial(_nsa_steps_kernel, nq=nq),
        out_shape=[jax.ShapeDtypeStruct((n_steps, nt), jnp.int32), jax.ShapeDtypeStruct((8, nt), jnp.int32)],
        name="nsa_steps",
    )(flags.T.astype(F32), pairT)
    return lists.T.reshape(-1), counts[0]


def _nsa_main(lists, counts, qT, qaug, ks, vsT, kw, vwT, sel, gT, ocmp, lowb, causb, B, T):
    G = NSA_KV_HEADS
    nq = T // Q_BLOCK
    ns = T // SLC_LEN
    whole = lambda b, g, i, *_: (b, g, 0, 0)
    tile = lambda b, g, i, *_: (b, g, i, 0, 0)
    const = lambda b, g, i, *_: (0, 0)
    grid_spec = pltpu.PrefetchScalarGridSpec(
        num_scalar_prefetch=2,
        grid=(B, G, nq // MAIN_TILES),
        in_specs=[pl.BlockSpec((None, None, MAIN_TILES, HEAD_DIM, GQ), tile),
                  pl.BlockSpec((None, HEAD_DIM, GQ), lambda b, g, i, *_: (g, 0, 0)),
                  pl.BlockSpec((None, None, T, 2 * HEAD_DIM), whole),
                  pl.BlockSpec((None, None, V_ROWS, T), whole),
                  pl.BlockSpec((None, None, T + WIN, 2 * HEAD_DIM), whole),
                  pl.BlockSpec((None, None, V_ROWS, T + WIN), whole),
                  pl.BlockSpec((None, None, MAIN_TILES, ns, Q_BLOCK), tile),
                  pl.BlockSpec((None, MAIN_TILES, 32, Q_BLOCK), lambda b, g, i, *_: (b, i, 0, 0)),
                  pl.BlockSpec((None, None, MAIN_TILES, HEAD_DIM, GQ), tile),
                  pl.BlockSpec((Q_BLOCK, Q_BLOCK), const),
                  pl.BlockSpec((Q_BLOCK, Q_BLOCK), const)],
        out_specs=pl.BlockSpec((None, MAIN_TILES * Q_BLOCK, NSA_GROUP * HEAD_DIM), lambda b, g, i, *_: (b, i, g)),
        scratch_shapes=[pltpu.VMEM((MAIN_TILES, 1, GQ), F32), pltpu.VMEM((MAIN_TILES, V_ROWS, GQ), F32),
                        pltpu.VMEM((MAIN_TILES, HEAD_DIM, GQ), F32)],
    )
    return pl.pallas_call(
        _nsa_main_kernel,
        grid_spec=grid_spec,
        out_shape=jax.ShapeDtypeStruct((B, T, NSA_WIDTH), F32),
        compiler_params=_params(3),
        name="nsa_main",
    )(lists, counts, qT, qaug, ks, vsT, kw, vwT, sel, gT, ocmp, lowb, causb)


def _ret_kernel(p_ref, decay_ref, xi_ref, zeta_ref, gch_ref, ng_ref, ones_ref, out_ref, state_ref):
    @pl.when(pl.program_id(0) == 0)
    def _():
        state_ref[...] = jnp.zeros(state_ref.shape, F32)

    rows = range(p_ref.shape[0])
    kw = RET_HEADS * RET_DK
    p = [p_ref[b] for b in rows]
    rq = [p[b][:, 0:kw] * (RET_DK ** -0.5) for b in rows]
    rk = [p[b][:, kw:2 * kw] for b in rows]
    rkT = [rk[b].T for b in rows]
    rv = [p[b][:, 2 * kw:2 * kw + RET_WIDTH] for b in rows]
    xi = xi_ref[...]
    outs = [[] for _ in rows]
    for h in range(RET_HEADS):
        dk = slice(h * RET_DK, (h + 1) * RET_DK)
        dv = slice(h * RET_DV, (h + 1) * RET_DV)
        st = [state_ref[b, h] for b in rows]
        inner = [_dot_nt(rq[b][:, dk], rk[b][:, dk]) * decay_ref[h] for b in rows]
        o = [_dot(inner[b], rv[b][:, dv]) + _dot(rq[b][:, dk], st[b]) * xi[:, h:h + 1] for b in rows]
        for b in rows:
            state_ref[b, h] = (st[b] * gch_ref[h:h + 1, 0:1]
                               + _dot(rkT[b][dk, :] * zeta_ref[h:h + 1, :], rv[b][:, dv]))
            outs[b].append(o[b])
    normed = [_group_rms(jnp.concatenate(outs[b], axis=1), ng_ref[...], ones_ref) for b in rows]
    for b in rows:
        rg = p[b][:, 2 * kw + RET_WIDTH:2 * kw + 2 * RET_WIDTH]
        out_ref[b] = normed[b] * (rg * jax.nn.sigmoid(rg))


def _ret_consts():
    H, C = RET_HEADS, RET_CHUNK
    log_g = jnp.log1p(-jnp.exp2(-5.0 - jnp.arange(H, dtype=F32)))
    idx = jnp.arange(C, dtype=F32)
    diff = idx[:, None] - idx[None, :]
    decay = jnp.where(diff >= 0, jnp.exp(jnp.maximum(diff, 0.0) * log_g[:, None, None]), 0.0)
    zeta = jnp.exp((C - 1 - idx) * log_g[:, None])
    xi = jnp.exp((idx + 1) * log_g[:, None]).T
    g_chunk = jnp.broadcast_to(jnp.exp(C * log_g)[:, None], (H, LANES))
    return decay, xi, zeta, g_chunk


def _retention(pret, ng, B, T):
    nch = T // RET_CHUNK
    decay, xi, zeta, gch = _ret_consts()
    c2 = lambda c: (0, 0)
    out = pl.pallas_call(
        _ret_kernel,
        grid=(nch,),
        in_specs=[pl.BlockSpec((B, RET_CHUNK, _RET_COLS), lambda c: (0, c, 0)),
                  pl.BlockSpec((RET_HEADS, RET_CHUNK, RET_CHUNK), lambda c: (0, 0, 0)),
                  pl.BlockSpec((RET_CHUNK, RET_HEADS), c2),
                  pl.BlockSpec((RET_HEADS, RET_CHUNK), c2),
                  pl.BlockSpec((RET_HEADS, LANES), c2),
                  pl.BlockSpec((1, RET_WIDTH), c2),
                  pl.BlockSpec((RET_WIDTH, RET_WIDTH), c2)],
        out_specs=pl.BlockSpec((B, RET_CHUNK, RET_WIDTH), lambda c: (0, c, 0)),
        out_shape=jax.ShapeDtypeStruct((B, T, RET_WIDTH), F32),
        scratch_shapes=[pltpu.VMEM((B, RET_HEADS, RET_DK, RET_DV), F32)],
        compiler_params=_params(1),
        name="retention",
    )(pret.reshape(B, T, _RET_COLS), decay, xi, zeta, gch, ng, _group_ones())
    return out.reshape(B * T, RET_WIDTH)


def _outproj_kernel(x_ref, nsa_ref, ret_ref, w_ref, o_ref):
    o_ref[...] = (x_ref[...] + _dot(nsa_ref[...].astype(BF16), w_ref[0:NSA_WIDTH, :])
                  + _dot(ret_ref[...].astype(BF16), w_ref[NSA_WIDTH:, :]))


def _outproj(x2, nsa, ret, w, tm=512):
    n, d = x2.shape
    return pl.pallas_call(
        _outproj_kernel,
        grid=(n // tm,),
        in_specs=[pl.BlockSpec((tm, d), lambda i: (i, 0)),
                  pl.BlockSpec((tm, NSA_WIDTH), lambda i: (i, 0)),
                  pl.BlockSpec((tm, RET_WIDTH), lambda i: (i, 0)),
                  pl.BlockSpec((NSA_WIDTH + RET_WIDTH, d), lambda i: (0, 0))],
        out_specs=pl.BlockSpec((tm, d), lambda i: (i, 0)),
        out_shape=jax.ShapeDtypeStruct((n, d), F32),
        compiler_params=_params(1),
        name="outproj",
    )(x2, nsa, ret, w)


def _ffn_kernel(x_ref, g_ref, wg_ref, wu_ref, wd_ref, o_ref, h_sc):
    f = pl.program_id(1)

    @pl.when(f == 0)
    def _():
        x = x_ref[...]
        h_sc[...] = _rms(x, g_ref[...]).astype(BF16)
        o_ref[...] = x

    h = h_sc[...]
    a = _dot(h, wg_ref[...])
    act = (a * jax.nn.sigmoid(a) * _dot(h, wu_ref[...])).astype(BF16)
    o_ref[...] += _dot(act, wd_ref[...])


def _ffn(x2, g, wg, wu, wd, tm=1024, fc=1408):
    n, d = x2.shape
    dff = wg.shape[1]
    return pl.pallas_call(
        _ffn_kernel,
        grid=(n // tm, dff // fc),
        in_specs=[pl.BlockSpec((tm, d), lambda i, f: (i, 0)),
                  pl.BlockSpec((1, d), lambda i, f: (0, 0)),
                  pl.BlockSpec((d, fc), lambda i, f: (0, f)),
                  pl.BlockSpec((d, fc), lambda i, f: (0, f)),
                  pl.BlockSpec((fc, d), lambda i, f: (f, 0))],
        out_specs=pl.BlockSpec((tm, d), lambda i, f: (i, 0)),
        out_shape=jax.ShapeDtypeStruct((n, d), F32),
        scratch_shapes=[pltpu.VMEM((tm, d), BF16)],
        compiler_params=_params(2),
        name="ffn_dense",
    )(x2, g, wg, wu, wd)


def _router_kernel(x_ref, g_ref, r_ref, rb_ref, tri_ref, h_ref, rank_ref, comb_ref, rankT_ref, cnt_ref):
    h = _rms(x_ref[...], g_ref[...])
    h_ref[...] = h.astype(BF16)
    hh, hm, hl = _split3(h)
    rh, rm, rl = _split3(r_ref[...])
    logits = (_dot(hh, rh) + (_dot(hh, rm) + _dot(hm, rh)) + (_dot(hh, rl) + _dot(hm, rm) + _dot(hl, rh))
              + rb_ref[...])
    lane = lax.broadcasted_iota(jnp.int32, logits.shape, 1).astype(F32)
    logits = jnp.where(lane < N_EXPERTS, logits, NEG)
    m1 = jnp.max(logits, axis=1, keepdims=True)
    i1 = jnp.min(jnp.where(logits == m1, lane, float(LANES)), axis=1, keepdims=True)
    l2 = jnp.where(lane == i1, NEG, logits)
    m2 = jnp.max(l2, axis=1, keepdims=True)
    i2 = jnp.min(jnp.where(l2 == m2, lane, float(LANES)), axis=1, keepdims=True)
    e2 = jnp.exp(m2 - m1)
    w1 = 1.0 / (1.0 + e2)
    w2 = e2 / (1.0 + e2)
    use1, use2 = lane == i1, lane == i2
    comb_ref[...] = jnp.where(use1, w1, 0.0) + jnp.where(use2, w2, 0.0)
    use = (use1 | use2).astype(F32)
    rank = jnp.where(use > 0, _dot(tri_ref[...], use.astype(BF16)), -1.0)
    rank_ref[...] = rank
    rankT_ref[...] = rank.T[0:N_EXPERTS, :]
    cnt_ref[...] = jnp.broadcast_to(jnp.sum(use, axis=0, keepdims=True), cnt_ref.shape).astype(jnp.int32)


def _router(x2, g, router, rb, tm):
    n, d = x2.shape
    nt = n // tm
    tri = (jnp.arange(tm)[:, None] > jnp.arange(tm)[None, :]).astype(BF16)
    rpad = jnp.zeros((d, LANES), F32).at[:, :N_EXPERTS].set(router)
    rbpad = jnp.zeros((1, LANES), F32).at[0, :N_EXPERTS].set(rb)
    c2 = lambda i: (0, 0)
    return pl.pallas_call(
        _router_kernel,
        grid=(nt,),
        in_specs=[pl.BlockSpec((tm, d), lambda i: (i, 0)),
                  pl.BlockSpec((1, d), c2),
                  pl.BlockSpec((d, LANES), c2),
                  pl.BlockSpec((1, LANES), c2),
                  pl.BlockSpec((tm, tm), c2)],
        out_specs=[pl.BlockSpec((tm, d), lambda i: (i, 0)),
                   pl.BlockSpec((tm, LANES), lambda i: (i, 0)),
                   pl.BlockSpec((tm, LANES), lambda i: (i, 0)),
                   pl.BlockSpec((N_EXPERTS, tm), lambda i: (0, i)),
                   pl.BlockSpec((None, 8, LANES), lambda i: (i, 0, 0))],
        out_shape=[jax.ShapeDtypeStruct((n, d), BF16),
                   jax.ShapeDtypeStruct((n, LANES), F32),
                   jax.ShapeDtypeStruct((n, LANES), F32),
                   jax.ShapeDtypeStruct((N_EXPERTS, n), F32),
                   jax.ShapeDtypeStruct((nt, 8, LANES), jnp.int32)],
        compiler_params=_params(1),
        name="moe_router",
    )(x2, g, rpad, rbpad, tri)


MOE_SUB = 144
MOE_MOVE = 2 * MOE_SUB


def _moe_kernel(cnt_ref, h_ref, rankT_ref, rank_ref, comb_ref, wg_ref, wu_ref, wd_ref, x_ref, o_ref, hc_sc, oacc_sc):
    t, e, f = pl.program_id(0), pl.program_id(1), pl.program_id(2)
    nf = pl.num_programs(2)
    tm = h_ref.shape[0]
    nsub = (cnt_ref[t * N_EXPERTS + e] + (MOE_SUB - 1)) // MOE_SUB
    nmove = (nsub + 1) // 2

    @pl.when((e == 0) & (f == 0))
    def _():
        o_ref[...] = x_ref[...]

    @pl.when(f == 0)
    def _():
        rank_row = rankT_ref[...]

        def gather(s, c):
            r0 = pl.multiple_of(s * MOE_MOVE, MOE_MOVE)
            rows = (lax.broadcasted_iota(jnp.int32, (MOE_MOVE, 1), 0) + r0).astype(F32)
            onehot = (rows == rank_row).astype(BF16)
            hc_sc[pl.ds(r0, MOE_MOVE), :] = _dot(onehot, h_ref[...]).astype(BF16)
            oacc_sc[pl.ds(r0, MOE_MOVE), :] = jnp.zeros((MOE_MOVE, oacc_sc.shape[1]), F32)
            return c

        lax.fori_loop(0, nmove, gather, 0)

    def expert(n_rows, s, c):
        r0 = pl.multiple_of(s * n_rows, n_rows)
        rows = hc_sc[pl.ds(r0, n_rows), :]
        a = _dot(rows, wg_ref[...])
        act = (a * jax.nn.sigmoid(a) * _dot(rows, wu_ref[...])).astype(BF16)
        oacc_sc[pl.ds(r0, n_rows), :] += _dot(act, wd_ref[...])
        return c

    lax.fori_loop(0, nsub // 2, functools.partial(expert, MOE_MOVE), 0)
    lax.fori_loop(nsub // 2 * 2, nsub, functools.partial(expert, MOE_SUB), 0)

    @pl.when(f == nf - 1)
    def _():
        is_e = lax.broadcasted_iota(jnp.int32, (1, LANES), 1) == e
        rank_col = jnp.sum(jnp.where(is_e, rank_ref[...], 0.0), axis=1, keepdims=True)
        comb_col = jnp.sum(jnp.where(is_e, comb_ref[...], 0.0), axis=1, keepdims=True)

        def scatter(s, c):
            r0 = pl.multiple_of(s * MOE_MOVE, MOE_MOVE)
            cols = (lax.broadcasted_iota(jnp.int32, (1, MOE_MOVE), 1) + r0).astype(F32)
            onehot = (rank_col == cols).astype(BF16)
            y = _dot(onehot, oacc_sc[pl.ds(r0, MOE_MOVE), :].astype(BF16))
            o_ref[...] += comb_col * y
            return c

        lax.fori_loop(0, nmove, scatter, 0)


def _moe(counts, h, rankT, rank, comb, wg, wu, wd, x2, tm, fc=1408):
    n, d = x2.shape
    dff = wg.shape[2]
    rows_cap = pl.cdiv(pl.cdiv(tm, MOE_SUB), 2) * MOE_MOVE
    grid_spec = pltpu.PrefetchScalarGridSpec(
        num_scalar_prefetch=1,
        grid=(n // tm, N_EXPERTS, dff // fc),
        in_specs=[pl.BlockSpec((tm, d), lambda t, e, f, c: (t, 0)),
                  pl.BlockSpec((None, 1, tm), lambda t, e, f, c: (e, 0, t)),
                  pl.BlockSpec((tm, LANES), lambda t, e, f, c: (t, 0)),
                  pl.BlockSpec((tm, LANES), lambda t, e, f, c: (t, 0)),
                  pl.BlockSpec((None, d, fc), lambda t, e, f, c: (e, 0, f)),
                  pl.BlockSpec((None, d, fc), lambda t, e, f, c: (e, 0, f)),
                  pl.BlockSpec((None, fc, d), lambda t, e, f, c: (e, f, 0)),
                  pl.BlockSpec((tm, d), lambda t, e, f, c: (t, 0))],
        out_specs=pl.BlockSpec((tm, d), lambda t, e, f, c: (t, 0)),
        scratch_shapes=[pltpu.VMEM((rows_cap, d), BF16), pltpu.VMEM((rows_cap, d), F32)],
    )
    return pl.pallas_call(
        _moe_kernel,
        grid_spec=grid_spec,
        out_shape=jax.ShapeDtypeStruct((n, d), F32),
        compiler_params=_params(3),
        name="moe_experts",
    )(counts, h, rankT.reshape(N_EXPERTS, 1, n), rank, comb, wg, wu, wd, x2)


def _permute_w_in(w):
    o = np.cumsum((0, NSA_WIDTH) + (KV_WIDTH,) * 6 + (3 * NSA_HEADS,))
    q, kc, vc, ks, vs, kw, vw, gts = (w[:, o[k]:o[k + 1]] for k in range(8))
    ret = w[:, o[8]:]
    pad = jnp.zeros((w.shape[0], LANES - 3 * NSA_HEADS), w.dtype)
    return jnp.concatenate([q, ks, kw, vs, vw, kc, vc, gts, pad, ret], axis=1).astype(BF16)


def _nsa_consts(T):
    ncp = T // CMP_STRIDE
    ns = T // SLC_LEN
    cs = np.arange(ncp) * CMP_STRIDE
    ss = np.arange(ns) * SLC_LEN
    ov = np.clip(np.minimum(cs[None, :] + CMP_LEN, ss[:, None] + SLC_LEN) - np.maximum(cs[None, :], ss[:, None]), 0, None)
    ovT = (ov.astype(np.float32) / CMP_LEN)
    ovT[:, ncp - 1] = 0.0
    h = np.arange(NSA_HEADS).reshape(NSA_KV_HEADS, NSA_GROUP) + 1
    slopes = np.exp2(-8.0 * h / NSA_HEADS).astype(np.float32)
    slopes = np.repeat(slopes, Q_BLOCK, axis=1)
    parts, rest = [], np.float64(LOG2E)
    for _ in range(3):
        part = np.float64(np.asarray(rest).astype(BF16))
        parts.append(part)
        rest = rest - part
    qaug = np.zeros((NSA_KV_HEADS, HEAD_DIM, GQ), np.float32)
    for k, part in enumerate(parts):
        qaug[:, k, :] = part * SLC_LEN * slopes
        qaug[:, 3 + k, :] = part * slopes
    kq = np.arange(Q_BLOCK)[:, None] - np.arange(Q_BLOCK)[None, :]
    causb = np.where(kq <= 0, 0.0, NEG).astype(np.float32)
    lowb = np.where(kq > 0, 0.0, NEG).astype(np.float32)
    return jnp.asarray(ovT, BF16), jnp.asarray(qaug, BF16), jnp.asarray(lowb), jnp.asarray(causb)


def _mixer(x2, B, T, norm_g, w_in, q_norm_g, k_norm_g, cmp_pos, w_cmp, ret_norm_g, w_out):
    ns = T // SLC_LEN
    q, kv, kc, vc, gt, pret = _inproj(x2, norm_g[None, :], _permute_w_in(w_in))
    qT, ks, kw, vsT, vwT, gT = _prep(q, kv, gt, q_norm_g[None, :], k_norm_g[1:3], B, T)
    wk, pk = _compress_weights(w_cmp[0], cmp_pos[0])
    wv, pv = _compress_weights(w_cmp[1], cmp_pos[1])
    kcmp, vcT = _compress(kc, vc, wk, wv, pk, pv, k_norm_g[0:1], B, T)
    ovT, qaug, lowb, causb = _nsa_consts(T)
    ocmp, sel, flags = _nsa_cmp(qT, qaug, kcmp, vcT, ovT, B, T)
    lists, counts = _nsa_steps(flags[:, :, :, 0, :].reshape(-1, ns), T // Q_BLOCK)
    kpad = jnp.zeros((WIN, 2 * HEAD_DIM), BF16).at[:, HEAD_DIM:HEAD_DIM + 3].set(-2.0 ** 100)
    kw = jnp.concatenate([jnp.broadcast_to(kpad, kw.shape[:2] + kpad.shape), kw], axis=2)
    vwT = jnp.pad(vwT, ((0, 0), (0, 0), (0, 0), (WIN, 0)))
    nsa = _nsa_main(lists, counts, qT, qaug, ks, vsT, kw, vwT, sel, gT, ocmp, lowb, causb, B, T)
    ret = _retention(pret, ret_norm_g[None, :], B, T)
    return _outproj(x2, nsa.reshape(B * T, NSA_WIDTH), ret, w_out.astype(BF16))


def _moe_layer(x2, norm_g, router, router_b, wg, wu, wd, tm=1024):
    tm = min(tm, x2.shape[0])
    h, rank, comb, rankT, cnt = _router(x2, norm_g[None, :], router, router_b, tm)
    counts = cnt[:, 0, :N_EXPERTS].reshape(-1)
    return _moe(counts, h, rankT, rank, comb, wg.astype(BF16), wu.astype(BF16), wd.astype(BF16), x2, tm)


def kernel(x, norm_mix_g, w_in, q_norm_g, k_norm_g, cmp_pos, w_cmp, ret_norm_g, w_out, norm_ffn_g,
           ffn_w_gate, ffn_w_up, ffn_w_down, moe_router, moe_router_b, moe_w_gate, moe_w_up, moe_w_down):
    B, T, D = x.shape
    depth = norm_mix_g.shape[0]
    x2 = x.reshape(B * T, D)
    for l in range(depth):
        x2 = _mixer(x2, B, T, norm_mix_g[l], w_in[l], q_norm_g[l], k_norm_g[l], cmp_pos[l], w_cmp[l],
                    ret_norm_g[l], w_out[l])
        j = l // 2
        if l % 2 == 0:
            x2 = _ffn(x2, norm_ffn_g[l][None, :], ffn_w_gate[j].astype(BF16), ffn_w_up[j].astype(BF16),
                      ffn_w_down[j].astype(BF16))
        else:
            x2 = _moe_layer(x2, norm_ffn_g[l], moe_router[j], moe_router_b[j], moe_w_gate[j], moe_w_up[j],
                            moe_w_down[j])
    return x2.reshape(B, T, D)
```

```python
import functools

import numpy as np
import jax
import jax.numpy as jnp
from jax import lax
from jax.experimental import pallas as pl
from jax.experimental.pallas import tpu as pltpu

F32 = jnp.float32
BF16 = jnp.bfloat16

HEAD_DIM = 64
NSA_HEADS = 8
NSA_KV_HEADS = 2
NSA_GROUP = NSA_HEADS // NSA_KV_HEADS
RET_HEADS = 8
RET_DK = 32
RET_DV = 64
NSA_WIDTH = NSA_HEADS * HEAD_DIM
RET_WIDTH = RET_HEADS * RET_DV
KV_WIDTH = NSA_KV_HEADS * HEAD_DIM
CMP_LEN = 32
CMP_STRIDE = 16
SLC_LEN = 64
SLC_TOPK = 16
WIN = 512
Q_BLOCK = 128
RET_CHUNK = 128
N_EXPERTS = 8
EPS = 1e-6
NEG = -1e30
BIG = 1e9
LANES = 128
GQ = NSA_GROUP * Q_BLOCK
KEY_STEP = 128
STEP_GROUP = 4
N_FORCED = 3
CMP_CHUNK = 128
CMP_TILES = 8
PREP_TILES = 4
MAIN_TILES = 4
LOOP_TILES = 2
CMP_TAIL = CMP_CHUNK + 8
WIN_KEYS = WIN + Q_BLOCK
V_ROWS = HEAD_DIM + 16
LOG2E = 1.4426950408889634
VMEM_LIMIT = 60 * 1024 * 1024

_C_Q = 0
_C_KV = _C_Q + NSA_WIDTH
_C_KC = _C_KV + 4 * KV_WIDTH
_C_VC = _C_KC + KV_WIDTH
_C_GT = _C_VC + KV_WIDTH
_C_RET = _C_GT + LANES
_RET_COLS = 2 * RET_HEADS * RET_DK + 2 * RET_WIDTH
_C_END = _C_RET + _RET_COLS


def _params(n_axes, vmem=VMEM_LIMIT):
    return pltpu.CompilerParams(dimension_semantics=("arbitrary",) * n_axes, vmem_limit_bytes=vmem)


def _dot(a, b):
    return jnp.dot(a, b, preferred_element_type=F32)


def _dot_nt(a, b):
    return lax.dot_general(a, b, (((1,), (1,)), ((), ())), preferred_element_type=F32)


def _rms(x, g):
    return x * lax.rsqrt(jnp.mean(x * x, axis=-1, keepdims=True) + EPS) * g


def _inproj_kernel(x_ref, g_ref, w_ref, q_ref, kv_ref, kc_ref, vc_ref, gt_ref, ret_ref):
    half = x_ref.shape[0] // 2
    rows = [slice(0, half), slice(half, 2 * half)]
    h = [_rms(x_ref[r, :], g_ref[...]).astype(BF16) for r in rows]
    cols = (_C_Q, _C_KV, _C_KC, _C_VC, _C_GT, _C_RET, _C_END)
    for k, o_ref in enumerate((q_ref, kv_ref, kc_ref, vc_ref, gt_ref, ret_ref)):
        for u, r in enumerate(rows):
            o_ref[r, :] = _dot(h[u], w_ref[:, cols[k]:cols[k + 1]])


def _inproj(x2, g, w, tm=1024):
    n, d = x2.shape
    widths = (NSA_WIDTH, 4 * KV_WIDTH, KV_WIDTH, KV_WIDTH, LANES, _RET_COLS)
    return pl.pallas_call(
        _inproj_kernel,
        grid=(n // tm,),
        in_specs=[pl.BlockSpec((tm, d), lambda i: (i, 0)),
                  pl.BlockSpec((1, d), lambda i: (0, 0)),
                  pl.BlockSpec((d, _C_END), lambda i: (0, 0))],
        out_specs=[pl.BlockSpec((tm, c), lambda i: (i, 0)) for c in widths],
        out_shape=[jax.ShapeDtypeStruct((n, c), F32) for c in widths],
        compiler_params=_params(1),
        name="inproj",
    )(x2, g, w)


def _group_rms(x, g, ones_ref):
    w = x.shape[1]
    ones = ones_ref[0:w, 0:w]
    hi, mid, lo = _split3(x * x)
    ms = (_dot(hi, ones) + _dot(mid, ones) + _dot(lo, ones)) * (1.0 / HEAD_DIM)
    return x * lax.rsqrt(ms + EPS) * g


def _group_ones():
    lane = np.arange(NSA_WIDTH) // HEAD_DIM
    return jnp.asarray(lane[:, None] == lane[None, :], BF16)


def _prep_kernel(q_ref, kv_ref, gt_ref, qg_ref, kg_ref, ones_ref, qT_ref, ks_ref, kw_ref, vsT_ref, vwT_ref, gT_ref):
    tiles = range(PREP_TILES)
    rows = [slice(u * Q_BLOCK, (u + 1) * Q_BLOCK) for u in tiles]
    n_tok = PREP_TILES * Q_BLOCK
    scale = HEAD_DIM ** -0.5 * LOG2E
    qn = _group_rms(q_ref[...], qg_ref[...], ones_ref) * scale
    qt = [qn[rows[u]].T for u in tiles]
    for g in range(NSA_KV_HEADS):
        for r in range(NSA_GROUP):
            h = g * NSA_GROUP + r
            for u in tiles:
                qT_ref[g, u, :, r * Q_BLOCK:(r + 1) * Q_BLOCK] = qt[u][h * HEAD_DIM:(h + 1) * HEAD_DIM, :].astype(BF16)
    kv = kv_ref[...]
    vst = [kv[rows[u], 2 * KV_WIDTH:3 * KV_WIDTH].T for u in tiles]
    vwt = [kv[rows[u], 3 * KV_WIDTH:4 * KV_WIDTH].T for u in tiles]
    gts = [jax.nn.sigmoid(gt_ref[rows[u], :].T[0:32, :]) for u in tiles]
    pos = pl.program_id(1) * n_tok + lax.broadcasted_iota(jnp.int32, (n_tok, HEAD_DIM), 0)
    col = lax.broadcasted_iota(jnp.int32, (n_tok, HEAD_DIM), 1)
    kpos = jnp.where(col < 3, pos // SLC_LEN, jnp.where(col < 6, pos % SLC_LEN, 0)).astype(F32)
    ones_row = (lax.broadcasted_iota(jnp.int32, (V_ROWS - HEAD_DIM, Q_BLOCK), 0) == 0).astype(F32)
    ks = _group_rms(kv[:, 0:KV_WIDTH], kg_ref[0:1, :], ones_ref)
    kw = _group_rms(kv[:, KV_WIDTH:2 * KV_WIDTH], kg_ref[1:2, :], ones_ref)
    for g in range(NSA_KV_HEADS):
        sl = slice(g * HEAD_DIM, (g + 1) * HEAD_DIM)
        ks_ref[g] = jnp.concatenate([ks[:, sl], kpos], axis=1).astype(BF16)
        kw_ref[g] = jnp.concatenate([kw[:, sl], kpos], axis=1).astype(BF16)
        for u in tiles:
            vsT_ref[g, :, rows[u]] = jnp.concatenate([vst[u][sl, :], ones_row], axis=0).astype(BF16)
            vwT_ref[g, :, rows[u]] = jnp.concatenate([vwt[u][sl, :], ones_row], axis=0).astype(BF16)
    for u in tiles:
        gT_ref[u] = gts[u]


def _prep(q, kv, gt, qg, kg, B, T):
    nq = T // Q_BLOCK
    G = NSA_KV_HEADS
    n_tok = PREP_TILES * Q_BLOCK
    steps = nq // PREP_TILES
    row = lambda b, i: (b * steps + i, 0)
    return pl.pallas_call(
        _prep_kernel,
        grid=(B, steps),
        in_specs=[pl.BlockSpec((n_tok, NSA_WIDTH), row),
                  pl.BlockSpec((n_tok, 4 * KV_WIDTH), row),
                  pl.BlockSpec((n_tok, LANES), row),
                  pl.BlockSpec((1, NSA_WIDTH), lambda b, i: (0, 0)),
                  pl.BlockSpec((2, KV_WIDTH), lambda b, i: (0, 0)),
                  pl.BlockSpec((NSA_WIDTH, NSA_WIDTH), lambda b, i: (0, 0))],
        out_specs=[pl.BlockSpec((None, G, PREP_TILES, HEAD_DIM, GQ), lambda b, i: (b, 0, i, 0, 0)),
                   pl.BlockSpec((None, G, n_tok, 2 * HEAD_DIM), lambda b, i: (b, 0, i, 0)),
                   pl.BlockSpec((None, G, n_tok, 2 * HEAD_DIM), lambda b, i: (b, 0, i, 0)),
                   pl.BlockSpec((None, G, V_ROWS, n_tok), lambda b, i: (b, 0, 0, i)),
                   pl.BlockSpec((None, G, V_ROWS, n_tok), lambda b, i: (b, 0, 0, i)),
                   pl.BlockSpec((None, PREP_TILES, 32, Q_BLOCK), lambda b, i: (b, i, 0, 0))],
        out_shape=[jax.ShapeDtypeStruct((B, G, nq, HEAD_DIM, GQ), BF16),
                   jax.ShapeDtypeStruct((B, G, T, 2 * HEAD_DIM), BF16),
                   jax.ShapeDtypeStruct((B, G, T, 2 * HEAD_DIM), BF16),
                   jax.ShapeDtypeStruct((B, G, V_ROWS, T), BF16),
                   jax.ShapeDtypeStruct((B, G, V_ROWS, T), BF16),
                   jax.ShapeDtypeStruct((B, nq, 32, Q_BLOCK), F32)],
        compiler_params=_params(2),
        name="nsa_prep",
    )(q, kv, gt, jnp.tile(qg, (1, NSA_HEADS)), jnp.tile(kg, (1, NSA_KV_HEADS)), _group_ones())


def _compress_kernel(kc_ref, vc_ref, wk_ref, wv_ref, pk_ref, pv_ref, kg_ref, kcmp_ref, vcT_ref):
    ncp = kc_ref.shape[0] // CMP_STRIDE

    def comp(a_ref, w_ref, p_ref):
        lo = jnp.zeros((ncp, KV_WIDTH), F32)
        hi = jnp.zeros((ncp, KV_WIDTH), F32)
        for l in range(CMP_STRIDE):
            a = a_ref[pl.ds(l, ncp, stride=CMP_STRIDE), :]
            lo += _dot((a + p_ref[0, l:l + 1, :]).astype(BF16), w_ref[0, l])
            hi += _dot((a + p_ref[1, l:l + 1, :]).astype(BF16), w_ref[1, l])
        return lo + pltpu.roll(hi, ncp - 1, 0)

    k = comp(kc_ref, wk_ref, pk_ref)
    v = comp(vc_ref, wv_ref, pv_ref).T
    cend = lax.broadcasted_iota(jnp.int32, (ncp, HEAD_DIM), 0) * CMP_STRIDE + (CMP_LEN - 1)
    col = lax.broadcasted_iota(jnp.int32, (ncp, HEAD_DIM), 1)
    kpos = jnp.where(col < 3, cend // SLC_LEN, jnp.where(col < 6, cend % SLC_LEN, 0)).astype(F32)
    for g in range(NSA_KV_HEADS):
        sl = slice(g * HEAD_DIM, (g + 1) * HEAD_DIM)
        kcmp_ref[g] = jnp.concatenate([_rms(k[:, sl], kg_ref[...]), kpos], axis=1).astype(BF16)
        vcT_ref[g] = v[sl, :].astype(BF16)


def _compress(kc, vc, wk, wv, pk, pv, kg, B, T):
    ncp = T // CMP_STRIDE
    G = NSA_KV_HEADS
    const4 = lambda b: (0, 0, 0, 0)
    const3 = lambda b: (0, 0, 0)
    const2 = lambda b: (0, 0)
    return pl.pallas_call(
        _compress_kernel,
        grid=(B,),
        in_specs=[pl.BlockSpec((T, KV_WIDTH), lambda b: (b, 0)),
                  pl.BlockSpec((T, KV_WIDTH), lambda b: (b, 0)),
                  pl.BlockSpec((2, CMP_STRIDE, KV_WIDTH, KV_WIDTH), const4),
                  pl.BlockSpec((2, CMP_STRIDE, KV_WIDTH, KV_WIDTH), const4),
                  pl.BlockSpec((2, CMP_STRIDE, KV_WIDTH), const3),
                  pl.BlockSpec((2, CMP_STRIDE, KV_WIDTH), const3),
                  pl.BlockSpec((1, HEAD_DIM), const2)],
        out_specs=[pl.BlockSpec((None, G, ncp, 2 * HEAD_DIM), lambda b: (b, 0, 0, 0)),
                   pl.BlockSpec((None, G, HEAD_DIM, ncp), lambda b: (b, 0, 0, 0))],
        out_shape=[jax.ShapeDtypeStruct((B, G, ncp, 2 * HEAD_DIM), BF16),
                   jax.ShapeDtypeStruct((B, G, HEAD_DIM, ncp), BF16)],
        compiler_params=_params(1),
        name="nsa_compress",
    )(kc, vc, wk, wv, pk, pv, kg)


def _compress_weights(w, pos):
    G = NSA_KV_HEADS
    w4 = w.reshape(2, CMP_STRIDE, HEAD_DIM, HEAD_DIM)
    eye = jnp.eye(G, dtype=w.dtype)
    wbd = jnp.einsum('hlde,gk->hlgdke', w4, eye).reshape(2, CMP_STRIDE, KV_WIDTH, KV_WIDTH)
    p = pos.reshape(2, CMP_STRIDE, 1, HEAD_DIM)
    p = jnp.broadcast_to(p, (2, CMP_STRIDE, G, HEAD_DIM)).reshape(2, CMP_STRIDE, KV_WIDTH)
    return wbd.astype(BF16), p


def _split3(x):
    hi = x.astype(BF16)
    r = x - hi.astype(F32)
    mid = r.astype(BF16)
    lo = (r - mid.astype(F32)).astype(BF16)
    return hi, mid, lo


def _nsa_cmp_kernel(qT_ref, qaug_ref, kc_ref, vcT_ref, ovT_ref, ocmp_ref, sel_ref, flag_ref, *, n_sel):
    ncp = kc_ref.shape[0]
    ns = ovT_ref.shape[0]
    tiles = range(CMP_TILES)
    i0 = pl.program_id(2) * CMP_TILES
    lane = lax.broadcasted_iota(jnp.int32, (1, GQ), 1)
    q = [jnp.concatenate([qT_ref[u], qaug_ref[...]], axis=0) for u in tiles]
    t_row = [(i0 + u) * Q_BLOCK + (lane & (Q_BLOCK - 1)) for u in tiles]
    has_cmp = [(t_row[u] >= CMP_LEN - 1).astype(F32) for u in tiles]
    tq = [(i0 + u) * Q_BLOCK + lax.broadcasted_iota(jnp.int32, (1, Q_BLOCK), 1) for u in tiles]
    cur = [tq[u] // SLC_LEN for u in tiles]

    def prefix(rows):
        nsk = rows * CMP_STRIDE // SLC_LEN
        tail0 = max(rows - CMP_TAIL, 0)
        kc = kc_ref[0:rows, :]
        s = [_dot(kc, q[u]) for u in tiles]
        cend = (lax.broadcasted_iota(jnp.int32, (rows - tail0, 1), 0) + tail0) * CMP_STRIDE + (CMP_LEN - 1)
        tail = [jnp.where(t_row[u] >= cend, s[u][tail0:], NEG) for u in tiles]
        s = [jnp.concatenate([s[u][0:tail0], tail[u]], axis=0) if tail0 else tail[u] for u in tiles]
        m = [jnp.max(s[u], axis=0, keepdims=True) for u in tiles]
        e = [jnp.exp2(s[u] - m[u]) for u in tiles]
        p = [e[u] * (has_cmp[u] / jnp.sum(e[u], axis=0, keepdims=True)) for u in tiles]
        vc = vcT_ref[:, 0:rows]
        for u in tiles:
            ocmp_ref[u] = _dot(vc, p[u].astype(BF16))

        ps = [p[u][:, 0:Q_BLOCK] for u in tiles]
        for r in range(1, NSA_GROUP):
            ps = [ps[u] + p[u][:, r * Q_BLOCK:(r + 1) * Q_BLOCK] for u in tiles]
        ov = ovT_ref[0:nsk, 0:rows]
        split = [_split3(ps[u]) for u in tiles]
        imp = [_dot(ov, split[u][0]) + _dot(ov, split[u][1]) + _dot(ov, split[u][2]) for u in tiles]

        blk = lax.broadcasted_iota(jnp.int32, (nsk, 1), 0)
        forced = [(blk == 0) | (blk == cur[u]) | (blk == cur[u] - 1) for u in tiles]
        valid = [blk * SLC_LEN <= tq[u] for u in tiles]
        imp = [jnp.where(forced[u], -3e38, jnp.where(valid[u], imp[u], -BIG)) for u in tiles]
        blk_f = blk.astype(F32)
        sel = [forced[u].astype(F32) for u in tiles]
        for _ in range(n_sel - N_FORCED):
            mx = [jnp.max(imp[u], axis=0, keepdims=True) for u in tiles]
            idx = [jnp.min(jnp.where(imp[u] == mx[u], blk_f, float(ns)), axis=0, keepdims=True) for u in tiles]
            pick = [blk_f == idx[u] for u in tiles]
            sel = [jnp.where(pick[u], 1.0, sel[u]) for u in tiles]
            imp = [jnp.where(pick[u], -3e38, imp[u]) for u in tiles]
        ones = jnp.ones((8, Q_BLOCK), BF16)
        for u in tiles:
            sel_ref[u, 0:nsk, :] = sel[u]
            cnt = _dot_nt(ones, sel[u].astype(BF16))
            flag_ref[u, :, 0:nsk] = (cnt > 0).astype(jnp.int32)
            if nsk < ns:
                sel_ref[u, nsk:, :] = jnp.zeros((ns - nsk, Q_BLOCK), F32)
                flag_ref[u, :, nsk:] = jnp.zeros((8, ns - nsk), jnp.int32)

    n_variants = ncp // CMP_CHUNK
    last = i0 + CMP_TILES - 1
    variant = (last * (Q_BLOCK // CMP_STRIDE) + (Q_BLOCK // CMP_STRIDE - 2)) // CMP_CHUNK
    for k in range(n_variants):
        pl.when(variant == k)(functools.partial(prefix, (k + 1) * CMP_CHUNK))


def _nsa_cmp(qT, qaug, kcmp, vcT, ovT, B, T):
    G = NSA_KV_HEADS
    nq = T // Q_BLOCK
    ncp = T // CMP_STRIDE
    ns = T // SLC_LEN
    n_sel = min(SLC_TOPK, ns)
    assert ncp % CMP_CHUNK == 0 and n_sel > N_FORCED and nq % CMP_TILES == 0
    tile = lambda b, g, i: (b, g, i, 0, 0)
    return pl.pallas_call(
        functools.partial(_nsa_cmp_kernel, n_sel=n_sel),
        grid=(B, G, nq // CMP_TILES),
        in_specs=[pl.BlockSpec((None, None, CMP_TILES, HEAD_DIM, GQ), tile),
                  pl.BlockSpec((None, HEAD_DIM, GQ), lambda b, g, i: (g, 0, 0)),
                  pl.BlockSpec((None, None, ncp, 2 * HEAD_DIM), lambda b, g, i: (b, g, 0, 0)),
                  pl.BlockSpec((None, None, HEAD_DIM, ncp), lambda b, g, i: (b, g, 0, 0)),
                  pl.BlockSpec((ns, ncp), lambda b, g, i: (0, 0))],
        out_specs=[pl.BlockSpec((None, None, CMP_TILES, HEAD_DIM, GQ), tile),
                   pl.BlockSpec((None, None, CMP_TILES, ns, Q_BLOCK), tile),
                   pl.BlockSpec((None, None, CMP_TILES, 8, ns), tile)],
        out_shape=[jax.ShapeDtypeStruct((B, G, nq, HEAD_DIM, GQ), F32),
                   jax.ShapeDtypeStruct((B, G, nq, ns, Q_BLOCK), F32),
                   jax.ShapeDtypeStruct((B, G, nq, 8, ns), jnp.int32)],
        compiler_params=_params(3),
        name="nsa_cmp",
    )(qT, qaug, kcmp, vcT, ovT)


def _nsa_main_kernel(list_ref, cnt_ref, qT_ref, qaug_ref, ks_ref, vsT_ref, kw_ref, vwT_ref, sel_ref, gT_ref, ocmp_ref,
                     lowb_ref, causb_ref, out_ref, m_sc, acc_sc, win_sc):
    b, g = pl.program_id(0), pl.program_id(1)
    tiles = range(MAIN_TILES)
    i = [pl.program_id(2) * MAIN_TILES + u for u in tiles]
    tile_id = [(b * pl.num_programs(1) + g) * (pl.num_programs(2) * MAIN_TILES) + i[u] for u in tiles]
    n_steps = sel_ref.shape[1] // 2
    q = [jnp.concatenate([qT_ref[u], qaug_ref[...]], axis=0) for u in tiles]
    k0 = [pl.multiple_of(i[u] * Q_BLOCK, Q_BLOCK) for u in tiles]

    def sel_bias(u, j, valid):
        def row(r):
            picked = (sel_ref[u, pl.ds(r, 1), :] > 0.5) & valid
            return jnp.concatenate([jnp.where(picked, 0.0, NEG)] * NSA_GROUP, axis=1)
        return row(2 * j), row(2 * j + 1)

    def add_sel_bias(s, ba, bb):
        return jnp.concatenate([s[0:SLC_LEN] + ba, s[SLC_LEN:] + bb], axis=0)

    lowb = jnp.concatenate([lowb_ref[...]] * NSA_GROUP, axis=1)
    causb = jnp.concatenate([causb_ref[...]] * NSA_GROUP, axis=1)

    bias_d = [sel_bias(u, i[u], True) for u in tiles]
    sd = [_dot(ks_ref[pl.ds(k0[u], KEY_STEP), :], q[u]) for u in tiles]
    sw = [_dot(kw_ref[pl.ds(k0[u], WIN_KEYS), :], q[u]) for u in tiles]
    sd = [add_sel_bias(sd[u], *bias_d[u]) + causb for u in tiles]
    sw = [jnp.concatenate([sw[u][0:Q_BLOCK] + lowb, sw[u][Q_BLOCK:WIN], sw[u][WIN:] + causb], axis=0) for u in tiles]
    md = [jnp.max(sd[u], axis=0, keepdims=True) for u in tiles]
    mw = [jnp.max(sw[u], axis=0, keepdims=True) for u in tiles]
    accd = [_dot(vsT_ref[:, pl.ds(k0[u], KEY_STEP)], jnp.exp2((sd[u] - md[u]).astype(BF16))) for u in tiles]
    ow = [_dot(vwT_ref[:, pl.ds(k0[u], WIN_KEYS)], jnp.exp2((sw[u] - mw[u]).astype(BF16))) for u in tiles]
    for u in tiles:
        m_sc[u] = md[u]
        acc_sc[u] = accd[u]
        win_sc[u] = ow[u][0:HEAD_DIM] / ow[u][HEAD_DIM:HEAD_DIM + 1]

    def scores(u, t):
        ks, vs, biases = [], [], []
        for x in range(STEP_GROUP):
            j = list_ref[tile_id[u] * n_steps + t * STEP_GROUP + x]
            valid = j >= 0
            j = jnp.maximum(j, 0)
            kj = pl.multiple_of(j * KEY_STEP, KEY_STEP)
            ks.append(ks_ref[pl.ds(kj, KEY_STEP), :])
            vs.append(vsT_ref[:, pl.ds(kj, KEY_STEP)])
            biases.append(sel_bias(u, j, valid))
        s = _dot(jnp.concatenate(ks, axis=0), q[u])
        s = jnp.concatenate([add_sel_bias(s[x * KEY_STEP:(x + 1) * KEY_STEP], *biases[x])
                             for x in range(STEP_GROUP)], axis=0)
        return s, jnp.max(s, axis=0, keepdims=True), jnp.concatenate(vs, axis=1)

    def accumulate(u, s, smax, vcat):
        m_old = m_sc[u]
        m_new = jnp.maximum(m_old, smax)
        alpha = jnp.exp2(m_old - m_new)
        acc_sc[u] = alpha * acc_sc[u] + _dot(vcat, jnp.exp2((s - m_new).astype(BF16)))
        m_sc[u] = m_new

    def run(work, t, carry):
        staged = [(u, scores(u, t * mult + off)) for (u, mult, off) in work]
        for u, args in staged:
            accumulate(u, *args)
        return carry

    for u0 in range(0, MAIN_TILES, LOOP_TILES):
        us = range(u0, u0 + LOOP_TILES)
        n_groups = functools.reduce(
            jnp.maximum, [(cnt_ref[tile_id[u]] + (STEP_GROUP - 1)) // STEP_GROUP for u in us])
        lax.fori_loop(0, n_groups // 2, functools.partial(run, [(u, 2, off) for off in (0, 1) for u in us]), 0)
        lax.fori_loop(n_groups // 2 * 2, n_groups, functools.partial(run, [(u, 1, 0) for u in us]), 0)

    def gate(u, k):
        rows = [gT_ref[u, pl.ds(g * (NSA_GROUP * 3) + r * 3 + k, 1), :] for r in range(NSA_GROUP)]
        return jnp.concatenate(rows, axis=1)

    o_slc = [acc_sc[u, 0:HEAD_DIM, :] / acc_sc[u, HEAD_DIM:HEAD_DIM + 1, :] for u in tiles]
    o = [gate(u, 0) * ocmp_ref[u] + gate(u, 1) * o_slc[u] + gate(u, 2) * win_sc[u] for u in tiles]
    o = [jnp.concatenate([o[u], jnp.zeros_like(o[u])], axis=0) for u in tiles]
    for r in range(NSA_GROUP):
        ot = [o[u][:, r * Q_BLOCK:(r + 1) * Q_BLOCK].T[:, 0:HEAD_DIM] for u in tiles]
        for u in tiles:
            out_ref[u * Q_BLOCK:(u + 1) * Q_BLOCK, r * HEAD_DIM:(r + 1) * HEAD_DIM] = ot[u]


def _nsa_steps_kernel(flagT_ref, pairT_ref, list_ref, cnt_ref, *, nq):
    n_steps, nt = list_ref.shape
    need = _dot(pairT_ref[...], flagT_ref[...].astype(BF16)) > 0
    step = lax.broadcasted_iota(jnp.int32, (n_steps, 1), 0)
    own = lax.broadcasted_iota(jnp.int32, (1, nt), 1) % nq
    need = need & (step < own)
    need_f = need.astype(F32)
    earlier = (lax.broadcasted_iota(jnp.int32, (n_steps, n_steps), 1) < step).astype(BF16)
    slot = _dot(earlier, need_f.astype(BF16))
    total = jnp.sum(need_f, axis=0, keepdims=True)
    cnt_ref[...] = jnp.broadcast_to(total, cnt_ref.shape).astype(jnp.int32)
    step_f = step.astype(F32)
    for p in range(n_steps):
        val = jnp.sum(jnp.where(need & (slot == p), step_f, 0.0), axis=0, keepdims=True)
        list_ref[p:p + 1, :] = jnp.where(total > p, val, -1.0).astype(jnp.int32)


def _nsa_steps(flags, nq):
    nt, ns = flags.shape
    n_steps = ns // 2
    pairT = jnp.asarray(np.arange(n_steps)[:, None] == np.arange(ns)[None, :] // 2, BF16)
    lists, counts = pl.pallas_call(
        functools.partial(_nsa_steps_kernel, nq=nq),
        out_shape=[jax.ShapeDtypeStruct((n_steps, nt), jnp.int32), jax.ShapeDtypeStruct((8, nt), jnp.int32)],
        name="nsa_steps",
    )(flags.T.astype(F32), pairT)
    return lists.T.reshape(-1), counts[0]


def _nsa_main(lists, counts, qT, qaug, ks, vsT, kw, vwT, sel, gT, ocmp, lowb, causb, B, T):
    G = NSA_KV_HEADS
    nq = T // Q_BLOCK
    ns = T // SLC_LEN
    whole = lambda b, g, i, *_: (b, g, 0, 0)
    tile = lambda b, g, i, *_: (b, g, i, 0, 0)
    const = lambda b, g, i, *_: (0, 0)
    grid_spec = pltpu.PrefetchScalarGridSpec(
        num_scalar_prefetch=2,
        grid=(B, G, nq // MAIN_TILES),
        in_specs=[pl.BlockSpec((None, None, MAIN_TILES, HEAD_DIM, GQ), tile),
                  pl.BlockSpec((None, HEAD_DIM, GQ), lambda b, g, i, *_: (g, 0, 0)),
                  pl.BlockSpec((None, None, T, 2 * HEAD_DIM), whole),
                  pl.BlockSpec((None, None, V_ROWS, T), whole),
                  pl.BlockSpec((None, None, T + WIN, 2 * HEAD_DIM), whole),
                  pl.BlockSpec((None, None, V_ROWS, T + WIN), whole),
                  pl.BlockSpec((None, None, MAIN_TILES, ns, Q_BLOCK), tile),
                  pl.BlockSpec((None, MAIN_TILES, 32, Q_BLOCK), lambda b, g, i, *_: (b, i, 0, 0)),
                  pl.BlockSpec((None, None, MAIN_TILES, HEAD_DIM, GQ), tile),
                  pl.BlockSpec((Q_BLOCK, Q_BLOCK), const),
                  pl.BlockSpec((Q_BLOCK, Q_BLOCK), const)],
        out_specs=pl.BlockSpec((None, MAIN_TILES * Q_BLOCK, NSA_GROUP * HEAD_DIM), lambda b, g, i, *_: (b, i, g)),
        scratch_shapes=[pltpu.VMEM((MAIN_TILES, 1, GQ), F32), pltpu.VMEM((MAIN_TILES, V_ROWS, GQ), F32),
                        pltpu.VMEM((MAIN_TILES, HEAD_DIM, GQ), F32)],
    )
    return pl.pallas_call(
        _nsa_main_kernel,
        grid_spec=grid_spec,
        out_shape=jax.ShapeDtypeStruct((B, T, NSA_WIDTH), F32),
        compiler_params=_params(3),
        name="nsa_main",
    )(lists, counts, qT, qaug, ks, vsT, kw, vwT, sel, gT, ocmp, lowb, causb)


def _ret_kernel(p_ref, decay_ref, xi_ref, zeta_ref, gch_ref, ng_ref, ones_ref, out_ref, state_ref):
    @pl.when(pl.program_id(0) == 0)
    def _():
        state_ref[...] = jnp.zeros(state_ref.shape, F32)

    rows = range(p_ref.shape[0])
    kw = RET_HEADS * RET_DK
    p = [p_ref[b] for b in rows]
    rq = [p[b][:, 0:kw] * (RET_DK ** -0.5) for b in rows]
    rk = [p[b][:, kw:2 * kw] for b in rows]
    rkT = [rk[b].T for b in rows]
    rv = [p[b][:, 2 * kw:2 * kw + RET_WIDTH] for b in rows]
    xi = xi_ref[...]
    outs = [[] for _ in rows]
    for h in range(RET_HEADS):
        dk = slice(h * RET_DK, (h + 1) * RET_DK)
        dv = slice(h * RET_DV, (h + 1) * RET_DV)
        st = [state_ref[b, h] for b in rows]
        inner = [_dot_nt(rq[b][:, dk], rk[b][:, dk]) * decay_ref[h] for b in rows]
        o = [_dot(inner[b], rv[b][:, dv]) + _dot(rq[b][:, dk], st[b]) * xi[:, h:h + 1] for b in rows]
        for b in rows:
            state_ref[b, h] = (st[b] * gch_ref[h:h + 1, 0:1]
                               + _dot(rkT[b][dk, :] * zeta_ref[h:h + 1, :], rv[b][:, dv]))
            outs[b].append(o[b])
    normed = [_group_rms(jnp.concatenate(outs[b], axis=1), ng_ref[...], ones_ref) for b in rows]
    for b in rows:
        rg = p[b][:, 2 * kw + RET_WIDTH:2 * kw + 2 * RET_WIDTH]
        out_ref[b] = normed[b] * (rg * jax.nn.sigmoid(rg))


def _ret_consts():
    H, C = RET_HEADS, RET_CHUNK
    log_g = jnp.log1p(-jnp.exp2(-5.0 - jnp.arange(H, dtype=F32)))
    idx = jnp.arange(C, dtype=F32)
    diff = idx[:, None] - idx[None, :]
    decay = jnp.where(diff >= 0, jnp.exp(jnp.maximum(diff, 0.0) * log_g[:, None, None]), 0.0)
    zeta = jnp.exp((C - 1 - idx) * log_g[:, None])
    xi = jnp.exp((idx + 1) * log_g[:, None]).T
    g_chunk = jnp.broadcast_to(jnp.exp(C * log_g)[:, None], (H, LANES))
    return decay, xi, zeta, g_chunk


def _retention(pret, ng, B, T):
    nch = T // RET_CHUNK
    decay, xi, zeta, gch = _ret_consts()
    c2 = lambda c: (0, 0)
    out = pl.pallas_call(
        _ret_kernel,
        grid=(nch,),
        in_specs=[pl.BlockSpec((B, RET_CHUNK, _RET_COLS), lambda c: (0, c, 0)),
                  pl.BlockSpec((RET_HEADS, RET_CHUNK, RET_CHUNK), lambda c: (0, 0, 0)),
                  pl.BlockSpec((RET_CHUNK, RET_HEADS), c2),
                  pl.BlockSpec((RET_HEADS, RET_CHUNK), c2),
                  pl.BlockSpec((RET_HEADS, LANES), c2),
                  pl.BlockSpec((1, RET_WIDTH), c2),
                  pl.BlockSpec((RET_WIDTH, RET_WIDTH), c2)],
        out_specs=pl.BlockSpec((B, RET_CHUNK, RET_WIDTH), lambda c: (0, c, 0)),
        out_shape=jax.ShapeDtypeStruct((B, T, RET_WIDTH), F32),
        scratch_shapes=[pltpu.VMEM((B, RET_HEADS, RET_DK, RET_DV), F32)],
        compiler_params=_params(1),
        name="retention",
    )(pret.reshape(B, T, _RET_COLS), decay, xi, zeta, gch, ng, _group_ones())
    return out.reshape(B * T, RET_WIDTH)


def _outproj_kernel(x_ref, nsa_ref, ret_ref, w_ref, o_ref):
    o_ref[...] = (x_ref[...] + _dot(nsa_ref[...].astype(BF16), w_ref[0:NSA_WIDTH, :])
                  + _dot(ret_ref[...].astype(BF16), w_ref[NSA_WIDTH:, :]))


def _outproj(x2, nsa, ret, w, tm=512):
    n, d = x2.shape
    return pl.pallas_call(
        _outproj_kernel,
        grid=(n // tm,),
        in_specs=[pl.BlockSpec((tm, d), lambda i: (i, 0)),
                  pl.BlockSpec((tm, NSA_WIDTH), lambda i: (i, 0)),
                  pl.BlockSpec((tm, RET_WIDTH), lambda i: (i, 0)),
                  pl.BlockSpec((NSA_WIDTH + RET_WIDTH, d), lambda i: (0, 0))],
        out_specs=pl.BlockSpec((tm, d), lambda i: (i, 0)),
        out_shape=jax.ShapeDtypeStruct((n, d), F32),
        compiler_params=_params(1),
        name="outproj",
    )(x2, nsa, ret, w)


def _ffn_kernel(x_ref, g_ref, wg_ref, wu_ref, wd_ref, o_ref, h_sc):
    f = pl.program_id(1)

    @pl.when(f == 0)
    def _():
        x = x_ref[...]
        h_sc[...] = _rms(x, g_ref[...]).astype(BF16)
        o_ref[...] = x

    h = h_sc[...]
    a = _dot(h, wg_ref[...])
    act = (a * jax.nn.sigmoid(a) * _dot(h, wu_ref[...])).astype(BF16)
    o_ref[...] += _dot(act, wd_ref[...])


def _ffn(x2, g, wg, wu, wd, tm=1024, fc=1408):
    n, d = x2.shape
    dff = wg.shape[1]
    return pl.pallas_call(
        _ffn_kernel,
        grid=(n // tm, dff // fc),
        in_specs=[pl.BlockSpec((tm, d), lambda i, f: (i, 0)),
                  pl.BlockSpec((1, d), lambda i, f: (0, 0)),
                  pl.BlockSpec((d, fc), lambda i, f: (0, f)),
                  pl.BlockSpec((d, fc), lambda i, f: (0, f)),
                  pl.BlockSpec((fc, d), lambda i, f: (f, 0))],
        out_specs=pl.BlockSpec((tm, d), lambda i, f: (i, 0)),
        out_shape=jax.ShapeDtypeStruct((n, d), F32),
        scratch_shapes=[pltpu.VMEM((tm, d), BF16)],
        compiler_params=_params(2),
        name="ffn_dense",
    )(x2, g, wg, wu, wd)


def _router_kernel(x_ref, g_ref, r_ref, rb_ref, tri_ref, h_ref, rank_ref, comb_ref, rankT_ref, cnt_ref):
    h = _rms(x_ref[...], g_ref[...])
    h_ref[...] = h.astype(BF16)
    hh, hm, hl = _split3(h)
    rh, rm, rl = _split3(r_ref[...])
    logits = (_dot(hh, rh) + (_dot(hh, rm) + _dot(hm, rh)) + (_dot(hh, rl) + _dot(hm, rm) + _dot(hl, rh))
              + rb_ref[...])
    lane = lax.broadcasted_iota(jnp.int32, logits.shape, 1).astype(F32)
    logits = jnp.where(lane < N_EXPERTS, logits, NEG)
    m1 = jnp.max(logits, axis=1, keepdims=True)
    i1 = jnp.min(jnp.where(logits == m1, lane, float(LANES)), axis=1, keepdims=True)
    l2 = jnp.where(lane == i1, NEG, logits)
    m2 = jnp.max(l2, axis=1, keepdims=True)
    i2 = jnp.min(jnp.where(l2 == m2, lane, float(LANES)), axis=1, keepdims=True)
    e2 = jnp.exp(m2 - m1)
    w1 = 1.0 / (1.0 + e2)
    w2 = e2 / (1.0 + e2)
    use1, use2 = lane == i1, lane == i2
    comb_ref[...] = jnp.where(use1, w1, 0.0) + jnp.where(use2, w2, 0.0)
    use = (use1 | use2).astype(F32)
    rank = jnp.where(use > 0, _dot(tri_ref[...], use.astype(BF16)), -1.0)
    rank_ref[...] = rank
    rankT_ref[...] = rank.T[0:N_EXPERTS, :]
    cnt_ref[...] = jnp.broadcast_to(jnp.sum(use, axis=0, keepdims=True), cnt_ref.shape).astype(jnp.int32)


def _router(x2, g, router, rb, tm):
    n, d = x2.shape
    nt = n // tm
    tri = (jnp.arange(tm)[:, None] > jnp.arange(tm)[None, :]).astype(BF16)
    rpad = jnp.zeros((d, LANES), F32).at[:, :N_EXPERTS].set(router)
    rbpad = jnp.zeros((1, LANES), F32).at[0, :N_EXPERTS].set(rb)
    c2 = lambda i: (0, 0)
    return pl.pallas_call(
        _router_kernel,
        grid=(nt,),
        in_specs=[pl.BlockSpec((tm, d), lambda i: (i, 0)),
                  pl.BlockSpec((1, d), c2),
                  pl.BlockSpec((d, LANES), c2),
                  pl.BlockSpec((1, LANES), c2),
                  pl.BlockSpec((tm, tm), c2)],
        out_specs=[pl.BlockSpec((tm, d), lambda i: (i, 0)),
                   pl.BlockSpec((tm, LANES), lambda i: (i, 0)),
                   pl.BlockSpec((tm, LANES), lambda i: (i, 0)),
                   pl.BlockSpec((N_EXPERTS, tm), lambda i: (0, i)),
                   pl.BlockSpec((None, 8, LANES), lambda i: (i, 0, 0))],
        out_shape=[jax.ShapeDtypeStruct((n, d), BF16),
                   jax.ShapeDtypeStruct((n, LANES), F32),
                   jax.ShapeDtypeStruct((n, LANES), F32),
                   jax.ShapeDtypeStruct((N_EXPERTS, n), F32),
                   jax.ShapeDtypeStruct((nt, 8, LANES), jnp.int32)],
        compiler_params=_params(1),
        name="moe_router",
    )(x2, g, rpad, rbpad, tri)


MOE_SUB = 144
MOE_MOVE = 2 * MOE_SUB


def _moe_kernel(cnt_ref, h_ref, rankT_ref, rank_ref, comb_ref, wg_ref, wu_ref, wd_ref, x_ref, o_ref, hc_sc, oacc_sc):
    t, e, f = pl.program_id(0), pl.program_id(1), pl.program_id(2)
    nf = pl.num_programs(2)
    tm = h_ref.shape[0]
    nsub = (cnt_ref[t * N_EXPERTS + e] + (MOE_SUB - 1)) // MOE_SUB
    nmove = (nsub + 1) // 2

    @pl.when((e == 0) & (f == 0))
    def _():
        o_ref[...] = x_ref[...]

    @pl.when(f == 0)
    def _():
        rank_row = rankT_ref[...]

        def gather(s, c):
            r0 = pl.multiple_of(s * MOE_MOVE, MOE_MOVE)
            rows = (lax.broadcasted_iota(jnp.int32, (MOE_MOVE, 1), 0) + r0).astype(F32)
            onehot = (rows == rank_row).astype(BF16)
            hc_sc[pl.ds(r0, MOE_MOVE), :] = _dot(onehot, h_ref[...]).astype(BF16)
            oacc_sc[pl.ds(r0, MOE_MOVE), :] = jnp.zeros((MOE_MOVE, oacc_sc.shape[1]), F32)
            return c

        lax.fori_loop(0, nmove, gather, 0)

    def expert(n_rows, s, c):
        r0 = pl.multiple_of(s * n_rows, n_rows)
        rows = hc_sc[pl.ds(r0, n_rows), :]
        a = _dot(rows, wg_ref[...])
        act = (a * jax.nn.sigmoid(a) * _dot(rows, wu_ref[...])).astype(BF16)
        oacc_sc[pl.ds(r0, n_rows), :] += _dot(act, wd_ref[...])
        return c

    lax.fori_loop(0, nsub // 2, functools.partial(expert, MOE_MOVE), 0)
    lax.fori_loop(nsub // 2 * 2, nsub, functools.partial(expert, MOE_SUB), 0)

    @pl.when(f == nf - 1)
    def _():
        is_e = lax.broadcasted_iota(jnp.int32, (1, LANES), 1) == e
        rank_col = jnp.sum(jnp.where(is_e, rank_ref[...], 0.0), axis=1, keepdims=True)
        comb_col = jnp.sum(jnp.where(is_e, comb_ref[...], 0.0), axis=1, keepdims=True)

        def scatter(s, c):
            r0 = pl.multiple_of(s * MOE_MOVE, MOE_MOVE)
            cols = (lax.broadcasted_iota(jnp.int32, (1, MOE_MOVE), 1) + r0).astype(F32)
            onehot = (rank_col == cols).astype(BF16)
            y = _dot(onehot, oacc_sc[pl.ds(r0, MOE_MOVE), :].astype(BF16))
            o_ref[...] += comb_col * y
            return c

        lax.fori_loop(0, nmove, scatter, 0)


def _moe(counts, h, rankT, rank, comb, wg, wu, wd, x2, tm, fc=1408):
    n, d = x2.shape
    dff = wg.shape[2]
    rows_cap = pl.cdiv(pl.cdiv(tm, MOE_SUB), 2) * MOE_MOVE
    grid_spec = pltpu.PrefetchScalarGridSpec(
        num_scalar_prefetch=1,
        grid=(n // tm, N_EXPERTS, dff // fc),
        in_specs=[pl.BlockSpec((tm, d), lambda t, e, f, c: (t, 0)),
                  pl.BlockSpec((None, 1, tm), lambda t, e, f, c: (e, 0, t)),
                  pl.BlockSpec((tm, LANES), lambda t, e, f, c: (t, 0)),
                  pl.BlockSpec((tm, LANES), lambda t, e, f, c: (t, 0)),
                  pl.BlockSpec((None, d, fc), lambda t, e, f, c: (e, 0, f)),
                  pl.BlockSpec((None, d, fc), lambda t, e, f, c: (e, 0, f)),
                  pl.BlockSpec((None, fc, d), lambda t, e, f, c: (e, f, 0)),
                  pl.BlockSpec((tm, d), lambda t, e, f, c: (t, 0))],
        out_specs=pl.BlockSpec((tm, d), lambda t, e, f, c: (t, 0)),
        scratch_shapes=[pltpu.VMEM((rows_cap, d), BF16), pltpu.VMEM((rows_cap, d), F32)],
    )
    return pl.pallas_call(
        _moe_kernel,
        grid_spec=grid_spec,
        out_shape=jax.ShapeDtypeStruct((n, d), F32),
        compiler_params=_params(3),
        name="moe_experts",
    )(counts, h, rankT.reshape(N_EXPERTS, 1, n), rank, comb, wg, wu, wd, x2)


def _permute_w_in(w):
    o = np.cumsum((0, NSA_WIDTH) + (KV_WIDTH,) * 6 + (3 * NSA_HEADS,))
    q, kc, vc, ks, vs, kw, vw, gts = (w[:, o[k]:o[k + 1]] for k in range(8))
    ret = w[:, o[8]:]
    pad = jnp.zeros((w.shape[0], LANES - 3 * NSA_HEADS), w.dtype)
    return jnp.concatenate([q, ks, kw, vs, vw, kc, vc, gts, pad, ret], axis=1).astype(BF16)


def _nsa_consts(T):
    ncp = T // CMP_STRIDE
    ns = T // SLC_LEN
    cs = np.arange(ncp) * CMP_STRIDE
    ss = np.arange(ns) * SLC_LEN
    ov = np.clip(np.minimum(cs[None, :] + CMP_LEN, ss[:, None] + SLC_LEN) - np.maximum(cs[None, :], ss[:, None]), 0, None)
    ovT = (ov.astype(np.float32) / CMP_LEN)
    ovT[:, ncp - 1] = 0.0
    h = np.arange(NSA_HEADS).reshape(NSA_KV_HEADS, NSA_GROUP) + 1
    slopes = np.exp2(-8.0 * h / NSA_HEADS).astype(np.float32)
    slopes = np.repeat(slopes, Q_BLOCK, axis=1)
    parts, rest = [], np.float64(LOG2E)
    for _ in range(3):
        part = np.float64(np.asarray(rest).astype(BF16))
        parts.append(part)
        rest = rest - part
    qaug = np.zeros((NSA_KV_HEADS, HEAD_DIM, GQ), np.float32)
    for k, part in enumerate(parts):
        qaug[:, k, :] = part * SLC_LEN * slopes
        qaug[:, 3 + k, :] = part * slopes
    kq = np.arange(Q_BLOCK)[:, None] - np.arange(Q_BLOCK)[None, :]
    causb = np.where(kq <= 0, 0.0, NEG).astype(np.float32)
    lowb = np.where(kq > 0, 0.0, NEG).astype(np.float32)
    return jnp.asarray(ovT, BF16), jnp.asarray(qaug, BF16), jnp.asarray(lowb), jnp.asarray(causb)


def _mixer(x2, B, T, norm_g, w_in, q_norm_g, k_norm_g, cmp_pos, w_cmp, ret_norm_g, w_out):
    ns = T // SLC_LEN
    q, kv, kc, vc, gt, pret = _inproj(x2, norm_g[None, :], _permute_w_in(w_in))
    qT, ks, kw, vsT, vwT, gT = _prep(q, kv, gt, q_norm_g[None, :], k_norm_g[1:3], B, T)
    wk, pk = _compress_weights(w_cmp[0], cmp_pos[0])
    wv, pv = _compress_weights(w_cmp[1], cmp_pos[1])
    kcmp, vcT = _compress(kc, vc, wk, wv, pk, pv, k_norm_g[0:1], B, T)
    ovT, qaug, lowb, causb = _nsa_consts(T)
    ocmp, sel, flags = _nsa_cmp(qT, qaug, kcmp, vcT, ovT, B, T)
    lists, counts = _nsa_steps(flags[:, :, :, 0, :].reshape(-1, ns), T // Q_BLOCK)
    kpad = jnp.zeros((WIN, 2 * HEAD_DIM), BF16).at[:, HEAD_DIM:HEAD_DIM + 3].set(-2.0 ** 100)
    kw = jnp.concatenate([jnp.broadcast_to(kpad, kw.shape[:2] + kpad.shape), kw], axis=2)
    vwT = jnp.pad(vwT, ((0, 0), (0, 0), (0, 0), (WIN, 0)))
    nsa = _nsa_main(lists, counts, qT, qaug, ks, vsT, kw, vwT, sel, gT, ocmp, lowb, causb, B, T)
    ret = _retention(pret, ret_norm_g[None, :], B, T)
    return _outproj(x2, nsa.reshape(B * T, NSA_WIDTH), ret, w_out.astype(BF16))


def _moe_layer(x2, norm_g, router, router_b, wg, wu, wd, tm=1024):
    tm = min(tm, x2.shape[0])
    h, rank, comb, rankT, cnt = _router(x2, norm_g[None, :], router, router_b, tm)
    counts = cnt[:, 0, :N_EXPERTS].reshape(-1)
    return _moe(counts, h, rankT, rank, comb, wg.astype(BF16), wu.astype(BF16), wd.astype(BF16), x2, tm)


def kernel(x, norm_mix_g, w_in, q_norm_g, k_norm_g, cmp_pos, w_cmp, ret_norm_g, w_out, norm_ffn_g,
           ffn_w_gate, ffn_w_up, ffn_w_down, moe_router, moe_router_b, moe_w_gate, moe_w_up, moe_w_down):
    B, T, D = x.shape
    depth = norm_mix_g.shape[0]
    x2 = x.reshape(B * T, D)
    for l in range(depth):
        x2 = _mixer(x2, B, T, norm_mix_g[l], w_in[l], q_norm_g[l], k_norm_g[l], cmp_pos[l], w_cmp[l],
                    ret_norm_g[l], w_out[l])
        j = l // 2
        if l % 2 == 0:
            x2 = _ffn(x2, norm_ffn_g[l][None, :], ffn_w_gate[j].astype(BF16), ffn_w_up[j].astype(BF16),
                      ffn_w_down[j].astype(BF16))
        else:
            x2 = _moe_layer(x2, norm_ffn_g[l], moe_router[j], moe_router_b[j], moe_w_gate[j], moe_w_up[j],
                            moe_w_down[j])
    return x2.reshape(B, T, D)
```

```python
import functools

import numpy as np
import jax
import jax.numpy as jnp
from jax import lax
from jax.experimental import pallas as pl
from jax.experimental.pallas import tpu as pltpu

F32 = jnp.float32
BF16 = jnp.bfloat16

HEAD_DIM = 64
NSA_HEADS = 8
NSA_KV_HEADS = 2
NSA_GROUP = NSA_HEADS // NSA_KV_HEADS
RET_HEADS = 8
RET_DK = 32
RET_DV = 64
NSA_WIDTH = NSA_HEADS * HEAD_DIM
RET_WIDTH = RET_HEADS * RET_DV
KV_WIDTH = NSA_KV_HEADS * HEAD_DIM
CMP_LEN = 32
CMP_STRIDE = 16
SLC_LEN = 64
SLC_TOPK = 16
WIN = 512
Q_BLOCK = 128
RET_CHUNK = 128
N_EXPERTS = 8
EPS = 1e-6
NEG = -1e30
BIG = 1e9
LANES = 128
GQ = NSA_GROUP * Q_BLOCK
KEY_STEP = 128
STEP_GROUP = 4
N_FORCED = 3
CMP_CHUNK = 128
CMP_TILES = 8
MAIN_TILES = 4
LOOP_TILES = 2
CMP_TAIL = CMP_CHUNK + 8
WIN_KEYS = WIN + Q_BLOCK
V_ROWS = HEAD_DIM + 16
LOG2E = 1.4426950408889634
VMEM_LIMIT = 60 * 1024 * 1024

_C_Q = 0
_C_KV = _C_Q + NSA_WIDTH
_C_KC = _C_KV + 4 * KV_WIDTH
_C_VC = _C_KC + KV_WIDTH
_C_GT = _C_VC + KV_WIDTH
_C_RET = _C_GT + LANES
_RET_COLS = 2 * RET_HEADS * RET_DK + 2 * RET_WIDTH
_C_END = _C_RET + _RET_COLS


def _params(n_axes, vmem=VMEM_LIMIT):
    return pltpu.CompilerParams(dimension_semantics=("arbitrary",) * n_axes, vmem_limit_bytes=vmem)


def _dot(a, b):
    return jnp.dot(a, b, preferred_element_type=F32)


def _dot_nt(a, b):
    return lax.dot_general(a, b, (((1,), (1,)), ((), ())), preferred_element_type=F32)


def _rms(x, g):
    return x * lax.rsqrt(jnp.mean(x * x, axis=-1, keepdims=True) + EPS) * g


def _group_rms(x, g, ones_ref):
    w = x.shape[1]
    ones = ones_ref[0:w, 0:w]
    hi, mid, lo = _split3(x * x)
    ms = (_dot(hi, ones) + _dot(mid, ones) + _dot(lo, ones)) * (1.0 / HEAD_DIM)
    return x * lax.rsqrt(ms + EPS) * g


def _group_ones():
    lane = np.arange(NSA_WIDTH) // HEAD_DIM
    return jnp.asarray(lane[:, None] == lane[None, :], BF16)


def _inproj_kernel(x_ref, g_ref, w_ref, qg_ref, kg_ref, ones_ref, kc_ref, vc_ref, ret_ref,
                   qT_ref, ks_ref, kw_ref, vsT_ref, vwT_ref, gT_ref, *, steps_per_row):
    n_tok = x_ref.shape[0]
    tiles = range(n_tok // Q_BLOCK)
    rows = [slice(u * Q_BLOCK, (u + 1) * Q_BLOCK) for u in tiles]
    xn = _rms(x_ref[...], g_ref[...]).astype(BF16)
    kc_ref[...] = _dot(xn, w_ref[:, _C_KC:_C_VC])
    vc_ref[...] = _dot(xn, w_ref[:, _C_VC:_C_GT])
    ret_ref[...] = _dot(xn, w_ref[:, _C_RET:_C_END])
    q = _dot(xn, w_ref[:, _C_Q:_C_KV])
    kv = _dot(xn, w_ref[:, _C_KV:_C_KC])
    gt = _dot(xn, w_ref[:, _C_GT:_C_RET])

    scale = HEAD_DIM ** -0.5 * LOG2E
    qn = _group_rms(q, qg_ref[...], ones_ref) * scale
    qt = [qn[rows[u]].T for u in tiles]
    for g in range(NSA_KV_HEADS):
        for r in range(NSA_GROUP):
            h = g * NSA_GROUP + r
            for u in tiles:
                qT_ref[g, u, :, r * Q_BLOCK:(r + 1) * Q_BLOCK] = qt[u][h * HEAD_DIM:(h + 1) * HEAD_DIM, :].astype(BF16)
    vst = [kv[rows[u], 2 * KV_WIDTH:3 * KV_WIDTH].T for u in tiles]
    vwt = [kv[rows[u], 3 * KV_WIDTH:4 * KV_WIDTH].T for u in tiles]
    gts = [jax.nn.sigmoid(gt[rows[u], :].T[0:32, :]) for u in tiles]
    pos0 = (pl.program_id(0) % steps_per_row) * n_tok
    pos = pos0 + lax.broadcasted_iota(jnp.int32, (n_tok, HEAD_DIM), 0)
    col = lax.broadcasted_iota(jnp.int32, (n_tok, HEAD_DIM), 1)
    kpos = jnp.where(col < 3, pos // SLC_LEN, jnp.where(col < 6, pos % SLC_LEN, 0)).astype(F32)
    ones_row = (lax.broadcasted_iota(jnp.int32, (V_ROWS - HEAD_DIM, Q_BLOCK), 0) == 0).astype(F32)
    ks = _group_rms(kv[:, 0:KV_WIDTH], kg_ref[0:1, :], ones_ref)
    kw = _group_rms(kv[:, KV_WIDTH:2 * KV_WIDTH], kg_ref[1:2, :], ones_ref)
    for g in range(NSA_KV_HEADS):
        sl = slice(g * HEAD_DIM, (g + 1) * HEAD_DIM)
        ks_ref[g] = jnp.concatenate([ks[:, sl], kpos], axis=1).astype(BF16)
        kw_ref[g] = jnp.concatenate([kw[:, sl], kpos], axis=1).astype(BF16)
        for u in tiles:
            vsT_ref[g, :, rows[u]] = jnp.concatenate([vst[u][sl, :], ones_row], axis=0).astype(BF16)
            vwT_ref[g, :, rows[u]] = jnp.concatenate([vwt[u][sl, :], ones_row], axis=0).astype(BF16)
    for u in tiles:
        gT_ref[u] = gts[u]


def _inproj(x2, g, w, qg, kg, B, T, tm=1024):
    n, d = x2.shape
    nq = T // Q_BLOCK
    G = NSA_KV_HEADS
    tiles = tm // Q_BLOCK
    spr = T // tm
    assert T % tm == 0
    const = lambda i: (0, 0)
    row = lambda i: (i, 0)
    return pl.pallas_call(
        functools.partial(_inproj_kernel, steps_per_row=spr),
        grid=(n // tm,),
        in_specs=[pl.BlockSpec((tm, d), row),
                  pl.BlockSpec((1, d), const),
                  pl.BlockSpec((d, _C_END), const),
                  pl.BlockSpec((1, NSA_WIDTH), const),
                  pl.BlockSpec((2, KV_WIDTH), const),
                  pl.BlockSpec((NSA_WIDTH, NSA_WIDTH), const)],
        out_specs=[pl.BlockSpec((tm, KV_WIDTH), row),
                   pl.BlockSpec((tm, KV_WIDTH), row),
                   pl.BlockSpec((tm, _RET_COLS), row),
                   pl.BlockSpec((None, G, tiles, HEAD_DIM, GQ), lambda i: (i // spr, 0, i % spr, 0, 0)),
                   pl.BlockSpec((None, G, tm, 2 * HEAD_DIM), lambda i: (i // spr, 0, i % spr, 0)),
                   pl.BlockSpec((None, G, tm, 2 * HEAD_DIM), lambda i: (i // spr, 0, i % spr, 0)),
                   pl.BlockSpec((None, G, V_ROWS, tm), lambda i: (i // spr, 0, 0, i % spr)),
                   pl.BlockSpec((None, G, V_ROWS, tm), lambda i: (i // spr, 0, 0, i % spr)),
                   pl.BlockSpec((None, tiles, 32, Q_BLOCK), lambda i: (i // spr, i % spr, 0, 0))],
        out_shape=[jax.ShapeDtypeStruct((n, KV_WIDTH), F32),
                   jax.ShapeDtypeStruct((n, KV_WIDTH), F32),
                   jax.ShapeDtypeStruct((n, _RET_COLS), F32),
                   jax.ShapeDtypeStruct((B, G, nq, HEAD_DIM, GQ), BF16),
                   jax.ShapeDtypeStruct((B, G, T, 2 * HEAD_DIM), BF16),
                   jax.ShapeDtypeStruct((B, G, T, 2 * HEAD_DIM), BF16),
                   jax.ShapeDtypeStruct((B, G, V_ROWS, T), BF16),
                   jax.ShapeDtypeStruct((B, G, V_ROWS, T), BF16),
                   jax.ShapeDtypeStruct((B, nq, 32, Q_BLOCK), F32)],
        compiler_params=_params(1),
        name="inproj",
    )(x2, g, w, jnp.tile(qg, (1, NSA_HEADS)), jnp.tile(kg, (1, NSA_KV_HEADS)), _group_ones())


def _compress_kernel(kc_ref, vc_ref, wk_ref, wv_ref, pk_ref, pv_ref, kg_ref, kcmp_ref, vcT_ref):
    ncp = kc_ref.shape[0] // CMP_STRIDE

    def comp(a_ref, w_ref, p_ref):
        lo = jnp.zeros((ncp, KV_WIDTH), F32)
        hi = jnp.zeros((ncp, KV_WIDTH), F32)
        for l in range(CMP_STRIDE):
            a = a_ref[pl.ds(l, ncp, stride=CMP_STRIDE), :]
            lo += _dot((a + p_ref[0, l:l + 1, :]).astype(BF16), w_ref[0, l])
            hi += _dot((a + p_ref[1, l:l + 1, :]).astype(BF16), w_ref[1, l])
        return lo + pltpu.roll(hi, ncp - 1, 0)

    k = comp(kc_ref, wk_ref, pk_ref)
    v = comp(vc_ref, wv_ref, pv_ref).T
    cend = lax.broadcasted_iota(jnp.int32, (ncp, HEAD_DIM), 0) * CMP_STRIDE + (CMP_LEN - 1)
    col = lax.broadcasted_iota(jnp.int32, (ncp, HEAD_DIM), 1)
    kpos = jnp.where(col < 3, cend // SLC_LEN, jnp.where(col < 6, cend % SLC_LEN, 0)).astype(F32)
    for g in range(NSA_KV_HEADS):
        sl = slice(g * HEAD_DIM, (g + 1) * HEAD_DIM)
        kcmp_ref[g] = jnp.concatenate([_rms(k[:, sl], kg_ref[...]), kpos], axis=1).astype(BF16)
        vcT_ref[g] = v[sl, :].astype(BF16)


def _compress(kc, vc, wk, wv, pk, pv, kg, B, T):
    ncp = T // CMP_STRIDE
    G = NSA_KV_HEADS
    const4 = lambda b: (0, 0, 0, 0)
    const3 = lambda b: (0, 0, 0)
    const2 = lambda b: (0, 0)
    return pl.pallas_call(
        _compress_kernel,
        grid=(B,),
        in_specs=[pl.BlockSpec((T, KV_WIDTH), lambda b: (b, 0)),
                  pl.BlockSpec((T, KV_WIDTH), lambda b: (b, 0)),
                  pl.BlockSpec((2, CMP_STRIDE, KV_WIDTH, KV_WIDTH), const4),
                  pl.BlockSpec((2, CMP_STRIDE, KV_WIDTH, KV_WIDTH), const4),
                  pl.BlockSpec((2, CMP_STRIDE, KV_WIDTH), const3),
                  pl.BlockSpec((2, CMP_STRIDE, KV_WIDTH), const3),
                  pl.BlockSpec((1, HEAD_DIM), const2)],
        out_specs=[pl.BlockSpec((None, G, ncp, 2 * HEAD_DIM), lambda b: (b, 0, 0, 0)),
                   pl.BlockSpec((None, G, HEAD_DIM, ncp), lambda b: (b, 0, 0, 0))],
        out_shape=[jax.ShapeDtypeStruct((B, G, ncp, 2 * HEAD_DIM), BF16),
                   jax.ShapeDtypeStruct((B, G, HEAD_DIM, ncp), BF16)],
        compiler_params=_params(1),
        name="nsa_compress",
    )(kc, vc, wk, wv, pk, pv, kg)


def _compress_weights(w, pos):
    G = NSA_KV_HEADS
    w4 = w.reshape(2, CMP_STRIDE, HEAD_DIM, HEAD_DIM)
    eye = jnp.eye(G, dtype=w.dtype)
    wbd = jnp.einsum('hlde,gk->hlgdke', w4, eye).reshape(2, CMP_STRIDE, KV_WIDTH, KV_WIDTH)
    p = pos.reshape(2, CMP_STRIDE, 1, HEAD_DIM)
    p = jnp.broadcast_to(p, (2, CMP_STRIDE, G, HEAD_DIM)).reshape(2, CMP_STRIDE, KV_WIDTH)
    return wbd.astype(BF16), p


def _split3(x):
    hi = x.astype(BF16)
    r = x - hi.astype(F32)
    mid = r.astype(BF16)
    lo = (r - mid.astype(F32)).astype(BF16)
    return hi, mid, lo


def _nsa_cmp_kernel(qT_ref, qaug_ref, kc_ref, vcT_ref, ovT_ref, ocmp_ref, sel_ref, flag_ref, *, n_sel):
    ncp = kc_ref.shape[0]
    ns = ovT_ref.shape[0]
    tiles = range(CMP_TILES)
    i0 = pl.program_id(2) * CMP_TILES
    lane = lax.broadcasted_iota(jnp.int32, (1, GQ), 1)
    q = [jnp.concatenate([qT_ref[u], qaug_ref[...]], axis=0) for u in tiles]
    t_row = [(i0 + u) * Q_BLOCK + (lane & (Q_BLOCK - 1)) for u in tiles]
    has_cmp = [(t_row[u] >= CMP_LEN - 1).astype(F32) for u in tiles]
    tq = [(i0 + u) * Q_BLOCK + lax.broadcasted_iota(jnp.int32, (1, Q_BLOCK), 1) for u in tiles]
    cur = [tq[u] // SLC_LEN for u in tiles]

    def prefix(rows):
        nsk = rows * CMP_STRIDE // SLC_LEN
        tail0 = max(rows - CMP_TAIL, 0)
        kc = kc_ref[0:rows, :]
        s = [_dot(kc, q[u]) for u in tiles]
        cend = (lax.broadcasted_iota(jnp.int32, (rows - tail0, 1), 0) + tail0) * CMP_STRIDE + (CMP_LEN - 1)
        tail = [jnp.where(t_row[u] >= cend, s[u][tail0:], NEG) for u in tiles]
        s = [jnp.concatenate([s[u][0:tail0], tail[u]], axis=0) if tail0 else tail[u] for u in tiles]
        m = [jnp.max(s[u], axis=0, keepdims=True) for u in tiles]
        e = [jnp.exp2(s[u] - m[u]) for u in tiles]
        p = [e[u] * (has_cmp[u] / jnp.sum(e[u], axis=0, keepdims=True)) for u in tiles]
        vc = vcT_ref[:, 0:rows]
        for u in tiles:
            ocmp_ref[u] = _dot(vc, p[u].astype(BF16))

        ps = [p[u][:, 0:Q_BLOCK] for u in tiles]
        for r in range(1, NSA_GROUP):
            ps = [ps[u] + p[u][:, r * Q_BLOCK:(r + 1) * Q_BLOCK] for u in tiles]
        ov = ovT_ref[0:nsk, 0:rows]
        split = [_split3(ps[u]) for u in tiles]
        imp = [_dot(ov, split[u][0]) + _dot(ov, split[u][1]) + _dot(ov, split[u][2]) for u in tiles]

        blk = lax.broadcasted_iota(jnp.int32, (nsk, 1), 0)
        forced = [(blk == 0) | (blk == cur[u]) | (blk == cur[u] - 1) for u in tiles]
        valid = [blk * SLC_LEN <= tq[u] for u in tiles]
        imp = [jnp.where(forced[u], -3e38, jnp.where(valid[u], imp[u], -BIG)) for u in tiles]
        blk_f = blk.astype(F32)
        sel = [forced[u].astype(F32) for u in tiles]
        for _ in range(n_sel - N_FORCED):
            mx = [jnp.max(imp[u], axis=0, keepdims=True) for u in tiles]
            idx = [jnp.min(jnp.where(imp[u] == mx[u], blk_f, float(ns)), axis=0, keepdims=True) for u in tiles]
            pick = [blk_f == idx[u] for u in tiles]
            sel = [jnp.where(pick[u], 1.0, sel[u]) for u in tiles]
            imp = [jnp.where(pick[u], -3e38, imp[u]) for u in tiles]
        ones = jnp.ones((8, Q_BLOCK), BF16)
        for u in tiles:
            sel_ref[u, 0:nsk, :] = sel[u]
            cnt = _dot_nt(ones, sel[u].astype(BF16))
            flag_ref[u, :, 0:nsk] = (cnt > 0).astype(jnp.int32)
            if nsk < ns:
                sel_ref[u, nsk:, :] = jnp.zeros((ns - nsk, Q_BLOCK), F32)
                flag_ref[u, :, nsk:] = jnp.zeros((8, ns - nsk), jnp.int32)

    n_variants = ncp // CMP_CHUNK
    last = i0 + CMP_TILES - 1
    variant = (last * (Q_BLOCK // CMP_STRIDE) + (Q_BLOCK // CMP_STRIDE - 2)) // CMP_CHUNK
    for k in range(n_variants):
        pl.when(variant == k)(functools.partial(prefix, (k + 1) * CMP_CHUNK))


def _nsa_cmp(qT, qaug, kcmp, vcT, ovT, B, T):
    G = NSA_KV_HEADS
    nq = T // Q_BLOCK
    ncp = T // CMP_STRIDE
    ns = T // SLC_LEN
    n_sel = min(SLC_TOPK, ns)
    assert ncp % CMP_CHUNK == 0 and n_sel > N_FORCED and nq % CMP_TILES == 0
    tile = lambda b, g, i: (b, g, i, 0, 0)
    return pl.pallas_call(
        functools.partial(_nsa_cmp_kernel, n_sel=n_sel),
        grid=(B, G, nq // CMP_TILES),
        in_specs=[pl.BlockSpec((None, None, CMP_TILES, HEAD_DIM, GQ), tile),
                  pl.BlockSpec((None, HEAD_DIM, GQ), lambda b, g, i: (g, 0, 0)),
                  pl.BlockSpec((None, None, ncp, 2 * HEAD_DIM), lambda b, g, i: (b, g, 0, 0)),
                  pl.BlockSpec((None, None, HEAD_DIM, ncp), lambda b, g, i: (b, g, 0, 0)),
                  pl.BlockSpec((ns, ncp), lambda b, g, i: (0, 0))],
        out_specs=[pl.BlockSpec((None, None, CMP_TILES, HEAD_DIM, GQ), tile),
                   pl.BlockSpec((None, None, CMP_TILES, ns, Q_BLOCK), tile),
                   pl.BlockSpec((None, None, CMP_TILES, 8, ns), tile)],
        out_shape=[jax.ShapeDtypeStruct((B, G, nq, HEAD_DIM, GQ), F32),
                   jax.ShapeDtypeStruct((B, G, nq, ns, Q_BLOCK), F32),
                   jax.ShapeDtypeStruct((B, G, nq, 8, ns), jnp.int32)],
        compiler_params=_params(3),
        name="nsa_cmp",
    )(qT, qaug, kcmp, vcT, ovT)


def _nsa_main_kernel(list_ref, cnt_ref, qT_ref, qaug_ref, ks_ref, vsT_ref, kw_ref, vwT_ref, sel_ref, gT_ref, ocmp_ref,
                     lowb_ref, causb_ref, out_ref, m_sc, acc_sc, win_sc):
    b, g = pl.program_id(0), pl.program_id(1)
    tiles = range(MAIN_TILES)
    i = [pl.program_id(2) * MAIN_TILES + u for u in tiles]
    tile_id = [(b * pl.num_programs(1) + g) * (pl.num_programs(2) * MAIN_TILES) + i[u] for u in tiles]
    n_steps = sel_ref.shape[1] // 2
    q = [jnp.concatenate([qT_ref[u], qaug_ref[...]], axis=0) for u in tiles]
    k0 = [pl.multiple_of(i[u] * Q_BLOCK, Q_BLOCK) for u in tiles]

    def sel_bias(u, j, valid):
        def row(r):
            picked = (sel_ref[u, pl.ds(r, 1), :] > 0.5) & valid
            return jnp.concatenate([jnp.where(picked, 0.0, NEG)] * NSA_GROUP, axis=1)
        return row(2 * j), row(2 * j + 1)

    def add_sel_bias(s, ba, bb):
        return jnp.concatenate([s[0:SLC_LEN] + ba, s[SLC_LEN:] + bb], axis=0)

    lowb = jnp.concatenate([lowb_ref[...]] * NSA_GROUP, axis=1)
    causb = jnp.concatenate([causb_ref[...]] * NSA_GROUP, axis=1)

    bias_d = [sel_bias(u, i[u], True) for u in tiles]
    sd = [_dot(ks_ref[pl.ds(k0[u], KEY_STEP), :], q[u]) for u in tiles]
    sw = [_dot(kw_ref[pl.ds(k0[u], WIN_KEYS), :], q[u]) for u in tiles]
    sd = [add_sel_bias(sd[u], *bias_d[u]) + causb for u in tiles]
    sw = [jnp.concatenate([sw[u][0:Q_BLOCK] + lowb, sw[u][Q_BLOCK:WIN], sw[u][WIN:] + causb], axis=0) for u in tiles]
    md = [jnp.max(sd[u], axis=0, keepdims=True) for u in tiles]
    mw = [jnp.max(sw[u], axis=0, keepdims=True) for u in tiles]
    accd = [_dot(vsT_ref[:, pl.ds(k0[u], KEY_STEP)], jnp.exp2((sd[u] - md[u]).astype(BF16))) for u in tiles]
    ow = [_dot(vwT_ref[:, pl.ds(k0[u], WIN_KEYS)], jnp.exp2((sw[u] - mw[u]).astype(BF16))) for u in tiles]
    for u in tiles:
        m_sc[u] = md[u]
        acc_sc[u] = accd[u]
        win_sc[u] = ow[u][0:HEAD_DIM] / ow[u][HEAD_DIM:HEAD_DIM + 1]

    def scores(u, t):
        ks, vs, biases = [], [], []
        for x in range(STEP_GROUP):
            j = list_ref[tile_id[u] * n_steps + t * STEP_GROUP + x]
            valid = j >= 0
            j = jnp.maximum(j, 0)
            kj = pl.multiple_of(j * KEY_STEP, KEY_STEP)
            ks.append(ks_ref[pl.ds(kj, KEY_STEP), :])
            vs.append(vsT_ref[:, pl.ds(kj, KEY_STEP)])
            biases.append(sel_bias(u, j, valid))
        s = _dot(jnp.concatenate(ks, axis=0), q[u])
        s = jnp.concatenate([add_sel_bias(s[x * KEY_STEP:(x + 1) * KEY_STEP], *biases[x])
                             for x in range(STEP_GROUP)], axis=0)
        return s, jnp.max(s, axis=0, keepdims=True), jnp.concatenate(vs, axis=1)

    def accumulate(u, s, smax, vcat):
        m_old = m_sc[u]
        m_new = jnp.maximum(m_old, smax)
        alpha = jnp.exp2(m_old - m_new)
        acc_sc[u] = alpha * acc_sc[u] + _dot(vcat, jnp.exp2((s - m_new).astype(BF16)))
        m_sc[u] = m_new

    def run(work, t, carry):
        staged = [(u, scores(u, t * mult + off)) for (u, mult, off) in work]
        for u, args in staged:
            accumulate(u, *args)
        return carry

    for u0 in range(0, MAIN_TILES, LOOP_TILES):
        us = range(u0, u0 + LOOP_TILES)
        n_groups = functools.reduce(
            jnp.maximum, [(cnt_ref[tile_id[u]] + (STEP_GROUP - 1)) // STEP_GROUP for u in us])
        lax.fori_loop(0, n_groups // 2, functools.partial(run, [(u, 2, off) for off in (0, 1) for u in us]), 0)
        lax.fori_loop(n_groups // 2 * 2, n_groups, functools.partial(run, [(u, 1, 0) for u in us]), 0)

    def gate(u, k):
        rows = [gT_ref[u, pl.ds(g * (NSA_GROUP * 3) + r * 3 + k, 1), :] for r in range(NSA_GROUP)]
        return jnp.concatenate(rows, axis=1)

    o_slc = [acc_sc[u, 0:HEAD_DIM, :] / acc_sc[u, HEAD_DIM:HEAD_DIM + 1, :] for u in tiles]
    o = [gate(u, 0) * ocmp_ref[u] + gate(u, 1) * o_slc[u] + gate(u, 2) * win_sc[u] for u in tiles]
    o = [jnp.concatenate([o[u], jnp.zeros_like(o[u])], axis=0) for u in tiles]
    for r in range(NSA_GROUP):
        ot = [o[u][:, r * Q_BLOCK:(r + 1) * Q_BLOCK].T[:, 0:HEAD_DIM] for u in tiles]
        for u in tiles:
            out_ref[u * Q_BLOCK:(u + 1) * Q_BLOCK, r * HEAD_DIM:(r + 1) * HEAD_DIM] = ot[u]


def _nsa_steps_kernel(flagT_ref, pairT_ref, list_ref, cnt_ref, *, nq):
    n_steps, nt = list_ref.shape
    need = _dot(pairT_ref[...], flagT_ref[...].astype(BF16)) > 0
    step = lax.broadcasted_iota(jnp.int32, (n_steps, 1), 0)
    own = lax.broadcasted_iota(jnp.int32, (1, nt), 1) % nq
    need = need & (step < own)
    need_f = need.astype(F32)
    earlier = (lax.broadcasted_iota(jnp.int32, (n_steps, n_steps), 1) < step).astype(BF16)
    slot = _dot(earlier, need_f.astype(BF16))
    total = jnp.sum(need_f, axis=0, keepdims=True)
    cnt_ref[...] = jnp.broadcast_to(total, cnt_ref.shape).astype(jnp.int32)
    step_f = step.astype(F32)
    for p in range(n_steps):
        val = jnp.sum(jnp.where(need & (slot == p), step_f, 0.0), axis=0, keepdims=True)
        list_ref[p:p + 1, :] = jnp.where(total > p, val, -1.0).astype(jnp.int32)


def _nsa_steps(flags, nq):
    nt, ns = flags.shape
    n_steps = ns // 2
    pairT = jnp.asarray(np.arange(n_steps)[:, None] == np.arange(ns)[None, :] // 2, BF16)
    lists, counts = pl.pallas_call(
        functools.partial(_nsa_steps_kernel, nq=nq),
        out_shape=[jax.ShapeDtypeStruct((n_steps, nt), jnp.int32), jax.ShapeDtypeStruct((8, nt), jnp.int32)],
        name="nsa_steps",
    )(flags.T.astype(F32), pairT)
    return lists.T.reshape(-1), counts[0]


def _nsa_main(lists, counts, qT, qaug, ks, vsT, kw, vwT, sel, gT, ocmp, lowb, causb, B, T):
    G = NSA_KV_HEADS
    nq = T // Q_BLOCK
    ns = T // SLC_LEN
    whole = lambda b, g, i, *_: (b, g, 0, 0)
    tile = lambda b, g, i, *_: (b, g, i, 0, 0)
    const = lambda b, g, i, *_: (0, 0)
    grid_spec = pltpu.PrefetchScalarGridSpec(
        num_scalar_prefetch=2,
        grid=(B, G, nq // MAIN_TILES),
        in_specs=[pl.BlockSpec((None, None, MAIN_TILES, HEAD_DIM, GQ), tile),
                  pl.BlockSpec((None, HEAD_DIM, GQ), lambda b, g, i, *_: (g, 0, 0)),
                  pl.BlockSpec((None, None, T, 2 * HEAD_DIM), whole),
                  pl.BlockSpec((None, None, V_ROWS, T), whole),
                  pl.BlockSpec((None, None, T + WIN, 2 * HEAD_DIM), whole),
                  pl.BlockSpec((None, None, V_ROWS, T + WIN), whole),
                  pl.BlockSpec((None, None, MAIN_TILES, ns, Q_BLOCK), tile),
                  pl.BlockSpec((None, MAIN_TILES, 32, Q_BLOCK), lambda b, g, i, *_: (b, i, 0, 0)),
                  pl.BlockSpec((None, None, MAIN_TILES, HEAD_DIM, GQ), tile),
                  pl.BlockSpec((Q_BLOCK, Q_BLOCK), const),
                  pl.BlockSpec((Q_BLOCK, Q_BLOCK), const)],
        out_specs=pl.BlockSpec((None, MAIN_TILES * Q_BLOCK, NSA_GROUP * HEAD_DIM), lambda b, g, i, *_: (b, i, g)),
        scratch_shapes=[pltpu.VMEM((MAIN_TILES, 1, GQ), F32), pltpu.VMEM((MAIN_TILES, V_ROWS, GQ), F32),
                        pltpu.VMEM((MAIN_TILES, HEAD_DIM, GQ), F32)],
    )
    return pl.pallas_call(
        _nsa_main_kernel,
        grid_spec=grid_spec,
        out_shape=jax.ShapeDtypeStruct((B, T, NSA_WIDTH), F32),
        compiler_params=_params(3),
        name="nsa_main",
    )(lists, counts, qT, qaug, ks, vsT, kw, vwT, sel, gT, ocmp, lowb, causb)


def _ret_kernel(p_ref, decay_ref, xi_ref, zeta_ref, gch_ref, ng_ref, ones_ref, out_ref, state_ref):
    @pl.when(pl.program_id(0) == 0)
    def _():
        state_ref[...] = jnp.zeros(state_ref.shape, F32)

    rows = range(p_ref.shape[0])
    kw = RET_HEADS * RET_DK
    p = [p_ref[b] for b in rows]
    rq = [p[b][:, 0:kw] * (RET_DK ** -0.5) for b in rows]
    rk = [p[b][:, kw:2 * kw] for b in rows]
    rkT = [rk[b].T for b in rows]
    rv = [p[b][:, 2 * kw:2 * kw + RET_WIDTH] for b in rows]
    xi = xi_ref[...]
    outs = [[] for _ in rows]
    for h in range(RET_HEADS):
        dk = slice(h * RET_DK, (h + 1) * RET_DK)
        dv = slice(h * RET_DV, (h + 1) * RET_DV)
        st = [state_ref[b, h] for b in rows]
        inner = [_dot_nt(rq[b][:, dk], rk[b][:, dk]) * decay_ref[h] for b in rows]
        o = [_dot(inner[b], rv[b][:, dv]) + _dot(rq[b][:, dk], st[b]) * xi[:, h:h + 1] for b in rows]
        for b in rows:
            state_ref[b, h] = (st[b] * gch_ref[h:h + 1, 0:1]
                               + _dot(rkT[b][dk, :] * zeta_ref[h:h + 1, :], rv[b][:, dv]))
            outs[b].append(o[b])
    normed = [_group_rms(jnp.concatenate(outs[b], axis=1), ng_ref[...], ones_ref) for b in rows]
    for b in rows:
        rg = p[b][:, 2 * kw + RET_WIDTH:2 * kw + 2 * RET_WIDTH]
        out_ref[b] = normed[b] * (rg * jax.nn.sigmoid(rg))


def _ret_consts():
    H, C = RET_HEADS, RET_CHUNK
    log_g = jnp.log1p(-jnp.exp2(-5.0 - jnp.arange(H, dtype=F32)))
    idx = jnp.arange(C, dtype=F32)
    diff = idx[:, None] - idx[None, :]
    decay = jnp.where(diff >= 0, jnp.exp(jnp.maximum(diff, 0.0) * log_g[:, None, None]), 0.0)
    zeta = jnp.exp((C - 1 - idx) * log_g[:, None])
    xi = jnp.exp((idx + 1) * log_g[:, None]).T
    g_chunk = jnp.broadcast_to(jnp.exp(C * log_g)[:, None], (H, LANES))
    return decay, xi, zeta, g_chunk


def _retention(pret, ng, B, T):
    nch = T // RET_CHUNK
    decay, xi, zeta, gch = _ret_consts()
    c2 = lambda c: (0, 0)
    out = pl.pallas_call(
        _ret_kernel,
        grid=(nch,),
        in_specs=[pl.BlockSpec((B, RET_CHUNK, _RET_COLS), lambda c: (0, c, 0)),
                  pl.BlockSpec((RET_HEADS, RET_CHUNK, RET_CHUNK), lambda c: (0, 0, 0)),
                  pl.BlockSpec((RET_CHUNK, RET_HEADS), c2),
                  pl.BlockSpec((RET_HEADS, RET_CHUNK), c2),
                  pl.BlockSpec((RET_HEADS, LANES), c2),
                  pl.BlockSpec((1, RET_WIDTH), c2),
                  pl.BlockSpec((RET_WIDTH, RET_WIDTH), c2)],
        out_specs=pl.BlockSpec((B, RET_CHUNK, RET_WIDTH), lambda c: (0, c, 0)),
        out_shape=jax.ShapeDtypeStruct((B, T, RET_WIDTH), F32),
        scratch_shapes=[pltpu.VMEM((B, RET_HEADS, RET_DK, RET_DV), F32)],
        compiler_params=_params(1),
        name="retention",
    )(pret.reshape(B, T, _RET_COLS), decay, xi, zeta, gch, ng, _group_ones())
    return out.reshape(B * T, RET_WIDTH)


def _outproj_kernel(x_ref, nsa_ref, ret_ref, w_ref, o_ref):
    o_ref[...] = (x_ref[...] + _dot(nsa_ref[...].astype(BF16), w_ref[0:NSA_WIDTH, :])
                  + _dot(ret_ref[...].astype(BF16), w_ref[NSA_WIDTH:, :]))


def _outproj(x2, nsa, ret, w, tm=512):
    n, d = x2.shape
    return pl.pallas_call(
        _outproj_kernel,
        grid=(n // tm,),
        in_specs=[pl.BlockSpec((tm, d), lambda i: (i, 0)),
                  pl.BlockSpec((tm, NSA_WIDTH), lambda i: (i, 0)),
                  pl.BlockSpec((tm, RET_WIDTH), lambda i: (i, 0)),
                  pl.BlockSpec((NSA_WIDTH + RET_WIDTH, d), lambda i: (0, 0))],
        out_specs=pl.BlockSpec((tm, d), lambda i: (i, 0)),
        out_shape=jax.ShapeDtypeStruct((n, d), F32),
        compiler_params=_params(1),
        name="outproj",
    )(x2, nsa, ret, w)


def _ffn_kernel(x_ref, g_ref, wg_ref, wu_ref, wd_ref, o_ref, h_sc):
    f = pl.program_id(1)

    @pl.when(f == 0)
    def _():
        x = x_ref[...]
        h_sc[...] = _rms(x, g_ref[...]).astype(BF16)
        o_ref[...] = x

    h = h_sc[...]
    a = _dot(h, wg_ref[...])
    act = (a * jax.nn.sigmoid(a) * _dot(h, wu_ref[...])).astype(BF16)
    o_ref[...] += _dot(act, wd_ref[...])


def _ffn(x2, g, wg, wu, wd, tm=1024, fc=1408):
    n, d = x2.shape
    dff = wg.shape[1]
    return pl.pallas_call(
        _ffn_kernel,
        grid=(n // tm, dff // fc),
        in_specs=[pl.BlockSpec((tm, d), lambda i, f: (i, 0)),
                  pl.BlockSpec((1, d), lambda i, f: (0, 0)),
                  pl.BlockSpec((d, fc), lambda i, f: (0, f)),
                  pl.BlockSpec((d, fc), lambda i, f: (0, f)),
                  pl.BlockSpec((fc, d), lambda i, f: (f, 0))],
        out_specs=pl.BlockSpec((tm, d), lambda i, f: (i, 0)),
        out_shape=jax.ShapeDtypeStruct((n, d), F32),
        scratch_shapes=[pltpu.VMEM((tm, d), BF16)],
        compiler_params=_params(2),
        name="ffn_dense",
    )(x2, g, wg, wu, wd)


def _router_kernel(x_ref, g_ref, r_ref, rb_ref, tri_ref, h_ref, rank_ref, comb_ref, rankT_ref, cnt_ref):
    h = _rms(x_ref[...], g_ref[...])
    h_ref[...] = h.astype(BF16)
    hh, hm, hl = _split3(h)
    rh, rm, rl = _split3(r_ref[...])
    logits = (_dot(hh, rh) + (_dot(hh, rm) + _dot(hm, rh)) + (_dot(hh, rl) + _dot(hm, rm) + _dot(hl, rh))
              + rb_ref[...])
    lane = lax.broadcasted_iota(jnp.int32, logits.shape, 1).astype(F32)
    logits = jnp.where(lane < N_EXPERTS, logits, NEG)
    m1 = jnp.max(logits, axis=1, keepdims=True)
    i1 = jnp.min(jnp.where(logits == m1, lane, float(LANES)), axis=1, keepdims=True)
    l2 = jnp.where(lane == i1, NEG, logits)
    m2 = jnp.max(l2, axis=1, keepdims=True)
    i2 = jnp.min(jnp.where(l2 == m2, lane, float(LANES)), axis=1, keepdims=True)
    e2 = jnp.exp(m2 - m1)
    w1 = 1.0 / (1.0 + e2)
    w2 = e2 / (1.0 + e2)
    use1, use2 = lane == i1, lane == i2
    comb_ref[...] = jnp.where(use1, w1, 0.0) + jnp.where(use2, w2, 0.0)
    use = (use1 | use2).astype(F32)
    rank = jnp.where(use > 0, _dot(tri_ref[...], use.astype(BF16)), -1.0)
    rank_ref[...] = rank
    rankT_ref[...] = rank.T[0:N_EXPERTS, :]
    cnt_ref[...] = jnp.broadcast_to(jnp.sum(use, axis=0, keepdims=True), cnt_ref.shape).astype(jnp.int32)


def _router(x2, g, router, rb, tm):
    n, d = x2.shape
    nt = n // tm
    tri = (jnp.arange(tm)[:, None] > jnp.arange(tm)[None, :]).astype(BF16)
    rpad = jnp.zeros((d, LANES), F32).at[:, :N_EXPERTS].set(router)
    rbpad = jnp.zeros((1, LANES), F32).at[0, :N_EXPERTS].set(rb)
    c2 = lambda i: (0, 0)
    return pl.pallas_call(
        _router_kernel,
        grid=(nt,),
        in_specs=[pl.BlockSpec((tm, d), lambda i: (i, 0)),
                  pl.BlockSpec((1, d), c2),
                  pl.BlockSpec((d, LANES), c2),
                  pl.BlockSpec((1, LANES), c2),
                  pl.BlockSpec((tm, tm), c2)],
        out_specs=[pl.BlockSpec((tm, d), lambda i: (i, 0)),
                   pl.BlockSpec((tm, LANES), lambda i: (i, 0)),
                   pl.BlockSpec((tm, LANES), lambda i: (i, 0)),
                   pl.BlockSpec((N_EXPERTS, tm), lambda i: (0, i)),
                   pl.BlockSpec((None, 8, LANES), lambda i: (i, 0, 0))],
        out_shape=[jax.ShapeDtypeStruct((n, d), BF16),
                   jax.ShapeDtypeStruct((n, LANES), F32),
                   jax.ShapeDtypeStruct((n, LANES), F32),
                   jax.ShapeDtypeStruct((N_EXPERTS, n), F32),
                   jax.ShapeDtypeStruct((nt, 8, LANES), jnp.int32)],
        compiler_params=_params(1),
        name="moe_router",
    )(x2, g, rpad, rbpad, tri)


MOE_SUB = 144
MOE_MOVE = 2 * MOE_SUB


def _moe_kernel(cnt_ref, h_ref, rankT_ref, rank_ref, comb_ref, wg_ref, wu_ref, wd_ref, x_ref, o_ref, hc_sc, oacc_sc):
    t, e, f = pl.program_id(0), pl.program_id(1), pl.program_id(2)
    nf = pl.num_programs(2)
    tm = h_ref.shape[0]
    nsub = (cnt_ref[t * N_EXPERTS + e] + (MOE_SUB - 1)) // MOE_SUB
    nmove = (nsub + 1) // 2

    @pl.when((e == 0) & (f == 0))
    def _():
        o_ref[...] = x_ref[...]

    @pl.when(f == 0)
    def _():
        rank_row = rankT_ref[...]

        def gather(s, c):
            r0 = pl.multiple_of(s * MOE_MOVE, MOE_MOVE)
            rows = (lax.broadcasted_iota(jnp.int32, (MOE_MOVE, 1), 0) + r0).astype(F32)
            onehot = (rows == rank_row).astype(BF16)
            hc_sc[pl.ds(r0, MOE_MOVE), :] = _dot(onehot, h_ref[...]).astype(BF16)
            oacc_sc[pl.ds(r0, MOE_MOVE), :] = jnp.zeros((MOE_MOVE, oacc_sc.shape[1]), F32)
            return c

        lax.fori_loop(0, nmove, gather, 0)

    def expert(n_rows, s, c):
        r0 = pl.multiple_of(s * n_rows, n_rows)
        rows = hc_sc[pl.ds(r0, n_rows), :]
        a = _dot(rows, wg_ref[...])
        act = (a * jax.nn.sigmoid(a) * _dot(rows, wu_ref[...])).astype(BF16)
        oacc_sc[pl.ds(r0, n_rows), :] += _dot(act, wd_ref[...])
        return c

    lax.fori_loop(0, nsub // 2, functools.partial(expert, MOE_MOVE), 0)
    lax.fori_loop(nsub // 2 * 2, nsub, functools.partial(expert, MOE_SUB), 0)

    @pl.when(f == nf - 1)
    def _():
        is_e = lax.broadcasted_iota(jnp.int32, (1, LANES), 1) == e
        rank_col = jnp.sum(jnp.where(is_e, rank_ref[...], 0.0), axis=1, keepdims=True)
        comb_col = jnp.sum(jnp.where(is_e, comb_ref[...], 0.0), axis=1, keepdims=True)

        def scatter(s, c):
            r0 = pl.multiple_of(s * MOE_MOVE, MOE_MOVE)
            cols = (lax.broadcasted_iota(jnp.int32, (1, MOE_MOVE), 1) + r0).astype(F32)
            onehot = (rank_col == cols).astype(BF16)
            y = _dot(onehot, oacc_sc[pl.ds(r0, MOE_MOVE), :].astype(BF16))
            o_ref[...] += comb_col * y
            return c

        lax.fori_loop(0, nmove, scatter, 0)


def _moe(counts, h, rankT, rank, comb, wg, wu, wd, x2, tm, fc=1408):
    n, d = x2.shape
    dff = wg.shape[2]
    rows_cap = pl.cdiv(pl.cdiv(tm, MOE_SUB), 2) * MOE_MOVE
    grid_spec = pltpu.PrefetchScalarGridSpec(
        num_scalar_prefetch=1,
        grid=(n // tm, N_EXPERTS, dff // fc),
        in_specs=[pl.BlockSpec((tm, d), lambda t, e, f, c: (t, 0)),
                  pl.BlockSpec((None, 1, tm), lambda t, e, f, c: (e, 0, t)),
                  pl.BlockSpec((tm, LANES), lambda t, e, f, c: (t, 0)),
                  pl.BlockSpec((tm, LANES), lambda t, e, f, c: (t, 0)),
                  pl.BlockSpec((None, d, fc), lambda t, e, f, c: (e, 0, f)),
                  pl.BlockSpec((None, d, fc), lambda t, e, f, c: (e, 0, f)),
                  pl.BlockSpec((None, fc, d), lambda t, e, f, c: (e, f, 0)),
                  pl.BlockSpec((tm, d), lambda t, e, f, c: (t, 0))],
        out_specs=pl.BlockSpec((tm, d), lambda t, e, f, c: (t, 0)),
        scratch_shapes=[pltpu.VMEM((rows_cap, d), BF16), pltpu.VMEM((rows_cap, d), F32)],
    )
    return pl.pallas_call(
        _moe_kernel,
        grid_spec=grid_spec,
        out_shape=jax.ShapeDtypeStruct((n, d), F32),
        compiler_params=_params(3),
        name="moe_experts",
    )(counts, h, rankT.reshape(N_EXPERTS, 1, n), rank, comb, wg, wu, wd, x2)


def _permute_w_in(w):
    o = np.cumsum((0, NSA_WIDTH) + (KV_WIDTH,) * 6 + (3 * NSA_HEADS,))
    q, kc, vc, ks, vs, kw, vw, gts = (w[:, o[k]:o[k + 1]] for k in range(8))
    ret = w[:, o[8]:]
    pad = jnp.zeros((w.shape[0], LANES - 3 * NSA_HEADS), w.dtype)
    return jnp.concatenate([q, ks, kw, vs, vw, kc, vc, gts, pad, ret], axis=1).astype(BF16)


def _nsa_consts(T):
    ncp = T // CMP_STRIDE
    ns = T // SLC_LEN
    cs = np.arange(ncp) * CMP_STRIDE
    ss = np.arange(ns) * SLC_LEN
    ov = np.clip(np.minimum(cs[None, :] + CMP_LEN, ss[:, None] + SLC_LEN) - np.maximum(cs[None, :], ss[:, None]), 0, None)
    ovT = (ov.astype(np.float32) / CMP_LEN)
    ovT[:, ncp - 1] = 0.0
    h = np.arange(NSA_HEADS).reshape(NSA_KV_HEADS, NSA_GROUP) + 1
    slopes = np.exp2(-8.0 * h / NSA_HEADS).astype(np.float32)
    slopes = np.repeat(slopes, Q_BLOCK, axis=1)
    parts, rest = [], np.float64(LOG2E)
    for _ in range(3):
        part = np.float64(np.asarray(rest).astype(BF16))
        parts.append(part)
        rest = rest - part
    qaug = np.zeros((NSA_KV_HEADS, HEAD_DIM, GQ), np.float32)
    for k, part in enumerate(parts):
        qaug[:, k, :] = part * SLC_LEN * slopes
        qaug[:, 3 + k, :] = part * slopes
    kq = np.arange(Q_BLOCK)[:, None] - np.arange(Q_BLOCK)[None, :]
    causb = np.where(kq <= 0, 0.0, NEG).astype(np.float32)
    lowb = np.where(kq > 0, 0.0, NEG).astype(np.float32)
    return jnp.asarray(ovT, BF16), jnp.asarray(qaug, BF16), jnp.asarray(lowb), jnp.asarray(causb)


def _mixer(x2, B, T, norm_g, w_in, q_norm_g, k_norm_g, cmp_pos, w_cmp, ret_norm_g, w_out):
    ns = T // SLC_LEN
    kc, vc, pret, qT, ks, kw, vsT, vwT, gT = _inproj(x2, norm_g[None, :], _permute_w_in(w_in), q_norm_g[None, :],
                                                     k_norm_g[1:3], B, T)
    wk, pk = _compress_weights(w_cmp[0], cmp_pos[0])
    wv, pv = _compress_weights(w_cmp[1], cmp_pos[1])
    kcmp, vcT = _compress(kc, vc, wk, wv, pk, pv, k_norm_g[0:1], B, T)
    ovT, qaug, lowb, causb = _nsa_consts(T)
    ocmp, sel, flags = _nsa_cmp(qT, qaug, kcmp, vcT, ovT, B, T)
    lists, counts = _nsa_steps(flags[:, :, :, 0, :].reshape(-1, ns), T // Q_BLOCK)
    kpad = jnp.zeros((WIN, 2 * HEAD_DIM), BF16).at[:, HEAD_DIM:HEAD_DIM + 3].set(-2.0 ** 100)
    kw = jnp.concatenate([jnp.broadcast_to(kpad, kw.shape[:2] + kpad.shape), kw], axis=2)
    vwT = jnp.pad(vwT, ((0, 0), (0, 0), (0, 0), (WIN, 0)))
    nsa = _nsa_main(lists, counts, qT, qaug, ks, vsT, kw, vwT, sel, gT, ocmp, lowb, causb, B, T)
    ret = _retention(pret, ret_norm_g[None, :], B, T)
    return _outproj(x2, nsa.reshape(B * T, NSA_WIDTH), ret, w_out.astype(BF16))


def _moe_layer(x2, norm_g, router, router_b, wg, wu, wd, tm=1024):
    tm = min(tm, x2.shape[0])
    h, rank, comb, rankT, cnt = _router(x2, norm_g[None, :], router, router_b, tm)
    counts = cnt[:, 0, :N_EXPERTS].reshape(-1)
    return _moe(counts, h, rankT, rank, comb, wg.astype(BF16), wu.astype(BF16), wd.astype(BF16), x2, tm)


def kernel(x, norm_mix_g, w_in, q_norm_g, k_norm_g, cmp_pos, w_cmp, ret_norm_g, w_out, norm_ffn_g,
           ffn_w_gate, ffn_w_up, ffn_w_down, moe_router, moe_router_b, moe_w_gate, moe_w_up, moe_w_down):
    B, T, D = x.shape
    depth = norm_mix_g.shape[0]
    x2 = x.reshape(B * T, D)
    for l in range(depth):
        x2 = _mixer(x2, B, T, norm_mix_g[l], w_in[l], q_norm_g[l], k_norm_g[l], cmp_pos[l], w_cmp[l],
                    ret_norm_g[l], w_out[l])
        j = l // 2
        if l % 2 == 0:
            x2 = _ffn(x2, norm_ffn_g[l][None, :], ffn_w_gate[j].astype(BF16), ffn_w_up[j].astype(BF16),
                      ffn_w_down[j].astype(BF16))
        else:
            x2 = _moe_layer(x2, norm_ffn_g[l], moe_router[j], moe_router_b[j], moe_w_gate[j], moe_w_up[j],
                            moe_w_down[j])
    return x2.reshape(B, T, D)
```

```python
import functools

import numpy as np
import jax
import jax.numpy as jnp
from jax import lax
from jax.experimental import pallas as pl
from jax.experimental.pallas import tpu as pltpu

F32 = jnp.float32
BF16 = jnp.bfloat16

HEAD_DIM = 64
NSA_HEADS = 8
NSA_KV_HEADS = 2
NSA_GROUP = NSA_HEADS // NSA_KV_HEADS
RET_HEADS = 8
RET_DK = 32
RET_DV = 64
NSA_WIDTH = NSA_HEADS * HEAD_DIM
RET_WIDTH = RET_HEADS * RET_DV
KV_WIDTH = NSA_KV_HEADS * HEAD_DIM
CMP_LEN = 32
CMP_STRIDE = 16
SLC_LEN = 64
SLC_TOPK = 16
WIN = 512
Q_BLOCK = 128
RET_CHUNK = 128
N_EXPERTS = 8
EPS = 1e-6
NEG = -1e30
BIG = 1e9
LANES = 128
GQ = NSA_GROUP * Q_BLOCK
KEY_STEP = 128
STEP_GROUP = 4
N_FORCED = 3
CMP_CHUNK = 128
CMP_TILES = 8
MAIN_TILES = 8
LOOP_TILES = 2
CMP_TAIL = CMP_CHUNK + 8
WIN_KEYS = WIN + Q_BLOCK
V_ROWS = HEAD_DIM + 16
LOG2E = 1.4426950408889634
VMEM_LIMIT = 60 * 1024 * 1024

_C_Q = 0
_C_KV = _C_Q + NSA_WIDTH
_C_KC = _C_KV + 4 * KV_WIDTH
_C_VC = _C_KC + KV_WIDTH
_C_GT = _C_VC + KV_WIDTH
_C_RET = _C_GT + LANES
_RET_COLS = 2 * RET_HEADS * RET_DK + 2 * RET_WIDTH
_C_END = _C_RET + _RET_COLS


def _params(n_axes, vmem=VMEM_LIMIT):
    return pltpu.CompilerParams(dimension_semantics=("arbitrary",) * n_axes, vmem_limit_bytes=vmem)


def _dot(a, b):
    return jnp.dot(a, b, preferred_element_type=F32)


def _dot_nt(a, b):
    return lax.dot_general(a, b, (((1,), (1,)), ((), ())), preferred_element_type=F32)


def _rms(x, g):
    return x * lax.rsqrt(jnp.mean(x * x, axis=-1, keepdims=True) + EPS) * g


def _group_rms(x, g, ones_ref):
    w = x.shape[1]
    ones = ones_ref[0:w, 0:w]
    hi, mid, lo = _split3(x * x)
    ms = (_dot(hi, ones) + _dot(mid, ones) + _dot(lo, ones)) * (1.0 / HEAD_DIM)
    return x * lax.rsqrt(ms + EPS) * g


def _group_ones():
    lane = np.arange(NSA_WIDTH) // HEAD_DIM
    return jnp.asarray(lane[:, None] == lane[None, :], BF16)


def _inproj_kernel(x_ref, g_ref, w_ref, qg_ref, kg_ref, ones_ref, kc_ref, vc_ref, ret_ref,
                   qT_ref, ks_ref, kw_ref, vsT_ref, vwT_ref, gT_ref, *, steps_per_row):
    n_tok = x_ref.shape[0]
    tiles = range(n_tok // Q_BLOCK)
    rows = [slice(u * Q_BLOCK, (u + 1) * Q_BLOCK) for u in tiles]
    xn = _rms(x_ref[...], g_ref[...]).astype(BF16)
    q = _dot(xn, w_ref[:, _C_Q:_C_KV])
    kv = _dot(xn, w_ref[:, _C_KV:_C_KC])
    gt = _dot(xn, w_ref[:, _C_GT:_C_RET])

    scale = HEAD_DIM ** -0.5 * LOG2E
    qn = _group_rms(q, qg_ref[...], ones_ref) * scale
    qt = [qn[rows[u]].T for u in tiles]
    for g in range(NSA_KV_HEADS):
        for r in range(NSA_GROUP):
            h = g * NSA_GROUP + r
            for u in tiles:
                qT_ref[g, u, :, r * Q_BLOCK:(r + 1) * Q_BLOCK] = qt[u][h * HEAD_DIM:(h + 1) * HEAD_DIM, :].astype(BF16)
    vst = [kv[rows[u], 2 * KV_WIDTH:3 * KV_WIDTH].T for u in tiles]
    vwt = [kv[rows[u], 3 * KV_WIDTH:4 * KV_WIDTH].T for u in tiles]
    gts = [jax.nn.sigmoid(gt[rows[u], :].T[0:32, :]) for u in tiles]
    pos0 = (pl.program_id(0) % steps_per_row) * n_tok
    pos = pos0 + lax.broadcasted_iota(jnp.int32, (n_tok, HEAD_DIM), 0)
    col = lax.broadcasted_iota(jnp.int32, (n_tok, HEAD_DIM), 1)
    kpos = jnp.where(col < 3, pos // SLC_LEN, jnp.where(col < 6, pos % SLC_LEN, 0)).astype(F32)
    ones_row = (lax.broadcasted_iota(jnp.int32, (V_ROWS - HEAD_DIM, Q_BLOCK), 0) == 0).astype(F32)
    ks = _group_rms(kv[:, 0:KV_WIDTH], kg_ref[0:1, :], ones_ref)
    kw = _group_rms(kv[:, KV_WIDTH:2 * KV_WIDTH], kg_ref[1:2, :], ones_ref)
    for g in range(NSA_KV_HEADS):
        sl = slice(g * HEAD_DIM, (g + 1) * HEAD_DIM)
        ks_ref[g] = jnp.concatenate([ks[:, sl], kpos], axis=1).astype(BF16)
        kw_ref[g] = jnp.concatenate([kw[:, sl], kpos], axis=1).astype(BF16)
        for u in tiles:
            vsT_ref[g, :, rows[u]] = jnp.concatenate([vst[u][sl, :], ones_row], axis=0).astype(BF16)
            vwT_ref[g, :, rows[u]] = jnp.concatenate([vwt[u][sl, :], ones_row], axis=0).astype(BF16)
    for u in tiles:
        gT_ref[u] = gts[u]
    ret_ref[...] = _dot(xn, w_ref[:, _C_RET:_C_END])
    kc_ref[...] = _dot(xn, w_ref[:, _C_KC:_C_VC])
    vc_ref[...] = _dot(xn, w_ref[:, _C_VC:_C_GT])


def _inproj(x2, g, w, qg, kg, B, T, tm=1024):
    n, d = x2.shape
    nq = T // Q_BLOCK
    G = NSA_KV_HEADS
    tiles = tm // Q_BLOCK
    spr = T // tm
    assert T % tm == 0
    const = lambda i: (0, 0)
    row = lambda i: (i, 0)
    return pl.pallas_call(
        functools.partial(_inproj_kernel, steps_per_row=spr),
        grid=(n // tm,),
        in_specs=[pl.BlockSpec((tm, d), row),
                  pl.BlockSpec((1, d), const),
                  pl.BlockSpec((d, _C_END), const),
                  pl.BlockSpec((1, NSA_WIDTH), const),
                  pl.BlockSpec((2, KV_WIDTH), const),
                  pl.BlockSpec((NSA_WIDTH, NSA_WIDTH), const)],
        out_specs=[pl.BlockSpec((tm, KV_WIDTH), row),
                   pl.BlockSpec((tm, KV_WIDTH), row),
                   pl.BlockSpec((tm, _RET_COLS), row),
                   pl.BlockSpec((None, G, tiles, HEAD_DIM, GQ), lambda i: (i // spr, 0, i % spr, 0, 0)),
                   pl.BlockSpec((None, G, tm, 2 * HEAD_DIM), lambda i: (i // spr, 0, i % spr, 0)),
                   pl.BlockSpec((None, G, tm, 2 * HEAD_DIM), lambda i: (i // spr, 0, i % spr, 0)),
                   pl.BlockSpec((None, G, V_ROWS, tm), lambda i: (i // spr, 0, 0, i % spr)),
                   pl.BlockSpec((None, G, V_ROWS, tm), lambda i: (i // spr, 0, 0, i % spr)),
                   pl.BlockSpec((None, tiles, 32, Q_BLOCK), lambda i: (i // spr, i % spr, 0, 0))],
        out_shape=[jax.ShapeDtypeStruct((n, KV_WIDTH), F32),
                   jax.ShapeDtypeStruct((n, KV_WIDTH), F32),
                   jax.ShapeDtypeStruct((n, _RET_COLS), F32),
                   jax.ShapeDtypeStruct((B, G, nq, HEAD_DIM, GQ), BF16),
                   jax.ShapeDtypeStruct((B, G, T, 2 * HEAD_DIM), BF16),
                   jax.ShapeDtypeStruct((B, G, T, 2 * HEAD_DIM), BF16),
                   jax.ShapeDtypeStruct((B, G, V_ROWS, T), BF16),
                   jax.ShapeDtypeStruct((B, G, V_ROWS, T), BF16),
                   jax.ShapeDtypeStruct((B, nq, 32, Q_BLOCK), F32)],
        compiler_params=_params(1),
        name="inproj",
    )(x2, g, w, jnp.tile(qg, (1, NSA_HEADS)), jnp.tile(kg, (1, NSA_KV_HEADS)), _group_ones())


def _compress_kernel(kc_ref, vc_ref, wk_ref, wv_ref, pk_ref, pv_ref, kg_ref, kcmp_ref, vcT_ref):
    ncp = kc_ref.shape[0] // CMP_STRIDE

    def comp(a_ref, w_ref, p_ref):
        lo = jnp.zeros((ncp, KV_WIDTH), F32)
        hi = jnp.zeros((ncp, KV_WIDTH), F32)
        for l in range(CMP_STRIDE):
            a = a_ref[pl.ds(l, ncp, stride=CMP_STRIDE), :]
            lo += _dot((a + p_ref[0, l:l + 1, :]).astype(BF16), w_ref[0, l])
            hi += _dot((a + p_ref[1, l:l + 1, :]).astype(BF16), w_ref[1, l])
        return lo + pltpu.roll(hi, ncp - 1, 0)

    k = comp(kc_ref, wk_ref, pk_ref)
    v = comp(vc_ref, wv_ref, pv_ref).T
    cend = lax.broadcasted_iota(jnp.int32, (ncp, HEAD_DIM), 0) * CMP_STRIDE + (CMP_LEN - 1)
    col = lax.broadcasted_iota(jnp.int32, (ncp, HEAD_DIM), 1)
    kpos = jnp.where(col < 3, cend // SLC_LEN, jnp.where(col < 6, cend % SLC_LEN, 0)).astype(F32)
    for g in range(NSA_KV_HEADS):
        sl = slice(g * HEAD_DIM, (g + 1) * HEAD_DIM)
        kcmp_ref[g] = jnp.concatenate([_rms(k[:, sl], kg_ref[...]), kpos], axis=1).astype(BF16)
        vcT_ref[g] = v[sl, :].astype(BF16)


def _compress(kc, vc, wk, wv, pk, pv, kg, B, T):
    ncp = T // CMP_STRIDE
    G = NSA_KV_HEADS
    const4 = lambda b: (0, 0, 0, 0)
    const3 = lambda b: (0, 0, 0)
    const2 = lambda b: (0, 0)
    return pl.pallas_call(
        _compress_kernel,
        grid=(B,),
        in_specs=[pl.BlockSpec((T, KV_WIDTH), lambda b: (b, 0)),
                  pl.BlockSpec((T, KV_WIDTH), lambda b: (b, 0)),
                  pl.BlockSpec((2, CMP_STRIDE, KV_WIDTH, KV_WIDTH), const4),
                  pl.BlockSpec((2, CMP_STRIDE, KV_WIDTH, KV_WIDTH), const4),
                  pl.BlockSpec((2, CMP_STRIDE, KV_WIDTH), const3),
                  pl.BlockSpec((2, CMP_STRIDE, KV_WIDTH), const3),
                  pl.BlockSpec((1, HEAD_DIM), const2)],
        out_specs=[pl.BlockSpec((None, G, ncp, 2 * HEAD_DIM), lambda b: (b, 0, 0, 0)),
                   pl.BlockSpec((None, G, HEAD_DIM, ncp), lambda b: (b, 0, 0, 0))],
        out_shape=[jax.ShapeDtypeStruct((B, G, ncp, 2 * HEAD_DIM), BF16),
                   jax.ShapeDtypeStruct((B, G, HEAD_DIM, ncp), BF16)],
        compiler_params=_params(1),
        name="nsa_compress",
    )(kc, vc, wk, wv, pk, pv, kg)


def _compress_weights(w, pos):
    G = NSA_KV_HEADS
    w4 = w.reshape(2, CMP_STRIDE, HEAD_DIM, HEAD_DIM)
    eye = jnp.eye(G, dtype=w.dtype)
    wbd = jnp.einsum('hlde,gk->hlgdke', w4, eye).reshape(2, CMP_STRIDE, KV_WIDTH, KV_WIDTH)
    p = pos.reshape(2, CMP_STRIDE, 1, HEAD_DIM)
    p = jnp.broadcast_to(p, (2, CMP_STRIDE, G, HEAD_DIM)).reshape(2, CMP_STRIDE, KV_WIDTH)
    return wbd.astype(BF16), p


def _split3(x):
    hi = x.astype(BF16)
    r = x - hi.astype(F32)
    mid = r.astype(BF16)
    lo = (r - mid.astype(F32)).astype(BF16)
    return hi, mid, lo


def _nsa_cmp_kernel(qT_ref, qaug_ref, kc_ref, vcT_ref, ovT_ref, ocmp_ref, sel_ref, flag_ref, *, n_sel):
    ncp = kc_ref.shape[0]
    ns = ovT_ref.shape[0]
    tiles = range(CMP_TILES)
    i0 = pl.program_id(2) * CMP_TILES
    lane = lax.broadcasted_iota(jnp.int32, (1, GQ), 1)
    q = [jnp.concatenate([qT_ref[u], qaug_ref[...]], axis=0) for u in tiles]
    t_row = [(i0 + u) * Q_BLOCK + (lane & (Q_BLOCK - 1)) for u in tiles]
    has_cmp = [(t_row[u] >= CMP_LEN - 1).astype(F32) for u in tiles]
    tq = [(i0 + u) * Q_BLOCK + lax.broadcasted_iota(jnp.int32, (1, Q_BLOCK), 1) for u in tiles]
    cur = [tq[u] // SLC_LEN for u in tiles]

    def prefix(rows):
        nsk = rows * CMP_STRIDE // SLC_LEN
        tail0 = max(rows - CMP_TAIL, 0)
        kc = kc_ref[0:rows, :]
        s = [_dot(kc, q[u]) for u in tiles]
        cend = (lax.broadcasted_iota(jnp.int32, (rows - tail0, 1), 0) + tail0) * CMP_STRIDE + (CMP_LEN - 1)
        tail = [jnp.where(t_row[u] >= cend, s[u][tail0:], NEG) for u in tiles]
        s = [jnp.concatenate([s[u][0:tail0], tail[u]], axis=0) if tail0 else tail[u] for u in tiles]
        m = [jnp.max(s[u], axis=0, keepdims=True) for u in tiles]
        e = [jnp.exp2(s[u] - m[u]) for u in tiles]
        p = [e[u] * (has_cmp[u] / jnp.sum(e[u], axis=0, keepdims=True)) for u in tiles]
        vc = vcT_ref[:, 0:rows]
        for u in tiles:
            ocmp_ref[u] = _dot(vc, p[u].astype(BF16))

        ps = [p[u][:, 0:Q_BLOCK] for u in tiles]
        for r in range(1, NSA_GROUP):
            ps = [ps[u] + p[u][:, r * Q_BLOCK:(r + 1) * Q_BLOCK] for u in tiles]
        ov = ovT_ref[0:nsk, 0:rows]
        split = [_split3(ps[u]) for u in tiles]
        imp = [_dot(ov, split[u][0]) + _dot(ov, split[u][1]) + _dot(ov, split[u][2]) for u in tiles]

        blk = lax.broadcasted_iota(jnp.int32, (nsk, 1), 0)
        forced = [(blk == 0) | (blk == cur[u]) | (blk == cur[u] - 1) for u in tiles]
        valid = [blk * SLC_LEN <= tq[u] for u in tiles]
        imp = [jnp.where(forced[u], -3e38, jnp.where(valid[u], imp[u], -BIG)) for u in tiles]
        blk_f = blk.astype(F32)
        sel = [forced[u].astype(F32) for u in tiles]
        for _ in range(n_sel - N_FORCED):
            mx = [jnp.max(imp[u], axis=0, keepdims=True) for u in tiles]
            idx = [jnp.min(jnp.where(imp[u] == mx[u], blk_f, float(ns)), axis=0, keepdims=True) for u in tiles]
            pick = [blk_f == idx[u] for u in tiles]
            sel = [jnp.where(pick[u], 1.0, sel[u]) for u in tiles]
            imp = [jnp.where(pick[u], -3e38, imp[u]) for u in tiles]
        ones = jnp.ones((8, Q_BLOCK), BF16)
        for u in tiles:
            sel_ref[u, 0:nsk, :] = sel[u]
            cnt = _dot_nt(ones, sel[u].astype(BF16))
            flag_ref[u, :, 0:nsk] = (cnt > 0).astype(jnp.int32)
            if nsk < ns:
                sel_ref[u, nsk:, :] = jnp.zeros((ns - nsk, Q_BLOCK), F32)
                flag_ref[u, :, nsk:] = jnp.zeros((8, ns - nsk), jnp.int32)

    n_variants = ncp // CMP_CHUNK
    last = i0 + CMP_TILES - 1
    variant = (last * (Q_BLOCK // CMP_STRIDE) + (Q_BLOCK // CMP_STRIDE - 2)) // CMP_CHUNK
    for k in range(n_variants):
        pl.when(variant == k)(functools.partial(prefix, (k + 1) * CMP_CHUNK))


def _nsa_cmp(qT, qaug, kcmp, vcT, ovT, B, T):
    G = NSA_KV_HEADS
    nq = T // Q_BLOCK
    ncp = T // CMP_STRIDE
    ns = T // SLC_LEN
    n_sel = min(SLC_TOPK, ns)
    assert ncp % CMP_CHUNK == 0 and n_sel > N_FORCED and nq % CMP_TILES == 0
    tile = lambda b, g, i: (b, g, i, 0, 0)
    return pl.pallas_call(
        functools.partial(_nsa_cmp_kernel, n_sel=n_sel),
        grid=(B, G, nq // CMP_TILES),
        in_specs=[pl.BlockSpec((None, None, CMP_TILES, HEAD_DIM, GQ), tile),
                  pl.BlockSpec((None, HEAD_DIM, GQ), lambda b, g, i: (g, 0, 0)),
                  pl.BlockSpec((None, None, ncp, 2 * HEAD_DIM), lambda b, g, i: (b, g, 0, 0)),
                  pl.BlockSpec((None, None, HEAD_DIM, ncp), lambda b, g, i: (b, g, 0, 0)),
                  pl.BlockSpec((ns, ncp), lambda b, g, i: (0, 0))],
        out_specs=[pl.BlockSpec((None, None, CMP_TILES, HEAD_DIM, GQ), tile),
                   pl.BlockSpec((None, None, CMP_TILES, ns, Q_BLOCK), tile),
                   pl.BlockSpec((None, None, CMP_TILES, 8, ns), tile)],
        out_shape=[jax.ShapeDtypeStruct((B, G, nq, HEAD_DIM, GQ), F32),
                   jax.ShapeDtypeStruct((B, G, nq, ns, Q_BLOCK), F32),
                   jax.ShapeDtypeStruct((B, G, nq, 8, ns), jnp.int32)],
        compiler_params=_params(3),
        name="nsa_cmp",
    )(qT, qaug, kcmp, vcT, ovT)


def _nsa_main_kernel(list_ref, cnt_ref, qT_ref, qaug_ref, ks_ref, vsT_ref, kw_ref, vwT_ref, sel_ref, gT_ref, ocmp_ref,
                     lowb_ref, causb_ref, out_ref, m_sc, acc_sc, win_sc):
    b, g = pl.program_id(0), pl.program_id(1)
    tiles = range(MAIN_TILES)
    i = [pl.program_id(2) * MAIN_TILES + u for u in tiles]
    tile_id = [(b * pl.num_programs(1) + g) * (pl.num_programs(2) * MAIN_TILES) + i[u] for u in tiles]
    n_steps = sel_ref.shape[1] // 2
    q = [jnp.concatenate([qT_ref[u], qaug_ref[...]], axis=0) for u in tiles]
    k0 = [pl.multiple_of(i[u] * Q_BLOCK, Q_BLOCK) for u in tiles]

    def sel_bias(u, j, valid):
        def row(r):
            picked = (sel_ref[u, pl.ds(r, 1), :] > 0.5) & valid
            return jnp.concatenate([jnp.where(picked, 0.0, NEG)] * NSA_GROUP, axis=1)
        return row(2 * j), row(2 * j + 1)

    def add_sel_bias(s, ba, bb):
        return jnp.concatenate([s[0:SLC_LEN] + ba, s[SLC_LEN:] + bb], axis=0)

    lowb = jnp.concatenate([lowb_ref[...]] * NSA_GROUP, axis=1)
    causb = jnp.concatenate([causb_ref[...]] * NSA_GROUP, axis=1)

    bias_d = [sel_bias(u, i[u], True) for u in tiles]
    sd = [_dot(ks_ref[pl.ds(k0[u], KEY_STEP), :], q[u]) for u in tiles]
    sw = [_dot(kw_ref[pl.ds(k0[u], WIN_KEYS), :], q[u]) for u in tiles]
    sd = [add_sel_bias(sd[u], *bias_d[u]) + causb for u in tiles]
    sw = [jnp.concatenate([sw[u][0:Q_BLOCK] + lowb, sw[u][Q_BLOCK:WIN], sw[u][WIN:] + causb], axis=0) for u in tiles]
    md = [jnp.max(sd[u], axis=0, keepdims=True) for u in tiles]
    mw = [jnp.max(sw[u], axis=0, keepdims=True) for u in tiles]
    accd = [_dot(vsT_ref[:, pl.ds(k0[u], KEY_STEP)], jnp.exp2((sd[u] - md[u]).astype(BF16))) for u in tiles]
    ow = [_dot(vwT_ref[:, pl.ds(k0[u], WIN_KEYS)], jnp.exp2((sw[u] - mw[u]).astype(BF16))) for u in tiles]
    for u in tiles:
        m_sc[u] = md[u]
        acc_sc[u] = accd[u]
        win_sc[u] = ow[u][0:HEAD_DIM] / ow[u][HEAD_DIM:HEAD_DIM + 1]

    def scores(u, t):
        ks, vs, biases = [], [], []
        for x in range(STEP_GROUP):
            j = list_ref[tile_id[u] * n_steps + t * STEP_GROUP + x]
            valid = j >= 0
            j = jnp.maximum(j, 0)
            kj = pl.multiple_of(j * KEY_STEP, KEY_STEP)
            ks.append(ks_ref[pl.ds(kj, KEY_STEP), :])
            vs.append(vsT_ref[:, pl.ds(kj, KEY_STEP)])
            biases.append(sel_bias(u, j, valid))
        s = _dot(jnp.concatenate(ks, axis=0), q[u])
        s = jnp.concatenate([add_sel_bias(s[x * KEY_STEP:(x + 1) * KEY_STEP], *biases[x])
                             for x in range(STEP_GROUP)], axis=0)
        return s, jnp.max(s, axis=0, keepdims=True), jnp.concatenate(vs, axis=1)

    def accumulate(u, s, smax, vcat):
        m_old = m_sc[u]
        m_new = jnp.maximum(m_old, smax)
        alpha = jnp.exp2(m_old - m_new)
        acc_sc[u] = alpha * acc_sc[u] + _dot(vcat, jnp.exp2((s - m_new).astype(BF16)))
        m_sc[u] = m_new

    def run(work, t, carry):
        staged = [(u, scores(u, t * mult + off)) for (u, mult, off) in work]
        for u, args in staged:
            accumulate(u, *args)
        return carry

    for u0 in range(0, MAIN_TILES, LOOP_TILES):
        us = range(u0, u0 + LOOP_TILES)
        n_groups = functools.reduce(
            jnp.maximum, [(cnt_ref[tile_id[u]] + (STEP_GROUP - 1)) // STEP_GROUP for u in us])
        lax.fori_loop(0, n_groups // 2, functools.partial(run, [(u, 2, off) for off in (0, 1) for u in us]), 0)
        lax.fori_loop(n_groups // 2 * 2, n_groups, functools.partial(run, [(u, 1, 0) for u in us]), 0)

    def gate(u, k):
        rows = [gT_ref[u, pl.ds(g * (NSA_GROUP * 3) + r * 3 + k, 1), :] for r in range(NSA_GROUP)]
        return jnp.concatenate(rows, axis=1)

    o_slc = [acc_sc[u, 0:HEAD_DIM, :] / acc_sc[u, HEAD_DIM:HEAD_DIM + 1, :] for u in tiles]
    o = [gate(u, 0) * ocmp_ref[u] + gate(u, 1) * o_slc[u] + gate(u, 2) * win_sc[u] for u in tiles]
    o = [jnp.concatenate([o[u], jnp.zeros_like(o[u])], axis=0) for u in tiles]
    for r in range(NSA_GROUP):
        ot = [o[u][:, r * Q_BLOCK:(r + 1) * Q_BLOCK].T[:, 0:HEAD_DIM] for u in tiles]
        for u in tiles:
            out_ref[u * Q_BLOCK:(u + 1) * Q_BLOCK, r * HEAD_DIM:(r + 1) * HEAD_DIM] = ot[u]


def _nsa_steps_kernel(flagT_ref, pairT_ref, list_ref, cnt_ref, *, nq):
    n_steps, nt = list_ref.shape
    need = _dot(pairT_ref[...], flagT_ref[...].astype(BF16)) > 0
    step = lax.broadcasted_iota(jnp.int32, (n_steps, 1), 0)
    own = lax.broadcasted_iota(jnp.int32, (1, nt), 1) % nq
    need = need & (step < own)
    need_f = need.astype(F32)
    earlier = (lax.broadcasted_iota(jnp.int32, (n_steps, n_steps), 1) < step).astype(BF16)
    slot = _dot(earlier, need_f.astype(BF16))
    total = jnp.sum(need_f, axis=0, keepdims=True)
    cnt_ref[...] = jnp.broadcast_to(total, cnt_ref.shape).astype(jnp.int32)
    step_f = step.astype(F32)
    for p in range(n_steps):
        val = jnp.sum(jnp.where(need & (slot == p), step_f, 0.0), axis=0, keepdims=True)
        list_ref[p:p + 1, :] = jnp.where(total > p, val, -1.0).astype(jnp.int32)


def _nsa_steps(flags, nq):
    nt, ns = flags.shape
    n_steps = ns // 2
    pairT = jnp.asarray(np.arange(n_steps)[:, None] == np.arange(ns)[None, :] // 2, BF16)
    lists, counts = pl.pallas_call(
        functools.partial(_nsa_steps_kernel, nq=nq),
        out_shape=[jax.ShapeDtypeStruct((n_steps, nt), jnp.int32), jax.ShapeDtypeStruct((8, nt), jnp.int32)],
        name="nsa_steps",
    )(flags.T.astype(F32), pairT)
    return lists.T.reshape(-1), counts[0]


def _nsa_main(lists, counts, qT, qaug, ks, vsT, kw, vwT, sel, gT, ocmp, lowb, causb, B, T):
    G = NSA_KV_HEADS
    nq = T // Q_BLOCK
    ns = T // SLC_LEN
    whole = lambda b, g, i, *_: (b, g, 0, 0)
    tile = lambda b, g, i, *_: (b, g, i, 0, 0)
    const = lambda b, g, i, *_: (0, 0)
    grid_spec = pltpu.PrefetchScalarGridSpec(
        num_scalar_prefetch=2,
        grid=(B, G, nq // MAIN_TILES),
        in_specs=[pl.BlockSpec((None, None, MAIN_TILES, HEAD_DIM, GQ), tile),
                  pl.BlockSpec((None, HEAD_DIM, GQ), lambda b, g, i, *_: (g, 0, 0)),
                  pl.BlockSpec((None, None, T, 2 * HEAD_DIM), whole),
                  pl.BlockSpec((None, None, V_ROWS, T), whole),
                  pl.BlockSpec((None, None, T + WIN, 2 * HEAD_DIM), whole),
                  pl.BlockSpec((None, None, V_ROWS, T + WIN), whole),
                  pl.BlockSpec((None, None, MAIN_TILES, ns, Q_BLOCK), tile),
                  pl.BlockSpec((None, MAIN_TILES, 32, Q_BLOCK), lambda b, g, i, *_: (b, i, 0, 0)),
                  pl.BlockSpec((None, None, MAIN_TILES, HEAD_DIM, GQ), tile),
                  pl.BlockSpec((Q_BLOCK, Q_BLOCK), const),
                  pl.BlockSpec((Q_BLOCK, Q_BLOCK), const)],
        out_specs=pl.BlockSpec((None, MAIN_TILES * Q_BLOCK, NSA_GROUP * HEAD_DIM), lambda b, g, i, *_: (b, i, g)),
        scratch_shapes=[pltpu.VMEM((MAIN_TILES, 1, GQ), F32), pltpu.VMEM((MAIN_TILES, V_ROWS, GQ), F32),
                        pltpu.VMEM((MAIN_TILES, HEAD_DIM, GQ), F32)],
    )
    return pl.pallas_call(
        _nsa_main_kernel,
        grid_spec=grid_spec,
        out_shape=jax.ShapeDtypeStruct((B, T, NSA_WIDTH), F32),
        compiler_params=_params(3),
        name="nsa_main",
    )(lists, counts, qT, qaug, ks, vsT, kw, vwT, sel, gT, ocmp, lowb, causb)


def _ret_kernel(p_ref, decay_ref, xi_ref, zeta_ref, gch_ref, ng_ref, ones_ref, out_ref, state_ref):
    @pl.when(pl.program_id(0) == 0)
    def _():
        state_ref[...] = jnp.zeros(state_ref.shape, F32)

    rows = range(p_ref.shape[0])
    kw = RET_HEADS * RET_DK
    p = [p_ref[b] for b in rows]
    rq = [p[b][:, 0:kw] * (RET_DK ** -0.5) for b in rows]
    rk = [p[b][:, kw:2 * kw] for b in rows]
    rkT = [rk[b].T for b in rows]
    rv = [p[b][:, 2 * kw:2 * kw + RET_WIDTH] for b in rows]
    xi = xi_ref[...]
    outs = [[] for _ in rows]
    for h in range(RET_HEADS):
        dk = slice(h * RET_DK, (h + 1) * RET_DK)
        dv = slice(h * RET_DV, (h + 1) * RET_DV)
        st = [state_ref[b, h] for b in rows]
        inner = [_dot_nt(rq[b][:, dk], rk[b][:, dk]) * decay_ref[h] for b in rows]
        o = [_dot(inner[b], rv[b][:, dv]) + _dot(rq[b][:, dk], st[b]) * xi[:, h:h + 1] for b in rows]
        for b in rows:
            state_ref[b, h] = (st[b] * gch_ref[h:h + 1, 0:1]
                               + _dot(rkT[b][dk, :] * zeta_ref[h:h + 1, :], rv[b][:, dv]))
            outs[b].append(o[b])
    normed = [_group_rms(jnp.concatenate(outs[b], axis=1), ng_ref[...], ones_ref) for b in rows]
    for b in rows:
        rg = p[b][:, 2 * kw + RET_WIDTH:2 * kw + 2 * RET_WIDTH]
        out_ref[b] = normed[b] * (rg * jax.nn.sigmoid(rg))


def _ret_consts():
    H, C = RET_HEADS, RET_CHUNK
    log_g = jnp.log1p(-jnp.exp2(-5.0 - jnp.arange(H, dtype=F32)))
    idx = jnp.arange(C, dtype=F32)
    diff = idx[:, None] - idx[None, :]
    decay = jnp.where(diff >= 0, jnp.exp(jnp.maximum(diff, 0.0) * log_g[:, None, None]), 0.0)
    zeta = jnp.exp((C - 1 - idx) * log_g[:, None])
    xi = jnp.exp((idx + 1) * log_g[:, None]).T
    g_chunk = jnp.broadcast_to(jnp.exp(C * log_g)[:, None], (H, LANES))
    return decay, xi, zeta, g_chunk


def _retention(pret, ng, B, T):
    nch = T // RET_CHUNK
    decay, xi, zeta, gch = _ret_consts()
    c2 = lambda c: (0, 0)
    out = pl.pallas_call(
        _ret_kernel,
        grid=(nch,),
        in_specs=[pl.BlockSpec((B, RET_CHUNK, _RET_COLS), lambda c: (0, c, 0)),
                  pl.BlockSpec((RET_HEADS, RET_CHUNK, RET_CHUNK), lambda c: (0, 0, 0)),
                  pl.BlockSpec((RET_CHUNK, RET_HEADS), c2),
                  pl.BlockSpec((RET_HEADS, RET_CHUNK), c2),
                  pl.BlockSpec((RET_HEADS, LANES), c2),
                  pl.BlockSpec((1, RET_WIDTH), c2),
                  pl.BlockSpec((RET_WIDTH, RET_WIDTH), c2)],
        out_specs=pl.BlockSpec((B, RET_CHUNK, RET_WIDTH), lambda c: (0, c, 0)),
        out_shape=jax.ShapeDtypeStruct((B, T, RET_WIDTH), F32),
        scratch_shapes=[pltpu.VMEM((B, RET_HEADS, RET_DK, RET_DV), F32)],
        compiler_params=_params(1),
        name="retention",
    )(pret.reshape(B, T, _RET_COLS), decay, xi, zeta, gch, ng, _group_ones())
    return out.reshape(B * T, RET_WIDTH)


def _outproj_kernel(x_ref, nsa_ref, ret_ref, w_ref, o_ref):
    o_ref[...] = (x_ref[...] + _dot(nsa_ref[...].astype(BF16), w_ref[0:NSA_WIDTH, :])
                  + _dot(ret_ref[...].astype(BF16), w_ref[NSA_WIDTH:, :]))


def _outproj(x2, nsa, ret, w, tm=512):
    n, d = x2.shape
    return pl.pallas_call(
        _outproj_kernel,
        grid=(n // tm,),
        in_specs=[pl.BlockSpec((tm, d), lambda i: (i, 0)),
                  pl.BlockSpec((tm, NSA_WIDTH), lambda i: (i, 0)),
                  pl.BlockSpec((tm, RET_WIDTH), lambda i: (i, 0)),
                  pl.BlockSpec((NSA_WIDTH + RET_WIDTH, d), lambda i: (0, 0))],
        out_specs=pl.BlockSpec((tm, d), lambda i: (i, 0)),
        out_shape=jax.ShapeDtypeStruct((n, d), F32),
        compiler_params=_params(1),
        name="outproj",
    )(x2, nsa, ret, w)


def _ffn_kernel(x_ref, g_ref, wg_ref, wu_ref, wd_ref, o_ref, h_sc):
    f = pl.program_id(1)

    @pl.when(f == 0)
    def _():
        x = x_ref[...]
        h_sc[...] = _rms(x, g_ref[...]).astype(BF16)
        o_ref[...] = x

    h = h_sc[...]
    a = _dot(h, wg_ref[...])
    act = (a * jax.nn.sigmoid(a) * _dot(h, wu_ref[...])).astype(BF16)
    o_ref[...] += _dot(act, wd_ref[...])


def _ffn(x2, g, wg, wu, wd, tm=1024, fc=1408):
    n, d = x2.shape
    dff = wg.shape[1]
    return pl.pallas_call(
        _ffn_kernel,
        grid=(n // tm, dff // fc),
        in_specs=[pl.BlockSpec((tm, d), lambda i, f: (i, 0)),
                  pl.BlockSpec((1, d), lambda i, f: (0, 0)),
                  pl.BlockSpec((d, fc), lambda i, f: (0, f)),
                  pl.BlockSpec((d, fc), lambda i, f: (0, f)),
                  pl.BlockSpec((fc, d), lambda i, f: (f, 0))],
        out_specs=pl.BlockSpec((tm, d), lambda i, f: (i, 0)),
        out_shape=jax.ShapeDtypeStruct((n, d), F32),
        scratch_shapes=[pltpu.VMEM((tm, d), BF16)],
        compiler_params=_params(2),
        name="ffn_dense",
    )(x2, g, wg, wu, wd)


def _router_kernel(x_ref, g_ref, r_ref, rb_ref, tri_ref, h_ref, rank_ref, comb_ref, rankT_ref, cnt_ref):
    h = _rms(x_ref[...], g_ref[...])
    h_ref[...] = h.astype(BF16)
    hh, hm, hl = _split3(h)
    rh, rm, rl = _split3(r_ref[...])
    logits = (_dot(hh, rh) + (_dot(hh, rm) + _dot(hm, rh)) + (_dot(hh, rl) + _dot(hm, rm) + _dot(hl, rh))
              + rb_ref[...])
    lane = lax.broadcasted_iota(jnp.int32, logits.shape, 1).astype(F32)
    logits = jnp.where(lane < N_EXPERTS, logits, NEG)
    m1 = jnp.max(logits, axis=1, keepdims=True)
    i1 = jnp.min(jnp.where(logits == m1, lane, float(LANES)), axis=1, keepdims=True)
    l2 = jnp.where(lane == i1, NEG, logits)
    m2 = jnp.max(l2, axis=1, keepdims=True)
    i2 = jnp.min(jnp.where(l2 == m2, lane, float(LANES)), axis=1, keepdims=True)
    e2 = jnp.exp(m2 - m1)
    w1 = 1.0 / (1.0 + e2)
    w2 = e2 / (1.0 + e2)
    use1, use2 = lane == i1, lane == i2
    comb_ref[...] = jnp.where(use1, w1, 0.0) + jnp.where(use2, w2, 0.0)
    use = (use1 | use2).astype(F32)
    rank = jnp.where(use > 0, _dot(tri_ref[...], use.astype(BF16)), -1.0)
    rank_ref[...] = rank
    rankT_ref[...] = rank.T[0:N_EXPERTS, :]
    cnt_ref[...] = jnp.broadcast_to(jnp.sum(use, axis=0, keepdims=True), cnt_ref.shape).astype(jnp.int32)


def _router(x2, g, router, rb, tm):
    n, d = x2.shape
    nt = n // tm
    tri = (jnp.arange(tm)[:, None] > jnp.arange(tm)[None, :]).astype(BF16)
    rpad = jnp.zeros((d, LANES), F32).at[:, :N_EXPERTS].set(router)
    rbpad = jnp.zeros((1, LANES), F32).at[0, :N_EXPERTS].set(rb)
    c2 = lambda i: (0, 0)
    return pl.pallas_call(
        _router_kernel,
        grid=(nt,),
        in_specs=[pl.BlockSpec((tm, d), lambda i: (i, 0)),
                  pl.BlockSpec((1, d), c2),
                  pl.BlockSpec((d, LANES), c2),
                  pl.BlockSpec((1, LANES), c2),
                  pl.BlockSpec((tm, tm), c2)],
        out_specs=[pl.BlockSpec((tm, d), lambda i: (i, 0)),
                   pl.BlockSpec((tm, LANES), lambda i: (i, 0)),
                   pl.BlockSpec((tm, LANES), lambda i: (i, 0)),
                   pl.BlockSpec((N_EXPERTS, tm), lambda i: (0, i)),
                   pl.BlockSpec((None, 8, LANES), lambda i: (i, 0, 0))],
        out_shape=[jax.ShapeDtypeStruct((n, d), BF16),
                   jax.ShapeDtypeStruct((n, LANES), F32),
                   jax.ShapeDtypeStruct((n, LANES), F32),
                   jax.ShapeDtypeStruct((N_EXPERTS, n), F32),
                   jax.ShapeDtypeStruct((nt, 8, LANES), jnp.int32)],
        compiler_params=_params(1),
        name="moe_router",
    )(x2, g, rpad, rbpad, tri)


MOE_SUB = 144
MOE_MOVE = 2 * MOE_SUB


def _moe_kernel(cnt_ref, h_ref, rankT_ref, rank_ref, comb_ref, wg_ref, wu_ref, wd_ref, x_ref, o_ref, hc_sc, oacc_sc):
    t, e, f = pl.program_id(0), pl.program_id(1), pl.program_id(2)
    nf = pl.num_programs(2)
    tm = h_ref.shape[0]
    nsub = (cnt_ref[t * N_EXPERTS + e] + (MOE_SUB - 1)) // MOE_SUB
    nmove = (nsub + 1) // 2

    @pl.when((e == 0) & (f == 0))
    def _():
        o_ref[...] = x_ref[...]

    @pl.when(f == 0)
    def _():
        rank_row = rankT_ref[...]

        def gather(s, c):
            r0 = pl.multiple_of(s * MOE_MOVE, MOE_MOVE)
            rows = (lax.broadcasted_iota(jnp.int32, (MOE_MOVE, 1), 0) + r0).astype(F32)
            onehot = (rows == rank_row).astype(BF16)
            hc_sc[pl.ds(r0, MOE_MOVE), :] = _dot(onehot, h_ref[...]).astype(BF16)
            oacc_sc[pl.ds(r0, MOE_MOVE), :] = jnp.zeros((MOE_MOVE, oacc_sc.shape[1]), F32)
            return c

        lax.fori_loop(0, nmove, gather, 0)

    def expert(n_rows, s, c):
        r0 = pl.multiple_of(s * n_rows, n_rows)
        rows = hc_sc[pl.ds(r0, n_rows), :]
        a = _dot(rows, wg_ref[...])
        act = (a * jax.nn.sigmoid(a) * _dot(rows, wu_ref[...])).astype(BF16)
        oacc_sc[pl.ds(r0, n_rows), :] += _dot(act, wd_ref[...])
        return c

    lax.fori_loop(0, nsub // 2, functools.partial(expert, MOE_MOVE), 0)
    lax.fori_loop(nsub // 2 * 2, nsub, functools.partial(expert, MOE_SUB), 0)

    @pl.when(f == nf - 1)
    def _():
        is_e = lax.broadcasted_iota(jnp.int32, (1, LANES), 1) == e
        rank_col = jnp.sum(jnp.where(is_e, rank_ref[...], 0.0), axis=1, keepdims=True)
        comb_col = jnp.sum(jnp.where(is_e, comb_ref[...], 0.0), axis=1, keepdims=True)

        def scatter(s, c):
            r0 = pl.multiple_of(s * MOE_MOVE, MOE_MOVE)
            cols = (lax.broadcasted_iota(jnp.int32, (1, MOE_MOVE), 1) + r0).astype(F32)
            onehot = (rank_col == cols).astype(BF16)
            y = _dot(onehot, oacc_sc[pl.ds(r0, MOE_MOVE), :].astype(BF16))
            o_ref[...] += comb_col * y
            return c

        lax.fori_loop(0, nmove, scatter, 0)


def _moe(counts, h, rankT, rank, comb, wg, wu, wd, x2, tm, fc=1408):
    n, d = x2.shape
    dff = wg.shape[2]
    rows_cap = pl.cdiv(pl.cdiv(tm, MOE_SUB), 2) * MOE_MOVE
    grid_spec = pltpu.PrefetchScalarGridSpec(
        num_scalar_prefetch=1,
        grid=(n // tm, N_EXPERTS, dff // fc),
        in_specs=[pl.BlockSpec((tm, d), lambda t, e, f, c: (t, 0)),
                  pl.BlockSpec((None, 1, tm), lambda t, e, f, c: (e, 0, t)),
                  pl.BlockSpec((tm, LANES), lambda t, e, f, c: (t, 0)),
                  pl.BlockSpec((tm, LANES), lambda t, e, f, c: (t, 0)),
                  pl.BlockSpec((None, d, fc), lambda t, e, f, c: (e, 0, f)),
                  pl.BlockSpec((None, d, fc), lambda t, e, f, c: (e, 0, f)),
                  pl.BlockSpec((None, fc, d), lambda t, e, f, c: (e, f, 0)),
                  pl.BlockSpec((tm, d), lambda t, e, f, c: (t, 0))],
        out_specs=pl.BlockSpec((tm, d), lambda t, e, f, c: (t, 0)),
        scratch_shapes=[pltpu.VMEM((rows_cap, d), BF16), pltpu.VMEM((rows_cap, d), F32)],
    )
    return pl.pallas_call(
        _moe_kernel,
        grid_spec=grid_spec,
        out_shape=jax.ShapeDtypeStruct((n, d), F32),
        compiler_params=_params(3),
        name="moe_experts",
    )(counts, h, rankT.reshape(N_EXPERTS, 1, n), rank, comb, wg, wu, wd, x2)


def _permute_w_in(w):
    o = np.cumsum((0, NSA_WIDTH) + (KV_WIDTH,) * 6 + (3 * NSA_HEADS,))
    q, kc, vc, ks, vs, kw, vw, gts = (w[:, o[k]:o[k + 1]] for k in range(8))
    ret = w[:, o[8]:]
    pad = jnp.zeros((w.shape[0], LANES - 3 * NSA_HEADS), w.dtype)
    return jnp.concatenate([q, ks, kw, vs, vw, kc, vc, gts, pad, ret], axis=1).astype(BF16)


def _nsa_consts(T):
    ncp = T // CMP_STRIDE
    ns = T // SLC_LEN
    cs = np.arange(ncp) * CMP_STRIDE
    ss = np.arange(ns) * SLC_LEN
    ov = np.clip(np.minimum(cs[None, :] + CMP_LEN, ss[:, None] + SLC_LEN) - np.maximum(cs[None, :], ss[:, None]), 0, None)
    ovT = (ov.astype(np.float32) / CMP_LEN)
    ovT[:, ncp - 1] = 0.0
    h = np.arange(NSA_HEADS).reshape(NSA_KV_HEADS, NSA_GROUP) + 1
    slopes = np.exp2(-8.0 * h / NSA_HEADS).astype(np.float32)
    slopes = np.repeat(slopes, Q_BLOCK, axis=1)
    parts, rest = [], np.float64(LOG2E)
    for _ in range(3):
        part = np.float64(np.asarray(rest).astype(BF16))
        parts.append(part)
        rest = rest - part
    qaug = np.zeros((NSA_KV_HEADS, HEAD_DIM, GQ), np.float32)
    for k, part in enumerate(parts):
        qaug[:, k, :] = part * SLC_LEN * slopes
        qaug[:, 3 + k, :] = part * slopes
    kq = np.arange(Q_BLOCK)[:, None] - np.arange(Q_BLOCK)[None, :]
    causb = np.where(kq <= 0, 0.0, NEG).astype(np.float32)
    lowb = np.where(kq > 0, 0.0, NEG).astype(np.float32)
    return jnp.asarray(ovT, BF16), jnp.asarray(qaug, BF16), jnp.asarray(lowb), jnp.asarray(causb)


def _mixer(x2, B, T, norm_g, w_in, q_norm_g, k_norm_g, cmp_pos, w_cmp, ret_norm_g, w_out):
    ns = T // SLC_LEN
    kc, vc, pret, qT, ks, kw, vsT, vwT, gT = _inproj(x2, norm_g[None, :], _permute_w_in(w_in), q_norm_g[None, :],
                                                     k_norm_g[1:3], B, T)
    wk, pk = _compress_weights(w_cmp[0], cmp_pos[0])
    wv, pv = _compress_weights(w_cmp[1], cmp_pos[1])
    kcmp, vcT = _compress(kc, vc, wk, wv, pk, pv, k_norm_g[0:1], B, T)
    ovT, qaug, lowb, causb = _nsa_consts(T)
    ocmp, sel, flags = _nsa_cmp(qT, qaug, kcmp, vcT, ovT, B, T)
    lists, counts = _nsa_steps(flags[:, :, :, 0, :].reshape(-1, ns), T // Q_BLOCK)
    kpad = jnp.zeros((WIN, 2 * HEAD_DIM), BF16).at[:, HEAD_DIM:HEAD_DIM + 3].set(-2.0 ** 100)
    kw = jnp.concatenate([jnp.broadcast_to(kpad, kw.shape[:2] + kpad.shape), kw], axis=2)
    vwT = jnp.pad(vwT, ((0, 0), (0, 0), (0, 0), (WIN, 0)))
    nsa = _nsa_main(lists, counts, qT, qaug, ks, vsT, kw, vwT, sel, gT, ocmp, lowb, causb, B, T)
    ret = _retention(pret, ret_norm_g[None, :], B, T)
    return _outproj(x2, nsa.reshape(B * T, NSA_WIDTH), ret, w_out.astype(BF16))


def _moe_layer(x2, norm_g, router, router_b, wg, wu, wd, tm=1024):
    tm = min(tm, x2.shape[0])
    h, rank, comb, rankT, cnt = _router(x2, norm_g[None, :], router, router_b, tm)
    counts = cnt[:, 0, :N_EXPERTS].reshape(-1)
    return _moe(counts, h, rankT, rank, comb, wg.astype(BF16), wu.astype(BF16), wd.astype(BF16), x2, tm)


def kernel(x, norm_mix_g, w_in, q_norm_g, k_norm_g, cmp_pos, w_cmp, ret_norm_g, w_out, norm_ffn_g,
           ffn_w_gate, ffn_w_up, ffn_w_down, moe_router, moe_router_b, moe_w_gate, moe_w_up, moe_w_down):
    B, T, D = x.shape
    depth = norm_mix_g.shape[0]
    x2 = x.reshape(B * T, D)
    for l in range(depth):
        x2 = _mixer(x2, B, T, norm_mix_g[l], w_in[l], q_norm_g[l], k_norm_g[l], cmp_pos[l], w_cmp[l],
                    ret_norm_g[l], w_out[l])
        j = l // 2
        if l % 2 == 0:
            x2 = _ffn(x2, norm_ffn_g[l][None, :], ffn_w_gate[j].astype(BF16), ffn_w_up[j].astype(BF16),
                      ffn_w_down[j].astype(BF16))
        else:
            x2 = _moe_layer(x2, norm_ffn_g[l], moe_router[j], moe_router_b[j], moe_w_gate[j], moe_w_up[j],
                            moe_w_down[j])
    return x2.reshape(B, T, D)
```

```python
import functools

import numpy as np
import jax
import jax.numpy as jnp
from jax import lax
from jax.experimental import pallas as pl
from jax.experimental.pallas import tpu as pltpu

F32 = jnp.float32
BF16 = jnp.bfloat16

HEAD_DIM = 64
NSA_HEADS = 8
NSA_KV_HEADS = 2
NSA_GROUP = NSA_HEADS // NSA_KV_HEADS
RET_HEADS = 8
RET_DK = 32
RET_DV = 64
NSA_WIDTH = NSA_HEADS * HEAD_DIM
RET_WIDTH = RET_HEADS * RET_DV
KV_WIDTH = NSA_KV_HEADS * HEAD_DIM
CMP_LEN = 32
CMP_STRIDE = 16
SLC_LEN = 64
SLC_TOPK = 16
WIN = 512
Q_BLOCK = 128
RET_CHUNK = 128
N_EXPERTS = 8
EPS = 1e-6
NEG = -1e30
BIG = 1e9
LANES = 128
GQ = NSA_GROUP * Q_BLOCK
KEY_STEP = 128
STEP_GROUP = 4
N_FORCED = 3
CMP_CHUNK = 128
CMP_TILES = 8
MAIN_TILES = 8
LOOP_TILES = 2
CMP_TAIL = CMP_CHUNK + 8
WIN_KEYS = WIN + Q_BLOCK
V_ROWS = HEAD_DIM + 16
LOG2E = 1.4426950408889634
VMEM_LIMIT = 60 * 1024 * 1024

_C_Q = 0
_C_KV = _C_Q + NSA_WIDTH
_C_KC = _C_KV + 4 * KV_WIDTH
_C_VC = _C_KC + KV_WIDTH
_C_GT = _C_VC + KV_WIDTH
_C_RET = _C_GT + LANES
_RET_COLS = 2 * RET_HEADS * RET_DK + 2 * RET_WIDTH
_C_END = _C_RET + _RET_COLS


def _params(n_axes, vmem=VMEM_LIMIT):
    return pltpu.CompilerParams(dimension_semantics=("arbitrary",) * n_axes, vmem_limit_bytes=vmem)


def _dot(a, b):
    return jnp.dot(a, b, preferred_element_type=F32)


def _dot_nt(a, b):
    return lax.dot_general(a, b, (((1,), (1,)), ((), ())), preferred_element_type=F32)


def _rms(x, g):
    return x * lax.rsqrt(jnp.mean(x * x, axis=-1, keepdims=True) + EPS) * g


def _group_rms(x, g, ones_ref):
    w = x.shape[1]
    ones = ones_ref[0:w, 0:w]
    hi, mid, lo = _split3(x * x)
    ms = (_dot(hi, ones) + _dot(mid, ones) + _dot(lo, ones)) * (1.0 / HEAD_DIM)
    return x * lax.rsqrt(ms + EPS) * g


def _group_ones():
    lane = np.arange(NSA_WIDTH) // HEAD_DIM
    return jnp.asarray(lane[:, None] == lane[None, :], BF16)


def _inproj_kernel(x_ref, g_ref, w_ref, qg_ref, kg_ref, ones_ref, kc_ref, vc_ref, ret_ref,
                   qT_ref, ks_ref, kw_ref, vsT_ref, vwT_ref, gT_ref, *, steps_per_row):
    n_tok = x_ref.shape[0]
    tiles = range(n_tok // Q_BLOCK)
    rows = [slice(u * Q_BLOCK, (u + 1) * Q_BLOCK) for u in tiles]
    xn = _rms(x_ref[...], g_ref[...]).astype(BF16)
    q = _dot(xn, w_ref[:, _C_Q:_C_KV])
    kv = _dot(xn, w_ref[:, _C_KV:_C_KC])
    gt = _dot(xn, w_ref[:, _C_GT:_C_RET])

    scale = HEAD_DIM ** -0.5 * LOG2E
    qn = _group_rms(q, qg_ref[...], ones_ref) * scale
    qt = [qn[rows[u]].T for u in tiles]
    for g in range(NSA_KV_HEADS):
        for r in range(NSA_GROUP):
            h = g * NSA_GROUP + r
            for u in tiles:
                qT_ref[g, u, :, r * Q_BLOCK:(r + 1) * Q_BLOCK] = qt[u][h * HEAD_DIM:(h + 1) * HEAD_DIM, :].astype(BF16)
    vst = [kv[rows[u], 2 * KV_WIDTH:3 * KV_WIDTH].T for u in tiles]
    vwt = [kv[rows[u], 3 * KV_WIDTH:4 * KV_WIDTH].T for u in tiles]
    gts = [jax.nn.sigmoid(gt[rows[u], :].T[0:32, :]) for u in tiles]
    pos0 = (pl.program_id(0) % steps_per_row) * n_tok
    pos = pos0 + lax.broadcasted_iota(jnp.int32, (n_tok, HEAD_DIM), 0)
    col = lax.broadcasted_iota(jnp.int32, (n_tok, HEAD_DIM), 1)
    kpos = jnp.where(col < 3, pos // SLC_LEN, jnp.where(col < 6, pos % SLC_LEN, 0)).astype(F32)
    ones_row = (lax.broadcasted_iota(jnp.int32, (V_ROWS - HEAD_DIM, Q_BLOCK), 0) == 0).astype(F32)
    ks = _group_rms(kv[:, 0:KV_WIDTH], kg_ref[0:1, :], ones_ref)
    kw = _group_rms(kv[:, KV_WIDTH:2 * KV_WIDTH], kg_ref[1:2, :], ones_ref)
    for g in range(NSA_KV_HEADS):
        sl = slice(g * HEAD_DIM, (g + 1) * HEAD_DIM)
        ks_ref[g] = jnp.concatenate([ks[:, sl], kpos], axis=1).astype(BF16)
        kw_ref[g] = jnp.concatenate([kw[:, sl], kpos], axis=1).astype(BF16)
        for u in tiles:
            vsT_ref[g, :, rows[u]] = jnp.concatenate([vst[u][sl, :], ones_row], axis=0).astype(BF16)
            vwT_ref[g, :, rows[u]] = jnp.concatenate([vwt[u][sl, :], ones_row], axis=0).astype(BF16)
    for u in tiles:
        gT_ref[u] = gts[u]
    ret_ref[...] = _dot(xn, w_ref[:, _C_RET:_C_END])
    kc_ref[...] = _dot(xn, w_ref[:, _C_KC:_C_VC])
    vc_ref[...] = _dot(xn, w_ref[:, _C_VC:_C_GT])


def _inproj(x2, g, w, qg, kg, B, T, tm=1024):
    n, d = x2.shape
    nq = T // Q_BLOCK
    G = NSA_KV_HEADS
    tiles = tm // Q_BLOCK
    spr = T // tm
    assert T % tm == 0
    const = lambda i: (0, 0)
    row = lambda i: (i, 0)
    return pl.pallas_call(
        functools.partial(_inproj_kernel, steps_per_row=spr),
        grid=(n // tm,),
        in_specs=[pl.BlockSpec((tm, d), row),
                  pl.BlockSpec((1, d), const),
                  pl.BlockSpec((d, _C_END), const),
                  pl.BlockSpec((1, NSA_WIDTH), const),
                  pl.BlockSpec((2, KV_WIDTH), const),
                  pl.BlockSpec((NSA_WIDTH, NSA_WIDTH), const)],
        out_specs=[pl.BlockSpec((tm, KV_WIDTH), row),
                   pl.BlockSpec((tm, KV_WIDTH), row),
                   pl.BlockSpec((tm, _RET_COLS), row),
                   pl.BlockSpec((None, G, tiles, HEAD_DIM, GQ), lambda i: (i // spr, 0, i % spr, 0, 0)),
                   pl.BlockSpec((None, G, tm, 2 * HEAD_DIM), lambda i: (i // spr, 0, i % spr, 0)),
                   pl.BlockSpec((None, G, tm, 2 * HEAD_DIM), lambda i: (i // spr, 0, i % spr, 0)),
                   pl.BlockSpec((None, G, V_ROWS, tm), lambda i: (i // spr, 0, 0, i % spr)),
                   pl.BlockSpec((None, G, V_ROWS, tm), lambda i: (i // spr, 0, 0, i % spr)),
                   pl.BlockSpec((None, tiles, 32, Q_BLOCK), lambda i: (i // spr, i % spr, 0, 0))],
        out_shape=[jax.ShapeDtypeStruct((n, KV_WIDTH), F32),
                   jax.ShapeDtypeStruct((n, KV_WIDTH), F32),
                   jax.ShapeDtypeStruct((n, _RET_COLS), F32),
                   jax.ShapeDtypeStruct((B, G, nq, HEAD_DIM, GQ), BF16),
                   jax.ShapeDtypeStruct((B, G, T, 2 * HEAD_DIM), BF16),
                   jax.ShapeDtypeStruct((B, G, T, 2 * HEAD_DIM), BF16),
                   jax.ShapeDtypeStruct((B, G, V_ROWS, T), BF16),
                   jax.ShapeDtypeStruct((B, G, V_ROWS, T), BF16),
                   jax.ShapeDtypeStruct((B, nq, 32, Q_BLOCK), F32)],
        compiler_params=_params(1),
        name="inproj",
    )(x2, g, w, jnp.tile(qg, (1, NSA_HEADS)), jnp.tile(kg, (1, NSA_KV_HEADS)), _group_ones())


def _compress_kernel(kc_ref, vc_ref, wk_ref, wv_ref, pk_ref, pv_ref, kg_ref, kcmp_ref, vcT_ref):
    ncp = kc_ref.shape[0] // CMP_STRIDE

    def comp(a_ref, w_ref, p_ref):
        lo = jnp.zeros((ncp, KV_WIDTH), F32)
        hi = jnp.zeros((ncp, KV_WIDTH), F32)
        for l in range(CMP_STRIDE):
            a = a_ref[pl.ds(l, ncp, stride=CMP_STRIDE), :]
            lo += _dot((a + p_ref[0, l:l + 1, :]).astype(BF16), w_ref[0, l])
            hi += _dot((a + p_ref[1, l:l + 1, :]).astype(BF16), w_ref[1, l])
        return lo + pltpu.roll(hi, ncp - 1, 0)

    k = comp(kc_ref, wk_ref, pk_ref)
    v = comp(vc_ref, wv_ref, pv_ref).T
    cend = lax.broadcasted_iota(jnp.int32, (ncp, HEAD_DIM), 0) * CMP_STRIDE + (CMP_LEN - 1)
    col = lax.broadcasted_iota(jnp.int32, (ncp, HEAD_DIM), 1)
    kpos = jnp.where(col < 3, cend // SLC_LEN, jnp.where(col < 6, cend % SLC_LEN, 0)).astype(F32)
    for g in range(NSA_KV_HEADS):
        sl = slice(g * HEAD_DIM, (g + 1) * HEAD_DIM)
        kcmp_ref[g] = jnp.concatenate([_rms(k[:, sl], kg_ref[...]), kpos], axis=1).astype(BF16)
        vcT_ref[g] = v[sl, :].astype(BF16)


def _compress(kc, vc, wk, wv, pk, pv, kg, B, T):
    ncp = T // CMP_STRIDE
    G = NSA_KV_HEADS
    const4 = lambda b: (0, 0, 0, 0)
    const3 = lambda b: (0, 0, 0)
    const2 = lambda b: (0, 0)
    return pl.pallas_call(
        _compress_kernel,
        grid=(B,),
        in_specs=[pl.BlockSpec((T, KV_WIDTH), lambda b: (b, 0)),
                  pl.BlockSpec((T, KV_WIDTH), lambda b: (b, 0)),
                  pl.BlockSpec((2, CMP_STRIDE, KV_WIDTH, KV_WIDTH), const4),
                  pl.BlockSpec((2, CMP_STRIDE, KV_WIDTH, KV_WIDTH), const4),
                  pl.BlockSpec((2, CMP_STRIDE, KV_WIDTH), const3),
                  pl.BlockSpec((2, CMP_STRIDE, KV_WIDTH), const3),
                  pl.BlockSpec((1, HEAD_DIM), const2)],
        out_specs=[pl.BlockSpec((None, G, ncp, 2 * HEAD_DIM), lambda b: (b, 0, 0, 0)),
                   pl.BlockSpec((None, G, HEAD_DIM, ncp), lambda b: (b, 0, 0, 0))],
        out_shape=[jax.ShapeDtypeStruct((B, G, ncp, 2 * HEAD_DIM), BF16),
                   jax.ShapeDtypeStruct((B, G, HEAD_DIM, ncp), BF16)],
        compiler_params=_params(1),
        name="nsa_compress",
    )(kc, vc, wk, wv, pk, pv, kg)


def _compress_weights(w, pos):
    G = NSA_KV_HEADS
    w4 = w.reshape(2, CMP_STRIDE, HEAD_DIM, HEAD_DIM)
    eye = jnp.eye(G, dtype=w.dtype)
    wbd = jnp.einsum('hlde,gk->hlgdke', w4, eye).reshape(2, CMP_STRIDE, KV_WIDTH, KV_WIDTH)
    p = pos.reshape(2, CMP_STRIDE, 1, HEAD_DIM)
    p = jnp.broadcast_to(p, (2, CMP_STRIDE, G, HEAD_DIM)).reshape(2, CMP_STRIDE, KV_WIDTH)
    return wbd.astype(BF16), p


def _split3(x):
    hi = x.astype(BF16)
    r = x - hi.astype(F32)
    mid = r.astype(BF16)
    lo = (r - mid.astype(F32)).astype(BF16)
    return hi, mid, lo


def _nsa_cmp_kernel(qT_ref, qaug_ref, kc_ref, vcT_ref, ovT_ref, ocmp_ref, sel_ref, flag_ref, *, n_sel):
    ncp = kc_ref.shape[0]
    ns = ovT_ref.shape[0]
    tiles = range(CMP_TILES)
    i0 = pl.program_id(2) * CMP_TILES
    lane = lax.broadcasted_iota(jnp.int32, (1, GQ), 1)
    q = [jnp.concatenate([qT_ref[u], qaug_ref[...]], axis=0) for u in tiles]
    t_row = [(i0 + u) * Q_BLOCK + (lane & (Q_BLOCK - 1)) for u in tiles]
    has_cmp = [(t_row[u] >= CMP_LEN - 1).astype(F32) for u in tiles]
    tq = [(i0 + u) * Q_BLOCK + lax.broadcasted_iota(jnp.int32, (1, Q_BLOCK), 1) for u in tiles]
    cur = [tq[u] // SLC_LEN for u in tiles]

    def prefix(rows):
        nsk = rows * CMP_STRIDE // SLC_LEN
        tail0 = max(rows - CMP_TAIL, 0)
        kc = kc_ref[0:rows, :]
        s = [_dot(kc, q[u]) for u in tiles]
        cend = (lax.broadcasted_iota(jnp.int32, (rows - tail0, 1), 0) + tail0) * CMP_STRIDE + (CMP_LEN - 1)
        tail = [jnp.where(t_row[u] >= cend, s[u][tail0:], NEG) for u in tiles]
        s = [jnp.concatenate([s[u][0:tail0], tail[u]], axis=0) if tail0 else tail[u] for u in tiles]
        m = [jnp.max(s[u], axis=0, keepdims=True) for u in tiles]
        e = [jnp.exp2(s[u] - m[u]) for u in tiles]
        p = [e[u] * (has_cmp[u] / jnp.sum(e[u], axis=0, keepdims=True)) for u in tiles]
        vc = vcT_ref[:, 0:rows]
        for u in tiles:
            ocmp_ref[u] = _dot(vc, p[u].astype(BF16))

        ps = [p[u][:, 0:Q_BLOCK] for u in tiles]
        for r in range(1, NSA_GROUP):
            ps = [ps[u] + p[u][:, r * Q_BLOCK:(r + 1) * Q_BLOCK] for u in tiles]
        ov = ovT_ref[0:nsk, 0:rows]
        split = [_split3(ps[u]) for u in tiles]
        imp = [_dot(ov, split[u][0]) + _dot(ov, split[u][1]) + _dot(ov, split[u][2]) for u in tiles]

        blk = lax.broadcasted_iota(jnp.int32, (nsk, 1), 0)
        forced = [(blk == 0) | (blk == cur[u]) | (blk == cur[u] - 1) for u in tiles]
        valid = [blk * SLC_LEN <= tq[u] for u in tiles]
        imp = [jnp.where(forced[u], -3e38, jnp.where(valid[u], imp[u], -BIG)) for u in tiles]
        blk_f = blk.astype(F32)
        sel = [forced[u].astype(F32) for u in tiles]
        for _ in range(n_sel - N_FORCED):
            mx = [jnp.max(imp[u], axis=0, keepdims=True) for u in tiles]
            idx = [jnp.min(jnp.where(imp[u] == mx[u], blk_f, float(ns)), axis=0, keepdims=True) for u in tiles]
            pick = [blk_f == idx[u] for u in tiles]
            sel = [jnp.where(pick[u], 1.0, sel[u]) for u in tiles]
            imp = [jnp.where(pick[u], -3e38, imp[u]) for u in tiles]
        ones = jnp.ones((8, Q_BLOCK), BF16)
        for u in tiles:
            sel_ref[u, 0:nsk, :] = sel[u]
            cnt = _dot_nt(ones, sel[u].astype(BF16))
            flag_ref[u, :, 0:nsk] = (cnt > 0).astype(jnp.int32)
            if nsk < ns:
                sel_ref[u, nsk:, :] = jnp.zeros((ns - nsk, Q_BLOCK), F32)
                flag_ref[u, :, nsk:] = jnp.zeros((8, ns - nsk), jnp.int32)

    n_variants = ncp // CMP_CHUNK
    last = i0 + CMP_TILES - 1
    variant = (last * (Q_BLOCK // CMP_STRIDE) + (Q_BLOCK // CMP_STRIDE - 2)) // CMP_CHUNK
    for k in range(n_variants):
        pl.when(variant == k)(functools.partial(prefix, (k + 1) * CMP_CHUNK))


def _nsa_cmp(qT, qaug, kcmp, vcT, ovT, B, T):
    G = NSA_KV_HEADS
    nq = T // Q_BLOCK
    ncp = T // CMP_STRIDE
    ns = T // SLC_LEN
    n_sel = min(SLC_TOPK, ns)
    assert ncp % CMP_CHUNK == 0 and n_sel > N_FORCED and nq % CMP_TILES == 0
    tile = lambda b, g, i: (b, g, i, 0, 0)
    return pl.pallas_call(
        functools.partial(_nsa_cmp_kernel, n_sel=n_sel),
        grid=(B, G, nq // CMP_TILES),
        in_specs=[pl.BlockSpec((None, None, CMP_TILES, HEAD_DIM, GQ), tile),
                  pl.BlockSpec((None, HEAD_DIM, GQ), lambda b, g, i: (g, 0, 0)),
                  pl.BlockSpec((None, None, ncp, 2 * HEAD_DIM), lambda b, g, i: (b, g, 0, 0)),
                  pl.BlockSpec((None, None, HEAD_DIM, ncp), lambda b, g, i: (b, g, 0, 0)),
                  pl.BlockSpec((ns, ncp), lambda b, g, i: (0, 0))],
        out_specs=[pl.BlockSpec((None, None, CMP_TILES, HEAD_DIM, GQ), tile),
                   pl.BlockSpec((None, None, CMP_TILES, ns, Q_BLOCK), tile),
                   pl.BlockSpec((None, None, CMP_TILES, 8, ns), tile)],
        out_shape=[jax.ShapeDtypeStruct((B, G, nq, HEAD_DIM, GQ), F32),
                   jax.ShapeDtypeStruct((B, G, nq, ns, Q_BLOCK), F32),
                   jax.ShapeDtypeStruct((B, G, nq, 8, ns), jnp.int32)],
        compiler_params=_params(3),
        name="nsa_cmp",
    )(qT, qaug, kcmp, vcT, ovT)


def _nsa_main_kernel(list_ref, cnt_ref, qT_ref, qaug_ref, ks_ref, vsT_ref, kw_ref, vwT_ref, sel_ref, gT_ref, ocmp_ref,
                     lowb_ref, causb_ref, out_ref, m_sc, acc_sc, win_sc):
    b, g = pl.program_id(0), pl.program_id(1)
    tiles = range(MAIN_TILES)
    i = [pl.program_id(2) * MAIN_TILES + u for u in tiles]
    tile_id = [(b * pl.num_programs(1) + g) * (pl.num_programs(2) * MAIN_TILES) + i[u] for u in tiles]
    n_steps = sel_ref.shape[1] // 2
    q = [jnp.concatenate([qT_ref[u], qaug_ref[...]], axis=0) for u in tiles]
    k0 = [pl.multiple_of(i[u] * Q_BLOCK, Q_BLOCK) for u in tiles]

    def sel_bias(u, j, valid):
        def row(r):
            picked = (sel_ref[u, pl.ds(r, 1), :] > 0.5) & valid
            return jnp.concatenate([jnp.where(picked, 0.0, NEG)] * NSA_GROUP, axis=1)
        return row(2 * j), row(2 * j + 1)

    def add_sel_bias(s, ba, bb):
        return jnp.concatenate([s[0:SLC_LEN] + ba, s[SLC_LEN:] + bb], axis=0)

    lowb = jnp.concatenate([lowb_ref[...]] * NSA_GROUP, axis=1)
    causb = jnp.concatenate([causb_ref[...]] * NSA_GROUP, axis=1)

    bias_d = [sel_bias(u, i[u], True) for u in tiles]
    sd = [_dot(ks_ref[pl.ds(k0[u], KEY_STEP), :], q[u]) for u in tiles]
    sw = [_dot(kw_ref[pl.ds(k0[u], WIN_KEYS), :], q[u]) for u in tiles]
    sd = [add_sel_bias(sd[u], *bias_d[u]) + causb for u in tiles]
    sw = [jnp.concatenate([sw[u][0:Q_BLOCK] + lowb, sw[u][Q_BLOCK:WIN], sw[u][WIN:] + causb], axis=0) for u in tiles]
    md = [jnp.max(sd[u], axis=0, keepdims=True) for u in tiles]
    mw = [jnp.max(sw[u], axis=0, keepdims=True) for u in tiles]
    accd = [_dot(vsT_ref[:, pl.ds(k0[u], KEY_STEP)], jnp.exp2((sd[u] - md[u]).astype(BF16))) for u in tiles]
    ow = [_dot(vwT_ref[:, pl.ds(k0[u], WIN_KEYS)], jnp.exp2((sw[u] - mw[u]).astype(BF16))) for u in tiles]
    for u in tiles:
        m_sc[u] = md[u]
        acc_sc[u] = accd[u]
        win_sc[u] = ow[u][0:HEAD_DIM] / ow[u][HEAD_DIM:HEAD_DIM + 1]

    def scores(u, t):
        ks, vs, biases = [], [], []
        for x in range(STEP_GROUP):
            j = list_ref[tile_id[u] * n_steps + t * STEP_GROUP + x]
            valid = j >= 0
            j = jnp.maximum(j, 0)
            kj = pl.multiple_of(j * KEY_STEP, KEY_STEP)
            ks.append(ks_ref[pl.ds(kj, KEY_STEP), :])
            vs.append(vsT_ref[:, pl.ds(kj, KEY_STEP)])
            biases.append(sel_bias(u, j, valid))
        s = _dot(jnp.concatenate(ks, axis=0), q[u])
        s = jnp.concatenate([add_sel_bias(s[x * KEY_STEP:(x + 1) * KEY_STEP], *biases[x])
                             for x in range(STEP_GROUP)], axis=0)
        return s, jnp.max(s, axis=0, keepdims=True), jnp.concatenate(vs, axis=1)

    def accumulate(u, s, smax, vcat):
        m_old = m_sc[u]
        m_new = jnp.maximum(m_old, smax)
        alpha = jnp.exp2(m_old - m_new)
        acc_sc[u] = alpha * acc_sc[u] + _dot(vcat, jnp.exp2((s - m_new).astype(BF16)))
        m_sc[u] = m_new

    def run(work, t, carry):
        staged = [(u, scores(u, t * mult + off)) for (u, mult, off) in work]
        for u, args in staged:
            accumulate(u, *args)
        return carry

    for u0 in range(0, MAIN_TILES, LOOP_TILES):
        us = range(u0, u0 + LOOP_TILES)
        n_groups = functools.reduce(
            jnp.maximum, [(cnt_ref[tile_id[u]] + (STEP_GROUP - 1)) // STEP_GROUP for u in us])
        lax.fori_loop(0, n_groups // 2, functools.partial(run, [(u, 2, off) for off in (0, 1) for u in us]), 0)
        lax.fori_loop(n_groups // 2 * 2, n_groups, functools.partial(run, [(u, 1, 0) for u in us]), 0)

    def gate(u, k):
        rows = [gT_ref[u, pl.ds(g * (NSA_GROUP * 3) + r * 3 + k, 1), :] for r in range(NSA_GROUP)]
        return jnp.concatenate(rows, axis=1)

    o_slc = [acc_sc[u, 0:HEAD_DIM, :] / acc_sc[u, HEAD_DIM:HEAD_DIM + 1, :] for u in tiles]
    o = [gate(u, 0) * ocmp_ref[u] + gate(u, 1) * o_slc[u] + gate(u, 2) * win_sc[u] for u in tiles]
    o = [jnp.concatenate([o[u], jnp.zeros_like(o[u])], axis=0) for u in tiles]
    for r in range(NSA_GROUP):
        ot = [o[u][:, r * Q_BLOCK:(r + 1) * Q_BLOCK].T[:, 0:HEAD_DIM] for u in tiles]
        for u in tiles:
            out_ref[u * Q_BLOCK:(u + 1) * Q_BLOCK, r * HEAD_DIM:(r + 1) * HEAD_DIM] = ot[u]


def _nsa_steps_kernel(flagT_ref, pairT_ref, list_ref, cnt_ref, *, nq):
    n_steps, nt = list_ref.shape
    need = _dot(pairT_ref[...], flagT_ref[...].astype(BF16)) > 0
    step = lax.broadcasted_iota(jnp.int32, (n_steps, 1), 0)
    own = lax.broadcasted_iota(jnp.int32, (1, nt), 1) % nq
    need = need & (step < own)
    need_f = need.astype(F32)
    earlier = (lax.broadcasted_iota(jnp.int32, (n_steps, n_steps), 1) < step).astype(BF16)
    slot = _dot(earlier, need_f.astype(BF16))
    total = jnp.sum(need_f, axis=0, keepdims=True)
    cnt_ref[...] = jnp.broadcast_to(total, cnt_ref.shape).astype(jnp.int32)
    step_f = step.astype(F32)
    for p in range(n_steps):
        val = jnp.sum(jnp.where(need & (slot == p), step_f, 0.0), axis=0, keepdims=True)
        list_ref[p:p + 1, :] = jnp.where(total > p, val, -1.0).astype(jnp.int32)


def _nsa_steps(flags, nq):
    nt, ns = flags.shape
    n_steps = ns // 2
    pairT = jnp.asarray(np.arange(n_steps)[:, None] == np.arange(ns)[None, :] // 2, BF16)
    lists, counts = pl.pallas_call(
        functools.partial(_nsa_steps_kernel, nq=nq),
        out_shape=[jax.ShapeDtypeStruct((n_steps, nt), jnp.int32), jax.ShapeDtypeStruct((8, nt), jnp.int32)],
        name="nsa_steps",
    )(flags.T.astype(F32), pairT)
    return lists.T.reshape(-1), counts[0]


def _nsa_main(lists, counts, qT, qaug, ks, vsT, kw, vwT, sel, gT, ocmp, lowb, causb, B, T):
    G = NSA_KV_HEADS
    nq = T // Q_BLOCK
    ns = T // SLC_LEN
    whole = lambda b, g, i, *_: (b, g, 0, 0)
    tile = lambda b, g, i, *_: (b, g, i, 0, 0)
    const = lambda b, g, i, *_: (0, 0)
    grid_spec = pltpu.PrefetchScalarGridSpec(
        num_scalar_prefetch=2,
        grid=(B, G, nq // MAIN_TILES),
        in_specs=[pl.BlockSpec((None, None, MAIN_TILES, HEAD_DIM, GQ), tile),
                  pl.BlockSpec((None, HEAD_DIM, GQ), lambda b, g, i, *_: (g, 0, 0)),
                  pl.BlockSpec((None, None, T, 2 * HEAD_DIM), whole),
                  pl.BlockSpec((None, None, V_ROWS, T), whole),
                  pl.BlockSpec((None, None, T + WIN, 2 * HEAD_DIM), whole),
                  pl.BlockSpec((None, None, V_ROWS, T + WIN), whole),
                  pl.BlockSpec((None, None, MAIN_TILES, ns, Q_BLOCK), tile),
                  pl.BlockSpec((None, MAIN_TILES, 32, Q_BLOCK), lambda b, g, i, *_: (b, i, 0, 0)),
                  pl.BlockSpec((None, None, MAIN_TILES, HEAD_DIM, GQ), tile),
                  pl.BlockSpec((Q_BLOCK, Q_BLOCK), const),
                  pl.BlockSpec((Q_BLOCK, Q_BLOCK), const)],
        out_specs=pl.BlockSpec((None, MAIN_TILES * Q_BLOCK, NSA_GROUP * HEAD_DIM), lambda b, g, i, *_: (b, i, g)),
        scratch_shapes=[pltpu.VMEM((MAIN_TILES, 1, GQ), F32), pltpu.VMEM((MAIN_TILES, V_ROWS, GQ), F32),
                        pltpu.VMEM((MAIN_TILES, HEAD_DIM, GQ), F32)],
    )
    return pl.pallas_call(
        _nsa_main_kernel,
        grid_spec=grid_spec,
        out_shape=jax.ShapeDtypeStruct((B, T, NSA_WIDTH), F32),
        compiler_params=_params(3),
        name="nsa_main",
    )(lists, counts, qT, qaug, ks, vsT, kw, vwT, sel, gT, ocmp, lowb, causb)


def _ret_kernel(p_ref, decay_ref, xi_ref, zeta_ref, gch_ref, ng_ref, ones_ref, out_ref, state_ref):
    @pl.when(pl.program_id(0) == 0)
    def _():
        state_ref[...] = jnp.zeros(state_ref.shape, F32)

    rows = range(p_ref.shape[0])
    kw = RET_HEADS * RET_DK
    p = [p_ref[b] for b in rows]
    rq = [p[b][:, 0:kw] * (RET_DK ** -0.5) for b in rows]
    rk = [p[b][:, kw:2 * kw] for b in rows]
    rkT = [rk[b].T for b in rows]
    rv = [p[b][:, 2 * kw:2 * kw + RET_WIDTH] for b in rows]
    xi = xi_ref[...]
    outs = [[] for _ in rows]
    for h in range(RET_HEADS):
        dk = slice(h * RET_DK, (h + 1) * RET_DK)
        dv = slice(h * RET_DV, (h + 1) * RET_DV)
        st = [state_ref[b, h] for b in rows]
        inner = [_dot_nt(rq[b][:, dk], rk[b][:, dk]) * decay_ref[h] for b in rows]
        o = [_dot(inner[b], rv[b][:, dv]) + _dot(rq[b][:, dk], st[b]) * xi[:, h:h + 1] for b in rows]
        for b in rows:
            state_ref[b, h] = (st[b] * gch_ref[h:h + 1, 0:1]
                               + _dot(rkT[b][dk, :] * zeta_ref[h:h + 1, :], rv[b][:, dv]))
            outs[b].append(o[b])
    normed = [_group_rms(jnp.concatenate(outs[b], axis=1), ng_ref[...], ones_ref) for b in rows]
    for b in rows:
        rg = p[b][:, 2 * kw + RET_WIDTH:2 * kw + 2 * RET_WIDTH]
        out_ref[b] = normed[b] * (rg * jax.nn.sigmoid(rg))


def _ret_consts():
    H, C = RET_HEADS, RET_CHUNK
    log_g = jnp.log1p(-jnp.exp2(-5.0 - jnp.arange(H, dtype=F32)))
    idx = jnp.arange(C, dtype=F32)
    diff = idx[:, None] - idx[None, :]
    decay = jnp.where(diff >= 0, jnp.exp(jnp.maximum(diff, 0.0) * log_g[:, None, None]), 0.0)
    zeta = jnp.exp((C - 1 - idx) * log_g[:, None])
    xi = jnp.exp((idx + 1) * log_g[:, None]).T
    g_chunk = jnp.broadcast_to(jnp.exp(C * log_g)[:, None], (H, LANES))
    return decay, xi, zeta, g_chunk


def _retention(pret, ng, B, T):
    nch = T // RET_CHUNK
    decay, xi, zeta, gch = _ret_consts()
    c2 = lambda c: (0, 0)
    out = pl.pallas_call(
        _ret_kernel,
        grid=(nch,),
        in_specs=[pl.BlockSpec((B, RET_CHUNK, _RET_COLS), lambda c: (0, c, 0)),
                  pl.BlockSpec((RET_HEADS, RET_CHUNK, RET_CHUNK), lambda c: (0, 0, 0)),
                  pl.BlockSpec((RET_CHUNK, RET_HEADS), c2),
                  pl.BlockSpec((RET_HEADS, RET_CHUNK), c2),
                  pl.BlockSpec((RET_HEADS, LANES), c2),
                  pl.BlockSpec((1, RET_WIDTH), c2),
                  pl.BlockSpec((RET_WIDTH, RET_WIDTH), c2)],
        out_specs=pl.BlockSpec((B, RET_CHUNK, RET_WIDTH), lambda c: (0, c, 0)),
        out_shape=jax.ShapeDtypeStruct((B, T, RET_WIDTH), F32),
        scratch_shapes=[pltpu.VMEM((B, RET_HEADS, RET_DK, RET_DV), F32)],
        compiler_params=_params(1),
        name="retention",
    )(pret.reshape(B, T, _RET_COLS), decay, xi, zeta, gch, ng, _group_ones())
    return out.reshape(B * T, RET_WIDTH)


def _mixer_residual(x_ref, nsa_ref, ret_ref, wo_ref):
    return (x_ref[...] + _dot(nsa_ref[...].astype(BF16), wo_ref[0:NSA_WIDTH, :])
            + _dot(ret_ref[...].astype(BF16), wo_ref[NSA_WIDTH:, :]))


def _mixer_specs(tm, d, row, const):
    return [pl.BlockSpec((tm, d), row), pl.BlockSpec((tm, NSA_WIDTH), row), pl.BlockSpec((tm, RET_WIDTH), row),
            pl.BlockSpec((NSA_WIDTH + RET_WIDTH, d), const)]


def _ffn_kernel(x_ref, nsa_ref, ret_ref, wo_ref, g_ref, wg_ref, wu_ref, wd_ref, o_ref, h_sc):
    f = pl.program_id(1)

    @pl.when(f == 0)
    def _():
        x = _mixer_residual(x_ref, nsa_ref, ret_ref, wo_ref)
        h_sc[...] = _rms(x, g_ref[...]).astype(BF16)
        o_ref[...] = x

    h = h_sc[...]
    a = _dot(h, wg_ref[...])
    act = (a * jax.nn.sigmoid(a) * _dot(h, wu_ref[...])).astype(BF16)
    o_ref[...] += _dot(act, wd_ref[...])


def _ffn(x2, nsa, ret, wo, g, wg, wu, wd, tm=512, fc=1408):
    n, d = x2.shape
    dff = wg.shape[1]
    return pl.pallas_call(
        _ffn_kernel,
        grid=(n // tm, dff // fc),
        in_specs=_mixer_specs(tm, d, lambda i, f: (i, 0), lambda i, f: (0, 0)) + [
            pl.BlockSpec((1, d), lambda i, f: (0, 0)),
            pl.BlockSpec((d, fc), lambda i, f: (0, f)),
            pl.BlockSpec((d, fc), lambda i, f: (0, f)),
            pl.BlockSpec((fc, d), lambda i, f: (f, 0))],
        out_specs=pl.BlockSpec((tm, d), lambda i, f: (i, 0)),
        out_shape=jax.ShapeDtypeStruct((n, d), F32),
        scratch_shapes=[pltpu.VMEM((tm, d), BF16)],
        compiler_params=_params(2),
        name="ffn_dense",
    )(x2, nsa, ret, wo, g, wg, wu, wd)


def _router_kernel(x_ref, nsa_ref, ret_ref, wo_ref, g_ref, r_ref, rb_ref, tri_ref,
                   x1_ref, h_ref, rank_ref, comb_ref, rankT_ref, cnt_ref):
    x = _mixer_residual(x_ref, nsa_ref, ret_ref, wo_ref)
    x1_ref[...] = x
    h = _rms(x, g_ref[...])
    h_ref[...] = h.astype(BF16)
    hh, hm, hl = _split3(h)
    rh, rm, rl = _split3(r_ref[...])
    logits = (_dot(hh, rh) + (_dot(hh, rm) + _dot(hm, rh)) + (_dot(hh, rl) + _dot(hm, rm) + _dot(hl, rh))
              + rb_ref[...])
    lane = lax.broadcasted_iota(jnp.int32, logits.shape, 1).astype(F32)
    logits = jnp.where(lane < N_EXPERTS, logits, NEG)
    m1 = jnp.max(logits, axis=1, keepdims=True)
    i1 = jnp.min(jnp.where(logits == m1, lane, float(LANES)), axis=1, keepdims=True)
    l2 = jnp.where(lane == i1, NEG, logits)
    m2 = jnp.max(l2, axis=1, keepdims=True)
    i2 = jnp.min(jnp.where(l2 == m2, lane, float(LANES)), axis=1, keepdims=True)
    e2 = jnp.exp(m2 - m1)
    w1 = 1.0 / (1.0 + e2)
    w2 = e2 / (1.0 + e2)
    use1, use2 = lane == i1, lane == i2
    comb_ref[...] = jnp.where(use1, w1, 0.0) + jnp.where(use2, w2, 0.0)
    use = (use1 | use2).astype(F32)
    rank = jnp.where(use > 0, _dot(tri_ref[...], use.astype(BF16)), -1.0)
    rank_ref[...] = rank
    rankT_ref[...] = rank.T[0:N_EXPERTS, :]
    cnt_ref[...] = jnp.broadcast_to(jnp.sum(use, axis=0, keepdims=True), cnt_ref.shape).astype(jnp.int32)


def _router(x2, nsa, ret, wo, g, router, rb, tm):
    n, d = x2.shape
    nt = n // tm
    tri = jnp.asarray(np.arange(tm)[:, None] > np.arange(tm)[None, :], BF16)
    rpad = jnp.zeros((d, LANES), F32).at[:, :N_EXPERTS].set(router)
    rbpad = jnp.zeros((1, LANES), F32).at[0, :N_EXPERTS].set(rb)
    c2 = lambda i: (0, 0)
    row = lambda i: (i, 0)
    return pl.pallas_call(
        _router_kernel,
        grid=(nt,),
        in_specs=_mixer_specs(tm, d, row, c2) + [
            pl.BlockSpec((1, d), c2),
            pl.BlockSpec((d, LANES), c2),
            pl.BlockSpec((1, LANES), c2),
            pl.BlockSpec((tm, tm), c2)],
        out_specs=[pl.BlockSpec((tm, d), row),
                   pl.BlockSpec((tm, d), row),
                   pl.BlockSpec((tm, LANES), row),
                   pl.BlockSpec((tm, LANES), row),
                   pl.BlockSpec((N_EXPERTS, tm), lambda i: (0, i)),
                   pl.BlockSpec((None, 8, LANES), lambda i: (i, 0, 0))],
        out_shape=[jax.ShapeDtypeStruct((n, d), F32),
                   jax.ShapeDtypeStruct((n, d), BF16),
                   jax.ShapeDtypeStruct((n, LANES), F32),
                   jax.ShapeDtypeStruct((n, LANES), F32),
                   jax.ShapeDtypeStruct((N_EXPERTS, n), F32),
                   jax.ShapeDtypeStruct((nt, 8, LANES), jnp.int32)],
        compiler_params=_params(1),
        name="moe_router",
    )(x2, nsa, ret, wo, g, rpad, rbpad, tri)


MOE_SUB = 144
MOE_MOVE = 2 * MOE_SUB


def _moe_kernel(cnt_ref, h_ref, rankT_ref, rank_ref, comb_ref, wg_ref, wu_ref, wd_ref, x_ref, o_ref, hc_sc, oacc_sc):
    t, e, f = pl.program_id(0), pl.program_id(1), pl.program_id(2)
    nf = pl.num_programs(2)
    tm = h_ref.shape[0]
    nsub = (cnt_ref[t * N_EXPERTS + e] + (MOE_SUB - 1)) // MOE_SUB
    nmove = (nsub + 1) // 2

    @pl.when((e == 0) & (f == 0))
    def _():
        o_ref[...] = x_ref[...]

    @pl.when(f == 0)
    def _():
        rank_row = rankT_ref[...]

        def gather(s, c):
            r0 = pl.multiple_of(s * MOE_MOVE, MOE_MOVE)
            rows = (lax.broadcasted_iota(jnp.int32, (MOE_MOVE, 1), 0) + r0).astype(F32)
            onehot = (rows == rank_row).astype(BF16)
            hc_sc[pl.ds(r0, MOE_MOVE), :] = _dot(onehot, h_ref[...]).astype(BF16)
            oacc_sc[pl.ds(r0, MOE_MOVE), :] = jnp.zeros((MOE_MOVE, oacc_sc.shape[1]), F32)
            return c

        lax.fori_loop(0, nmove, gather, 0)

    def expert(n_rows, s, c):
        r0 = pl.multiple_of(s * n_rows, n_rows)
        rows = hc_sc[pl.ds(r0, n_rows), :]
        a = _dot(rows, wg_ref[...])
        act = (a * jax.nn.sigmoid(a) * _dot(rows, wu_ref[...])).astype(BF16)
        oacc_sc[pl.ds(r0, n_rows), :] += _dot(act, wd_ref[...])
        return c

    lax.fori_loop(0, nsub // 2, functools.partial(expert, MOE_MOVE), 0)
    lax.fori_loop(nsub // 2 * 2, nsub, functools.partial(expert, MOE_SUB), 0)

    @pl.when(f == nf - 1)
    def _():
        is_e = lax.broadcasted_iota(jnp.int32, (1, LANES), 1) == e
        rank_col = jnp.sum(jnp.where(is_e, rank_ref[...], 0.0), axis=1, keepdims=True)
        comb_col = jnp.sum(jnp.where(is_e, comb_ref[...], 0.0), axis=1, keepdims=True)

        def scatter(s, c):
            r0 = pl.multiple_of(s * MOE_MOVE, MOE_MOVE)
            cols = (lax.broadcasted_iota(jnp.int32, (1, MOE_MOVE), 1) + r0).astype(F32)
            onehot = (rank_col == cols).astype(BF16)
            y = _dot(onehot, oacc_sc[pl.ds(r0, MOE_MOVE), :].astype(BF16))
            o_ref[...] += comb_col * y
            return c

        lax.fori_loop(0, nmove, scatter, 0)


def _moe(counts, h, rankT, rank, comb, wg, wu, wd, x2, tm, fc=1408):
    n, d = x2.shape
    dff = wg.shape[2]
    rows_cap = pl.cdiv(pl.cdiv(tm, MOE_SUB), 2) * MOE_MOVE
    grid_spec = pltpu.PrefetchScalarGridSpec(
        num_scalar_prefetch=1,
        grid=(n // tm, N_EXPERTS, dff // fc),
        in_specs=[pl.BlockSpec((tm, d), lambda t, e, f, c: (t, 0)),
                  pl.BlockSpec((None, 1, tm), lambda t, e, f, c: (e, 0, t)),
                  pl.BlockSpec((tm, LANES), lambda t, e, f, c: (t, 0)),
                  pl.BlockSpec((tm, LANES), lambda t, e, f, c: (t, 0)),
                  pl.BlockSpec((None, d, fc), lambda t, e, f, c: (e, 0, f)),
                  pl.BlockSpec((None, d, fc), lambda t, e, f, c: (e, 0, f)),
                  pl.BlockSpec((None, fc, d), lambda t, e, f, c: (e, f, 0)),
                  pl.BlockSpec((tm, d), lambda t, e, f, c: (t, 0))],
        out_specs=pl.BlockSpec((tm, d), lambda t, e, f, c: (t, 0)),
        scratch_shapes=[pltpu.VMEM((rows_cap, d), BF16), pltpu.VMEM((rows_cap, d), F32)],
    )
    return pl.pallas_call(
        _moe_kernel,
        grid_spec=grid_spec,
        out_shape=jax.ShapeDtypeStruct((n, d), F32),
        compiler_params=_params(3),
        name="moe_experts",
    )(counts, h, rankT.reshape(N_EXPERTS, 1, n), rank, comb, wg, wu, wd, x2)


def _permute_w_in(w):
    o = np.cumsum((0, NSA_WIDTH) + (KV_WIDTH,) * 6 + (3 * NSA_HEADS,))
    q, kc, vc, ks, vs, kw, vw, gts = (w[:, o[k]:o[k + 1]] for k in range(8))
    ret = w[:, o[8]:]
    pad = jnp.zeros((w.shape[0], LANES - 3 * NSA_HEADS), w.dtype)
    return jnp.concatenate([q, ks, kw, vs, vw, kc, vc, gts, pad, ret], axis=1).astype(BF16)


def _nsa_consts(T):
    ncp = T // CMP_STRIDE
    ns = T // SLC_LEN
    cs = np.arange(ncp) * CMP_STRIDE
    ss = np.arange(ns) * SLC_LEN
    ov = np.clip(np.minimum(cs[None, :] + CMP_LEN, ss[:, None] + SLC_LEN) - np.maximum(cs[None, :], ss[:, None]), 0, None)
    ovT = (ov.astype(np.float32) / CMP_LEN)
    ovT[:, ncp - 1] = 0.0
    h = np.arange(NSA_HEADS).reshape(NSA_KV_HEADS, NSA_GROUP) + 1
    slopes = np.exp2(-8.0 * h / NSA_HEADS).astype(np.float32)
    slopes = np.repeat(slopes, Q_BLOCK, axis=1)
    parts, rest = [], np.float64(LOG2E)
    for _ in range(3):
        part = np.float64(np.asarray(rest).astype(BF16))
        parts.append(part)
        rest = rest - part
    qaug = np.zeros((NSA_KV_HEADS, HEAD_DIM, GQ), np.float32)
    for k, part in enumerate(parts):
        qaug[:, k, :] = part * SLC_LEN * slopes
        qaug[:, 3 + k, :] = part * slopes
    kq = np.arange(Q_BLOCK)[:, None] - np.arange(Q_BLOCK)[None, :]
    causb = np.where(kq <= 0, 0.0, NEG).astype(np.float32)
    lowb = np.where(kq > 0, 0.0, NEG).astype(np.float32)
    return jnp.asarray(ovT, BF16), jnp.asarray(qaug, BF16), jnp.asarray(lowb), jnp.asarray(causb)


def _mixer(x2, B, T, norm_g, w_in, q_norm_g, k_norm_g, cmp_pos, w_cmp, ret_norm_g, w_out):
    ns = T // SLC_LEN
    kc, vc, pret, qT, ks, kw, vsT, vwT, gT = _inproj(x2, norm_g[None, :], _permute_w_in(w_in), q_norm_g[None, :],
                                                     k_norm_g[1:3], B, T)
    wk, pk = _compress_weights(w_cmp[0], cmp_pos[0])
    wv, pv = _compress_weights(w_cmp[1], cmp_pos[1])
    kcmp, vcT = _compress(kc, vc, wk, wv, pk, pv, k_norm_g[0:1], B, T)
    ovT, qaug, lowb, causb = _nsa_consts(T)
    ocmp, sel, flags = _nsa_cmp(qT, qaug, kcmp, vcT, ovT, B, T)
    lists, counts = _nsa_steps(flags[:, :, :, 0, :].reshape(-1, ns), T // Q_BLOCK)
    kpad = jnp.zeros((WIN, 2 * HEAD_DIM), BF16).at[:, HEAD_DIM:HEAD_DIM + 3].set(-2.0 ** 100)
    kw = jnp.concatenate([jnp.broadcast_to(kpad, kw.shape[:2] + kpad.shape), kw], axis=2)
    vwT = jnp.pad(vwT, ((0, 0), (0, 0), (0, 0), (WIN, 0)))
    nsa = _nsa_main(lists, counts, qT, qaug, ks, vsT, kw, vwT, sel, gT, ocmp, lowb, causb, B, T)
    ret = _retention(pret, ret_norm_g[None, :], B, T)
    return nsa.reshape(B * T, NSA_WIDTH), ret, w_out.astype(BF16)


def _moe_layer(x2, nsa, ret, wo, norm_g, router, router_b, wg, wu, wd, tm=1024):
    tm = min(tm, x2.shape[0])
    x1, h, rank, comb, rankT, cnt = _router(x2, nsa, ret, wo, norm_g[None, :], router, router_b, tm)
    counts = cnt[:, 0, :N_EXPERTS].reshape(-1)
    return _moe(counts, h, rankT, rank, comb, wg.astype(BF16), wu.astype(BF16), wd.astype(BF16), x1, tm)


def kernel(x, norm_mix_g, w_in, q_norm_g, k_norm_g, cmp_pos, w_cmp, ret_norm_g, w_out, norm_ffn_g,
           ffn_w_gate, ffn_w_up, ffn_w_down, moe_router, moe_router_b, moe_w_gate, moe_w_up, moe_w_down):
    B, T, D = x.shape
    depth = norm_mix_g.shape[0]
    x2 = x.reshape(B * T, D)
    for l in range(depth):
        nsa, ret, wo = _mixer(x2, B, T, norm_mix_g[l], w_in[l], q_norm_g[l], k_norm_g[l], cmp_pos[l], w_cmp[l],
                              ret_norm_g[l], w_out[l])
        j = l // 2
        if l % 2 == 0:
            x2 = _ffn(x2, nsa, ret, wo, norm_ffn_g[l][None, :], ffn_w_gate[j].astype(BF16),
                      ffn_w_up[j].astype(BF16), ffn_w_down[j].astype(BF16))
        else:
            x2 = _moe_layer(x2, nsa, ret, wo, norm_ffn_g[l], moe_router[j], moe_router_b[j], moe_w_gate[j],
                            moe_w_up[j], moe_w_down[j])
    return x2.reshape(B, T, D)
```

```python
import functools

import numpy as np
import jax
import jax.numpy as jnp
from jax import lax
from jax.experimental import pallas as pl
from jax.experimental.pallas import tpu as pltpu

F32 = jnp.float32
BF16 = jnp.bfloat16

HEAD_DIM = 64
NSA_HEADS = 8
NSA_KV_HEADS = 2
NSA_GROUP = NSA_HEADS // NSA_KV_HEADS
RET_HEADS = 8
RET_DK = 32
RET_DV = 64
NSA_WIDTH = NSA_HEADS * HEAD_DIM
RET_WIDTH = RET_HEADS * RET_DV
KV_WIDTH = NSA_KV_HEADS * HEAD_DIM
CMP_LEN = 32
CMP_STRIDE = 16
SLC_LEN = 64
SLC_TOPK = 16
WIN = 512
Q_BLOCK = 128
RET_CHUNK = 128
N_EXPERTS = 8
EPS = 1e-6
NEG = -1e30
BIG = 1e9
LANES = 128
GQ = NSA_GROUP * Q_BLOCK
KEY_STEP = 128
STEP_GROUP = 4
N_FORCED = 3
CMP_CHUNK = 128
CMP_TILES = 8
MAIN_TILES = 8
LOOP_TILES = 2
CMP_TAIL = CMP_CHUNK + 8
WIN_KEYS = WIN + Q_BLOCK
V_ROWS = HEAD_DIM + 16
LOG2E = 1.4426950408889634
VMEM_LIMIT = 60 * 1024 * 1024

_C_Q = 0
_C_KV = _C_Q + NSA_WIDTH
_C_KC = _C_KV + 4 * KV_WIDTH
_C_VC = _C_KC + KV_WIDTH
_C_GT = _C_VC + KV_WIDTH
_C_RET = _C_GT + LANES
_RET_COLS = 2 * RET_HEADS * RET_DK + 2 * RET_WIDTH
_C_END = _C_RET + _RET_COLS


def _params(n_axes, vmem=VMEM_LIMIT):
    return pltpu.CompilerParams(dimension_semantics=("arbitrary",) * n_axes, vmem_limit_bytes=vmem)


def _dot(a, b):
    return jnp.dot(a, b, preferred_element_type=F32)


def _dot_nt(a, b):
    return lax.dot_general(a, b, (((1,), (1,)), ((), ())), preferred_element_type=F32)


def _rms(x, g):
    return x * lax.rsqrt(jnp.mean(x * x, axis=-1, keepdims=True) + EPS) * g


def _group_rms(x, g, ones_ref):
    w = x.shape[1]
    ones = ones_ref[0:w, 0:w]
    hi, mid, lo = _split3(x * x)
    ms = (_dot(hi, ones) + _dot(mid, ones) + _dot(lo, ones)) * (1.0 / HEAD_DIM)
    return x * lax.rsqrt(ms + EPS) * g


def _group_ones():
    lane = np.arange(NSA_WIDTH) // HEAD_DIM
    return jnp.asarray(lane[:, None] == lane[None, :], BF16)


def _inproj_kernel(x_ref, g_ref, w_ref, qg_ref, kg_ref, ones_ref, kc_ref, vc_ref, ret_ref,
                   qT_ref, ks_ref, kw_ref, vsT_ref, vwT_ref, gT_ref, *, steps_per_row):
    n_tok = x_ref.shape[0]
    tiles = range(n_tok // Q_BLOCK)
    rows = [slice(u * Q_BLOCK, (u + 1) * Q_BLOCK) for u in tiles]
    xn = _rms(x_ref[...], g_ref[...]).astype(BF16)
    q = _dot(xn, w_ref[:, _C_Q:_C_KV])
    kv = _dot(xn, w_ref[:, _C_KV:_C_KC])
    gt = _dot(xn, w_ref[:, _C_GT:_C_RET])

    scale = HEAD_DIM ** -0.5 * LOG2E
    qn = _group_rms(q, qg_ref[...], ones_ref) * scale
    qt = [qn[rows[u]].T for u in tiles]
    for g in range(NSA_KV_HEADS):
        for r in range(NSA_GROUP):
            h = g * NSA_GROUP + r
            for u in tiles:
                qT_ref[g, u, :, r * Q_BLOCK:(r + 1) * Q_BLOCK] = qt[u][h * HEAD_DIM:(h + 1) * HEAD_DIM, :].astype(BF16)
    vst = [kv[rows[u], 2 * KV_WIDTH:3 * KV_WIDTH].T for u in tiles]
    vwt = [kv[rows[u], 3 * KV_WIDTH:4 * KV_WIDTH].T for u in tiles]
    gts = [jax.nn.sigmoid(gt[rows[u], :].T[0:32, :]) for u in tiles]
    pos0 = (pl.program_id(0) % steps_per_row) * n_tok
    pos = pos0 + lax.broadcasted_iota(jnp.int32, (n_tok, HEAD_DIM), 0)
    col = lax.broadcasted_iota(jnp.int32, (n_tok, HEAD_DIM), 1)
    kpos = jnp.where(col < 3, pos // SLC_LEN, jnp.where(col < 6, pos % SLC_LEN, 0)).astype(F32)
    ones_row = (lax.broadcasted_iota(jnp.int32, (V_ROWS - HEAD_DIM, Q_BLOCK), 0) == 0).astype(F32)
    ks = _group_rms(kv[:, 0:KV_WIDTH], kg_ref[0:1, :], ones_ref)
    kw = _group_rms(kv[:, KV_WIDTH:2 * KV_WIDTH], kg_ref[1:2, :], ones_ref)
    for g in range(NSA_KV_HEADS):
        sl = slice(g * HEAD_DIM, (g + 1) * HEAD_DIM)
        ks_ref[g] = jnp.concatenate([ks[:, sl], kpos], axis=1).astype(BF16)
        kw_ref[g] = jnp.concatenate([kw[:, sl], kpos], axis=1).astype(BF16)
        for u in tiles:
            vsT_ref[g, :, rows[u]] = jnp.concatenate([vst[u][sl, :], ones_row], axis=0).astype(BF16)
            vwT_ref[g, :, rows[u]] = jnp.concatenate([vwt[u][sl, :], ones_row], axis=0).astype(BF16)
    for u in tiles:
        gT_ref[u] = gts[u]
    ret_ref[...] = _dot(xn, w_ref[:, _C_RET:_C_END])
    kc_ref[...] = _dot(xn, w_ref[:, _C_KC:_C_VC])
    vc_ref[...] = _dot(xn, w_ref[:, _C_VC:_C_GT])


def _inproj(x2, g, w, qg, kg, B, T, tm=1024):
    n, d = x2.shape
    nq = T // Q_BLOCK
    G = NSA_KV_HEADS
    tiles = tm // Q_BLOCK
    spr = T // tm
    assert T % tm == 0
    const = lambda i: (0, 0)
    row = lambda i: (i, 0)
    return pl.pallas_call(
        functools.partial(_inproj_kernel, steps_per_row=spr),
        grid=(n // tm,),
        in_specs=[pl.BlockSpec((tm, d), row),
                  pl.BlockSpec((1, d), const),
                  pl.BlockSpec((d, _C_END), const),
                  pl.BlockSpec((1, NSA_WIDTH), const),
                  pl.BlockSpec((2, KV_WIDTH), const),
                  pl.BlockSpec((NSA_WIDTH, NSA_WIDTH), const)],
        out_specs=[pl.BlockSpec((tm, KV_WIDTH), row),
                   pl.BlockSpec((tm, KV_WIDTH), row),
                   pl.BlockSpec((tm, _RET_COLS), row),
                   pl.BlockSpec((None, G, tiles, HEAD_DIM, GQ), lambda i: (i // spr, 0, i % spr, 0, 0)),
                   pl.BlockSpec((None, G, tm, 2 * HEAD_DIM), lambda i: (i // spr, 0, i % spr, 0)),
                   pl.BlockSpec((None, G, tm, 2 * HEAD_DIM), lambda i: (i // spr, 0, i % spr, 0)),
                   pl.BlockSpec((None, G, V_ROWS, tm), lambda i: (i // spr, 0, 0, i % spr)),
                   pl.BlockSpec((None, G, V_ROWS, tm), lambda i: (i // spr, 0, 0, i % spr)),
                   pl.BlockSpec((None, tiles, 32, Q_BLOCK), lambda i: (i // spr, i % spr, 0, 0))],
        out_shape=[jax.ShapeDtypeStruct((n, KV_WIDTH), F32),
                   jax.ShapeDtypeStruct((n, KV_WIDTH), F32),
                   jax.ShapeDtypeStruct((n, _RET_COLS), F32),
                   jax.ShapeDtypeStruct((B, G, nq, HEAD_DIM, GQ), BF16),
                   jax.ShapeDtypeStruct((B, G, T, 2 * HEAD_DIM), BF16),
                   jax.ShapeDtypeStruct((B, G, T, 2 * HEAD_DIM), BF16),
                   jax.ShapeDtypeStruct((B, G, V_ROWS, T), BF16),
                   jax.ShapeDtypeStruct((B, G, V_ROWS, T), BF16),
                   jax.ShapeDtypeStruct((B, nq, 32, Q_BLOCK), F32)],
        compiler_params=_params(1),
        name="inproj",
    )(x2, g, w, jnp.tile(qg, (1, NSA_HEADS)), jnp.tile(kg, (1, NSA_KV_HEADS)), _group_ones())


def _compress_kernel(kc_ref, vc_ref, wk_ref, wv_ref, pk_ref, pv_ref, kg_ref, kcmp_ref, vcT_ref):
    ncp = kc_ref.shape[0] // CMP_STRIDE

    def comp(a_ref, w_ref, p_ref):
        lo = jnp.zeros((ncp, KV_WIDTH), F32)
        hi = jnp.zeros((ncp, KV_WIDTH), F32)
        for l in range(CMP_STRIDE):
            a = a_ref[pl.ds(l, ncp, stride=CMP_STRIDE), :]
            lo += _dot((a + p_ref[0, l:l + 1, :]).astype(BF16), w_ref[0, l])
            hi += _dot((a + p_ref[1, l:l + 1, :]).astype(BF16), w_ref[1, l])
        return lo + pltpu.roll(hi, ncp - 1, 0)

    k = comp(kc_ref, wk_ref, pk_ref)
    v = comp(vc_ref, wv_ref, pv_ref).T
    cend = lax.broadcasted_iota(jnp.int32, (ncp, HEAD_DIM), 0) * CMP_STRIDE + (CMP_LEN - 1)
    col = lax.broadcasted_iota(jnp.int32, (ncp, HEAD_DIM), 1)
    kpos = jnp.where(col < 3, cend // SLC_LEN, jnp.where(col < 6, cend % SLC_LEN, 0)).astype(F32)
    for g in range(NSA_KV_HEADS):
        sl = slice(g * HEAD_DIM, (g + 1) * HEAD_DIM)
        kcmp_ref[g] = jnp.concatenate([_rms(k[:, sl], kg_ref[...]), kpos], axis=1).astype(BF16)
        vcT_ref[g] = v[sl, :].astype(BF16)


def _compress(kc, vc, wk, wv, pk, pv, kg, B, T):
    ncp = T // CMP_STRIDE
    G = NSA_KV_HEADS
    const4 = lambda b: (0, 0, 0, 0)
    const3 = lambda b: (0, 0, 0)
    const2 = lambda b: (0, 0)
    return pl.pallas_call(
        _compress_kernel,
        grid=(B,),
        in_specs=[pl.BlockSpec((T, KV_WIDTH), lambda b: (b, 0)),
                  pl.BlockSpec((T, KV_WIDTH), lambda b: (b, 0)),
                  pl.BlockSpec((2, CMP_STRIDE, KV_WIDTH, KV_WIDTH), const4),
                  pl.BlockSpec((2, CMP_STRIDE, KV_WIDTH, KV_WIDTH), const4),
                  pl.BlockSpec((2, CMP_STRIDE, KV_WIDTH), const3),
                  pl.BlockSpec((2, CMP_STRIDE, KV_WIDTH), const3),
                  pl.BlockSpec((1, HEAD_DIM), const2)],
        out_specs=[pl.BlockSpec((None, G, ncp, 2 * HEAD_DIM), lambda b: (b, 0, 0, 0)),
                   pl.BlockSpec((None, G, HEAD_DIM, ncp), lambda b: (b, 0, 0, 0))],
        out_shape=[jax.ShapeDtypeStruct((B, G, ncp, 2 * HEAD_DIM), BF16),
                   jax.ShapeDtypeStruct((B, G, HEAD_DIM, ncp), BF16)],
        compiler_params=_params(1),
        name="nsa_compress",
    )(kc, vc, wk, wv, pk, pv, kg)


def _compress_weights(w, pos):
    G = NSA_KV_HEADS
    w4 = w.reshape(2, CMP_STRIDE, HEAD_DIM, HEAD_DIM)
    eye = jnp.eye(G, dtype=w.dtype)
    wbd = jnp.einsum('hlde,gk->hlgdke', w4, eye).reshape(2, CMP_STRIDE, KV_WIDTH, KV_WIDTH)
    p = pos.reshape(2, CMP_STRIDE, 1, HEAD_DIM)
    p = jnp.broadcast_to(p, (2, CMP_STRIDE, G, HEAD_DIM)).reshape(2, CMP_STRIDE, KV_WIDTH)
    return wbd.astype(BF16), p


def _split3(x):
    hi = x.astype(BF16)
    r = x - hi.astype(F32)
    mid = r.astype(BF16)
    lo = (r - mid.astype(F32)).astype(BF16)
    return hi, mid, lo


def _nsa_cmp_kernel(qT_ref, qaug_ref, kc_ref, vcT_ref, ovT_ref, ocmp_ref, sel_ref, flag_ref, *, n_sel):
    ncp = kc_ref.shape[0]
    ns = ovT_ref.shape[0]
    tiles = range(CMP_TILES)
    i0 = pl.program_id(2) * CMP_TILES
    lane = lax.broadcasted_iota(jnp.int32, (1, GQ), 1)
    q = [jnp.concatenate([qT_ref[u], qaug_ref[...]], axis=0) for u in tiles]
    t_row = [(i0 + u) * Q_BLOCK + (lane & (Q_BLOCK - 1)) for u in tiles]
    has_cmp = [(t_row[u] >= CMP_LEN - 1).astype(F32) for u in tiles]
    tq = [(i0 + u) * Q_BLOCK + lax.broadcasted_iota(jnp.int32, (1, Q_BLOCK), 1) for u in tiles]
    cur = [tq[u] // SLC_LEN for u in tiles]

    def prefix(rows):
        nsk = rows * CMP_STRIDE // SLC_LEN
        tail0 = max(rows - CMP_TAIL, 0)
        kc = kc_ref[0:rows, :]
        s = [_dot(kc, q[u]) for u in tiles]
        cend = (lax.broadcasted_iota(jnp.int32, (rows - tail0, 1), 0) + tail0) * CMP_STRIDE + (CMP_LEN - 1)
        tail = [jnp.where(t_row[u] >= cend, s[u][tail0:], NEG) for u in tiles]
        s = [jnp.concatenate([s[u][0:tail0], tail[u]], axis=0) if tail0 else tail[u] for u in tiles]
        m = [jnp.max(s[u], axis=0, keepdims=True) for u in tiles]
        e = [jnp.exp2(s[u] - m[u]) for u in tiles]
        p = [e[u] * (has_cmp[u] / jnp.sum(e[u], axis=0, keepdims=True)) for u in tiles]
        vc = vcT_ref[:, 0:rows]
        for u in tiles:
            ocmp_ref[u] = _dot(vc, p[u].astype(BF16))

        ps = [p[u][:, 0:Q_BLOCK] for u in tiles]
        for r in range(1, NSA_GROUP):
            ps = [ps[u] + p[u][:, r * Q_BLOCK:(r + 1) * Q_BLOCK] for u in tiles]
        ov = ovT_ref[0:nsk, 0:rows]
        split = [_split3(ps[u]) for u in tiles]
        imp = [_dot(ov, split[u][0]) + _dot(ov, split[u][1]) + _dot(ov, split[u][2]) for u in tiles]

        blk = lax.broadcasted_iota(jnp.int32, (nsk, 1), 0)
        forced = [(blk == 0) | (blk == cur[u]) | (blk == cur[u] - 1) for u in tiles]
        valid = [blk * SLC_LEN <= tq[u] for u in tiles]
        imp = [jnp.where(forced[u], -3e38, jnp.where(valid[u], imp[u], -BIG)) for u in tiles]
        blk_f = blk.astype(F32)
        sel = [forced[u].astype(F32) for u in tiles]
        for _ in range(n_sel - N_FORCED):
            mx = [jnp.max(imp[u], axis=0, keepdims=True) for u in tiles]
            idx = [jnp.min(jnp.where(imp[u] == mx[u], blk_f, float(ns)), axis=0, keepdims=True) for u in tiles]
            pick = [blk_f == idx[u] for u in tiles]
            sel = [jnp.where(pick[u], 1.0, sel[u]) for u in tiles]
            imp = [jnp.where(pick[u], -3e38, imp[u]) for u in tiles]
        ones = jnp.ones((8, Q_BLOCK), BF16)
        for u in tiles:
            sel_ref[u, 0:nsk, :] = sel[u]
            cnt = _dot_nt(ones, sel[u].astype(BF16))
            flag_ref[u, :, 0:nsk] = (cnt > 0).astype(jnp.int32)
            if nsk < ns:
                sel_ref[u, nsk:, :] = jnp.zeros((ns - nsk, Q_BLOCK), F32)
                flag_ref[u, :, nsk:] = jnp.zeros((8, ns - nsk), jnp.int32)

    n_variants = ncp // CMP_CHUNK
    last = i0 + CMP_TILES - 1
    variant = (last * (Q_BLOCK // CMP_STRIDE) + (Q_BLOCK // CMP_STRIDE - 2)) // CMP_CHUNK
    for k in range(n_variants):
        pl.when(variant == k)(functools.partial(prefix, (k + 1) * CMP_CHUNK))


def _nsa_cmp(qT, qaug, kcmp, vcT, ovT, B, T):
    G = NSA_KV_HEADS
    nq = T // Q_BLOCK
    ncp = T // CMP_STRIDE
    ns = T // SLC_LEN
    n_sel = min(SLC_TOPK, ns)
    assert ncp % CMP_CHUNK == 0 and n_sel > N_FORCED and nq % CMP_TILES == 0
    tile = lambda b, g, i: (b, g, i, 0, 0)
    return pl.pallas_call(
        functools.partial(_nsa_cmp_kernel, n_sel=n_sel),
        grid=(B, G, nq // CMP_TILES),
        in_specs=[pl.BlockSpec((None, None, CMP_TILES, HEAD_DIM, GQ), tile),
                  pl.BlockSpec((None, HEAD_DIM, GQ), lambda b, g, i: (g, 0, 0)),
                  pl.BlockSpec((None, None, ncp, 2 * HEAD_DIM), lambda b, g, i: (b, g, 0, 0)),
                  pl.BlockSpec((None, None, HEAD_DIM, ncp), lambda b, g, i: (b, g, 0, 0)),
                  pl.BlockSpec((ns, ncp), lambda b, g, i: (0, 0))],
        out_specs=[pl.BlockSpec((None, None, CMP_TILES, HEAD_DIM, GQ), tile),
                   pl.BlockSpec((None, None, CMP_TILES, ns, Q_BLOCK), tile),
                   pl.BlockSpec((None, None, CMP_TILES, 8, ns), tile)],
        out_shape=[jax.ShapeDtypeStruct((B, G, nq, HEAD_DIM, GQ), F32),
                   jax.ShapeDtypeStruct((B, G, nq, ns, Q_BLOCK), F32),
                   jax.ShapeDtypeStruct((B, G, nq, 8, ns), jnp.int32)],
        compiler_params=_params(3),
        name="nsa_cmp",
    )(qT, qaug, kcmp, vcT, ovT)


def _nsa_main_kernel(list_ref, cnt_ref, qT_ref, qaug_ref, ks_ref, vsT_ref, kw_ref, vwT_ref, sel_ref, gT_ref, ocmp_ref,
                     lowb_ref, causb_ref, out_ref, m_sc, acc_sc, win_sc):
    b, g = pl.program_id(0), pl.program_id(1)
    tiles = range(MAIN_TILES)
    i = [pl.program_id(2) * MAIN_TILES + u for u in tiles]
    tile_id = [(b * pl.num_programs(1) + g) * (pl.num_programs(2) * MAIN_TILES) + i[u] for u in tiles]
    n_steps = sel_ref.shape[1] // 2
    q = [jnp.concatenate([qT_ref[u], qaug_ref[...]], axis=0) for u in tiles]
    k0 = [pl.multiple_of(i[u] * Q_BLOCK, Q_BLOCK) for u in tiles]

    def sel_bias(u, j, valid):
        def row(r):
            picked = (sel_ref[u, pl.ds(r, 1), :] > 0.5) & valid
            return jnp.concatenate([jnp.where(picked, 0.0, NEG)] * NSA_GROUP, axis=1)
        return row(2 * j), row(2 * j + 1)

    def add_sel_bias(s, ba, bb):
        return jnp.concatenate([s[0:SLC_LEN] + ba, s[SLC_LEN:] + bb], axis=0)

    lowb = jnp.concatenate([lowb_ref[...]] * NSA_GROUP, axis=1)
    causb = jnp.concatenate([causb_ref[...]] * NSA_GROUP, axis=1)

    bias_d = [sel_bias(u, i[u], True) for u in tiles]
    sd = [_dot(ks_ref[pl.ds(k0[u], KEY_STEP), :], q[u]) for u in tiles]
    sw = [_dot(kw_ref[pl.ds(k0[u], WIN_KEYS), :], q[u]) for u in tiles]
    sd = [add_sel_bias(sd[u], *bias_d[u]) + causb for u in tiles]
    sw = [jnp.concatenate([sw[u][0:Q_BLOCK] + lowb, sw[u][Q_BLOCK:WIN], sw[u][WIN:] + causb], axis=0) for u in tiles]
    md = [jnp.max(sd[u], axis=0, keepdims=True) for u in tiles]
    mw = [jnp.max(sw[u], axis=0, keepdims=True) for u in tiles]
    accd = [_dot(vsT_ref[:, pl.ds(k0[u], KEY_STEP)], jnp.exp2((sd[u] - md[u]).astype(BF16))) for u in tiles]
    ow = [_dot(vwT_ref[:, pl.ds(k0[u], WIN_KEYS)], jnp.exp2((sw[u] - mw[u]).astype(BF16))) for u in tiles]
    for u in tiles:
        m_sc[u] = md[u]
        acc_sc[u] = accd[u]
        win_sc[u] = ow[u][0:HEAD_DIM] / ow[u][HEAD_DIM:HEAD_DIM + 1]

    def scores(u, t):
        ks, vs, biases = [], [], []
        for x in range(STEP_GROUP):
            j = list_ref[tile_id[u] * n_steps + t * STEP_GROUP + x]
            valid = j >= 0
            j = jnp.maximum(j, 0)
            kj = pl.multiple_of(j * KEY_STEP, KEY_STEP)
            ks.append(ks_ref[pl.ds(kj, KEY_STEP), :])
            vs.append(vsT_ref[:, pl.ds(kj, KEY_STEP)])
            biases.append(sel_bias(u, j, valid))
        s = _dot(jnp.concatenate(ks, axis=0), q[u])
        s = jnp.concatenate([add_sel_bias(s[x * KEY_STEP:(x + 1) * KEY_STEP], *biases[x])
                             for x in range(STEP_GROUP)], axis=0)
        return s, jnp.max(s, axis=0, keepdims=True), jnp.concatenate(vs, axis=1)

    def accumulate(u, s, smax, vcat):
        m_old = m_sc[u]
        m_new = jnp.maximum(m_old, smax)
        alpha = jnp.exp2(m_old - m_new)
        acc_sc[u] = alpha * acc_sc[u] + _dot(vcat, jnp.exp2((s - m_new).astype(BF16)))
        m_sc[u] = m_new

    def run(work, t, carry):
        staged = [(u, scores(u, t * mult + off)) for (u, mult, off) in work]
        for u, args in staged:
            accumulate(u, *args)
        return carry

    for u0 in range(0, MAIN_TILES, LOOP_TILES):
        us = range(u0, u0 + LOOP_TILES)
        n_groups = functools.reduce(
            jnp.maximum, [(cnt_ref[tile_id[u]] + (STEP_GROUP - 1)) // STEP_GROUP for u in us])
        lax.fori_loop(0, n_groups // 2, functools.partial(run, [(u, 2, off) for off in (0, 1) for u in us]), 0)
        lax.fori_loop(n_groups // 2 * 2, n_groups, functools.partial(run, [(u, 1, 0) for u in us]), 0)

    def gate(u, k):
        rows = [gT_ref[u, pl.ds(g * (NSA_GROUP * 3) + r * 3 + k, 1), :] for r in range(NSA_GROUP)]
        return jnp.concatenate(rows, axis=1)

    o_slc = [acc_sc[u, 0:HEAD_DIM, :] / acc_sc[u, HEAD_DIM:HEAD_DIM + 1, :] for u in tiles]
    o = [gate(u, 0) * ocmp_ref[u] + gate(u, 1) * o_slc[u] + gate(u, 2) * win_sc[u] for u in tiles]
    o = [jnp.concatenate([o[u], jnp.zeros_like(o[u])], axis=0) for u in tiles]
    for r in range(NSA_GROUP):
        ot = [o[u][:, r * Q_BLOCK:(r + 1) * Q_BLOCK].T[:, 0:HEAD_DIM] for u in tiles]
        for u in tiles:
            out_ref[u * Q_BLOCK:(u + 1) * Q_BLOCK, r * HEAD_DIM:(r + 1) * HEAD_DIM] = ot[u]


def _nsa_steps_kernel(flagT_ref, pairT_ref, list_ref, cnt_ref, *, nq):
    n_steps, nt = list_ref.shape
    need = _dot(pairT_ref[...], flagT_ref[...].astype(BF16)) > 0
    step = lax.broadcasted_iota(jnp.int32, (n_steps, 1), 0)
    own = lax.broadcasted_iota(jnp.int32, (1, nt), 1) % nq
    need = need & (step < own)
    need_f = need.astype(F32)
    earlier = (lax.broadcasted_iota(jnp.int32, (n_steps, n_steps), 1) < step).astype(BF16)
    slot = _dot(earlier, need_f.astype(BF16))
    total = jnp.sum(need_f, axis=0, keepdims=True)
    cnt_ref[...] = jnp.broadcast_to(total, cnt_ref.shape).astype(jnp.int32)
    step_f = step.astype(F32)
    for p in range(n_steps):
        val = jnp.sum(jnp.where(need & (slot == p), step_f, 0.0), axis=0, keepdims=True)
        list_ref[p:p + 1, :] = jnp.where(total > p, val, -1.0).astype(jnp.int32)


def _nsa_steps(flags, nq):
    nt, ns = flags.shape
    n_steps = ns // 2
    pairT = jnp.asarray(np.arange(n_steps)[:, None] == np.arange(ns)[None, :] // 2, BF16)
    lists, counts = pl.pallas_call(
        functools.partial(_nsa_steps_kernel, nq=nq),
        out_shape=[jax.ShapeDtypeStruct((n_steps, nt), jnp.int32), jax.ShapeDtypeStruct((8, nt), jnp.int32)],
        name="nsa_steps",
    )(flags.T.astype(F32), pairT)
    return lists.T.reshape(-1), counts[0]


def _nsa_main(lists, counts, qT, qaug, ks, vsT, kw, vwT, sel, gT, ocmp, lowb, causb, B, T):
    G = NSA_KV_HEADS
    nq = T // Q_BLOCK
    ns = T // SLC_LEN
    whole = lambda b, g, i, *_: (b, g, 0, 0)
    tile = lambda b, g, i, *_: (b, g, i, 0, 0)
    const = lambda b, g, i, *_: (0, 0)
    grid_spec = pltpu.PrefetchScalarGridSpec(
        num_scalar_prefetch=2,
        grid=(B, G, nq // MAIN_TILES),
        in_specs=[pl.BlockSpec((None, None, MAIN_TILES, HEAD_DIM, GQ), tile),
                  pl.BlockSpec((None, HEAD_DIM, GQ), lambda b, g, i, *_: (g, 0, 0)),
                  pl.BlockSpec((None, None, T, 2 * HEAD_DIM), whole),
                  pl.BlockSpec((None, None, V_ROWS, T), whole),
                  pl.BlockSpec((None, None, T + WIN, 2 * HEAD_DIM), whole),
                  pl.BlockSpec((None, None, V_ROWS, T + WIN), whole),
                  pl.BlockSpec((None, None, MAIN_TILES, ns, Q_BLOCK), tile),
                  pl.BlockSpec((None, MAIN_TILES, 32, Q_BLOCK), lambda b, g, i, *_: (b, i, 0, 0)),
                  pl.BlockSpec((None, None, MAIN_TILES, HEAD_DIM, GQ), tile),
                  pl.BlockSpec((Q_BLOCK, Q_BLOCK), const),
                  pl.BlockSpec((Q_BLOCK, Q_BLOCK), const)],
        out_specs=pl.BlockSpec((None, MAIN_TILES * Q_BLOCK, NSA_GROUP * HEAD_DIM), lambda b, g, i, *_: (b, i, g)),
        scratch_shapes=[pltpu.VMEM((MAIN_TILES, 1, GQ), F32), pltpu.VMEM((MAIN_TILES, V_ROWS, GQ), F32),
                        pltpu.VMEM((MAIN_TILES, HEAD_DIM, GQ), F32)],
    )
    return pl.pallas_call(
        _nsa_main_kernel,
        grid_spec=grid_spec,
        out_shape=jax.ShapeDtypeStruct((B, T, NSA_WIDTH), F32),
        compiler_params=_params(3),
        name="nsa_main",
    )(lists, counts, qT, qaug, ks, vsT, kw, vwT, sel, gT, ocmp, lowb, causb)


def _ret_kernel(p_ref, decay_ref, xi_ref, zeta_ref, gch_ref, ng_ref, ones_ref, out_ref, state_ref):
    @pl.when(pl.program_id(0) == 0)
    def _():
        state_ref[...] = jnp.zeros(state_ref.shape, F32)

    rows = range(p_ref.shape[0])
    kw = RET_HEADS * RET_DK
    p = [p_ref[b] for b in rows]
    rq = [p[b][:, 0:kw] * (RET_DK ** -0.5) for b in rows]
    rk = [p[b][:, kw:2 * kw] for b in rows]
    rkT = [rk[b].T for b in rows]
    rv = [p[b][:, 2 * kw:2 * kw + RET_WIDTH] for b in rows]
    xi = xi_ref[...]
    outs = [[] for _ in rows]
    for h in range(RET_HEADS):
        dk = slice(h * RET_DK, (h + 1) * RET_DK)
        dv = slice(h * RET_DV, (h + 1) * RET_DV)
        st = [state_ref[b, h] for b in rows]
        inner = [_dot_nt(rq[b][:, dk], rk[b][:, dk]) * decay_ref[h] for b in rows]
        o = [_dot(inner[b], rv[b][:, dv]) + _dot(rq[b][:, dk], st[b]) * xi[:, h:h + 1] for b in rows]
        for b in rows:
            state_ref[b, h] = (st[b] * gch_ref[h:h + 1, 0:1]
                               + _dot(rkT[b][dk, :] * zeta_ref[h:h + 1, :], rv[b][:, dv]))
            outs[b].append(o[b])
    normed = [_group_rms(jnp.concatenate(outs[b], axis=1), ng_ref[...], ones_ref) for b in rows]
    for b in rows:
        rg = p[b][:, 2 * kw + RET_WIDTH:2 * kw + 2 * RET_WIDTH]
        out_ref[b] = normed[b] * (rg * jax.nn.sigmoid(rg))


def _ret_consts():
    H, C = RET_HEADS, RET_CHUNK
    log_g = np.log1p(-np.exp2(-5.0 - np.arange(H, dtype=np.float64)))
    idx = np.arange(C, dtype=np.float64)
    diff = idx[:, None] - idx[None, :]
    decay = np.where(diff >= 0, np.exp(np.maximum(diff, 0.0) * log_g[:, None, None]), 0.0)
    zeta = np.exp((C - 1 - idx) * log_g[:, None])
    xi = np.exp((idx + 1) * log_g[:, None]).T
    g_chunk = np.broadcast_to(np.exp(C * log_g)[:, None], (H, LANES))
    return tuple(jnp.asarray(a, F32) for a in (decay, xi, zeta, g_chunk))


def _retention(pret, ng, B, T):
    nch = T // RET_CHUNK
    decay, xi, zeta, gch = _ret_consts()
    c2 = lambda c: (0, 0)
    out = pl.pallas_call(
        _ret_kernel,
        grid=(nch,),
        in_specs=[pl.BlockSpec((B, RET_CHUNK, _RET_COLS), lambda c: (0, c, 0)),
                  pl.BlockSpec((RET_HEADS, RET_CHUNK, RET_CHUNK), lambda c: (0, 0, 0)),
                  pl.BlockSpec((RET_CHUNK, RET_HEADS), c2),
                  pl.BlockSpec((RET_HEADS, RET_CHUNK), c2),
                  pl.BlockSpec((RET_HEADS, LANES), c2),
                  pl.BlockSpec((1, RET_WIDTH), c2),
                  pl.BlockSpec((RET_WIDTH, RET_WIDTH), c2)],
        out_specs=pl.BlockSpec((B, RET_CHUNK, RET_WIDTH), lambda c: (0, c, 0)),
        out_shape=jax.ShapeDtypeStruct((B, T, RET_WIDTH), F32),
        scratch_shapes=[pltpu.VMEM((B, RET_HEADS, RET_DK, RET_DV), F32)],
        compiler_params=_params(1),
        name="retention",
    )(pret.reshape(B, T, _RET_COLS), decay, xi, zeta, gch, ng, _group_ones())
    return out.reshape(B * T, RET_WIDTH)


def _mixer_residual(x_ref, nsa_ref, ret_ref, wo_ref):
    return (x_ref[...] + _dot(nsa_ref[...].astype(BF16), wo_ref[0:NSA_WIDTH, :])
            + _dot(ret_ref[...].astype(BF16), wo_ref[NSA_WIDTH:, :]))


def _mixer_specs(tm, d, row, const):
    return [pl.BlockSpec((tm, d), row), pl.BlockSpec((tm, NSA_WIDTH), row), pl.BlockSpec((tm, RET_WIDTH), row),
            pl.BlockSpec((NSA_WIDTH + RET_WIDTH, d), const)]


def _ffn_kernel(x_ref, nsa_ref, ret_ref, wo_ref, g_ref, wg_ref, wu_ref, wd_ref, o_ref, h_sc):
    f = pl.program_id(1)

    @pl.when(f == 0)
    def _():
        x = _mixer_residual(x_ref, nsa_ref, ret_ref, wo_ref)
        h_sc[...] = _rms(x, g_ref[...]).astype(BF16)
        o_ref[...] = x

    h = h_sc[...]
    a = _dot(h, wg_ref[...])
    act = (a * jax.nn.sigmoid(a) * _dot(h, wu_ref[...])).astype(BF16)
    o_ref[...] += _dot(act, wd_ref[...])


def _ffn(x2, nsa, ret, wo, g, wg, wu, wd, tm=1024, fc=1408):
    n, d = x2.shape
    dff = wg.shape[1]
    return pl.pallas_call(
        _ffn_kernel,
        grid=(n // tm, dff // fc),
        in_specs=_mixer_specs(tm, d, lambda i, f: (i, 0), lambda i, f: (0, 0)) + [
            pl.BlockSpec((1, d), lambda i, f: (0, 0)),
            pl.BlockSpec((d, fc), lambda i, f: (0, f)),
            pl.BlockSpec((d, fc), lambda i, f: (0, f)),
            pl.BlockSpec((fc, d), lambda i, f: (f, 0))],
        out_specs=pl.BlockSpec((tm, d), lambda i, f: (i, 0)),
        out_shape=jax.ShapeDtypeStruct((n, d), F32),
        scratch_shapes=[pltpu.VMEM((tm, d), BF16)],
        compiler_params=_params(2),
        name="ffn_dense",
    )(x2, nsa, ret, wo, g, wg, wu, wd)


def _router_kernel(x_ref, nsa_ref, ret_ref, wo_ref, g_ref, r_ref, rb_ref, tri_ref,
                   x1_ref, h_ref, rank_ref, comb_ref, rankT_ref, cnt_ref):
    x = _mixer_residual(x_ref, nsa_ref, ret_ref, wo_ref)
    x1_ref[...] = x
    h = _rms(x, g_ref[...])
    h_ref[...] = h.astype(BF16)
    hh, hm, hl = _split3(h)
    rh, rm, rl = _split3(r_ref[...])
    logits = (_dot(hh, rh) + (_dot(hh, rm) + _dot(hm, rh)) + (_dot(hh, rl) + _dot(hm, rm) + _dot(hl, rh))
              + rb_ref[...])
    lane = lax.broadcasted_iota(jnp.int32, logits.shape, 1).astype(F32)
    logits = jnp.where(lane < N_EXPERTS, logits, NEG)
    m1 = jnp.max(logits, axis=1, keepdims=True)
    i1 = jnp.min(jnp.where(logits == m1, lane, float(LANES)), axis=1, keepdims=True)
    l2 = jnp.where(lane == i1, NEG, logits)
    m2 = jnp.max(l2, axis=1, keepdims=True)
    i2 = jnp.min(jnp.where(l2 == m2, lane, float(LANES)), axis=1, keepdims=True)
    e2 = jnp.exp(m2 - m1)
    w1 = 1.0 / (1.0 + e2)
    w2 = e2 / (1.0 + e2)
    use1, use2 = lane == i1, lane == i2
    comb_ref[...] = jnp.where(use1, w1, 0.0) + jnp.where(use2, w2, 0.0)
    use = (use1 | use2).astype(F32)
    rank = jnp.where(use > 0, _dot(tri_ref[...], use.astype(BF16)), -1.0)
    rank_ref[...] = rank
    rankT_ref[...] = rank.T[0:N_EXPERTS, :]
    cnt_ref[...] = jnp.broadcast_to(jnp.sum(use, axis=0, keepdims=True), cnt_ref.shape).astype(jnp.int32)


def _router(x2, nsa, ret, wo, g, router, rb, tm):
    n, d = x2.shape
    nt = n // tm
    tri = jnp.asarray(np.arange(tm)[:, None] > np.arange(tm)[None, :], BF16)
    rpad = jnp.zeros((d, LANES), F32).at[:, :N_EXPERTS].set(router)
    rbpad = jnp.zeros((1, LANES), F32).at[0, :N_EXPERTS].set(rb)
    c2 = lambda i: (0, 0)
    row = lambda i: (i, 0)
    return pl.pallas_call(
        _router_kernel,
        grid=(nt,),
        in_specs=_mixer_specs(tm, d, row, c2) + [
            pl.BlockSpec((1, d), c2),
            pl.BlockSpec((d, LANES), c2),
            pl.BlockSpec((1, LANES), c2),
            pl.BlockSpec((tm, tm), c2)],
        out_specs=[pl.BlockSpec((tm, d), row),
                   pl.BlockSpec((tm, d), row),
                   pl.BlockSpec((tm, LANES), row),
                   pl.BlockSpec((tm, LANES), row),
                   pl.BlockSpec((N_EXPERTS, tm), lambda i: (0, i)),
                   pl.BlockSpec((None, 8, LANES), lambda i: (i, 0, 0))],
        out_shape=[jax.ShapeDtypeStruct((n, d), F32),
                   jax.ShapeDtypeStruct((n, d), BF16),
                   jax.ShapeDtypeStruct((n, LANES), F32),
                   jax.ShapeDtypeStruct((n, LANES), F32),
                   jax.ShapeDtypeStruct((N_EXPERTS, n), F32),
                   jax.ShapeDtypeStruct((nt, 8, LANES), jnp.int32)],
        compiler_params=_params(1),
        name="moe_router",
    )(x2, nsa, ret, wo, g, rpad, rbpad, tri)


MOE_SUB = 144
MOE_MOVE = 2 * MOE_SUB


def _moe_kernel(cnt_ref, h_ref, rankT_ref, rank_ref, comb_ref, wg_ref, wu_ref, wd_ref, x_ref, o_ref, hc_sc, oacc_sc):
    t, e, f = pl.program_id(0), pl.program_id(1), pl.program_id(2)
    nf = pl.num_programs(2)
    tm = h_ref.shape[0]
    nsub = (cnt_ref[t * N_EXPERTS + e] + (MOE_SUB - 1)) // MOE_SUB
    nmove = (nsub + 1) // 2

    @pl.when((e == 0) & (f == 0))
    def _():
        o_ref[...] = x_ref[...]

    @pl.when(f == 0)
    def _():
        rank_row = rankT_ref[...]

        def gather(s, c):
            r0 = pl.multiple_of(s * MOE_MOVE, MOE_MOVE)
            rows = (lax.broadcasted_iota(jnp.int32, (MOE_MOVE, 1), 0) + r0).astype(F32)
            onehot = (rows == rank_row).astype(BF16)
            hc_sc[pl.ds(r0, MOE_MOVE), :] = _dot(onehot, h_ref[...]).astype(BF16)
            oacc_sc[pl.ds(r0, MOE_MOVE), :] = jnp.zeros((MOE_MOVE, oacc_sc.shape[1]), F32)
            return c

        lax.fori_loop(0, nmove, gather, 0)

    def expert(n_rows, s, c):
        r0 = pl.multiple_of(s * n_rows, n_rows)
        rows = hc_sc[pl.ds(r0, n_rows), :]
        a = _dot(rows, wg_ref[...])
        act = (a * jax.nn.sigmoid(a) * _dot(rows, wu_ref[...])).astype(BF16)
        oacc_sc[pl.ds(r0, n_rows), :] += _dot(act, wd_ref[...])
        return c

    lax.fori_loop(0, nsub // 2, functools.partial(expert, MOE_MOVE), 0)
    lax.fori_loop(nsub // 2 * 2, nsub, functools.partial(expert, MOE_SUB), 0)

    @pl.when(f == nf - 1)
    def _():
        is_e = lax.broadcasted_iota(jnp.int32, (1, LANES), 1) == e
        rank_col = jnp.sum(jnp.where(is_e, rank_ref[...], 0.0), axis=1, keepdims=True)
        comb_col = jnp.sum(jnp.where(is_e, comb_ref[...], 0.0), axis=1, keepdims=True)

        def scatter(s, c):
            r0 = pl.multiple_of(s * MOE_MOVE, MOE_MOVE)
            cols = (lax.broadcasted_iota(jnp.int32, (1, MOE_MOVE), 1) + r0).astype(F32)
            onehot = (rank_col == cols).astype(BF16)
            y = _dot(onehot, oacc_sc[pl.ds(r0, MOE_MOVE), :].astype(BF16))
            o_ref[...] += comb_col * y
            return c

        lax.fori_loop(0, nmove, scatter, 0)


def _moe(counts, h, rankT, rank, comb, wg, wu, wd, x2, tm, fc=1408):
    n, d = x2.shape
    dff = wg.shape[2]
    rows_cap = pl.cdiv(pl.cdiv(tm, MOE_SUB), 2) * MOE_MOVE
    grid_spec = pltpu.PrefetchScalarGridSpec(
        num_scalar_prefetch=1,
        grid=(n // tm, N_EXPERTS, dff // fc),
        in_specs=[pl.BlockSpec((tm, d), lambda t, e, f, c: (t, 0)),
                  pl.BlockSpec((None, 1, tm), lambda t, e, f, c: (e, 0, t)),
                  pl.BlockSpec((tm, LANES), lambda t, e, f, c: (t, 0)),
                  pl.BlockSpec((tm, LANES), lambda t, e, f, c: (t, 0)),
                  pl.BlockSpec((None, d, fc), lambda t, e, f, c: (e, 0, f)),
                  pl.BlockSpec((None, d, fc), lambda t, e, f, c: (e, 0, f)),
                  pl.BlockSpec((None, fc, d), lambda t, e, f, c: (e, f, 0)),
                  pl.BlockSpec((tm, d), lambda t, e, f, c: (t, 0))],
        out_specs=pl.BlockSpec((tm, d), lambda t, e, f, c: (t, 0)),
        scratch_shapes=[pltpu.VMEM((rows_cap, d), BF16), pltpu.VMEM((rows_cap, d), F32)],
    )
    return pl.pallas_call(
        _moe_kernel,
        grid_spec=grid_spec,
        out_shape=jax.ShapeDtypeStruct((n, d), F32),
        compiler_params=_params(3),
        name="moe_experts",
    )(counts, h, rankT.reshape(N_EXPERTS, 1, n), rank, comb, wg, wu, wd, x2)


def _permute_w_in(w):
    o = np.cumsum((0, NSA_WIDTH) + (KV_WIDTH,) * 6 + (3 * NSA_HEADS,))
    q, kc, vc, ks, vs, kw, vw, gts = (w[:, o[k]:o[k + 1]] for k in range(8))
    ret = w[:, o[8]:]
    pad = jnp.zeros((w.shape[0], LANES - 3 * NSA_HEADS), w.dtype)
    return jnp.concatenate([q, ks, kw, vs, vw, kc, vc, gts, pad, ret], axis=1).astype(BF16)


def _nsa_consts(T):
    ncp = T // CMP_STRIDE
    ns = T // SLC_LEN
    cs = np.arange(ncp) * CMP_STRIDE
    ss = np.arange(ns) * SLC_LEN
    ov = np.clip(np.minimum(cs[None, :] + CMP_LEN, ss[:, None] + SLC_LEN) - np.maximum(cs[None, :], ss[:, None]), 0, None)
    ovT = (ov.astype(np.float32) / CMP_LEN)
    ovT[:, ncp - 1] = 0.0
    h = np.arange(NSA_HEADS).reshape(NSA_KV_HEADS, NSA_GROUP) + 1
    slopes = np.exp2(-8.0 * h / NSA_HEADS).astype(np.float32)
    slopes = np.repeat(slopes, Q_BLOCK, axis=1)
    parts, rest = [], np.float64(LOG2E)
    for _ in range(3):
        part = np.float64(np.asarray(rest).astype(BF16))
        parts.append(part)
        rest = rest - part
    qaug = np.zeros((NSA_KV_HEADS, HEAD_DIM, GQ), np.float32)
    for k, part in enumerate(parts):
        qaug[:, k, :] = part * SLC_LEN * slopes
        qaug[:, 3 + k, :] = part * slopes
    kq = np.arange(Q_BLOCK)[:, None] - np.arange(Q_BLOCK)[None, :]
    causb = np.where(kq <= 0, 0.0, NEG).astype(np.float32)
    lowb = np.where(kq > 0, 0.0, NEG).astype(np.float32)
    return jnp.asarray(ovT, BF16), jnp.asarray(qaug, BF16), jnp.asarray(lowb), jnp.asarray(causb)


def _mixer(x2, B, T, norm_g, w_in, q_norm_g, k_norm_g, cmp_pos, w_cmp, ret_norm_g, w_out):
    ns = T // SLC_LEN
    kc, vc, pret, qT, ks, kw, vsT, vwT, gT = _inproj(x2, norm_g[None, :], _permute_w_in(w_in), q_norm_g[None, :],
                                                     k_norm_g[1:3], B, T)
    wk, pk = _compress_weights(w_cmp[0], cmp_pos[0])
    wv, pv = _compress_weights(w_cmp[1], cmp_pos[1])
    kcmp, vcT = _compress(kc, vc, wk, wv, pk, pv, k_norm_g[0:1], B, T)
    ovT, qaug, lowb, causb = _nsa_consts(T)
    ocmp, sel, flags = _nsa_cmp(qT, qaug, kcmp, vcT, ovT, B, T)
    lists, counts = _nsa_steps(flags[:, :, :, 0, :].reshape(-1, ns), T // Q_BLOCK)
    kpad = jnp.zeros((WIN, 2 * HEAD_DIM), BF16).at[:, HEAD_DIM:HEAD_DIM + 3].set(-2.0 ** 100)
    kw = jnp.concatenate([jnp.broadcast_to(kpad, kw.shape[:2] + kpad.shape), kw], axis=2)
    vwT = jnp.pad(vwT, ((0, 0), (0, 0), (0, 0), (WIN, 0)))
    nsa = _nsa_main(lists, counts, qT, qaug, ks, vsT, kw, vwT, sel, gT, ocmp, lowb, causb, B, T)
    ret = _retention(pret, ret_norm_g[None, :], B, T)
    return nsa.reshape(B * T, NSA_WIDTH), ret, w_out.astype(BF16)


def _moe_layer(x2, nsa, ret, wo, norm_g, router, router_b, wg, wu, wd, tm=1024):
    tm = min(tm, x2.shape[0])
    x1, h, rank, comb, rankT, cnt = _router(x2, nsa, ret, wo, norm_g[None, :], router, router_b, tm)
    counts = cnt[:, 0, :N_EXPERTS].reshape(-1)
    return _moe(counts, h, rankT, rank, comb, wg.astype(BF16), wu.astype(BF16), wd.astype(BF16), x1, tm)


def kernel(x, norm_mix_g, w_in, q_norm_g, k_norm_g, cmp_pos, w_cmp, ret_norm_g, w_out, norm_ffn_g,
           ffn_w_gate, ffn_w_up, ffn_w_down, moe_router, moe_router_b, moe_w_gate, moe_w_up, moe_w_down):
    B, T, D = x.shape
    depth = norm_mix_g.shape[0]
    x2 = x.reshape(B * T, D)
    for l in range(depth):
        nsa, ret, wo = _mixer(x2, B, T, norm_mix_g[l], w_in[l], q_norm_g[l], k_norm_g[l], cmp_pos[l], w_cmp[l],
                              ret_norm_g[l], w_out[l])
        j = l // 2
        if l % 2 == 0:
            x2 = _ffn(x2, nsa, ret, wo, norm_ffn_g[l][None, :], ffn_w_gate[j].astype(BF16),
                      ffn_w_up[j].astype(BF16), ffn_w_down[j].astype(BF16))
        else:
            x2 = _moe_layer(x2, nsa, ret, wo, norm_ffn_g[l], moe_router[j], moe_router_b[j], moe_w_gate[j],
                            moe_w_up[j], moe_w_down[j])
    return x2.reshape(B, T, D)
```

```python
import functools

import numpy as np
import jax
import jax.numpy as jnp
from jax import lax
from jax.experimental import pallas as pl
from jax.experimental.pallas import tpu as pltpu

F32 = jnp.float32
BF16 = jnp.bfloat16

HEAD_DIM = 64
NSA_HEADS = 8
NSA_KV_HEADS = 2
NSA_GROUP = NSA_HEADS // NSA_KV_HEADS
RET_HEADS = 8
RET_DK = 32
RET_DV = 64
NSA_WIDTH = NSA_HEADS * HEAD_DIM
RET_WIDTH = RET_HEADS * RET_DV
KV_WIDTH = NSA_KV_HEADS * HEAD_DIM
CMP_LEN = 32
CMP_STRIDE = 16
SLC_LEN = 64
SLC_TOPK = 16
WIN = 512
Q_BLOCK = 128
RET_CHUNK = 128
N_EXPERTS = 8
EPS = 1e-6
NEG = -1e30
BIG = 1e9
LANES = 128
GQ = NSA_GROUP * Q_BLOCK
KEY_STEP = 128
STEP_GROUP = 4
N_FORCED = 3
CMP_CHUNK = 128
CMP_TILES = 8
MAIN_TILES = 8
LOOP_TILES = 2
CMP_TAIL = CMP_CHUNK + 8
WIN_KEYS = WIN + Q_BLOCK
V_ROWS = HEAD_DIM + 16
LOG2E = 1.4426950408889634
VMEM_LIMIT = 60 * 1024 * 1024

_C_Q = 0
_C_KV = _C_Q + NSA_WIDTH
_C_KC = _C_KV + 4 * KV_WIDTH
_C_VC = _C_KC + KV_WIDTH
_C_GT = _C_VC + KV_WIDTH
_C_RET = _C_GT + LANES
_RET_COLS = 2 * RET_HEADS * RET_DK + 2 * RET_WIDTH
_C_END = _C_RET + _RET_COLS


def _params(n_axes, vmem=VMEM_LIMIT):
    return pltpu.CompilerParams(dimension_semantics=("arbitrary",) * n_axes, vmem_limit_bytes=vmem)


def _dot(a, b):
    return jnp.dot(a, b, preferred_element_type=F32)


def _dot_nt(a, b):
    return lax.dot_general(a, b, (((1,), (1,)), ((), ())), preferred_element_type=F32)


def _rms(x, g):
    return x * lax.rsqrt(jnp.mean(x * x, axis=-1, keepdims=True) + EPS) * g


def _group_rms(x, g, ones_ref):
    w = x.shape[1]
    ones = ones_ref[0:w, 0:w]
    sq = x * x
    hi = sq.astype(BF16)
    lo = (sq - hi.astype(F32)).astype(BF16)
    ms = (_dot(hi, ones) + _dot(lo, ones)) * (1.0 / HEAD_DIM)
    return x * lax.rsqrt(ms + EPS) * g


def _group_ones():
    lane = np.arange(NSA_WIDTH) // HEAD_DIM
    return jnp.asarray(lane[:, None] == lane[None, :], BF16)


def _inproj_kernel(x_ref, g_ref, w_ref, qg_ref, kg_ref, ones_ref, kc_ref, vc_ref, ret_ref,
                   qT_ref, ks_ref, kw_ref, vsT_ref, vwT_ref, gT_ref, *, steps_per_row):
    n_tok = x_ref.shape[0]
    tiles = range(n_tok // Q_BLOCK)
    rows = [slice(u * Q_BLOCK, (u + 1) * Q_BLOCK) for u in tiles]
    xn = _rms(x_ref[...], g_ref[...]).astype(BF16)
    q = _dot(xn, w_ref[:, _C_Q:_C_KV])
    kv = _dot(xn, w_ref[:, _C_KV:_C_KC])
    gt = _dot(xn, w_ref[:, _C_GT:_C_RET])

    scale = HEAD_DIM ** -0.5 * LOG2E
    qn = _group_rms(q, qg_ref[...], ones_ref) * scale
    qt = [qn[rows[u]].T for u in tiles]
    for g in range(NSA_KV_HEADS):
        for r in range(NSA_GROUP):
            h = g * NSA_GROUP + r
            for u in tiles:
                qT_ref[g, u, :, r * Q_BLOCK:(r + 1) * Q_BLOCK] = qt[u][h * HEAD_DIM:(h + 1) * HEAD_DIM, :].astype(BF16)
    vst = [kv[rows[u], 2 * KV_WIDTH:3 * KV_WIDTH].T for u in tiles]
    vwt = [kv[rows[u], 3 * KV_WIDTH:4 * KV_WIDTH].T for u in tiles]
    gts = [jax.nn.sigmoid(gt[rows[u], :].T[0:32, :]) for u in tiles]
    pos0 = (pl.program_id(0) % steps_per_row) * n_tok
    pos = pos0 + lax.broadcasted_iota(jnp.int32, (n_tok, HEAD_DIM), 0)
    col = lax.broadcasted_iota(jnp.int32, (n_tok, HEAD_DIM), 1)
    kpos = jnp.where(col < 3, pos // SLC_LEN, jnp.where(col < 6, pos % SLC_LEN, 0)).astype(F32)
    ones_row = (lax.broadcasted_iota(jnp.int32, (V_ROWS - HEAD_DIM, Q_BLOCK), 0) == 0).astype(F32)
    ks = _group_rms(kv[:, 0:KV_WIDTH], kg_ref[0:1, :], ones_ref)
    kw = _group_rms(kv[:, KV_WIDTH:2 * KV_WIDTH], kg_ref[1:2, :], ones_ref)
    for g in range(NSA_KV_HEADS):
        sl = slice(g * HEAD_DIM, (g + 1) * HEAD_DIM)
        ks_ref[g] = jnp.concatenate([ks[:, sl], kpos], axis=1).astype(BF16)
        kw_ref[g] = jnp.concatenate([kw[:, sl], kpos], axis=1).astype(BF16)
        for u in tiles:
            vsT_ref[g, :, rows[u]] = jnp.concatenate([vst[u][sl, :], ones_row], axis=0).astype(BF16)
            vwT_ref[g, :, rows[u]] = jnp.concatenate([vwt[u][sl, :], ones_row], axis=0).astype(BF16)
    for u in tiles:
        gT_ref[u] = gts[u]
    ret_ref[...] = _dot(xn, w_ref[:, _C_RET:_C_END])
    kc_ref[...] = _dot(xn, w_ref[:, _C_KC:_C_VC])
    vc_ref[...] = _dot(xn, w_ref[:, _C_VC:_C_GT])


def _inproj(x2, g, w, qg, kg, B, T, tm=1024):
    n, d = x2.shape
    nq = T // Q_BLOCK
    G = NSA_KV_HEADS
    tiles = tm // Q_BLOCK
    spr = T // tm
    assert T % tm == 0
    const = lambda i: (0, 0)
    row = lambda i: (i, 0)
    return pl.pallas_call(
        functools.partial(_inproj_kernel, steps_per_row=spr),
        grid=(n // tm,),
        in_specs=[pl.BlockSpec((tm, d), row),
                  pl.BlockSpec((1, d), const),
                  pl.BlockSpec((d, _C_END), const),
                  pl.BlockSpec((1, NSA_WIDTH), const),
                  pl.BlockSpec((2, KV_WIDTH), const),
                  pl.BlockSpec((NSA_WIDTH, NSA_WIDTH), const)],
        out_specs=[pl.BlockSpec((tm, KV_WIDTH), row),
                   pl.BlockSpec((tm, KV_WIDTH), row),
                   pl.BlockSpec((tm, _RET_COLS), row),
                   pl.BlockSpec((None, G, tiles, HEAD_DIM, GQ), lambda i: (i // spr, 0, i % spr, 0, 0)),
                   pl.BlockSpec((None, G, tm, 2 * HEAD_DIM), lambda i: (i // spr, 0, i % spr, 0)),
                   pl.BlockSpec((None, G, tm, 2 * HEAD_DIM), lambda i: (i // spr, 0, i % spr, 0)),
                   pl.BlockSpec((None, G, V_ROWS, tm), lambda i: (i // spr, 0, 0, i % spr)),
                   pl.BlockSpec((None, G, V_ROWS, tm), lambda i: (i // spr, 0, 0, i % spr)),
                   pl.BlockSpec((None, tiles, 32, Q_BLOCK), lambda i: (i // spr, i % spr, 0, 0))],
        out_shape=[jax.ShapeDtypeStruct((n, KV_WIDTH), F32),
                   jax.ShapeDtypeStruct((n, KV_WIDTH), F32),
                   jax.ShapeDtypeStruct((n, _RET_COLS), F32),
                   jax.ShapeDtypeStruct((B, G, nq, HEAD_DIM, GQ), BF16),
                   jax.ShapeDtypeStruct((B, G, T, 2 * HEAD_DIM), BF16),
                   jax.ShapeDtypeStruct((B, G, T, 2 * HEAD_DIM), BF16),
                   jax.ShapeDtypeStruct((B, G, V_ROWS, T), BF16),
                   jax.ShapeDtypeStruct((B, G, V_ROWS, T), BF16),
                   jax.ShapeDtypeStruct((B, nq, 32, Q_BLOCK), F32)],
        compiler_params=_params(1),
        name="inproj",
    )(x2, g, w, jnp.tile(qg, (1, NSA_HEADS)), jnp.tile(kg, (1, NSA_KV_HEADS)), _group_ones())


def _compress_kernel(kc_ref, vc_ref, wk_ref, wv_ref, pk_ref, pv_ref, kg_ref, kcmp_ref, vcT_ref):
    ncp = kc_ref.shape[0] // CMP_STRIDE

    def comp(a_ref, w_ref, p_ref):
        lo = jnp.zeros((ncp, KV_WIDTH), F32)
        hi = jnp.zeros((ncp, KV_WIDTH), F32)
        for l in range(CMP_STRIDE):
            a = a_ref[pl.ds(l, ncp, stride=CMP_STRIDE), :]
            lo += _dot((a + p_ref[0, l:l + 1, :]).astype(BF16), w_ref[0, l])
            hi += _dot((a + p_ref[1, l:l + 1, :]).astype(BF16), w_ref[1, l])
        return lo + pltpu.roll(hi, ncp - 1, 0)

    k = comp(kc_ref, wk_ref, pk_ref)
    v = comp(vc_ref, wv_ref, pv_ref).T
    cend = lax.broadcasted_iota(jnp.int32, (ncp, HEAD_DIM), 0) * CMP_STRIDE + (CMP_LEN - 1)
    col = lax.broadcasted_iota(jnp.int32, (ncp, HEAD_DIM), 1)
    kpos = jnp.where(col < 3, cend // SLC_LEN, jnp.where(col < 6, cend % SLC_LEN, 0)).astype(F32)
    for g in range(NSA_KV_HEADS):
        sl = slice(g * HEAD_DIM, (g + 1) * HEAD_DIM)
        kcmp_ref[g] = jnp.concatenate([_rms(k[:, sl], kg_ref[...]), kpos], axis=1).astype(BF16)
        vcT_ref[g] = v[sl, :].astype(BF16)


def _compress(kc, vc, wk, wv, pk, pv, kg, B, T):
    ncp = T // CMP_STRIDE
    G = NSA_KV_HEADS
    const4 = lambda b: (0, 0, 0, 0)
    const3 = lambda b: (0, 0, 0)
    const2 = lambda b: (0, 0)
    return pl.pallas_call(
        _compress_kernel,
        grid=(B,),
        in_specs=[pl.BlockSpec((T, KV_WIDTH), lambda b: (b, 0)),
                  pl.BlockSpec((T, KV_WIDTH), lambda b: (b, 0)),
                  pl.BlockSpec((2, CMP_STRIDE, KV_WIDTH, KV_WIDTH), const4),
                  pl.BlockSpec((2, CMP_STRIDE, KV_WIDTH, KV_WIDTH), const4),
                  pl.BlockSpec((2, CMP_STRIDE, KV_WIDTH), const3),
                  pl.BlockSpec((2, CMP_STRIDE, KV_WIDTH), const3),
                  pl.BlockSpec((1, HEAD_DIM), const2)],
        out_specs=[pl.BlockSpec((None, G, ncp, 2 * HEAD_DIM), lambda b: (b, 0, 0, 0)),
                   pl.BlockSpec((None, G, HEAD_DIM, ncp), lambda b: (b, 0, 0, 0))],
        out_shape=[jax.ShapeDtypeStruct((B, G, ncp, 2 * HEAD_DIM), BF16),
                   jax.ShapeDtypeStruct((B, G, HEAD_DIM, ncp), BF16)],
        compiler_params=_params(1),
        name="nsa_compress",
    )(kc, vc, wk, wv, pk, pv, kg)


def _compress_weights(w, pos):
    G = NSA_KV_HEADS
    w4 = w.reshape(2, CMP_STRIDE, HEAD_DIM, HEAD_DIM)
    eye = jnp.eye(G, dtype=w.dtype)
    wbd = jnp.einsum('hlde,gk->hlgdke', w4, eye).reshape(2, CMP_STRIDE, KV_WIDTH, KV_WIDTH)
    p = pos.reshape(2, CMP_STRIDE, 1, HEAD_DIM)
    p = jnp.broadcast_to(p, (2, CMP_STRIDE, G, HEAD_DIM)).reshape(2, CMP_STRIDE, KV_WIDTH)
    return wbd.astype(BF16), p


def _split3(x):
    hi = x.astype(BF16)
    r = x - hi.astype(F32)
    mid = r.astype(BF16)
    lo = (r - mid.astype(F32)).astype(BF16)
    return hi, mid, lo


def _nsa_cmp_kernel(qT_ref, qaug_ref, kc_ref, vcT_ref, ovT_ref, ocmp_ref, sel_ref, flag_ref, *, n_sel):
    ncp = kc_ref.shape[0]
    ns = ovT_ref.shape[0]
    tiles = range(CMP_TILES)
    i0 = pl.program_id(2) * CMP_TILES
    lane = lax.broadcasted_iota(jnp.int32, (1, GQ), 1)
    q = [jnp.concatenate([qT_ref[u], qaug_ref[...]], axis=0) for u in tiles]
    t_row = [(i0 + u) * Q_BLOCK + (lane & (Q_BLOCK - 1)) for u in tiles]
    has_cmp = [(t_row[u] >= CMP_LEN - 1).astype(F32) for u in tiles]
    tq = [(i0 + u) * Q_BLOCK + lax.broadcasted_iota(jnp.int32, (1, Q_BLOCK), 1) for u in tiles]
    cur = [tq[u] // SLC_LEN for u in tiles]

    def prefix(rows):
        nsk = rows * CMP_STRIDE // SLC_LEN
        tail0 = max(rows - CMP_TAIL, 0)
        kc = kc_ref[0:rows, :]
        s = [_dot(kc, q[u]) for u in tiles]
        cend = (lax.broadcasted_iota(jnp.int32, (rows - tail0, 1), 0) + tail0) * CMP_STRIDE + (CMP_LEN - 1)
        tail = [jnp.where(t_row[u] >= cend, s[u][tail0:], NEG) for u in tiles]
        s = [jnp.concatenate([s[u][0:tail0], tail[u]], axis=0) if tail0 else tail[u] for u in tiles]
        m = [jnp.max(s[u], axis=0, keepdims=True) for u in tiles]
        e = [jnp.exp2(s[u] - m[u]) for u in tiles]
        p = [e[u] * (has_cmp[u] / jnp.sum(e[u], axis=0, keepdims=True)) for u in tiles]
        vc = vcT_ref[:, 0:rows]
        for u in tiles:
            ocmp_ref[u] = _dot(vc, p[u].astype(BF16))

        ps = [p[u][:, 0:Q_BLOCK] for u in tiles]
        for r in range(1, NSA_GROUP):
            ps = [ps[u] + p[u][:, r * Q_BLOCK:(r + 1) * Q_BLOCK] for u in tiles]
        ov = ovT_ref[0:nsk, 0:rows]
        split = [_split3(ps[u]) for u in tiles]
        imp = [_dot(ov, split[u][0]) + _dot(ov, split[u][1]) + _dot(ov, split[u][2]) for u in tiles]

        blk = lax.broadcasted_iota(jnp.int32, (nsk, 1), 0)
        forced = [(blk == 0) | (blk == cur[u]) | (blk == cur[u] - 1) for u in tiles]
        valid = [blk * SLC_LEN <= tq[u] for u in tiles]
        imp = [jnp.where(forced[u], -3e38, jnp.where(valid[u], imp[u], -BIG)) for u in tiles]
        blk_f = blk.astype(F32)
        sel = [forced[u].astype(F32) for u in tiles]
        for _ in range(n_sel - N_FORCED):
            mx = [jnp.max(imp[u], axis=0, keepdims=True) for u in tiles]
            idx = [jnp.min(jnp.where(imp[u] == mx[u], blk_f, float(ns)), axis=0, keepdims=True) for u in tiles]
            pick = [blk_f == idx[u] for u in tiles]
            sel = [jnp.where(pick[u], 1.0, sel[u]) for u in tiles]
            imp = [jnp.where(pick[u], -3e38, imp[u]) for u in tiles]
        ones = jnp.ones((8, Q_BLOCK), BF16)
        for u in tiles:
            sel_ref[u, 0:nsk, :] = sel[u]
            cnt = _dot_nt(ones, sel[u].astype(BF16))
            flag_ref[u, :, 0:nsk] = (cnt > 0).astype(jnp.int32)
            if nsk < ns:
                sel_ref[u, nsk:, :] = jnp.zeros((ns - nsk, Q_BLOCK), F32)
                flag_ref[u, :, nsk:] = jnp.zeros((8, ns - nsk), jnp.int32)

    n_variants = ncp // CMP_CHUNK
    last = i0 + CMP_TILES - 1
    variant = (last * (Q_BLOCK // CMP_STRIDE) + (Q_BLOCK // CMP_STRIDE - 2)) // CMP_CHUNK
    for k in range(n_variants):
        pl.when(variant == k)(functools.partial(prefix, (k + 1) * CMP_CHUNK))


def _nsa_cmp(qT, qaug, kcmp, vcT, ovT, B, T):
    G = NSA_KV_HEADS
    nq = T // Q_BLOCK
    ncp = T // CMP_STRIDE
    ns = T // SLC_LEN
    n_sel = min(SLC_TOPK, ns)
    assert ncp % CMP_CHUNK == 0 and n_sel > N_FORCED and nq % CMP_TILES == 0
    tile = lambda b, g, i: (b, g, i, 0, 0)
    return pl.pallas_call(
        functools.partial(_nsa_cmp_kernel, n_sel=n_sel),
        grid=(B, G, nq // CMP_TILES),
        in_specs=[pl.BlockSpec((None, None, CMP_TILES, HEAD_DIM, GQ), tile),
                  pl.BlockSpec((None, HEAD_DIM, GQ), lambda b, g, i: (g, 0, 0)),
                  pl.BlockSpec((None, None, ncp, 2 * HEAD_DIM), lambda b, g, i: (b, g, 0, 0)),
                  pl.BlockSpec((None, None, HEAD_DIM, ncp), lambda b, g, i: (b, g, 0, 0)),
                  pl.BlockSpec((ns, ncp), lambda b, g, i: (0, 0))],
        out_specs=[pl.BlockSpec((None, None, CMP_TILES, HEAD_DIM, GQ), tile),
                   pl.BlockSpec((None, None, CMP_TILES, ns, Q_BLOCK), tile),
                   pl.BlockSpec((None, None, CMP_TILES, 8, ns), tile)],
        out_shape=[jax.ShapeDtypeStruct((B, G, nq, HEAD_DIM, GQ), F32),
                   jax.ShapeDtypeStruct((B, G, nq, ns, Q_BLOCK), F32),
                   jax.ShapeDtypeStruct((B, G, nq, 8, ns), jnp.int32)],
        compiler_params=_params(3),
        name="nsa_cmp",
    )(qT, qaug, kcmp, vcT, ovT)


def _nsa_main_kernel(list_ref, cnt_ref, qT_ref, qaug_ref, ks_ref, vsT_ref, kw_ref, vwT_ref, sel_ref, gT_ref, ocmp_ref,
                     lowb_ref, causb_ref, out_ref, m_sc, acc_sc, win_sc):
    b, g = pl.program_id(0), pl.program_id(1)
    tiles = range(MAIN_TILES)
    i = [pl.program_id(2) * MAIN_TILES + u for u in tiles]
    tile_id = [(b * pl.num_programs(1) + g) * (pl.num_programs(2) * MAIN_TILES) + i[u] for u in tiles]
    n_steps = sel_ref.shape[1] // 2
    q = [jnp.concatenate([qT_ref[u], qaug_ref[...]], axis=0) for u in tiles]
    k0 = [pl.multiple_of(i[u] * Q_BLOCK, Q_BLOCK) for u in tiles]

    def sel_bias(u, j, valid):
        def row(r):
            picked = (sel_ref[u, pl.ds(r, 1), :] > 0.5) & valid
            return jnp.concatenate([jnp.where(picked, 0.0, NEG)] * NSA_GROUP, axis=1)
        return row(2 * j), row(2 * j + 1)

    def add_sel_bias(s, ba, bb):
        return jnp.concatenate([s[0:SLC_LEN] + ba, s[SLC_LEN:] + bb], axis=0)

    lowb = jnp.concatenate([lowb_ref[...]] * NSA_GROUP, axis=1)
    causb = jnp.concatenate([causb_ref[...]] * NSA_GROUP, axis=1)

    bias_d = [sel_bias(u, i[u], True) for u in tiles]
    sd = [_dot(ks_ref[pl.ds(k0[u], KEY_STEP), :], q[u]) for u in tiles]
    sw = [_dot(kw_ref[pl.ds(k0[u], WIN_KEYS), :], q[u]) for u in tiles]
    sd = [add_sel_bias(sd[u], *bias_d[u]) + causb for u in tiles]
    sw = [jnp.concatenate([sw[u][0:Q_BLOCK] + lowb, sw[u][Q_BLOCK:WIN], sw[u][WIN:] + causb], axis=0) for u in tiles]
    md = [jnp.max(sd[u], axis=0, keepdims=True) for u in tiles]
    mw = [jnp.max(sw[u], axis=0, keepdims=True) for u in tiles]
    accd = [_dot(vsT_ref[:, pl.ds(k0[u], KEY_STEP)], jnp.exp2((sd[u] - md[u]).astype(BF16))) for u in tiles]
    ow = [_dot(vwT_ref[:, pl.ds(k0[u], WIN_KEYS)], jnp.exp2((sw[u] - mw[u]).astype(BF16))) for u in tiles]
    for u in tiles:
        m_sc[u] = md[u]
        acc_sc[u] = accd[u]
        win_sc[u] = ow[u][0:HEAD_DIM] / ow[u][HEAD_DIM:HEAD_DIM + 1]

    def scores(u, t):
        ks, vs, biases = [], [], []
        for x in range(STEP_GROUP):
            j = list_ref[tile_id[u] * n_steps + t * STEP_GROUP + x]
            valid = j >= 0
            j = jnp.maximum(j, 0)
            kj = pl.multiple_of(j * KEY_STEP, KEY_STEP)
            ks.append(ks_ref[pl.ds(kj, KEY_STEP), :])
            vs.append(vsT_ref[:, pl.ds(kj, KEY_STEP)])
            biases.append(sel_bias(u, j, valid))
        s = _dot(jnp.concatenate(ks, axis=0), q[u])
        s = jnp.concatenate([add_sel_bias(s[x * KEY_STEP:(x + 1) * KEY_STEP], *biases[x])
                             for x in range(STEP_GROUP)], axis=0)
        return s, jnp.max(s, axis=0, keepdims=True), jnp.concatenate(vs, axis=1)

    def accumulate(u, s, smax, vcat):
        m_old = m_sc[u]
        m_new = jnp.maximum(m_old, smax)
        alpha = jnp.exp2(m_old - m_new)
        acc_sc[u] = alpha * acc_sc[u] + _dot(vcat, jnp.exp2((s - m_new).astype(BF16)))
        m_sc[u] = m_new

    def run(work, t, carry):
        staged = [(u, scores(u, t * mult + off)) for (u, mult, off) in work]
        for u, args in staged:
            accumulate(u, *args)
        return carry

    for u0 in range(0, MAIN_TILES, LOOP_TILES):
        us = range(u0, u0 + LOOP_TILES)
        n_groups = functools.reduce(
            jnp.maximum, [(cnt_ref[tile_id[u]] + (STEP_GROUP - 1)) // STEP_GROUP for u in us])
        lax.fori_loop(0, n_groups // 2, functools.partial(run, [(u, 2, off) for off in (0, 1) for u in us]), 0)
        lax.fori_loop(n_groups // 2 * 2, n_groups, functools.partial(run, [(u, 1, 0) for u in us]), 0)

    def gate(u, k):
        rows = [gT_ref[u, pl.ds(g * (NSA_GROUP * 3) + r * 3 + k, 1), :] for r in range(NSA_GROUP)]
        return jnp.concatenate(rows, axis=1)

    o_slc = [acc_sc[u, 0:HEAD_DIM, :] / acc_sc[u, HEAD_DIM:HEAD_DIM + 1, :] for u in tiles]
    o = [gate(u, 0) * ocmp_ref[u] + gate(u, 1) * o_slc[u] + gate(u, 2) * win_sc[u] for u in tiles]
    o = [jnp.concatenate([o[u], jnp.zeros_like(o[u])], axis=0) for u in tiles]
    for r in range(NSA_GROUP):
        ot = [o[u][:, r * Q_BLOCK:(r + 1) * Q_BLOCK].T[:, 0:HEAD_DIM] for u in tiles]
        for u in tiles:
            out_ref[u * Q_BLOCK:(u + 1) * Q_BLOCK, r * HEAD_DIM:(r + 1) * HEAD_DIM] = ot[u]


def _nsa_steps_kernel(flagT_ref, pairT_ref, list_ref, cnt_ref, *, nq):
    n_steps, nt = list_ref.shape
    need = _dot(pairT_ref[...], flagT_ref[...].astype(BF16)) > 0
    step = lax.broadcasted_iota(jnp.int32, (n_steps, 1), 0)
    own = lax.broadcasted_iota(jnp.int32, (1, nt), 1) % nq
    need = need & (step < own)
    need_f = need.astype(F32)
    earlier = (lax.broadcasted_iota(jnp.int32, (n_steps, n_steps), 1) < step).astype(BF16)
    slot = _dot(earlier, need_f.astype(BF16))
    total = jnp.sum(need_f, axis=0, keepdims=True)
    cnt_ref[...] = jnp.broadcast_to(total, cnt_ref.shape).astype(jnp.int32)
    step_f = step.astype(F32)
    for p in range(n_steps):
        val = jnp.sum(jnp.where(need & (slot == p), step_f, 0.0), axis=0, keepdims=True)
        list_ref[p:p + 1, :] = jnp.where(total > p, val, -1.0).astype(jnp.int32)


def _nsa_steps(flags, nq):
    nt, ns = flags.shape
    n_steps = ns // 2
    pairT = jnp.asarray(np.arange(n_steps)[:, None] == np.arange(ns)[None, :] // 2, BF16)
    lists, counts = pl.pallas_call(
        functools.partial(_nsa_steps_kernel, nq=nq),
        out_shape=[jax.ShapeDtypeStruct((n_steps, nt), jnp.int32), jax.ShapeDtypeStruct((8, nt), jnp.int32)],
        name="nsa_steps",
    )(flags.T.astype(F32), pairT)
    return lists.T.reshape(-1), counts[0]


def _nsa_main(lists, counts, qT, qaug, ks, vsT, kw, vwT, sel, gT, ocmp, lowb, causb, B, T):
    G = NSA_KV_HEADS
    nq = T // Q_BLOCK
    ns = T // SLC_LEN
    whole = lambda b, g, i, *_: (b, g, 0, 0)
    tile = lambda b, g, i, *_: (b, g, i, 0, 0)
    const = lambda b, g, i, *_: (0, 0)
    grid_spec = pltpu.PrefetchScalarGridSpec(
        num_scalar_prefetch=2,
        grid=(B, G, nq // MAIN_TILES),
        in_specs=[pl.BlockSpec((None, None, MAIN_TILES, HEAD_DIM, GQ), tile),
                  pl.BlockSpec((None, HEAD_DIM, GQ), lambda b, g, i, *_: (g, 0, 0)),
                  pl.BlockSpec((None, None, T, 2 * HEAD_DIM), whole),
                  pl.BlockSpec((None, None, V_ROWS, T), whole),
                  pl.BlockSpec((None, None, T + WIN, 2 * HEAD_DIM), whole),
                  pl.BlockSpec((None, None, V_ROWS, T + WIN), whole),
                  pl.BlockSpec((None, None, MAIN_TILES, ns, Q_BLOCK), tile),
                  pl.BlockSpec((None, MAIN_TILES, 32, Q_BLOCK), lambda b, g, i, *_: (b, i, 0, 0)),
                  pl.BlockSpec((None, None, MAIN_TILES, HEAD_DIM, GQ), tile),
                  pl.BlockSpec((Q_BLOCK, Q_BLOCK), const),
                  pl.BlockSpec((Q_BLOCK, Q_BLOCK), const)],
        out_specs=pl.BlockSpec((None, MAIN_TILES * Q_BLOCK, NSA_GROUP * HEAD_DIM), lambda b, g, i, *_: (b, i, g)),
        scratch_shapes=[pltpu.VMEM((MAIN_TILES, 1, GQ), F32), pltpu.VMEM((MAIN_TILES, V_ROWS, GQ), F32),
                        pltpu.VMEM((MAIN_TILES, HEAD_DIM, GQ), F32)],
    )
    return pl.pallas_call(
        _nsa_main_kernel,
        grid_spec=grid_spec,
        out_shape=jax.ShapeDtypeStruct((B, T, NSA_WIDTH), F32),
        compiler_params=_params(3),
        name="nsa_main",
    )(lists, counts, qT, qaug, ks, vsT, kw, vwT, sel, gT, ocmp, lowb, causb)


def _ret_kernel(p_ref, decay_ref, xi_ref, zeta_ref, gch_ref, ng_ref, ones_ref, out_ref, state_ref):
    @pl.when(pl.program_id(0) == 0)
    def _():
        state_ref[...] = jnp.zeros(state_ref.shape, F32)

    rows = range(p_ref.shape[0])
    kw = RET_HEADS * RET_DK
    p = [p_ref[b] for b in rows]
    rq = [p[b][:, 0:kw] * (RET_DK ** -0.5) for b in rows]
    rk = [p[b][:, kw:2 * kw] for b in rows]
    rkT = [rk[b].T for b in rows]
    rv = [p[b][:, 2 * kw:2 * kw + RET_WIDTH] for b in rows]
    xi = xi_ref[...]
    outs = [[] for _ in rows]
    for h in range(RET_HEADS):
        dk = slice(h * RET_DK, (h + 1) * RET_DK)
        dv = slice(h * RET_DV, (h + 1) * RET_DV)
        st = [state_ref[b, h] for b in rows]
        inner = [_dot_nt(rq[b][:, dk], rk[b][:, dk]) * decay_ref[h] for b in rows]
        o = [_dot(inner[b], rv[b][:, dv]) + _dot(rq[b][:, dk], st[b]) * xi[:, h:h + 1] for b in rows]
        for b in rows:
            state_ref[b, h] = (st[b] * gch_ref[h:h + 1, 0:1]
                               + _dot(rkT[b][dk, :] * zeta_ref[h:h + 1, :], rv[b][:, dv]))
            outs[b].append(o[b])
    normed = [_group_rms(jnp.concatenate(outs[b], axis=1), ng_ref[...], ones_ref) for b in rows]
    for b in rows:
        rg = p[b][:, 2 * kw + RET_WIDTH:2 * kw + 2 * RET_WIDTH]
        out_ref[b] = normed[b] * (rg * jax.nn.sigmoid(rg))


def _ret_consts():
    H, C = RET_HEADS, RET_CHUNK
    log_g = np.log1p(-np.exp2(-5.0 - np.arange(H, dtype=np.float64)))
    idx = np.arange(C, dtype=np.float64)
    diff = idx[:, None] - idx[None, :]
    decay = np.where(diff >= 0, np.exp(np.maximum(diff, 0.0) * log_g[:, None, None]), 0.0)
    zeta = np.exp((C - 1 - idx) * log_g[:, None])
    xi = np.exp((idx + 1) * log_g[:, None]).T
    g_chunk = np.broadcast_to(np.exp(C * log_g)[:, None], (H, LANES))
    return tuple(jnp.asarray(a, F32) for a in (decay, xi, zeta, g_chunk))


def _retention(pret, ng, B, T):
    nch = T // RET_CHUNK
    decay, xi, zeta, gch = _ret_consts()
    c2 = lambda c: (0, 0)
    out = pl.pallas_call(
        _ret_kernel,
        grid=(nch,),
        in_specs=[pl.BlockSpec((B, RET_CHUNK, _RET_COLS), lambda c: (0, c, 0)),
                  pl.BlockSpec((RET_HEADS, RET_CHUNK, RET_CHUNK), lambda c: (0, 0, 0)),
                  pl.BlockSpec((RET_CHUNK, RET_HEADS), c2),
                  pl.BlockSpec((RET_HEADS, RET_CHUNK), c2),
                  pl.BlockSpec((RET_HEADS, LANES), c2),
                  pl.BlockSpec((1, RET_WIDTH), c2),
                  pl.BlockSpec((RET_WIDTH, RET_WIDTH), c2)],
        out_specs=pl.BlockSpec((B, RET_CHUNK, RET_WIDTH), lambda c: (0, c, 0)),
        out_shape=jax.ShapeDtypeStruct((B, T, RET_WIDTH), F32),
        scratch_shapes=[pltpu.VMEM((B, RET_HEADS, RET_DK, RET_DV), F32)],
        compiler_params=_params(1),
        name="retention",
    )(pret.reshape(B, T, _RET_COLS), decay, xi, zeta, gch, ng, _group_ones())
    return out.reshape(B * T, RET_WIDTH)


def _mixer_residual(x_ref, nsa_ref, ret_ref, wo_ref):
    return (x_ref[...] + _dot(nsa_ref[...].astype(BF16), wo_ref[0:NSA_WIDTH, :])
            + _dot(ret_ref[...].astype(BF16), wo_ref[NSA_WIDTH:, :]))


def _mixer_specs(tm, d, row, const):
    return [pl.BlockSpec((tm, d), row), pl.BlockSpec((tm, NSA_WIDTH), row), pl.BlockSpec((tm, RET_WIDTH), row),
            pl.BlockSpec((NSA_WIDTH + RET_WIDTH, d), const)]


def _ffn_kernel(x_ref, nsa_ref, ret_ref, wo_ref, g_ref, wg_ref, wu_ref, wd_ref, o_ref, h_sc):
    f = pl.program_id(1)

    @pl.when(f == 0)
    def _():
        x = _mixer_residual(x_ref, nsa_ref, ret_ref, wo_ref)
        h_sc[...] = _rms(x, g_ref[...]).astype(BF16)
        o_ref[...] = x

    h = h_sc[...]
    a = _dot(h, wg_ref[...])
    act = (a * jax.nn.sigmoid(a) * _dot(h, wu_ref[...])).astype(BF16)
    o_ref[...] += _dot(act, wd_ref[...])


def _ffn(x2, nsa, ret, wo, g, wg, wu, wd, tm=1024, fc=1408):
    n, d = x2.shape
    dff = wg.shape[1]
    return pl.pallas_call(
        _ffn_kernel,
        grid=(n // tm, dff // fc),
        in_specs=_mixer_specs(tm, d, lambda i, f: (i, 0), lambda i, f: (0, 0)) + [
            pl.BlockSpec((1, d), lambda i, f: (0, 0)),
            pl.BlockSpec((d, fc), lambda i, f: (0, f)),
            pl.BlockSpec((d, fc), lambda i, f: (0, f)),
            pl.BlockSpec((fc, d), lambda i, f: (f, 0))],
        out_specs=pl.BlockSpec((tm, d), lambda i, f: (i, 0)),
        out_shape=jax.ShapeDtypeStruct((n, d), F32),
        scratch_shapes=[pltpu.VMEM((tm, d), BF16)],
        compiler_params=_params(2),
        name="ffn_dense",
    )(x2, nsa, ret, wo, g, wg, wu, wd)


def _router_kernel(x_ref, nsa_ref, ret_ref, wo_ref, g_ref, r_ref, rb_ref, tri_ref,
                   x1_ref, h_ref, rank_ref, comb_ref, rankT_ref, cnt_ref):
    x = _mixer_residual(x_ref, nsa_ref, ret_ref, wo_ref)
    x1_ref[...] = x
    h = _rms(x, g_ref[...])
    h_ref[...] = h.astype(BF16)
    hh, hm, hl = _split3(h)
    rh, rm, rl = _split3(r_ref[...])
    logits = (_dot(hh, rh) + (_dot(hh, rm) + _dot(hm, rh)) + (_dot(hh, rl) + _dot(hm, rm) + _dot(hl, rh))
              + rb_ref[...])
    lane = lax.broadcasted_iota(jnp.int32, logits.shape, 1).astype(F32)
    logits = jnp.where(lane < N_EXPERTS, logits, NEG)
    m1 = jnp.max(logits, axis=1, keepdims=True)
    i1 = jnp.min(jnp.where(logits == m1, lane, float(LANES)), axis=1, keepdims=True)
    l2 = jnp.where(lane == i1, NEG, logits)
    m2 = jnp.max(l2, axis=1, keepdims=True)
    i2 = jnp.min(jnp.where(l2 == m2, lane, float(LANES)), axis=1, keepdims=True)
    e2 = jnp.exp(m2 - m1)
    w1 = 1.0 / (1.0 + e2)
    w2 = e2 / (1.0 + e2)
    use1, use2 = lane == i1, lane == i2
    comb_ref[...] = jnp.where(use1, w1, 0.0) + jnp.where(use2, w2, 0.0)
    use = (use1 | use2).astype(F32)
    rank = jnp.where(use > 0, _dot(tri_ref[...], use.astype(BF16)), -1.0)
    rank_ref[...] = rank
    rankT_ref[...] = rank.T[0:N_EXPERTS, :]
    cnt_ref[...] = jnp.broadcast_to(jnp.sum(use, axis=0, keepdims=True), cnt_ref.shape).astype(jnp.int32)


def _router(x2, nsa, ret, wo, g, router, rb, tm):
    n, d = x2.shape
    nt = n // tm
    tri = jnp.asarray(np.arange(tm)[:, None] > np.arange(tm)[None, :], BF16)
    rpad = jnp.zeros((d, LANES), F32).at[:, :N_EXPERTS].set(router)
    rbpad = jnp.zeros((1, LANES), F32).at[0, :N_EXPERTS].set(rb)
    c2 = lambda i: (0, 0)
    row = lambda i: (i, 0)
    return pl.pallas_call(
        _router_kernel,
        grid=(nt,),
        in_specs=_mixer_specs(tm, d, row, c2) + [
            pl.BlockSpec((1, d), c2),
            pl.BlockSpec((d, LANES), c2),
            pl.BlockSpec((1, LANES), c2),
            pl.BlockSpec((tm, tm), c2)],
        out_specs=[pl.BlockSpec((tm, d), row),
                   pl.BlockSpec((tm, d), row),
                   pl.BlockSpec((tm, LANES), row),
                   pl.BlockSpec((tm, LANES), row),
                   pl.BlockSpec((N_EXPERTS, tm), lambda i: (0, i)),
                   pl.BlockSpec((None, 8, LANES), lambda i: (i, 0, 0))],
        out_shape=[jax.ShapeDtypeStruct((n, d), F32),
                   jax.ShapeDtypeStruct((n, d), BF16),
                   jax.ShapeDtypeStruct((n, LANES), F32),
                   jax.ShapeDtypeStruct((n, LANES), F32),
                   jax.ShapeDtypeStruct((N_EXPERTS, n), F32),
                   jax.ShapeDtypeStruct((nt, 8, LANES), jnp.int32)],
        compiler_params=_params(1),
        name="moe_router",
    )(x2, nsa, ret, wo, g, rpad, rbpad, tri)


MOE_SUB = 144
MOE_MOVE = 2 * MOE_SUB


def _moe_kernel(cnt_ref, h_ref, rankT_ref, rank_ref, comb_ref, wg_ref, wu_ref, wd_ref, x_ref, o_ref, hc_sc, oacc_sc):
    t, e, f = pl.program_id(0), pl.program_id(1), pl.program_id(2)
    nf = pl.num_programs(2)
    tm = h_ref.shape[0]
    nsub = (cnt_ref[t * N_EXPERTS + e] + (MOE_SUB - 1)) // MOE_SUB
    nmove = (nsub + 1) // 2

    @pl.when((e == 0) & (f == 0))
    def _():
        o_ref[...] = x_ref[...]

    @pl.when(f == 0)
    def _():
        rank_row = rankT_ref[...]

        def gather(s, c):
            r0 = pl.multiple_of(s * MOE_MOVE, MOE_MOVE)
            rows = (lax.broadcasted_iota(jnp.int32, (MOE_MOVE, 1), 0) + r0).astype(F32)
            onehot = (rows == rank_row).astype(BF16)
            hc_sc[pl.ds(r0, MOE_MOVE), :] = _dot(onehot, h_ref[...]).astype(BF16)
            oacc_sc[pl.ds(r0, MOE_MOVE), :] = jnp.zeros((MOE_MOVE, oacc_sc.shape[1]), F32)
            return c

        lax.fori_loop(0, nmove, gather, 0)

    def expert(n_rows, s, c):
        r0 = pl.multiple_of(s * n_rows, n_rows)
        rows = hc_sc[pl.ds(r0, n_rows), :]
        a = _dot(rows, wg_ref[...])
        act = (a * jax.nn.sigmoid(a) * _dot(rows, wu_ref[...])).astype(BF16)
        oacc_sc[pl.ds(r0, n_rows), :] += _dot(act, wd_ref[...])
        return c

    lax.fori_loop(0, nsub // 2, functools.partial(expert, MOE_MOVE), 0)
    lax.fori_loop(nsub // 2 * 2, nsub, functools.partial(expert, MOE_SUB), 0)

    @pl.when(f == nf - 1)
    def _():
        is_e = lax.broadcasted_iota(jnp.int32, (1, LANES), 1) == e
        rank_col = jnp.sum(jnp.where(is_e, rank_ref[...], 0.0), axis=1, keepdims=True)
        comb_col = jnp.sum(jnp.where(is_e, comb_ref[...], 0.0), axis=1, keepdims=True)

        def scatter(s, c):
            r0 = pl.multiple_of(s * MOE_MOVE, MOE_MOVE)
            cols = (lax.broadcasted_iota(jnp.int32, (1, MOE_MOVE), 1) + r0).astype(F32)
            onehot = (rank_col == cols).astype(BF16)
            y = _dot(onehot, oacc_sc[pl.ds(r0, MOE_MOVE), :].astype(BF16))
            o_ref[...] += comb_col * y
            return c

        lax.fori_loop(0, nmove, scatter, 0)


def _moe(counts, h, rankT, rank, comb, wg, wu, wd, x2, tm, fc=1408):
    n, d = x2.shape
    dff = wg.shape[2]
    rows_cap = pl.cdiv(pl.cdiv(tm, MOE_SUB), 2) * MOE_MOVE
    grid_spec = pltpu.PrefetchScalarGridSpec(
        num_scalar_prefetch=1,
        grid=(n // tm, N_EXPERTS, dff // fc),
        in_specs=[pl.BlockSpec((tm, d), lambda t, e, f, c: (t, 0)),
                  pl.BlockSpec((None, 1, tm), lambda t, e, f, c: (e, 0, t)),
                  pl.BlockSpec((tm, LANES), lambda t, e, f, c: (t, 0)),
                  pl.BlockSpec((tm, LANES), lambda t, e, f, c: (t, 0)),
                  pl.BlockSpec((None, d, fc), lambda t, e, f, c: (e, 0, f)),
                  pl.BlockSpec((None, d, fc), lambda t, e, f, c: (e, 0, f)),
                  pl.BlockSpec((None, fc, d), lambda t, e, f, c: (e, f, 0)),
                  pl.BlockSpec((tm, d), lambda t, e, f, c: (t, 0))],
        out_specs=pl.BlockSpec((tm, d), lambda t, e, f, c: (t, 0)),
        scratch_shapes=[pltpu.VMEM((rows_cap, d), BF16), pltpu.VMEM((rows_cap, d), F32)],
    )
    return pl.pallas_call(
        _moe_kernel,
        grid_spec=grid_spec,
        out_shape=jax.ShapeDtypeStruct((n, d), F32),
        compiler_params=_params(3),
        name="moe_experts",
    )(counts, h, rankT.reshape(N_EXPERTS, 1, n), rank, comb, wg, wu, wd, x2)


def _permute_w_in(w):
    o = np.cumsum((0, NSA_WIDTH) + (KV_WIDTH,) * 6 + (3 * NSA_HEADS,))
    q, kc, vc, ks, vs, kw, vw, gts = (w[:, o[k]:o[k + 1]] for k in range(8))
    ret = w[:, o[8]:]
    pad = jnp.zeros((w.shape[0], LANES - 3 * NSA_HEADS), w.dtype)
    return jnp.concatenate([q, ks, kw, vs, vw, kc, vc, gts, pad, ret], axis=1).astype(BF16)


def _nsa_consts(T):
    ncp = T // CMP_STRIDE
    ns = T // SLC_LEN
    cs = np.arange(ncp) * CMP_STRIDE
    ss = np.arange(ns) * SLC_LEN
    ov = np.clip(np.minimum(cs[None, :] + CMP_LEN, ss[:, None] + SLC_LEN) - np.maximum(cs[None, :], ss[:, None]), 0, None)
    ovT = (ov.astype(np.float32) / CMP_LEN)
    ovT[:, ncp - 1] = 0.0
    h = np.arange(NSA_HEADS).reshape(NSA_KV_HEADS, NSA_GROUP) + 1
    slopes = np.exp2(-8.0 * h / NSA_HEADS).astype(np.float32)
    slopes = np.repeat(slopes, Q_BLOCK, axis=1)
    parts, rest = [], np.float64(LOG2E)
    for _ in range(3):
        part = np.float64(np.asarray(rest).astype(BF16))
        parts.append(part)
        rest = rest - part
    qaug = np.zeros((NSA_KV_HEADS, HEAD_DIM, GQ), np.float32)
    for k, part in enumerate(parts):
        qaug[:, k, :] = part * SLC_LEN * slopes
        qaug[:, 3 + k, :] = part * slopes
    kq = np.arange(Q_BLOCK)[:, None] - np.arange(Q_BLOCK)[None, :]
    causb = np.where(kq <= 0, 0.0, NEG).astype(np.float32)
    lowb = np.where(kq > 0, 0.0, NEG).astype(np.float32)
    return jnp.asarray(ovT, BF16), jnp.asarray(qaug, BF16), jnp.asarray(lowb), jnp.asarray(causb)


def _mixer(x2, B, T, norm_g, w_in, q_norm_g, k_norm_g, cmp_pos, w_cmp, ret_norm_g, w_out):
    ns = T // SLC_LEN
    kc, vc, pret, qT, ks, kw, vsT, vwT, gT = _inproj(x2, norm_g[None, :], _permute_w_in(w_in), q_norm_g[None, :],
                                                     k_norm_g[1:3], B, T)
    wk, pk = _compress_weights(w_cmp[0], cmp_pos[0])
    wv, pv = _compress_weights(w_cmp[1], cmp_pos[1])
    kcmp, vcT = _compress(kc, vc, wk, wv, pk, pv, k_norm_g[0:1], B, T)
    ovT, qaug, lowb, causb = _nsa_consts(T)
    ocmp, sel, flags = _nsa_cmp(qT, qaug, kcmp, vcT, ovT, B, T)
    lists, counts = _nsa_steps(flags[:, :, :, 0, :].reshape(-1, ns), T // Q_BLOCK)
    kpad = jnp.zeros((WIN, 2 * HEAD_DIM), BF16).at[:, HEAD_DIM:HEAD_DIM + 3].set(-2.0 ** 100)
    kw = jnp.concatenate([jnp.broadcast_to(kpad, kw.shape[:2] + kpad.shape), kw], axis=2)
    vwT = jnp.pad(vwT, ((0, 0), (0, 0), (0, 0), (WIN, 0)))
    nsa = _nsa_main(lists, counts, qT, qaug, ks, vsT, kw, vwT, sel, gT, ocmp, lowb, causb, B, T)
    ret = _retention(pret, ret_norm_g[None, :], B, T)
    return nsa.reshape(B * T, NSA_WIDTH), ret, w_out.astype(BF16)


def _moe_layer(x2, nsa, ret, wo, norm_g, router, router_b, wg, wu, wd, tm=1024):
    tm = min(tm, x2.shape[0])
    x1, h, rank, comb, rankT, cnt = _router(x2, nsa, ret, wo, norm_g[None, :], router, router_b, tm)
    counts = cnt[:, 0, :N_EXPERTS].reshape(-1)
    return _moe(counts, h, rankT, rank, comb, wg.astype(BF16), wu.astype(BF16), wd.astype(BF16), x1, tm)


def kernel(x, norm_mix_g, w_in, q_norm_g, k_norm_g, cmp_pos, w_cmp, ret_norm_g, w_out, norm_ffn_g,
           ffn_w_gate, ffn_w_up, ffn_w_down, moe_router, moe_router_b, moe_w_gate, moe_w_up, moe_w_down):
    B, T, D = x.shape
    depth = norm_mix_g.shape[0]
    x2 = x.reshape(B * T, D)
    for l in range(depth):
        nsa, ret, wo = _mixer(x2, B, T, norm_mix_g[l], w_in[l], q_norm_g[l], k_norm_g[l], cmp_pos[l], w_cmp[l],
                              ret_norm_g[l], w_out[l])
        j = l // 2
        if l % 2 == 0:
            x2 = _ffn(x2, nsa, ret, wo, norm_ffn_g[l][None, :], ffn_w_gate[j].astype(BF16),
                      ffn_w_up[j].astype(BF16), ffn_w_down[j].astype(BF16))
        else:
            x2 = _moe_layer(x2, nsa, ret, wo, norm_ffn_g[l], moe_router[j], moe_router_b[j], moe_w_gate[j],
                            moe_w_up[j], moe_w_down[j])
    return x2.reshape(B, T, D)
```

```python
import functools

import numpy as np
import jax
import jax.numpy as jnp
from jax import lax
from jax.experimental import pallas as pl
from jax.experimental.pallas import tpu as pltpu

F32 = jnp.float32
BF16 = jnp.bfloat16

HEAD_DIM = 64
NSA_HEADS = 8
NSA_KV_HEADS = 2
NSA_GROUP = NSA_HEADS // NSA_KV_HEADS
RET_HEADS = 8
RET_DK = 32
RET_DV = 64
NSA_WIDTH = NSA_HEADS * HEAD_DIM
RET_WIDTH = RET_HEADS * RET_DV
KV_WIDTH = NSA_KV_HEADS * HEAD_DIM
CMP_LEN = 32
CMP_STRIDE = 16
SLC_LEN = 64
SLC_TOPK = 16
WIN = 512
Q_BLOCK = 128
RET_CHUNK = 128
N_EXPERTS = 8
EPS = 1e-6
NEG = -1e30
BIG = 1e9
LANES = 128
GQ = NSA_GROUP * Q_BLOCK
KEY_STEP = 128
STEP_GROUP = 4
N_FORCED = 3
CMP_CHUNK = 128
CMP_TILES = 8
MAIN_TILES = 8
LOOP_TILES = 2
CMP_TAIL = CMP_CHUNK + 8
WIN_KEYS = WIN + Q_BLOCK
V_ROWS = HEAD_DIM + 16
LOG2E = 1.4426950408889634
VMEM_LIMIT = 60 * 1024 * 1024

_C_Q = 0
_C_KV = _C_Q + NSA_WIDTH
_C_KC = _C_KV + 4 * KV_WIDTH
_C_VC = _C_KC + KV_WIDTH
_C_GT = _C_VC + KV_WIDTH
_C_RET = _C_GT + LANES
_RET_COLS = 2 * RET_HEADS * RET_DK + 2 * RET_WIDTH
_C_END = _C_RET + _RET_COLS


def _params(n_axes, vmem=VMEM_LIMIT):
    return pltpu.CompilerParams(dimension_semantics=("arbitrary",) * n_axes, vmem_limit_bytes=vmem)


def _dot(a, b):
    return jnp.dot(a, b, preferred_element_type=F32)


def _dot_nt(a, b):
    return lax.dot_general(a, b, (((1,), (1,)), ((), ())), preferred_element_type=F32)


def _rms(x, g):
    return x * lax.rsqrt(jnp.mean(x * x, axis=-1, keepdims=True) + EPS) * g


def _group_rms(x, g, ones_ref):
    w = x.shape[1]
    ones = ones_ref[0:w, 0:w]
    sq = x * x
    hi = sq.astype(BF16)
    lo = (sq - hi.astype(F32)).astype(BF16)
    ms = (_dot(hi, ones) + _dot(lo, ones)) * (1.0 / HEAD_DIM)
    return x * lax.rsqrt(ms + EPS) * g


def _group_ones():
    lane = np.arange(NSA_WIDTH) // HEAD_DIM
    return jnp.asarray(lane[:, None] == lane[None, :], BF16)


def _inproj_kernel(x_ref, g_ref, w_ref, qg_ref, kg_ref, ones_ref, kc_ref, vc_ref, ret_ref,
                   qT_ref, ks_ref, kw_ref, vsT_ref, vwT_ref, gT_ref, *, steps_per_row):
    n_tok = x_ref.shape[0]
    tiles = range(n_tok // Q_BLOCK)
    rows = [slice(u * Q_BLOCK, (u + 1) * Q_BLOCK) for u in tiles]
    xn = _rms(x_ref[...], g_ref[...]).astype(BF16)
    q = _dot(xn, w_ref[:, _C_Q:_C_KV])
    kv = _dot(xn, w_ref[:, _C_KV:_C_KC])
    gt = _dot(xn, w_ref[:, _C_GT:_C_RET])

    scale = HEAD_DIM ** -0.5 * LOG2E
    qn = _group_rms(q, qg_ref[...], ones_ref) * scale
    qt = [qn[rows[u]].T for u in tiles]
    for g in range(NSA_KV_HEADS):
        for r in range(NSA_GROUP):
            h = g * NSA_GROUP + r
            for u in tiles:
                qT_ref[g, u, :, r * Q_BLOCK:(r + 1) * Q_BLOCK] = qt[u][h * HEAD_DIM:(h + 1) * HEAD_DIM, :].astype(BF16)
    vst = [kv[rows[u], 2 * KV_WIDTH:3 * KV_WIDTH].T for u in tiles]
    vwt = [kv[rows[u], 3 * KV_WIDTH:4 * KV_WIDTH].T for u in tiles]
    gts = [jax.nn.sigmoid(gt[rows[u], :].T[0:32, :]) for u in tiles]
    pos0 = (pl.program_id(0) % steps_per_row) * n_tok
    pos = pos0 + lax.broadcasted_iota(jnp.int32, (n_tok, HEAD_DIM), 0)
    col = lax.broadcasted_iota(jnp.int32, (n_tok, HEAD_DIM), 1)
    kpos = jnp.where(col < 3, pos // SLC_LEN, jnp.where(col < 6, pos % SLC_LEN, 0)).astype(F32)
    ones_row = (lax.broadcasted_iota(jnp.int32, (V_ROWS - HEAD_DIM, Q_BLOCK), 0) == 0).astype(F32)
    ks = _group_rms(kv[:, 0:KV_WIDTH], kg_ref[0:1, :], ones_ref)
    kw = _group_rms(kv[:, KV_WIDTH:2 * KV_WIDTH], kg_ref[1:2, :], ones_ref)
    for g in range(NSA_KV_HEADS):
        sl = slice(g * HEAD_DIM, (g + 1) * HEAD_DIM)
        ks_ref[g] = jnp.concatenate([ks[:, sl], kpos], axis=1).astype(BF16)
        kw_ref[g] = jnp.concatenate([kw[:, sl], kpos], axis=1).astype(BF16)
        for u in tiles:
            vsT_ref[g, :, rows[u]] = jnp.concatenate([vst[u][sl, :], ones_row], axis=0).astype(BF16)
            vwT_ref[g, :, rows[u]] = jnp.concatenate([vwt[u][sl, :], ones_row], axis=0).astype(BF16)
    for u in tiles:
        gT_ref[u] = gts[u]
    ret_ref[...] = _dot(xn, w_ref[:, _C_RET:_C_END])
    kc_ref[...] = _dot(xn, w_ref[:, _C_KC:_C_VC])
    vc_ref[...] = _dot(xn, w_ref[:, _C_VC:_C_GT])


def _inproj(x2, g, w, qg, kg, B, T, tm=1024):
    n, d = x2.shape
    nq = T // Q_BLOCK
    G = NSA_KV_HEADS
    tiles = tm // Q_BLOCK
    spr = T // tm
    assert T % tm == 0
    const = lambda i: (0, 0)
    row = lambda i: (i, 0)
    return pl.pallas_call(
        functools.partial(_inproj_kernel, steps_per_row=spr),
        grid=(n // tm,),
        in_specs=[pl.BlockSpec((tm, d), row),
                  pl.BlockSpec((1, d), const),
                  pl.BlockSpec((d, _C_END), const),
                  pl.BlockSpec((1, NSA_WIDTH), const),
                  pl.BlockSpec((2, KV_WIDTH), const),
                  pl.BlockSpec((NSA_WIDTH, NSA_WIDTH), const)],
        out_specs=[pl.BlockSpec((tm, KV_WIDTH), row),
                   pl.BlockSpec((tm, KV_WIDTH), row),
                   pl.BlockSpec((tm, _RET_COLS), row),
                   pl.BlockSpec((None, G, tiles, HEAD_DIM, GQ), lambda i: (i // spr, 0, i % spr, 0, 0)),
                   pl.BlockSpec((None, G, tm, 2 * HEAD_DIM), lambda i: (i // spr, 0, i % spr, 0)),
                   pl.BlockSpec((None, G, tm, 2 * HEAD_DIM), lambda i: (i // spr, 0, i % spr, 0)),
                   pl.BlockSpec((None, G, V_ROWS, tm), lambda i: (i // spr, 0, 0, i % spr)),
                   pl.BlockSpec((None, G, V_ROWS, tm), lambda i: (i // spr, 0, 0, i % spr)),
                   pl.BlockSpec((None, tiles, 32, Q_BLOCK), lambda i: (i // spr, i % spr, 0, 0))],
        out_shape=[jax.ShapeDtypeStruct((n, KV_WIDTH), F32),
                   jax.ShapeDtypeStruct((n, KV_WIDTH), F32),
                   jax.ShapeDtypeStruct((n, _RET_COLS), F32),
                   jax.ShapeDtypeStruct((B, G, nq, HEAD_DIM, GQ), BF16),
                   jax.ShapeDtypeStruct((B, G, T, 2 * HEAD_DIM), BF16),
                   jax.ShapeDtypeStruct((B, G, T, 2 * HEAD_DIM), BF16),
                   jax.ShapeDtypeStruct((B, G, V_ROWS, T), BF16),
                   jax.ShapeDtypeStruct((B, G, V_ROWS, T), BF16),
                   jax.ShapeDtypeStruct((B, nq, 32, Q_BLOCK), F32)],
        compiler_params=_params(1),
        name="inproj",
    )(x2, g, w, jnp.tile(qg, (1, NSA_HEADS)), jnp.tile(kg, (1, NSA_KV_HEADS)), _group_ones())


def _compress_kernel(kc_ref, vc_ref, wk_ref, wv_ref, pk_ref, pv_ref, kg_ref, kcmp_ref, vcT_ref):
    ncp = kc_ref.shape[0] // CMP_STRIDE

    def comp(a_ref, w_ref, p_ref):
        lo = jnp.zeros((ncp, KV_WIDTH), F32)
        hi = jnp.zeros((ncp, KV_WIDTH), F32)
        for l in range(CMP_STRIDE):
            a = a_ref[pl.ds(l, ncp, stride=CMP_STRIDE), :]
            lo += _dot((a + p_ref[0, l:l + 1, :]).astype(BF16), w_ref[0, l])
            hi += _dot((a + p_ref[1, l:l + 1, :]).astype(BF16), w_ref[1, l])
        return lo + pltpu.roll(hi, ncp - 1, 0)

    k = comp(kc_ref, wk_ref, pk_ref)
    v = comp(vc_ref, wv_ref, pv_ref).T
    cend = lax.broadcasted_iota(jnp.int32, (ncp, HEAD_DIM), 0) * CMP_STRIDE + (CMP_LEN - 1)
    col = lax.broadcasted_iota(jnp.int32, (ncp, HEAD_DIM), 1)
    kpos = jnp.where(col < 3, cend // SLC_LEN, jnp.where(col < 6, cend % SLC_LEN, 0)).astype(F32)
    for g in range(NSA_KV_HEADS):
        sl = slice(g * HEAD_DIM, (g + 1) * HEAD_DIM)
        kcmp_ref[g] = jnp.concatenate([_rms(k[:, sl], kg_ref[...]), kpos], axis=1).astype(BF16)
        vcT_ref[g] = v[sl, :].astype(BF16)


def _compress(kc, vc, wk, wv, pk, pv, kg, B, T):
    ncp = T // CMP_STRIDE
    G = NSA_KV_HEADS
    const4 = lambda b: (0, 0, 0, 0)
    const3 = lambda b: (0, 0, 0)
    const2 = lambda b: (0, 0)
    return pl.pallas_call(
        _compress_kernel,
        grid=(B,),
        in_specs=[pl.BlockSpec((T, KV_WIDTH), lambda b: (b, 0)),
                  pl.BlockSpec((T, KV_WIDTH), lambda b: (b, 0)),
                  pl.BlockSpec((2, CMP_STRIDE, KV_WIDTH, KV_WIDTH), const4),
                  pl.BlockSpec((2, CMP_STRIDE, KV_WIDTH, KV_WIDTH), const4),
                  pl.BlockSpec((2, CMP_STRIDE, KV_WIDTH), const3),
                  pl.BlockSpec((2, CMP_STRIDE, KV_WIDTH), const3),
                  pl.BlockSpec((1, HEAD_DIM), const2)],
        out_specs=[pl.BlockSpec((None, G, ncp, 2 * HEAD_DIM), lambda b: (b, 0, 0, 0)),
                   pl.BlockSpec((None, G, HEAD_DIM, ncp), lambda b: (b, 0, 0, 0))],
        out_shape=[jax.ShapeDtypeStruct((B, G, ncp, 2 * HEAD_DIM), BF16),
                   jax.ShapeDtypeStruct((B, G, HEAD_DIM, ncp), BF16)],
        compiler_params=_params(1),
        name="nsa_compress",
    )(kc, vc, wk, wv, pk, pv, kg)


def _compress_weights(w, pos):
    G = NSA_KV_HEADS
    w4 = w.reshape(2, CMP_STRIDE, HEAD_DIM, HEAD_DIM)
    eye = jnp.eye(G, dtype=w.dtype)
    wbd = jnp.einsum('hlde,gk->hlgdke', w4, eye).reshape(2, CMP_STRIDE, KV_WIDTH, KV_WIDTH)
    p = pos.reshape(2, CMP_STRIDE, 1, HEAD_DIM)
    p = jnp.broadcast_to(p, (2, CMP_STRIDE, G, HEAD_DIM)).reshape(2, CMP_STRIDE, KV_WIDTH)
    return wbd.astype(BF16), p


def _split3(x):
    hi = x.astype(BF16)
    r = x - hi.astype(F32)
    mid = r.astype(BF16)
    lo = (r - mid.astype(F32)).astype(BF16)
    return hi, mid, lo


def _nsa_cmp_kernel(qT_ref, qaug_ref, kc_ref, vcT_ref, ovT_ref, ocmp_ref, sel_ref, flag_ref, *, n_sel):
    ncp = kc_ref.shape[0]
    ns = ovT_ref.shape[0]
    tiles = range(CMP_TILES)
    i0 = pl.program_id(2) * CMP_TILES
    lane = lax.broadcasted_iota(jnp.int32, (1, GQ), 1)
    q = [jnp.concatenate([qT_ref[u], qaug_ref[...]], axis=0) for u in tiles]
    t_row = [(i0 + u) * Q_BLOCK + (lane & (Q_BLOCK - 1)) for u in tiles]
    has_cmp = [(t_row[u] >= CMP_LEN - 1).astype(F32) for u in tiles]
    tq = [(i0 + u) * Q_BLOCK + lax.broadcasted_iota(jnp.int32, (1, Q_BLOCK), 1) for u in tiles]
    cur = [tq[u] // SLC_LEN for u in tiles]

    def prefix(rows):
        nsk = rows * CMP_STRIDE // SLC_LEN
        tail0 = max(rows - CMP_TAIL, 0)
        kc = kc_ref[0:rows, :]
        s = [_dot(kc, q[u]) for u in tiles]
        cend = (lax.broadcasted_iota(jnp.int32, (rows - tail0, 1), 0) + tail0) * CMP_STRIDE + (CMP_LEN - 1)
        tail = [jnp.where(t_row[u] >= cend, s[u][tail0:], NEG) for u in tiles]
        s = [jnp.concatenate([s[u][0:tail0], tail[u]], axis=0) if tail0 else tail[u] for u in tiles]
        m = [jnp.max(s[u], axis=0, keepdims=True) for u in tiles]
        e = [jnp.exp2(s[u] - m[u]) for u in tiles]
        p = [e[u] * (has_cmp[u] / jnp.sum(e[u], axis=0, keepdims=True)) for u in tiles]
        vc = vcT_ref[:, 0:rows]
        for u in tiles:
            ocmp_ref[u] = _dot(vc, p[u].astype(BF16))

        ps = [p[u][:, 0:Q_BLOCK] for u in tiles]
        for r in range(1, NSA_GROUP):
            ps = [ps[u] + p[u][:, r * Q_BLOCK:(r + 1) * Q_BLOCK] for u in tiles]
        ov = ovT_ref[0:nsk, 0:rows]
        split = [_split3(ps[u]) for u in tiles]
        imp = [_dot(ov, split[u][0]) + _dot(ov, split[u][1]) + _dot(ov, split[u][2]) for u in tiles]

        blk = lax.broadcasted_iota(jnp.int32, (nsk, 1), 0)
        forced = [(blk == 0) | (blk == cur[u]) | (blk == cur[u] - 1) for u in tiles]
        valid = [blk * SLC_LEN <= tq[u] for u in tiles]
        imp = [jnp.where(forced[u], -3e38, jnp.where(valid[u], imp[u], -BIG)) for u in tiles]
        blk_f = blk.astype(F32)
        sel = [forced[u].astype(F32) for u in tiles]
        for _ in range(n_sel - N_FORCED):
            mx = [jnp.max(imp[u], axis=0, keepdims=True) for u in tiles]
            idx = [jnp.min(jnp.where(imp[u] == mx[u], blk_f, float(ns)), axis=0, keepdims=True) for u in tiles]
            pick = [blk_f == idx[u] for u in tiles]
            sel = [jnp.where(pick[u], 1.0, sel[u]) for u in tiles]
            imp = [jnp.where(pick[u], -3e38, imp[u]) for u in tiles]
        ones = jnp.ones((8, Q_BLOCK), BF16)
        for u in tiles:
            sel_ref[u, 0:nsk, :] = sel[u]
            cnt = _dot_nt(ones, sel[u].astype(BF16))
            flag_ref[u, :, 0:nsk] = (cnt > 0).astype(jnp.int32)
            if nsk < ns:
                sel_ref[u, nsk:, :] = jnp.zeros((ns - nsk, Q_BLOCK), F32)
                flag_ref[u, :, nsk:] = jnp.zeros((8, ns - nsk), jnp.int32)

    n_variants = ncp // CMP_CHUNK
    last = i0 + CMP_TILES - 1
    variant = (last * (Q_BLOCK // CMP_STRIDE) + (Q_BLOCK // CMP_STRIDE - 2)) // CMP_CHUNK
    for k in range(n_variants):
        pl.when(variant == k)(functools.partial(prefix, (k + 1) * CMP_CHUNK))


def _nsa_cmp(qT, qaug, kcmp, vcT, ovT, B, T):
    G = NSA_KV_HEADS
    nq = T // Q_BLOCK
    ncp = T // CMP_STRIDE
    ns = T // SLC_LEN
    n_sel = min(SLC_TOPK, ns)
    assert ncp % CMP_CHUNK == 0 and n_sel > N_FORCED and nq % CMP_TILES == 0
    tile = lambda b, g, i: (b, g, i, 0, 0)
    return pl.pallas_call(
        functools.partial(_nsa_cmp_kernel, n_sel=n_sel),
        grid=(B, G, nq // CMP_TILES),
        in_specs=[pl.BlockSpec((None, None, CMP_TILES, HEAD_DIM, GQ), tile),
                  pl.BlockSpec((None, HEAD_DIM, GQ), lambda b, g, i: (g, 0, 0)),
                  pl.BlockSpec((None, None, ncp, 2 * HEAD_DIM), lambda b, g, i: (b, g, 0, 0)),
                  pl.BlockSpec((None, None, HEAD_DIM, ncp), lambda b, g, i: (b, g, 0, 0)),
                  pl.BlockSpec((ns, ncp), lambda b, g, i: (0, 0))],
        out_specs=[pl.BlockSpec((None, None, CMP_TILES, HEAD_DIM, GQ), tile),
                   pl.BlockSpec((None, None, CMP_TILES, ns, Q_BLOCK), tile),
                   pl.BlockSpec((None, None, CMP_TILES, 8, ns), tile)],
        out_shape=[jax.ShapeDtypeStruct((B, G, nq, HEAD_DIM, GQ), F32),
                   jax.ShapeDtypeStruct((B, G, nq, ns, Q_BLOCK), F32),
                   jax.ShapeDtypeStruct((B, G, nq, 8, ns), jnp.int32)],
        compiler_params=_params(3),
        name="nsa_cmp",
    )(qT, qaug, kcmp, vcT, ovT)


def _nsa_main_kernel(list_ref, cnt_ref, qT_ref, qaug_ref, ks_ref, vsT_ref, kw_ref, vwT_ref, sel_ref, gT_ref, ocmp_ref,
                     lowb_ref, causb_ref, out_ref, m_sc, acc_sc, win_sc):
    b, g = pl.program_id(0), pl.program_id(1)
    tiles = range(MAIN_TILES)
    i = [pl.program_id(2) * MAIN_TILES + u for u in tiles]
    tile_id = [(b * pl.num_programs(1) + g) * (pl.num_programs(2) * MAIN_TILES) + i[u] for u in tiles]
    n_steps = sel_ref.shape[1] // 2
    q = [jnp.concatenate([qT_ref[u], qaug_ref[...]], axis=0) for u in tiles]
    k0 = [pl.multiple_of(i[u] * Q_BLOCK, Q_BLOCK) for u in tiles]

    def sel_bias(u, j, valid):
        def row(r):
            picked = (sel_ref[u, pl.ds(r, 1), :] > 0.5) & valid
            return jnp.concatenate([jnp.where(picked, 0.0, NEG)] * NSA_GROUP, axis=1)
        return row(2 * j), row(2 * j + 1)

    def add_sel_bias(s, ba, bb):
        return jnp.concatenate([s[0:SLC_LEN] + ba, s[SLC_LEN:] + bb], axis=0)

    lowb = jnp.concatenate([lowb_ref[...]] * NSA_GROUP, axis=1)
    causb = jnp.concatenate([causb_ref[...]] * NSA_GROUP, axis=1)

    bias_d = [sel_bias(u, i[u], True) for u in tiles]
    sd = [_dot(ks_ref[pl.ds(k0[u], KEY_STEP), :], q[u]) for u in tiles]
    sw = [_dot(kw_ref[pl.ds(k0[u], WIN_KEYS), :], q[u]) for u in tiles]
    sd = [add_sel_bias(sd[u], *bias_d[u]) + causb for u in tiles]
    sw = [jnp.concatenate([sw[u][0:Q_BLOCK] + lowb, sw[u][Q_BLOCK:WIN], sw[u][WIN:] + causb], axis=0) for u in tiles]
    md = [jnp.max(sd[u], axis=0, keepdims=True) for u in tiles]
    mw = [jnp.max(sw[u], axis=0, keepdims=True) for u in tiles]
    accd = [_dot(vsT_ref[:, pl.ds(k0[u], KEY_STEP)], jnp.exp2((sd[u] - md[u]).astype(BF16))) for u in tiles]
    ow = [_dot(vwT_ref[:, pl.ds(k0[u], WIN_KEYS)], jnp.exp2((sw[u] - mw[u]).astype(BF16))) for u in tiles]
    for u in tiles:
        m_sc[u] = md[u]
        acc_sc[u] = accd[u]
        win_sc[u] = ow[u][0:HEAD_DIM] / ow[u][HEAD_DIM:HEAD_DIM + 1]

    def scores(u, t):
        ks, vs, biases = [], [], []
        for x in range(STEP_GROUP):
            j = list_ref[tile_id[u] * n_steps + t * STEP_GROUP + x]
            valid = j >= 0
            j = jnp.maximum(j, 0)
            kj = pl.multiple_of(j * KEY_STEP, KEY_STEP)
            ks.append(ks_ref[pl.ds(kj, KEY_STEP), :])
            vs.append(vsT_ref[:, pl.ds(kj, KEY_STEP)])
            biases.append(sel_bias(u, j, valid))
        s = _dot(jnp.concatenate(ks, axis=0), q[u])
        s = jnp.concatenate([add_sel_bias(s[x * KEY_STEP:(x + 1) * KEY_STEP], *biases[x])
                             for x in range(STEP_GROUP)], axis=0)
        return s, jnp.max(s, axis=0, keepdims=True), jnp.concatenate(vs, axis=1)

    def accumulate(u, s, smax, vcat):
        m_old = m_sc[u]
        m_new = jnp.maximum(m_old, smax)
        alpha = jnp.exp2(m_old - m_new)
        acc_sc[u] = alpha * acc_sc[u] + _dot(vcat, jnp.exp2((s - m_new).astype(BF16)))
        m_sc[u] = m_new

    def run(work, t, carry):
        staged = [(u, scores(u, t * mult + off)) for (u, mult, off) in work]
        for u, args in staged:
            accumulate(u, *args)
        return carry

    for u0 in range(0, MAIN_TILES, LOOP_TILES):
        us = range(u0, u0 + LOOP_TILES)
        n_groups = functools.reduce(
            jnp.maximum, [(cnt_ref[tile_id[u]] + (STEP_GROUP - 1)) // STEP_GROUP for u in us])
        lax.fori_loop(0, n_groups // 2, functools.partial(run, [(u, 2, off) for off in (0, 1) for u in us]), 0)
        lax.fori_loop(n_groups // 2 * 2, n_groups, functools.partial(run, [(u, 1, 0) for u in us]), 0)

    def gate(u, k):
        rows = [gT_ref[u, pl.ds(g * (NSA_GROUP * 3) + r * 3 + k, 1), :] for r in range(NSA_GROUP)]
        return jnp.concatenate(rows, axis=1)

    o_slc = [acc_sc[u, 0:HEAD_DIM, :] / acc_sc[u, HEAD_DIM:HEAD_DIM + 1, :] for u in tiles]
    o = [gate(u, 0) * ocmp_ref[u] + gate(u, 1) * o_slc[u] + gate(u, 2) * win_sc[u] for u in tiles]
    o = [jnp.concatenate([o[u], jnp.zeros_like(o[u])], axis=0) for u in tiles]
    for r in range(NSA_GROUP):
        ot = [o[u][:, r * Q_BLOCK:(r + 1) * Q_BLOCK].T[:, 0:HEAD_DIM] for u in tiles]
        for u in tiles:
            out_ref[u * Q_BLOCK:(u + 1) * Q_BLOCK, r * HEAD_DIM:(r + 1) * HEAD_DIM] = ot[u]


def _nsa_steps_kernel(flagT_ref, pairT_ref, list_ref, cnt_ref, *, nq):
    n_steps, nt = list_ref.shape
    need = _dot(pairT_ref[...], flagT_ref[...].astype(BF16)) > 0
    step = lax.broadcasted_iota(jnp.int32, (n_steps, 1), 0)
    own = lax.broadcasted_iota(jnp.int32, (1, nt), 1) % nq
    need = need & (step < own)
    need_f = need.astype(F32)
    earlier = (lax.broadcasted_iota(jnp.int32, (n_steps, n_steps), 1) < step).astype(BF16)
    slot = _dot(earlier, need_f.astype(BF16))
    total = jnp.sum(need_f, axis=0, keepdims=True)
    cnt_ref[...] = jnp.broadcast_to(total, cnt_ref.shape).astype(jnp.int32)
    step_f = step.astype(F32)
    for p in range(n_steps):
        val = jnp.sum(jnp.where(need & (slot == p), step_f, 0.0), axis=0, keepdims=True)
        list_ref[p:p + 1, :] = jnp.where(total > p, val, -1.0).astype(jnp.int32)


def _nsa_steps(flags, nq):
    nt, ns = flags.shape
    n_steps = ns // 2
    pairT = jnp.asarray(np.arange(n_steps)[:, None] == np.arange(ns)[None, :] // 2, BF16)
    lists, counts = pl.pallas_call(
        functools.partial(_nsa_steps_kernel, nq=nq),
        out_shape=[jax.ShapeDtypeStruct((n_steps, nt), jnp.int32), jax.ShapeDtypeStruct((8, nt), jnp.int32)],
        name="nsa_steps",
    )(flags.T.astype(F32), pairT)
    return lists.T.reshape(-1), counts[0]


def _nsa_main(lists, counts, qT, qaug, ks, vsT, kw, vwT, sel, gT, ocmp, lowb, causb, B, T):
    G = NSA_KV_HEADS
    nq = T // Q_BLOCK
    ns = T // SLC_LEN
    whole = lambda b, g, i, *_: (b, g, 0, 0)
    tile = lambda b, g, i, *_: (b, g, i, 0, 0)
    const = lambda b, g, i, *_: (0, 0)
    grid_spec = pltpu.PrefetchScalarGridSpec(
        num_scalar_prefetch=2,
        grid=(B, G, nq // MAIN_TILES),
        in_specs=[pl.BlockSpec((None, None, MAIN_TILES, HEAD_DIM, GQ), tile),
                  pl.BlockSpec((None, HEAD_DIM, GQ), lambda b, g, i, *_: (g, 0, 0)),
                  pl.BlockSpec((None, None, T, 2 * HEAD_DIM), whole),
                  pl.BlockSpec((None, None, V_ROWS, T), whole),
                  pl.BlockSpec((None, None, T + WIN, 2 * HEAD_DIM), whole),
                  pl.BlockSpec((None, None, V_ROWS, T + WIN), whole),
                  pl.BlockSpec((None, None, MAIN_TILES, ns, Q_BLOCK), tile),
                  pl.BlockSpec((None, MAIN_TILES, 32, Q_BLOCK), lambda b, g, i, *_: (b, i, 0, 0)),
                  pl.BlockSpec((None, None, MAIN_TILES, HEAD_DIM, GQ), tile),
                  pl.BlockSpec((Q_BLOCK, Q_BLOCK), const),
                  pl.BlockSpec((Q_BLOCK, Q_BLOCK), const)],
        out_specs=pl.BlockSpec((None, MAIN_TILES * Q_BLOCK, NSA_GROUP * HEAD_DIM), lambda b, g, i, *_: (b, i, g)),
        scratch_shapes=[pltpu.VMEM((MAIN_TILES, 1, GQ), F32), pltpu.VMEM((MAIN_TILES, V_ROWS, GQ), F32),
                        pltpu.VMEM((MAIN_TILES, HEAD_DIM, GQ), F32)],
    )
    return pl.pallas_call(
        _nsa_main_kernel,
        grid_spec=grid_spec,
        out_shape=jax.ShapeDtypeStruct((B, T, NSA_WIDTH), F32),
        compiler_params=_params(3),
        name="nsa_main",
    )(lists, counts, qT, qaug, ks, vsT, kw, vwT, sel, gT, ocmp, lowb, causb)


def _ret_kernel(p_ref, decay_ref, xi_ref, zeta_ref, gch_ref, ng_ref, ones_ref, out_ref, state_ref):
    @pl.when(pl.program_id(0) == 0)
    def _():
        state_ref[...] = jnp.zeros(state_ref.shape, F32)

    rows = range(p_ref.shape[0])
    kw = RET_HEADS * RET_DK
    p = [p_ref[b] for b in rows]
    rq = [p[b][:, 0:kw] * (RET_DK ** -0.5) for b in rows]
    rk = [p[b][:, kw:2 * kw] for b in rows]
    rkT = [rk[b].T for b in rows]
    rv = [p[b][:, 2 * kw:2 * kw + RET_WIDTH] for b in rows]
    xi = xi_ref[...]
    outs = [[] for _ in rows]
    for h in range(RET_HEADS):
        dk = slice(h * RET_DK, (h + 1) * RET_DK)
        dv = slice(h * RET_DV, (h + 1) * RET_DV)
        st = [state_ref[b, h] for b in rows]
        inner = [_dot_nt(rq[b][:, dk], rk[b][:, dk]) * decay_ref[h] for b in rows]
        o = [_dot(inner[b], rv[b][:, dv]) + _dot(rq[b][:, dk], st[b]) * xi[:, h:h + 1] for b in rows]
        for b in rows:
            state_ref[b, h] = (st[b] * gch_ref[h:h + 1, 0:1]
                               + _dot(rkT[b][dk, :] * zeta_ref[h:h + 1, :], rv[b][:, dv]))
            outs[b].append(o[b])
    normed = [_group_rms(jnp.concatenate(outs[b], axis=1), ng_ref[...], ones_ref) for b in rows]
    for b in rows:
        rg = p[b][:, 2 * kw + RET_WIDTH:2 * kw + 2 * RET_WIDTH]
        out_ref[b] = normed[b] * (rg * jax.nn.sigmoid(rg))


def _ret_consts():
    H, C = RET_HEADS, RET_CHUNK
    log_g = np.log1p(-np.exp2(-5.0 - np.arange(H, dtype=np.float64)))
    idx = np.arange(C, dtype=np.float64)
    diff = idx[:, None] - idx[None, :]
    decay = np.where(diff >= 0, np.exp(np.maximum(diff, 0.0) * log_g[:, None, None]), 0.0)
    zeta = np.exp((C - 1 - idx) * log_g[:, None])
    xi = np.exp((idx + 1) * log_g[:, None]).T
    g_chunk = np.broadcast_to(np.exp(C * log_g)[:, None], (H, LANES))
    return tuple(jnp.asarray(a, F32) for a in (decay, xi, zeta, g_chunk))


def _retention(pret, ng, B, T):
    nch = T // RET_CHUNK
    decay, xi, zeta, gch = _ret_consts()
    c2 = lambda c: (0, 0)
    out = pl.pallas_call(
        _ret_kernel,
        grid=(nch,),
        in_specs=[pl.BlockSpec((B, RET_CHUNK, _RET_COLS), lambda c: (0, c, 0)),
                  pl.BlockSpec((RET_HEADS, RET_CHUNK, RET_CHUNK), lambda c: (0, 0, 0)),
                  pl.BlockSpec((RET_CHUNK, RET_HEADS), c2),
                  pl.BlockSpec((RET_HEADS, RET_CHUNK), c2),
                  pl.BlockSpec((RET_HEADS, LANES), c2),
                  pl.BlockSpec((1, RET_WIDTH), c2),
                  pl.BlockSpec((RET_WIDTH, RET_WIDTH), c2)],
        out_specs=pl.BlockSpec((B, RET_CHUNK, RET_WIDTH), lambda c: (0, c, 0)),
        out_shape=jax.ShapeDtypeStruct((B, T, RET_WIDTH), F32),
        scratch_shapes=[pltpu.VMEM((B, RET_HEADS, RET_DK, RET_DV), F32)],
        compiler_params=_params(1),
        name="retention",
    )(pret.reshape(B, T, _RET_COLS), decay, xi, zeta, gch, ng, _group_ones())
    return out.reshape(B * T, RET_WIDTH)


def _mixer_residual(x_ref, nsa_ref, ret_ref, wo_ref):
    return (x_ref[...] + _dot(nsa_ref[...].astype(BF16), wo_ref[0:NSA_WIDTH, :])
            + _dot(ret_ref[...].astype(BF16), wo_ref[NSA_WIDTH:, :]))


def _mixer_specs(tm, d, row, const):
    return [pl.BlockSpec((tm, d), row), pl.BlockSpec((tm, NSA_WIDTH), row), pl.BlockSpec((tm, RET_WIDTH), row),
            pl.BlockSpec((NSA_WIDTH + RET_WIDTH, d), const)]


def _ffn_kernel(x_ref, nsa_ref, ret_ref, wo_ref, g_ref, wg_ref, wu_ref, wd_ref, o_ref, h_sc):
    f = pl.program_id(1)

    @pl.when(f == 0)
    def _():
        x = _mixer_residual(x_ref, nsa_ref, ret_ref, wo_ref)
        h_sc[...] = _rms(x, g_ref[...]).astype(BF16)
        o_ref[...] = x

    h = h_sc[...]
    a = _dot(h, wg_ref[...])
    act = (a * jax.nn.sigmoid(a) * _dot(h, wu_ref[...])).astype(BF16)
    o_ref[...] += _dot(act, wd_ref[...])


def _ffn(x2, nsa, ret, wo, g, wg, wu, wd, tm=1024, fc=1408):
    n, d = x2.shape
    dff = wg.shape[1]
    return pl.pallas_call(
        _ffn_kernel,
        grid=(n // tm, dff // fc),
        in_specs=_mixer_specs(tm, d, lambda i, f: (i, 0), lambda i, f: (0, 0)) + [
            pl.BlockSpec((1, d), lambda i, f: (0, 0)),
            pl.BlockSpec((d, fc), lambda i, f: (0, f)),
            pl.BlockSpec((d, fc), lambda i, f: (0, f)),
            pl.BlockSpec((fc, d), lambda i, f: (f, 0))],
        out_specs=pl.BlockSpec((tm, d), lambda i, f: (i, 0)),
        out_shape=jax.ShapeDtypeStruct((n, d), F32),
        scratch_shapes=[pltpu.VMEM((tm, d), BF16)],
        compiler_params=_params(2),
        name="ffn_dense",
    )(x2, nsa, ret, wo, g, wg, wu, wd)


def _router_kernel(x_ref, nsa_ref, ret_ref, wo_ref, g_ref, r_ref, rb_ref, tri_ref,
                   x1_ref, h_ref, rank_ref, comb_ref, rankT_ref, cnt_ref):
    x = _mixer_residual(x_ref, nsa_ref, ret_ref, wo_ref)
    x1_ref[...] = x
    h = _rms(x, g_ref[...])
    h_ref[...] = h.astype(BF16)
    hh, hm, _ = _split3(h)
    rh, rm, _ = _split3(r_ref[...])
    logits = _dot(hh, rh) + (_dot(hh, rm) + _dot(hm, rh)) + rb_ref[...]
    lane = lax.broadcasted_iota(jnp.int32, logits.shape, 1).astype(F32)
    logits = jnp.where(lane < N_EXPERTS, logits, NEG)
    m1 = jnp.max(logits, axis=1, keepdims=True)
    i1 = jnp.min(jnp.where(logits == m1, lane, float(LANES)), axis=1, keepdims=True)
    l2 = jnp.where(lane == i1, NEG, logits)
    m2 = jnp.max(l2, axis=1, keepdims=True)
    i2 = jnp.min(jnp.where(l2 == m2, lane, float(LANES)), axis=1, keepdims=True)
    e2 = jnp.exp(m2 - m1)
    w1 = 1.0 / (1.0 + e2)
    w2 = e2 / (1.0 + e2)
    use1, use2 = lane == i1, lane == i2
    comb_ref[...] = jnp.where(use1, w1, 0.0) + jnp.where(use2, w2, 0.0)
    use = (use1 | use2).astype(F32)
    rank = jnp.where(use > 0, _dot(tri_ref[...], use.astype(BF16)), -1.0)
    rank_ref[...] = rank
    rankT_ref[...] = rank.T[0:N_EXPERTS, :]
    cnt_ref[...] = jnp.broadcast_to(jnp.sum(use, axis=0, keepdims=True), cnt_ref.shape).astype(jnp.int32)


def _router(x2, nsa, ret, wo, g, router, rb, tm):
    n, d = x2.shape
    nt = n // tm
    tri = jnp.asarray(np.arange(tm)[:, None] > np.arange(tm)[None, :], BF16)
    rpad = jnp.zeros((d, LANES), F32).at[:, :N_EXPERTS].set(router)
    rbpad = jnp.zeros((1, LANES), F32).at[0, :N_EXPERTS].set(rb)
    c2 = lambda i: (0, 0)
    row = lambda i: (i, 0)
    return pl.pallas_call(
        _router_kernel,
        grid=(nt,),
        in_specs=_mixer_specs(tm, d, row, c2) + [
            pl.BlockSpec((1, d), c2),
            pl.BlockSpec((d, LANES), c2),
            pl.BlockSpec((1, LANES), c2),
            pl.BlockSpec((tm, tm), c2)],
        out_specs=[pl.BlockSpec((tm, d), row),
                   pl.BlockSpec((tm, d), row),
                   pl.BlockSpec((tm, LANES), row),
                   pl.BlockSpec((tm, LANES), row),
                   pl.BlockSpec((N_EXPERTS, tm), lambda i: (0, i)),
                   pl.BlockSpec((None, 8, LANES), lambda i: (i, 0, 0))],
        out_shape=[jax.ShapeDtypeStruct((n, d), F32),
                   jax.ShapeDtypeStruct((n, d), BF16),
                   jax.ShapeDtypeStruct((n, LANES), F32),
                   jax.ShapeDtypeStruct((n, LANES), F32),
                   jax.ShapeDtypeStruct((N_EXPERTS, n), F32),
                   jax.ShapeDtypeStruct((nt, 8, LANES), jnp.int32)],
        compiler_params=_params(1),
        name="moe_router",
    )(x2, nsa, ret, wo, g, rpad, rbpad, tri)


MOE_SUB = 144
MOE_MOVE = 2 * MOE_SUB


def _moe_kernel(cnt_ref, h_ref, rankT_ref, rank_ref, comb_ref, wg_ref, wu_ref, wd_ref, x_ref, o_ref, hc_sc, oacc_sc):
    t, e, f = pl.program_id(0), pl.program_id(1), pl.program_id(2)
    nf = pl.num_programs(2)
    tm = h_ref.shape[0]
    nsub = (cnt_ref[t * N_EXPERTS + e] + (MOE_SUB - 1)) // MOE_SUB
    nmove = (nsub + 1) // 2

    @pl.when((e == 0) & (f == 0))
    def _():
        o_ref[...] = x_ref[...]

    @pl.when(f == 0)
    def _():
        rank_row = rankT_ref[...]

        def gather(s, c):
            r0 = pl.multiple_of(s * MOE_MOVE, MOE_MOVE)
            rows = (lax.broadcasted_iota(jnp.int32, (MOE_MOVE, 1), 0) + r0).astype(F32)
            onehot = (rows == rank_row).astype(BF16)
            hc_sc[pl.ds(r0, MOE_MOVE), :] = _dot(onehot, h_ref[...]).astype(BF16)
            oacc_sc[pl.ds(r0, MOE_MOVE), :] = jnp.zeros((MOE_MOVE, oacc_sc.shape[1]), F32)
            return c

        lax.fori_loop(0, nmove, gather, 0)

    def expert(n_rows, s, c):
        r0 = pl.multiple_of(s * n_rows, n_rows)
        rows = hc_sc[pl.ds(r0, n_rows), :]
        a = _dot(rows, wg_ref[...])
        act = (a * jax.nn.sigmoid(a) * _dot(rows, wu_ref[...])).astype(BF16)
        oacc_sc[pl.ds(r0, n_rows), :] += _dot(act, wd_ref[...])
        return c

    lax.fori_loop(0, nsub // 2, functools.partial(expert, MOE_MOVE), 0)
    lax.fori_loop(nsub // 2 * 2, nsub, functools.partial(expert, MOE_SUB), 0)

    @pl.when(f == nf - 1)
    def _():
        is_e = lax.broadcasted_iota(jnp.int32, (1, LANES), 1) == e
        rank_col = jnp.sum(jnp.where(is_e, rank_ref[...], 0.0), axis=1, keepdims=True)
        comb_col = jnp.sum(jnp.where(is_e, comb_ref[...], 0.0), axis=1, keepdims=True)

        def scatter(s, c):
            r0 = pl.multiple_of(s * MOE_MOVE, MOE_MOVE)
            cols = (lax.broadcasted_iota(jnp.int32, (1, MOE_MOVE), 1) + r0).astype(F32)
            onehot = (rank_col == cols).astype(BF16)
            y = _dot(onehot, oacc_sc[pl.ds(r0, MOE_MOVE), :].astype(BF16))
            o_ref[...] += comb_col * y
            return c

        lax.fori_loop(0, nmove, scatter, 0)


def _moe(counts, h, rankT, rank, comb, wg, wu, wd, x2, tm, fc=1408):
    n, d = x2.shape
    dff = wg.shape[2]
    rows_cap = pl.cdiv(pl.cdiv(tm, MOE_SUB), 2) * MOE_MOVE
    grid_spec = pltpu.PrefetchScalarGridSpec(
        num_scalar_prefetch=1,
        grid=(n // tm, N_EXPERTS, dff // fc),
        in_specs=[pl.BlockSpec((tm, d), lambda t, e, f, c: (t, 0)),
                  pl.BlockSpec((None, 1, tm), lambda t, e, f, c: (e, 0, t)),
                  pl.BlockSpec((tm, LANES), lambda t, e, f, c: (t, 0)),
                  pl.BlockSpec((tm, LANES), lambda t, e, f, c: (t, 0)),
                  pl.BlockSpec((None, d, fc), lambda t, e, f, c: (e, 0, f)),
                  pl.BlockSpec((None, d, fc), lambda t, e, f, c: (e, 0, f)),
                  pl.BlockSpec((None, fc, d), lambda t, e, f, c: (e, f, 0)),
                  pl.BlockSpec((tm, d), lambda t, e, f, c: (t, 0))],
        out_specs=pl.BlockSpec((tm, d), lambda t, e, f, c: (t, 0)),
        scratch_shapes=[pltpu.VMEM((rows_cap, d), BF16), pltpu.VMEM((rows_cap, d), F32)],
    )
    return pl.pallas_call(
        _moe_kernel,
        grid_spec=grid_spec,
        out_shape=jax.ShapeDtypeStruct((n, d), F32),
        compiler_params=_params(3),
        name="moe_experts",
    )(counts, h, rankT.reshape(N_EXPERTS, 1, n), rank, comb, wg, wu, wd, x2)


def _permute_w_in(w):
    o = np.cumsum((0, NSA_WIDTH) + (KV_WIDTH,) * 6 + (3 * NSA_HEADS,))
    q, kc, vc, ks, vs, kw, vw, gts = (w[:, o[k]:o[k + 1]] for k in range(8))
    ret = w[:, o[8]:]
    pad = jnp.zeros((w.shape[0], LANES - 3 * NSA_HEADS), w.dtype)
    return jnp.concatenate([q, ks, kw, vs, vw, kc, vc, gts, pad, ret], axis=1).astype(BF16)


def _nsa_consts(T):
    ncp = T // CMP_STRIDE
    ns = T // SLC_LEN
    cs = np.arange(ncp) * CMP_STRIDE
    ss = np.arange(ns) * SLC_LEN
    ov = np.clip(np.minimum(cs[None, :] + CMP_LEN, ss[:, None] + SLC_LEN) - np.maximum(cs[None, :], ss[:, None]), 0, None)
    ovT = (ov.astype(np.float32) / CMP_LEN)
    ovT[:, ncp - 1] = 0.0
    h = np.arange(NSA_HEADS).reshape(NSA_KV_HEADS, NSA_GROUP) + 1
    slopes = np.exp2(-8.0 * h / NSA_HEADS).astype(np.float32)
    slopes = np.repeat(slopes, Q_BLOCK, axis=1)
    parts, rest = [], np.float64(LOG2E)
    for _ in range(3):
        part = np.float64(np.asarray(rest).astype(BF16))
        parts.append(part)
        rest = rest - part
    qaug = np.zeros((NSA_KV_HEADS, HEAD_DIM, GQ), np.float32)
    for k, part in enumerate(parts):
        qaug[:, k, :] = part * SLC_LEN * slopes
        qaug[:, 3 + k, :] = part * slopes
    kq = np.arange(Q_BLOCK)[:, None] - np.arange(Q_BLOCK)[None, :]
    causb = np.where(kq <= 0, 0.0, NEG).astype(np.float32)
    lowb = np.where(kq > 0, 0.0, NEG).astype(np.float32)
    return jnp.asarray(ovT, BF16), jnp.asarray(qaug, BF16), jnp.asarray(lowb), jnp.asarray(causb)


def _mixer(x2, B, T, norm_g, w_in, q_norm_g, k_norm_g, cmp_pos, w_cmp, ret_norm_g, w_out):
    ns = T // SLC_LEN
    kc, vc, pret, qT, ks, kw, vsT, vwT, gT = _inproj(x2, norm_g[None, :], _permute_w_in(w_in), q_norm_g[None, :],
                                                     k_norm_g[1:3], B, T)
    wk, pk = _compress_weights(w_cmp[0], cmp_pos[0])
    wv, pv = _compress_weights(w_cmp[1], cmp_pos[1])
    kcmp, vcT = _compress(kc, vc, wk, wv, pk, pv, k_norm_g[0:1], B, T)
    ovT, qaug, lowb, causb = _nsa_consts(T)
    ocmp, sel, flags = _nsa_cmp(qT, qaug, kcmp, vcT, ovT, B, T)
    lists, counts = _nsa_steps(flags[:, :, :, 0, :].reshape(-1, ns), T // Q_BLOCK)
    kpad = jnp.zeros((WIN, 2 * HEAD_DIM), BF16).at[:, HEAD_DIM:HEAD_DIM + 3].set(-2.0 ** 100)
    kw = jnp.concatenate([jnp.broadcast_to(kpad, kw.shape[:2] + kpad.shape), kw], axis=2)
    vwT = jnp.pad(vwT, ((0, 0), (0, 0), (0, 0), (WIN, 0)))
    nsa = _nsa_main(lists, counts, qT, qaug, ks, vsT, kw, vwT, sel, gT, ocmp, lowb, causb, B, T)
    ret = _retention(pret, ret_norm_g[None, :], B, T)
    return nsa.reshape(B * T, NSA_WIDTH), ret, w_out.astype(BF16)


def _moe_layer(x2, nsa, ret, wo, norm_g, router, router_b, wg, wu, wd, tm=1024):
    tm = min(tm, x2.shape[0])
    x1, h, rank, comb, rankT, cnt = _router(x2, nsa, ret, wo, norm_g[None, :], router, router_b, tm)
    counts = cnt[:, 0, :N_EXPERTS].reshape(-1)
    return _moe(counts, h, rankT, rank, comb, wg.astype(BF16), wu.astype(BF16), wd.astype(BF16), x1, tm)


def kernel(x, norm_mix_g, w_in, q_norm_g, k_norm_g, cmp_pos, w_cmp, ret_norm_g, w_out, norm_ffn_g,
           ffn_w_gate, ffn_w_up, ffn_w_down, moe_router, moe_router_b, moe_w_gate, moe_w_up, moe_w_down):
    B, T, D = x.shape
    depth = norm_mix_g.shape[0]
    x2 = x.reshape(B * T, D)
    for l in range(depth):
        nsa, ret, wo = _mixer(x2, B, T, norm_mix_g[l], w_in[l], q_norm_g[l], k_norm_g[l], cmp_pos[l], w_cmp[l],
                              ret_norm_g[l], w_out[l])
        j = l // 2
        if l % 2 == 0:
            x2 = _ffn(x2, nsa, ret, wo, norm_ffn_g[l][None, :], ffn_w_gate[j].astype(BF16),
                      ffn_w_up[j].astype(BF16), ffn_w_down[j].astype(BF16))
        else:
            x2 = _moe_layer(x2, nsa, ret, wo, norm_ffn_g[l], moe_router[j], moe_router_b[j], moe_w_gate[j],
                            moe_w_up[j], moe_w_down[j])
    return x2.reshape(B, T, D)
```

```python
import functools

import numpy as np
import jax
import jax.numpy as jnp
from jax import lax
from jax.experimental import pallas as pl
from jax.experimental.pallas import tpu as pltpu

F32 = jnp.float32
BF16 = jnp.bfloat16

HEAD_DIM = 64
NSA_HEADS = 8
NSA_KV_HEADS = 2
NSA_GROUP = NSA_HEADS // NSA_KV_HEADS
RET_HEADS = 8
RET_DK = 32
RET_DV = 64
NSA_WIDTH = NSA_HEADS * HEAD_DIM
RET_WIDTH = RET_HEADS * RET_DV
KV_WIDTH = NSA_KV_HEADS * HEAD_DIM
CMP_LEN = 32
CMP_STRIDE = 16
SLC_LEN = 64
SLC_TOPK = 16
WIN = 512
Q_BLOCK = 128
RET_CHUNK = 128
N_EXPERTS = 8
EPS = 1e-6
NEG = -1e30
BIG = 1e9
LANES = 128
GQ = NSA_GROUP * Q_BLOCK
KEY_STEP = 128
STEP_GROUP = 4
N_FORCED = 3
CMP_CHUNK = 128
CMP_TILES = 8
MAIN_TILES = 8
LOOP_TILES = 2
CMP_TAIL = CMP_CHUNK + 8
WIN_KEYS = WIN + Q_BLOCK
V_ROWS = HEAD_DIM + 16
LOG2E = 1.4426950408889634
FF_CHUNK = 1408
VMEM_LIMIT = 60 * 1024 * 1024

_C_Q = 0
_C_KV = _C_Q + NSA_WIDTH
_C_KC = _C_KV + 4 * KV_WIDTH
_C_VC = _C_KC + KV_WIDTH
_C_GT = _C_VC + KV_WIDTH
_C_RET = _C_GT + LANES
_RET_COLS = 2 * RET_HEADS * RET_DK + 2 * RET_WIDTH
_C_END = _C_RET + _RET_COLS


def _params(n_axes, vmem=VMEM_LIMIT):
    return pltpu.CompilerParams(dimension_semantics=("arbitrary",) * n_axes, vmem_limit_bytes=vmem)


def _dot(a, b):
    return jnp.dot(a, b, preferred_element_type=F32)


def _dot_nt(a, b):
    return lax.dot_general(a, b, (((1,), (1,)), ((), ())), preferred_element_type=F32)


def _rms(x, g):
    return x * lax.rsqrt(jnp.mean(x * x, axis=-1, keepdims=True) + EPS) * g


def _group_rms(x, g, ones_ref):
    w = x.shape[1]
    ones = ones_ref[0:w, 0:w]
    sq = x * x
    hi = sq.astype(BF16)
    lo = (sq - hi.astype(F32)).astype(BF16)
    ms = (_dot(hi, ones) + _dot(lo, ones)) * (1.0 / HEAD_DIM)
    return x * lax.rsqrt(ms + EPS) * g


def _group_ones():
    lane = np.arange(NSA_WIDTH) // HEAD_DIM
    return jnp.asarray(lane[:, None] == lane[None, :], BF16)


def _inproj_kernel(x_ref, g_ref, w_ref, qg_ref, kg_ref, ones_ref, kc_ref, vc_ref, ret_ref,
                   qT_ref, ks_ref, kw_ref, vsT_ref, vwT_ref, gT_ref, *, steps_per_row):
    n_tok = x_ref.shape[0]
    tiles = range(n_tok // Q_BLOCK)
    rows = [slice(u * Q_BLOCK, (u + 1) * Q_BLOCK) for u in tiles]
    xn = _rms(x_ref[...], g_ref[...]).astype(BF16)
    q = _dot(xn, w_ref[:, _C_Q:_C_KV])
    mid = _dot(xn, w_ref[:, _C_KV:_C_RET])
    kv = mid[:, 0:_C_KC - _C_KV]
    kc_ref[...] = mid[:, _C_KC - _C_KV:_C_VC - _C_KV]
    vc_ref[...] = mid[:, _C_VC - _C_KV:_C_GT - _C_KV]
    gt = mid[:, _C_GT - _C_KV:_C_RET - _C_KV]

    scale = HEAD_DIM ** -0.5 * LOG2E
    qn = _group_rms(q, qg_ref[...], ones_ref) * scale
    qt = [qn[rows[u]].T for u in tiles]
    for g in range(NSA_KV_HEADS):
        for r in range(NSA_GROUP):
            h = g * NSA_GROUP + r
            for u in tiles:
                qT_ref[g, u, :, r * Q_BLOCK:(r + 1) * Q_BLOCK] = qt[u][h * HEAD_DIM:(h + 1) * HEAD_DIM, :].astype(BF16)
    vst = [kv[rows[u], 2 * KV_WIDTH:3 * KV_WIDTH].T for u in tiles]
    vwt = [kv[rows[u], 3 * KV_WIDTH:4 * KV_WIDTH].T for u in tiles]
    gts = [jax.nn.sigmoid(gt[rows[u], :].T[0:32, :]) for u in tiles]
    pos0 = (pl.program_id(0) % steps_per_row) * n_tok
    pos = pos0 + lax.broadcasted_iota(jnp.int32, (n_tok, HEAD_DIM), 0)
    col = lax.broadcasted_iota(jnp.int32, (n_tok, HEAD_DIM), 1)
    kpos = jnp.where(col < 3, pos // SLC_LEN, jnp.where(col < 6, pos % SLC_LEN, 0)).astype(F32)
    ones_row = (lax.broadcasted_iota(jnp.int32, (V_ROWS - HEAD_DIM, Q_BLOCK), 0) == 0).astype(F32)
    ks = _group_rms(kv[:, 0:KV_WIDTH], kg_ref[0:1, :], ones_ref)
    kw = _group_rms(kv[:, KV_WIDTH:2 * KV_WIDTH], kg_ref[1:2, :], ones_ref)
    for g in range(NSA_KV_HEADS):
        sl = slice(g * HEAD_DIM, (g + 1) * HEAD_DIM)
        ks_ref[g] = jnp.concatenate([ks[:, sl], kpos], axis=1).astype(BF16)
        kw_ref[g] = jnp.concatenate([kw[:, sl], kpos], axis=1).astype(BF16)
        for u in tiles:
            vsT_ref[g, :, rows[u]] = jnp.concatenate([vst[u][sl, :], ones_row], axis=0).astype(BF16)
            vwT_ref[g, :, rows[u]] = jnp.concatenate([vwt[u][sl, :], ones_row], axis=0).astype(BF16)
    for u in tiles:
        gT_ref[u] = gts[u]
    ret_ref[...] = _dot(xn, w_ref[:, _C_RET:_C_END])


def _inproj(x2, g, w, qg, kg, B, T, tm=1024):
    n, d = x2.shape
    nq = T // Q_BLOCK
    G = NSA_KV_HEADS
    tiles = tm // Q_BLOCK
    spr = T // tm
    assert T % tm == 0
    const = lambda i: (0, 0)
    row = lambda i: (i, 0)
    return pl.pallas_call(
        functools.partial(_inproj_kernel, steps_per_row=spr),
        grid=(n // tm,),
        in_specs=[pl.BlockSpec((tm, d), row),
                  pl.BlockSpec((1, d), const),
                  pl.BlockSpec((d, _C_END), const),
                  pl.BlockSpec((1, NSA_WIDTH), const),
                  pl.BlockSpec((2, KV_WIDTH), const),
                  pl.BlockSpec((NSA_WIDTH, NSA_WIDTH), const)],
        out_specs=[pl.BlockSpec((tm, KV_WIDTH), row),
                   pl.BlockSpec((tm, KV_WIDTH), row),
                   pl.BlockSpec((tm, _RET_COLS), row),
                   pl.BlockSpec((None, G, tiles, HEAD_DIM, GQ), lambda i: (i // spr, 0, i % spr, 0, 0)),
                   pl.BlockSpec((None, G, tm, 2 * HEAD_DIM), lambda i: (i // spr, 0, i % spr, 0)),
                   pl.BlockSpec((None, G, tm, 2 * HEAD_DIM), lambda i: (i // spr, 0, i % spr, 0)),
                   pl.BlockSpec((None, G, V_ROWS, tm), lambda i: (i // spr, 0, 0, i % spr)),
                   pl.BlockSpec((None, G, V_ROWS, tm), lambda i: (i // spr, 0, 0, i % spr)),
                   pl.BlockSpec((None, tiles, 32, Q_BLOCK), lambda i: (i // spr, i % spr, 0, 0))],
        out_shape=[jax.ShapeDtypeStruct((n, KV_WIDTH), F32),
                   jax.ShapeDtypeStruct((n, KV_WIDTH), F32),
                   jax.ShapeDtypeStruct((n, _RET_COLS), F32),
                   jax.ShapeDtypeStruct((B, G, nq, HEAD_DIM, GQ), BF16),
                   jax.ShapeDtypeStruct((B, G, T, 2 * HEAD_DIM), BF16),
                   jax.ShapeDtypeStruct((B, G, T, 2 * HEAD_DIM), BF16),
                   jax.ShapeDtypeStruct((B, G, V_ROWS, T), BF16),
                   jax.ShapeDtypeStruct((B, G, V_ROWS, T), BF16),
                   jax.ShapeDtypeStruct((B, nq, 32, Q_BLOCK), F32)],
        compiler_params=_params(1),
        name="inproj",
    )(x2, g, w, jnp.tile(qg, (1, NSA_HEADS)), jnp.tile(kg, (1, NSA_KV_HEADS)), _group_ones())


def _compress_kernel(kc_ref, vc_ref, wk_ref, wv_ref, pk_ref, pv_ref, kg_ref, kcmp_ref, vcT_ref):
    ncp = kc_ref.shape[0] // CMP_STRIDE

    def comp(a_ref, w_ref, p_ref):
        lo = jnp.zeros((ncp, KV_WIDTH), F32)
        hi = jnp.zeros((ncp, KV_WIDTH), F32)
        for l in range(CMP_STRIDE):
            a = a_ref[pl.ds(l, ncp, stride=CMP_STRIDE), :]
            lo += _dot((a + p_ref[0, l:l + 1, :]).astype(BF16), w_ref[0, l])
            hi += _dot((a + p_ref[1, l:l + 1, :]).astype(BF16), w_ref[1, l])
        return lo + pltpu.roll(hi, ncp - 1, 0)

    k = comp(kc_ref, wk_ref, pk_ref)
    v = comp(vc_ref, wv_ref, pv_ref).T
    cend = lax.broadcasted_iota(jnp.int32, (ncp, HEAD_DIM), 0) * CMP_STRIDE + (CMP_LEN - 1)
    col = lax.broadcasted_iota(jnp.int32, (ncp, HEAD_DIM), 1)
    kpos = jnp.where(col < 3, cend // SLC_LEN, jnp.where(col < 6, cend % SLC_LEN, 0)).astype(F32)
    for g in range(NSA_KV_HEADS):
        sl = slice(g * HEAD_DIM, (g + 1) * HEAD_DIM)
        kcmp_ref[g] = jnp.concatenate([_rms(k[:, sl], kg_ref[...]), kpos], axis=1).astype(BF16)
        vcT_ref[g] = v[sl, :].astype(BF16)


def _compress(kc, vc, wk, wv, pk, pv, kg, B, T):
    ncp = T // CMP_STRIDE
    G = NSA_KV_HEADS
    const4 = lambda b: (0, 0, 0, 0)
    const3 = lambda b: (0, 0, 0)
    const2 = lambda b: (0, 0)
    return pl.pallas_call(
        _compress_kernel,
        grid=(B,),
        in_specs=[pl.BlockSpec((T, KV_WIDTH), lambda b: (b, 0)),
                  pl.BlockSpec((T, KV_WIDTH), lambda b: (b, 0)),
                  pl.BlockSpec((2, CMP_STRIDE, KV_WIDTH, KV_WIDTH), const4),
                  pl.BlockSpec((2, CMP_STRIDE, KV_WIDTH, KV_WIDTH), const4),
                  pl.BlockSpec((2, CMP_STRIDE, KV_WIDTH), const3),
                  pl.BlockSpec((2, CMP_STRIDE, KV_WIDTH), const3),
                  pl.BlockSpec((1, HEAD_DIM), const2)],
        out_specs=[pl.BlockSpec((None, G, ncp, 2 * HEAD_DIM), lambda b: (b, 0, 0, 0)),
                   pl.BlockSpec((None, G, HEAD_DIM, ncp), lambda b: (b, 0, 0, 0))],
        out_shape=[jax.ShapeDtypeStruct((B, G, ncp, 2 * HEAD_DIM), BF16),
                   jax.ShapeDtypeStruct((B, G, HEAD_DIM, ncp), BF16)],
        compiler_params=_params(1),
        name="nsa_compress",
    )(kc, vc, wk, wv, pk, pv, kg)


def _compress_weights(w, pos):
    G = NSA_KV_HEADS
    w4 = w.reshape(2, CMP_STRIDE, HEAD_DIM, HEAD_DIM)
    eye = jnp.eye(G, dtype=w.dtype)
    wbd = jnp.einsum('hlde,gk->hlgdke', w4, eye).reshape(2, CMP_STRIDE, KV_WIDTH, KV_WIDTH)
    p = pos.reshape(2, CMP_STRIDE, 1, HEAD_DIM)
    p = jnp.broadcast_to(p, (2, CMP_STRIDE, G, HEAD_DIM)).reshape(2, CMP_STRIDE, KV_WIDTH)
    return wbd.astype(BF16), p


def _split3(x):
    hi = x.astype(BF16)
    r = x - hi.astype(F32)
    mid = r.astype(BF16)
    lo = (r - mid.astype(F32)).astype(BF16)
    return hi, mid, lo


def _nsa_cmp_kernel(qT_ref, qaug_ref, kc_ref, vcT_ref, ovT_ref, ocmp_ref, sel_ref, flag_ref, *, n_sel):
    ncp = kc_ref.shape[0]
    ns = ovT_ref.shape[0]
    tiles = range(CMP_TILES)
    i0 = pl.program_id(2) * CMP_TILES
    lane = lax.broadcasted_iota(jnp.int32, (1, GQ), 1)
    q = [jnp.concatenate([qT_ref[u], qaug_ref[...]], axis=0) for u in tiles]
    t_row = [(i0 + u) * Q_BLOCK + (lane & (Q_BLOCK - 1)) for u in tiles]
    has_cmp = [(t_row[u] >= CMP_LEN - 1).astype(F32) for u in tiles]
    tq = [(i0 + u) * Q_BLOCK + lax.broadcasted_iota(jnp.int32, (1, Q_BLOCK), 1) for u in tiles]
    cur = [tq[u] // SLC_LEN for u in tiles]

    def prefix(rows):
        nsk = rows * CMP_STRIDE // SLC_LEN
        tail0 = max(rows - CMP_TAIL, 0)
        kc = kc_ref[0:rows, :]
        s = [_dot(kc, q[u]) for u in tiles]
        cend = (lax.broadcasted_iota(jnp.int32, (rows - tail0, 1), 0) + tail0) * CMP_STRIDE + (CMP_LEN - 1)
        tail = [jnp.where(t_row[u] >= cend, s[u][tail0:], NEG) for u in tiles]
        s = [jnp.concatenate([s[u][0:tail0], tail[u]], axis=0) if tail0 else tail[u] for u in tiles]
        m = [jnp.max(s[u], axis=0, keepdims=True) for u in tiles]
        e = [jnp.exp2(s[u] - m[u]) for u in tiles]
        p = [e[u] * (has_cmp[u] / jnp.sum(e[u], axis=0, keepdims=True)) for u in tiles]
        vc = vcT_ref[:, 0:rows]
        for u in tiles:
            ocmp_ref[u] = _dot(vc, p[u].astype(BF16))

        ps = [p[u][:, 0:Q_BLOCK] for u in tiles]
        for r in range(1, NSA_GROUP):
            ps = [ps[u] + p[u][:, r * Q_BLOCK:(r + 1) * Q_BLOCK] for u in tiles]
        ov = ovT_ref[0:nsk, 0:rows]
        split = [_split3(ps[u]) for u in tiles]
        imp = [_dot(ov, split[u][0]) + _dot(ov, split[u][1]) + _dot(ov, split[u][2]) for u in tiles]

        blk = lax.broadcasted_iota(jnp.int32, (nsk, 1), 0)
        forced = [(blk == 0) | (blk == cur[u]) | (blk == cur[u] - 1) for u in tiles]
        valid = [blk * SLC_LEN <= tq[u] for u in tiles]
        imp = [jnp.where(forced[u], -3e38, jnp.where(valid[u], imp[u], -BIG)) for u in tiles]
        blk_f = blk.astype(F32)
        sel = [forced[u].astype(F32) for u in tiles]
        for _ in range(n_sel - N_FORCED):
            mx = [jnp.max(imp[u], axis=0, keepdims=True) for u in tiles]
            idx = [jnp.min(jnp.where(imp[u] == mx[u], blk_f, float(ns)), axis=0, keepdims=True) for u in tiles]
            pick = [blk_f == idx[u] for u in tiles]
            sel = [jnp.where(pick[u], 1.0, sel[u]) for u in tiles]
            imp = [jnp.where(pick[u], -3e38, imp[u]) for u in tiles]
        ones = jnp.ones((8, Q_BLOCK), BF16)
        for u in tiles:
            sel_ref[u, 0:nsk, :] = sel[u]
            cnt = _dot_nt(ones, sel[u].astype(BF16))
            flag_ref[u, :, 0:nsk] = (cnt > 0).astype(jnp.int32)
            if nsk < ns:
                sel_ref[u, nsk:, :] = jnp.zeros((ns - nsk, Q_BLOCK), F32)
                flag_ref[u, :, nsk:] = jnp.zeros((8, ns - nsk), jnp.int32)

    n_variants = ncp // CMP_CHUNK
    last = i0 + CMP_TILES - 1
    variant = (last * (Q_BLOCK // CMP_STRIDE) + (Q_BLOCK // CMP_STRIDE - 2)) // CMP_CHUNK
    for k in range(n_variants):
        pl.when(variant == k)(functools.partial(prefix, (k + 1) * CMP_CHUNK))


def _nsa_cmp(qT, qaug, kcmp, vcT, ovT, B, T):
    G = NSA_KV_HEADS
    nq = T // Q_BLOCK
    ncp = T // CMP_STRIDE
    ns = T // SLC_LEN
    n_sel = min(SLC_TOPK, ns)
    assert ncp % CMP_CHUNK == 0 and n_sel > N_FORCED and nq % CMP_TILES == 0
    tile = lambda b, g, i: (b, g, i, 0, 0)
    return pl.pallas_call(
        functools.partial(_nsa_cmp_kernel, n_sel=n_sel),
        grid=(B, G, nq // CMP_TILES),
        in_specs=[pl.BlockSpec((None, None, CMP_TILES, HEAD_DIM, GQ), tile),
                  pl.BlockSpec((None, HEAD_DIM, GQ), lambda b, g, i: (g, 0, 0)),
                  pl.BlockSpec((None, None, ncp, 2 * HEAD_DIM), lambda b, g, i: (b, g, 0, 0)),
                  pl.BlockSpec((None, None, HEAD_DIM, ncp), lambda b, g, i: (b, g, 0, 0)),
                  pl.BlockSpec((ns, ncp), lambda b, g, i: (0, 0))],
        out_specs=[pl.BlockSpec((None, None, CMP_TILES, HEAD_DIM, GQ), tile),
                   pl.BlockSpec((None, None, CMP_TILES, ns, Q_BLOCK), tile),
                   pl.BlockSpec((None, None, CMP_TILES, 8, ns), tile)],
        out_shape=[jax.ShapeDtypeStruct((B, G, nq, HEAD_DIM, GQ), F32),
                   jax.ShapeDtypeStruct((B, G, nq, ns, Q_BLOCK), F32),
                   jax.ShapeDtypeStruct((B, G, nq, 8, ns), jnp.int32)],
        compiler_params=_params(3),
        name="nsa_cmp",
    )(qT, qaug, kcmp, vcT, ovT)


def _nsa_main_kernel(list_ref, cnt_ref, qT_ref, qaug_ref, ks_ref, vsT_ref, kw_ref, vwT_ref, sel_ref, gT_ref, ocmp_ref,
                     lowb_ref, causb_ref, out_ref, m_sc, acc_sc, win_sc):
    b, g = pl.program_id(0), pl.program_id(1)
    tiles = range(MAIN_TILES)
    i = [pl.program_id(2) * MAIN_TILES + u for u in tiles]
    tile_id = [(b * pl.num_programs(1) + g) * (pl.num_programs(2) * MAIN_TILES) + i[u] for u in tiles]
    n_steps = sel_ref.shape[1] // 2
    q = [jnp.concatenate([qT_ref[u], qaug_ref[...]], axis=0) for u in tiles]
    k0 = [pl.multiple_of(i[u] * Q_BLOCK, Q_BLOCK) for u in tiles]

    def sel_bias(u, j, valid):
        def row(r):
            picked = (sel_ref[u, pl.ds(r, 1), :] > 0.5) & valid
            return jnp.concatenate([jnp.where(picked, 0.0, NEG)] * NSA_GROUP, axis=1)
        return row(2 * j), row(2 * j + 1)

    def add_sel_bias(s, ba, bb):
        return jnp.concatenate([s[0:SLC_LEN] + ba, s[SLC_LEN:] + bb], axis=0)

    lowb = jnp.concatenate([lowb_ref[...]] * NSA_GROUP, axis=1)
    causb = jnp.concatenate([causb_ref[...]] * NSA_GROUP, axis=1)

    bias_d = [sel_bias(u, i[u], True) for u in tiles]
    sd = [_dot(ks_ref[pl.ds(k0[u], KEY_STEP), :], q[u]) for u in tiles]
    sw = [_dot(kw_ref[pl.ds(k0[u], WIN_KEYS), :], q[u]) for u in tiles]
    sd = [add_sel_bias(sd[u], *bias_d[u]) + causb for u in tiles]
    sw = [jnp.concatenate([sw[u][0:Q_BLOCK] + lowb, sw[u][Q_BLOCK:WIN], sw[u][WIN:] + causb], axis=0) for u in tiles]
    md = [jnp.max(sd[u], axis=0, keepdims=True) for u in tiles]
    mw = [jnp.max(sw[u], axis=0, keepdims=True) for u in tiles]
    accd = [_dot(vsT_ref[:, pl.ds(k0[u], KEY_STEP)], jnp.exp2((sd[u] - md[u]).astype(BF16))) for u in tiles]
    ow = [_dot(vwT_ref[:, pl.ds(k0[u], WIN_KEYS)], jnp.exp2((sw[u] - mw[u]).astype(BF16))) for u in tiles]
    for u in tiles:
        m_sc[u] = md[u]
        acc_sc[u] = accd[u]
        win_sc[u] = ow[u][0:HEAD_DIM] / ow[u][HEAD_DIM:HEAD_DIM + 1]

    def scores(u, t):
        ks, vs, biases = [], [], []
        for x in range(STEP_GROUP):
            j = list_ref[tile_id[u] * n_steps + t * STEP_GROUP + x]
            valid = j >= 0
            j = jnp.maximum(j, 0)
            kj = pl.multiple_of(j * KEY_STEP, KEY_STEP)
            ks.append(ks_ref[pl.ds(kj, KEY_STEP), :])
            vs.append(vsT_ref[:, pl.ds(kj, KEY_STEP)])
            biases.append(sel_bias(u, j, valid))
        s = _dot(jnp.concatenate(ks, axis=0), q[u])
        s = jnp.concatenate([add_sel_bias(s[x * KEY_STEP:(x + 1) * KEY_STEP], *biases[x])
                             for x in range(STEP_GROUP)], axis=0)
        return s, jnp.max(s, axis=0, keepdims=True), jnp.concatenate(vs, axis=1)

    def accumulate(u, s, smax, vcat):
        m_old = m_sc[u]
        m_new = jnp.maximum(m_old, smax)
        alpha = jnp.exp2(m_old - m_new)
        acc_sc[u] = alpha * acc_sc[u] + _dot(vcat, jnp.exp2((s - m_new).astype(BF16)))
        m_sc[u] = m_new

    def run(work, t, carry):
        staged = [(u, scores(u, t * mult + off)) for (u, mult, off) in work]
        for u, args in staged:
            accumulate(u, *args)
        return carry

    for u0 in range(0, MAIN_TILES, LOOP_TILES):
        us = range(u0, u0 + LOOP_TILES)
        n_groups = functools.reduce(
            jnp.maximum, [(cnt_ref[tile_id[u]] + (STEP_GROUP - 1)) // STEP_GROUP for u in us])
        lax.fori_loop(0, n_groups // 2, functools.partial(run, [(u, 2, off) for off in (0, 1) for u in us]), 0)
        lax.fori_loop(n_groups // 2 * 2, n_groups, functools.partial(run, [(u, 1, 0) for u in us]), 0)

    def gate(u, k):
        rows = [gT_ref[u, pl.ds(g * (NSA_GROUP * 3) + r * 3 + k, 1), :] for r in range(NSA_GROUP)]
        return jnp.concatenate(rows, axis=1)

    o_slc = [acc_sc[u, 0:HEAD_DIM, :] / acc_sc[u, HEAD_DIM:HEAD_DIM + 1, :] for u in tiles]
    o = [gate(u, 0) * ocmp_ref[u] + gate(u, 1) * o_slc[u] + gate(u, 2) * win_sc[u] for u in tiles]
    o = [jnp.concatenate([o[u], jnp.zeros_like(o[u])], axis=0) for u in tiles]
    for r in range(NSA_GROUP):
        ot = [o[u][:, r * Q_BLOCK:(r + 1) * Q_BLOCK].T[:, 0:HEAD_DIM] for u in tiles]
        for u in tiles:
            out_ref[u * Q_BLOCK:(u + 1) * Q_BLOCK, r * HEAD_DIM:(r + 1) * HEAD_DIM] = ot[u]


def _nsa_steps_kernel(flagT_ref, pairT_ref, list_ref, cnt_ref, *, nq):
    n_steps, nt = list_ref.shape
    need = _dot(pairT_ref[...], flagT_ref[...].astype(BF16)) > 0
    step = lax.broadcasted_iota(jnp.int32, (n_steps, 1), 0)
    own = lax.broadcasted_iota(jnp.int32, (1, nt), 1) % nq
    need = need & (step < own)
    need_f = need.astype(F32)
    earlier = (lax.broadcasted_iota(jnp.int32, (n_steps, n_steps), 1) < step).astype(BF16)
    slot = _dot(earlier, need_f.astype(BF16))
    total = jnp.sum(need_f, axis=0, keepdims=True)
    cnt_ref[...] = jnp.broadcast_to(total, cnt_ref.shape).astype(jnp.int32)
    step_f = step.astype(F32)
    for p in range(n_steps):
        val = jnp.sum(jnp.where(need & (slot == p), step_f, 0.0), axis=0, keepdims=True)
        list_ref[p:p + 1, :] = jnp.where(total > p, val, -1.0).astype(jnp.int32)


def _nsa_steps(flags, nq):
    nt, ns = flags.shape
    n_steps = ns // 2
    pairT = jnp.asarray(np.arange(n_steps)[:, None] == np.arange(ns)[None, :] // 2, BF16)
    lists, counts = pl.pallas_call(
        functools.partial(_nsa_steps_kernel, nq=nq),
        out_shape=[jax.ShapeDtypeStruct((n_steps, nt), jnp.int32), jax.ShapeDtypeStruct((8, nt), jnp.int32)],
        name="nsa_steps",
    )(flags.T.astype(F32), pairT)
    return lists.T.reshape(-1), counts[0]


def _nsa_main(lists, counts, qT, qaug, ks, vsT, kw, vwT, sel, gT, ocmp, lowb, causb, B, T):
    G = NSA_KV_HEADS
    nq = T // Q_BLOCK
    ns = T // SLC_LEN
    whole = lambda b, g, i, *_: (b, g, 0, 0)
    tile = lambda b, g, i, *_: (b, g, i, 0, 0)
    const = lambda b, g, i, *_: (0, 0)
    grid_spec = pltpu.PrefetchScalarGridSpec(
        num_scalar_prefetch=2,
        grid=(B, G, nq // MAIN_TILES),
        in_specs=[pl.BlockSpec((None, None, MAIN_TILES, HEAD_DIM, GQ), tile),
                  pl.BlockSpec((None, HEAD_DIM, GQ), lambda b, g, i, *_: (g, 0, 0)),
                  pl.BlockSpec((None, None, T, 2 * HEAD_DIM), whole),
                  pl.BlockSpec((None, None, V_ROWS, T), whole),
                  pl.BlockSpec((None, None, T + WIN, 2 * HEAD_DIM), whole),
                  pl.BlockSpec((None, None, V_ROWS, T + WIN), whole),
                  pl.BlockSpec((None, None, MAIN_TILES, ns, Q_BLOCK), tile),
                  pl.BlockSpec((None, MAIN_TILES, 32, Q_BLOCK), lambda b, g, i, *_: (b, i, 0, 0)),
                  pl.BlockSpec((None, None, MAIN_TILES, HEAD_DIM, GQ), tile),
                  pl.BlockSpec((Q_BLOCK, Q_BLOCK), const),
                  pl.BlockSpec((Q_BLOCK, Q_BLOCK), const)],
        out_specs=pl.BlockSpec((None, MAIN_TILES * Q_BLOCK, NSA_GROUP * HEAD_DIM), lambda b, g, i, *_: (b, i, g)),
        scratch_shapes=[pltpu.VMEM((MAIN_TILES, 1, GQ), F32), pltpu.VMEM((MAIN_TILES, V_ROWS, GQ), F32),
                        pltpu.VMEM((MAIN_TILES, HEAD_DIM, GQ), F32)],
    )
    return pl.pallas_call(
        _nsa_main_kernel,
        grid_spec=grid_spec,
        out_shape=jax.ShapeDtypeStruct((B, T, NSA_WIDTH), F32),
        compiler_params=_params(3),
        name="nsa_main",
    )(lists, counts, qT, qaug, ks, vsT, kw, vwT, sel, gT, ocmp, lowb, causb)


def _ret_kernel(p_ref, decay_ref, xi_ref, zeta_ref, gch_ref, ng_ref, ones_ref, out_ref, state_ref):
    @pl.when(pl.program_id(0) == 0)
    def _():
        state_ref[...] = jnp.zeros(state_ref.shape, F32)

    rows = range(p_ref.shape[0])
    kw = RET_HEADS * RET_DK
    p = [p_ref[b] for b in rows]
    rq = [p[b][:, 0:kw] * (RET_DK ** -0.5) for b in rows]
    rk = [p[b][:, kw:2 * kw] for b in rows]
    rkT = [rk[b].T for b in rows]
    rv = [p[b][:, 2 * kw:2 * kw + RET_WIDTH] for b in rows]
    xi = xi_ref[...]
    outs = [[] for _ in rows]
    for h in range(RET_HEADS):
        dk = slice(h * RET_DK, (h + 1) * RET_DK)
        dv = slice(h * RET_DV, (h + 1) * RET_DV)
        st = [state_ref[b, h] for b in rows]
        inner = [_dot_nt(rq[b][:, dk], rk[b][:, dk]) * decay_ref[h] for b in rows]
        o = [_dot(inner[b], rv[b][:, dv]) + _dot(rq[b][:, dk], st[b]) * xi[:, h:h + 1] for b in rows]
        for b in rows:
            state_ref[b, h] = (st[b] * gch_ref[h:h + 1, 0:1]
                               + _dot(rkT[b][dk, :] * zeta_ref[h:h + 1, :], rv[b][:, dv]))
            outs[b].append(o[b])
    normed = [_group_rms(jnp.concatenate(outs[b], axis=1), ng_ref[...], ones_ref) for b in rows]
    for b in rows:
        rg = p[b][:, 2 * kw + RET_WIDTH:2 * kw + 2 * RET_WIDTH]
        out_ref[b] = normed[b] * (rg * jax.nn.sigmoid(rg))


def _ret_consts():
    H, C = RET_HEADS, RET_CHUNK
    log_g = np.log1p(-np.exp2(-5.0 - np.arange(H, dtype=np.float64)))
    idx = np.arange(C, dtype=np.float64)
    diff = idx[:, None] - idx[None, :]
    decay = np.where(diff >= 0, np.exp(np.maximum(diff, 0.0) * log_g[:, None, None]), 0.0)
    zeta = np.exp((C - 1 - idx) * log_g[:, None])
    xi = np.exp((idx + 1) * log_g[:, None]).T
    g_chunk = np.broadcast_to(np.exp(C * log_g)[:, None], (H, LANES))
    return tuple(jnp.asarray(a, F32) for a in (decay, xi, zeta, g_chunk))


def _retention(pret, ng, B, T):
    nch = T // RET_CHUNK
    decay, xi, zeta, gch = _ret_consts()
    c2 = lambda c: (0, 0)
    out = pl.pallas_call(
        _ret_kernel,
        grid=(nch,),
        in_specs=[pl.BlockSpec((B, RET_CHUNK, _RET_COLS), lambda c: (0, c, 0)),
                  pl.BlockSpec((RET_HEADS, RET_CHUNK, RET_CHUNK), lambda c: (0, 0, 0)),
                  pl.BlockSpec((RET_CHUNK, RET_HEADS), c2),
                  pl.BlockSpec((RET_HEADS, RET_CHUNK), c2),
                  pl.BlockSpec((RET_HEADS, LANES), c2),
                  pl.BlockSpec((1, RET_WIDTH), c2),
                  pl.BlockSpec((RET_WIDTH, RET_WIDTH), c2)],
        out_specs=pl.BlockSpec((B, RET_CHUNK, RET_WIDTH), lambda c: (0, c, 0)),
        out_shape=jax.ShapeDtypeStruct((B, T, RET_WIDTH), F32),
        scratch_shapes=[pltpu.VMEM((B, RET_HEADS, RET_DK, RET_DV), F32)],
        compiler_params=_params(1),
        name="retention",
    )(pret.reshape(B, T, _RET_COLS), decay, xi, zeta, gch, ng, _group_ones())
    return out.reshape(B * T, RET_WIDTH)


def _mixer_residual(x_ref, nsa_ref, ret_ref, wo_ref):
    return (x_ref[...] + _dot(nsa_ref[...].astype(BF16), wo_ref[0:NSA_WIDTH, :])
            + _dot(ret_ref[...].astype(BF16), wo_ref[NSA_WIDTH:, :]))


def _mixer_specs(tm, d, row, const):
    return [pl.BlockSpec((tm, d), row), pl.BlockSpec((tm, NSA_WIDTH), row), pl.BlockSpec((tm, RET_WIDTH), row),
            pl.BlockSpec((NSA_WIDTH + RET_WIDTH, d), const)]


def _gate_up(w_gate, w_up, fc):
    lead, dff = w_gate.shape[:-1], w_gate.shape[-1]
    chunks = lambda w: w.reshape(lead + (dff // fc, fc))
    return jnp.concatenate([chunks(w_gate), chunks(w_up)], axis=-1).reshape(lead + (2 * dff,)).astype(BF16)


def _swiglu_hidden(rows, wgu_ref):
    au = _dot(rows, wgu_ref[...])
    fc = au.shape[1] // 2
    a = au[:, 0:fc]
    return (a * jax.nn.sigmoid(a) * au[:, fc:]).astype(BF16)


def _ffn_kernel(x_ref, nsa_ref, ret_ref, wo_ref, g_ref, wgu_ref, wd_ref, o_ref, h_sc):
    f = pl.program_id(1)

    @pl.when(f == 0)
    def _():
        x = _mixer_residual(x_ref, nsa_ref, ret_ref, wo_ref)
        h_sc[...] = _rms(x, g_ref[...]).astype(BF16)
        o_ref[...] = x

    o_ref[...] += _dot(_swiglu_hidden(h_sc[...], wgu_ref), wd_ref[...])


def _ffn(x2, nsa, ret, wo, g, wgu, wd, fc, tm=1024):
    n, d = x2.shape
    dff = wd.shape[0]
    return pl.pallas_call(
        _ffn_kernel,
        grid=(n // tm, dff // fc),
        in_specs=_mixer_specs(tm, d, lambda i, f: (i, 0), lambda i, f: (0, 0)) + [
            pl.BlockSpec((1, d), lambda i, f: (0, 0)),
            pl.BlockSpec((d, 2 * fc), lambda i, f: (0, f)),
            pl.BlockSpec((fc, d), lambda i, f: (f, 0))],
        out_specs=pl.BlockSpec((tm, d), lambda i, f: (i, 0)),
        out_shape=jax.ShapeDtypeStruct((n, d), F32),
        scratch_shapes=[pltpu.VMEM((tm, d), BF16)],
        compiler_params=_params(2),
        name="ffn_dense",
    )(x2, nsa, ret, wo, g, wgu, wd)


def _router_kernel(x_ref, nsa_ref, ret_ref, wo_ref, g_ref, r_ref, rb_ref, tri_ref,
                   x1_ref, h_ref, rank_ref, comb_ref, rankT_ref, cnt_ref):
    x = _mixer_residual(x_ref, nsa_ref, ret_ref, wo_ref)
    x1_ref[...] = x
    h = _rms(x, g_ref[...])
    h_ref[...] = h.astype(BF16)
    hh, hm, _ = _split3(h)
    rh, rm, _ = _split3(r_ref[...])
    logits = _dot(hh, rh) + (_dot(hh, rm) + _dot(hm, rh)) + rb_ref[...]
    lane = lax.broadcasted_iota(jnp.int32, logits.shape, 1).astype(F32)
    logits = jnp.where(lane < N_EXPERTS, logits, NEG)
    m1 = jnp.max(logits, axis=1, keepdims=True)
    i1 = jnp.min(jnp.where(logits == m1, lane, float(LANES)), axis=1, keepdims=True)
    l2 = jnp.where(lane == i1, NEG, logits)
    m2 = jnp.max(l2, axis=1, keepdims=True)
    i2 = jnp.min(jnp.where(l2 == m2, lane, float(LANES)), axis=1, keepdims=True)
    e2 = jnp.exp(m2 - m1)
    w1 = 1.0 / (1.0 + e2)
    w2 = e2 / (1.0 + e2)
    use1, use2 = lane == i1, lane == i2
    comb_ref[...] = jnp.where(use1, w1, 0.0) + jnp.where(use2, w2, 0.0)
    use = (use1 | use2).astype(F32)
    rank = jnp.where(use > 0, _dot(tri_ref[...], use.astype(BF16)), -1.0)
    rank_ref[...] = rank
    rankT_ref[...] = rank.T[0:N_EXPERTS, :]
    cnt_ref[...] = jnp.broadcast_to(jnp.sum(use, axis=0, keepdims=True), cnt_ref.shape).astype(jnp.int32)


def _router(x2, nsa, ret, wo, g, router, rb, tm):
    n, d = x2.shape
    nt = n // tm
    tri = jnp.asarray(np.arange(tm)[:, None] > np.arange(tm)[None, :], BF16)
    rpad = jnp.zeros((d, LANES), F32).at[:, :N_EXPERTS].set(router)
    rbpad = jnp.zeros((1, LANES), F32).at[0, :N_EXPERTS].set(rb)
    c2 = lambda i: (0, 0)
    row = lambda i: (i, 0)
    return pl.pallas_call(
        _router_kernel,
        grid=(nt,),
        in_specs=_mixer_specs(tm, d, row, c2) + [
            pl.BlockSpec((1, d), c2),
            pl.BlockSpec((d, LANES), c2),
            pl.BlockSpec((1, LANES), c2),
            pl.BlockSpec((tm, tm), c2)],
        out_specs=[pl.BlockSpec((tm, d), row),
                   pl.BlockSpec((tm, d), row),
                   pl.BlockSpec((tm, LANES), row),
                   pl.BlockSpec((tm, LANES), row),
                   pl.BlockSpec((N_EXPERTS, tm), lambda i: (0, i)),
                   pl.BlockSpec((None, 8, LANES), lambda i: (i, 0, 0))],
        out_shape=[jax.ShapeDtypeStruct((n, d), F32),
                   jax.ShapeDtypeStruct((n, d), BF16),
                   jax.ShapeDtypeStruct((n, LANES), F32),
                   jax.ShapeDtypeStruct((n, LANES), F32),
                   jax.ShapeDtypeStruct((N_EXPERTS, n), F32),
                   jax.ShapeDtypeStruct((nt, 8, LANES), jnp.int32)],
        compiler_params=_params(1),
        name="moe_router",
    )(x2, nsa, ret, wo, g, rpad, rbpad, tri)


MOE_SUB = 144
MOE_MOVE = 2 * MOE_SUB


def _moe_kernel(cnt_ref, h_ref, rankT_ref, rank_ref, comb_ref, wgu_ref, wd_ref, x_ref, o_ref, hc_sc, oacc_sc):
    t, e, f = pl.program_id(0), pl.program_id(1), pl.program_id(2)
    nf = pl.num_programs(2)
    tm = h_ref.shape[0]
    nsub = (cnt_ref[t * N_EXPERTS + e] + (MOE_SUB - 1)) // MOE_SUB
    nmove = (nsub + 1) // 2

    @pl.when((e == 0) & (f == 0))
    def _():
        o_ref[...] = x_ref[...]

    @pl.when(f == 0)
    def _():
        rank_row = rankT_ref[...]

        def gather(s, c):
            r0 = pl.multiple_of(s * MOE_MOVE, MOE_MOVE)
            rows = (lax.broadcasted_iota(jnp.int32, (MOE_MOVE, 1), 0) + r0).astype(F32)
            onehot = (rows == rank_row).astype(BF16)
            hc_sc[pl.ds(r0, MOE_MOVE), :] = _dot(onehot, h_ref[...]).astype(BF16)
            oacc_sc[pl.ds(r0, MOE_MOVE), :] = jnp.zeros((MOE_MOVE, oacc_sc.shape[1]), F32)
            return c

        lax.fori_loop(0, nmove, gather, 0)

    def expert(n_rows, s, c):
        r0 = pl.multiple_of(s * n_rows, n_rows)
        act = _swiglu_hidden(hc_sc[pl.ds(r0, n_rows), :], wgu_ref)
        oacc_sc[pl.ds(r0, n_rows), :] += _dot(act, wd_ref[...])
        return c

    lax.fori_loop(0, nsub // 2, functools.partial(expert, MOE_MOVE), 0)
    lax.fori_loop(nsub // 2 * 2, nsub, functools.partial(expert, MOE_SUB), 0)

    @pl.when(f == nf - 1)
    def _():
        is_e = lax.broadcasted_iota(jnp.int32, (1, LANES), 1) == e
        rank_col = jnp.sum(jnp.where(is_e, rank_ref[...], 0.0), axis=1, keepdims=True)
        comb_col = jnp.sum(jnp.where(is_e, comb_ref[...], 0.0), axis=1, keepdims=True)

        def scatter(s, c):
            r0 = pl.multiple_of(s * MOE_MOVE, MOE_MOVE)
            cols = (lax.broadcasted_iota(jnp.int32, (1, MOE_MOVE), 1) + r0).astype(F32)
            onehot = (rank_col == cols).astype(BF16)
            y = _dot(onehot, oacc_sc[pl.ds(r0, MOE_MOVE), :].astype(BF16))
            o_ref[...] += comb_col * y
            return c

        lax.fori_loop(0, nmove, scatter, 0)


def _moe(counts, h, rankT, rank, comb, wgu, wd, x2, tm, fc):
    n, d = x2.shape
    dff = wd.shape[1]
    rows_cap = pl.cdiv(pl.cdiv(tm, MOE_SUB), 2) * MOE_MOVE
    grid_spec = pltpu.PrefetchScalarGridSpec(
        num_scalar_prefetch=1,
        grid=(n // tm, N_EXPERTS, dff // fc),
        in_specs=[pl.BlockSpec((tm, d), lambda t, e, f, c: (t, 0)),
                  pl.BlockSpec((None, 1, tm), lambda t, e, f, c: (e, 0, t)),
                  pl.BlockSpec((tm, LANES), lambda t, e, f, c: (t, 0)),
                  pl.BlockSpec((tm, LANES), lambda t, e, f, c: (t, 0)),
                  pl.BlockSpec((None, d, 2 * fc), lambda t, e, f, c: (e, 0, f)),
                  pl.BlockSpec((None, fc, d), lambda t, e, f, c: (e, f, 0)),
                  pl.BlockSpec((tm, d), lambda t, e, f, c: (t, 0))],
        out_specs=pl.BlockSpec((tm, d), lambda t, e, f, c: (t, 0)),
        scratch_shapes=[pltpu.VMEM((rows_cap, d), BF16), pltpu.VMEM((rows_cap, d), F32)],
    )
    return pl.pallas_call(
        _moe_kernel,
        grid_spec=grid_spec,
        out_shape=jax.ShapeDtypeStruct((n, d), F32),
        compiler_params=_params(3),
        name="moe_experts",
    )(counts, h, rankT.reshape(N_EXPERTS, 1, n), rank, comb, wgu, wd, x2)


def _permute_w_in(w):
    o = np.cumsum((0, NSA_WIDTH) + (KV_WIDTH,) * 6 + (3 * NSA_HEADS,))
    q, kc, vc, ks, vs, kw, vw, gts = (w[:, o[k]:o[k + 1]] for k in range(8))
    ret = w[:, o[8]:]
    pad = jnp.zeros((w.shape[0], LANES - 3 * NSA_HEADS), w.dtype)
    return jnp.concatenate([q, ks, kw, vs, vw, kc, vc, gts, pad, ret], axis=1).astype(BF16)


def _nsa_consts(T):
    ncp = T // CMP_STRIDE
    ns = T // SLC_LEN
    cs = np.arange(ncp) * CMP_STRIDE
    ss = np.arange(ns) * SLC_LEN
    ov = np.clip(np.minimum(cs[None, :] + CMP_LEN, ss[:, None] + SLC_LEN) - np.maximum(cs[None, :], ss[:, None]), 0, None)
    ovT = (ov.astype(np.float32) / CMP_LEN)
    ovT[:, ncp - 1] = 0.0
    h = np.arange(NSA_HEADS).reshape(NSA_KV_HEADS, NSA_GROUP) + 1
    slopes = np.exp2(-8.0 * h / NSA_HEADS).astype(np.float32)
    slopes = np.repeat(slopes, Q_BLOCK, axis=1)
    parts, rest = [], np.float64(LOG2E)
    for _ in range(3):
        part = np.float64(np.asarray(rest).astype(BF16))
        parts.append(part)
        rest = rest - part
    qaug = np.zeros((NSA_KV_HEADS, HEAD_DIM, GQ), np.float32)
    for k, part in enumerate(parts):
        qaug[:, k, :] = part * SLC_LEN * slopes
        qaug[:, 3 + k, :] = part * slopes
    kq = np.arange(Q_BLOCK)[:, None] - np.arange(Q_BLOCK)[None, :]
    causb = np.where(kq <= 0, 0.0, NEG).astype(np.float32)
    lowb = np.where(kq > 0, 0.0, NEG).astype(np.float32)
    return jnp.asarray(ovT, BF16), jnp.asarray(qaug, BF16), jnp.asarray(lowb), jnp.asarray(causb)


def _mixer(x2, B, T, norm_g, w_in, q_norm_g, k_norm_g, cmp_pos, w_cmp, ret_norm_g, w_out):
    ns = T // SLC_LEN
    kc, vc, pret, qT, ks, kw, vsT, vwT, gT = _inproj(x2, norm_g[None, :], _permute_w_in(w_in), q_norm_g[None, :],
                                                     k_norm_g[1:3], B, T)
    wk, pk = _compress_weights(w_cmp[0], cmp_pos[0])
    wv, pv = _compress_weights(w_cmp[1], cmp_pos[1])
    kcmp, vcT = _compress(kc, vc, wk, wv, pk, pv, k_norm_g[0:1], B, T)
    ovT, qaug, lowb, causb = _nsa_consts(T)
    ocmp, sel, flags = _nsa_cmp(qT, qaug, kcmp, vcT, ovT, B, T)
    lists, counts = _nsa_steps(flags[:, :, :, 0, :].reshape(-1, ns), T // Q_BLOCK)
    kpad = jnp.zeros((WIN, 2 * HEAD_DIM), BF16).at[:, HEAD_DIM:HEAD_DIM + 3].set(-2.0 ** 100)
    kw = jnp.concatenate([jnp.broadcast_to(kpad, kw.shape[:2] + kpad.shape), kw], axis=2)
    vwT = jnp.pad(vwT, ((0, 0), (0, 0), (0, 0), (WIN, 0)))
    nsa = _nsa_main(lists, counts, qT, qaug, ks, vsT, kw, vwT, sel, gT, ocmp, lowb, causb, B, T)
    ret = _retention(pret, ret_norm_g[None, :], B, T)
    return nsa.reshape(B * T, NSA_WIDTH), ret, w_out.astype(BF16)


def _moe_layer(x2, nsa, ret, wo, norm_g, router, router_b, wg, wu, wd, tm=1024):
    tm = min(tm, x2.shape[0])
    x1, h, rank, comb, rankT, cnt = _router(x2, nsa, ret, wo, norm_g[None, :], router, router_b, tm)
    counts = cnt[:, 0, :N_EXPERTS].reshape(-1)
    return _moe(counts, h, rankT, rank, comb, _gate_up(wg, wu, FF_CHUNK), wd.astype(BF16), x1, tm, FF_CHUNK)


def kernel(x, norm_mix_g, w_in, q_norm_g, k_norm_g, cmp_pos, w_cmp, ret_norm_g, w_out, norm_ffn_g,
           ffn_w_gate, ffn_w_up, ffn_w_down, moe_router, moe_router_b, moe_w_gate, moe_w_up, moe_w_down):
    B, T, D = x.shape
    depth = norm_mix_g.shape[0]
    x2 = x.reshape(B * T, D)
    for l in range(depth):
        nsa, ret, wo = _mixer(x2, B, T, norm_mix_g[l], w_in[l], q_norm_g[l], k_norm_g[l], cmp_pos[l], w_cmp[l],
                              ret_norm_g[l], w_out[l])
        j = l // 2
        if l % 2 == 0:
            x2 = _ffn(x2, nsa, ret, wo, norm_ffn_g[l][None, :], _gate_up(ffn_w_gate[j], ffn_w_up[j], FF_CHUNK),
                      ffn_w_down[j].astype(BF16), FF_CHUNK)
        else:
            x2 = _moe_layer(x2, nsa, ret, wo, norm_ffn_g[l], moe_router[j], moe_router_b[j], moe_w_gate[j],
                            moe_w_up[j], moe_w_down[j])
    return x2.reshape(B, T, D)
```

```python
import functools

import numpy as np
import jax
import jax.numpy as jnp
from jax import lax
from jax.experimental import pallas as pl
from jax.experimental.pallas import tpu as pltpu

F32 = jnp.float32
BF16 = jnp.bfloat16

HEAD_DIM = 64
NSA_HEADS = 8
NSA_KV_HEADS = 2
NSA_GROUP = NSA_HEADS // NSA_KV_HEADS
RET_HEADS = 8
RET_DK = 32
RET_DV = 64
NSA_WIDTH = NSA_HEADS * HEAD_DIM
RET_WIDTH = RET_HEADS * RET_DV
KV_WIDTH = NSA_KV_HEADS * HEAD_DIM
CMP_LEN = 32
CMP_STRIDE = 16
SLC_LEN = 64
SLC_TOPK = 16
WIN = 512
Q_BLOCK = 128
RET_CHUNK = 128
N_EXPERTS = 8
EPS = 1e-6
NEG = -1e30
BIG = 1e9
LANES = 128
GQ = NSA_GROUP * Q_BLOCK
KEY_STEP = 128
STEP_GROUP = 4
N_FORCED = 3
CMP_CHUNK = 128
CMP_TILES = 8
MAIN_TILES = 8
LOOP_TILES = 2
CMP_TAIL = CMP_CHUNK + 8
WIN_KEYS = WIN + Q_BLOCK
V_ROWS = HEAD_DIM + 16
LOG2E = 1.4426950408889634
FF_CHUNK = 1408
VMEM_LIMIT = 60 * 1024 * 1024

_C_Q = 0
_C_KV = _C_Q + NSA_WIDTH
_C_KC = _C_KV + 4 * KV_WIDTH
_C_VC = _C_KC + KV_WIDTH
_C_GT = _C_VC + KV_WIDTH
_C_RET = _C_GT + LANES
_RET_COLS = 2 * RET_HEADS * RET_DK + 2 * RET_WIDTH
_C_END = _C_RET + _RET_COLS


def _params(n_axes, vmem=VMEM_LIMIT):
    return pltpu.CompilerParams(dimension_semantics=("arbitrary",) * n_axes, vmem_limit_bytes=vmem)


def _dot(a, b):
    return jnp.dot(a, b, preferred_element_type=F32)


def _dot_nt(a, b):
    return lax.dot_general(a, b, (((1,), (1,)), ((), ())), preferred_element_type=F32)


def _rms(x, g):
    return x * lax.rsqrt(jnp.mean(x * x, axis=-1, keepdims=True) + EPS) * g


def _group_rms(x, g, ones_ref):
    w = x.shape[1]
    ones = ones_ref[0:w, 0:w]
    sq = x * x
    hi = sq.astype(BF16)
    lo = (sq - hi.astype(F32)).astype(BF16)
    ms = (_dot(hi, ones) + _dot(lo, ones)) * (1.0 / HEAD_DIM)
    return x * lax.rsqrt(ms + EPS) * g


def _group_ones():
    lane = np.arange(NSA_WIDTH) // HEAD_DIM
    return jnp.asarray(lane[:, None] == lane[None, :], BF16)


def _inproj_kernel(x_ref, g_ref, w_ref, qg_ref, kg_ref, ones_ref, kc_ref, vc_ref, ret_ref,
                   qT_ref, ks_ref, kw_ref, vsT_ref, vwT_ref, gT_ref, *, steps_per_row):
    n_tok = x_ref.shape[0]
    tiles = range(n_tok // Q_BLOCK)
    rows = [slice(u * Q_BLOCK, (u + 1) * Q_BLOCK) for u in tiles]
    xn = _rms(x_ref[...], g_ref[...]).astype(BF16)
    q = _dot(xn, w_ref[:, _C_Q:_C_KV])
    mid = _dot(xn, w_ref[:, _C_KV:_C_RET])
    kv = mid[:, 0:_C_KC - _C_KV]
    kc_ref[...] = mid[:, _C_KC - _C_KV:_C_VC - _C_KV]
    vc_ref[...] = mid[:, _C_VC - _C_KV:_C_GT - _C_KV]
    gt = mid[:, _C_GT - _C_KV:_C_RET - _C_KV]

    scale = HEAD_DIM ** -0.5 * LOG2E
    qn = _group_rms(q, qg_ref[...], ones_ref) * scale
    qt = [qn[rows[u]].T for u in tiles]
    for g in range(NSA_KV_HEADS):
        for r in range(NSA_GROUP):
            h = g * NSA_GROUP + r
            for u in tiles:
                qT_ref[g, u, :, r * Q_BLOCK:(r + 1) * Q_BLOCK] = qt[u][h * HEAD_DIM:(h + 1) * HEAD_DIM, :].astype(BF16)
    vst = [kv[rows[u], 2 * KV_WIDTH:3 * KV_WIDTH].T for u in tiles]
    vwt = [kv[rows[u], 3 * KV_WIDTH:4 * KV_WIDTH].T for u in tiles]
    gts = [jax.nn.sigmoid(gt[rows[u], :].T[0:32, :]) for u in tiles]
    pos0 = (pl.program_id(0) % steps_per_row) * n_tok
    pos = pos0 + lax.broadcasted_iota(jnp.int32, (n_tok, HEAD_DIM), 0)
    col = lax.broadcasted_iota(jnp.int32, (n_tok, HEAD_DIM), 1)
    kpos = jnp.where(col < 3, pos // SLC_LEN, jnp.where(col < 6, pos % SLC_LEN, 0)).astype(F32)
    ones_row = (lax.broadcasted_iota(jnp.int32, (V_ROWS - HEAD_DIM, Q_BLOCK), 0) == 0).astype(F32)
    ks = _group_rms(kv[:, 0:KV_WIDTH], kg_ref[0:1, :], ones_ref)
    kw = _group_rms(kv[:, KV_WIDTH:2 * KV_WIDTH], kg_ref[1:2, :], ones_ref)
    for g in range(NSA_KV_HEADS):
        sl = slice(g * HEAD_DIM, (g + 1) * HEAD_DIM)
        ks_ref[g] = jnp.concatenate([ks[:, sl], kpos], axis=1).astype(BF16)
        kw_ref[g] = jnp.concatenate([kw[:, sl], kpos], axis=1).astype(BF16)
        for u in tiles:
            vsT_ref[g, :, rows[u]] = jnp.concatenate([vst[u][sl, :], ones_row], axis=0).astype(BF16)
            vwT_ref[g, :, rows[u]] = jnp.concatenate([vwt[u][sl, :], ones_row], axis=0).astype(BF16)
    for u in tiles:
        gT_ref[u] = gts[u]
    ret_ref[...] = _dot(xn, w_ref[:, _C_RET:_C_END])


def _inproj(x2, g, w, qg, kg, B, T, tm=1024):
    n, d = x2.shape
    nq = T // Q_BLOCK
    G = NSA_KV_HEADS
    tiles = tm // Q_BLOCK
    spr = T // tm
    assert T % tm == 0
    const = lambda i: (0, 0)
    row = lambda i: (i, 0)
    return pl.pallas_call(
        functools.partial(_inproj_kernel, steps_per_row=spr),
        grid=(n // tm,),
        in_specs=[pl.BlockSpec((tm, d), row),
                  pl.BlockSpec((1, d), const),
                  pl.BlockSpec((d, _C_END), const),
                  pl.BlockSpec((1, NSA_WIDTH), const),
                  pl.BlockSpec((2, KV_WIDTH), const),
                  pl.BlockSpec((NSA_WIDTH, NSA_WIDTH), const)],
        out_specs=[pl.BlockSpec((tm, KV_WIDTH), row),
                   pl.BlockSpec((tm, KV_WIDTH), row),
                   pl.BlockSpec((tm, _RET_COLS), row),
                   pl.BlockSpec((None, G, tiles, HEAD_DIM, GQ), lambda i: (i // spr, 0, i % spr, 0, 0)),
                   pl.BlockSpec((None, G, tm, 2 * HEAD_DIM), lambda i: (i // spr, 0, i % spr, 0)),
                   pl.BlockSpec((None, G, tm, 2 * HEAD_DIM), lambda i: (i // spr, 0, i % spr, 0)),
                   pl.BlockSpec((None, G, V_ROWS, tm), lambda i: (i // spr, 0, 0, i % spr)),
                   pl.BlockSpec((None, G, V_ROWS, tm), lambda i: (i // spr, 0, 0, i % spr)),
                   pl.BlockSpec((None, tiles, 32, Q_BLOCK), lambda i: (i // spr, i % spr, 0, 0))],
        out_shape=[jax.ShapeDtypeStruct((n, KV_WIDTH), F32),
                   jax.ShapeDtypeStruct((n, KV_WIDTH), F32),
                   jax.ShapeDtypeStruct((n, _RET_COLS), F32),
                   jax.ShapeDtypeStruct((B, G, nq, HEAD_DIM, GQ), BF16),
                   jax.ShapeDtypeStruct((B, G, T, 2 * HEAD_DIM), BF16),
                   jax.ShapeDtypeStruct((B, G, T, 2 * HEAD_DIM), BF16),
                   jax.ShapeDtypeStruct((B, G, V_ROWS, T), BF16),
                   jax.ShapeDtypeStruct((B, G, V_ROWS, T), BF16),
                   jax.ShapeDtypeStruct((B, nq, 32, Q_BLOCK), F32)],
        compiler_params=_params(1),
        name="inproj",
    )(x2, g, w, jnp.tile(qg, (1, NSA_HEADS)), jnp.tile(kg, (1, NSA_KV_HEADS)), _group_ones())


def _compress_kernel(kc_ref, vc_ref, wk_ref, wv_ref, pk_ref, pv_ref, kg_ref, kcmp_ref, vcT_ref):
    ncp = kc_ref.shape[0] // CMP_STRIDE

    def comp(a_ref, w_ref, p_ref):
        lo = jnp.zeros((ncp, KV_WIDTH), F32)
        hi = jnp.zeros((ncp, KV_WIDTH), F32)
        for l in range(CMP_STRIDE):
            a = a_ref[pl.ds(l, ncp, stride=CMP_STRIDE), :]
            lo += _dot((a + p_ref[0, l:l + 1, :]).astype(BF16), w_ref[0, l])
            hi += _dot((a + p_ref[1, l:l + 1, :]).astype(BF16), w_ref[1, l])
        return lo + pltpu.roll(hi, ncp - 1, 0)

    k = comp(kc_ref, wk_ref, pk_ref)
    v = comp(vc_ref, wv_ref, pv_ref).T
    cend = lax.broadcasted_iota(jnp.int32, (ncp, HEAD_DIM), 0) * CMP_STRIDE + (CMP_LEN - 1)
    col = lax.broadcasted_iota(jnp.int32, (ncp, HEAD_DIM), 1)
    kpos = jnp.where(col < 3, cend // SLC_LEN, jnp.where(col < 6, cend % SLC_LEN, 0)).astype(F32)
    for g in range(NSA_KV_HEADS):
        sl = slice(g * HEAD_DIM, (g + 1) * HEAD_DIM)
        kcmp_ref[g] = jnp.concatenate([_rms(k[:, sl], kg_ref[...]), kpos], axis=1).astype(BF16)
        vcT_ref[g] = v[sl, :].astype(BF16)


def _compress(kc, vc, wk, wv, pk, pv, kg, B, T):
    ncp = T // CMP_STRIDE
    G = NSA_KV_HEADS
    const4 = lambda b: (0, 0, 0, 0)
    const3 = lambda b: (0, 0, 0)
    const2 = lambda b: (0, 0)
    return pl.pallas_call(
        _compress_kernel,
        grid=(B,),
        in_specs=[pl.BlockSpec((T, KV_WIDTH), lambda b: (b, 0)),
                  pl.BlockSpec((T, KV_WIDTH), lambda b: (b, 0)),
                  pl.BlockSpec((2, CMP_STRIDE, KV_WIDTH, KV_WIDTH), const4),
                  pl.BlockSpec((2, CMP_STRIDE, KV_WIDTH, KV_WIDTH), const4),
                  pl.BlockSpec((2, CMP_STRIDE, KV_WIDTH), const3),
                  pl.BlockSpec((2, CMP_STRIDE, KV_WIDTH), const3),
                  pl.BlockSpec((1, HEAD_DIM), const2)],
        out_specs=[pl.BlockSpec((None, G, ncp, 2 * HEAD_DIM), lambda b: (b, 0, 0, 0)),
                   pl.BlockSpec((None, G, HEAD_DIM, ncp), lambda b: (b, 0, 0, 0))],
        out_shape=[jax.ShapeDtypeStruct((B, G, ncp, 2 * HEAD_DIM), BF16),
                   jax.ShapeDtypeStruct((B, G, HEAD_DIM, ncp), BF16)],
        compiler_params=_params(1),
        name="nsa_compress",
    )(kc, vc, wk, wv, pk, pv, kg)


def _compress_weights(w, pos):
    G = NSA_KV_HEADS
    w4 = w.reshape(2, CMP_STRIDE, HEAD_DIM, HEAD_DIM)
    eye = jnp.eye(G, dtype=w.dtype)
    wbd = jnp.einsum('hlde,gk->hlgdke', w4, eye).reshape(2, CMP_STRIDE, KV_WIDTH, KV_WIDTH)
    p = pos.reshape(2, CMP_STRIDE, 1, HEAD_DIM)
    p = jnp.broadcast_to(p, (2, CMP_STRIDE, G, HEAD_DIM)).reshape(2, CMP_STRIDE, KV_WIDTH)
    return wbd.astype(BF16), p


def _split3(x):
    hi = x.astype(BF16)
    r = x - hi.astype(F32)
    mid = r.astype(BF16)
    lo = (r - mid.astype(F32)).astype(BF16)
    return hi, mid, lo


def _nsa_cmp_kernel(qT_ref, qaug_ref, kc_ref, vcT_ref, ovT_ref, ocmp_ref, sel_ref, flag_ref, *, n_sel):
    ncp = kc_ref.shape[0]
    ns = ovT_ref.shape[0]
    tiles = range(CMP_TILES)
    i0 = pl.program_id(2) * CMP_TILES
    lane = lax.broadcasted_iota(jnp.int32, (1, GQ), 1)
    q = [jnp.concatenate([qT_ref[u], qaug_ref[...]], axis=0) for u in tiles]
    t_row = [(i0 + u) * Q_BLOCK + (lane & (Q_BLOCK - 1)) for u in tiles]
    has_cmp = [(t_row[u] >= CMP_LEN - 1).astype(F32) for u in tiles]
    tq = [(i0 + u) * Q_BLOCK + lax.broadcasted_iota(jnp.int32, (1, Q_BLOCK), 1) for u in tiles]
    cur = [tq[u] // SLC_LEN for u in tiles]

    def prefix(rows):
        nsk = rows * CMP_STRIDE // SLC_LEN
        tail0 = max(rows - CMP_TAIL, 0)
        kc = kc_ref[0:rows, :]
        s = [_dot(kc, q[u]) for u in tiles]
        cend = (lax.broadcasted_iota(jnp.int32, (rows - tail0, 1), 0) + tail0) * CMP_STRIDE + (CMP_LEN - 1)
        tail = [jnp.where(t_row[u] >= cend, s[u][tail0:], NEG) for u in tiles]
        s = [jnp.concatenate([s[u][0:tail0], tail[u]], axis=0) if tail0 else tail[u] for u in tiles]
        m = [jnp.max(s[u], axis=0, keepdims=True) for u in tiles]
        e = [jnp.exp2(s[u] - m[u]) for u in tiles]
        p = [e[u] * (has_cmp[u] / jnp.sum(e[u], axis=0, keepdims=True)) for u in tiles]
        vc = vcT_ref[:, 0:rows]
        for u in tiles:
            ocmp_ref[u] = _dot(vc, p[u].astype(BF16))

        ps = [p[u][:, 0:Q_BLOCK] for u in tiles]
        for r in range(1, NSA_GROUP):
            ps = [ps[u] + p[u][:, r * Q_BLOCK:(r + 1) * Q_BLOCK] for u in tiles]
        ov = ovT_ref[0:nsk, 0:rows]
        split = [_split3(ps[u]) for u in tiles]
        imp = [_dot(ov, split[u][0]) + _dot(ov, split[u][1]) + _dot(ov, split[u][2]) for u in tiles]

        blk = lax.broadcasted_iota(jnp.int32, (nsk, 1), 0)
        forced = [(blk == 0) | (blk == cur[u]) | (blk == cur[u] - 1) for u in tiles]
        valid = [blk * SLC_LEN <= tq[u] for u in tiles]
        imp = [jnp.where(forced[u], -3e38, jnp.where(valid[u], imp[u], -BIG)) for u in tiles]
        blk_f = blk.astype(F32)
        sel = [forced[u].astype(F32) for u in tiles]
        for _ in range(n_sel - N_FORCED):
            mx = [jnp.max(imp[u], axis=0, keepdims=True) for u in tiles]
            idx = [jnp.min(jnp.where(imp[u] == mx[u], blk_f, float(ns)), axis=0, keepdims=True) for u in tiles]
            pick = [blk_f == idx[u] for u in tiles]
            sel = [jnp.where(pick[u], 1.0, sel[u]) for u in tiles]
            imp = [jnp.where(pick[u], -3e38, imp[u]) for u in tiles]
        ones = jnp.ones((8, Q_BLOCK), BF16)
        for u in tiles:
            sel_ref[u, 0:nsk, :] = sel[u]
            cnt = _dot_nt(ones, sel[u].astype(BF16))
            flag_ref[u, :, 0:nsk] = (cnt > 0).astype(jnp.int32)
            if nsk < ns:
                sel_ref[u, nsk:, :] = jnp.zeros((ns - nsk, Q_BLOCK), F32)
                flag_ref[u, :, nsk:] = jnp.zeros((8, ns - nsk), jnp.int32)

    n_variants = ncp // CMP_CHUNK
    last = i0 + CMP_TILES - 1
    variant = (last * (Q_BLOCK // CMP_STRIDE) + (Q_BLOCK // CMP_STRIDE - 2)) // CMP_CHUNK
    for k in range(n_variants):
        pl.when(variant == k)(functools.partial(prefix, (k + 1) * CMP_CHUNK))


def _nsa_cmp(qT, qaug, kcmp, vcT, ovT, B, T):
    G = NSA_KV_HEADS
    nq = T // Q_BLOCK
    ncp = T // CMP_STRIDE
    ns = T // SLC_LEN
    n_sel = min(SLC_TOPK, ns)
    assert ncp % CMP_CHUNK == 0 and n_sel > N_FORCED and nq % CMP_TILES == 0
    tile = lambda b, g, i: (b, g, i, 0, 0)
    return pl.pallas_call(
        functools.partial(_nsa_cmp_kernel, n_sel=n_sel),
        grid=(B, G, nq // CMP_TILES),
        in_specs=[pl.BlockSpec((None, None, CMP_TILES, HEAD_DIM, GQ), tile),
                  pl.BlockSpec((None, HEAD_DIM, GQ), lambda b, g, i: (g, 0, 0)),
                  pl.BlockSpec((None, None, ncp, 2 * HEAD_DIM), lambda b, g, i: (b, g, 0, 0)),
                  pl.BlockSpec((None, None, HEAD_DIM, ncp), lambda b, g, i: (b, g, 0, 0)),
                  pl.BlockSpec((ns, ncp), lambda b, g, i: (0, 0))],
        out_specs=[pl.BlockSpec((None, None, CMP_TILES, HEAD_DIM, GQ), tile),
                   pl.BlockSpec((None, None, CMP_TILES, ns, Q_BLOCK), tile),
                   pl.BlockSpec((None, None, CMP_TILES, 8, ns), tile)],
        out_shape=[jax.ShapeDtypeStruct((B, G, nq, HEAD_DIM, GQ), F32),
                   jax.ShapeDtypeStruct((B, G, nq, ns, Q_BLOCK), F32),
                   jax.ShapeDtypeStruct((B, G, nq, 8, ns), jnp.int32)],
        compiler_params=_params(3),
        name="nsa_cmp",
    )(qT, qaug, kcmp, vcT, ovT)


def _nsa_main_kernel(list_ref, cnt_ref, qT_ref, qaug_ref, ks_ref, vsT_ref, kw_ref, vwT_ref, sel_ref, gT_ref, ocmp_ref,
                     lowb_ref, causb_ref, out_ref, m_sc, acc_sc, win_sc):
    b, g = pl.program_id(0), pl.program_id(1)
    tiles = range(MAIN_TILES)
    i = [pl.program_id(2) * MAIN_TILES + u for u in tiles]
    tile_id = [(b * pl.num_programs(1) + g) * (pl.num_programs(2) * MAIN_TILES) + i[u] for u in tiles]
    n_steps = sel_ref.shape[1] // 2
    q = [jnp.concatenate([qT_ref[u], qaug_ref[...]], axis=0) for u in tiles]
    k0 = [pl.multiple_of(i[u] * Q_BLOCK, Q_BLOCK) for u in tiles]

    def sel_bias(u, j, valid):
        def row(r):
            picked = (sel_ref[u, pl.ds(r, 1), :] > 0.5) & valid
            return jnp.concatenate([jnp.where(picked, 0.0, NEG)] * NSA_GROUP, axis=1)
        return row(2 * j), row(2 * j + 1)

    def add_sel_bias(s, ba, bb):
        return jnp.concatenate([s[0:SLC_LEN] + ba, s[SLC_LEN:] + bb], axis=0)

    lowb = jnp.concatenate([lowb_ref[...]] * NSA_GROUP, axis=1)
    causb = jnp.concatenate([causb_ref[...]] * NSA_GROUP, axis=1)

    bias_d = [sel_bias(u, i[u], True) for u in tiles]
    sd = [_dot(ks_ref[pl.ds(k0[u], KEY_STEP), :], q[u]) for u in tiles]
    sw = [_dot(kw_ref[pl.ds(k0[u], WIN_KEYS), :], q[u]) for u in tiles]
    sd = [add_sel_bias(sd[u], *bias_d[u]) + causb for u in tiles]
    sw = [jnp.concatenate([sw[u][0:Q_BLOCK] + lowb, sw[u][Q_BLOCK:WIN], sw[u][WIN:] + causb], axis=0) for u in tiles]
    md = [jnp.max(sd[u], axis=0, keepdims=True) for u in tiles]
    mw = [jnp.max(sw[u], axis=0, keepdims=True) for u in tiles]
    accd = [_dot(vsT_ref[:, pl.ds(k0[u], KEY_STEP)], jnp.exp2((sd[u] - md[u]).astype(BF16))) for u in tiles]
    ow = [_dot(vwT_ref[:, pl.ds(k0[u], WIN_KEYS)], jnp.exp2((sw[u] - mw[u]).astype(BF16))) for u in tiles]
    for u in tiles:
        m_sc[u] = md[u]
        acc_sc[u] = accd[u]
        win_sc[u] = ow[u][0:HEAD_DIM] / ow[u][HEAD_DIM:HEAD_DIM + 1]

    def scores(u, t):
        ks, vs, biases = [], [], []
        for x in range(STEP_GROUP):
            j = list_ref[tile_id[u] * n_steps + t * STEP_GROUP + x]
            valid = j >= 0
            j = jnp.maximum(j, 0)
            kj = pl.multiple_of(j * KEY_STEP, KEY_STEP)
            ks.append(ks_ref[pl.ds(kj, KEY_STEP), :])
            vs.append(vsT_ref[:, pl.ds(kj, KEY_STEP)])
            biases.append(sel_bias(u, j, valid))
        s = _dot(jnp.concatenate(ks, axis=0), q[u])
        s = jnp.concatenate([add_sel_bias(s[x * KEY_STEP:(x + 1) * KEY_STEP], *biases[x])
                             for x in range(STEP_GROUP)], axis=0)
        return s, jnp.max(s, axis=0, keepdims=True), jnp.concatenate(vs, axis=1)

    def accumulate(u, s, smax, vcat):
        m_old = m_sc[u]
        m_new = jnp.maximum(m_old, smax)
        alpha = jnp.exp2(m_old - m_new)
        acc_sc[u] = alpha * acc_sc[u] + _dot(vcat, jnp.exp2((s - m_new).astype(BF16)))
        m_sc[u] = m_new

    def run(work, t, carry):
        staged = [(u, scores(u, t * mult + off)) for (u, mult, off) in work]
        for u, args in staged:
            accumulate(u, *args)
        return carry

    for u0 in range(0, MAIN_TILES, LOOP_TILES):
        us = range(u0, u0 + LOOP_TILES)
        n_groups = functools.reduce(
            jnp.maximum, [(cnt_ref[tile_id[u]] + (STEP_GROUP - 1)) // STEP_GROUP for u in us])
        lax.fori_loop(0, n_groups // 2, functools.partial(run, [(u, 2, off) for off in (0, 1) for u in us]), 0)
        lax.fori_loop(n_groups // 2 * 2, n_groups, functools.partial(run, [(u, 1, 0) for u in us]), 0)

    def gate(u, k):
        rows = [gT_ref[u, pl.ds(g * (NSA_GROUP * 3) + r * 3 + k, 1), :] for r in range(NSA_GROUP)]
        return jnp.concatenate(rows, axis=1)

    o_slc = [acc_sc[u, 0:HEAD_DIM, :] / acc_sc[u, HEAD_DIM:HEAD_DIM + 1, :] for u in tiles]
    o = [gate(u, 0) * ocmp_ref[u] + gate(u, 1) * o_slc[u] + gate(u, 2) * win_sc[u] for u in tiles]
    o = [jnp.concatenate([o[u], jnp.zeros_like(o[u])], axis=0) for u in tiles]
    for r in range(NSA_GROUP):
        ot = [o[u][:, r * Q_BLOCK:(r + 1) * Q_BLOCK].T[:, 0:HEAD_DIM] for u in tiles]
        for u in tiles:
            out_ref[u * Q_BLOCK:(u + 1) * Q_BLOCK, r * HEAD_DIM:(r + 1) * HEAD_DIM] = ot[u]


def _nsa_steps_kernel(flagT_ref, pairT_ref, list_ref, cnt_ref, *, nq):
    n_steps, nt = list_ref.shape
    need = _dot(pairT_ref[...], flagT_ref[...].astype(BF16)) > 0
    step = lax.broadcasted_iota(jnp.int32, (n_steps, 1), 0)
    own = lax.broadcasted_iota(jnp.int32, (1, nt), 1) % nq
    need = need & (step < own)
    need_f = need.astype(F32)
    earlier = (lax.broadcasted_iota(jnp.int32, (n_steps, n_steps), 1) < step).astype(BF16)
    slot = _dot(earlier, need_f.astype(BF16))
    total = jnp.sum(need_f, axis=0, keepdims=True)
    cnt_ref[...] = jnp.broadcast_to(total, cnt_ref.shape).astype(jnp.int32)
    step_f = step.astype(F32)
    for p in range(n_steps):
        val = jnp.sum(jnp.where(need & (slot == p), step_f, 0.0), axis=0, keepdims=True)
        list_ref[p:p + 1, :] = jnp.where(total > p, val, -1.0).astype(jnp.int32)


def _nsa_steps(flags, nq):
    nt, ns = flags.shape
    n_steps = ns // 2
    pairT = jnp.asarray(np.arange(n_steps)[:, None] == np.arange(ns)[None, :] // 2, BF16)
    lists, counts = pl.pallas_call(
        functools.partial(_nsa_steps_kernel, nq=nq),
        out_shape=[jax.ShapeDtypeStruct((n_steps, nt), jnp.int32), jax.ShapeDtypeStruct((8, nt), jnp.int32)],
        name="nsa_steps",
    )(flags.T.astype(F32), pairT)
    return lists.T.reshape(-1), counts[0]


def _nsa_main(lists, counts, qT, qaug, ks, vsT, kw, vwT, sel, gT, ocmp, lowb, causb, B, T):
    G = NSA_KV_HEADS
    nq = T // Q_BLOCK
    ns = T // SLC_LEN
    whole = lambda b, g, i, *_: (b, g, 0, 0)
    tile = lambda b, g, i, *_: (b, g, i, 0, 0)
    const = lambda b, g, i, *_: (0, 0)
    grid_spec = pltpu.PrefetchScalarGridSpec(
        num_scalar_prefetch=2,
        grid=(B, G, nq // MAIN_TILES),
        in_specs=[pl.BlockSpec((None, None, MAIN_TILES, HEAD_DIM, GQ), tile),
                  pl.BlockSpec((None, HEAD_DIM, GQ), lambda b, g, i, *_: (g, 0, 0)),
                  pl.BlockSpec((None, None, T, 2 * HEAD_DIM), whole),
                  pl.BlockSpec((None, None, V_ROWS, T), whole),
                  pl.BlockSpec((None, None, T + WIN, 2 * HEAD_DIM), whole),
                  pl.BlockSpec((None, None, V_ROWS, T + WIN), whole),
                  pl.BlockSpec((None, None, MAIN_TILES, ns, Q_BLOCK), tile),
                  pl.BlockSpec((None, MAIN_TILES, 32, Q_BLOCK), lambda b, g, i, *_: (b, i, 0, 0)),
                  pl.BlockSpec((None, None, MAIN_TILES, HEAD_DIM, GQ), tile),
                  pl.BlockSpec((Q_BLOCK, Q_BLOCK), const),
                  pl.BlockSpec((Q_BLOCK, Q_BLOCK), const)],
        out_specs=pl.BlockSpec((None, MAIN_TILES * Q_BLOCK, NSA_GROUP * HEAD_DIM), lambda b, g, i, *_: (b, i, g)),
        scratch_shapes=[pltpu.VMEM((MAIN_TILES, 1, GQ), F32), pltpu.VMEM((MAIN_TILES, V_ROWS, GQ), F32),
                        pltpu.VMEM((MAIN_TILES, HEAD_DIM, GQ), F32)],
    )
    return pl.pallas_call(
        _nsa_main_kernel,
        grid_spec=grid_spec,
        out_shape=jax.ShapeDtypeStruct((B, T, NSA_WIDTH), F32),
        compiler_params=_params(3),
        name="nsa_main",
    )(lists, counts, qT, qaug, ks, vsT, kw, vwT, sel, gT, ocmp, lowb, causb)


def _ret_kernel(p_ref, decay_ref, xi_ref, zeta_ref, gch_ref, ng_ref, ones_ref, out_ref, state_ref):
    @pl.when(pl.program_id(0) == 0)
    def _():
        state_ref[...] = jnp.zeros(state_ref.shape, F32)

    rows = range(p_ref.shape[0])
    kw = RET_HEADS * RET_DK
    p = [p_ref[b] for b in rows]
    rq = [p[b][:, 0:kw] * (RET_DK ** -0.5) for b in rows]
    rk = [p[b][:, kw:2 * kw] for b in rows]
    rkT = [rk[b].T for b in rows]
    rv = [p[b][:, 2 * kw:2 * kw + RET_WIDTH] for b in rows]
    xi = xi_ref[...]
    outs = [[] for _ in rows]
    for h in range(RET_HEADS):
        dk = slice(h * RET_DK, (h + 1) * RET_DK)
        dv = slice(h * RET_DV, (h + 1) * RET_DV)
        st = [state_ref[b, h] for b in rows]
        inner = [_dot_nt(rq[b][:, dk], rk[b][:, dk]) * decay_ref[h] for b in rows]
        o = [_dot(inner[b], rv[b][:, dv]) + _dot(rq[b][:, dk], st[b]) * xi[:, h:h + 1] for b in rows]
        for b in rows:
            state_ref[b, h] = (st[b] * gch_ref[h:h + 1, 0:1]
                               + _dot(rkT[b][dk, :] * zeta_ref[h:h + 1, :], rv[b][:, dv]))
            outs[b].append(o[b])
    normed = [_group_rms(jnp.concatenate(outs[b], axis=1), ng_ref[...], ones_ref) for b in rows]
    for b in rows:
        rg = p[b][:, 2 * kw + RET_WIDTH:2 * kw + 2 * RET_WIDTH]
        out_ref[b] = normed[b] * (rg * jax.nn.sigmoid(rg))


def _ret_consts():
    H, C = RET_HEADS, RET_CHUNK
    log_g = np.log1p(-np.exp2(-5.0 - np.arange(H, dtype=np.float64)))
    idx = np.arange(C, dtype=np.float64)
    diff = idx[:, None] - idx[None, :]
    decay = np.where(diff >= 0, np.exp(np.maximum(diff, 0.0) * log_g[:, None, None]), 0.0)
    zeta = np.exp((C - 1 - idx) * log_g[:, None])
    xi = np.exp((idx + 1) * log_g[:, None]).T
    g_chunk = np.broadcast_to(np.exp(C * log_g)[:, None], (H, LANES))
    return tuple(jnp.asarray(a, F32) for a in (decay, xi, zeta, g_chunk))


def _retention(pret, ng, B, T):
    nch = T // RET_CHUNK
    decay, xi, zeta, gch = _ret_consts()
    c2 = lambda c: (0, 0)
    out = pl.pallas_call(
        _ret_kernel,
        grid=(nch,),
        in_specs=[pl.BlockSpec((B, RET_CHUNK, _RET_COLS), lambda c: (0, c, 0)),
                  pl.BlockSpec((RET_HEADS, RET_CHUNK, RET_CHUNK), lambda c: (0, 0, 0)),
                  pl.BlockSpec((RET_CHUNK, RET_HEADS), c2),
                  pl.BlockSpec((RET_HEADS, RET_CHUNK), c2),
                  pl.BlockSpec((RET_HEADS, LANES), c2),
                  pl.BlockSpec((1, RET_WIDTH), c2),
                  pl.BlockSpec((RET_WIDTH, RET_WIDTH), c2)],
        out_specs=pl.BlockSpec((B, RET_CHUNK, RET_WIDTH), lambda c: (0, c, 0)),
        out_shape=jax.ShapeDtypeStruct((B, T, RET_WIDTH), F32),
        scratch_shapes=[pltpu.VMEM((B, RET_HEADS, RET_DK, RET_DV), F32)],
        compiler_params=_params(1),
        name="retention",
    )(pret.reshape(B, T, _RET_COLS), decay, xi, zeta, gch, ng, _group_ones())
    return out.reshape(B * T, RET_WIDTH)


def _mixer_residual(x_ref, nsa_ref, ret_ref, wo_ref):
    return (x_ref[...] + _dot(nsa_ref[...].astype(BF16), wo_ref[0:NSA_WIDTH, :])
            + _dot(ret_ref[...].astype(BF16), wo_ref[NSA_WIDTH:, :]))


def _mixer_specs(tm, d, row, const):
    return [pl.BlockSpec((tm, d), row), pl.BlockSpec((tm, NSA_WIDTH), row), pl.BlockSpec((tm, RET_WIDTH), row),
            pl.BlockSpec((NSA_WIDTH + RET_WIDTH, d), const)]


def _gate_up(w_gate, w_up, fc):
    dff = w_gate.shape[-1]
    chunks = [jnp.concatenate([w_gate[..., c:c + fc], w_up[..., c:c + fc]], axis=-1) for c in range(0, dff, fc)]
    return jnp.stack(chunks, axis=-3).astype(BF16)


def _swiglu_hidden(rows, wgu_ref):
    au = _dot(rows, wgu_ref[...])
    fc = au.shape[1] // 2
    a = au[:, 0:fc]
    return (a * jax.nn.sigmoid(a) * au[:, fc:]).astype(BF16)


def _ffn_kernel(x_ref, nsa_ref, ret_ref, wo_ref, g_ref, wgu_ref, wd_ref, o_ref, h_sc):
    f = pl.program_id(1)

    @pl.when(f == 0)
    def _():
        x = _mixer_residual(x_ref, nsa_ref, ret_ref, wo_ref)
        h_sc[...] = _rms(x, g_ref[...]).astype(BF16)
        o_ref[...] = x

    o_ref[...] += _dot(_swiglu_hidden(h_sc[...], wgu_ref), wd_ref[...])


def _ffn(x2, nsa, ret, wo, g, wgu, wd, fc, tm=1024):
    n, d = x2.shape
    dff = wd.shape[0]
    return pl.pallas_call(
        _ffn_kernel,
        grid=(n // tm, dff // fc),
        in_specs=_mixer_specs(tm, d, lambda i, f: (i, 0), lambda i, f: (0, 0)) + [
            pl.BlockSpec((1, d), lambda i, f: (0, 0)),
            pl.BlockSpec((None, d, 2 * fc), lambda i, f: (f, 0, 0)),
            pl.BlockSpec((fc, d), lambda i, f: (f, 0))],
        out_specs=pl.BlockSpec((tm, d), lambda i, f: (i, 0)),
        out_shape=jax.ShapeDtypeStruct((n, d), F32),
        scratch_shapes=[pltpu.VMEM((tm, d), BF16)],
        compiler_params=_params(2),
        name="ffn_dense",
    )(x2, nsa, ret, wo, g, wgu, wd)


def _router_kernel(x_ref, nsa_ref, ret_ref, wo_ref, g_ref, r_ref, rb_ref, tri_ref,
                   x1_ref, h_ref, rank_ref, comb_ref, rankT_ref, cnt_ref):
    x = _mixer_residual(x_ref, nsa_ref, ret_ref, wo_ref)
    x1_ref[...] = x
    h = _rms(x, g_ref[...])
    h_ref[...] = h.astype(BF16)
    hh, hm, _ = _split3(h)
    rh, rm, _ = _split3(r_ref[...])
    logits = _dot(hh, rh) + (_dot(hh, rm) + _dot(hm, rh)) + rb_ref[...]
    lane = lax.broadcasted_iota(jnp.int32, logits.shape, 1).astype(F32)
    logits = jnp.where(lane < N_EXPERTS, logits, NEG)
    m1 = jnp.max(logits, axis=1, keepdims=True)
    i1 = jnp.min(jnp.where(logits == m1, lane, float(LANES)), axis=1, keepdims=True)
    l2 = jnp.where(lane == i1, NEG, logits)
    m2 = jnp.max(l2, axis=1, keepdims=True)
    i2 = jnp.min(jnp.where(l2 == m2, lane, float(LANES)), axis=1, keepdims=True)
    e2 = jnp.exp(m2 - m1)
    w1 = 1.0 / (1.0 + e2)
    w2 = e2 / (1.0 + e2)
    use1, use2 = lane == i1, lane == i2
    comb_ref[...] = jnp.where(use1, w1, 0.0) + jnp.where(use2, w2, 0.0)
    use = (use1 | use2).astype(F32)
    rank = jnp.where(use > 0, _dot(tri_ref[...], use.astype(BF16)), -1.0)
    rank_ref[...] = rank
    rankT_ref[...] = rank.T[0:N_EXPERTS, :]
    cnt_ref[...] = jnp.broadcast_to(jnp.sum(use, axis=0, keepdims=True), cnt_ref.shape).astype(jnp.int32)


def _router(x2, nsa, ret, wo, g, router, rb, tm):
    n, d = x2.shape
    nt = n // tm
    tri = jnp.asarray(np.arange(tm)[:, None] > np.arange(tm)[None, :], BF16)
    rpad = jnp.zeros((d, LANES), F32).at[:, :N_EXPERTS].set(router)
    rbpad = jnp.zeros((1, LANES), F32).at[0, :N_EXPERTS].set(rb)
    c2 = lambda i: (0, 0)
    row = lambda i: (i, 0)
    return pl.pallas_call(
        _router_kernel,
        grid=(nt,),
        in_specs=_mixer_specs(tm, d, row, c2) + [
            pl.BlockSpec((1, d), c2),
            pl.BlockSpec((d, LANES), c2),
            pl.BlockSpec((1, LANES), c2),
            pl.BlockSpec((tm, tm), c2)],
        out_specs=[pl.BlockSpec((tm, d), row),
                   pl.BlockSpec((tm, d), row),
                   pl.BlockSpec((tm, LANES), row),
                   pl.BlockSpec((tm, LANES), row),
                   pl.BlockSpec((N_EXPERTS, tm), lambda i: (0, i)),
                   pl.BlockSpec((None, 8, LANES), lambda i: (i, 0, 0))],
        out_shape=[jax.ShapeDtypeStruct((n, d), F32),
                   jax.ShapeDtypeStruct((n, d), BF16),
                   jax.ShapeDtypeStruct((n, LANES), F32),
                   jax.ShapeDtypeStruct((n, LANES), F32),
                   jax.ShapeDtypeStruct((N_EXPERTS, n), F32),
                   jax.ShapeDtypeStruct((nt, 8, LANES), jnp.int32)],
        compiler_params=_params(1),
        name="moe_router",
    )(x2, nsa, ret, wo, g, rpad, rbpad, tri)


MOE_SUB = 144
MOE_MOVE = 2 * MOE_SUB


def _moe_kernel(cnt_ref, h_ref, rankT_ref, rank_ref, comb_ref, wgu_ref, wd_ref, x_ref, o_ref, hc_sc, oacc_sc):
    t, e, f = pl.program_id(0), pl.program_id(1), pl.program_id(2)
    nf = pl.num_programs(2)
    tm = h_ref.shape[0]
    nsub = (cnt_ref[t * N_EXPERTS + e] + (MOE_SUB - 1)) // MOE_SUB
    nmove = (nsub + 1) // 2

    @pl.when((e == 0) & (f == 0))
    def _():
        o_ref[...] = x_ref[...]

    @pl.when(f == 0)
    def _():
        rank_row = rankT_ref[...]

        def gather(s, c):
            r0 = pl.multiple_of(s * MOE_MOVE, MOE_MOVE)
            rows = (lax.broadcasted_iota(jnp.int32, (MOE_MOVE, 1), 0) + r0).astype(F32)
            onehot = (rows == rank_row).astype(BF16)
            hc_sc[pl.ds(r0, MOE_MOVE), :] = _dot(onehot, h_ref[...]).astype(BF16)
            oacc_sc[pl.ds(r0, MOE_MOVE), :] = jnp.zeros((MOE_MOVE, oacc_sc.shape[1]), F32)
            return c

        lax.fori_loop(0, nmove, gather, 0)

    def expert(n_rows, s, c):
        r0 = pl.multiple_of(s * n_rows, n_rows)
        act = _swiglu_hidden(hc_sc[pl.ds(r0, n_rows), :], wgu_ref)
        oacc_sc[pl.ds(r0, n_rows), :] += _dot(act, wd_ref[...])
        return c

    lax.fori_loop(0, nsub // 2, functools.partial(expert, MOE_MOVE), 0)
    lax.fori_loop(nsub // 2 * 2, nsub, functools.partial(expert, MOE_SUB), 0)

    @pl.when(f == nf - 1)
    def _():
        is_e = lax.broadcasted_iota(jnp.int32, (1, LANES), 1) == e
        rank_col = jnp.sum(jnp.where(is_e, rank_ref[...], 0.0), axis=1, keepdims=True)
        comb_col = jnp.sum(jnp.where(is_e, comb_ref[...], 0.0), axis=1, keepdims=True)

        def scatter(s, c):
            r0 = pl.multiple_of(s * MOE_MOVE, MOE_MOVE)
            cols = (lax.broadcasted_iota(jnp.int32, (1, MOE_MOVE), 1) + r0).astype(F32)
            onehot = (rank_col == cols).astype(BF16)
            y = _dot(onehot, oacc_sc[pl.ds(r0, MOE_MOVE), :].astype(BF16))
            o_ref[...] += comb_col * y
            return c

        lax.fori_loop(0, nmove, scatter, 0)


def _moe(counts, h, rankT, rank, comb, wgu, wd, x2, tm, fc):
    n, d = x2.shape
    dff = wd.shape[1]
    rows_cap = pl.cdiv(pl.cdiv(tm, MOE_SUB), 2) * MOE_MOVE
    grid_spec = pltpu.PrefetchScalarGridSpec(
        num_scalar_prefetch=1,
        grid=(n // tm, N_EXPERTS, dff // fc),
        in_specs=[pl.BlockSpec((tm, d), lambda t, e, f, c: (t, 0)),
                  pl.BlockSpec((None, 1, tm), lambda t, e, f, c: (e, 0, t)),
                  pl.BlockSpec((tm, LANES), lambda t, e, f, c: (t, 0)),
                  pl.BlockSpec((tm, LANES), lambda t, e, f, c: (t, 0)),
                  pl.BlockSpec((None, None, d, 2 * fc), lambda t, e, f, c: (e, f, 0, 0)),
                  pl.BlockSpec((None, fc, d), lambda t, e, f, c: (e, f, 0)),
                  pl.BlockSpec((tm, d), lambda t, e, f, c: (t, 0))],
        out_specs=pl.BlockSpec((tm, d), lambda t, e, f, c: (t, 0)),
        scratch_shapes=[pltpu.VMEM((rows_cap, d), BF16), pltpu.VMEM((rows_cap, d), F32)],
    )
    return pl.pallas_call(
        _moe_kernel,
        grid_spec=grid_spec,
        out_shape=jax.ShapeDtypeStruct((n, d), F32),
        compiler_params=_params(3),
        name="moe_experts",
    )(counts, h, rankT.reshape(N_EXPERTS, 1, n), rank, comb, wgu, wd, x2)


def _permute_w_in(w):
    o = np.cumsum((0, NSA_WIDTH) + (KV_WIDTH,) * 6 + (3 * NSA_HEADS,))
    q, kc, vc, ks, vs, kw, vw, gts = (w[:, o[k]:o[k + 1]] for k in range(8))
    ret = w[:, o[8]:]
    pad = jnp.zeros((w.shape[0], LANES - 3 * NSA_HEADS), w.dtype)
    return jnp.concatenate([q, ks, kw, vs, vw, kc, vc, gts, pad, ret], axis=1).astype(BF16)


def _nsa_consts(T):
    ncp = T // CMP_STRIDE
    ns = T // SLC_LEN
    cs = np.arange(ncp) * CMP_STRIDE
    ss = np.arange(ns) * SLC_LEN
    ov = np.clip(np.minimum(cs[None, :] + CMP_LEN, ss[:, None] + SLC_LEN) - np.maximum(cs[None, :], ss[:, None]), 0, None)
    ovT = (ov.astype(np.float32) / CMP_LEN)
    ovT[:, ncp - 1] = 0.0
    h = np.arange(NSA_HEADS).reshape(NSA_KV_HEADS, NSA_GROUP) + 1
    slopes = np.exp2(-8.0 * h / NSA_HEADS).astype(np.float32)
    slopes = np.repeat(slopes, Q_BLOCK, axis=1)
    parts, rest = [], np.float64(LOG2E)
    for _ in range(3):
        part = np.float64(np.asarray(rest).astype(BF16))
        parts.append(part)
        rest = rest - part
    qaug = np.zeros((NSA_KV_HEADS, HEAD_DIM, GQ), np.float32)
    for k, part in enumerate(parts):
        qaug[:, k, :] = part * SLC_LEN * slopes
        qaug[:, 3 + k, :] = part * slopes
    kq = np.arange(Q_BLOCK)[:, None] - np.arange(Q_BLOCK)[None, :]
    causb = np.where(kq <= 0, 0.0, NEG).astype(np.float32)
    lowb = np.where(kq > 0, 0.0, NEG).astype(np.float32)
    return jnp.asarray(ovT, BF16), jnp.asarray(qaug, BF16), jnp.asarray(lowb), jnp.asarray(causb)


def _mixer(x2, B, T, norm_g, w_in, q_norm_g, k_norm_g, cmp_pos, w_cmp, ret_norm_g, w_out):
    ns = T // SLC_LEN
    kc, vc, pret, qT, ks, kw, vsT, vwT, gT = _inproj(x2, norm_g[None, :], _permute_w_in(w_in), q_norm_g[None, :],
                                                     k_norm_g[1:3], B, T)
    wk, pk = _compress_weights(w_cmp[0], cmp_pos[0])
    wv, pv = _compress_weights(w_cmp[1], cmp_pos[1])
    kcmp, vcT = _compress(kc, vc, wk, wv, pk, pv, k_norm_g[0:1], B, T)
    ovT, qaug, lowb, causb = _nsa_consts(T)
    ocmp, sel, flags = _nsa_cmp(qT, qaug, kcmp, vcT, ovT, B, T)
    lists, counts = _nsa_steps(flags[:, :, :, 0, :].reshape(-1, ns), T // Q_BLOCK)
    kpad = jnp.zeros((WIN, 2 * HEAD_DIM), BF16).at[:, HEAD_DIM:HEAD_DIM + 3].set(-2.0 ** 100)
    kw = jnp.concatenate([jnp.broadcast_to(kpad, kw.shape[:2] + kpad.shape), kw], axis=2)
    vwT = jnp.pad(vwT, ((0, 0), (0, 0), (0, 0), (WIN, 0)))
    nsa = _nsa_main(lists, counts, qT, qaug, ks, vsT, kw, vwT, sel, gT, ocmp, lowb, causb, B, T)
    ret = _retention(pret, ret_norm_g[None, :], B, T)
    return nsa.reshape(B * T, NSA_WIDTH), ret, w_out.astype(BF16)


def _moe_layer(x2, nsa, ret, wo, norm_g, router, router_b, wg, wu, wd, tm=1024):
    tm = min(tm, x2.shape[0])
    x1, h, rank, comb, rankT, cnt = _router(x2, nsa, ret, wo, norm_g[None, :], router, router_b, tm)
    counts = cnt[:, 0, :N_EXPERTS].reshape(-1)
    return _moe(counts, h, rankT, rank, comb, _gate_up(wg, wu, FF_CHUNK), wd.astype(BF16), x1, tm, FF_CHUNK)


def kernel(x, norm_mix_g, w_in, q_norm_g, k_norm_g, cmp_pos, w_cmp, ret_norm_g, w_out, norm_ffn_g,
           ffn_w_gate, ffn_w_up, ffn_w_down, moe_router, moe_router_b, moe_w_gate, moe_w_up, moe_w_down):
    B, T, D = x.shape
    depth = norm_mix_g.shape[0]
    x2 = x.reshape(B * T, D)
    for l in range(depth):
        nsa, ret, wo = _mixer(x2, B, T, norm_mix_g[l], w_in[l], q_norm_g[l], k_norm_g[l], cmp_pos[l], w_cmp[l],
                              ret_norm_g[l], w_out[l])
        j = l // 2
        if l % 2 == 0:
            x2 = _ffn(x2, nsa, ret, wo, norm_ffn_g[l][None, :], _gate_up(ffn_w_gate[j], ffn_w_up[j], FF_CHUNK),
                      ffn_w_down[j].astype(BF16), FF_CHUNK)
        else:
            x2 = _moe_layer(x2, nsa, ret, wo, norm_ffn_g[l], moe_router[j], moe_router_b[j], moe_w_gate[j],
                            moe_w_up[j], moe_w_down[j])
    return x2.reshape(B, T, D)
```

```python
import functools

import numpy as np
import jax
import jax.numpy as jnp
from jax import lax
from jax.experimental import pallas as pl
from jax.experimental.pallas import tpu as pltpu

F32 = jnp.float32
BF16 = jnp.bfloat16

HEAD_DIM = 64
NSA_HEADS = 8
NSA_KV_HEADS = 2
NSA_GROUP = NSA_HEADS // NSA_KV_HEADS
RET_HEADS = 8
RET_DK = 32
RET_DV = 64
NSA_WIDTH = NSA_HEADS * HEAD_DIM
RET_WIDTH = RET_HEADS * RET_DV
KV_WIDTH = NSA_KV_HEADS * HEAD_DIM
CMP_LEN = 32
CMP_STRIDE = 16
SLC_LEN = 64
SLC_TOPK = 16
WIN = 512
Q_BLOCK = 128
RET_CHUNK = 128
N_EXPERTS = 8
EPS = 1e-6
NEG = -1e30
BIG = 1e9
LANES = 128
GQ = NSA_GROUP * Q_BLOCK
KEY_STEP = 128
STEP_GROUP = 4
N_FORCED = 3
CMP_CHUNK = 128
CMP_TILES = 8
MAIN_TILES = 8
LOOP_TILES = 2
CMP_TAIL = CMP_CHUNK + 8
WIN_KEYS = WIN + Q_BLOCK
V_ROWS = HEAD_DIM + 16
LOG2E = 1.4426950408889634
FF_CHUNK = 1408
VMEM_LIMIT = 60 * 1024 * 1024

_C_Q = 0
_C_KV = _C_Q + NSA_WIDTH
_C_KC = _C_KV + 4 * KV_WIDTH
_C_VC = _C_KC + KV_WIDTH
_C_GT = _C_VC + KV_WIDTH
_C_RET = _C_GT + LANES
_RET_COLS = 2 * RET_HEADS * RET_DK + 2 * RET_WIDTH
_C_END = _C_RET + _RET_COLS


def _params(n_axes, vmem=VMEM_LIMIT):
    return pltpu.CompilerParams(dimension_semantics=("arbitrary",) * n_axes, vmem_limit_bytes=vmem)


def _dot(a, b):
    return jnp.dot(a, b, preferred_element_type=F32)


def _dot_nt(a, b):
    return lax.dot_general(a, b, (((1,), (1,)), ((), ())), preferred_element_type=F32)


def _rms(x, g):
    return x * lax.rsqrt(jnp.mean(x * x, axis=-1, keepdims=True) + EPS) * g


def _group_rms(x, g, ones_ref):
    w = x.shape[1]
    ones = ones_ref[0:w, 0:w]
    sq = x * x
    hi = sq.astype(BF16)
    lo = (sq - hi.astype(F32)).astype(BF16)
    ms = (_dot(hi, ones) + _dot(lo, ones)) * (1.0 / HEAD_DIM)
    return x * lax.rsqrt(ms + EPS) * g


def _group_ones():
    lane = np.arange(NSA_WIDTH) // HEAD_DIM
    return jnp.asarray(lane[:, None] == lane[None, :], BF16)


def _inproj_kernel(x_ref, g_ref, w_ref, qg_ref, kg_ref, ones_ref, kc_ref, vc_ref, ret_ref,
                   qT_ref, ks_ref, kw_ref, vsT_ref, vwT_ref, gT_ref, *, steps_per_row):
    n_tok = x_ref.shape[0]
    tiles = range(n_tok // Q_BLOCK)
    rows = [slice(u * Q_BLOCK, (u + 1) * Q_BLOCK) for u in tiles]
    xn = _rms(x_ref[...], g_ref[...]).astype(BF16)
    q = _dot(xn, w_ref[:, _C_Q:_C_KV])
    mid = _dot(xn, w_ref[:, _C_KV:_C_RET])
    kv = mid[:, 0:_C_KC - _C_KV]
    kc_ref[...] = mid[:, _C_KC - _C_KV:_C_VC - _C_KV]
    vc_ref[...] = mid[:, _C_VC - _C_KV:_C_GT - _C_KV]
    gt = mid[:, _C_GT - _C_KV:_C_RET - _C_KV]

    scale = HEAD_DIM ** -0.5 * LOG2E
    qn = _group_rms(q, qg_ref[...], ones_ref) * scale
    qt = [qn[rows[u]].T for u in tiles]
    for g in range(NSA_KV_HEADS):
        for r in range(NSA_GROUP):
            h = g * NSA_GROUP + r
            for u in tiles:
                qT_ref[g, u, :, r * Q_BLOCK:(r + 1) * Q_BLOCK] = qt[u][h * HEAD_DIM:(h + 1) * HEAD_DIM, :].astype(BF16)
    vst = [kv[rows[u], 2 * KV_WIDTH:3 * KV_WIDTH].T for u in tiles]
    vwt = [kv[rows[u], 3 * KV_WIDTH:4 * KV_WIDTH].T for u in tiles]
    gts = [jax.nn.sigmoid(gt[rows[u], :].T[0:32, :]) for u in tiles]
    pos0 = (pl.program_id(0) % steps_per_row) * n_tok
    pos = pos0 + lax.broadcasted_iota(jnp.int32, (n_tok, HEAD_DIM), 0)
    col = lax.broadcasted_iota(jnp.int32, (n_tok, HEAD_DIM), 1)
    kpos = jnp.where(col < 3, pos // SLC_LEN, jnp.where(col < 6, pos % SLC_LEN, 0)).astype(F32)
    ones_row = (lax.broadcasted_iota(jnp.int32, (V_ROWS - HEAD_DIM, Q_BLOCK), 0) == 0).astype(F32)
    ks = _group_rms(kv[:, 0:KV_WIDTH], kg_ref[0:1, :], ones_ref)
    kw = _group_rms(kv[:, KV_WIDTH:2 * KV_WIDTH], kg_ref[1:2, :], ones_ref)
    for g in range(NSA_KV_HEADS):
        sl = slice(g * HEAD_DIM, (g + 1) * HEAD_DIM)
        ks_ref[g] = jnp.concatenate([ks[:, sl], kpos], axis=1).astype(BF16)
        kw_ref[g] = jnp.concatenate([kw[:, sl], kpos], axis=1).astype(BF16)
        for u in tiles:
            vsT_ref[g, :, rows[u]] = jnp.concatenate([vst[u][sl, :], ones_row], axis=0).astype(BF16)
            vwT_ref[g, :, rows[u]] = jnp.concatenate([vwt[u][sl, :], ones_row], axis=0).astype(BF16)
    for u in tiles:
        gT_ref[u] = gts[u]
    ret_ref[...] = _dot(xn, w_ref[:, _C_RET:_C_END])


def _inproj(x2, g, w, qg, kg, B, T, tm=1024):
    n, d = x2.shape
    nq = T // Q_BLOCK
    G = NSA_KV_HEADS
    tiles = tm // Q_BLOCK
    spr = T // tm
    assert T % tm == 0
    const = lambda i: (0, 0)
    row = lambda i: (i, 0)
    return pl.pallas_call(
        functools.partial(_inproj_kernel, steps_per_row=spr),
        grid=(n // tm,),
        in_specs=[pl.BlockSpec((tm, d), row),
                  pl.BlockSpec((1, d), const),
                  pl.BlockSpec((d, _C_END), const),
                  pl.BlockSpec((1, NSA_WIDTH), const),
                  pl.BlockSpec((2, KV_WIDTH), const),
                  pl.BlockSpec((NSA_WIDTH, NSA_WIDTH), const)],
        out_specs=[pl.BlockSpec((tm, KV_WIDTH), row),
                   pl.BlockSpec((tm, KV_WIDTH), row),
                   pl.BlockSpec((tm, _RET_COLS), row),
                   pl.BlockSpec((None, G, tiles, HEAD_DIM, GQ), lambda i: (i // spr, 0, i % spr, 0, 0)),
                   pl.BlockSpec((None, G, tm, 2 * HEAD_DIM), lambda i: (i // spr, 0, i % spr, 0)),
                   pl.BlockSpec((None, G, tm, 2 * HEAD_DIM), lambda i: (i // spr, 0, i % spr, 0)),
                   pl.BlockSpec((None, G, V_ROWS, tm), lambda i: (i // spr, 0, 0, i % spr)),
                   pl.BlockSpec((None, G, V_ROWS, tm), lambda i: (i // spr, 0, 0, i % spr)),
                   pl.BlockSpec((None, tiles, 32, Q_BLOCK), lambda i: (i // spr, i % spr, 0, 0))],
        out_shape=[jax.ShapeDtypeStruct((n, KV_WIDTH), F32),
                   jax.ShapeDtypeStruct((n, KV_WIDTH), F32),
                   jax.ShapeDtypeStruct((n, _RET_COLS), F32),
                   jax.ShapeDtypeStruct((B, G, nq, HEAD_DIM, GQ), BF16),
                   jax.ShapeDtypeStruct((B, G, T, 2 * HEAD_DIM), BF16),
                   jax.ShapeDtypeStruct((B, G, T, 2 * HEAD_DIM), BF16),
                   jax.ShapeDtypeStruct((B, G, V_ROWS, T), BF16),
                   jax.ShapeDtypeStruct((B, G, V_ROWS, T), BF16),
                   jax.ShapeDtypeStruct((B, nq, 32, Q_BLOCK), F32)],
        compiler_params=_params(1),
        name="inproj",
    )(x2, g, w, jnp.tile(qg, (1, NSA_HEADS)), jnp.tile(kg, (1, NSA_KV_HEADS)), _group_ones())


def _compress_kernel(kc_ref, vc_ref, wk_ref, wv_ref, pk_ref, pv_ref, kg_ref, kcmp_ref, vcT_ref):
    ncp = kc_ref.shape[0] // CMP_STRIDE

    def comp(a_ref, w_ref, p_ref):
        lo = jnp.zeros((ncp, KV_WIDTH), F32)
        hi = jnp.zeros((ncp, KV_WIDTH), F32)
        for l in range(CMP_STRIDE):
            a = a_ref[pl.ds(l, ncp, stride=CMP_STRIDE), :]
            lo += _dot((a + p_ref[0, l:l + 1, :]).astype(BF16), w_ref[0, l])
            hi += _dot((a + p_ref[1, l:l + 1, :]).astype(BF16), w_ref[1, l])
        return lo + pltpu.roll(hi, ncp - 1, 0)

    k = comp(kc_ref, wk_ref, pk_ref)
    v = comp(vc_ref, wv_ref, pv_ref).T
    cend = lax.broadcasted_iota(jnp.int32, (ncp, HEAD_DIM), 0) * CMP_STRIDE + (CMP_LEN - 1)
    col = lax.broadcasted_iota(jnp.int32, (ncp, HEAD_DIM), 1)
    kpos = jnp.where(col < 3, cend // SLC_LEN, jnp.where(col < 6, cend % SLC_LEN, 0)).astype(F32)
    for g in range(NSA_KV_HEADS):
        sl = slice(g * HEAD_DIM, (g + 1) * HEAD_DIM)
        kcmp_ref[g] = jnp.concatenate([_rms(k[:, sl], kg_ref[...]), kpos], axis=1).astype(BF16)
        vcT_ref[g] = v[sl, :].astype(BF16)


def _compress(kc, vc, wk, wv, pk, pv, kg, B, T):
    ncp = T // CMP_STRIDE
    G = NSA_KV_HEADS
    const4 = lambda b: (0, 0, 0, 0)
    const3 = lambda b: (0, 0, 0)
    const2 = lambda b: (0, 0)
    return pl.pallas_call(
        _compress_kernel,
        grid=(B,),
        in_specs=[pl.BlockSpec((T, KV_WIDTH), lambda b: (b, 0)),
                  pl.BlockSpec((T, KV_WIDTH), lambda b: (b, 0)),
                  pl.BlockSpec((2, CMP_STRIDE, KV_WIDTH, KV_WIDTH), const4),
                  pl.BlockSpec((2, CMP_STRIDE, KV_WIDTH, KV_WIDTH), const4),
                  pl.BlockSpec((2, CMP_STRIDE, KV_WIDTH), const3),
                  pl.BlockSpec((2, CMP_STRIDE, KV_WIDTH), const3),
                  pl.BlockSpec((1, HEAD_DIM), const2)],
        out_specs=[pl.BlockSpec((None, G, ncp, 2 * HEAD_DIM), lambda b: (b, 0, 0, 0)),
                   pl.BlockSpec((None, G, HEAD_DIM, ncp), lambda b: (b, 0, 0, 0))],
        out_shape=[jax.ShapeDtypeStruct((B, G, ncp, 2 * HEAD_DIM), BF16),
                   jax.ShapeDtypeStruct((B, G, HEAD_DIM, ncp), BF16)],
        compiler_params=_params(1),
        name="nsa_compress",
    )(kc, vc, wk, wv, pk, pv, kg)


def _compress_weights(w, pos):
    G = NSA_KV_HEADS
    w4 = w.reshape(2, CMP_STRIDE, HEAD_DIM, HEAD_DIM)
    eye = jnp.eye(G, dtype=w.dtype)
    wbd = jnp.einsum('hlde,gk->hlgdke', w4, eye).reshape(2, CMP_STRIDE, KV_WIDTH, KV_WIDTH)
    p = pos.reshape(2, CMP_STRIDE, 1, HEAD_DIM)
    p = jnp.broadcast_to(p, (2, CMP_STRIDE, G, HEAD_DIM)).reshape(2, CMP_STRIDE, KV_WIDTH)
    return wbd.astype(BF16), p


def _split3(x):
    hi = x.astype(BF16)
    r = x - hi.astype(F32)
    mid = r.astype(BF16)
    lo = (r - mid.astype(F32)).astype(BF16)
    return hi, mid, lo


def _nsa_cmp_kernel(qT_ref, qaug_ref, kc_ref, vcT_ref, ovT_ref, ocmp_ref, sel_ref, flag_ref, *, n_sel):
    ncp = kc_ref.shape[0]
    ns = ovT_ref.shape[0]
    tiles = range(CMP_TILES)
    i0 = pl.program_id(2) * CMP_TILES
    lane = lax.broadcasted_iota(jnp.int32, (1, GQ), 1)
    q = [jnp.concatenate([qT_ref[u], qaug_ref[...]], axis=0) for u in tiles]
    t_row = [(i0 + u) * Q_BLOCK + (lane & (Q_BLOCK - 1)) for u in tiles]
    has_cmp = [(t_row[u] >= CMP_LEN - 1).astype(F32) for u in tiles]
    tq = [(i0 + u) * Q_BLOCK + lax.broadcasted_iota(jnp.int32, (1, Q_BLOCK), 1) for u in tiles]
    cur = [tq[u] // SLC_LEN for u in tiles]

    def prefix(rows):
        nsk = rows * CMP_STRIDE // SLC_LEN
        tail0 = max(rows - CMP_TAIL, 0)
        kc = kc_ref[0:rows, :]
        s = [_dot(kc, q[u]) for u in tiles]
        cend = (lax.broadcasted_iota(jnp.int32, (rows - tail0, 1), 0) + tail0) * CMP_STRIDE + (CMP_LEN - 1)
        tail = [jnp.where(t_row[u] >= cend, s[u][tail0:], NEG) for u in tiles]
        s = [jnp.concatenate([s[u][0:tail0], tail[u]], axis=0) if tail0 else tail[u] for u in tiles]
        m = [jnp.max(s[u], axis=0, keepdims=True) for u in tiles]
        e = [jnp.exp2(s[u] - m[u]) for u in tiles]
        p = [e[u] * (has_cmp[u] / jnp.sum(e[u], axis=0, keepdims=True)) for u in tiles]
        vc = vcT_ref[:, 0:rows]
        for u in tiles:
            ocmp_ref[u] = _dot(vc, p[u].astype(BF16))

        ps = [p[u][:, 0:Q_BLOCK] for u in tiles]
        for r in range(1, NSA_GROUP):
            ps = [ps[u] + p[u][:, r * Q_BLOCK:(r + 1) * Q_BLOCK] for u in tiles]
        ov = ovT_ref[0:nsk, 0:rows]
        split = [_split3(ps[u]) for u in tiles]
        imp = [_dot(ov, split[u][0]) + _dot(ov, split[u][1]) + _dot(ov, split[u][2]) for u in tiles]

        blk = lax.broadcasted_iota(jnp.int32, (nsk, 1), 0)
        forced = [(blk == 0) | (blk == cur[u]) | (blk == cur[u] - 1) for u in tiles]
        valid = [blk * SLC_LEN <= tq[u] for u in tiles]
        imp = [jnp.where(forced[u], -3e38, jnp.where(valid[u], imp[u], -BIG)) for u in tiles]
        blk_f = blk.astype(F32)
        sel = [forced[u].astype(F32) for u in tiles]
        for _ in range(n_sel - N_FORCED):
            mx = [jnp.max(imp[u], axis=0, keepdims=True) for u in tiles]
            idx = [jnp.min(jnp.where(imp[u] == mx[u], blk_f, float(ns)), axis=0, keepdims=True) for u in tiles]
            pick = [blk_f == idx[u] for u in tiles]
            sel = [jnp.where(pick[u], 1.0, sel[u]) for u in tiles]
            imp = [jnp.where(pick[u], -3e38, imp[u]) for u in tiles]
        ones = jnp.ones((8, Q_BLOCK), BF16)
        for u in tiles:
            sel_ref[u, 0:nsk, :] = sel[u]
            cnt = _dot_nt(ones, sel[u].astype(BF16))
            flag_ref[u, :, 0:nsk] = (cnt > 0).astype(jnp.int32)
            if nsk < ns:
                sel_ref[u, nsk:, :] = jnp.zeros((ns - nsk, Q_BLOCK), F32)
                flag_ref[u, :, nsk:] = jnp.zeros((8, ns - nsk), jnp.int32)

    n_variants = ncp // CMP_CHUNK
    last = i0 + CMP_TILES - 1
    variant = (last * (Q_BLOCK // CMP_STRIDE) + (Q_BLOCK // CMP_STRIDE - 2)) // CMP_CHUNK
    for k in range(n_variants):
        pl.when(variant == k)(functools.partial(prefix, (k + 1) * CMP_CHUNK))


def _nsa_cmp(qT, qaug, kcmp, vcT, ovT, B, T):
    G = NSA_KV_HEADS
    nq = T // Q_BLOCK
    ncp = T // CMP_STRIDE
    ns = T // SLC_LEN
    n_sel = min(SLC_TOPK, ns)
    assert ncp % CMP_CHUNK == 0 and n_sel > N_FORCED and nq % CMP_TILES == 0
    tile = lambda b, g, i: (b, g, i, 0, 0)
    return pl.pallas_call(
        functools.partial(_nsa_cmp_kernel, n_sel=n_sel),
        grid=(B, G, nq // CMP_TILES),
        in_specs=[pl.BlockSpec((None, None, CMP_TILES, HEAD_DIM, GQ), tile),
                  pl.BlockSpec((None, HEAD_DIM, GQ), lambda b, g, i: (g, 0, 0)),
                  pl.BlockSpec((None, None, ncp, 2 * HEAD_DIM), lambda b, g, i: (b, g, 0, 0)),
                  pl.BlockSpec((None, None, HEAD_DIM, ncp), lambda b, g, i: (b, g, 0, 0)),
                  pl.BlockSpec((ns, ncp), lambda b, g, i: (0, 0))],
        out_specs=[pl.BlockSpec((None, None, CMP_TILES, HEAD_DIM, GQ), tile),
                   pl.BlockSpec((None, None, CMP_TILES, ns, Q_BLOCK), tile),
                   pl.BlockSpec((None, None, CMP_TILES, 8, ns), tile)],
        out_shape=[jax.ShapeDtypeStruct((B, G, nq, HEAD_DIM, GQ), F32),
                   jax.ShapeDtypeStruct((B, G, nq, ns, Q_BLOCK), F32),
                   jax.ShapeDtypeStruct((B, G, nq, 8, ns), jnp.int32)],
        compiler_params=_params(3),
        name="nsa_cmp",
    )(qT, qaug, kcmp, vcT, ovT)


def _nsa_main_kernel(list_ref, cnt_ref, qT_ref, qaug_ref, ks_ref, vsT_ref, kw_ref, vwT_ref, sel_ref, gT_ref, ocmp_ref,
                     lowb_ref, causb_ref, out_ref, m_sc, acc_sc, win_sc):
    b, g = pl.program_id(0), pl.program_id(1)
    tiles = range(MAIN_TILES)
    i = [pl.program_id(2) * MAIN_TILES + u for u in tiles]
    tile_id = [(b * pl.num_programs(1) + g) * (pl.num_programs(2) * MAIN_TILES) + i[u] for u in tiles]
    n_steps = sel_ref.shape[1] // 2
    q = [jnp.concatenate([qT_ref[u], qaug_ref[...]], axis=0) for u in tiles]
    k0 = [pl.multiple_of(i[u] * Q_BLOCK, Q_BLOCK) for u in tiles]

    def sel_bias(u, j, valid):
        def row(r):
            picked = (sel_ref[u, pl.ds(r, 1), :] > 0.5) & valid
            return jnp.concatenate([jnp.where(picked, 0.0, NEG)] * NSA_GROUP, axis=1)
        return row(2 * j), row(2 * j + 1)

    def add_sel_bias(s, ba, bb):
        return jnp.concatenate([s[0:SLC_LEN] + ba, s[SLC_LEN:] + bb], axis=0)

    lowb = jnp.concatenate([lowb_ref[...]] * NSA_GROUP, axis=1)
    causb = jnp.concatenate([causb_ref[...]] * NSA_GROUP, axis=1)

    bias_d = [sel_bias(u, i[u], True) for u in tiles]
    sd = [_dot(ks_ref[pl.ds(k0[u], KEY_STEP), :], q[u]) for u in tiles]
    sw = [_dot(kw_ref[pl.ds(k0[u], WIN_KEYS), :], q[u]) for u in tiles]
    sd = [add_sel_bias(sd[u], *bias_d[u]) + causb for u in tiles]
    sw = [jnp.concatenate([sw[u][0:Q_BLOCK] + lowb, sw[u][Q_BLOCK:WIN], sw[u][WIN:] + causb], axis=0) for u in tiles]
    md = [jnp.max(sd[u], axis=0, keepdims=True) for u in tiles]
    mw = [jnp.max(sw[u], axis=0, keepdims=True) for u in tiles]
    accd = [_dot(vsT_ref[:, pl.ds(k0[u], KEY_STEP)], jnp.exp2((sd[u] - md[u]).astype(BF16))) for u in tiles]
    ow = [_dot(vwT_ref[:, pl.ds(k0[u], WIN_KEYS)], jnp.exp2((sw[u] - mw[u]).astype(BF16))) for u in tiles]
    for u in tiles:
        m_sc[u] = md[u]
        acc_sc[u] = accd[u]
        win_sc[u] = ow[u][0:HEAD_DIM] / ow[u][HEAD_DIM:HEAD_DIM + 1]

    def scores(u, t):
        ks, vs, biases = [], [], []
        for x in range(STEP_GROUP):
            j = list_ref[tile_id[u] * n_steps + t * STEP_GROUP + x]
            valid = j >= 0
            j = jnp.maximum(j, 0)
            kj = pl.multiple_of(j * KEY_STEP, KEY_STEP)
            ks.append(ks_ref[pl.ds(kj, KEY_STEP), :])
            vs.append(vsT_ref[:, pl.ds(kj, KEY_STEP)])
            biases.append(sel_bias(u, j, valid))
        s = _dot(jnp.concatenate(ks, axis=0), q[u])
        s = jnp.concatenate([add_sel_bias(s[x * KEY_STEP:(x + 1) * KEY_STEP], *biases[x])
                             for x in range(STEP_GROUP)], axis=0)
        return s, jnp.max(s, axis=0, keepdims=True), jnp.concatenate(vs, axis=1)

    def accumulate(u, s, smax, vcat):
        m_old = m_sc[u]
        m_new = jnp.maximum(m_old, smax)
        alpha = jnp.exp2(m_old - m_new)
        acc_sc[u] = alpha * acc_sc[u] + _dot(vcat, jnp.exp2((s - m_new).astype(BF16)))
        m_sc[u] = m_new

    def run(work, t, carry):
        staged = [(u, scores(u, t * mult + off)) for (u, mult, off) in work]
        for u, args in staged:
            accumulate(u, *args)
        return carry

    for u0 in range(0, MAIN_TILES, LOOP_TILES):
        us = range(u0, u0 + LOOP_TILES)
        n_groups = functools.reduce(
            jnp.maximum, [(cnt_ref[tile_id[u]] + (STEP_GROUP - 1)) // STEP_GROUP for u in us])
        lax.fori_loop(0, n_groups // 2, functools.partial(run, [(u, 2, off) for off in (0, 1) for u in us]), 0)
        lax.fori_loop(n_groups // 2 * 2, n_groups, functools.partial(run, [(u, 1, 0) for u in us]), 0)

    def gate(u, k):
        rows = [gT_ref[u, pl.ds(g * (NSA_GROUP * 3) + r * 3 + k, 1), :] for r in range(NSA_GROUP)]
        return jnp.concatenate(rows, axis=1)

    o_slc = [acc_sc[u, 0:HEAD_DIM, :] / acc_sc[u, HEAD_DIM:HEAD_DIM + 1, :] for u in tiles]
    o = [gate(u, 0) * ocmp_ref[u] + gate(u, 1) * o_slc[u] + gate(u, 2) * win_sc[u] for u in tiles]
    o = [jnp.concatenate([o[u], jnp.zeros_like(o[u])], axis=0) for u in tiles]
    for r in range(NSA_GROUP):
        ot = [o[u][:, r * Q_BLOCK:(r + 1) * Q_BLOCK].T[:, 0:HEAD_DIM] for u in tiles]
        for u in tiles:
            out_ref[u * Q_BLOCK:(u + 1) * Q_BLOCK, r * HEAD_DIM:(r + 1) * HEAD_DIM] = ot[u]


def _nsa_steps_kernel(flagT_ref, pairT_ref, list_ref, cnt_ref, *, nq):
    n_steps, nt = list_ref.shape
    need = _dot(pairT_ref[...], flagT_ref[...].astype(BF16)) > 0
    step = lax.broadcasted_iota(jnp.int32, (n_steps, 1), 0)
    own = lax.broadcasted_iota(jnp.int32, (1, nt), 1) % nq
    need = need & (step < own)
    need_f = need.astype(F32)
    earlier = (lax.broadcasted_iota(jnp.int32, (n_steps, n_steps), 1) < step).astype(BF16)
    slot = _dot(earlier, need_f.astype(BF16))
    total = jnp.sum(need_f, axis=0, keepdims=True)
    cnt_ref[...] = jnp.broadcast_to(total, cnt_ref.shape).astype(jnp.int32)
    step_f = step.astype(F32)
    for p in range(n_steps):
        val = jnp.sum(jnp.where(need & (slot == p), step_f, 0.0), axis=0, keepdims=True)
        list_ref[p:p + 1, :] = jnp.where(total > p, val, -1.0).astype(jnp.int32)


def _nsa_steps(flags, nq):
    nt, ns = flags.shape
    n_steps = ns // 2
    pairT = jnp.asarray(np.arange(n_steps)[:, None] == np.arange(ns)[None, :] // 2, BF16)
    lists, counts = pl.pallas_call(
        functools.partial(_nsa_steps_kernel, nq=nq),
        out_shape=[jax.ShapeDtypeStruct((n_steps, nt), jnp.int32), jax.ShapeDtypeStruct((8, nt), jnp.int32)],
        name="nsa_steps",
    )(flags.T.astype(F32), pairT)
    return lists.T.reshape(-1), counts[0]


def _nsa_main(lists, counts, qT, qaug, ks, vsT, kw, vwT, sel, gT, ocmp, lowb, causb, B, T):
    G = NSA_KV_HEADS
    nq = T // Q_BLOCK
    ns = T // SLC_LEN
    whole = lambda b, g, i, *_: (b, g, 0, 0)
    tile = lambda b, g, i, *_: (b, g, i, 0, 0)
    const = lambda b, g, i, *_: (0, 0)
    grid_spec = pltpu.PrefetchScalarGridSpec(
        num_scalar_prefetch=2,
        grid=(B, G, nq // MAIN_TILES),
        in_specs=[pl.BlockSpec((None, None, MAIN_TILES, HEAD_DIM, GQ), tile),
                  pl.BlockSpec((None, HEAD_DIM, GQ), lambda b, g, i, *_: (g, 0, 0)),
                  pl.BlockSpec((None, None, T, 2 * HEAD_DIM), whole),
                  pl.BlockSpec((None, None, V_ROWS, T), whole),
                  pl.BlockSpec((None, None, T + WIN, 2 * HEAD_DIM), whole),
                  pl.BlockSpec((None, None, V_ROWS, T + WIN), whole),
                  pl.BlockSpec((None, None, MAIN_TILES, ns, Q_BLOCK), tile),
                  pl.BlockSpec((None, MAIN_TILES, 32, Q_BLOCK), lambda b, g, i, *_: (b, i, 0, 0)),
                  pl.BlockSpec((None, None, MAIN_TILES, HEAD_DIM, GQ), tile),
                  pl.BlockSpec((Q_BLOCK, Q_BLOCK), const),
                  pl.BlockSpec((Q_BLOCK, Q_BLOCK), const)],
        out_specs=pl.BlockSpec((None, MAIN_TILES * Q_BLOCK, NSA_GROUP * HEAD_DIM), lambda b, g, i, *_: (b, i, g)),
        scratch_shapes=[pltpu.VMEM((MAIN_TILES, 1, GQ), F32), pltpu.VMEM((MAIN_TILES, V_ROWS, GQ), F32),
                        pltpu.VMEM((MAIN_TILES, HEAD_DIM, GQ), F32)],
    )
    return pl.pallas_call(
        _nsa_main_kernel,
        grid_spec=grid_spec,
        out_shape=jax.ShapeDtypeStruct((B, T, NSA_WIDTH), F32),
        compiler_params=_params(3),
        name="nsa_main",
    )(lists, counts, qT, qaug, ks, vsT, kw, vwT, sel, gT, ocmp, lowb, causb)


def _ret_kernel(p_ref, decay_ref, xi_ref, zeta_ref, gch_ref, ng_ref, ones_ref, out_ref, state_ref):
    @pl.when(pl.program_id(0) == 0)
    def _():
        state_ref[...] = jnp.zeros(state_ref.shape, F32)

    rows = range(p_ref.shape[0])
    kw = RET_HEADS * RET_DK
    p = [p_ref[b] for b in rows]
    rq = [p[b][:, 0:kw] * (RET_DK ** -0.5) for b in rows]
    rk = [p[b][:, kw:2 * kw] for b in rows]
    rkT = [rk[b].T for b in rows]
    rv = [p[b][:, 2 * kw:2 * kw + RET_WIDTH] for b in rows]
    xi = xi_ref[...]
    outs = [[] for _ in rows]
    for h in range(RET_HEADS):
        dk = slice(h * RET_DK, (h + 1) * RET_DK)
        dv = slice(h * RET_DV, (h + 1) * RET_DV)
        st = [state_ref[b, h] for b in rows]
        inner = [_dot_nt(rq[b][:, dk], rk[b][:, dk]) * decay_ref[h] for b in rows]
        o = [_dot(inner[b], rv[b][:, dv]) + _dot(rq[b][:, dk], st[b]) * xi[:, h:h + 1] for b in rows]
        for b in rows:
            state_ref[b, h] = (st[b] * gch_ref[h:h + 1, 0:1]
                               + _dot(rkT[b][dk, :] * zeta_ref[h:h + 1, :], rv[b][:, dv]))
            outs[b].append(o[b])
    normed = [_group_rms(jnp.concatenate(outs[b], axis=1), ng_ref[...], ones_ref) for b in rows]
    for b in rows:
        rg = p[b][:, 2 * kw + RET_WIDTH:2 * kw + 2 * RET_WIDTH]
        out_ref[b] = normed[b] * (rg * jax.nn.sigmoid(rg))


def _ret_consts():
    H, C = RET_HEADS, RET_CHUNK
    log_g = np.log1p(-np.exp2(-5.0 - np.arange(H, dtype=np.float64)))
    idx = np.arange(C, dtype=np.float64)
    diff = idx[:, None] - idx[None, :]
    decay = np.where(diff >= 0, np.exp(np.maximum(diff, 0.0) * log_g[:, None, None]), 0.0)
    zeta = np.exp((C - 1 - idx) * log_g[:, None])
    xi = np.exp((idx + 1) * log_g[:, None]).T
    g_chunk = np.broadcast_to(np.exp(C * log_g)[:, None], (H, LANES))
    return tuple(jnp.asarray(a, F32) for a in (decay, xi, zeta, g_chunk))


def _retention(pret, ng, B, T):
    nch = T // RET_CHUNK
    decay, xi, zeta, gch = _ret_consts()
    c2 = lambda c: (0, 0)
    out = pl.pallas_call(
        _ret_kernel,
        grid=(nch,),
        in_specs=[pl.BlockSpec((B, RET_CHUNK, _RET_COLS), lambda c: (0, c, 0)),
                  pl.BlockSpec((RET_HEADS, RET_CHUNK, RET_CHUNK), lambda c: (0, 0, 0)),
                  pl.BlockSpec((RET_CHUNK, RET_HEADS), c2),
                  pl.BlockSpec((RET_HEADS, RET_CHUNK), c2),
                  pl.BlockSpec((RET_HEADS, LANES), c2),
                  pl.BlockSpec((1, RET_WIDTH), c2),
                  pl.BlockSpec((RET_WIDTH, RET_WIDTH), c2)],
        out_specs=pl.BlockSpec((B, RET_CHUNK, RET_WIDTH), lambda c: (0, c, 0)),
        out_shape=jax.ShapeDtypeStruct((B, T, RET_WIDTH), F32),
        scratch_shapes=[pltpu.VMEM((B, RET_HEADS, RET_DK, RET_DV), F32)],
        compiler_params=_params(1),
        name="retention",
    )(pret.reshape(B, T, _RET_COLS), decay, xi, zeta, gch, ng, _group_ones())
    return out.reshape(B * T, RET_WIDTH)


def _mixer_residual(x_ref, nsa_ref, ret_ref, wo_ref):
    return (x_ref[...] + _dot(nsa_ref[...].astype(BF16), wo_ref[0:NSA_WIDTH, :])
            + _dot(ret_ref[...].astype(BF16), wo_ref[NSA_WIDTH:, :]))


def _mixer_specs(tm, d, row, const):
    return [pl.BlockSpec((tm, d), row), pl.BlockSpec((tm, NSA_WIDTH), row), pl.BlockSpec((tm, RET_WIDTH), row),
            pl.BlockSpec((NSA_WIDTH + RET_WIDTH, d), const)]


def _swiglu(rows, wg_ref, wu_ref, wd_ref):
    a = _dot(rows, wg_ref[...])
    act = (a * jax.nn.sigmoid(a) * _dot(rows, wu_ref[...])).astype(BF16)
    return _dot(act, wd_ref[...])


def _ffn_kernel(x_ref, nsa_ref, ret_ref, wo_ref, g_ref, wg_ref, wu_ref, wd_ref, o_ref, h_sc):
    f = pl.program_id(1)

    @pl.when(f == 0)
    def _():
        x = _mixer_residual(x_ref, nsa_ref, ret_ref, wo_ref)
        h_sc[...] = _rms(x, g_ref[...]).astype(BF16)
        o_ref[...] = x

    o_ref[...] += _swiglu(h_sc[...], wg_ref, wu_ref, wd_ref)


def _ffn(x2, nsa, ret, wo, g, wg, wu, wd, tm=1024):
    n, d = x2.shape
    dff = wd.shape[0]
    fc = FF_CHUNK
    return pl.pallas_call(
        _ffn_kernel,
        grid=(n // tm, dff // fc),
        in_specs=_mixer_specs(tm, d, lambda i, f: (i, 0), lambda i, f: (0, 0)) + [
            pl.BlockSpec((1, d), lambda i, f: (0, 0)),
            pl.BlockSpec((d, fc), lambda i, f: (0, f)),
            pl.BlockSpec((d, fc), lambda i, f: (0, f)),
            pl.BlockSpec((fc, d), lambda i, f: (f, 0))],
        out_specs=pl.BlockSpec((tm, d), lambda i, f: (i, 0)),
        out_shape=jax.ShapeDtypeStruct((n, d), F32),
        scratch_shapes=[pltpu.VMEM((tm, d), BF16)],
        compiler_params=_params(2),
        name="ffn_dense",
    )(x2, nsa, ret, wo, g, wg, wu, wd)


def _router_kernel(x_ref, nsa_ref, ret_ref, wo_ref, g_ref, r_ref, rb_ref, tri_ref,
                   x1_ref, h_ref, rank_ref, comb_ref, rankT_ref, cnt_ref):
    x = _mixer_residual(x_ref, nsa_ref, ret_ref, wo_ref)
    x1_ref[...] = x
    h = _rms(x, g_ref[...])
    h_ref[...] = h.astype(BF16)
    hh, hm, _ = _split3(h)
    rh, rm, _ = _split3(r_ref[...])
    logits = _dot(hh, rh) + (_dot(hh, rm) + _dot(hm, rh)) + rb_ref[...]
    lane = lax.broadcasted_iota(jnp.int32, logits.shape, 1).astype(F32)
    logits = jnp.where(lane < N_EXPERTS, logits, NEG)
    m1 = jnp.max(logits, axis=1, keepdims=True)
    i1 = jnp.min(jnp.where(logits == m1, lane, float(LANES)), axis=1, keepdims=True)
    l2 = jnp.where(lane == i1, NEG, logits)
    m2 = jnp.max(l2, axis=1, keepdims=True)
    i2 = jnp.min(jnp.where(l2 == m2, lane, float(LANES)), axis=1, keepdims=True)
    e2 = jnp.exp(m2 - m1)
    w1 = 1.0 / (1.0 + e2)
    w2 = e2 / (1.0 + e2)
    use1, use2 = lane == i1, lane == i2
    comb_ref[...] = jnp.where(use1, w1, 0.0) + jnp.where(use2, w2, 0.0)
    use = (use1 | use2).astype(F32)
    rank = jnp.where(use > 0, _dot(tri_ref[...], use.astype(BF16)), -1.0)
    rank_ref[...] = rank
    rankT_ref[...] = rank.T[0:N_EXPERTS, :]
    cnt_ref[...] = jnp.broadcast_to(jnp.sum(use, axis=0, keepdims=True), cnt_ref.shape).astype(jnp.int32)


def _router(x2, nsa, ret, wo, g, router, rb, tm):
    n, d = x2.shape
    nt = n // tm
    tri = jnp.asarray(np.arange(tm)[:, None] > np.arange(tm)[None, :], BF16)
    rpad = jnp.zeros((d, LANES), F32).at[:, :N_EXPERTS].set(router)
    rbpad = jnp.zeros((1, LANES), F32).at[0, :N_EXPERTS].set(rb)
    c2 = lambda i: (0, 0)
    row = lambda i: (i, 0)
    return pl.pallas_call(
        _router_kernel,
        grid=(nt,),
        in_specs=_mixer_specs(tm, d, row, c2) + [
            pl.BlockSpec((1, d), c2),
            pl.BlockSpec((d, LANES), c2),
            pl.BlockSpec((1, LANES), c2),
            pl.BlockSpec((tm, tm), c2)],
        out_specs=[pl.BlockSpec((tm, d), row),
                   pl.BlockSpec((tm, d), row),
                   pl.BlockSpec((tm, LANES), row),
                   pl.BlockSpec((tm, LANES), row),
                   pl.BlockSpec((N_EXPERTS, tm), lambda i: (0, i)),
                   pl.BlockSpec((None, 8, LANES), lambda i: (i, 0, 0))],
        out_shape=[jax.ShapeDtypeStruct((n, d), F32),
                   jax.ShapeDtypeStruct((n, d), BF16),
                   jax.ShapeDtypeStruct((n, LANES), F32),
                   jax.ShapeDtypeStruct((n, LANES), F32),
                   jax.ShapeDtypeStruct((N_EXPERTS, n), F32),
                   jax.ShapeDtypeStruct((nt, 8, LANES), jnp.int32)],
        compiler_params=_params(1),
        name="moe_router",
    )(x2, nsa, ret, wo, g, rpad, rbpad, tri)


MOE_SUB = 144
MOE_MOVE = 2 * MOE_SUB


def _moe_kernel(cnt_ref, h_ref, rankT_ref, rank_ref, comb_ref, wg_ref, wu_ref, wd_ref, x_ref, o_ref, hc_sc, oacc_sc):
    t, e, f = pl.program_id(0), pl.program_id(1), pl.program_id(2)
    nf = pl.num_programs(2)
    tm = h_ref.shape[0]
    nsub = (cnt_ref[t * N_EXPERTS + e] + (MOE_SUB - 1)) // MOE_SUB
    nmove = (nsub + 1) // 2

    @pl.when((e == 0) & (f == 0))
    def _():
        o_ref[...] = x_ref[...]

    @pl.when(f == 0)
    def _():
        rank_row = rankT_ref[...]

        def gather(s, c):
            r0 = pl.multiple_of(s * MOE_MOVE, MOE_MOVE)
            rows = (lax.broadcasted_iota(jnp.int32, (MOE_MOVE, 1), 0) + r0).astype(F32)
            onehot = (rows == rank_row).astype(BF16)
            hc_sc[pl.ds(r0, MOE_MOVE), :] = _dot(onehot, h_ref[...]).astype(BF16)
            oacc_sc[pl.ds(r0, MOE_MOVE), :] = jnp.zeros((MOE_MOVE, oacc_sc.shape[1]), F32)
            return c

        lax.fori_loop(0, nmove, gather, 0)

    def expert(n_rows, s, c):
        r0 = pl.multiple_of(s * n_rows, n_rows)
        oacc_sc[pl.ds(r0, n_rows), :] += _swiglu(hc_sc[pl.ds(r0, n_rows), :], wg_ref, wu_ref, wd_ref)
        return c

    lax.fori_loop(0, nsub // 2, functools.partial(expert, MOE_MOVE), 0)
    lax.fori_loop(nsub // 2 * 2, nsub, functools.partial(expert, MOE_SUB), 0)

    @pl.when(f == nf - 1)
    def _():
        is_e = lax.broadcasted_iota(jnp.int32, (1, LANES), 1) == e
        rank_col = jnp.sum(jnp.where(is_e, rank_ref[...], 0.0), axis=1, keepdims=True)
        comb_col = jnp.sum(jnp.where(is_e, comb_ref[...], 0.0), axis=1, keepdims=True)

        def scatter(s, c):
            r0 = pl.multiple_of(s * MOE_MOVE, MOE_MOVE)
            cols = (lax.broadcasted_iota(jnp.int32, (1, MOE_MOVE), 1) + r0).astype(F32)
            onehot = (rank_col == cols).astype(BF16)
            y = _dot(onehot, oacc_sc[pl.ds(r0, MOE_MOVE), :].astype(BF16))
            o_ref[...] += comb_col * y
            return c

        lax.fori_loop(0, nmove, scatter, 0)


def _moe(counts, h, rankT, rank, comb, wg, wu, wd, x2, tm):
    n, d = x2.shape
    dff = wd.shape[1]
    fc = FF_CHUNK
    rows_cap = pl.cdiv(pl.cdiv(tm, MOE_SUB), 2) * MOE_MOVE
    grid_spec = pltpu.PrefetchScalarGridSpec(
        num_scalar_prefetch=1,
        grid=(n // tm, N_EXPERTS, dff // fc),
        in_specs=[pl.BlockSpec((tm, d), lambda t, e, f, c: (t, 0)),
                  pl.BlockSpec((None, 1, tm), lambda t, e, f, c: (e, 0, t)),
                  pl.BlockSpec((tm, LANES), lambda t, e, f, c: (t, 0)),
                  pl.BlockSpec((tm, LANES), lambda t, e, f, c: (t, 0)),
                  pl.BlockSpec((None, d, fc), lambda t, e, f, c: (e, 0, f)),
                  pl.BlockSpec((None, d, fc), lambda t, e, f, c: (e, 0, f)),
                  pl.BlockSpec((None, fc, d), lambda t, e, f, c: (e, f, 0)),
                  pl.BlockSpec((tm, d), lambda t, e, f, c: (t, 0))],
        out_specs=pl.BlockSpec((tm, d), lambda t, e, f, c: (t, 0)),
        scratch_shapes=[pltpu.VMEM((rows_cap, d), BF16), pltpu.VMEM((rows_cap, d), F32)],
    )
    return pl.pallas_call(
        _moe_kernel,
        grid_spec=grid_spec,
        out_shape=jax.ShapeDtypeStruct((n, d), F32),
        compiler_params=_params(3),
        name="moe_experts",
    )(counts, h, rankT.reshape(N_EXPERTS, 1, n), rank, comb, wg, wu, wd, x2)


def _permute_w_in(w):
    o = np.cumsum((0, NSA_WIDTH) + (KV_WIDTH,) * 6 + (3 * NSA_HEADS,))
    q, kc, vc, ks, vs, kw, vw, gts = (w[:, o[k]:o[k + 1]] for k in range(8))
    ret = w[:, o[8]:]
    pad = jnp.zeros((w.shape[0], LANES - 3 * NSA_HEADS), w.dtype)
    return jnp.concatenate([q, ks, kw, vs, vw, kc, vc, gts, pad, ret], axis=1).astype(BF16)


def _nsa_consts(T):
    ncp = T // CMP_STRIDE
    ns = T // SLC_LEN
    cs = np.arange(ncp) * CMP_STRIDE
    ss = np.arange(ns) * SLC_LEN
    ov = np.clip(np.minimum(cs[None, :] + CMP_LEN, ss[:, None] + SLC_LEN) - np.maximum(cs[None, :], ss[:, None]), 0, None)
    ovT = (ov.astype(np.float32) / CMP_LEN)
    ovT[:, ncp - 1] = 0.0
    h = np.arange(NSA_HEADS).reshape(NSA_KV_HEADS, NSA_GROUP) + 1
    slopes = np.exp2(-8.0 * h / NSA_HEADS).astype(np.float32)
    slopes = np.repeat(slopes, Q_BLOCK, axis=1)
    parts, rest = [], np.float64(LOG2E)
    for _ in range(3):
        part = np.float64(np.asarray(rest).astype(BF16))
        parts.append(part)
        rest = rest - part
    qaug = np.zeros((NSA_KV_HEADS, HEAD_DIM, GQ), np.float32)
    for k, part in enumerate(parts):
        qaug[:, k, :] = part * SLC_LEN * slopes
        qaug[:, 3 + k, :] = part * slopes
    kq = np.arange(Q_BLOCK)[:, None] - np.arange(Q_BLOCK)[None, :]
    causb = np.where(kq <= 0, 0.0, NEG).astype(np.float32)
    lowb = np.where(kq > 0, 0.0, NEG).astype(np.float32)
    return jnp.asarray(ovT, BF16), jnp.asarray(qaug, BF16), jnp.asarray(lowb), jnp.asarray(causb)


def _mixer(x2, B, T, norm_g, w_in, q_norm_g, k_norm_g, cmp_pos, w_cmp, ret_norm_g, w_out):
    ns = T // SLC_LEN
    kc, vc, pret, qT, ks, kw, vsT, vwT, gT = _inproj(x2, norm_g[None, :], _permute_w_in(w_in), q_norm_g[None, :],
                                                     k_norm_g[1:3], B, T)
    wk, pk = _compress_weights(w_cmp[0], cmp_pos[0])
    wv, pv = _compress_weights(w_cmp[1], cmp_pos[1])
    kcmp, vcT = _compress(kc, vc, wk, wv, pk, pv, k_norm_g[0:1], B, T)
    ovT, qaug, lowb, causb = _nsa_consts(T)
    ocmp, sel, flags = _nsa_cmp(qT, qaug, kcmp, vcT, ovT, B, T)
    lists, counts = _nsa_steps(flags[:, :, :, 0, :].reshape(-1, ns), T // Q_BLOCK)
    kpad = jnp.zeros((WIN, 2 * HEAD_DIM), BF16).at[:, HEAD_DIM:HEAD_DIM + 3].set(-2.0 ** 100)
    kw = jnp.concatenate([jnp.broadcast_to(kpad, kw.shape[:2] + kpad.shape), kw], axis=2)
    vwT = jnp.pad(vwT, ((0, 0), (0, 0), (0, 0), (WIN, 0)))
    nsa = _nsa_main(lists, counts, qT, qaug, ks, vsT, kw, vwT, sel, gT, ocmp, lowb, causb, B, T)
    ret = _retention(pret, ret_norm_g[None, :], B, T)
    return nsa.reshape(B * T, NSA_WIDTH), ret, w_out.astype(BF16)


def _moe_layer(x2, nsa, ret, wo, norm_g, router, router_b, wg, wu, wd, tm=1024):
    tm = min(tm, x2.shape[0])
    x1, h, rank, comb, rankT, cnt = _router(x2, nsa, ret, wo, norm_g[None, :], router, router_b, tm)
    counts = cnt[:, 0, :N_EXPERTS].reshape(-1)
    return _moe(counts, h, rankT, rank, comb, wg.astype(BF16), wu.astype(BF16), wd.astype(BF16), x1, tm)


def kernel(x, norm_mix_g, w_in, q_norm_g, k_norm_g, cmp_pos, w_cmp, ret_norm_g, w_out, norm_ffn_g,
           ffn_w_gate, ffn_w_up, ffn_w_down, moe_router, moe_router_b, moe_w_gate, moe_w_up, moe_w_down):
    B, T, D = x.shape
    depth = norm_mix_g.shape[0]
    x2 = x.reshape(B * T, D)
    for l in range(depth):
        nsa, ret, wo = _mixer(x2, B, T, norm_mix_g[l], w_in[l], q_norm_g[l], k_norm_g[l], cmp_pos[l], w_cmp[l],
                              ret_norm_g[l], w_out[l])
        j = l // 2
        if l % 2 == 0:
            x2 = _ffn(x2, nsa, ret, wo, norm_ffn_g[l][None, :], ffn_w_gate[j].astype(BF16),
                      ffn_w_up[j].astype(BF16), ffn_w_down[j].astype(BF16))
        else:
            x2 = _moe_layer(x2, nsa, ret, wo, norm_ffn_g[l], moe_router[j], moe_router_b[j], moe_w_gate[j],
                            moe_w_up[j], moe_w_down[j])
    return x2.reshape(B, T, D)
```

```python
import functools

import numpy as np
import jax
import jax.numpy as jnp
from jax import lax
from jax.experimental import pallas as pl
from jax.experimental.pallas import tpu as pltpu

F32 = jnp.float32
BF16 = jnp.bfloat16

HEAD_DIM = 64
NSA_HEADS = 8
NSA_KV_HEADS = 2
NSA_GROUP = NSA_HEADS // NSA_KV_HEADS
RET_HEADS = 8
RET_DK = 32
RET_DV = 64
NSA_WIDTH = NSA_HEADS * HEAD_DIM
RET_WIDTH = RET_HEADS * RET_DV
KV_WIDTH = NSA_KV_HEADS * HEAD_DIM
CMP_LEN = 32
CMP_STRIDE = 16
SLC_LEN = 64
SLC_TOPK = 16
WIN = 512
Q_BLOCK = 128
RET_CHUNK = 128
N_EXPERTS = 8
EPS = 1e-6
NEG = -1e30
BIG = 1e9
LANES = 128
GQ = NSA_GROUP * Q_BLOCK
KEY_STEP = 128
STEP_GROUP = 4
N_FORCED = 3
CMP_CHUNK = 128
CMP_TILES = 8
MAIN_TILES = 8
LOOP_TILES = 2
CMP_TAIL = CMP_CHUNK + 8
WIN_KEYS = WIN + Q_BLOCK
V_ROWS = HEAD_DIM + 16
LOG2E = 1.4426950408889634
FF_CHUNK = 1408
VMEM_LIMIT = 60 * 1024 * 1024

_C_Q = 0
_C_KV = _C_Q + NSA_WIDTH
_C_KC = _C_KV + 4 * KV_WIDTH
_C_VC = _C_KC + KV_WIDTH
_C_GT = _C_VC + KV_WIDTH
_C_RET = _C_GT + LANES
_RET_COLS = 2 * RET_HEADS * RET_DK + 2 * RET_WIDTH
_C_END = _C_RET + _RET_COLS


def _params(n_axes, vmem=VMEM_LIMIT):
    return pltpu.CompilerParams(dimension_semantics=("arbitrary",) * n_axes, vmem_limit_bytes=vmem)


def _dot(a, b):
    return jnp.dot(a, b, preferred_element_type=F32)


def _dot_nt(a, b):
    return lax.dot_general(a, b, (((1,), (1,)), ((), ())), preferred_element_type=F32)


def _rms(x, g):
    return x * lax.rsqrt(jnp.mean(x * x, axis=-1, keepdims=True) + EPS) * g


def _group_rms(x, g, ones_ref):
    w = x.shape[1]
    ones = ones_ref[0:w, 0:w]
    sq = x * x
    hi = sq.astype(BF16)
    lo = (sq - hi.astype(F32)).astype(BF16)
    ms = (_dot(hi, ones) + _dot(lo, ones)) * (1.0 / HEAD_DIM)
    return x * lax.rsqrt(ms + EPS) * g


def _group_ones():
    lane = np.arange(NSA_WIDTH) // HEAD_DIM
    return jnp.asarray(lane[:, None] == lane[None, :], BF16)


def _inproj_kernel(x_ref, g_ref, w_ref, qg_ref, kg_ref, ones_ref, kc_ref, vc_ref, ret_ref,
                   qT_ref, ks_ref, kw_ref, vsT_ref, vwT_ref, gT_ref, *, steps_per_row):
    n_tok = x_ref.shape[0]
    tiles = range(n_tok // Q_BLOCK)
    rows = [slice(u * Q_BLOCK, (u + 1) * Q_BLOCK) for u in tiles]
    xn = _rms(x_ref[...], g_ref[...]).astype(BF16)
    q = _dot(xn, w_ref[:, _C_Q:_C_KV])
    mid = _dot(xn, w_ref[:, _C_KV:_C_RET])
    kv = mid[:, 0:_C_KC - _C_KV]
    kc_ref[...] = mid[:, _C_KC - _C_KV:_C_VC - _C_KV]
    vc_ref[...] = mid[:, _C_VC - _C_KV:_C_GT - _C_KV]
    gt = mid[:, _C_GT - _C_KV:_C_RET - _C_KV]

    scale = HEAD_DIM ** -0.5 * LOG2E
    qn = _group_rms(q, qg_ref[...], ones_ref) * scale
    qt = [qn[rows[u]].T for u in tiles]
    for g in range(NSA_KV_HEADS):
        for r in range(NSA_GROUP):
            h = g * NSA_GROUP + r
            for u in tiles:
                qT_ref[g, u, :, r * Q_BLOCK:(r + 1) * Q_BLOCK] = qt[u][h * HEAD_DIM:(h + 1) * HEAD_DIM, :].astype(BF16)
    vst = [kv[rows[u], 2 * KV_WIDTH:3 * KV_WIDTH].T for u in tiles]
    vwt = [kv[rows[u], 3 * KV_WIDTH:4 * KV_WIDTH].T for u in tiles]
    gts = [jax.nn.sigmoid(gt[rows[u], :].T[0:32, :]) for u in tiles]
    pos0 = (pl.program_id(0) % steps_per_row) * n_tok
    pos = pos0 + lax.broadcasted_iota(jnp.int32, (n_tok, HEAD_DIM), 0)
    col = lax.broadcasted_iota(jnp.int32, (n_tok, HEAD_DIM), 1)
    kpos = jnp.where(col < 3, pos // SLC_LEN, jnp.where(col < 6, pos % SLC_LEN, 0)).astype(F32)
    ones_row = (lax.broadcasted_iota(jnp.int32, (V_ROWS - HEAD_DIM, Q_BLOCK), 0) == 0).astype(F32)
    ks = _group_rms(kv[:, 0:KV_WIDTH], kg_ref[0:1, :], ones_ref)
    kw = _group_rms(kv[:, KV_WIDTH:2 * KV_WIDTH], kg_ref[1:2, :], ones_ref)
    for g in range(NSA_KV_HEADS):
        sl = slice(g * HEAD_DIM, (g + 1) * HEAD_DIM)
        ks_ref[g] = jnp.concatenate([ks[:, sl], kpos], axis=1).astype(BF16)
        kw_ref[g] = jnp.concatenate([kw[:, sl], kpos], axis=1).astype(BF16)
        for u in tiles:
            vsT_ref[g, :, rows[u]] = jnp.concatenate([vst[u][sl, :], ones_row], axis=0).astype(BF16)
            vwT_ref[g, :, rows[u]] = jnp.concatenate([vwt[u][sl, :], ones_row], axis=0).astype(BF16)
    for u in tiles:
        gT_ref[u] = gts[u]
    ret_ref[...] = _dot(xn, w_ref[:, _C_RET:_C_END])


def _inproj(x2, g, w, qg, kg, B, T, tm=1024):
    n, d = x2.shape
    nq = T // Q_BLOCK
    G = NSA_KV_HEADS
    tiles = tm // Q_BLOCK
    spr = T // tm
    assert T % tm == 0
    const = lambda i: (0, 0)
    row = lambda i: (i, 0)
    return pl.pallas_call(
        functools.partial(_inproj_kernel, steps_per_row=spr),
        grid=(n // tm,),
        in_specs=[pl.BlockSpec((tm, d), row),
                  pl.BlockSpec((1, d), const),
                  pl.BlockSpec((d, _C_END), const),
                  pl.BlockSpec((1, NSA_WIDTH), const),
                  pl.BlockSpec((2, KV_WIDTH), const),
                  pl.BlockSpec((NSA_WIDTH, NSA_WIDTH), const)],
        out_specs=[pl.BlockSpec((tm, KV_WIDTH), row),
                   pl.BlockSpec((tm, KV_WIDTH), row),
                   pl.BlockSpec((tm, _RET_COLS), row),
                   pl.BlockSpec((None, G, tiles, HEAD_DIM, GQ), lambda i: (i // spr, 0, i % spr, 0, 0)),
                   pl.BlockSpec((None, G, tm, 2 * HEAD_DIM), lambda i: (i // spr, 0, i % spr, 0)),
                   pl.BlockSpec((None, G, tm, 2 * HEAD_DIM), lambda i: (i // spr, 0, i % spr, 0)),
                   pl.BlockSpec((None, G, V_ROWS, tm), lambda i: (i // spr, 0, 0, i % spr)),
                   pl.BlockSpec((None, G, V_ROWS, tm), lambda i: (i // spr, 0, 0, i % spr)),
                   pl.BlockSpec((None, tiles, 32, Q_BLOCK), lambda i: (i // spr, i % spr, 0, 0))],
        out_shape=[jax.ShapeDtypeStruct((n, KV_WIDTH), F32),
                   jax.ShapeDtypeStruct((n, KV_WIDTH), F32),
                   jax.ShapeDtypeStruct((n, _RET_COLS), F32),
                   jax.ShapeDtypeStruct((B, G, nq, HEAD_DIM, GQ), BF16),
                   jax.ShapeDtypeStruct((B, G, T, 2 * HEAD_DIM), BF16),
                   jax.ShapeDtypeStruct((B, G, T, 2 * HEAD_DIM), BF16),
                   jax.ShapeDtypeStruct((B, G, V_ROWS, T), BF16),
                   jax.ShapeDtypeStruct((B, G, V_ROWS, T), BF16),
                   jax.ShapeDtypeStruct((B, nq, 32, Q_BLOCK), F32)],
        compiler_params=_params(1),
        name="inproj",
    )(x2, g, w, jnp.tile(qg, (1, NSA_HEADS)), jnp.tile(kg, (1, NSA_KV_HEADS)), _group_ones())


def _compress_kernel(kc_ref, vc_ref, wk_ref, wv_ref, pk_ref, pv_ref, kg_ref, kcmp_ref, vcT_ref):
    ncp = kc_ref.shape[0] // CMP_STRIDE

    def comp(a_ref, w_ref, p_ref):
        lo = jnp.zeros((ncp, KV_WIDTH), F32)
        hi = jnp.zeros((ncp, KV_WIDTH), F32)
        for l in range(CMP_STRIDE):
            a = a_ref[pl.ds(l, ncp, stride=CMP_STRIDE), :]
            lo += _dot((a + p_ref[0, l:l + 1, :]).astype(BF16), w_ref[0, l])
            hi += _dot((a + p_ref[1, l:l + 1, :]).astype(BF16), w_ref[1, l])
        return lo + pltpu.roll(hi, ncp - 1, 0)

    k = comp(kc_ref, wk_ref, pk_ref)
    v = comp(vc_ref, wv_ref, pv_ref).T
    cend = lax.broadcasted_iota(jnp.int32, (ncp, HEAD_DIM), 0) * CMP_STRIDE + (CMP_LEN - 1)
    col = lax.broadcasted_iota(jnp.int32, (ncp, HEAD_DIM), 1)
    kpos = jnp.where(col < 3, cend // SLC_LEN, jnp.where(col < 6, cend % SLC_LEN, 0)).astype(F32)
    for g in range(NSA_KV_HEADS):
        sl = slice(g * HEAD_DIM, (g + 1) * HEAD_DIM)
        kcmp_ref[g] = jnp.concatenate([_rms(k[:, sl], kg_ref[...]), kpos], axis=1).astype(BF16)
        vcT_ref[g] = v[sl, :].astype(BF16)


def _compress(kc, vc, wk, wv, pk, pv, kg, B, T):
    ncp = T // CMP_STRIDE
    G = NSA_KV_HEADS
    const4 = lambda b: (0, 0, 0, 0)
    const3 = lambda b: (0, 0, 0)
    const2 = lambda b: (0, 0)
    return pl.pallas_call(
        _compress_kernel,
        grid=(B,),
        in_specs=[pl.BlockSpec((T, KV_WIDTH), lambda b: (b, 0)),
                  pl.BlockSpec((T, KV_WIDTH), lambda b: (b, 0)),
                  pl.BlockSpec((2, CMP_STRIDE, KV_WIDTH, KV_WIDTH), const4),
                  pl.BlockSpec((2, CMP_STRIDE, KV_WIDTH, KV_WIDTH), const4),
                  pl.BlockSpec((2, CMP_STRIDE, KV_WIDTH), const3),
                  pl.BlockSpec((2, CMP_STRIDE, KV_WIDTH), const3),
                  pl.BlockSpec((1, HEAD_DIM), const2)],
        out_specs=[pl.BlockSpec((None, G, ncp, 2 * HEAD_DIM), lambda b: (b, 0, 0, 0)),
                   pl.BlockSpec((None, G, HEAD_DIM, ncp), lambda b: (b, 0, 0, 0))],
        out_shape=[jax.ShapeDtypeStruct((B, G, ncp, 2 * HEAD_DIM), BF16),
                   jax.ShapeDtypeStruct((B, G, HEAD_DIM, ncp), BF16)],
        compiler_params=_params(1),
        name="nsa_compress",
    )(kc, vc, wk, wv, pk, pv, kg)


def _compress_weights(w, pos):
    G = NSA_KV_HEADS
    w4 = w.reshape(2, CMP_STRIDE, HEAD_DIM, HEAD_DIM)
    eye = jnp.eye(G, dtype=w.dtype)
    wbd = jnp.einsum('hlde,gk->hlgdke', w4, eye).reshape(2, CMP_STRIDE, KV_WIDTH, KV_WIDTH)
    p = pos.reshape(2, CMP_STRIDE, 1, HEAD_DIM)
    p = jnp.broadcast_to(p, (2, CMP_STRIDE, G, HEAD_DIM)).reshape(2, CMP_STRIDE, KV_WIDTH)
    return wbd.astype(BF16), p


def _split3(x):
    hi = x.astype(BF16)
    r = x - hi.astype(F32)
    mid = r.astype(BF16)
    lo = (r - mid.astype(F32)).astype(BF16)
    return hi, mid, lo


def _nsa_cmp_kernel(qT_ref, qaug_ref, kc_ref, vcT_ref, ovT_ref, ocmp_ref, sel_ref, flag_ref, *, n_sel):
    ncp = kc_ref.shape[0]
    ns = ovT_ref.shape[0]
    tiles = range(CMP_TILES)
    i0 = pl.program_id(2) * CMP_TILES
    lane = lax.broadcasted_iota(jnp.int32, (1, GQ), 1)
    q = [jnp.concatenate([qT_ref[u], qaug_ref[...]], axis=0) for u in tiles]
    t_row = [(i0 + u) * Q_BLOCK + (lane & (Q_BLOCK - 1)) for u in tiles]
    has_cmp = [(t_row[u] >= CMP_LEN - 1).astype(F32) for u in tiles]
    tq = [(i0 + u) * Q_BLOCK + lax.broadcasted_iota(jnp.int32, (1, Q_BLOCK), 1) for u in tiles]
    cur = [tq[u] // SLC_LEN for u in tiles]

    def prefix(rows):
        nsk = rows * CMP_STRIDE // SLC_LEN
        tail0 = max(rows - CMP_TAIL, 0)
        kc = kc_ref[0:rows, :]
        s = [_dot(kc, q[u]) for u in tiles]
        cend = (lax.broadcasted_iota(jnp.int32, (rows - tail0, 1), 0) + tail0) * CMP_STRIDE + (CMP_LEN - 1)
        tail = [jnp.where(t_row[u] >= cend, s[u][tail0:], NEG) for u in tiles]
        s = [jnp.concatenate([s[u][0:tail0], tail[u]], axis=0) if tail0 else tail[u] for u in tiles]
        m = [jnp.max(s[u], axis=0, keepdims=True) for u in tiles]
        e = [jnp.exp2(s[u] - m[u]) for u in tiles]
        p = [e[u] * (has_cmp[u] / jnp.sum(e[u], axis=0, keepdims=True)) for u in tiles]
        vc = vcT_ref[:, 0:rows]
        for u in tiles:
            ocmp_ref[u] = _dot(vc, p[u].astype(BF16))

        ps = [p[u][:, 0:Q_BLOCK] for u in tiles]
        for r in range(1, NSA_GROUP):
            ps = [ps[u] + p[u][:, r * Q_BLOCK:(r + 1) * Q_BLOCK] for u in tiles]
        ov = ovT_ref[0:nsk, 0:rows]
        split = [_split3(ps[u]) for u in tiles]
        imp = [_dot(ov, split[u][0]) + _dot(ov, split[u][1]) + _dot(ov, split[u][2]) for u in tiles]

        blk = lax.broadcasted_iota(jnp.int32, (nsk, 1), 0)
        forced = [(blk == 0) | (blk == cur[u]) | (blk == cur[u] - 1) for u in tiles]
        valid = [blk * SLC_LEN <= tq[u] for u in tiles]
        imp = [jnp.where(forced[u], -3e38, jnp.where(valid[u], imp[u], -BIG)) for u in tiles]
        blk_f = blk.astype(F32)
        sel = [forced[u].astype(F32) for u in tiles]
        for _ in range(n_sel - N_FORCED):
            mx = [jnp.max(imp[u], axis=0, keepdims=True) for u in tiles]
            idx = [jnp.min(jnp.where(imp[u] == mx[u], blk_f, float(ns)), axis=0, keepdims=True) for u in tiles]
            pick = [blk_f == idx[u] for u in tiles]
            sel = [jnp.where(pick[u], 1.0, sel[u]) for u in tiles]
            imp = [jnp.where(pick[u], -3e38, imp[u]) for u in tiles]
        ones = jnp.ones((8, Q_BLOCK), BF16)
        for u in tiles:
            sel_ref[u, 0:nsk, :] = sel[u]
            cnt = _dot_nt(ones, sel[u].astype(BF16))
            flag_ref[u, :, 0:nsk] = (cnt > 0).astype(jnp.int32)
            if nsk < ns:
                sel_ref[u, nsk:, :] = jnp.zeros((ns - nsk, Q_BLOCK), F32)
                flag_ref[u, :, nsk:] = jnp.zeros((8, ns - nsk), jnp.int32)

    n_variants = ncp // CMP_CHUNK
    last = i0 + CMP_TILES - 1
    variant = (last * (Q_BLOCK // CMP_STRIDE) + (Q_BLOCK // CMP_STRIDE - 2)) // CMP_CHUNK
    for k in range(n_variants):
        pl.when(variant == k)(functools.partial(prefix, (k + 1) * CMP_CHUNK))


def _nsa_cmp(qT, qaug, kcmp, vcT, ovT, B, T):
    G = NSA_KV_HEADS
    nq = T // Q_BLOCK
    ncp = T // CMP_STRIDE
    ns = T // SLC_LEN
    n_sel = min(SLC_TOPK, ns)
    assert ncp % CMP_CHUNK == 0 and n_sel > N_FORCED and nq % CMP_TILES == 0
    tile = lambda b, g, i: (b, g, i, 0, 0)
    return pl.pallas_call(
        functools.partial(_nsa_cmp_kernel, n_sel=n_sel),
        grid=(B, G, nq // CMP_TILES),
        in_specs=[pl.BlockSpec((None, None, CMP_TILES, HEAD_DIM, GQ), tile),
                  pl.BlockSpec((None, HEAD_DIM, GQ), lambda b, g, i: (g, 0, 0)),
                  pl.BlockSpec((None, None, ncp, 2 * HEAD_DIM), lambda b, g, i: (b, g, 0, 0)),
                  pl.BlockSpec((None, None, HEAD_DIM, ncp), lambda b, g, i: (b, g, 0, 0)),
                  pl.BlockSpec((ns, ncp), lambda b, g, i: (0, 0))],
        out_specs=[pl.BlockSpec((None, None, CMP_TILES, HEAD_DIM, GQ), tile),
                   pl.BlockSpec((None, None, CMP_TILES, ns, Q_BLOCK), tile),
                   pl.BlockSpec((None, None, CMP_TILES, 8, ns), tile)],
        out_shape=[jax.ShapeDtypeStruct((B, G, nq, HEAD_DIM, GQ), F32),
                   jax.ShapeDtypeStruct((B, G, nq, ns, Q_BLOCK), F32),
                   jax.ShapeDtypeStruct((B, G, nq, 8, ns), jnp.int32)],
        compiler_params=_params(3),
        name="nsa_cmp",
    )(qT, qaug, kcmp, vcT, ovT)


def _nsa_main_kernel(list_ref, cnt_ref, qT_ref, qaug_ref, ks_ref, vsT_ref, kw_ref, vwT_ref, sel_ref, gT_ref, ocmp_ref,
                     lowb_ref, causb_ref, out_ref, m_sc, acc_sc, win_sc):
    b, g = pl.program_id(0), pl.program_id(1)
    tiles = range(MAIN_TILES)
    i = [pl.program_id(2) * MAIN_TILES + u for u in tiles]
    tile_id = [(b * pl.num_programs(1) + g) * (pl.num_programs(2) * MAIN_TILES) + i[u] for u in tiles]
    n_steps = sel_ref.shape[1] // 2
    q = [jnp.concatenate([qT_ref[u], qaug_ref[...]], axis=0) for u in tiles]
    k0 = [pl.multiple_of(i[u] * Q_BLOCK, Q_BLOCK) for u in tiles]

    def sel_bias(u, j, valid):
        def row(r):
            picked = (sel_ref[u, pl.ds(r, 1), :] > 0.5) & valid
            return jnp.concatenate([jnp.where(picked, 0.0, NEG)] * NSA_GROUP, axis=1)
        return row(2 * j), row(2 * j + 1)

    def add_sel_bias(s, ba, bb):
        return jnp.concatenate([s[0:SLC_LEN] + ba, s[SLC_LEN:] + bb], axis=0)

    lowb = jnp.concatenate([lowb_ref[...]] * NSA_GROUP, axis=1)
    causb = jnp.concatenate([causb_ref[...]] * NSA_GROUP, axis=1)

    bias_d = [sel_bias(u, i[u], True) for u in tiles]
    sd = [_dot(ks_ref[pl.ds(k0[u], KEY_STEP), :], q[u]) for u in tiles]
    sw = [_dot(kw_ref[pl.ds(k0[u], WIN_KEYS), :], q[u]) for u in tiles]
    sd = [add_sel_bias(sd[u], *bias_d[u]) + causb for u in tiles]
    sw = [jnp.concatenate([sw[u][0:Q_BLOCK] + lowb, sw[u][Q_BLOCK:WIN], sw[u][WIN:] + causb], axis=0) for u in tiles]
    md = [jnp.max(sd[u], axis=0, keepdims=True) for u in tiles]
    mw = [jnp.max(sw[u], axis=0, keepdims=True) for u in tiles]
    accd = [_dot(vsT_ref[:, pl.ds(k0[u], KEY_STEP)], jnp.exp2((sd[u] - md[u]).astype(BF16))) for u in tiles]
    ow = [_dot(vwT_ref[:, pl.ds(k0[u], WIN_KEYS)], jnp.exp2((sw[u] - mw[u]).astype(BF16))) for u in tiles]
    for u in tiles:
        m_sc[u] = md[u]
        acc_sc[u] = accd[u]
        win_sc[u] = ow[u][0:HEAD_DIM] / ow[u][HEAD_DIM:HEAD_DIM + 1]

    def scores(u, t):
        ks, vs, biases = [], [], []
        for x in range(STEP_GROUP):
            j = list_ref[tile_id[u] * n_steps + t * STEP_GROUP + x]
            valid = j >= 0
            j = jnp.maximum(j, 0)
            kj = pl.multiple_of(j * KEY_STEP, KEY_STEP)
            ks.append(ks_ref[pl.ds(kj, KEY_STEP), :])
            vs.append(vsT_ref[:, pl.ds(kj, KEY_STEP)])
            biases.append(sel_bias(u, j, valid))
        s = _dot(jnp.concatenate(ks, axis=0), q[u])
        s = jnp.concatenate([add_sel_bias(s[x * KEY_STEP:(x + 1) * KEY_STEP], *biases[x])
                             for x in range(STEP_GROUP)], axis=0)
        return s, jnp.max(s, axis=0, keepdims=True), jnp.concatenate(vs, axis=1)

    def accumulate(u, s, smax, vcat):
        m_old = m_sc[u]
        m_new = jnp.maximum(m_old, smax)
        alpha = jnp.exp2(m_old - m_new)
        acc_sc[u] = alpha * acc_sc[u] + _dot(vcat, jnp.exp2((s - m_new).astype(BF16)))
        m_sc[u] = m_new

    def run(work, t, carry):
        staged = [(u, scores(u, t * mult + off)) for (u, mult, off) in work]
        for u, args in staged:
            accumulate(u, *args)
        return carry

    for u0 in range(0, MAIN_TILES, LOOP_TILES):
        us = range(u0, u0 + LOOP_TILES)
        n_groups = functools.reduce(
            jnp.maximum, [(cnt_ref[tile_id[u]] + (STEP_GROUP - 1)) // STEP_GROUP for u in us])
        lax.fori_loop(0, n_groups // 2, functools.partial(run, [(u, 2, off) for off in (0, 1) for u in us]), 0)
        lax.fori_loop(n_groups // 2 * 2, n_groups, functools.partial(run, [(u, 1, 0) for u in us]), 0)

    def gate(u, k):
        rows = [gT_ref[u, pl.ds(g * (NSA_GROUP * 3) + r * 3 + k, 1), :] for r in range(NSA_GROUP)]
        return jnp.concatenate(rows, axis=1)

    o_slc = [acc_sc[u, 0:HEAD_DIM, :] / acc_sc[u, HEAD_DIM:HEAD_DIM + 1, :] for u in tiles]
    o = [gate(u, 0) * ocmp_ref[u] + gate(u, 1) * o_slc[u] + gate(u, 2) * win_sc[u] for u in tiles]
    o = [jnp.concatenate([o[u], jnp.zeros_like(o[u])], axis=0) for u in tiles]
    for r in range(NSA_GROUP):
        ot = [o[u][:, r * Q_BLOCK:(r + 1) * Q_BLOCK].T[:, 0:HEAD_DIM] for u in tiles]
        for u in tiles:
            out_ref[u * Q_BLOCK:(u + 1) * Q_BLOCK, r * HEAD_DIM:(r + 1) * HEAD_DIM] = ot[u]


def _nsa_steps_kernel(flagT_ref, pairT_ref, list_ref, cnt_ref, *, nq):
    n_steps, nt = list_ref.shape
    need = _dot(pairT_ref[...], flagT_ref[...].astype(BF16)) > 0
    step = lax.broadcasted_iota(jnp.int32, (n_steps, 1), 0)
    own = lax.broadcasted_iota(jnp.int32, (1, nt), 1) % nq
    need = need & (step < own)
    need_f = need.astype(F32)
    earlier = (lax.broadcasted_iota(jnp.int32, (n_steps, n_steps), 1) < step).astype(BF16)
    slot = _dot(earlier, need_f.astype(BF16))
    total = jnp.sum(need_f, axis=0, keepdims=True)
    cnt_ref[...] = jnp.broadcast_to(total, cnt_ref.shape).astype(jnp.int32)
    step_f = step.astype(F32)
    for p in range(n_steps):
        val = jnp.sum(jnp.where(need & (slot == p), step_f, 0.0), axis=0, keepdims=True)
        list_ref[p:p + 1, :] = jnp.where(total > p, val, -1.0).astype(jnp.int32)


def _nsa_steps(flags, nq):
    nt, ns = flags.shape
    n_steps = ns // 2
    pairT = jnp.asarray(np.arange(n_steps)[:, None] == np.arange(ns)[None, :] // 2, BF16)
    lists, counts = pl.pallas_call(
        functools.partial(_nsa_steps_kernel, nq=nq),
        out_shape=[jax.ShapeDtypeStruct((n_steps, nt), jnp.int32), jax.ShapeDtypeStruct((8, nt), jnp.int32)],
        name="nsa_steps",
    )(flags.T.astype(F32), pairT)
    return lists.T.reshape(-1), counts[0]


def _nsa_main(lists, counts, qT, qaug, ks, vsT, kw, vwT, sel, gT, ocmp, lowb, causb, B, T):
    G = NSA_KV_HEADS
    nq = T // Q_BLOCK
    ns = T // SLC_LEN
    whole = lambda b, g, i, *_: (b, g, 0, 0)
    tile = lambda b, g, i, *_: (b, g, i, 0, 0)
    const = lambda b, g, i, *_: (0, 0)
    grid_spec = pltpu.PrefetchScalarGridSpec(
        num_scalar_prefetch=2,
        grid=(B, G, nq // MAIN_TILES),
        in_specs=[pl.BlockSpec((None, None, MAIN_TILES, HEAD_DIM, GQ), tile),
                  pl.BlockSpec((None, HEAD_DIM, GQ), lambda b, g, i, *_: (g, 0, 0)),
                  pl.BlockSpec((None, None, T, 2 * HEAD_DIM), whole),
                  pl.BlockSpec((None, None, V_ROWS, T), whole),
                  pl.BlockSpec((None, None, T + WIN, 2 * HEAD_DIM), whole),
                  pl.BlockSpec((None, None, V_ROWS, T + WIN), whole),
                  pl.BlockSpec((None, None, MAIN_TILES, ns, Q_BLOCK), tile),
                  pl.BlockSpec((None, MAIN_TILES, 32, Q_BLOCK), lambda b, g, i, *_: (b, i, 0, 0)),
                  pl.BlockSpec((None, None, MAIN_TILES, HEAD_DIM, GQ), tile),
                  pl.BlockSpec((Q_BLOCK, Q_BLOCK), const),
                  pl.BlockSpec((Q_BLOCK, Q_BLOCK), const)],
        out_specs=pl.BlockSpec((None, MAIN_TILES * Q_BLOCK, NSA_GROUP * HEAD_DIM), lambda b, g, i, *_: (b, i, g)),
        scratch_shapes=[pltpu.VMEM((MAIN_TILES, 1, GQ), F32), pltpu.VMEM((MAIN_TILES, V_ROWS, GQ), F32),
                        pltpu.VMEM((MAIN_TILES, HEAD_DIM, GQ), F32)],
    )
    return pl.pallas_call(
        _nsa_main_kernel,
        grid_spec=grid_spec,
        out_shape=jax.ShapeDtypeStruct((B, T, NSA_WIDTH), F32),
        compiler_params=_params(3),
        name="nsa_main",
    )(lists, counts, qT, qaug, ks, vsT, kw, vwT, sel, gT, ocmp, lowb, causb)


def _ret_kernel(p_ref, decay_ref, xi_ref, zeta_ref, gch_ref, ng_ref, ones_ref, out_ref, state_ref):
    @pl.when(pl.program_id(0) == 0)
    def _():
        state_ref[...] = jnp.zeros(state_ref.shape, F32)

    rows = range(p_ref.shape[0])
    kw = RET_HEADS * RET_DK
    p = [p_ref[b] for b in rows]
    rq = [p[b][:, 0:kw] * (RET_DK ** -0.5) for b in rows]
    rk = [p[b][:, kw:2 * kw] for b in rows]
    rkT = [rk[b].T for b in rows]
    rv = [p[b][:, 2 * kw:2 * kw + RET_WIDTH] for b in rows]
    xi = xi_ref[...]
    outs = [[] for _ in rows]
    for h in range(RET_HEADS):
        dk = slice(h * RET_DK, (h + 1) * RET_DK)
        dv = slice(h * RET_DV, (h + 1) * RET_DV)
        st = [state_ref[b, h] for b in rows]
        inner = [_dot_nt(rq[b][:, dk], rk[b][:, dk]) * decay_ref[h] for b in rows]
        o = [_dot(inner[b], rv[b][:, dv]) + _dot(rq[b][:, dk], st[b]) * xi[:, h:h + 1] for b in rows]
        for b in rows:
            state_ref[b, h] = (st[b] * gch_ref[h:h + 1, 0:1]
                               + _dot(rkT[b][dk, :] * zeta_ref[h:h + 1, :], rv[b][:, dv]))
            outs[b].append(o[b])
    normed = [_group_rms(jnp.concatenate(outs[b], axis=1), ng_ref[...], ones_ref) for b in rows]
    for b in rows:
        rg = p[b][:, 2 * kw + RET_WIDTH:2 * kw + 2 * RET_WIDTH]
        out_ref[b] = normed[b] * (rg * jax.nn.sigmoid(rg))


def _ret_consts():
    H, C = RET_HEADS, RET_CHUNK
    log_g = np.log1p(-np.exp2(-5.0 - np.arange(H, dtype=np.float64)))
    idx = np.arange(C, dtype=np.float64)
    diff = idx[:, None] - idx[None, :]
    decay = np.where(diff >= 0, np.exp(np.maximum(diff, 0.0) * log_g[:, None, None]), 0.0)
    zeta = np.exp((C - 1 - idx) * log_g[:, None])
    xi = np.exp((idx + 1) * log_g[:, None]).T
    g_chunk = np.broadcast_to(np.exp(C * log_g)[:, None], (H, LANES))
    return tuple(jnp.asarray(a, F32) for a in (decay, xi, zeta, g_chunk))


def _retention(pret, ng, B, T):
    nch = T // RET_CHUNK
    decay, xi, zeta, gch = _ret_consts()
    c2 = lambda c: (0, 0)
    out = pl.pallas_call(
        _ret_kernel,
        grid=(nch,),
        in_specs=[pl.BlockSpec((B, RET_CHUNK, _RET_COLS), lambda c: (0, c, 0)),
                  pl.BlockSpec((RET_HEADS, RET_CHUNK, RET_CHUNK), lambda c: (0, 0, 0)),
                  pl.BlockSpec((RET_CHUNK, RET_HEADS), c2),
                  pl.BlockSpec((RET_HEADS, RET_CHUNK), c2),
                  pl.BlockSpec((RET_HEADS, LANES), c2),
                  pl.BlockSpec((1, RET_WIDTH), c2),
                  pl.BlockSpec((RET_WIDTH, RET_WIDTH), c2)],
        out_specs=pl.BlockSpec((B, RET_CHUNK, RET_WIDTH), lambda c: (0, c, 0)),
        out_shape=jax.ShapeDtypeStruct((B, T, RET_WIDTH), F32),
        scratch_shapes=[pltpu.VMEM((B, RET_HEADS, RET_DK, RET_DV), F32)],
        compiler_params=_params(1),
        name="retention",
    )(pret.reshape(B, T, _RET_COLS), decay, xi, zeta, gch, ng, _group_ones())
    return out.reshape(B * T, RET_WIDTH)


def _mixer_residual(x_ref, nsa_ref, ret_ref, wo_ref):
    return (x_ref[...] + _dot(nsa_ref[...].astype(BF16), wo_ref[0:NSA_WIDTH, :])
            + _dot(ret_ref[...].astype(BF16), wo_ref[NSA_WIDTH:, :]))


def _mixer_specs(tm, d, row, const):
    return [pl.BlockSpec((tm, d), row), pl.BlockSpec((tm, NSA_WIDTH), row), pl.BlockSpec((tm, RET_WIDTH), row),
            pl.BlockSpec((NSA_WIDTH + RET_WIDTH, d), const)]


def _pack_kernel(wg_ref, wu_ref, o_ref):
    fc = wg_ref.shape[1]
    o_ref[:, 0:fc] = wg_ref[...].astype(BF16)
    o_ref[:, fc:] = wu_ref[...].astype(BF16)


def _pack_gate_up(w_gate, w_up):
    E, d, dff = w_gate.shape
    fc = FF_CHUNK
    spec = pl.BlockSpec((None, d, fc), lambda e, f: (e, 0, f))
    return pl.pallas_call(
        _pack_kernel,
        grid=(E, dff // fc),
        in_specs=[spec, spec],
        out_specs=pl.BlockSpec((None, None, d, 2 * fc), lambda e, f: (e, f, 0, 0)),
        out_shape=jax.ShapeDtypeStruct((E, dff // fc, d, 2 * fc), BF16),
        compiler_params=_params(2),
        name="pack_gate_up",
    )(w_gate, w_up)


def _swiglu(rows, wgu_ref, wd_ref):
    au = _dot(rows, wgu_ref[...])
    fc = au.shape[1] // 2
    a = au[:, 0:fc]
    return _dot((a * jax.nn.sigmoid(a) * au[:, fc:]).astype(BF16), wd_ref[...])


def _ffn_kernel(x_ref, nsa_ref, ret_ref, wo_ref, g_ref, wgu_ref, wd_ref, o_ref, h_sc):
    f = pl.program_id(1)

    @pl.when(f == 0)
    def _():
        x = _mixer_residual(x_ref, nsa_ref, ret_ref, wo_ref)
        h_sc[...] = _rms(x, g_ref[...]).astype(BF16)
        o_ref[...] = x

    o_ref[...] += _swiglu(h_sc[...], wgu_ref, wd_ref)


def _ffn(x2, nsa, ret, wo, g, wgu, wd, tm=1024):
    n, d = x2.shape
    dff = wd.shape[0]
    fc = FF_CHUNK
    return pl.pallas_call(
        _ffn_kernel,
        grid=(n // tm, dff // fc),
        in_specs=_mixer_specs(tm, d, lambda i, f: (i, 0), lambda i, f: (0, 0)) + [
            pl.BlockSpec((1, d), lambda i, f: (0, 0)),
            pl.BlockSpec((None, None, d, 2 * fc), lambda i, f: (0, f, 0, 0)),
            pl.BlockSpec((fc, d), lambda i, f: (f, 0))],
        out_specs=pl.BlockSpec((tm, d), lambda i, f: (i, 0)),
        out_shape=jax.ShapeDtypeStruct((n, d), F32),
        scratch_shapes=[pltpu.VMEM((tm, d), BF16)],
        compiler_params=_params(2),
        name="ffn_dense",
    )(x2, nsa, ret, wo, g, wgu, wd)


def _router_kernel(x_ref, nsa_ref, ret_ref, wo_ref, g_ref, r_ref, rb_ref, tri_ref,
                   x1_ref, h_ref, rank_ref, comb_ref, rankT_ref, cnt_ref):
    x = _mixer_residual(x_ref, nsa_ref, ret_ref, wo_ref)
    x1_ref[...] = x
    h = _rms(x, g_ref[...])
    h_ref[...] = h.astype(BF16)
    hh, hm, _ = _split3(h)
    rh, rm, _ = _split3(r_ref[...])
    logits = _dot(hh, rh) + (_dot(hh, rm) + _dot(hm, rh)) + rb_ref[...]
    lane = lax.broadcasted_iota(jnp.int32, logits.shape, 1).astype(F32)
    logits = jnp.where(lane < N_EXPERTS, logits, NEG)
    m1 = jnp.max(logits, axis=1, keepdims=True)
    i1 = jnp.min(jnp.where(logits == m1, lane, float(LANES)), axis=1, keepdims=True)
    l2 = jnp.where(lane == i1, NEG, logits)
    m2 = jnp.max(l2, axis=1, keepdims=True)
    i2 = jnp.min(jnp.where(l2 == m2, lane, float(LANES)), axis=1, keepdims=True)
    e2 = jnp.exp(m2 - m1)
    w1 = 1.0 / (1.0 + e2)
    w2 = e2 / (1.0 + e2)
    use1, use2 = lane == i1, lane == i2
    comb_ref[...] = jnp.where(use1, w1, 0.0) + jnp.where(use2, w2, 0.0)
    use = (use1 | use2).astype(F32)
    rank = jnp.where(use > 0, _dot(tri_ref[...], use.astype(BF16)), -1.0)
    rank_ref[...] = rank
    rankT_ref[...] = rank.T[0:N_EXPERTS, :]
    cnt_ref[...] = jnp.broadcast_to(jnp.sum(use, axis=0, keepdims=True), cnt_ref.shape).astype(jnp.int32)


def _router(x2, nsa, ret, wo, g, router, rb, tm):
    n, d = x2.shape
    nt = n // tm
    tri = jnp.asarray(np.arange(tm)[:, None] > np.arange(tm)[None, :], BF16)
    rpad = jnp.zeros((d, LANES), F32).at[:, :N_EXPERTS].set(router)
    rbpad = jnp.zeros((1, LANES), F32).at[0, :N_EXPERTS].set(rb)
    c2 = lambda i: (0, 0)
    row = lambda i: (i, 0)
    return pl.pallas_call(
        _router_kernel,
        grid=(nt,),
        in_specs=_mixer_specs(tm, d, row, c2) + [
            pl.BlockSpec((1, d), c2),
            pl.BlockSpec((d, LANES), c2),
            pl.BlockSpec((1, LANES), c2),
            pl.BlockSpec((tm, tm), c2)],
        out_specs=[pl.BlockSpec((tm, d), row),
                   pl.BlockSpec((tm, d), row),
                   pl.BlockSpec((tm, LANES), row),
                   pl.BlockSpec((tm, LANES), row),
                   pl.BlockSpec((N_EXPERTS, tm), lambda i: (0, i)),
                   pl.BlockSpec((None, 8, LANES), lambda i: (i, 0, 0))],
        out_shape=[jax.ShapeDtypeStruct((n, d), F32),
                   jax.ShapeDtypeStruct((n, d), BF16),
                   jax.ShapeDtypeStruct((n, LANES), F32),
                   jax.ShapeDtypeStruct((n, LANES), F32),
                   jax.ShapeDtypeStruct((N_EXPERTS, n), F32),
                   jax.ShapeDtypeStruct((nt, 8, LANES), jnp.int32)],
        compiler_params=_params(1),
        name="moe_router",
    )(x2, nsa, ret, wo, g, rpad, rbpad, tri)


MOE_SUB = 144
MOE_MOVE = 2 * MOE_SUB


def _moe_kernel(cnt_ref, h_ref, rankT_ref, rank_ref, comb_ref, wgu_ref, wd_ref, x_ref, o_ref, hc_sc, oacc_sc):
    t, e, f = pl.program_id(0), pl.program_id(1), pl.program_id(2)
    nf = pl.num_programs(2)
    tm = h_ref.shape[0]
    nsub = (cnt_ref[t * N_EXPERTS + e] + (MOE_SUB - 1)) // MOE_SUB
    nmove = (nsub + 1) // 2

    @pl.when((e == 0) & (f == 0))
    def _():
        o_ref[...] = x_ref[...]

    @pl.when(f == 0)
    def _():
        rank_row = rankT_ref[...]

        def gather(s, c):
            r0 = pl.multiple_of(s * MOE_MOVE, MOE_MOVE)
            rows = (lax.broadcasted_iota(jnp.int32, (MOE_MOVE, 1), 0) + r0).astype(F32)
            onehot = (rows == rank_row).astype(BF16)
            hc_sc[pl.ds(r0, MOE_MOVE), :] = _dot(onehot, h_ref[...]).astype(BF16)
            oacc_sc[pl.ds(r0, MOE_MOVE), :] = jnp.zeros((MOE_MOVE, oacc_sc.shape[1]), F32)
            return c

        lax.fori_loop(0, nmove, gather, 0)

    def expert(n_rows, s, c):
        r0 = pl.multiple_of(s * n_rows, n_rows)
        oacc_sc[pl.ds(r0, n_rows), :] += _swiglu(hc_sc[pl.ds(r0, n_rows), :], wgu_ref, wd_ref)
        return c

    lax.fori_loop(0, nsub // 2, functools.partial(expert, MOE_MOVE), 0)
    lax.fori_loop(nsub // 2 * 2, nsub, functools.partial(expert, MOE_SUB), 0)

    @pl.when(f == nf - 1)
    def _():
        is_e = lax.broadcasted_iota(jnp.int32, (1, LANES), 1) == e
        rank_col = jnp.sum(jnp.where(is_e, rank_ref[...], 0.0), axis=1, keepdims=True)
        comb_col = jnp.sum(jnp.where(is_e, comb_ref[...], 0.0), axis=1, keepdims=True)

        def scatter(s, c):
            r0 = pl.multiple_of(s * MOE_MOVE, MOE_MOVE)
            cols = (lax.broadcasted_iota(jnp.int32, (1, MOE_MOVE), 1) + r0).astype(F32)
            onehot = (rank_col == cols).astype(BF16)
            y = _dot(onehot, oacc_sc[pl.ds(r0, MOE_MOVE), :].astype(BF16))
            o_ref[...] += comb_col * y
            return c

        lax.fori_loop(0, nmove, scatter, 0)


def _moe(counts, h, rankT, rank, comb, wgu, wd, x2, tm):
    n, d = x2.shape
    dff = wd.shape[1]
    fc = FF_CHUNK
    rows_cap = pl.cdiv(pl.cdiv(tm, MOE_SUB), 2) * MOE_MOVE
    grid_spec = pltpu.PrefetchScalarGridSpec(
        num_scalar_prefetch=1,
        grid=(n // tm, N_EXPERTS, dff // fc),
        in_specs=[pl.BlockSpec((tm, d), lambda t, e, f, c: (t, 0)),
                  pl.BlockSpec((None, 1, tm), lambda t, e, f, c: (e, 0, t)),
                  pl.BlockSpec((tm, LANES), lambda t, e, f, c: (t, 0)),
                  pl.BlockSpec((tm, LANES), lambda t, e, f, c: (t, 0)),
                  pl.BlockSpec((None, None, d, 2 * fc), lambda t, e, f, c: (e, f, 0, 0)),
                  pl.BlockSpec((None, fc, d), lambda t, e, f, c: (e, f, 0)),
                  pl.BlockSpec((tm, d), lambda t, e, f, c: (t, 0))],
        out_specs=pl.BlockSpec((tm, d), lambda t, e, f, c: (t, 0)),
        scratch_shapes=[pltpu.VMEM((rows_cap, d), BF16), pltpu.VMEM((rows_cap, d), F32)],
    )
    return pl.pallas_call(
        _moe_kernel,
        grid_spec=grid_spec,
        out_shape=jax.ShapeDtypeStruct((n, d), F32),
        compiler_params=_params(3),
        name="moe_experts",
    )(counts, h, rankT.reshape(N_EXPERTS, 1, n), rank, comb, wgu, wd, x2)


def _permute_w_in(w):
    o = np.cumsum((0, NSA_WIDTH) + (KV_WIDTH,) * 6 + (3 * NSA_HEADS,))
    q, kc, vc, ks, vs, kw, vw, gts = (w[:, o[k]:o[k + 1]] for k in range(8))
    ret = w[:, o[8]:]
    pad = jnp.zeros((w.shape[0], LANES - 3 * NSA_HEADS), w.dtype)
    return jnp.concatenate([q, ks, kw, vs, vw, kc, vc, gts, pad, ret], axis=1).astype(BF16)


def _nsa_consts(T):
    ncp = T // CMP_STRIDE
    ns = T // SLC_LEN
    cs = np.arange(ncp) * CMP_STRIDE
    ss = np.arange(ns) * SLC_LEN
    ov = np.clip(np.minimum(cs[None, :] + CMP_LEN, ss[:, None] + SLC_LEN) - np.maximum(cs[None, :], ss[:, None]), 0, None)
    ovT = (ov.astype(np.float32) / CMP_LEN)
    ovT[:, ncp - 1] = 0.0
    h = np.arange(NSA_HEADS).reshape(NSA_KV_HEADS, NSA_GROUP) + 1
    slopes = np.exp2(-8.0 * h / NSA_HEADS).astype(np.float32)
    slopes = np.repeat(slopes, Q_BLOCK, axis=1)
    parts, rest = [], np.float64(LOG2E)
    for _ in range(3):
        part = np.float64(np.asarray(rest).astype(BF16))
        parts.append(part)
        rest = rest - part
    qaug = np.zeros((NSA_KV_HEADS, HEAD_DIM, GQ), np.float32)
    for k, part in enumerate(parts):
        qaug[:, k, :] = part * SLC_LEN * slopes
        qaug[:, 3 + k, :] = part * slopes
    kq = np.arange(Q_BLOCK)[:, None] - np.arange(Q_BLOCK)[None, :]
    causb = np.where(kq <= 0, 0.0, NEG).astype(np.float32)
    lowb = np.where(kq > 0, 0.0, NEG).astype(np.float32)
    return jnp.asarray(ovT, BF16), jnp.asarray(qaug, BF16), jnp.asarray(lowb), jnp.asarray(causb)


def _mixer(x2, B, T, norm_g, w_in, q_norm_g, k_norm_g, cmp_pos, w_cmp, ret_norm_g, w_out):
    ns = T // SLC_LEN
    kc, vc, pret, qT, ks, kw, vsT, vwT, gT = _inproj(x2, norm_g[None, :], _permute_w_in(w_in), q_norm_g[None, :],
                                                     k_norm_g[1:3], B, T)
    wk, pk = _compress_weights(w_cmp[0], cmp_pos[0])
    wv, pv = _compress_weights(w_cmp[1], cmp_pos[1])
    kcmp, vcT = _compress(kc, vc, wk, wv, pk, pv, k_norm_g[0:1], B, T)
    ovT, qaug, lowb, causb = _nsa_consts(T)
    ocmp, sel, flags = _nsa_cmp(qT, qaug, kcmp, vcT, ovT, B, T)
    lists, counts = _nsa_steps(flags[:, :, :, 0, :].reshape(-1, ns), T // Q_BLOCK)
    kpad = jnp.zeros((WIN, 2 * HEAD_DIM), BF16).at[:, HEAD_DIM:HEAD_DIM + 3].set(-2.0 ** 100)
    kw = jnp.concatenate([jnp.broadcast_to(kpad, kw.shape[:2] + kpad.shape), kw], axis=2)
    vwT = jnp.pad(vwT, ((0, 0), (0, 0), (0, 0), (WIN, 0)))
    nsa = _nsa_main(lists, counts, qT, qaug, ks, vsT, kw, vwT, sel, gT, ocmp, lowb, causb, B, T)
    ret = _retention(pret, ret_norm_g[None, :], B, T)
    return nsa.reshape(B * T, NSA_WIDTH), ret, w_out.astype(BF16)


def _moe_layer(x2, nsa, ret, wo, norm_g, router, router_b, wg, wu, wd, tm=1024):
    tm = min(tm, x2.shape[0])
    x1, h, rank, comb, rankT, cnt = _router(x2, nsa, ret, wo, norm_g[None, :], router, router_b, tm)
    counts = cnt[:, 0, :N_EXPERTS].reshape(-1)
    return _moe(counts, h, rankT, rank, comb, _pack_gate_up(wg, wu), wd.astype(BF16), x1, tm)


def kernel(x, norm_mix_g, w_in, q_norm_g, k_norm_g, cmp_pos, w_cmp, ret_norm_g, w_out, norm_ffn_g,
           ffn_w_gate, ffn_w_up, ffn_w_down, moe_router, moe_router_b, moe_w_gate, moe_w_up, moe_w_down):
    B, T, D = x.shape
    depth = norm_mix_g.shape[0]
    x2 = x.reshape(B * T, D)
    for l in range(depth):
        nsa, ret, wo = _mixer(x2, B, T, norm_mix_g[l], w_in[l], q_norm_g[l], k_norm_g[l], cmp_pos[l], w_cmp[l],
                              ret_norm_g[l], w_out[l])
        j = l // 2
        if l % 2 == 0:
            x2 = _ffn(x2, nsa, ret, wo, norm_ffn_g[l][None, :],
                      _pack_gate_up(ffn_w_gate[j:j + 1], ffn_w_up[j:j + 1]), ffn_w_down[j].astype(BF16))
        else:
            x2 = _moe_layer(x2, nsa, ret, wo, norm_ffn_g[l], moe_router[j], moe_router_b[j], moe_w_gate[j],
                            moe_w_up[j], moe_w_down[j])
    return x2.reshape(B, T, D)
```

```python
import functools

import numpy as np
import jax
import jax.numpy as jnp
from jax import lax
from jax.experimental import pallas as pl
from jax.experimental.pallas import tpu as pltpu

F32 = jnp.float32
BF16 = jnp.bfloat16

HEAD_DIM = 64
NSA_HEADS = 8
NSA_KV_HEADS = 2
NSA_GROUP = NSA_HEADS // NSA_KV_HEADS
RET_HEADS = 8
RET_DK = 32
RET_DV = 64
NSA_WIDTH = NSA_HEADS * HEAD_DIM
RET_WIDTH = RET_HEADS * RET_DV
KV_WIDTH = NSA_KV_HEADS * HEAD_DIM
CMP_LEN = 32
CMP_STRIDE = 16
SLC_LEN = 64
SLC_TOPK = 16
WIN = 512
Q_BLOCK = 128
RET_CHUNK = 128
N_EXPERTS = 8
EPS = 1e-6
NEG = -1e30
BIG = 1e9
LANES = 128
GQ = NSA_GROUP * Q_BLOCK
KEY_STEP = 128
STEP_GROUP = 4
N_FORCED = 3
CMP_CHUNK = 128
CMP_TILES = 8
MAIN_TILES = 8
LOOP_TILES = 2
CMP_TAIL = CMP_CHUNK + 8
WIN_KEYS = WIN + Q_BLOCK
V_ROWS = HEAD_DIM + 16
LOG2E = 1.4426950408889634
FF_CHUNK = 1408
VMEM_LIMIT = 60 * 1024 * 1024

_C_Q = 0
_C_KV = _C_Q + NSA_WIDTH
_C_KC = _C_KV + 4 * KV_WIDTH
_C_VC = _C_KC + KV_WIDTH
_C_GT = _C_VC + KV_WIDTH
_C_RET = _C_GT + LANES
_RET_COLS = 2 * RET_HEADS * RET_DK + 2 * RET_WIDTH
_C_END = _C_RET + _RET_COLS


def _params(n_axes, vmem=VMEM_LIMIT):
    return pltpu.CompilerParams(dimension_semantics=("arbitrary",) * n_axes, vmem_limit_bytes=vmem)


def _dot(a, b):
    return jnp.dot(a, b, preferred_element_type=F32)


def _dot_nt(a, b):
    return lax.dot_general(a, b, (((1,), (1,)), ((), ())), preferred_element_type=F32)


def _rms(x, g):
    return x * lax.rsqrt(jnp.mean(x * x, axis=-1, keepdims=True) + EPS) * g


def _group_rms(x, g, ones_ref):
    w = x.shape[1]
    ones = ones_ref[0:w, 0:w]
    sq = x * x
    hi = sq.astype(BF16)
    lo = (sq - hi.astype(F32)).astype(BF16)
    ms = (_dot(hi, ones) + _dot(lo, ones)) * (1.0 / HEAD_DIM)
    return x * lax.rsqrt(ms + EPS) * g


def _group_ones():
    lane = np.arange(NSA_WIDTH) // HEAD_DIM
    return jnp.asarray(lane[:, None] == lane[None, :], BF16)


def _inproj_kernel(x_ref, g_ref, w_ref, qg_ref, kg_ref, ones_ref, kc_ref, vc_ref, ret_ref,
                   qT_ref, ks_ref, kw_ref, vsT_ref, vwT_ref, gT_ref, *, steps_per_row):
    n_tok = x_ref.shape[0]
    tiles = range(n_tok // Q_BLOCK)
    rows = [slice(u * Q_BLOCK, (u + 1) * Q_BLOCK) for u in tiles]
    xn = _rms(x_ref[...], g_ref[...]).astype(BF16)
    q = _dot(xn, w_ref[:, _C_Q:_C_KV])
    mid = _dot(xn, w_ref[:, _C_KV:_C_RET])
    kv = mid[:, 0:_C_KC - _C_KV]
    kc_ref[...] = mid[:, _C_KC - _C_KV:_C_VC - _C_KV]
    vc_ref[...] = mid[:, _C_VC - _C_KV:_C_GT - _C_KV]
    gt = mid[:, _C_GT - _C_KV:_C_RET - _C_KV]

    scale = HEAD_DIM ** -0.5 * LOG2E
    qn = _group_rms(q, qg_ref[...], ones_ref) * scale
    qt = [qn[rows[u]].T for u in tiles]
    for g in range(NSA_KV_HEADS):
        for r in range(NSA_GROUP):
            h = g * NSA_GROUP + r
            for u in tiles:
                qT_ref[g, u, :, r * Q_BLOCK:(r + 1) * Q_BLOCK] = qt[u][h * HEAD_DIM:(h + 1) * HEAD_DIM, :].astype(BF16)
    vst = [kv[rows[u], 2 * KV_WIDTH:3 * KV_WIDTH].T for u in tiles]
    vwt = [kv[rows[u], 3 * KV_WIDTH:4 * KV_WIDTH].T for u in tiles]
    gts = [jax.nn.sigmoid(gt[rows[u], :].T[0:32, :]) for u in tiles]
    pos0 = (pl.program_id(0) % steps_per_row) * n_tok
    pos = pos0 + lax.broadcasted_iota(jnp.int32, (n_tok, HEAD_DIM), 0)
    col = lax.broadcasted_iota(jnp.int32, (n_tok, HEAD_DIM), 1)
    kpos = jnp.where(col < 3, pos // SLC_LEN, jnp.where(col < 6, pos % SLC_LEN, 0)).astype(F32)
    ones_row = (lax.broadcasted_iota(jnp.int32, (V_ROWS - HEAD_DIM, Q_BLOCK), 0) == 0).astype(F32)
    ks = _group_rms(kv[:, 0:KV_WIDTH], kg_ref[0:1, :], ones_ref)
    kw = _group_rms(kv[:, KV_WIDTH:2 * KV_WIDTH], kg_ref[1:2, :], ones_ref)
    for g in range(NSA_KV_HEADS):
        sl = slice(g * HEAD_DIM, (g + 1) * HEAD_DIM)
        ks_ref[g] = jnp.concatenate([ks[:, sl], kpos], axis=1).astype(BF16)
        kw_ref[g] = jnp.concatenate([kw[:, sl], kpos], axis=1).astype(BF16)
        for u in tiles:
            vsT_ref[g, :, rows[u]] = jnp.concatenate([vst[u][sl, :], ones_row], axis=0).astype(BF16)
            vwT_ref[g, :, rows[u]] = jnp.concatenate([vwt[u][sl, :], ones_row], axis=0).astype(BF16)
    for u in tiles:
        gT_ref[u] = gts[u]
    ret_ref[...] = _dot(xn, w_ref[:, _C_RET:_C_END])


def _inproj(x2, g, w, qg, kg, B, T, tm=1024):
    n, d = x2.shape
    nq = T // Q_BLOCK
    G = NSA_KV_HEADS
    tiles = tm // Q_BLOCK
    spr = T // tm
    assert T % tm == 0
    const = lambda i: (0, 0)
    row = lambda i: (i, 0)
    return pl.pallas_call(
        functools.partial(_inproj_kernel, steps_per_row=spr),
        grid=(n // tm,),
        in_specs=[pl.BlockSpec((tm, d), row),
                  pl.BlockSpec((1, d), const),
                  pl.BlockSpec((d, _C_END), const),
                  pl.BlockSpec((1, NSA_WIDTH), const),
                  pl.BlockSpec((2, KV_WIDTH), const),
                  pl.BlockSpec((NSA_WIDTH, NSA_WIDTH), const)],
        out_specs=[pl.BlockSpec((tm, KV_WIDTH), row),
                   pl.BlockSpec((tm, KV_WIDTH), row),
                   pl.BlockSpec((tm, _RET_COLS), row),
                   pl.BlockSpec((None, G, tiles, HEAD_DIM, GQ), lambda i: (i // spr, 0, i % spr, 0, 0)),
                   pl.BlockSpec((None, G, tm, 2 * HEAD_DIM), lambda i: (i // spr, 0, i % spr, 0)),
                   pl.BlockSpec((None, G, tm, 2 * HEAD_DIM), lambda i: (i // spr, 0, i % spr, 0)),
                   pl.BlockSpec((None, G, V_ROWS, tm), lambda i: (i // spr, 0, 0, i % spr)),
                   pl.BlockSpec((None, G, V_ROWS, tm), lambda i: (i // spr, 0, 0, i % spr)),
                   pl.BlockSpec((None, tiles, 32, Q_BLOCK), lambda i: (i // spr, i % spr, 0, 0))],
        out_shape=[jax.ShapeDtypeStruct((n, KV_WIDTH), F32),
                   jax.ShapeDtypeStruct((n, KV_WIDTH), F32),
                   jax.ShapeDtypeStruct((n, _RET_COLS), F32),
                   jax.ShapeDtypeStruct((B, G, nq, HEAD_DIM, GQ), BF16),
                   jax.ShapeDtypeStruct((B, G, T, 2 * HEAD_DIM), BF16),
                   jax.ShapeDtypeStruct((B, G, T, 2 * HEAD_DIM), BF16),
                   jax.ShapeDtypeStruct((B, G, V_ROWS, T), BF16),
                   jax.ShapeDtypeStruct((B, G, V_ROWS, T), BF16),
                   jax.ShapeDtypeStruct((B, nq, 32, Q_BLOCK), F32)],
        compiler_params=_params(1),
        name="inproj",
    )(x2, g, w, jnp.tile(qg, (1, NSA_HEADS)), jnp.tile(kg, (1, NSA_KV_HEADS)), _group_ones())


def _compress_kernel(kc_ref, vc_ref, wk_ref, wv_ref, pk_ref, pv_ref, kg_ref, kcmp_ref, vcT_ref):
    ncp = kc_ref.shape[0] // CMP_STRIDE

    def comp(a_ref, w_ref, p_ref):
        lo = jnp.zeros((ncp, KV_WIDTH), F32)
        hi = jnp.zeros((ncp, KV_WIDTH), F32)
        for l in range(CMP_STRIDE):
            a = a_ref[pl.ds(l, ncp, stride=CMP_STRIDE), :]
            lo += _dot((a + p_ref[0, l:l + 1, :]).astype(BF16), w_ref[0, l])
            hi += _dot((a + p_ref[1, l:l + 1, :]).astype(BF16), w_ref[1, l])
        return lo + pltpu.roll(hi, ncp - 1, 0)

    k = comp(kc_ref, wk_ref, pk_ref)
    v = comp(vc_ref, wv_ref, pv_ref).T
    cend = lax.broadcasted_iota(jnp.int32, (ncp, HEAD_DIM), 0) * CMP_STRIDE + (CMP_LEN - 1)
    col = lax.broadcasted_iota(jnp.int32, (ncp, HEAD_DIM), 1)
    kpos = jnp.where(col < 3, cend // SLC_LEN, jnp.where(col < 6, cend % SLC_LEN, 0)).astype(F32)
    for g in range(NSA_KV_HEADS):
        sl = slice(g * HEAD_DIM, (g + 1) * HEAD_DIM)
        kcmp_ref[g] = jnp.concatenate([_rms(k[:, sl], kg_ref[...]), kpos], axis=1).astype(BF16)
        vcT_ref[g] = v[sl, :].astype(BF16)


def _compress(kc, vc, wk, wv, pk, pv, kg, B, T):
    ncp = T // CMP_STRIDE
    G = NSA_KV_HEADS
    const4 = lambda b: (0, 0, 0, 0)
    const3 = lambda b: (0, 0, 0)
    const2 = lambda b: (0, 0)
    return pl.pallas_call(
        _compress_kernel,
        grid=(B,),
        in_specs=[pl.BlockSpec((T, KV_WIDTH), lambda b: (b, 0)),
                  pl.BlockSpec((T, KV_WIDTH), lambda b: (b, 0)),
                  pl.BlockSpec((2, CMP_STRIDE, KV_WIDTH, KV_WIDTH), const4),
                  pl.BlockSpec((2, CMP_STRIDE, KV_WIDTH, KV_WIDTH), const4),
                  pl.BlockSpec((2, CMP_STRIDE, KV_WIDTH), const3),
                  pl.BlockSpec((2, CMP_STRIDE, KV_WIDTH), const3),
                  pl.BlockSpec((1, HEAD_DIM), const2)],
        out_specs=[pl.BlockSpec((None, G, ncp, 2 * HEAD_DIM), lambda b: (b, 0, 0, 0)),
                   pl.BlockSpec((None, G, HEAD_DIM, ncp), lambda b: (b, 0, 0, 0))],
        out_shape=[jax.ShapeDtypeStruct((B, G, ncp, 2 * HEAD_DIM), BF16),
                   jax.ShapeDtypeStruct((B, G, HEAD_DIM, ncp), BF16)],
        compiler_params=_params(1),
        name="nsa_compress",
    )(kc, vc, wk, wv, pk, pv, kg)


def _compress_weights(w, pos):
    G = NSA_KV_HEADS
    w4 = w.reshape(2, CMP_STRIDE, HEAD_DIM, HEAD_DIM)
    eye = jnp.eye(G, dtype=w.dtype)
    wbd = jnp.einsum('hlde,gk->hlgdke', w4, eye).reshape(2, CMP_STRIDE, KV_WIDTH, KV_WIDTH)
    p = pos.reshape(2, CMP_STRIDE, 1, HEAD_DIM)
    p = jnp.broadcast_to(p, (2, CMP_STRIDE, G, HEAD_DIM)).reshape(2, CMP_STRIDE, KV_WIDTH)
    return wbd.astype(BF16), p


def _split3(x):
    hi = x.astype(BF16)
    r = x - hi.astype(F32)
    mid = r.astype(BF16)
    lo = (r - mid.astype(F32)).astype(BF16)
    return hi, mid, lo


def _nsa_cmp_kernel(qT_ref, qaug_ref, kc_ref, vcT_ref, ovT_ref, ocmp_ref, sel_ref, flag_ref, *, n_sel):
    ncp = kc_ref.shape[0]
    ns = ovT_ref.shape[0]
    tiles = range(CMP_TILES)
    i0 = pl.program_id(2) * CMP_TILES
    lane = lax.broadcasted_iota(jnp.int32, (1, GQ), 1)
    q = [jnp.concatenate([qT_ref[u], qaug_ref[...]], axis=0) for u in tiles]
    t_row = [(i0 + u) * Q_BLOCK + (lane & (Q_BLOCK - 1)) for u in tiles]
    has_cmp = [(t_row[u] >= CMP_LEN - 1).astype(F32) for u in tiles]
    tq = [(i0 + u) * Q_BLOCK + lax.broadcasted_iota(jnp.int32, (1, Q_BLOCK), 1) for u in tiles]
    cur = [tq[u] // SLC_LEN for u in tiles]

    def prefix(rows):
        nsk = rows * CMP_STRIDE // SLC_LEN
        tail0 = max(rows - CMP_TAIL, 0)
        kc = kc_ref[0:rows, :]
        s = [_dot(kc, q[u]) for u in tiles]
        cend = (lax.broadcasted_iota(jnp.int32, (rows - tail0, 1), 0) + tail0) * CMP_STRIDE + (CMP_LEN - 1)
        tail = [jnp.where(t_row[u] >= cend, s[u][tail0:], NEG) for u in tiles]
        s = [jnp.concatenate([s[u][0:tail0], tail[u]], axis=0) if tail0 else tail[u] for u in tiles]
        m = [jnp.max(s[u], axis=0, keepdims=True) for u in tiles]
        e = [jnp.exp2(s[u] - m[u]) for u in tiles]
        p = [e[u] * (has_cmp[u] / jnp.sum(e[u], axis=0, keepdims=True)) for u in tiles]
        vc = vcT_ref[:, 0:rows]
        for u in tiles:
            ocmp_ref[u] = _dot(vc, p[u].astype(BF16))

        ps = [p[u][:, 0:Q_BLOCK] for u in tiles]
        for r in range(1, NSA_GROUP):
            ps = [ps[u] + p[u][:, r * Q_BLOCK:(r + 1) * Q_BLOCK] for u in tiles]
        ov = ovT_ref[0:nsk, 0:rows]
        split = [_split3(ps[u]) for u in tiles]
        imp = [_dot(ov, split[u][0]) + _dot(ov, split[u][1]) + _dot(ov, split[u][2]) for u in tiles]

        blk = lax.broadcasted_iota(jnp.int32, (nsk, 1), 0)
        forced = [(blk == 0) | (blk == cur[u]) | (blk == cur[u] - 1) for u in tiles]
        valid = [blk * SLC_LEN <= tq[u] for u in tiles]
        imp = [jnp.where(forced[u], -3e38, jnp.where(valid[u], imp[u], -BIG)) for u in tiles]
        blk_f = blk.astype(F32)
        sel = [forced[u].astype(F32) for u in tiles]
        for _ in range(n_sel - N_FORCED):
            mx = [jnp.max(imp[u], axis=0, keepdims=True) for u in tiles]
            idx = [jnp.min(jnp.where(imp[u] == mx[u], blk_f, float(ns)), axis=0, keepdims=True) for u in tiles]
            pick = [blk_f == idx[u] for u in tiles]
            sel = [jnp.where(pick[u], 1.0, sel[u]) for u in tiles]
            imp = [jnp.where(pick[u], -3e38, imp[u]) for u in tiles]
        ones = jnp.ones((8, Q_BLOCK), BF16)
        for u in tiles:
            sel_ref[u, 0:nsk, :] = sel[u]
            cnt = _dot_nt(ones, sel[u].astype(BF16))
            flag_ref[u, :, 0:nsk] = (cnt > 0).astype(jnp.int32)
            if nsk < ns:
                sel_ref[u, nsk:, :] = jnp.zeros((ns - nsk, Q_BLOCK), F32)
                flag_ref[u, :, nsk:] = jnp.zeros((8, ns - nsk), jnp.int32)

    n_variants = ncp // CMP_CHUNK
    last = i0 + CMP_TILES - 1
    variant = (last * (Q_BLOCK // CMP_STRIDE) + (Q_BLOCK // CMP_STRIDE - 2)) // CMP_CHUNK
    for k in range(n_variants):
        pl.when(variant == k)(functools.partial(prefix, (k + 1) * CMP_CHUNK))


def _nsa_cmp(qT, qaug, kcmp, vcT, ovT, B, T):
    G = NSA_KV_HEADS
    nq = T // Q_BLOCK
    ncp = T // CMP_STRIDE
    ns = T // SLC_LEN
    n_sel = min(SLC_TOPK, ns)
    assert ncp % CMP_CHUNK == 0 and n_sel > N_FORCED and nq % CMP_TILES == 0
    tile = lambda b, g, i: (b, g, i, 0, 0)
    return pl.pallas_call(
        functools.partial(_nsa_cmp_kernel, n_sel=n_sel),
        grid=(B, G, nq // CMP_TILES),
        in_specs=[pl.BlockSpec((None, None, CMP_TILES, HEAD_DIM, GQ), tile),
                  pl.BlockSpec((None, HEAD_DIM, GQ), lambda b, g, i: (g, 0, 0)),
                  pl.BlockSpec((None, None, ncp, 2 * HEAD_DIM), lambda b, g, i: (b, g, 0, 0)),
                  pl.BlockSpec((None, None, HEAD_DIM, ncp), lambda b, g, i: (b, g, 0, 0)),
                  pl.BlockSpec((ns, ncp), lambda b, g, i: (0, 0))],
        out_specs=[pl.BlockSpec((None, None, CMP_TILES, HEAD_DIM, GQ), tile),
                   pl.BlockSpec((None, None, CMP_TILES, ns, Q_BLOCK), tile),
                   pl.BlockSpec((None, None, CMP_TILES, 8, ns), tile)],
        out_shape=[jax.ShapeDtypeStruct((B, G, nq, HEAD_DIM, GQ), F32),
                   jax.ShapeDtypeStruct((B, G, nq, ns, Q_BLOCK), F32),
                   jax.ShapeDtypeStruct((B, G, nq, 8, ns), jnp.int32)],
        compiler_params=_params(3),
        name="nsa_cmp",
    )(qT, qaug, kcmp, vcT, ovT)


def _nsa_main_kernel(list_ref, cnt_ref, qT_ref, qaug_ref, ks_ref, vsT_ref, kw_ref, vwT_ref, sel_ref, gT_ref, ocmp_ref,
                     lowb_ref, causb_ref, out_ref, m_sc, acc_sc, win_sc):
    b, g = pl.program_id(0), pl.program_id(1)
    tiles = range(MAIN_TILES)
    i = [pl.program_id(2) * MAIN_TILES + u for u in tiles]
    tile_id = [(b * pl.num_programs(1) + g) * (pl.num_programs(2) * MAIN_TILES) + i[u] for u in tiles]
    n_steps = sel_ref.shape[1] // 2
    q = [jnp.concatenate([qT_ref[u], qaug_ref[...]], axis=0) for u in tiles]
    k0 = [pl.multiple_of(i[u] * Q_BLOCK, Q_BLOCK) for u in tiles]

    def sel_bias(u, j, valid):
        def row(r):
            picked = (sel_ref[u, pl.ds(r, 1), :] > 0.5) & valid
            return jnp.concatenate([jnp.where(picked, 0.0, NEG)] * NSA_GROUP, axis=1)
        return row(2 * j), row(2 * j + 1)

    def add_sel_bias(s, ba, bb):
        return jnp.concatenate([s[0:SLC_LEN] + ba, s[SLC_LEN:] + bb], axis=0)

    lowb = jnp.concatenate([lowb_ref[...]] * NSA_GROUP, axis=1)
    causb = jnp.concatenate([causb_ref[...]] * NSA_GROUP, axis=1)

    bias_d = [sel_bias(u, i[u], True) for u in tiles]
    sd = [_dot(ks_ref[pl.ds(k0[u], KEY_STEP), :], q[u]) for u in tiles]
    sw = [_dot(kw_ref[pl.ds(k0[u], WIN_KEYS), :], q[u]) for u in tiles]
    sd = [add_sel_bias(sd[u], *bias_d[u]) + causb for u in tiles]
    sw = [jnp.concatenate([sw[u][0:Q_BLOCK] + lowb, sw[u][Q_BLOCK:WIN], sw[u][WIN:] + causb], axis=0) for u in tiles]
    md = [jnp.max(sd[u], axis=0, keepdims=True) for u in tiles]
    mw = [jnp.max(sw[u], axis=0, keepdims=True) for u in tiles]
    accd = [_dot(vsT_ref[:, pl.ds(k0[u], KEY_STEP)], jnp.exp2((sd[u] - md[u]).astype(BF16))) for u in tiles]
    ow = [_dot(vwT_ref[:, pl.ds(k0[u], WIN_KEYS)], jnp.exp2((sw[u] - mw[u]).astype(BF16))) for u in tiles]
    for u in tiles:
        m_sc[u] = md[u]
        acc_sc[u] = accd[u]
        win_sc[u] = ow[u][0:HEAD_DIM] / ow[u][HEAD_DIM:HEAD_DIM + 1]

    def scores(u, t):
        ks, vs, biases = [], [], []
        for x in range(STEP_GROUP):
            j = list_ref[tile_id[u] * n_steps + t * STEP_GROUP + x]
            valid = j >= 0
            j = jnp.maximum(j, 0)
            kj = pl.multiple_of(j * KEY_STEP, KEY_STEP)
            ks.append(ks_ref[pl.ds(kj, KEY_STEP), :])
            vs.append(vsT_ref[:, pl.ds(kj, KEY_STEP)])
            biases.append(sel_bias(u, j, valid))
        s = _dot(jnp.concatenate(ks, axis=0), q[u])
        s = jnp.concatenate([add_sel_bias(s[x * KEY_STEP:(x + 1) * KEY_STEP], *biases[x])
                             for x in range(STEP_GROUP)], axis=0)
        return s, jnp.max(s, axis=0, keepdims=True), jnp.concatenate(vs, axis=1)

    def accumulate(u, s, smax, vcat):
        m_old = m_sc[u]
        m_new = jnp.maximum(m_old, smax)
        alpha = jnp.exp2(m_old - m_new)
        acc_sc[u] = alpha * acc_sc[u] + _dot(vcat, jnp.exp2((s - m_new).astype(BF16)))
        m_sc[u] = m_new

    def run(work, t, carry):
        staged = [(u, scores(u, t * mult + off)) for (u, mult, off) in work]
        for u, args in staged:
            accumulate(u, *args)
        return carry

    for u0 in range(0, MAIN_TILES, LOOP_TILES):
        us = range(u0, u0 + LOOP_TILES)
        n_groups = functools.reduce(
            jnp.maximum, [(cnt_ref[tile_id[u]] + (STEP_GROUP - 1)) // STEP_GROUP for u in us])
        lax.fori_loop(0, n_groups // 2, functools.partial(run, [(u, 2, off) for off in (0, 1) for u in us]), 0)
        lax.fori_loop(n_groups // 2 * 2, n_groups, functools.partial(run, [(u, 1, 0) for u in us]), 0)

    def gate(u, k):
        rows = [gT_ref[u, pl.ds(g * (NSA_GROUP * 3) + r * 3 + k, 1), :] for r in range(NSA_GROUP)]
        return jnp.concatenate(rows, axis=1)

    o_slc = [acc_sc[u, 0:HEAD_DIM, :] / acc_sc[u, HEAD_DIM:HEAD_DIM + 1, :] for u in tiles]
    o = [gate(u, 0) * ocmp_ref[u] + gate(u, 1) * o_slc[u] + gate(u, 2) * win_sc[u] for u in tiles]
    o = [jnp.concatenate([o[u], jnp.zeros_like(o[u])], axis=0) for u in tiles]
    for r in range(NSA_GROUP):
        ot = [o[u][:, r * Q_BLOCK:(r + 1) * Q_BLOCK].T[:, 0:HEAD_DIM] for u in tiles]
        for u in tiles:
            out_ref[u * Q_BLOCK:(u + 1) * Q_BLOCK, r * HEAD_DIM:(r + 1) * HEAD_DIM] = ot[u]


def _nsa_steps_kernel(flagT_ref, pairT_ref, list_ref, cnt_ref, *, nq):
    n_steps, nt = list_ref.shape
    need = _dot(pairT_ref[...], flagT_ref[...].astype(BF16)) > 0
    step = lax.broadcasted_iota(jnp.int32, (n_steps, 1), 0)
    own = lax.broadcasted_iota(jnp.int32, (1, nt), 1) % nq
    need = need & (step < own)
    need_f = need.astype(F32)
    earlier = (lax.broadcasted_iota(jnp.int32, (n_steps, n_steps), 1) < step).astype(BF16)
    slot = _dot(earlier, need_f.astype(BF16))
    total = jnp.sum(need_f, axis=0, keepdims=True)
    cnt_ref[...] = jnp.broadcast_to(total, cnt_ref.shape).astype(jnp.int32)
    step_f = step.astype(F32)
    for p in range(n_steps):
        val = jnp.sum(jnp.where(need & (slot == p), step_f, 0.0), axis=0, keepdims=True)
        list_ref[p:p + 1, :] = jnp.where(total > p, val, -1.0).astype(jnp.int32)


def _nsa_steps(flags, nq):
    nt, ns = flags.shape
    n_steps = ns // 2
    pairT = jnp.asarray(np.arange(n_steps)[:, None] == np.arange(ns)[None, :] // 2, BF16)
    lists, counts = pl.pallas_call(
        functools.partial(_nsa_steps_kernel, nq=nq),
        out_shape=[jax.ShapeDtypeStruct((n_steps, nt), jnp.int32), jax.ShapeDtypeStruct((8, nt), jnp.int32)],
        name="nsa_steps",
    )(flags.T.astype(F32), pairT)
    return lists.T.reshape(-1), counts[0]


def _nsa_main(lists, counts, qT, qaug, ks, vsT, kw, vwT, sel, gT, ocmp, lowb, causb, B, T):
    G = NSA_KV_HEADS
    nq = T // Q_BLOCK
    ns = T // SLC_LEN
    whole = lambda b, g, i, *_: (b, g, 0, 0)
    tile = lambda b, g, i, *_: (b, g, i, 0, 0)
    const = lambda b, g, i, *_: (0, 0)
    grid_spec = pltpu.PrefetchScalarGridSpec(
        num_scalar_prefetch=2,
        grid=(B, G, nq // MAIN_TILES),
        in_specs=[pl.BlockSpec((None, None, MAIN_TILES, HEAD_DIM, GQ), tile),
                  pl.BlockSpec((None, HEAD_DIM, GQ), lambda b, g, i, *_: (g, 0, 0)),
                  pl.BlockSpec((None, None, T, 2 * HEAD_DIM), whole),
                  pl.BlockSpec((None, None, V_ROWS, T), whole),
                  pl.BlockSpec((None, None, T + WIN, 2 * HEAD_DIM), whole),
                  pl.BlockSpec((None, None, V_ROWS, T + WIN), whole),
                  pl.BlockSpec((None, None, MAIN_TILES, ns, Q_BLOCK), tile),
                  pl.BlockSpec((None, MAIN_TILES, 32, Q_BLOCK), lambda b, g, i, *_: (b, i, 0, 0)),
                  pl.BlockSpec((None, None, MAIN_TILES, HEAD_DIM, GQ), tile),
                  pl.BlockSpec((Q_BLOCK, Q_BLOCK), const),
                  pl.BlockSpec((Q_BLOCK, Q_BLOCK), const)],
        out_specs=pl.BlockSpec((None, MAIN_TILES * Q_BLOCK, NSA_GROUP * HEAD_DIM), lambda b, g, i, *_: (b, i, g)),
        scratch_shapes=[pltpu.VMEM((MAIN_TILES, 1, GQ), F32), pltpu.VMEM((MAIN_TILES, V_ROWS, GQ), F32),
                        pltpu.VMEM((MAIN_TILES, HEAD_DIM, GQ), F32)],
    )
    return pl.pallas_call(
        _nsa_main_kernel,
        grid_spec=grid_spec,
        out_shape=jax.ShapeDtypeStruct((B, T, NSA_WIDTH), F32),
        compiler_params=_params(3),
        name="nsa_main",
    )(lists, counts, qT, qaug, ks, vsT, kw, vwT, sel, gT, ocmp, lowb, causb)


def _ret_kernel(p_ref, decay_ref, xi_ref, zeta_ref, gch_ref, ng_ref, ones_ref, out_ref, state_ref):
    @pl.when(pl.program_id(0) == 0)
    def _():
        state_ref[...] = jnp.zeros(state_ref.shape, F32)

    rows = range(p_ref.shape[0])
    kw = RET_HEADS * RET_DK
    p = [p_ref[b] for b in rows]
    rq = [p[b][:, 0:kw] * (RET_DK ** -0.5) for b in rows]
    rk = [p[b][:, kw:2 * kw] for b in rows]
    rkT = [rk[b].T for b in rows]
    rv = [p[b][:, 2 * kw:2 * kw + RET_WIDTH] for b in rows]
    xi = xi_ref[...]
    outs = [[] for _ in rows]
    for h in range(RET_HEADS):
        dk = slice(h * RET_DK, (h + 1) * RET_DK)
        dv = slice(h * RET_DV, (h + 1) * RET_DV)
        st = [state_ref[b, h] for b in rows]
        inner = [_dot_nt(rq[b][:, dk], rk[b][:, dk]) * decay_ref[h] for b in rows]
        o = [_dot(inner[b], rv[b][:, dv]) + _dot(rq[b][:, dk], st[b]) * xi[:, h:h + 1] for b in rows]
        for b in rows:
            state_ref[b, h] = (st[b] * gch_ref[h:h + 1, 0:1]
                               + _dot(rkT[b][dk, :] * zeta_ref[h:h + 1, :], rv[b][:, dv]))
            outs[b].append(o[b])
    normed = [_group_rms(jnp.concatenate(outs[b], axis=1), ng_ref[...], ones_ref) for b in rows]
    for b in rows:
        rg = p[b][:, 2 * kw + RET_WIDTH:2 * kw + 2 * RET_WIDTH]
        out_ref[b] = normed[b] * (rg * jax.nn.sigmoid(rg))


def _ret_consts():
    H, C = RET_HEADS, RET_CHUNK
    log_g = np.log1p(-np.exp2(-5.0 - np.arange(H, dtype=np.float64)))
    idx = np.arange(C, dtype=np.float64)
    diff = idx[:, None] - idx[None, :]
    decay = np.where(diff >= 0, np.exp(np.maximum(diff, 0.0) * log_g[:, None, None]), 0.0)
    zeta = np.exp((C - 1 - idx) * log_g[:, None])
    xi = np.exp((idx + 1) * log_g[:, None]).T
    g_chunk = np.broadcast_to(np.exp(C * log_g)[:, None], (H, LANES))
    return tuple(jnp.asarray(a, F32) for a in (decay, xi, zeta, g_chunk))


def _retention(pret, ng, B, T):
    nch = T // RET_CHUNK
    decay, xi, zeta, gch = _ret_consts()
    c2 = lambda c: (0, 0)
    out = pl.pallas_call(
        _ret_kernel,
        grid=(nch,),
        in_specs=[pl.BlockSpec((B, RET_CHUNK, _RET_COLS), lambda c: (0, c, 0)),
                  pl.BlockSpec((RET_HEADS, RET_CHUNK, RET_CHUNK), lambda c: (0, 0, 0)),
                  pl.BlockSpec((RET_CHUNK, RET_HEADS), c2),
                  pl.BlockSpec((RET_HEADS, RET_CHUNK), c2),
                  pl.BlockSpec((RET_HEADS, LANES), c2),
                  pl.BlockSpec((1, RET_WIDTH), c2),
                  pl.BlockSpec((RET_WIDTH, RET_WIDTH), c2)],
        out_specs=pl.BlockSpec((B, RET_CHUNK, RET_WIDTH), lambda c: (0, c, 0)),
        out_shape=jax.ShapeDtypeStruct((B, T, RET_WIDTH), F32),
        scratch_shapes=[pltpu.VMEM((B, RET_HEADS, RET_DK, RET_DV), F32)],
        compiler_params=_params(1),
        name="retention",
    )(pret.reshape(B, T, _RET_COLS), decay, xi, zeta, gch, ng, _group_ones())
    return out.reshape(B * T, RET_WIDTH)


def _mixer_residual(x_ref, nsa_ref, ret_ref, wo_ref):
    return (x_ref[...] + _dot(nsa_ref[...].astype(BF16), wo_ref[0:NSA_WIDTH, :])
            + _dot(ret_ref[...].astype(BF16), wo_ref[NSA_WIDTH:, :]))


def _mixer_specs(tm, d, row, const):
    return [pl.BlockSpec((tm, d), row), pl.BlockSpec((tm, NSA_WIDTH), row), pl.BlockSpec((tm, RET_WIDTH), row),
            pl.BlockSpec((NSA_WIDTH + RET_WIDTH, d), const)]


def _pack_kernel(wg_ref, wu_ref, o_ref):
    fc = wg_ref.shape[1]
    o_ref[:, 0:fc] = wg_ref[...].astype(BF16)
    o_ref[:, fc:] = wu_ref[...].astype(BF16)


def _pack_gate_up(w_gate, w_up):
    E, d, dff = w_gate.shape
    fc = FF_CHUNK
    spec = pl.BlockSpec((None, d, fc), lambda e, f: (e, 0, f))
    return pl.pallas_call(
        _pack_kernel,
        grid=(E, dff // fc),
        in_specs=[spec, spec],
        out_specs=pl.BlockSpec((None, None, d, 2 * fc), lambda e, f: (e, f, 0, 0)),
        out_shape=jax.ShapeDtypeStruct((E, dff // fc, d, 2 * fc), BF16),
        compiler_params=_params(2),
        name="pack_gate_up",
    )(w_gate, w_up)


def _swiglu(rows, wgu_ref, wd_ref):
    au = _dot(rows, wgu_ref[...])
    fc = au.shape[1] // 2
    a = au[:, 0:fc]
    return _dot((a * jax.nn.sigmoid(a) * au[:, fc:]).astype(BF16), wd_ref[...])


def _ffn_kernel(x_ref, nsa_ref, ret_ref, wo_ref, g_ref, wgu_ref, wd_ref, o_ref, h_sc):
    f = pl.program_id(1)

    @pl.when(f == 0)
    def _():
        x = _mixer_residual(x_ref, nsa_ref, ret_ref, wo_ref)
        h_sc[...] = _rms(x, g_ref[...]).astype(BF16)
        o_ref[...] = x

    o_ref[...] += _swiglu(h_sc[...], wgu_ref, wd_ref)


def _ffn(x2, nsa, ret, wo, g, wgu, wd, tm=1024):
    n, d = x2.shape
    dff = wd.shape[0]
    fc = FF_CHUNK
    return pl.pallas_call(
        _ffn_kernel,
        grid=(n // tm, dff // fc),
        in_specs=_mixer_specs(tm, d, lambda i, f: (i, 0), lambda i, f: (0, 0)) + [
            pl.BlockSpec((1, d), lambda i, f: (0, 0)),
            pl.BlockSpec((None, None, d, 2 * fc), lambda i, f: (0, f, 0, 0)),
            pl.BlockSpec((fc, d), lambda i, f: (f, 0))],
        out_specs=pl.BlockSpec((tm, d), lambda i, f: (i, 0)),
        out_shape=jax.ShapeDtypeStruct((n, d), F32),
        scratch_shapes=[pltpu.VMEM((tm, d), BF16)],
        compiler_params=_params(2),
        name="ffn_dense",
    )(x2, nsa, ret, wo, g, wgu, wd)


def _router_kernel(x_ref, nsa_ref, ret_ref, wo_ref, g_ref, r_ref, rb_ref, tri_ref,
                   x1_ref, h_ref, rank_ref, comb_ref, rankT_ref, cnt_ref):
    x = _mixer_residual(x_ref, nsa_ref, ret_ref, wo_ref)
    x1_ref[...] = x
    h = _rms(x, g_ref[...])
    h_ref[...] = h.astype(BF16)
    hh, hm, _ = _split3(h)
    rh, rm, _ = _split3(r_ref[...])
    both = _dot(hh, jnp.concatenate([rh, rm], axis=1))
    logits = both[:, 0:LANES] + (both[:, LANES:] + _dot(hm, rh)) + rb_ref[...]
    lane = lax.broadcasted_iota(jnp.int32, logits.shape, 1).astype(F32)
    logits = jnp.where(lane < N_EXPERTS, logits, NEG)
    m1 = jnp.max(logits, axis=1, keepdims=True)
    i1 = jnp.min(jnp.where(logits == m1, lane, float(LANES)), axis=1, keepdims=True)
    l2 = jnp.where(lane == i1, NEG, logits)
    m2 = jnp.max(l2, axis=1, keepdims=True)
    i2 = jnp.min(jnp.where(l2 == m2, lane, float(LANES)), axis=1, keepdims=True)
    e2 = jnp.exp(m2 - m1)
    w1 = 1.0 / (1.0 + e2)
    w2 = e2 / (1.0 + e2)
    use1, use2 = lane == i1, lane == i2
    comb_ref[...] = jnp.where(use1, w1, 0.0) + jnp.where(use2, w2, 0.0)
    use = (use1 | use2).astype(F32)
    rank = jnp.where(use > 0, _dot(tri_ref[...], use.astype(BF16)), -1.0)
    rank_ref[...] = rank
    rankT_ref[...] = rank.T[0:N_EXPERTS, :]
    cnt_ref[...] = jnp.broadcast_to(jnp.sum(use, axis=0, keepdims=True), cnt_ref.shape).astype(jnp.int32)


def _router(x2, nsa, ret, wo, g, router, rb, tm):
    n, d = x2.shape
    nt = n // tm
    tri = jnp.asarray(np.arange(tm)[:, None] > np.arange(tm)[None, :], BF16)
    rpad = jnp.zeros((d, LANES), F32).at[:, :N_EXPERTS].set(router)
    rbpad = jnp.zeros((1, LANES), F32).at[0, :N_EXPERTS].set(rb)
    c2 = lambda i: (0, 0)
    row = lambda i: (i, 0)
    return pl.pallas_call(
        _router_kernel,
        grid=(nt,),
        in_specs=_mixer_specs(tm, d, row, c2) + [
            pl.BlockSpec((1, d), c2),
            pl.BlockSpec((d, LANES), c2),
            pl.BlockSpec((1, LANES), c2),
            pl.BlockSpec((tm, tm), c2)],
        out_specs=[pl.BlockSpec((tm, d), row),
                   pl.BlockSpec((tm, d), row),
                   pl.BlockSpec((tm, LANES), row),
                   pl.BlockSpec((tm, LANES), row),
                   pl.BlockSpec((N_EXPERTS, tm), lambda i: (0, i)),
                   pl.BlockSpec((None, 8, LANES), lambda i: (i, 0, 0))],
        out_shape=[jax.ShapeDtypeStruct((n, d), F32),
                   jax.ShapeDtypeStruct((n, d), BF16),
                   jax.ShapeDtypeStruct((n, LANES), F32),
                   jax.ShapeDtypeStruct((n, LANES), F32),
                   jax.ShapeDtypeStruct((N_EXPERTS, n), F32),
                   jax.ShapeDtypeStruct((nt, 8, LANES), jnp.int32)],
        compiler_params=_params(1),
        name="moe_router",
    )(x2, nsa, ret, wo, g, rpad, rbpad, tri)


MOE_SUB = 144
MOE_MOVE = 2 * MOE_SUB


def _moe_kernel(cnt_ref, h_ref, rankT_ref, rank_ref, comb_ref, wgu_ref, wd_ref, x_ref, o_ref, hc_sc, oacc_sc):
    t, e, f = pl.program_id(0), pl.program_id(1), pl.program_id(2)
    nf = pl.num_programs(2)
    tm = h_ref.shape[0]
    nsub = (cnt_ref[t * N_EXPERTS + e] + (MOE_SUB - 1)) // MOE_SUB
    nmove = (nsub + 1) // 2

    @pl.when((e == 0) & (f == 0))
    def _():
        o_ref[...] = x_ref[...]

    @pl.when(f == 0)
    def _():
        rank_row = rankT_ref[...]

        def gather(s, c):
            r0 = pl.multiple_of(s * MOE_MOVE, MOE_MOVE)
            rows = (lax.broadcasted_iota(jnp.int32, (MOE_MOVE, 1), 0) + r0).astype(F32)
            onehot = (rows == rank_row).astype(BF16)
            hc_sc[pl.ds(r0, MOE_MOVE), :] = _dot(onehot, h_ref[...]).astype(BF16)
            oacc_sc[pl.ds(r0, MOE_MOVE), :] = jnp.zeros((MOE_MOVE, oacc_sc.shape[1]), F32)
            return c

        lax.fori_loop(0, nmove, gather, 0)

    def expert(n_rows, s, c):
        r0 = pl.multiple_of(s * n_rows, n_rows)
        oacc_sc[pl.ds(r0, n_rows), :] += _swiglu(hc_sc[pl.ds(r0, n_rows), :], wgu_ref, wd_ref)
        return c

    lax.fori_loop(0, nsub // 2, functools.partial(expert, MOE_MOVE), 0)
    lax.fori_loop(nsub // 2 * 2, nsub, functools.partial(expert, MOE_SUB), 0)

    @pl.when(f == nf - 1)
    def _():
        is_e = lax.broadcasted_iota(jnp.int32, (1, LANES), 1) == e
        rank_col = jnp.sum(jnp.where(is_e, rank_ref[...], 0.0), axis=1, keepdims=True)
        comb_col = jnp.sum(jnp.where(is_e, comb_ref[...], 0.0), axis=1, keepdims=True)

        def scatter(s, c):
            r0 = pl.multiple_of(s * MOE_MOVE, MOE_MOVE)
            cols = (lax.broadcasted_iota(jnp.int32, (1, MOE_MOVE), 1) + r0).astype(F32)
            onehot = (rank_col == cols).astype(BF16)
            y = _dot(onehot, oacc_sc[pl.ds(r0, MOE_MOVE), :].astype(BF16))
            o_ref[...] += comb_col * y
            return c

        lax.fori_loop(0, nmove, scatter, 0)


def _moe(counts, h, rankT, rank, comb, wgu, wd, x2, tm):
    n, d = x2.shape
    dff = wd.shape[1]
    fc = FF_CHUNK
    rows_cap = pl.cdiv(pl.cdiv(tm, MOE_SUB), 2) * MOE_MOVE
    grid_spec = pltpu.PrefetchScalarGridSpec(
        num_scalar_prefetch=1,
        grid=(n // tm, N_EXPERTS, dff // fc),
        in_specs=[pl.BlockSpec((tm, d), lambda t, e, f, c: (t, 0)),
                  pl.BlockSpec((None, 1, tm), lambda t, e, f, c: (e, 0, t)),
                  pl.BlockSpec((tm, LANES), lambda t, e, f, c: (t, 0)),
                  pl.BlockSpec((tm, LANES), lambda t, e, f, c: (t, 0)),
                  pl.BlockSpec((None, None, d, 2 * fc), lambda t, e, f, c: (e, f, 0, 0)),
                  pl.BlockSpec((None, fc, d), lambda t, e, f, c: (e, f, 0)),
                  pl.BlockSpec((tm, d), lambda t, e, f, c: (t, 0))],
        out_specs=pl.BlockSpec((tm, d), lambda t, e, f, c: (t, 0)),
        scratch_shapes=[pltpu.VMEM((rows_cap, d), BF16), pltpu.VMEM((rows_cap, d), F32)],
    )
    return pl.pallas_call(
        _moe_kernel,
        grid_spec=grid_spec,
        out_shape=jax.ShapeDtypeStruct((n, d), F32),
        compiler_params=_params(3),
        name="moe_experts",
    )(counts, h, rankT.reshape(N_EXPERTS, 1, n), rank, comb, wgu, wd, x2)


def _permute_w_in(w):
    o = np.cumsum((0, NSA_WIDTH) + (KV_WIDTH,) * 6 + (3 * NSA_HEADS,))
    q, kc, vc, ks, vs, kw, vw, gts = (w[:, o[k]:o[k + 1]] for k in range(8))
    ret = w[:, o[8]:]
    pad = jnp.zeros((w.shape[0], LANES - 3 * NSA_HEADS), w.dtype)
    return jnp.concatenate([q, ks, kw, vs, vw, kc, vc, gts, pad, ret], axis=1).astype(BF16)


def _nsa_consts(T):
    ncp = T // CMP_STRIDE
    ns = T // SLC_LEN
    cs = np.arange(ncp) * CMP_STRIDE
    ss = np.arange(ns) * SLC_LEN
    ov = np.clip(np.minimum(cs[None, :] + CMP_LEN, ss[:, None] + SLC_LEN) - np.maximum(cs[None, :], ss[:, None]), 0, None)
    ovT = (ov.astype(np.float32) / CMP_LEN)
    ovT[:, ncp - 1] = 0.0
    h = np.arange(NSA_HEADS).reshape(NSA_KV_HEADS, NSA_GROUP) + 1
    slopes = np.exp2(-8.0 * h / NSA_HEADS).astype(np.float32)
    slopes = np.repeat(slopes, Q_BLOCK, axis=1)
    parts, rest = [], np.float64(LOG2E)
    for _ in range(3):
        part = np.float64(np.asarray(rest).astype(BF16))
        parts.append(part)
        rest = rest - part
    qaug = np.zeros((NSA_KV_HEADS, HEAD_DIM, GQ), np.float32)
    for k, part in enumerate(parts):
        qaug[:, k, :] = part * SLC_LEN * slopes
        qaug[:, 3 + k, :] = part * slopes
    kq = np.arange(Q_BLOCK)[:, None] - np.arange(Q_BLOCK)[None, :]
    causb = np.where(kq <= 0, 0.0, NEG).astype(np.float32)
    lowb = np.where(kq > 0, 0.0, NEG).astype(np.float32)
    return jnp.asarray(ovT, BF16), jnp.asarray(qaug, BF16), jnp.asarray(lowb), jnp.asarray(causb)


def _mixer(x2, B, T, norm_g, w_in, q_norm_g, k_norm_g, cmp_pos, w_cmp, ret_norm_g, w_out):
    ns = T // SLC_LEN
    kc, vc, pret, qT, ks, kw, vsT, vwT, gT = _inproj(x2, norm_g[None, :], _permute_w_in(w_in), q_norm_g[None, :],
                                                     k_norm_g[1:3], B, T)
    wk, pk = _compress_weights(w_cmp[0], cmp_pos[0])
    wv, pv = _compress_weights(w_cmp[1], cmp_pos[1])
    kcmp, vcT = _compress(kc, vc, wk, wv, pk, pv, k_norm_g[0:1], B, T)
    ovT, qaug, lowb, causb = _nsa_consts(T)
    ocmp, sel, flags = _nsa_cmp(qT, qaug, kcmp, vcT, ovT, B, T)
    lists, counts = _nsa_steps(flags[:, :, :, 0, :].reshape(-1, ns), T // Q_BLOCK)
    kpad = jnp.zeros((WIN, 2 * HEAD_DIM), BF16).at[:, HEAD_DIM:HEAD_DIM + 3].set(-2.0 ** 100)
    kw = jnp.concatenate([jnp.broadcast_to(kpad, kw.shape[:2] + kpad.shape), kw], axis=2)
    vwT = jnp.pad(vwT, ((0, 0), (0, 0), (0, 0), (WIN, 0)))
    nsa = _nsa_main(lists, counts, qT, qaug, ks, vsT, kw, vwT, sel, gT, ocmp, lowb, causb, B, T)
    ret = _retention(pret, ret_norm_g[None, :], B, T)
    return nsa.reshape(B * T, NSA_WIDTH), ret, w_out.astype(BF16)


def _moe_layer(x2, nsa, ret, wo, norm_g, router, router_b, wg, wu, wd, tm=1024):
    tm = min(tm, x2.shape[0])
    x1, h, rank, comb, rankT, cnt = _router(x2, nsa, ret, wo, norm_g[None, :], router, router_b, tm)
    counts = cnt[:, 0, :N_EXPERTS].reshape(-1)
    return _moe(counts, h, rankT, rank, comb, _pack_gate_up(wg, wu), wd.astype(BF16), x1, tm)


def kernel(x, norm_mix_g, w_in, q_norm_g, k_norm_g, cmp_pos, w_cmp, ret_norm_g, w_out, norm_ffn_g,
           ffn_w_gate, ffn_w_up, ffn_w_down, moe_router, moe_router_b, moe_w_gate, moe_w_up, moe_w_down):
    B, T, D = x.shape
    depth = norm_mix_g.shape[0]
    x2 = x.reshape(B * T, D)
    for l in range(depth):
        nsa, ret, wo = _mixer(x2, B, T, norm_mix_g[l], w_in[l], q_norm_g[l], k_norm_g[l], cmp_pos[l], w_cmp[l],
                              ret_norm_g[l], w_out[l])
        j = l // 2
        if l % 2 == 0:
            x2 = _ffn(x2, nsa, ret, wo, norm_ffn_g[l][None, :],
                      _pack_gate_up(ffn_w_gate[j:j + 1], ffn_w_up[j:j + 1]), ffn_w_down[j].astype(BF16))
        else:
            x2 = _moe_layer(x2, nsa, ret, wo, norm_ffn_g[l], moe_router[j], moe_router_b[j], moe_w_gate[j],
                            moe_w_up[j], moe_w_down[j])
    return x2.reshape(B, T, D)
```

```python
import functools

import numpy as np
import jax
import jax.numpy as jnp
from jax import lax
from jax.experimental import pallas as pl
from jax.experimental.pallas import tpu as pltpu

F32 = jnp.float32
BF16 = jnp.bfloat16

HEAD_DIM = 64
NSA_HEADS = 8
NSA_KV_HEADS = 2
NSA_GROUP = NSA_HEADS // NSA_KV_HEADS
RET_HEADS = 8
RET_DK = 32
RET_DV = 64
NSA_WIDTH = NSA_HEADS * HEAD_DIM
RET_WIDTH = RET_HEADS * RET_DV
KV_WIDTH = NSA_KV_HEADS * HEAD_DIM
CMP_LEN = 32
CMP_STRIDE = 16
SLC_LEN = 64
SLC_TOPK = 16
WIN = 512
Q_BLOCK = 128
RET_CHUNK = 128
N_EXPERTS = 8
EPS = 1e-6
NEG = -1e30
BIG = 1e9
LANES = 128
GQ = NSA_GROUP * Q_BLOCK
KEY_STEP = 128
STEP_GROUP = 2
N_FORCED = 3
CMP_CHUNK = 128
CMP_TILES = 8
MAIN_TILES = 8
LOOP_TILES = 2
CMP_TAIL = CMP_CHUNK + 8
WIN_KEYS = WIN + Q_BLOCK
V_ROWS = HEAD_DIM + 16
LOG2E = 1.4426950408889634
FF_CHUNK = 1408
VMEM_LIMIT = 60 * 1024 * 1024

_C_Q = 0
_C_KV = _C_Q + NSA_WIDTH
_C_KC = _C_KV + 4 * KV_WIDTH
_C_VC = _C_KC + KV_WIDTH
_C_GT = _C_VC + KV_WIDTH
_C_RET = _C_GT + LANES
_RET_COLS = 2 * RET_HEADS * RET_DK + 2 * RET_WIDTH
_C_END = _C_RET + _RET_COLS


def _params(n_axes, vmem=VMEM_LIMIT):
    return pltpu.CompilerParams(dimension_semantics=("arbitrary",) * n_axes, vmem_limit_bytes=vmem)


def _dot(a, b):
    return jnp.dot(a, b, preferred_element_type=F32)


def _dot_nt(a, b):
    return lax.dot_general(a, b, (((1,), (1,)), ((), ())), preferred_element_type=F32)


def _rms(x, g):
    return x * lax.rsqrt(jnp.mean(x * x, axis=-1, keepdims=True) + EPS) * g


def _group_rms(x, g, ones_ref):
    w = x.shape[1]
    ones = ones_ref[0:w, 0:w]
    sq = x * x
    hi = sq.astype(BF16)
    lo = (sq - hi.astype(F32)).astype(BF16)
    ms = (_dot(hi, ones) + _dot(lo, ones)) * (1.0 / HEAD_DIM)
    return x * lax.rsqrt(ms + EPS) * g


def _group_ones():
    lane = np.arange(NSA_WIDTH) // HEAD_DIM
    return jnp.asarray(lane[:, None] == lane[None, :], BF16)


def _inproj_kernel(x_ref, g_ref, w_ref, qg_ref, kg_ref, ones_ref, kc_ref, vc_ref, ret_ref,
                   qT_ref, ks_ref, kw_ref, vsT_ref, vwT_ref, gT_ref, *, steps_per_row):
    n_tok = x_ref.shape[0]
    tiles = range(n_tok // Q_BLOCK)
    rows = [slice(u * Q_BLOCK, (u + 1) * Q_BLOCK) for u in tiles]
    xn = _rms(x_ref[...], g_ref[...]).astype(BF16)
    q = _dot(xn, w_ref[:, _C_Q:_C_KV])
    mid = _dot(xn, w_ref[:, _C_KV:_C_RET])
    kv = mid[:, 0:_C_KC - _C_KV]
    kc_ref[...] = mid[:, _C_KC - _C_KV:_C_VC - _C_KV]
    vc_ref[...] = mid[:, _C_VC - _C_KV:_C_GT - _C_KV]
    gt = mid[:, _C_GT - _C_KV:_C_RET - _C_KV]

    scale = HEAD_DIM ** -0.5 * LOG2E
    qn = _group_rms(q, qg_ref[...], ones_ref) * scale
    qt = [qn[rows[u]].T for u in tiles]
    for g in range(NSA_KV_HEADS):
        for r in range(NSA_GROUP):
            h = g * NSA_GROUP + r
            for u in tiles:
                qT_ref[g, u, :, r * Q_BLOCK:(r + 1) * Q_BLOCK] = qt[u][h * HEAD_DIM:(h + 1) * HEAD_DIM, :].astype(BF16)
    vst = [kv[rows[u], 2 * KV_WIDTH:3 * KV_WIDTH].T for u in tiles]
    vwt = [kv[rows[u], 3 * KV_WIDTH:4 * KV_WIDTH].T for u in tiles]
    gts = [jax.nn.sigmoid(gt[rows[u], :].T[0:32, :]) for u in tiles]
    pos0 = (pl.program_id(0) % steps_per_row) * n_tok
    pos = pos0 + lax.broadcasted_iota(jnp.int32, (n_tok, HEAD_DIM), 0)
    col = lax.broadcasted_iota(jnp.int32, (n_tok, HEAD_DIM), 1)
    kpos = jnp.where(col < 3, pos // SLC_LEN, jnp.where(col < 6, pos % SLC_LEN, 0)).astype(F32)
    ones_row = (lax.broadcasted_iota(jnp.int32, (V_ROWS - HEAD_DIM, Q_BLOCK), 0) == 0).astype(F32)
    ks = _group_rms(kv[:, 0:KV_WIDTH], kg_ref[0:1, :], ones_ref)
    kw = _group_rms(kv[:, KV_WIDTH:2 * KV_WIDTH], kg_ref[1:2, :], ones_ref)
    for g in range(NSA_KV_HEADS):
        sl = slice(g * HEAD_DIM, (g + 1) * HEAD_DIM)
        ks_ref[g] = jnp.concatenate([ks[:, sl], kpos], axis=1).astype(BF16)
        kw_ref[g] = jnp.concatenate([kw[:, sl], kpos], axis=1).astype(BF16)
        for u in tiles:
            vsT_ref[g, :, rows[u]] = jnp.concatenate([vst[u][sl, :], ones_row], axis=0).astype(BF16)
            vwT_ref[g, :, rows[u]] = jnp.concatenate([vwt[u][sl, :], ones_row], axis=0).astype(BF16)
    for u in tiles:
        gT_ref[u] = gts[u]
    ret_ref[...] = _dot(xn, w_ref[:, _C_RET:_C_END])


def _inproj(x2, g, w, qg, kg, B, T, tm=1024):
    n, d = x2.shape
    nq = T // Q_BLOCK
    G = NSA_KV_HEADS
    tiles = tm // Q_BLOCK
    spr = T // tm
    assert T % tm == 0
    const = lambda i: (0, 0)
    row = lambda i: (i, 0)
    return pl.pallas_call(
        functools.partial(_inproj_kernel, steps_per_row=spr),
        grid=(n // tm,),
        in_specs=[pl.BlockSpec((tm, d), row),
                  pl.BlockSpec((1, d), const),
                  pl.BlockSpec((d, _C_END), const),
                  pl.BlockSpec((1, NSA_WIDTH), const),
                  pl.BlockSpec((2, KV_WIDTH), const),
                  pl.BlockSpec((NSA_WIDTH, NSA_WIDTH), const)],
        out_specs=[pl.BlockSpec((tm, KV_WIDTH), row),
                   pl.BlockSpec((tm, KV_WIDTH), row),
                   pl.BlockSpec((tm, _RET_COLS), row),
                   pl.BlockSpec((None, G, tiles, HEAD_DIM, GQ), lambda i: (i // spr, 0, i % spr, 0, 0)),
                   pl.BlockSpec((None, G, tm, 2 * HEAD_DIM), lambda i: (i // spr, 0, i % spr, 0)),
                   pl.BlockSpec((None, G, tm, 2 * HEAD_DIM), lambda i: (i // spr, 0, i % spr, 0)),
                   pl.BlockSpec((None, G, V_ROWS, tm), lambda i: (i // spr, 0, 0, i % spr)),
                   pl.BlockSpec((None, G, V_ROWS, tm), lambda i: (i // spr, 0, 0, i % spr)),
                   pl.BlockSpec((None, tiles, 32, Q_BLOCK), lambda i: (i // spr, i % spr, 0, 0))],
        out_shape=[jax.ShapeDtypeStruct((n, KV_WIDTH), F32),
                   jax.ShapeDtypeStruct((n, KV_WIDTH), F32),
                   jax.ShapeDtypeStruct((n, _RET_COLS), F32),
                   jax.ShapeDtypeStruct((B, G, nq, HEAD_DIM, GQ), BF16),
                   jax.ShapeDtypeStruct((B, G, T, 2 * HEAD_DIM), BF16),
                   jax.ShapeDtypeStruct((B, G, T, 2 * HEAD_DIM), BF16),
                   jax.ShapeDtypeStruct((B, G, V_ROWS, T), BF16),
                   jax.ShapeDtypeStruct((B, G, V_ROWS, T), BF16),
                   jax.ShapeDtypeStruct((B, nq, 32, Q_BLOCK), F32)],
        compiler_params=_params(1),
        name="inproj",
    )(x2, g, w, jnp.tile(qg, (1, NSA_HEADS)), jnp.tile(kg, (1, NSA_KV_HEADS)), _group_ones())


def _compress_kernel(kc_ref, vc_ref, wk_ref, wv_ref, pk_ref, pv_ref, kg_ref, kcmp_ref, vcT_ref):
    ncp = kc_ref.shape[0] // CMP_STRIDE

    def comp(a_ref, w_ref, p_ref):
        lo = jnp.zeros((ncp, KV_WIDTH), F32)
        hi = jnp.zeros((ncp, KV_WIDTH), F32)
        for l in range(CMP_STRIDE):
            a = a_ref[pl.ds(l, ncp, stride=CMP_STRIDE), :]
            lo += _dot((a + p_ref[0, l:l + 1, :]).astype(BF16), w_ref[0, l])
            hi += _dot((a + p_ref[1, l:l + 1, :]).astype(BF16), w_ref[1, l])
        return lo + pltpu.roll(hi, ncp - 1, 0)

    k = comp(kc_ref, wk_ref, pk_ref)
    v = comp(vc_ref, wv_ref, pv_ref).T
    cend = lax.broadcasted_iota(jnp.int32, (ncp, HEAD_DIM), 0) * CMP_STRIDE + (CMP_LEN - 1)
    col = lax.broadcasted_iota(jnp.int32, (ncp, HEAD_DIM), 1)
    kpos = jnp.where(col < 3, cend // SLC_LEN, jnp.where(col < 6, cend % SLC_LEN, 0)).astype(F32)
    for g in range(NSA_KV_HEADS):
        sl = slice(g * HEAD_DIM, (g + 1) * HEAD_DIM)
        kcmp_ref[g] = jnp.concatenate([_rms(k[:, sl], kg_ref[...]), kpos], axis=1).astype(BF16)
        vcT_ref[g] = v[sl, :].astype(BF16)


def _compress(kc, vc, wk, wv, pk, pv, kg, B, T):
    ncp = T // CMP_STRIDE
    G = NSA_KV_HEADS
    const4 = lambda b: (0, 0, 0, 0)
    const3 = lambda b: (0, 0, 0)
    const2 = lambda b: (0, 0)
    return pl.pallas_call(
        _compress_kernel,
        grid=(B,),
        in_specs=[pl.BlockSpec((T, KV_WIDTH), lambda b: (b, 0)),
                  pl.BlockSpec((T, KV_WIDTH), lambda b: (b, 0)),
                  pl.BlockSpec((2, CMP_STRIDE, KV_WIDTH, KV_WIDTH), const4),
                  pl.BlockSpec((2, CMP_STRIDE, KV_WIDTH, KV_WIDTH), const4),
                  pl.BlockSpec((2, CMP_STRIDE, KV_WIDTH), const3),
                  pl.BlockSpec((2, CMP_STRIDE, KV_WIDTH), const3),
                  pl.BlockSpec((1, HEAD_DIM), const2)],
        out_specs=[pl.BlockSpec((None, G, ncp, 2 * HEAD_DIM), lambda b: (b, 0, 0, 0)),
                   pl.BlockSpec((None, G, HEAD_DIM, ncp), lambda b: (b, 0, 0, 0))],
        out_shape=[jax.ShapeDtypeStruct((B, G, ncp, 2 * HEAD_DIM), BF16),
                   jax.ShapeDtypeStruct((B, G, HEAD_DIM, ncp), BF16)],
        compiler_params=_params(1),
        name="nsa_compress",
    )(kc, vc, wk, wv, pk, pv, kg)


def _compress_weights(w, pos):
    G = NSA_KV_HEADS
    w4 = w.reshape(2, CMP_STRIDE, HEAD_DIM, HEAD_DIM)
    eye = jnp.eye(G, dtype=w.dtype)
    wbd = jnp.einsum('hlde,gk->hlgdke', w4, eye).reshape(2, CMP_STRIDE, KV_WIDTH, KV_WIDTH)
    p = pos.reshape(2, CMP_STRIDE, 1, HEAD_DIM)
    p = jnp.broadcast_to(p, (2, CMP_STRIDE, G, HEAD_DIM)).reshape(2, CMP_STRIDE, KV_WIDTH)
    return wbd.astype(BF16), p


def _split3(x):
    hi = x.astype(BF16)
    r = x - hi.astype(F32)
    mid = r.astype(BF16)
    lo = (r - mid.astype(F32)).astype(BF16)
    return hi, mid, lo


def _nsa_cmp_kernel(qT_ref, qaug_ref, kc_ref, vcT_ref, ovT_ref, ocmp_ref, sel_ref, flag_ref, *, n_sel):
    ncp = kc_ref.shape[0]
    ns = ovT_ref.shape[0]
    tiles = range(CMP_TILES)
    i0 = pl.program_id(2) * CMP_TILES
    lane = lax.broadcasted_iota(jnp.int32, (1, GQ), 1)
    q = [jnp.concatenate([qT_ref[u], qaug_ref[...]], axis=0) for u in tiles]
    t_row = [(i0 + u) * Q_BLOCK + (lane & (Q_BLOCK - 1)) for u in tiles]
    has_cmp = [(t_row[u] >= CMP_LEN - 1).astype(F32) for u in tiles]
    tq = [(i0 + u) * Q_BLOCK + lax.broadcasted_iota(jnp.int32, (1, Q_BLOCK), 1) for u in tiles]
    cur = [tq[u] // SLC_LEN for u in tiles]

    def prefix(rows):
        nsk = rows * CMP_STRIDE // SLC_LEN
        tail0 = max(rows - CMP_TAIL, 0)
        kc = kc_ref[0:rows, :]
        s = [_dot(kc, q[u]) for u in tiles]
        cend = (lax.broadcasted_iota(jnp.int32, (rows - tail0, 1), 0) + tail0) * CMP_STRIDE + (CMP_LEN - 1)
        tail = [jnp.where(t_row[u] >= cend, s[u][tail0:], NEG) for u in tiles]
        s = [jnp.concatenate([s[u][0:tail0], tail[u]], axis=0) if tail0 else tail[u] for u in tiles]
        m = [jnp.max(s[u], axis=0, keepdims=True) for u in tiles]
        e = [jnp.exp2(s[u] - m[u]) for u in tiles]
        p = [e[u] * (has_cmp[u] / jnp.sum(e[u], axis=0, keepdims=True)) for u in tiles]
        vc = vcT_ref[:, 0:rows]
        for u in tiles:
            ocmp_ref[u] = _dot(vc, p[u].astype(BF16))

        ps = [p[u][:, 0:Q_BLOCK] for u in tiles]
        for r in range(1, NSA_GROUP):
            ps = [ps[u] + p[u][:, r * Q_BLOCK:(r + 1) * Q_BLOCK] for u in tiles]
        ov = ovT_ref[0:nsk, 0:rows]
        split = [_split3(ps[u]) for u in tiles]
        imp = [_dot(ov, split[u][0]) + _dot(ov, split[u][1]) + _dot(ov, split[u][2]) for u in tiles]

        blk = lax.broadcasted_iota(jnp.int32, (nsk, 1), 0)
        forced = [(blk == 0) | (blk == cur[u]) | (blk == cur[u] - 1) for u in tiles]
        valid = [blk * SLC_LEN <= tq[u] for u in tiles]
        imp = [jnp.where(forced[u], -3e38, jnp.where(valid[u], imp[u], -BIG)) for u in tiles]
        blk_f = blk.astype(F32)
        sel = [forced[u].astype(F32) for u in tiles]
        for _ in range(n_sel - N_FORCED):
            mx = [jnp.max(imp[u], axis=0, keepdims=True) for u in tiles]
            idx = [jnp.min(jnp.where(imp[u] == mx[u], blk_f, float(ns)), axis=0, keepdims=True) for u in tiles]
            pick = [blk_f == idx[u] for u in tiles]
            sel = [jnp.where(pick[u], 1.0, sel[u]) for u in tiles]
            imp = [jnp.where(pick[u], -3e38, imp[u]) for u in tiles]
        ones = jnp.ones((8, Q_BLOCK), BF16)
        for u in tiles:
            sel_ref[u, 0:nsk, :] = sel[u]
            cnt = _dot_nt(ones, sel[u].astype(BF16))
            flag_ref[u, :, 0:nsk] = (cnt > 0).astype(jnp.int32)
            if nsk < ns:
                sel_ref[u, nsk:, :] = jnp.zeros((ns - nsk, Q_BLOCK), F32)
                flag_ref[u, :, nsk:] = jnp.zeros((8, ns - nsk), jnp.int32)

    n_variants = ncp // CMP_CHUNK
    last = i0 + CMP_TILES - 1
    variant = (last * (Q_BLOCK // CMP_STRIDE) + (Q_BLOCK // CMP_STRIDE - 2)) // CMP_CHUNK
    for k in range(n_variants):
        pl.when(variant == k)(functools.partial(prefix, (k + 1) * CMP_CHUNK))


def _nsa_cmp(qT, qaug, kcmp, vcT, ovT, B, T):
    G = NSA_KV_HEADS
    nq = T // Q_BLOCK
    ncp = T // CMP_STRIDE
    ns = T // SLC_LEN
    n_sel = min(SLC_TOPK, ns)
    assert ncp % CMP_CHUNK == 0 and n_sel > N_FORCED and nq % CMP_TILES == 0
    tile = lambda b, g, i: (b, g, i, 0, 0)
    return pl.pallas_call(
        functools.partial(_nsa_cmp_kernel, n_sel=n_sel),
        grid=(B, G, nq // CMP_TILES),
        in_specs=[pl.BlockSpec((None, None, CMP_TILES, HEAD_DIM, GQ), tile),
                  pl.BlockSpec((None, HEAD_DIM, GQ), lambda b, g, i: (g, 0, 0)),
                  pl.BlockSpec((None, None, ncp, 2 * HEAD_DIM), lambda b, g, i: (b, g, 0, 0)),
                  pl.BlockSpec((None, None, HEAD_DIM, ncp), lambda b, g, i: (b, g, 0, 0)),
                  pl.BlockSpec((ns, ncp), lambda b, g, i: (0, 0))],
        out_specs=[pl.BlockSpec((None, None, CMP_TILES, HEAD_DIM, GQ), tile),
                   pl.BlockSpec((None, None, CMP_TILES, ns, Q_BLOCK), tile),
                   pl.BlockSpec((None, None, CMP_TILES, 8, ns), tile)],
        out_shape=[jax.ShapeDtypeStruct((B, G, nq, HEAD_DIM, GQ), F32),
                   jax.ShapeDtypeStruct((B, G, nq, ns, Q_BLOCK), F32),
                   jax.ShapeDtypeStruct((B, G, nq, 8, ns), jnp.int32)],
        compiler_params=_params(3),
        name="nsa_cmp",
    )(qT, qaug, kcmp, vcT, ovT)


def _nsa_main_kernel(list_ref, cnt_ref, qT_ref, qaug_ref, ks_ref, vsT_ref, kw_ref, vwT_ref, sel_ref, gT_ref, ocmp_ref,
                     lowb_ref, causb_ref, out_ref, m_sc, acc_sc, win_sc):
    b, g = pl.program_id(0), pl.program_id(1)
    tiles = range(MAIN_TILES)
    i = [pl.program_id(2) * MAIN_TILES + u for u in tiles]
    tile_id = [(b * pl.num_programs(1) + g) * (pl.num_programs(2) * MAIN_TILES) + i[u] for u in tiles]
    n_steps = sel_ref.shape[1] // 2
    q = [jnp.concatenate([qT_ref[u], qaug_ref[...]], axis=0) for u in tiles]
    k0 = [pl.multiple_of(i[u] * Q_BLOCK, Q_BLOCK) for u in tiles]

    def sel_bias(u, j, valid):
        def row(r):
            picked = (sel_ref[u, pl.ds(r, 1), :] > 0.5) & valid
            return jnp.concatenate([jnp.where(picked, 0.0, NEG)] * NSA_GROUP, axis=1)
        return row(2 * j), row(2 * j + 1)

    def add_sel_bias(s, ba, bb):
        return jnp.concatenate([s[0:SLC_LEN] + ba, s[SLC_LEN:] + bb], axis=0)

    lowb = jnp.concatenate([lowb_ref[...]] * NSA_GROUP, axis=1)
    causb = jnp.concatenate([causb_ref[...]] * NSA_GROUP, axis=1)

    bias_d = [sel_bias(u, i[u], True) for u in tiles]
    sd = [_dot(ks_ref[pl.ds(k0[u], KEY_STEP), :], q[u]) for u in tiles]
    sw = [_dot(kw_ref[pl.ds(k0[u], WIN_KEYS), :], q[u]) for u in tiles]
    sd = [add_sel_bias(sd[u], *bias_d[u]) + causb for u in tiles]
    sw = [jnp.concatenate([sw[u][0:Q_BLOCK] + lowb, sw[u][Q_BLOCK:WIN], sw[u][WIN:] + causb], axis=0) for u in tiles]
    md = [jnp.max(sd[u], axis=0, keepdims=True) for u in tiles]
    mw = [jnp.max(sw[u], axis=0, keepdims=True) for u in tiles]
    accd = [_dot(vsT_ref[:, pl.ds(k0[u], KEY_STEP)], jnp.exp2((sd[u] - md[u]).astype(BF16))) for u in tiles]
    ow = [_dot(vwT_ref[:, pl.ds(k0[u], WIN_KEYS)], jnp.exp2((sw[u] - mw[u]).astype(BF16))) for u in tiles]
    for u in tiles:
        m_sc[u] = md[u]
        acc_sc[u] = accd[u]
        win_sc[u] = ow[u][0:HEAD_DIM] / ow[u][HEAD_DIM:HEAD_DIM + 1]

    def scores(u, t):
        ks, vs, biases = [], [], []
        for x in range(STEP_GROUP):
            j = list_ref[tile_id[u] * n_steps + t * STEP_GROUP + x]
            valid = j >= 0
            j = jnp.maximum(j, 0)
            kj = pl.multiple_of(j * KEY_STEP, KEY_STEP)
            ks.append(ks_ref[pl.ds(kj, KEY_STEP), :])
            vs.append(vsT_ref[:, pl.ds(kj, KEY_STEP)])
            biases.append(sel_bias(u, j, valid))
        s = _dot(jnp.concatenate(ks, axis=0), q[u])
        s = jnp.concatenate([add_sel_bias(s[x * KEY_STEP:(x + 1) * KEY_STEP], *biases[x])
                             for x in range(STEP_GROUP)], axis=0)
        return s, jnp.max(s, axis=0, keepdims=True), jnp.concatenate(vs, axis=1)

    def accumulate(u, s, smax, vcat):
        m_old = m_sc[u]
        m_new = jnp.maximum(m_old, smax)
        alpha = jnp.exp2(m_old - m_new)
        acc_sc[u] = alpha * acc_sc[u] + _dot(vcat, jnp.exp2((s - m_new).astype(BF16)))
        m_sc[u] = m_new

    def run(work, t, carry):
        staged = [(u, scores(u, t * mult + off)) for (u, mult, off) in work]
        for u, args in staged:
            accumulate(u, *args)
        return carry

    for u0 in range(0, MAIN_TILES, LOOP_TILES):
        us = range(u0, u0 + LOOP_TILES)
        n_groups = functools.reduce(
            jnp.maximum, [(cnt_ref[tile_id[u]] + (STEP_GROUP - 1)) // STEP_GROUP for u in us])
        lax.fori_loop(0, n_groups // 2, functools.partial(run, [(u, 2, off) for off in (0, 1) for u in us]), 0)
        lax.fori_loop(n_groups // 2 * 2, n_groups, functools.partial(run, [(u, 1, 0) for u in us]), 0)

    def gate(u, k):
        rows = [gT_ref[u, pl.ds(g * (NSA_GROUP * 3) + r * 3 + k, 1), :] for r in range(NSA_GROUP)]
        return jnp.concatenate(rows, axis=1)

    o_slc = [acc_sc[u, 0:HEAD_DIM, :] / acc_sc[u, HEAD_DIM:HEAD_DIM + 1, :] for u in tiles]
    o = [gate(u, 0) * ocmp_ref[u] + gate(u, 1) * o_slc[u] + gate(u, 2) * win_sc[u] for u in tiles]
    o = [jnp.concatenate([o[u], jnp.zeros_like(o[u])], axis=0) for u in tiles]
    for r in range(NSA_GROUP):
        ot = [o[u][:, r * Q_BLOCK:(r + 1) * Q_BLOCK].T[:, 0:HEAD_DIM] for u in tiles]
        for u in tiles:
            out_ref[u * Q_BLOCK:(u + 1) * Q_BLOCK, r * HEAD_DIM:(r + 1) * HEAD_DIM] = ot[u]


def _nsa_steps_kernel(flagT_ref, pairT_ref, list_ref, cnt_ref, *, nq):
    n_steps, nt = list_ref.shape
    need = _dot(pairT_ref[...], flagT_ref[...].astype(BF16)) > 0
    step = lax.broadcasted_iota(jnp.int32, (n_steps, 1), 0)
    own = lax.broadcasted_iota(jnp.int32, (1, nt), 1) % nq
    need = need & (step < own)
    need_f = need.astype(F32)
    earlier = (lax.broadcasted_iota(jnp.int32, (n_steps, n_steps), 1) < step).astype(BF16)
    slot = _dot(earlier, need_f.astype(BF16))
    total = jnp.sum(need_f, axis=0, keepdims=True)
    cnt_ref[...] = jnp.broadcast_to(total, cnt_ref.shape).astype(jnp.int32)
    step_f = step.astype(F32)
    for p in range(n_steps):
        val = jnp.sum(jnp.where(need & (slot == p), step_f, 0.0), axis=0, keepdims=True)
        list_ref[p:p + 1, :] = jnp.where(total > p, val, -1.0).astype(jnp.int32)


def _nsa_steps(flags, nq):
    nt, ns = flags.shape
    n_steps = ns // 2
    pairT = jnp.asarray(np.arange(n_steps)[:, None] == np.arange(ns)[None, :] // 2, BF16)
    lists, counts = pl.pallas_call(
        functools.partial(_nsa_steps_kernel, nq=nq),
        out_shape=[jax.ShapeDtypeStruct((n_steps, nt), jnp.int32), jax.ShapeDtypeStruct((8, nt), jnp.int32)],
        name="nsa_steps",
    )(flags.T.astype(F32), pairT)
    return lists.T.reshape(-1), counts[0]


def _nsa_main(lists, counts, qT, qaug, ks, vsT, kw, vwT, sel, gT, ocmp, lowb, causb, B, T):
    G = NSA_KV_HEADS
    nq = T // Q_BLOCK
    ns = T // SLC_LEN
    whole = lambda b, g, i, *_: (b, g, 0, 0)
    tile = lambda b, g, i, *_: (b, g, i, 0, 0)
    const = lambda b, g, i, *_: (0, 0)
    grid_spec = pltpu.PrefetchScalarGridSpec(
        num_scalar_prefetch=2,
        grid=(B, G, nq // MAIN_TILES),
        in_specs=[pl.BlockSpec((None, None, MAIN_TILES, HEAD_DIM, GQ), tile),
                  pl.BlockSpec((None, HEAD_DIM, GQ), lambda b, g, i, *_: (g, 0, 0)),
                  pl.BlockSpec((None, None, T, 2 * HEAD_DIM), whole),
                  pl.BlockSpec((None, None, V_ROWS, T), whole),
                  pl.BlockSpec((None, None, T + WIN, 2 * HEAD_DIM), whole),
                  pl.BlockSpec((None, None, V_ROWS, T + WIN), whole),
                  pl.BlockSpec((None, None, MAIN_TILES, ns, Q_BLOCK), tile),
                  pl.BlockSpec((None, MAIN_TILES, 32, Q_BLOCK), lambda b, g, i, *_: (b, i, 0, 0)),
                  pl.BlockSpec((None, None, MAIN_TILES, HEAD_DIM, GQ), tile),
                  pl.BlockSpec((Q_BLOCK, Q_BLOCK), const),
                  pl.BlockSpec((Q_BLOCK, Q_BLOCK), const)],
        out_specs=pl.BlockSpec((None, MAIN_TILES * Q_BLOCK, NSA_GROUP * HEAD_DIM), lambda b, g, i, *_: (b, i, g)),
        scratch_shapes=[pltpu.VMEM((MAIN_TILES, 1, GQ), F32), pltpu.VMEM((MAIN_TILES, V_ROWS, GQ), F32),
                        pltpu.VMEM((MAIN_TILES, HEAD_DIM, GQ), F32)],
    )
    return pl.pallas_call(
        _nsa_main_kernel,
        grid_spec=grid_spec,
        out_shape=jax.ShapeDtypeStruct((B, T, NSA_WIDTH), F32),
        compiler_params=_params(3),
        name="nsa_main",
    )(lists, counts, qT, qaug, ks, vsT, kw, vwT, sel, gT, ocmp, lowb, causb)


def _ret_kernel(p_ref, decay_ref, xi_ref, zeta_ref, gch_ref, ng_ref, ones_ref, out_ref, state_ref):
    @pl.when(pl.program_id(0) == 0)
    def _():
        state_ref[...] = jnp.zeros(state_ref.shape, F32)

    rows = range(p_ref.shape[0])
    kw = RET_HEADS * RET_DK
    p = [p_ref[b] for b in rows]
    rq = [p[b][:, 0:kw] * (RET_DK ** -0.5) for b in rows]
    rk = [p[b][:, kw:2 * kw] for b in rows]
    rkT = [rk[b].T for b in rows]
    rv = [p[b][:, 2 * kw:2 * kw + RET_WIDTH] for b in rows]
    xi = xi_ref[...]
    outs = [[] for _ in rows]
    for h in range(RET_HEADS):
        dk = slice(h * RET_DK, (h + 1) * RET_DK)
        dv = slice(h * RET_DV, (h + 1) * RET_DV)
        st = [state_ref[b, h] for b in rows]
        inner = [_dot_nt(rq[b][:, dk], rk[b][:, dk]) * decay_ref[h] for b in rows]
        o = [_dot(inner[b], rv[b][:, dv]) + _dot(rq[b][:, dk], st[b]) * xi[:, h:h + 1] for b in rows]
        for b in rows:
            state_ref[b, h] = (st[b] * gch_ref[h:h + 1, 0:1]
                               + _dot(rkT[b][dk, :] * zeta_ref[h:h + 1, :], rv[b][:, dv]))
            outs[b].append(o[b])
    normed = [_group_rms(jnp.concatenate(outs[b], axis=1), ng_ref[...], ones_ref) for b in rows]
    for b in rows:
        rg = p[b][:, 2 * kw + RET_WIDTH:2 * kw + 2 * RET_WIDTH]
        out_ref[b] = normed[b] * (rg * jax.nn.sigmoid(rg))


def _ret_consts():
    H, C = RET_HEADS, RET_CHUNK
    log_g = np.log1p(-np.exp2(-5.0 - np.arange(H, dtype=np.float64)))
    idx = np.arange(C, dtype=np.float64)
    diff = idx[:, None] - idx[None, :]
    decay = np.where(diff >= 0, np.exp(np.maximum(diff, 0.0) * log_g[:, None, None]), 0.0)
    zeta = np.exp((C - 1 - idx) * log_g[:, None])
    xi = np.exp((idx + 1) * log_g[:, None]).T
    g_chunk = np.broadcast_to(np.exp(C * log_g)[:, None], (H, LANES))
    return tuple(jnp.asarray(a, F32) for a in (decay, xi, zeta, g_chunk))


def _retention(pret, ng, B, T):
    nch = T // RET_CHUNK
    decay, xi, zeta, gch = _ret_consts()
    c2 = lambda c: (0, 0)
    out = pl.pallas_call(
        _ret_kernel,
        grid=(nch,),
        in_specs=[pl.BlockSpec((B, RET_CHUNK, _RET_COLS), lambda c: (0, c, 0)),
                  pl.BlockSpec((RET_HEADS, RET_CHUNK, RET_CHUNK), lambda c: (0, 0, 0)),
                  pl.BlockSpec((RET_CHUNK, RET_HEADS), c2),
                  pl.BlockSpec((RET_HEADS, RET_CHUNK), c2),
                  pl.BlockSpec((RET_HEADS, LANES), c2),
                  pl.BlockSpec((1, RET_WIDTH), c2),
                  pl.BlockSpec((RET_WIDTH, RET_WIDTH), c2)],
        out_specs=pl.BlockSpec((B, RET_CHUNK, RET_WIDTH), lambda c: (0, c, 0)),
        out_shape=jax.ShapeDtypeStruct((B, T, RET_WIDTH), F32),
        scratch_shapes=[pltpu.VMEM((B, RET_HEADS, RET_DK, RET_DV), F32)],
        compiler_params=_params(1),
        name="retention",
    )(pret.reshape(B, T, _RET_COLS), decay, xi, zeta, gch, ng, _group_ones())
    return out.reshape(B * T, RET_WIDTH)


def _mixer_residual(x_ref, nsa_ref, ret_ref, wo_ref):
    return (x_ref[...] + _dot(nsa_ref[...].astype(BF16), wo_ref[0:NSA_WIDTH, :])
            + _dot(ret_ref[...].astype(BF16), wo_ref[NSA_WIDTH:, :]))


def _mixer_specs(tm, d, row, const):
    return [pl.BlockSpec((tm, d), row), pl.BlockSpec((tm, NSA_WIDTH), row), pl.BlockSpec((tm, RET_WIDTH), row),
            pl.BlockSpec((NSA_WIDTH + RET_WIDTH, d), const)]


def _pack_kernel(wg_ref, wu_ref, o_ref):
    fc = wg_ref.shape[1]
    o_ref[:, 0:fc] = wg_ref[...].astype(BF16)
    o_ref[:, fc:] = wu_ref[...].astype(BF16)


def _pack_gate_up(w_gate, w_up):
    E, d, dff = w_gate.shape
    fc = FF_CHUNK
    spec = pl.BlockSpec((None, d, fc), lambda e, f: (e, 0, f))
    return pl.pallas_call(
        _pack_kernel,
        grid=(E, dff // fc),
        in_specs=[spec, spec],
        out_specs=pl.BlockSpec((None, None, d, 2 * fc), lambda e, f: (e, f, 0, 0)),
        out_shape=jax.ShapeDtypeStruct((E, dff // fc, d, 2 * fc), BF16),
        compiler_params=_params(2),
        name="pack_gate_up",
    )(w_gate, w_up)


def _swiglu(rows, wgu_ref, wd_ref):
    au = _dot(rows, wgu_ref[...])
    fc = au.shape[1] // 2
    a = au[:, 0:fc]
    return _dot((a * jax.nn.sigmoid(a) * au[:, fc:]).astype(BF16), wd_ref[...])


def _ffn_kernel(x_ref, nsa_ref, ret_ref, wo_ref, g_ref, wgu_ref, wd_ref, o_ref, h_sc):
    f = pl.program_id(1)

    @pl.when(f == 0)
    def _():
        x = _mixer_residual(x_ref, nsa_ref, ret_ref, wo_ref)
        h_sc[...] = _rms(x, g_ref[...]).astype(BF16)
        o_ref[...] = x

    o_ref[...] += _swiglu(h_sc[...], wgu_ref, wd_ref)


def _ffn(x2, nsa, ret, wo, g, wgu, wd, tm=1024):
    n, d = x2.shape
    dff = wd.shape[0]
    fc = FF_CHUNK
    return pl.pallas_call(
        _ffn_kernel,
        grid=(n // tm, dff // fc),
        in_specs=_mixer_specs(tm, d, lambda i, f: (i, 0), lambda i, f: (0, 0)) + [
            pl.BlockSpec((1, d), lambda i, f: (0, 0)),
            pl.BlockSpec((None, None, d, 2 * fc), lambda i, f: (0, f, 0, 0)),
            pl.BlockSpec((fc, d), lambda i, f: (f, 0))],
        out_specs=pl.BlockSpec((tm, d), lambda i, f: (i, 0)),
        out_shape=jax.ShapeDtypeStruct((n, d), F32),
        scratch_shapes=[pltpu.VMEM((tm, d), BF16)],
        compiler_params=_params(2),
        name="ffn_dense",
    )(x2, nsa, ret, wo, g, wgu, wd)


def _router_kernel(x_ref, nsa_ref, ret_ref, wo_ref, g_ref, r_ref, rb_ref, tri_ref,
                   x1_ref, h_ref, rank_ref, comb_ref, rankT_ref, cnt_ref):
    x = _mixer_residual(x_ref, nsa_ref, ret_ref, wo_ref)
    x1_ref[...] = x
    h = _rms(x, g_ref[...])
    h_ref[...] = h.astype(BF16)
    hh, hm, _ = _split3(h)
    rh, rm, _ = _split3(r_ref[...])
    both = _dot(hh, jnp.concatenate([rh, rm], axis=1))
    logits = both[:, 0:LANES] + (both[:, LANES:] + _dot(hm, rh)) + rb_ref[...]
    lane = lax.broadcasted_iota(jnp.int32, logits.shape, 1).astype(F32)
    logits = jnp.where(lane < N_EXPERTS, logits, NEG)
    m1 = jnp.max(logits, axis=1, keepdims=True)
    i1 = jnp.min(jnp.where(logits == m1, lane, float(LANES)), axis=1, keepdims=True)
    l2 = jnp.where(lane == i1, NEG, logits)
    m2 = jnp.max(l2, axis=1, keepdims=True)
    i2 = jnp.min(jnp.where(l2 == m2, lane, float(LANES)), axis=1, keepdims=True)
    e2 = jnp.exp(m2 - m1)
    w1 = 1.0 / (1.0 + e2)
    w2 = e2 / (1.0 + e2)
    use1, use2 = lane == i1, lane == i2
    comb_ref[...] = jnp.where(use1, w1, 0.0) + jnp.where(use2, w2, 0.0)
    use = (use1 | use2).astype(F32)
    rank = jnp.where(use > 0, _dot(tri_ref[...], use.astype(BF16)), -1.0)
    rank_ref[...] = rank
    rankT_ref[...] = rank.T[0:N_EXPERTS, :]
    cnt_ref[...] = jnp.broadcast_to(jnp.sum(use, axis=0, keepdims=True), cnt_ref.shape).astype(jnp.int32)


def _router(x2, nsa, ret, wo, g, router, rb, tm):
    n, d = x2.shape
    nt = n // tm
    tri = jnp.asarray(np.arange(tm)[:, None] > np.arange(tm)[None, :], BF16)
    rpad = jnp.zeros((d, LANES), F32).at[:, :N_EXPERTS].set(router)
    rbpad = jnp.zeros((1, LANES), F32).at[0, :N_EXPERTS].set(rb)
    c2 = lambda i: (0, 0)
    row = lambda i: (i, 0)
    return pl.pallas_call(
        _router_kernel,
        grid=(nt,),
        in_specs=_mixer_specs(tm, d, row, c2) + [
            pl.BlockSpec((1, d), c2),
            pl.BlockSpec((d, LANES), c2),
            pl.BlockSpec((1, LANES), c2),
            pl.BlockSpec((tm, tm), c2)],
        out_specs=[pl.BlockSpec((tm, d), row),
                   pl.BlockSpec((tm, d), row),
                   pl.BlockSpec((tm, LANES), row),
                   pl.BlockSpec((tm, LANES), row),
                   pl.BlockSpec((N_EXPERTS, tm), lambda i: (0, i)),
                   pl.BlockSpec((None, 8, LANES), lambda i: (i, 0, 0))],
        out_shape=[jax.ShapeDtypeStruct((n, d), F32),
                   jax.ShapeDtypeStruct((n, d), BF16),
                   jax.ShapeDtypeStruct((n, LANES), F32),
                   jax.ShapeDtypeStruct((n, LANES), F32),
                   jax.ShapeDtypeStruct((N_EXPERTS, n), F32),
                   jax.ShapeDtypeStruct((nt, 8, LANES), jnp.int32)],
        compiler_params=_params(1),
        name="moe_router",
    )(x2, nsa, ret, wo, g, rpad, rbpad, tri)


MOE_SUB = 144
MOE_MOVE = 2 * MOE_SUB


def _moe_kernel(cnt_ref, h_ref, rankT_ref, rank_ref, comb_ref, wgu_ref, wd_ref, x_ref, o_ref, hc_sc, oacc_sc):
    t, e, f = pl.program_id(0), pl.program_id(1), pl.program_id(2)
    nf = pl.num_programs(2)
    tm = h_ref.shape[0]
    nsub = (cnt_ref[t * N_EXPERTS + e] + (MOE_SUB - 1)) // MOE_SUB
    nmove = (nsub + 1) // 2

    @pl.when((e == 0) & (f == 0))
    def _():
        o_ref[...] = x_ref[...]

    @pl.when(f == 0)
    def _():
        rank_row = rankT_ref[...]

        def gather(s, c):
            r0 = pl.multiple_of(s * MOE_MOVE, MOE_MOVE)
            rows = (lax.broadcasted_iota(jnp.int32, (MOE_MOVE, 1), 0) + r0).astype(F32)
            onehot = (rows == rank_row).astype(BF16)
            hc_sc[pl.ds(r0, MOE_MOVE), :] = _dot(onehot, h_ref[...]).astype(BF16)
            oacc_sc[pl.ds(r0, MOE_MOVE), :] = jnp.zeros((MOE_MOVE, oacc_sc.shape[1]), F32)
            return c

        lax.fori_loop(0, nmove, gather, 0)

    def expert(n_rows, s, c):
        r0 = pl.multiple_of(s * n_rows, n_rows)
        oacc_sc[pl.ds(r0, n_rows), :] += _swiglu(hc_sc[pl.ds(r0, n_rows), :], wgu_ref, wd_ref)
        return c

    lax.fori_loop(0, nsub // 2, functools.partial(expert, MOE_MOVE), 0)
    lax.fori_loop(nsub // 2 * 2, nsub, functools.partial(expert, MOE_SUB), 0)

    @pl.when(f == nf - 1)
    def _():
        is_e = lax.broadcasted_iota(jnp.int32, (1, LANES), 1) == e
        rank_col = jnp.sum(jnp.where(is_e, rank_ref[...], 0.0), axis=1, keepdims=True)
        comb_col = jnp.sum(jnp.where(is_e, comb_ref[...], 0.0), axis=1, keepdims=True)

        def scatter(s, c):
            r0 = pl.multiple_of(s * MOE_MOVE, MOE_MOVE)
            cols = (lax.broadcasted_iota(jnp.int32, (1, MOE_MOVE), 1) + r0).astype(F32)
            onehot = (rank_col == cols).astype(BF16)
            y = _dot(onehot, oacc_sc[pl.ds(r0, MOE_MOVE), :].astype(BF16))
            o_ref[...] += comb_col * y
            return c

        lax.fori_loop(0, nmove, scatter, 0)


def _moe(counts, h, rankT, rank, comb, wgu, wd, x2, tm):
    n, d = x2.shape
    dff = wd.shape[1]
    fc = FF_CHUNK
    rows_cap = pl.cdiv(pl.cdiv(tm, MOE_SUB), 2) * MOE_MOVE
    grid_spec = pltpu.PrefetchScalarGridSpec(
        num_scalar_prefetch=1,
        grid=(n // tm, N_EXPERTS, dff // fc),
        in_specs=[pl.BlockSpec((tm, d), lambda t, e, f, c: (t, 0)),
                  pl.BlockSpec((None, 1, tm), lambda t, e, f, c: (e, 0, t)),
                  pl.BlockSpec((tm, LANES), lambda t, e, f, c: (t, 0)),
                  pl.BlockSpec((tm, LANES), lambda t, e, f, c: (t, 0)),
                  pl.BlockSpec((None, None, d, 2 * fc), lambda t, e, f, c: (e, f, 0, 0)),
                  pl.BlockSpec((None, fc, d), lambda t, e, f, c: (e, f, 0)),
                  pl.BlockSpec((tm, d), lambda t, e, f, c: (t, 0))],
        out_specs=pl.BlockSpec((tm, d), lambda t, e, f, c: (t, 0)),
        scratch_shapes=[pltpu.VMEM((rows_cap, d), BF16), pltpu.VMEM((rows_cap, d), F32)],
    )
    return pl.pallas_call(
        _moe_kernel,
        grid_spec=grid_spec,
        out_shape=jax.ShapeDtypeStruct((n, d), F32),
        compiler_params=_params(3),
        name="moe_experts",
    )(counts, h, rankT.reshape(N_EXPERTS, 1, n), rank, comb, wgu, wd, x2)


def _permute_w_in(w):
    o = np.cumsum((0, NSA_WIDTH) + (KV_WIDTH,) * 6 + (3 * NSA_HEADS,))
    q, kc, vc, ks, vs, kw, vw, gts = (w[:, o[k]:o[k + 1]] for k in range(8))
    ret = w[:, o[8]:]
    pad = jnp.zeros((w.shape[0], LANES - 3 * NSA_HEADS), w.dtype)
    return jnp.concatenate([q, ks, kw, vs, vw, kc, vc, gts, pad, ret], axis=1).astype(BF16)


def _nsa_consts(T):
    ncp = T // CMP_STRIDE
    ns = T // SLC_LEN
    cs = np.arange(ncp) * CMP_STRIDE
    ss = np.arange(ns) * SLC_LEN
    ov = np.clip(np.minimum(cs[None, :] + CMP_LEN, ss[:, None] + SLC_LEN) - np.maximum(cs[None, :], ss[:, None]), 0, None)
    ovT = (ov.astype(np.float32) / CMP_LEN)
    ovT[:, ncp - 1] = 0.0
    h = np.arange(NSA_HEADS).reshape(NSA_KV_HEADS, NSA_GROUP) + 1
    slopes = np.exp2(-8.0 * h / NSA_HEADS).astype(np.float32)
    slopes = np.repeat(slopes, Q_BLOCK, axis=1)
    parts, rest = [], np.float64(LOG2E)
    for _ in range(3):
        part = np.float64(np.asarray(rest).astype(BF16))
        parts.append(part)
        rest = rest - part
    qaug = np.zeros((NSA_KV_HEADS, HEAD_DIM, GQ), np.float32)
    for k, part in enumerate(parts):
        qaug[:, k, :] = part * SLC_LEN * slopes
        qaug[:, 3 + k, :] = part * slopes
    kq = np.arange(Q_BLOCK)[:, None] - np.arange(Q_BLOCK)[None, :]
    causb = np.where(kq <= 0, 0.0, NEG).astype(np.float32)
    lowb = np.where(kq > 0, 0.0, NEG).astype(np.float32)
    return jnp.asarray(ovT, BF16), jnp.asarray(qaug, BF16), jnp.asarray(lowb), jnp.asarray(causb)


def _mixer(x2, B, T, norm_g, w_in, q_norm_g, k_norm_g, cmp_pos, w_cmp, ret_norm_g, w_out):
    ns = T // SLC_LEN
    kc, vc, pret, qT, ks, kw, vsT, vwT, gT = _inproj(x2, norm_g[None, :], _permute_w_in(w_in), q_norm_g[None, :],
                                                     k_norm_g[1:3], B, T)
    wk, pk = _compress_weights(w_cmp[0], cmp_pos[0])
    wv, pv = _compress_weights(w_cmp[1], cmp_pos[1])
    kcmp, vcT = _compress(kc, vc, wk, wv, pk, pv, k_norm_g[0:1], B, T)
    ovT, qaug, lowb, causb = _nsa_consts(T)
    ocmp, sel, flags = _nsa_cmp(qT, qaug, kcmp, vcT, ovT, B, T)
    lists, counts = _nsa_steps(flags[:, :, :, 0, :].reshape(-1, ns), T // Q_BLOCK)
    kpad = jnp.zeros((WIN, 2 * HEAD_DIM), BF16).at[:, HEAD_DIM:HEAD_DIM + 3].set(-2.0 ** 100)
    kw = jnp.concatenate([jnp.broadcast_to(kpad, kw.shape[:2] + kpad.shape), kw], axis=2)
    vwT = jnp.pad(vwT, ((0, 0), (0, 0), (0, 0), (WIN, 0)))
    nsa = _nsa_main(lists, counts, qT, qaug, ks, vsT, kw, vwT, sel, gT, ocmp, lowb, causb, B, T)
    ret = _retention(pret, ret_norm_g[None, :], B, T)
    return nsa.reshape(B * T, NSA_WIDTH), ret, w_out.astype(BF16)


def _moe_layer(x2, nsa, ret, wo, norm_g, router, router_b, wg, wu, wd, tm=1024):
    tm = min(tm, x2.shape[0])
    x1, h, rank, comb, rankT, cnt = _router(x2, nsa, ret, wo, norm_g[None, :], router, router_b, tm)
    counts = cnt[:, 0, :N_EXPERTS].reshape(-1)
    return _moe(counts, h, rankT, rank, comb, _pack_gate_up(wg, wu), wd.astype(BF16), x1, tm)


def kernel(x, norm_mix_g, w_in, q_norm_g, k_norm_g, cmp_pos, w_cmp, ret_norm_g, w_out, norm_ffn_g,
           ffn_w_gate, ffn_w_up, ffn_w_down, moe_router, moe_router_b, moe_w_gate, moe_w_up, moe_w_down):
    B, T, D = x.shape
    depth = norm_mix_g.shape[0]
    x2 = x.reshape(B * T, D)
    for l in range(depth):
        nsa, ret, wo = _mixer(x2, B, T, norm_mix_g[l], w_in[l], q_norm_g[l], k_norm_g[l], cmp_pos[l], w_cmp[l],
                              ret_norm_g[l], w_out[l])
        j = l // 2
        if l % 2 == 0:
            x2 = _ffn(x2, nsa, ret, wo, norm_ffn_g[l][None, :],
                      _pack_gate_up(ffn_w_gate[j:j + 1], ffn_w_up[j:j + 1]), ffn_w_down[j].astype(BF16))
        else:
            x2 = _moe_layer(x2, nsa, ret, wo, norm_ffn_g[l], moe_router[j], moe_router_b[j], moe_w_gate[j],
                            moe_w_up[j], moe_w_down[j])
    return x2.reshape(B, T, D)
```
